```python
import jax, jax.numpy as jnp
from jax import lax
import numpy as np

D_MODEL = 1024
BATCH = 8
SEQ = 4096
DEPTH = 1

MEM_LEN = 256
MIX_WIDTH = D_MODEL
CONV_CH = MIX_WIDTH // 2
CONV_WIDTH = 31
CONV_PAD = CONV_WIDTH // 2
HEAD_DIM = 64
ATTN_CH = MIX_WIDTH - CONV_CH
N_Q_HEADS = ATTN_CH // HEAD_DIM
N_KV_HEADS = max(N_Q_HEADS // 4, 1)
KV_CH = N_KV_HEADS * HEAD_DIM
WINDOW = 128
BLOCK = 128
ROPE_THETA = 10000.0
MEM_HEADS = 4
MEM_HEAD_DIM = D_MODEL // MEM_HEADS
D_FF = -(-8 * D_MODEL // (3 * 256)) * 256
IN_COLS = 2 * CONV_CH + ATTN_CH + 2 * KV_CH
EPS = 1e-6

kernel_name = "hybrid_conformer_swa_encoder_block"


def rms_norm(x, g):
    xf = x.astype(jnp.float32)
    y = xf * lax.rsqrt(jnp.mean(xf * xf, axis=-1, keepdims=True) + EPS)
    return (y * g.astype(jnp.float32)).astype(x.dtype)


def layer_norm(x, g, b):
    xf = x.astype(jnp.float32)
    mu = jnp.mean(xf, axis=-1, keepdims=True)
    var = jnp.mean(jnp.square(xf - mu), axis=-1, keepdims=True)
    y = (xf - mu) * lax.rsqrt(var + EPS)
    return (y * g.astype(jnp.float32) + b.astype(jnp.float32)).astype(x.dtype)


def rope_tables(seq):
    pos = jnp.arange(seq, dtype=jnp.float32)
    inv_freq = ROPE_THETA ** (-jnp.arange(0, HEAD_DIM, 2, dtype=jnp.float32) / HEAD_DIM)
    ang = pos[:, None] * inv_freq[None, :]
    return jnp.cos(ang), jnp.sin(ang)


def apply_rope(t, cos, sin):
    t1, t2 = jnp.split(t.astype(jnp.float32), 2, axis=-1)
    c = cos[None, :, None, :]
    s = sin[None, :, None, :]
    return jnp.concatenate([t1 * c - t2 * s, t1 * s + t2 * c], axis=-1).astype(t.dtype)


def conformer_conv_group(u_glu, w_dw, b_dw, ln_g, ln_b):
    a, gate = jnp.split(u_glu, 2, axis=-1)
    v = a * jax.nn.sigmoid(gate)
    y = lax.conv_general_dilated(
        v, w_dw[:, None, :].astype(v.dtype), window_strides=(1,),
        padding=[(CONV_PAD, CONV_PAD)], dimension_numbers=("NWC", "WIO", "NWC"),
        feature_group_count=CONV_CH) + b_dw
    return jax.nn.silu(layer_norm(y, ln_g, ln_b))


def windowed_gqa_with_sink(q, k, v, sink):
    B, S = q.shape[0], q.shape[1]
    nb = S // BLOCK
    G = N_Q_HEADS // N_KV_HEADS
    qb = q.reshape(B, nb, BLOCK, N_KV_HEADS, G, HEAD_DIM)
    pad = ((0, 0), (BLOCK, BLOCK), (0, 0), (0, 0))
    kp = jnp.pad(k, pad).reshape(B, nb + 2, BLOCK, N_KV_HEADS, HEAD_DIM)
    vp = jnp.pad(v, pad).reshape(B, nb + 2, BLOCK, N_KV_HEADS, HEAD_DIM)
    kb = jnp.concatenate([kp[:, :-2], kp[:, 1:-1], kp[:, 2:]], axis=2)
    vb = jnp.concatenate([vp[:, :-2], vp[:, 1:-1], vp[:, 2:]], axis=2)
    a_idx = jnp.arange(BLOCK)[:, None]
    c_idx = jnp.arange(3 * BLOCK)[None, :]
    rel = c_idx - BLOCK - a_idx
    kpos = jnp.arange(nb)[:, None] * BLOCK - BLOCK + jnp.arange(3 * BLOCK)[None, :]
    in_seq = (kpos >= 0) & (kpos < S)
    mask = (jnp.abs(rel) <= WINDOW)[None] & in_seq[:, None, :]
    scale = HEAD_DIM ** -0.5
    s = jnp.einsum("bnqhgd,bnkhd->bnhgqk", qb, kb,
                   preferred_element_type=jnp.float32) * scale
    s = jnp.where(mask[None, :, None, None], s, -jnp.inf)
    sk = sink.astype(jnp.float32).reshape(N_KV_HEADS, G)[None, None, :, :, None, None]
    m = jnp.maximum(jnp.max(s, axis=-1, keepdims=True), sk)
    p = jnp.exp(s - m)
    denom = jnp.sum(p, axis=-1, keepdims=True) + jnp.exp(sk - m)
    o = jnp.einsum("bnhgqk,bnkhd->bnqhgd", (p / denom).astype(v.dtype), vb)
    return o.reshape(B, S, N_Q_HEADS * HEAD_DIM)


def memory_cross_attention(h, mem_n, w_q, w_kv, w_o):
    B, S = h.shape[0], h.shape[1]
    M = mem_n.shape[1]
    q = (h @ w_q).reshape(B, S, MEM_HEADS, MEM_HEAD_DIM)
    km, vm = jnp.split(mem_n @ w_kv, 2, axis=-1)
    km = km.reshape(B, M, MEM_HEADS, MEM_HEAD_DIM)
    vm = vm.reshape(B, M, MEM_HEADS, MEM_HEAD_DIM)
    s = jnp.einsum("bshd,bmhd->bhsm", q, km,
                   preferred_element_type=jnp.float32) * (MEM_HEAD_DIM ** -0.5)
    p = jax.nn.softmax(s, axis=-1)
    o = jnp.einsum("bhsm,bmhd->bshd", p.astype(vm.dtype), vm).reshape(B, S, D_MODEL)
    return o @ w_o


def swiglu(h, w_gate, w_up, w_down):
    return (jax.nn.silu(h @ w_gate) * (h @ w_up)) @ w_down


def _fwd_setup_inputs(seed: int = 0) -> dict:
    key = jax.random.key(seed)
    ks = jax.random.split(key, 24)
    L = DEPTH

    def nrm(k, shape, scale):
        return jax.random.normal(k, shape, jnp.float32) * scale

    def gain(k, shape):
        return 1.0 + 0.05 * jax.random.normal(k, shape, jnp.float32)

    return {
        "x": nrm(ks[0], (BATCH, SEQ, D_MODEL), 1.0),
        "mem": nrm(ks[1], (BATCH, MEM_LEN, D_MODEL), 1.0),
        "g_mix": gain(ks[2], (L, D_MODEL)),
        "w_in": nrm(ks[3], (L, D_MODEL, IN_COLS), D_MODEL ** -0.5),
        "b_in": nrm(ks[4], (L, IN_COLS), 0.02),
        "w_dw": nrm(ks[5], (L, CONV_WIDTH, CONV_CH), CONV_WIDTH ** -0.5),
        "b_dw": nrm(ks[6], (L, CONV_CH), 0.02),
        "g_conv_ln": gain(ks[7], (L, CONV_CH)),
        "b_conv_ln": nrm(ks[8], (L, CONV_CH), 0.02),
        "attn_sink": nrm(ks[9], (L, N_Q_HEADS), 0.5),
        "w_out": nrm(ks[10], (L, MIX_WIDTH, D_MODEL), MIX_WIDTH ** -0.5),
        "b_out": nrm(ks[11], (L, D_MODEL), 0.02),
        "g_mem_q": gain(ks[12], (L, D_MODEL)),
        "g_mem_kv": gain(ks[13], (L, D_MODEL)),
        "w_mem_q": nrm(ks[14], (L, D_MODEL, D_MODEL), D_MODEL ** -0.5),
        "w_mem_kv": nrm(ks[15], (L, D_MODEL, 2 * D_MODEL), D_MODEL ** -0.5),
        "w_mem_o": nrm(ks[16], (L, D_MODEL, D_MODEL), D_MODEL ** -0.5),
        "g_ffn": gain(ks[17], (L, D_MODEL)),
        "w_gate": nrm(ks[18], (L, D_MODEL, D_FF), D_MODEL ** -0.5),
        "w_up": nrm(ks[19], (L, D_MODEL, D_FF), D_MODEL ** -0.5),
        "w_down": nrm(ks[20], (L, D_FF, D_MODEL), D_FF ** -0.5),
        "g_final": gain(ks[21], (D_MODEL,)),
    }


def _fwd_reference(x, mem, g_mix, w_in, b_in, w_dw, b_dw, g_conv_ln, b_conv_ln, attn_sink,
              w_out, b_out, g_mem_q, g_mem_kv, w_mem_q, w_mem_kv, w_mem_o,
              g_ffn, w_gate, w_up, w_down, g_final):
    B, S = x.shape[0], x.shape[1]
    cos, sin = rope_tables(S)
    split_pts = [2 * CONV_CH, 2 * CONV_CH + ATTN_CH, 2 * CONV_CH + ATTN_CH + KV_CH]
    for l in range(DEPTH):
        h = rms_norm(x, g_mix[l])
        u = h @ w_in[l] + b_in[l]
        u_glu, q, k, v = jnp.split(u, split_pts, axis=-1)
        y_conv = conformer_conv_group(u_glu, w_dw[l], b_dw[l], g_conv_ln[l], b_conv_ln[l])
        q = apply_rope(q.reshape(B, S, N_Q_HEADS, HEAD_DIM), cos, sin)
        k = apply_rope(k.reshape(B, S, N_KV_HEADS, HEAD_DIM), cos, sin)
        v = v.reshape(B, S, N_KV_HEADS, HEAD_DIM)
        y_attn = windowed_gqa_with_sink(q, k, v, attn_sink[l])
        y_mix = jnp.concatenate([y_conv, y_attn], axis=-1)
        x = x + (y_mix @ w_out[l] + b_out[l])
        x = x + memory_cross_attention(rms_norm(x, g_mem_q[l]), rms_norm(mem, g_mem_kv[l]),
                                       w_mem_q[l], w_mem_kv[l], w_mem_o[l])
        x = x + swiglu(rms_norm(x, g_ffn[l]), w_gate[l], w_up[l], w_down[l])
    return rms_norm(x, g_final)


import jax as _jax
import jax.numpy as _jnp

TWIN_FORMAT = 'train_step'
FWD_PARAMS = ['x', 'mem', 'g_mix', 'w_in', 'b_in', 'w_dw', 'b_dw', 'g_conv_ln', 'b_conv_ln', 'attn_sink', 'w_out', 'b_out', 'g_mem_q', 'g_mem_kv', 'w_mem_q', 'w_mem_kv', 'w_mem_o', 'g_ffn', 'w_gate', 'w_up', 'w_down', 'g_final']
TWIN_WEIGHTS = ['g_mix', 'w_in', 'b_in', 'w_dw', 'b_dw', 'g_conv_ln', 'b_conv_ln', 'attn_sink', 'w_out', 'b_out', 'g_mem_q', 'g_mem_kv', 'w_mem_q', 'w_mem_kv', 'w_mem_o', 'g_ffn', 'w_gate', 'w_up', 'w_down', 'g_final']
TWIN_DIFF_INPUT = 'x'
TWIN_INPUTS = ['x', 'mem', 'g_mix', 'w_in', 'b_in', 'w_dw', 'b_dw', 'g_conv_ln', 'b_conv_ln', 'attn_sink', 'w_out', 'b_out', 'g_mem_q', 'g_mem_kv', 'w_mem_q', 'w_mem_kv', 'w_mem_o', 'g_ffn', 'w_gate', 'w_up', 'w_down', 'g_final', 'loss_target', 'm_g_mix', 'm_w_in', 'm_b_in', 'm_w_dw', 'm_b_dw', 'm_g_conv_ln', 'm_b_conv_ln', 'm_attn_sink', 'm_w_out', 'm_b_out', 'm_g_mem_q', 'm_g_mem_kv', 'm_w_mem_q', 'm_w_mem_kv', 'm_w_mem_o', 'm_g_ffn', 'm_w_gate', 'm_w_up', 'm_w_down', 'm_g_final', 'v_g_mix', 'v_w_in', 'v_b_in', 'v_w_dw', 'v_b_dw', 'v_g_conv_ln', 'v_b_conv_ln', 'v_attn_sink', 'v_w_out', 'v_b_out', 'v_g_mem_q', 'v_g_mem_kv', 'v_w_mem_q', 'v_w_mem_kv', 'v_w_mem_o', 'v_g_ffn', 'v_w_gate', 'v_w_up', 'v_w_down', 'v_g_final']
TWIN_OUTPUTS = ['loss', 'grad_x', 'grad_g_mix', 'grad_w_in', 'grad_b_in', 'grad_w_dw', 'grad_b_dw', 'grad_g_conv_ln', 'grad_b_conv_ln', 'grad_attn_sink', 'grad_w_out', 'grad_b_out', 'grad_g_mem_q', 'grad_g_mem_kv', 'grad_w_mem_q', 'grad_w_mem_kv', 'grad_w_mem_o', 'grad_g_ffn', 'grad_w_gate', 'grad_w_up', 'grad_w_down', 'grad_g_final', 'delta_g_mix', 'delta_w_in', 'delta_b_in', 'delta_w_dw', 'delta_b_dw', 'delta_g_conv_ln', 'delta_b_conv_ln', 'delta_attn_sink', 'delta_w_out', 'delta_b_out', 'delta_g_mem_q', 'delta_g_mem_kv', 'delta_w_mem_q', 'delta_w_mem_kv', 'delta_w_mem_o', 'delta_g_ffn', 'delta_w_gate', 'delta_w_up', 'delta_w_down', 'delta_g_final', 'new_m_g_mix', 'new_m_w_in', 'new_m_b_in', 'new_m_w_dw', 'new_m_b_dw', 'new_m_g_conv_ln', 'new_m_b_conv_ln', 'new_m_attn_sink', 'new_m_w_out', 'new_m_b_out', 'new_m_g_mem_q', 'new_m_g_mem_kv', 'new_m_w_mem_q', 'new_m_w_mem_kv', 'new_m_w_mem_o', 'new_m_g_ffn', 'new_m_w_gate', 'new_m_w_up', 'new_m_w_down', 'new_m_g_final', 'new_v_g_mix', 'new_v_w_in', 'new_v_b_in', 'new_v_w_dw', 'new_v_b_dw', 'new_v_g_conv_ln', 'new_v_b_conv_ln', 'new_v_attn_sink', 'new_v_w_out', 'new_v_b_out', 'new_v_g_mem_q', 'new_v_g_mem_kv', 'new_v_w_mem_q', 'new_v_w_mem_kv', 'new_v_w_mem_o', 'new_v_g_ffn', 'new_v_w_gate', 'new_v_w_up', 'new_v_w_down', 'new_v_g_final']
TWIN_LEAF_KINDS = {'loss': 'loss', 'grad_x': 'grad_x', 'grad_g_mix': 'grad_w', 'grad_w_in': 'grad_w', 'grad_b_in': 'grad_w', 'grad_w_dw': 'grad_w', 'grad_b_dw': 'grad_w', 'grad_g_conv_ln': 'grad_w', 'grad_b_conv_ln': 'grad_w', 'grad_attn_sink': 'grad_w', 'grad_w_out': 'grad_w', 'grad_b_out': 'grad_w', 'grad_g_mem_q': 'grad_w', 'grad_g_mem_kv': 'grad_w', 'grad_w_mem_q': 'grad_w', 'grad_w_mem_kv': 'grad_w', 'grad_w_mem_o': 'grad_w', 'grad_g_ffn': 'grad_w', 'grad_w_gate': 'grad_w', 'grad_w_up': 'grad_w', 'grad_w_down': 'grad_w', 'grad_g_final': 'grad_w', 'delta_g_mix': 'delta_w', 'delta_w_in': 'delta_w', 'delta_b_in': 'delta_w', 'delta_w_dw': 'delta_w', 'delta_b_dw': 'delta_w', 'delta_g_conv_ln': 'delta_w', 'delta_b_conv_ln': 'delta_w', 'delta_attn_sink': 'delta_w', 'delta_w_out': 'delta_w', 'delta_b_out': 'delta_w', 'delta_g_mem_q': 'delta_w', 'delta_g_mem_kv': 'delta_w', 'delta_w_mem_q': 'delta_w', 'delta_w_mem_kv': 'delta_w', 'delta_w_mem_o': 'delta_w', 'delta_g_ffn': 'delta_w', 'delta_w_gate': 'delta_w', 'delta_w_up': 'delta_w', 'delta_w_down': 'delta_w', 'delta_g_final': 'delta_w', 'new_m_g_mix': 'new_m', 'new_m_w_in': 'new_m', 'new_m_b_in': 'new_m', 'new_m_w_dw': 'new_m', 'new_m_b_dw': 'new_m', 'new_m_g_conv_ln': 'new_m', 'new_m_b_conv_ln': 'new_m', 'new_m_attn_sink': 'new_m', 'new_m_w_out': 'new_m', 'new_m_b_out': 'new_m', 'new_m_g_mem_q': 'new_m', 'new_m_g_mem_kv': 'new_m', 'new_m_w_mem_q': 'new_m', 'new_m_w_mem_kv': 'new_m', 'new_m_w_mem_o': 'new_m', 'new_m_g_ffn': 'new_m', 'new_m_w_gate': 'new_m', 'new_m_w_up': 'new_m', 'new_m_w_down': 'new_m', 'new_m_g_final': 'new_m', 'new_v_g_mix': 'new_v', 'new_v_w_in': 'new_v', 'new_v_b_in': 'new_v', 'new_v_w_dw': 'new_v', 'new_v_b_dw': 'new_v', 'new_v_g_conv_ln': 'new_v', 'new_v_b_conv_ln': 'new_v', 'new_v_attn_sink': 'new_v', 'new_v_w_out': 'new_v', 'new_v_b_out': 'new_v', 'new_v_g_mem_q': 'new_v', 'new_v_g_mem_kv': 'new_v', 'new_v_w_mem_q': 'new_v', 'new_v_w_mem_kv': 'new_v', 'new_v_w_mem_o': 'new_v', 'new_v_g_ffn': 'new_v', 'new_v_w_gate': 'new_v', 'new_v_w_up': 'new_v', 'new_v_w_down': 'new_v', 'new_v_g_final': 'new_v'}


def _forward(args):
    return _fwd_reference(*[args[k] for k in FWD_PARAMS])


def _output_shape():
    out = _jax.eval_shape(lambda: _forward(_fwd_setup_inputs(0)))
    return out.shape, out.dtype

N_MICROBATCH = 1
ADAM_LR = 0.001
ADAM_B1 = 0.9
ADAM_B2 = 0.999
ADAM_EPS = 1e-08
ADAM_WD = 0.01
ADAM_STEP = 10
PER_EXAMPLE_BATCH_AXIS = {'x': 0, 'mem': 0, 'loss_target': 0}
SHARED_INPUTS = []
_WEIGHT_DTYPES = {'g_mix': _jnp.float32, 'w_in': _jnp.float32, 'b_in': _jnp.float32, 'w_dw': _jnp.float32, 'b_dw': _jnp.float32, 'g_conv_ln': _jnp.float32, 'b_conv_ln': _jnp.float32, 'attn_sink': _jnp.float32, 'w_out': _jnp.float32, 'b_out': _jnp.float32, 'g_mem_q': _jnp.float32, 'g_mem_kv': _jnp.float32, 'w_mem_q': _jnp.float32, 'w_mem_kv': _jnp.float32, 'w_mem_o': _jnp.float32, 'g_ffn': _jnp.float32, 'w_gate': _jnp.float32, 'w_up': _jnp.float32, 'w_down': _jnp.float32, 'g_final': _jnp.float32}
MOMENT_SCALE = {'g_mix': 9.354267e-02, 'w_in': 6.816760e-02, 'b_in': 1.630390e-01, 'w_dw': 1.166826e-01, 'b_dw': 2.657669e-01, 'g_conv_ln': 1.560295e-01, 'b_conv_ln': 1.449131e-01, 'attn_sink': 6.424660e-04, 'w_out': 8.312438e-02, 'b_out': 2.597330e-01, 'g_mem_q': 2.025541e-02, 'g_mem_kv': 3.171994e-02, 'w_mem_q': 1.946508e-02, 'w_mem_kv': 1.962106e-02, 'w_mem_o': 2.057205e-02, 'g_ffn': 1.281025e-01, 'w_gate': 5.377787e-02, 'w_up': 5.252472e-02, 'w_down': 8.694453e-02, 'g_final': 3.211712e+01}


def _to_microbatches(a, axis):
    t = _jnp.moveaxis(a, axis, 0)
    t = t.reshape((N_MICROBATCH, t.shape[0] // N_MICROBATCH) + t.shape[1:])
    return _jnp.moveaxis(t, 1, axis + 1)


def setup_inputs(seed: int = 0) -> dict:
    inp = _fwd_setup_inputs(seed)
    key = _jax.random.fold_in(_jax.random.key(seed), 7919)
    shape, _ = _output_shape()
    out = dict(inp)
    out["loss_target"] = _jax.random.normal(_jax.random.fold_in(key, 0), shape, _jnp.float32)
    for i, name in enumerate(TWIN_WEIGHTS):
        w = inp[name].astype(_jnp.float32)
        if MOMENT_SCALE is None:
            s = _jnp.sqrt(_jnp.mean(_jnp.square(w)) + 1e-30)
        else:
            s = MOMENT_SCALE[name]
        km, kv = _jax.random.split(_jax.random.fold_in(key, i + 1))
        out[name] = w
        out["m_" + name] = s * _jax.random.normal(km, w.shape, _jnp.float32)
        out["v_" + name] = (s * s) * _jax.random.uniform(kv, w.shape, _jnp.float32, 0.5, 1.5)
    if N_MICROBATCH > 1:
        for name, axis in PER_EXAMPLE_BATCH_AXIS.items():
            out[name] = _to_microbatches(out[name], axis)
    return {'x': out['x'], 'mem': out['mem'], 'g_mix': out['g_mix'], 'w_in': out['w_in'], 'b_in': out['b_in'], 'w_dw': out['w_dw'], 'b_dw': out['b_dw'], 'g_conv_ln': out['g_conv_ln'], 'b_conv_ln': out['b_conv_ln'], 'attn_sink': out['attn_sink'], 'w_out': out['w_out'], 'b_out': out['b_out'], 'g_mem_q': out['g_mem_q'], 'g_mem_kv': out['g_mem_kv'], 'w_mem_q': out['w_mem_q'], 'w_mem_kv': out['w_mem_kv'], 'w_mem_o': out['w_mem_o'], 'g_ffn': out['g_ffn'], 'w_gate': out['w_gate'], 'w_up': out['w_up'], 'w_down': out['w_down'], 'g_final': out['g_final'], 'loss_target': out['loss_target'], 'm_g_mix': out['m_g_mix'], 'm_w_in': out['m_w_in'], 'm_b_in': out['m_b_in'], 'm_w_dw': out['m_w_dw'], 'm_b_dw': out['m_b_dw'], 'm_g_conv_ln': out['m_g_conv_ln'], 'm_b_conv_ln': out['m_b_conv_ln'], 'm_attn_sink': out['m_attn_sink'], 'm_w_out': out['m_w_out'], 'm_b_out': out['m_b_out'], 'm_g_mem_q': out['m_g_mem_q'], 'm_g_mem_kv': out['m_g_mem_kv'], 'm_w_mem_q': out['m_w_mem_q'], 'm_w_mem_kv': out['m_w_mem_kv'], 'm_w_mem_o': out['m_w_mem_o'], 'm_g_ffn': out['m_g_ffn'], 'm_w_gate': out['m_w_gate'], 'm_w_up': out['m_w_up'], 'm_w_down': out['m_w_down'], 'm_g_final': out['m_g_final'], 'v_g_mix': out['v_g_mix'], 'v_w_in': out['v_w_in'], 'v_b_in': out['v_b_in'], 'v_w_dw': out['v_w_dw'], 'v_b_dw': out['v_b_dw'], 'v_g_conv_ln': out['v_g_conv_ln'], 'v_b_conv_ln': out['v_b_conv_ln'], 'v_attn_sink': out['v_attn_sink'], 'v_w_out': out['v_w_out'], 'v_b_out': out['v_b_out'], 'v_g_mem_q': out['v_g_mem_q'], 'v_g_mem_kv': out['v_g_mem_kv'], 'v_w_mem_q': out['v_w_mem_q'], 'v_w_mem_kv': out['v_w_mem_kv'], 'v_w_mem_o': out['v_w_mem_o'], 'v_g_ffn': out['v_g_ffn'], 'v_w_gate': out['v_w_gate'], 'v_w_up': out['v_w_up'], 'v_w_down': out['v_w_down'], 'v_g_final': out['v_g_final']}


def _loss(weights, diff, rest, loss_target):
    with _jax.named_scope("forward"):
        args = {**rest, TWIN_DIFF_INPUT: diff, **{k: w.astype(_WEIGHT_DTYPES[k]) for k, w in weights.items()}}
        y = _forward(args)
    with _jax.named_scope("loss_head"):
        err = _jnp.square(y.astype(_jnp.float32) - loss_target)
        return 0.5 * _jnp.sum(_jnp.mean(err, axis=-1)) if err.ndim else 0.5 * err


def _adamw(w, g, m, v):
    m = ADAM_B1 * m + (1.0 - ADAM_B1) * g
    v = ADAM_B2 * v + (1.0 - ADAM_B2) * _jnp.square(g)
    m_hat = m / (1.0 - ADAM_B1 ** ADAM_STEP)
    v_hat = v / (1.0 - ADAM_B2 ** ADAM_STEP)
    delta = -ADAM_LR * (m_hat / (_jnp.sqrt(v_hat) + ADAM_EPS) + ADAM_WD * w)
    return delta, m, v


def reference(x, mem, g_mix, w_in, b_in, w_dw, b_dw, g_conv_ln, b_conv_ln, attn_sink, w_out, b_out, g_mem_q, g_mem_kv, w_mem_q, w_mem_kv, w_mem_o, g_ffn, w_gate, w_up, w_down, g_final, loss_target, m_g_mix, m_w_in, m_b_in, m_w_dw, m_b_dw, m_g_conv_ln, m_b_conv_ln, m_attn_sink, m_w_out, m_b_out, m_g_mem_q, m_g_mem_kv, m_w_mem_q, m_w_mem_kv, m_w_mem_o, m_g_ffn, m_w_gate, m_w_up, m_w_down, m_g_final, v_g_mix, v_w_in, v_b_in, v_w_dw, v_b_dw, v_g_conv_ln, v_b_conv_ln, v_attn_sink, v_w_out, v_b_out, v_g_mem_q, v_g_mem_kv, v_w_mem_q, v_w_mem_kv, v_w_mem_o, v_g_ffn, v_w_gate, v_w_up, v_w_down, v_g_final):
    given = dict(x=x, mem=mem, g_mix=g_mix, w_in=w_in, b_in=b_in, w_dw=w_dw, b_dw=b_dw, g_conv_ln=g_conv_ln, b_conv_ln=b_conv_ln, attn_sink=attn_sink, w_out=w_out, b_out=b_out, g_mem_q=g_mem_q, g_mem_kv=g_mem_kv, w_mem_q=w_mem_q, w_mem_kv=w_mem_kv, w_mem_o=w_mem_o, g_ffn=g_ffn, w_gate=w_gate, w_up=w_up, w_down=w_down, g_final=g_final, loss_target=loss_target, m_g_mix=m_g_mix, m_w_in=m_w_in, m_b_in=m_b_in, m_w_dw=m_w_dw, m_b_dw=m_b_dw, m_g_conv_ln=m_g_conv_ln, m_b_conv_ln=m_b_conv_ln, m_attn_sink=m_attn_sink, m_w_out=m_w_out, m_b_out=m_b_out, m_g_mem_q=m_g_mem_q, m_g_mem_kv=m_g_mem_kv, m_w_mem_q=m_w_mem_q, m_w_mem_kv=m_w_mem_kv, m_w_mem_o=m_w_mem_o, m_g_ffn=m_g_ffn, m_w_gate=m_w_gate, m_w_up=m_w_up, m_w_down=m_w_down, m_g_final=m_g_final, v_g_mix=v_g_mix, v_w_in=v_w_in, v_b_in=v_b_in, v_w_dw=v_w_dw, v_b_dw=v_b_dw, v_g_conv_ln=v_g_conv_ln, v_b_conv_ln=v_b_conv_ln, v_attn_sink=v_attn_sink, v_w_out=v_w_out, v_b_out=v_b_out, v_g_mem_q=v_g_mem_q, v_g_mem_kv=v_g_mem_kv, v_w_mem_q=v_w_mem_q, v_w_mem_kv=v_w_mem_kv, v_w_mem_o=v_w_mem_o, v_g_ffn=v_g_ffn, v_w_gate=v_w_gate, v_w_up=v_w_up, v_w_down=v_w_down, v_g_final=v_g_final)
    weights = {n: given[n] for n in TWIN_WEIGHTS}
    shared = {n: given[n] for n in SHARED_INPUTS}
    per_example = {n: given[n] for n in ['x', 'mem']}
    grad_fn = _jax.value_and_grad(_loss, argnums=(0, 1))

    def one_microbatch(ex, loss_target):
        ex = dict(ex)
        diff = ex.pop(TWIN_DIFF_INPUT)
        return grad_fn(weights, diff, {**shared, **ex}, loss_target)

    if N_MICROBATCH == 1:
        loss, (grad_w, grad_x) = one_microbatch(per_example, given["loss_target"])
    else:
        def body(carry, xs):
            loss_sum, grad_sum = carry
            l_k, (gw_k, gx_k) = one_microbatch(xs[0], xs[1])
            with _jax.named_scope("update"):
                return (loss_sum + l_k, _jax.tree.map(_jnp.add, grad_sum, gw_k)), gx_k

        init = (_jnp.zeros((), _jnp.float32), _jax.tree.map(_jnp.zeros_like, weights))
        (loss, grad_w), grad_x = _jax.lax.scan(body, init, (per_example, given["loss_target"]))
    with _jax.named_scope("update"):
        delta_w, new_m, new_v = {}, {}, {}
        for n in TWIN_WEIGHTS:
            delta_w[n], new_m[n], new_v[n] = _adamw(weights[n], grad_w[n], given["m_" + n], given["v_" + n])
    return (loss, grad_x, *[grad_w[n] for n in TWIN_WEIGHTS], *[delta_w[n] for n in TWIN_WEIGHTS],
            *[new_m[n] for n in TWIN_WEIGHTS], *[new_v[n] for n in TWIN_WEIGHTS])
```

```python
import functools

import jax
import jax.numpy as jnp
from jax import lax
from jax.experimental import pallas as pl
from jax.experimental.pallas import tpu as pltpu

D = 1024
CC = 512
HD = 64
NQ = 8
NKV = 2
GRP = NQ // NKV
ATT = NQ * HD
KVC = NKV * HD
BLK = 128
MH = 4
MHD = D // MH
F = 2816
FCH = F // 2
INC = 2 * CC + ATT + 2 * KVC
TAPS = 31
EPS = 1e-6
ROPE_THETA = 10000.0
SCALE = HD ** -0.5
MSCALE = MHD ** -0.5
LANES = 128
NDEV = 8

ADAM_LR = 0.001
ADAM_B1 = 0.9
ADAM_B2 = 0.999
ADAM_EPS = 1e-08
ADAM_WD = 0.01
ADAM_STEP = 10

F32 = jnp.float32
_MXU = jnp.bfloat16
MESH = pl.DeviceIdType.MESH

W_NAMES = ("w_inT", "w_out", "w_mem_q", "w_mem_kvT", "w_mem_o", "w_gateT", "w_upT", "w_down")
W_ROWS = (INC // NDEV, D // NDEV, D // NDEV, 2 * D // NDEV, D // NDEV, F // NDEV, F // NDEV, F // NDEV)


def _mm(a, b, ca, cb):
    return lax.dot_general(a.astype(_MXU), b.astype(_MXU), (((ca,), (cb,)), ((), ())),
                           preferred_element_type=F32)


def _nn(a, b):
    return _mm(a, b, 1, 0)


def _nt(a, b):
    return _mm(a, b, 1, 1)


def _tn(a, b):
    return _mm(a, b, 0, 0)


def _sigmoid(x):
    return 1.0 / (1.0 + jnp.exp(-x))


def _rms(x, g):
    r = lax.rsqrt(jnp.mean(x * x, axis=-1, keepdims=True) + EPS)
    return x * r * g, r


def _rms_bwd(dy, x, r, g):
    xh = x * r
    dg = jnp.sum(dy * xh, axis=0, keepdims=True)
    dxh = dy * g
    dx = r * (dxh - xh * jnp.mean(dxh * xh, axis=-1, keepdims=True))
    return dx, dg


def _rope(t, c, s):
    n = t.shape[1]
    reps = n // LANES
    if reps > 1:
        c = jnp.tile(c, (1, reps))
        s = jnp.tile(s, (1, reps))
    lane = lax.broadcasted_iota(jnp.int32, t.shape, 1)
    first = (lane & (HD - 1)) < (HD // 2)
    partner = jnp.where(first, pltpu.roll(t, n - HD // 2, 1), pltpu.roll(t, HD // 2, 1))
    return t * c + partner * s


def _rope_bwd(dt, c, s):
    n = dt.shape[1]
    reps = n // LANES
    if reps > 1:
        c = jnp.tile(c, (1, reps))
        s = jnp.tile(s, (1, reps))
    lane = lax.broadcasted_iota(jnp.int32, dt.shape, 1)
    first = (lane & (HD - 1)) < (HD // 2)
    ds = dt * s
    partner = jnp.where(first, pltpu.roll(ds, n - HD // 2, 1), pltpu.roll(ds, HD // 2, 1))
    return dt * c + partner


def _acc_init(i, *refs):
    @pl.when(i == 0)
    def _():
        for r in refs:
            r[...] = jnp.zeros(r.shape, r.dtype)


def _acc_row(ref, v):
    ref[...] += jnp.broadcast_to(v, ref.shape)


def _rows(tm, n):
    return pl.BlockSpec((tm, n), lambda i: (i, 0))


def _full(shape):
    nd = len(shape)
    return pl.BlockSpec(shape, lambda i: (0,) * nd)


def _once(shape):
    nd = len(shape)
    return pl.BlockSpec(shape, lambda i: (0,) * nd, pipeline_mode=pl.Buffered(1))


def _params(vmem_mb):
    return pltpu.CompilerParams(dimension_semantics=("arbitrary",), vmem_limit_bytes=vmem_mb * 2 ** 20)


def _sds(shape, dtype):
    return jax.ShapeDtypeStruct(shape, dtype)


def _mem_kv_fwd(mem, g_kv, wkvT):
    M = mem.shape[0]

    def body(mem_ref, g_ref, w_ref, km_ref, vm_ref, mn_ref):
        y, _ = _rms(mem_ref[...], g_ref[...])
        kv = _nt(y, w_ref[...])
        km_ref[...] = kv[:, :D].astype(km_ref.dtype)
        vm_ref[...] = kv[:, D:].astype(vm_ref.dtype)
        mn_ref[...] = y.astype(mn_ref.dtype)

    return pl.pallas_call(
        body, name="mem_kv_fwd", grid=(1,),
        in_specs=[_full((M, D)), _full((1, D)), _full((2 * D, D))],
        out_specs=[_full((M, D))] * 3,
        out_shape=[_sds((M, D), _MXU)] * 3,
        compiler_params=_params(40),
    )(mem, g_kv, wkvT)


def _in_proj_fwd(x, g_mix, winT, b_in, cosq, sinq, tm=512):
    S = x.shape[0]

    def body(x_ref, g_ref, w_ref, b_ref, c_ref, s_ref, h_ref, uglu_ref, vg_ref, q_ref, k_ref, v_ref):
        h, _ = _rms(x_ref[...], g_ref[...])
        hb = h.astype(h_ref.dtype)
        h_ref[...] = hb
        u = _nt(hb, w_ref[...]) + b_ref[...]
        uglu_ref[...] = u[:, :2 * CC]
        vg_ref[...] = u[:, :CC] * _sigmoid(u[:, CC:2 * CC])
        c = c_ref[...]
        s = s_ref[...]
        q_ref[...] = _rope(u[:, 2 * CC:2 * CC + ATT], c, s).astype(q_ref.dtype)
        k_ref[...] = _rope(u[:, 2 * CC + ATT:2 * CC + ATT + KVC], c, s).astype(k_ref.dtype)
        v_ref[...] = u[:, 2 * CC + ATT + KVC:].astype(v_ref.dtype)

    return pl.pallas_call(
        body, name="in_proj_fwd", grid=(S // tm,),
        in_specs=[_rows(tm, D), _full((1, D)), _full((INC, D)), _full((1, INC)), _rows(tm, LANES), _rows(tm, LANES)],
        out_specs=[_rows(tm, D), _rows(tm, 2 * CC), _rows(tm, CC), _rows(tm, ATT), _rows(tm, KVC), _rows(tm, KVC)],
        out_shape=[_sds((S, D), _MXU), _sds((S, 2 * CC), F32), _sds((S, CC), F32), _sds((S, ATT), _MXU),
                   _sds((S, KVC), _MXU), _sds((S, KVC), _MXU)],
        compiler_params=_params(48),
    )(x, g_mix, winT, b_in, cosq, sinq)


CONV_HALO = 16


def _dwconv_tile(pad_ref, t0, tm, w_ref, flip):
    cols = []
    for cl in range(CC // LANES):
        lanes = slice(cl * LANES, (cl + 1) * LANES)
        blk = pad_ref[pl.ds(t0, tm + 2 * CONV_HALO), lanes]
        acc = jnp.zeros((tm, LANES), F32)
        for b in range(8):
            sh = blk[b:b + tm + 24, :]
            for a in range(4):
                k = 8 * a + b - 1
                if 0 <= k < TAPS:
                    kk = TAPS - 1 - k if flip else k
                    acc = acc + sh[8 * a:8 * a + tm, :] * w_ref[kk:kk + 1, lanes]
        cols.append(acc)
    return jnp.concatenate(cols, axis=1)


def _fill_padded(i, src_hbm, pad_ref, sem, S):
    @pl.when(i == 0)
    def _():
        zeros = jnp.zeros((CONV_HALO, pad_ref.shape[1]), pad_ref.dtype)
        pad_ref[0:CONV_HALO, :] = zeros
        pad_ref[CONV_HALO + S:2 * CONV_HALO + S, :] = zeros
        cp = pltpu.make_async_copy(src_hbm, pad_ref.at[pl.ds(CONV_HALO, S), :], sem)
        cp.start()
        cp.wait()


def _ln_silu(y, g, b):
    mu = jnp.mean(y, axis=-1, keepdims=True)
    yc = y - mu
    var = jnp.mean(yc * yc, axis=-1, keepdims=True)
    rstd = lax.rsqrt(var + EPS)
    yh = yc * rstd
    z = yh * g + b
    return yh, rstd, z


def _conv_fwd(vg, w_dw, b_dw, ln_g, ln_b, tm=128):
    S = vg.shape[0]

    def body(vg_hbm, w_ref, b_ref, g_ref, bb_ref, y_ref, yc_ref, pad_ref, sem):
        i = pl.program_id(0)
        _fill_padded(i, vg_hbm, pad_ref, sem, S)
        t0 = pl.multiple_of(i * tm, tm)
        y = _dwconv_tile(pad_ref, t0, tm, w_ref, False) + b_ref[...]
        y_ref[...] = y
        _, _, z = _ln_silu(y, g_ref[...], bb_ref[...])
        yc_ref[...] = (z * _sigmoid(z)).astype(yc_ref.dtype)

    return pl.pallas_call(
        body, name="conv_fwd", grid=(S // tm,),
        in_specs=[pl.BlockSpec(memory_space=pl.ANY), _full((TAPS, CC)), _full((1, CC)), _full((1, CC)), _full((1, CC))],
        out_specs=[_rows(tm, CC), _rows(tm, CC)],
        out_shape=[_sds((S, CC), F32), _sds((S, CC), _MXU)],
        scratch_shapes=[pltpu.VMEM((S + 2 * CONV_HALO, CC), F32), pltpu.SemaphoreType.DMA],
        compiler_params=_params(40),
    )(vg, w_dw, b_dw, ln_g, ln_b)


def _attn_window(i, S):
    start = jnp.clip((i - 1) * BLK, 0, S - 3 * BLK)
    return pl.multiple_of(start, BLK)


def _attn_mask(i, start):
    rows = lax.broadcasted_iota(jnp.int32, (GRP * BLK, 3 * BLK), 0)
    cols = lax.broadcasted_iota(jnp.int32, (GRP * BLK, 3 * BLK), 1)
    rel = (start + cols) - (i * BLK + (rows & (BLK - 1)))
    return jnp.abs(rel) <= BLK


def _stack_heads(t, hk):
    return jnp.concatenate([t[:, (hk * GRP + g) * HD:(hk * GRP + g + 1) * HD] for g in range(GRP)], axis=0)


def _attn_probs(qs, kh, sk_ref, hk, mask):
    s = _nt(qs, kh) * SCALE
    s = jnp.where(mask, s, -1e30)
    sk = jnp.concatenate(
        [jnp.broadcast_to(sk_ref[0:1, hk * GRP + g:hk * GRP + g + 1], (BLK, 1)) for g in range(GRP)], axis=0)
    m = jnp.maximum(jnp.max(s, axis=-1, keepdims=True), sk)
    e = jnp.exp(s - m)
    es = jnp.exp(sk - m)
    den = jnp.sum(e, axis=-1, keepdims=True) + es
    return e / den, es / den


def _attn_fwd(qr, kr, vv, sink):
    S = qr.shape[0]

    def body(q_ref, k_ref, v_ref, sk_ref, o_ref):
        i = pl.program_id(0)
        start = _attn_window(i, S)
        kb = k_ref[pl.ds(start, 3 * BLK), :]
        vb = v_ref[pl.ds(start, 3 * BLK), :]
        q = q_ref[...]
        mask = _attn_mask(i, start)
        outs = []
        for hk in range(NKV):
            p, _ = _attn_probs(_stack_heads(q, hk), kb[:, hk * HD:(hk + 1) * HD], sk_ref, hk, mask)
            o = _nn(p, vb[:, hk * HD:(hk + 1) * HD])
            outs += [o[g * BLK:(g + 1) * BLK, :] for g in range(GRP)]
        o_ref[...] = jnp.concatenate(outs, axis=1).astype(o_ref.dtype)

    return pl.pallas_call(
        body, name="attn_fwd", grid=(S // BLK,),
        in_specs=[_rows(BLK, ATT), _full((S, KVC)), _full((S, KVC)), _full((1, NQ))],
        out_specs=_rows(BLK, ATT),
        out_shape=_sds((S, ATT), _MXU),
        compiler_params=_params(40),
    )(qr, kr, vv, sink)


def _out_proj_fwd(x, yc, ya, wout, b_out, tm=512):
    S = x.shape[0]

    def body(x_ref, yc_ref, ya_ref, w_ref, b_ref, x1_ref):
        x1_ref[...] = (x_ref[...] + _nn(yc_ref[...], w_ref[0:CC, :]) + _nn(ya_ref[...], w_ref[CC:2 * CC, :])
                       + b_ref[...])

    return pl.pallas_call(
        body, name="out_proj_fwd", grid=(S // tm,),
        in_specs=[_rows(tm, D), _rows(tm, CC), _rows(tm, ATT), _full((D, D)), _full((1, D))],
        out_specs=_rows(tm, D),
        out_shape=_sds((S, D), F32),
        compiler_params=_params(40),
    )(x, yc, ya, wout, b_out)


def _softmax(s):
    m = jnp.max(s, axis=-1, keepdims=True)
    e = jnp.exp(s - m)
    return e / jnp.sum(e, axis=-1, keepdims=True)


def _mem_attn_fwd(x1, g_q, wq, km, vm, wo, tm=512):
    S = x1.shape[0]
    M = km.shape[0]

    def body(x1_ref, g_ref, wq_ref, km_ref, vm_ref, wo_ref, x2_ref, qm_ref, om_ref):
        x1v = x1_ref[...]
        hq, _ = _rms(x1v, g_ref[...])
        qm = _nn(hq, wq_ref[...]).astype(qm_ref.dtype)
        qm_ref[...] = qm
        oms = []
        for h in range(MH):
            sl = slice(h * MHD, (h + 1) * MHD)
            p = _softmax(_nt(qm[:, sl], km_ref[:, sl]) * MSCALE)
            oms.append(_nn(p, vm_ref[:, sl]))
        om = jnp.concatenate(oms, axis=1).astype(om_ref.dtype)
        om_ref[...] = om
        x2_ref[...] = x1v + _nn(om, wo_ref[...])

    return pl.pallas_call(
        body, name="mem_attn_fwd", grid=(S // tm,),
        in_specs=[_rows(tm, D), _full((1, D)), _full((D, D)), _full((M, D)), _full((M, D)), _full((D, D))],
        out_specs=[_rows(tm, D)] * 3,
        out_shape=[_sds((S, D), F32), _sds((S, D), _MXU), _sds((S, D), _MXU)],
        compiler_params=_params(48),
    )(x1, g_q, wq, km, vm, wo)


def _ffn_loss_fwd(x2, tgt, g_ffn, g_final, wgT, wuT, wd, tm=256):
    S = x2.shape[0]

    def body(x2_ref, t_ref, gf_ref, gl_ref, wg_ref, wu_ref, wd_ref,
             gt_ref, up_ref, hf_ref, dx3_ref, dx3b_ref, loss_ref, dgl_ref):
        i = pl.program_id(0)
        _acc_init(i, loss_ref, dgl_ref)
        x2v = x2_ref[...]
        hf, _ = _rms(x2v, gf_ref[...])
        hb = hf.astype(hf_ref.dtype)
        hf_ref[...] = hb
        x3 = x2v
        for c in range(F // FCH):
            sl = slice(c * FCH, (c + 1) * FCH)
            gt = _nt(hb, wg_ref[sl, :])
            up = _nt(hb, wu_ref[sl, :])
            gt_ref[:, sl] = gt.astype(gt_ref.dtype)
            up_ref[:, sl] = up.astype(up_ref.dtype)
            x3 = x3 + _nn(gt * _sigmoid(gt) * up, wd_ref[sl, :])
        gl = gl_ref[...]
        y, r = _rms(x3, gl)
        diff = y - t_ref[...]
        _acc_row(loss_ref, jnp.sum(jnp.sum(diff * diff, axis=-1, keepdims=True), axis=0, keepdims=True) * (0.5 / D))
        dx3, dgl = _rms_bwd(diff * (1.0 / D), x3, r, gl)
        dx3_ref[...] = dx3
        dx3b_ref[...] = dx3.astype(dx3b_ref.dtype)
        _acc_row(dgl_ref, dgl)

    return pl.pallas_call(
        body, name="ffn_loss_fwd", grid=(S // tm,),
        in_specs=[_rows(tm, D), _rows(tm, D), _full((1, D)), _full((1, D)),
                  _once((F, D)), _once((F, D)), _once((F, D))],
        out_specs=[_rows(tm, F), _rows(tm, F), _rows(tm, D), _rows(tm, D), _rows(tm, D),
                   _full((8, LANES)), _full((8, D))],
        out_shape=[_sds((S, F), _MXU), _sds((S, F), _MXU), _sds((S, D), _MXU), _sds((S, D), F32),
                   _sds((S, D), _MXU), _sds((8, LANES), F32), _sds((8, D), F32)],
        compiler_params=_params(56),
    )(x2, tgt, g_ffn, g_final, wgT, wuT, wd)


def _ffn_bwd(dx3, x2, gt, up, g_ffn, wd, wgT, wuT, tm=256):
    S = x2.shape[0]

    def body(dx3_ref, x2_ref, gt_ref, up_ref, g_ref, wd_ref, wg_ref, wu_ref,
             dx2_ref, dx2b_ref, dgt_ref, dup_ref, act_ref, dg_ref):
        i = pl.program_id(0)
        _acc_init(i, dg_ref)
        dx3v = dx3_ref[...]
        db = dx3v.astype(_MXU)
        dhf = jnp.zeros((tm, D), F32)
        for c in range(F // FCH):
            sl = slice(c * FCH, (c + 1) * FCH)
            gtv = gt_ref[:, sl].astype(F32)
            upv = up_ref[:, sl].astype(F32)
            dact = _nt(db, wd_ref[sl, :])
            sig = _sigmoid(gtv)
            silu = gtv * sig
            dup = (dact * silu).astype(dup_ref.dtype)
            dgt = (dact * upv * (sig * (1.0 + gtv * (1.0 - sig)))).astype(dgt_ref.dtype)
            act_ref[:, sl] = (silu * upv).astype(act_ref.dtype)
            dgt_ref[:, sl] = dgt
            dup_ref[:, sl] = dup
            dhf = dhf + _nn(dgt, wg_ref[sl, :]) + _nn(dup, wu_ref[sl, :])
        x2v = x2_ref[...]
        g = g_ref[...]
        _, r = _rms(x2v, g)
        dxn, dg = _rms_bwd(dhf, x2v, r, g)
        dx2 = dx3v + dxn
        dx2_ref[...] = dx2
        dx2b_ref[...] = dx2.astype(dx2b_ref.dtype)
        _acc_row(dg_ref, dg)

    return pl.pallas_call(
        body, name="ffn_bwd", grid=(S // tm,),
        in_specs=[_rows(tm, D), _rows(tm, D), _rows(tm, F), _rows(tm, F), _full((1, D)),
                  _once((F, D)), _once((F, D)), _once((F, D))],
        out_specs=[_rows(tm, D), _rows(tm, D), _rows(tm, F), _rows(tm, F), _rows(tm, F), _full((8, D))],
        out_shape=[_sds((S, D), F32), _sds((S, D), _MXU), _sds((S, F), _MXU), _sds((S, F), _MXU),
                   _sds((S, F), _MXU), _sds((8, D), F32)],
        compiler_params=_params(56),
    )(dx3, x2, gt, up, g_ffn, wd, wgT, wuT)


def _wgrad(a, b, name, rc=256):
    S, R = a.shape
    N = b.shape[1]

    def body(a_ref, b_ref, o_ref, ob_ref):
        g = _tn(a_ref[...], b_ref[...])
        o_ref[...] = g
        ob_ref[...] = g.astype(ob_ref.dtype)

    return pl.pallas_call(
        body, name=name, grid=(R // rc,),
        in_specs=[pl.BlockSpec((S, rc), lambda j: (0, j)), _once((S, N))],
        out_specs=[_rows(rc, N), _rows(rc, N)],
        out_shape=[_sds((R, N), F32), _sds((R, N), jnp.bfloat16)],
        compiler_params=_params(48),
    )(a, b)


def _mem_attn_bwd(dx2, x1, qm, km, vm, g_q, wq, wo, wout, tm=512):
    S = x1.shape[0]
    M = km.shape[0]

    def body(dx2_ref, x1_ref, qm_ref, km_ref, vm_ref, g_ref, wq_ref, wo_ref, wout_ref,
             dx1_ref, dx1b_ref, hq_ref, dqm_ref, dyc_ref, dya_ref, dkm_ref, dvm_ref, dg_ref, dbo_ref):
        i = pl.program_id(0)
        _acc_init(i, dkm_ref, dvm_ref, dg_ref, dbo_ref)
        dx2v = dx2_ref[...]
        dom = _nt(dx2v, wo_ref[...]).astype(_MXU)
        dqs = []
        for h in range(MH):
            sl = slice(h * MHD, (h + 1) * MHD)
            qh = qm_ref[:, sl]
            kh = km_ref[:, sl]
            p = _softmax(_nt(qh, kh) * MSCALE)
            dp = _nt(dom[:, sl], vm_ref[:, sl])
            ds = (p * (dp - jnp.sum(dp * p, axis=-1, keepdims=True))).astype(_MXU)
            dqs.append(_nn(ds, kh) * MSCALE)
            dkm_ref[:, sl] += _tn(ds, qh) * MSCALE
            dvm_ref[:, sl] += _tn(p, dom[:, sl])
        dqm = jnp.concatenate(dqs, axis=1).astype(dqm_ref.dtype)
        dqm_ref[...] = dqm
        dhq = _nt(dqm, wq_ref[...])
        x1v = x1_ref[...]
        g = g_ref[...]
        hq, r = _rms(x1v, g)
        hq_ref[...] = hq.astype(hq_ref.dtype)
        dxn, dg = _rms_bwd(dhq, x1v, r, g)
        dx1 = dx2v + dxn
        dx1_ref[...] = dx1
        dx1b = dx1.astype(dx1b_ref.dtype)
        dx1b_ref[...] = dx1b
        dym = _nt(dx1b, wout_ref[...])
        dyc_ref[...] = dym[:, :CC]
        dya_ref[...] = dym[:, CC:].astype(dya_ref.dtype)
        _acc_row(dg_ref, dg)
        _acc_row(dbo_ref, jnp.sum(dx1, axis=0, keepdims=True))

    return pl.pallas_call(
        body, name="mem_attn_bwd", grid=(S // tm,),
        in_specs=[_rows(tm, D), _rows(tm, D), _rows(tm, D), _full((M, D)), _full((M, D)), _full((1, D)),
                  _once((D, D)), _once((D, D)), _once((D, D))],
        out_specs=[_rows(tm, D), _rows(tm, D), _rows(tm, D), _rows(tm, D), _rows(tm, CC), _rows(tm, ATT),
                   _full((M, D)), _full((M, D)), _full((8, D)), _full((8, D))],
        out_shape=[_sds((S, D), F32), _sds((S, D), _MXU), _sds((S, D), _MXU), _sds((S, D), _MXU),
                   _sds((S, CC), F32), _sds((S, ATT), _MXU),
                   _sds((M, D), F32), _sds((M, D), F32), _sds((8, D), F32), _sds((8, D), F32)],
        compiler_params=_params(56),
    )(dx2, x1, qm, km, vm, g_q, wq, wo, wout)


def _mem_kv_bwd(dkm, dvm, mem, g_kv, memn, wkvT):
    M = mem.shape[0]

    def body(dkm_ref, dvm_ref, mem_ref, g_ref, mn_ref, w_ref, gw_ref, gwb_ref, dg_ref):
        dkv = jnp.concatenate([dkm_ref[...], dvm_ref[...]], axis=1).astype(_MXU)
        gw = _tn(dkv, mn_ref[...])
        gw_ref[...] = gw
        gwb_ref[...] = gw.astype(gwb_ref.dtype)
        dmn = _nn(dkv, w_ref[...])
        memv = mem_ref[...]
        g = g_ref[...]
        _, r = _rms(memv, g)
        _, dg = _rms_bwd(dmn, memv, r, g)
        dg_ref[...] = jnp.broadcast_to(dg, dg_ref.shape)

    return pl.pallas_call(
        body, name="mem_kv_bwd", grid=(1,),
        in_specs=[_full((M, D)), _full((M, D)), _full((M, D)), _full((1, D)), _full((M, D)), _full((2 * D, D))],
        out_specs=[_full((2 * D, D)), _full((2 * D, D)), _full((8, D))],
        out_shape=[_sds((2 * D, D), F32), _sds((2 * D, D), jnp.bfloat16), _sds((8, D), F32)],
        compiler_params=_params(48),
    )(dkm, dvm, mem, g_kv, memn, wkvT)


def _attn_bwd(qr, kr, vv, sink, dya):
    S = qr.shape[0]

    def body(q_ref, k_ref, v_ref, sk_ref, do_ref, dq_ref, dk_ref, dv_ref, dsk_ref):
        i = pl.program_id(0)
        _acc_init(i, dk_ref, dv_ref, dsk_ref)
        start = _attn_window(i, S)
        kb = k_ref[pl.ds(start, 3 * BLK), :]
        vb = v_ref[pl.ds(start, 3 * BLK), :]
        q = q_ref[...]
        do = do_ref[...]
        mask = _attn_mask(i, start)
        dqs, dks, dvs = [], [], []
        for hk in range(NKV):
            hs = slice(hk * HD, (hk + 1) * HD)
            qs = _stack_heads(q, hk)
            dos = _stack_heads(do, hk)
            p, ps = _attn_probs(qs, kb[:, hs], sk_ref, hk, mask)
            dp = _nt(dos, vb[:, hs])
            delta = jnp.sum(dp * p, axis=-1, keepdims=True)
            ds = (p * (dp - delta)).astype(_MXU)
            dq = _nn(ds, kb[:, hs]) * SCALE
            dqs += [dq[g * BLK:(g + 1) * BLK, :] for g in range(GRP)]
            dks.append(_tn(ds, qs) * SCALE)
            dvs.append(_tn(p, dos))
            dsk = ps * delta
            for g in range(GRP):
                h = hk * GRP + g
                tot = jnp.sum(dsk[g * BLK:(g + 1) * BLK, :], axis=0, keepdims=True)
                dsk_ref[h:h + 1, :] -= jnp.broadcast_to(tot, (1, LANES))
        dq_ref[...] = jnp.concatenate(dqs, axis=1)
        dk_ref[pl.ds(start, 3 * BLK), :] += jnp.concatenate(dks, axis=1)
        dv_ref[pl.ds(start, 3 * BLK), :] += jnp.concatenate(dvs, axis=1)

    return pl.pallas_call(
        body, name="attn_bwd", grid=(S // BLK,),
        in_specs=[_rows(BLK, ATT), _full((S, KVC)), _full((S, KVC)), _full((1, NQ)), _rows(BLK, ATT)],
        out_specs=[_rows(BLK, ATT), _full((S, KVC)), _full((S, KVC)), _full((8, LANES))],
        out_shape=[_sds((S, ATT), F32), _sds((S, KVC), F32), _sds((S, KVC), F32), _sds((8, LANES), F32)],
        compiler_params=_params(40),
    )(qr, kr, vv, sink, dya)


def _conv_norm_bwd(dyc, y, ln_g, ln_b, tm=512):
    S = y.shape[0]

    def body(dyc_ref, y_ref, g_ref, b_ref, dy_ref, dg_ref, db_ref, dbdw_ref):
        i = pl.program_id(0)
        _acc_init(i, dg_ref, db_ref, dbdw_ref)
        g = g_ref[...]
        yh, rstd, z = _ln_silu(y_ref[...], g, b_ref[...])
        sig = _sigmoid(z)
        dz = dyc_ref[...] * (sig * (1.0 + z * (1.0 - sig)))
        dyh = dz * g
        dy = rstd * (dyh - jnp.mean(dyh, axis=-1, keepdims=True)
                     - yh * jnp.mean(dyh * yh, axis=-1, keepdims=True))
        dy_ref[...] = dy
        _acc_row(dg_ref, jnp.sum(dz * yh, axis=0, keepdims=True))
        _acc_row(db_ref, jnp.sum(dz, axis=0, keepdims=True))
        _acc_row(dbdw_ref, jnp.sum(dy, axis=0, keepdims=True))

    return pl.pallas_call(
        body, name="conv_norm_bwd", grid=(S // tm,),
        in_specs=[_rows(tm, CC), _rows(tm, CC), _full((1, CC)), _full((1, CC))],
        out_specs=[_rows(tm, CC), _full((8, CC)), _full((8, CC)), _full((8, CC))],
        out_shape=[_sds((S, CC), F32), _sds((8, CC), F32), _sds((8, CC), F32), _sds((8, CC), F32)],
        compiler_params=_params(40),
    )(dyc, y, ln_g, ln_b)


def _conv_bwd(dy, vg, uglu, w_dw, tm=128):
    S = dy.shape[0]

    def body(dy_hbm, vg_hbm, uglu_ref, w_ref, du_ref, gw_ref, dbin_ref, dyp_ref, vgp_ref, gacc_ref, sems):
        i = pl.program_id(0)
        _acc_init(i, gacc_ref, dbin_ref)
        _fill_padded(i, dy_hbm, dyp_ref, sems.at[0], S)
        _fill_padded(i, vg_hbm, vgp_ref, sems.at[1], S)
        t0 = pl.multiple_of(i * tm, tm)
        dvg = _dwconv_tile(dyp_ref, t0, tm, w_ref, True)
        for cl in range(CC // LANES):
            lanes = slice(cl * LANES, (cl + 1) * LANES)
            dyt = dyp_ref[pl.ds(t0 + CONV_HALO, tm), lanes]
            blk = vgp_ref[pl.ds(t0, tm + 2 * CONV_HALO), lanes]
            for b in range(8):
                sh = blk[b:b + tm + 24, :]
                for a in range(4):
                    k = 8 * a + b - 1
                    if 0 <= k < TAPS:
                        prod = dyt * sh[8 * a:8 * a + tm, :]
                        gacc_ref[k, :, lanes] += jnp.sum(prod.reshape(tm // 8, 8, LANES), axis=0)
        a_v = uglu_ref[:, :CC]
        sg = _sigmoid(uglu_ref[:, CC:])
        da = dvg * sg
        dgate = dvg * a_v * sg * (1.0 - sg)
        du = jnp.concatenate([da, dgate], axis=1)
        du_ref[...] = du.astype(du_ref.dtype)
        _acc_row(dbin_ref, jnp.sum(du, axis=0, keepdims=True))

        @pl.when(i == pl.num_programs(0) - 1)
        def _():
            gw_ref[...] = jnp.zeros(gw_ref.shape, F32)
            for k in range(TAPS):
                gw_ref[k:k + 1, :] = jnp.sum(gacc_ref[k], axis=0, keepdims=True)

    return pl.pallas_call(
        body, name="conv_bwd", grid=(S // tm,),
        in_specs=[pl.BlockSpec(memory_space=pl.ANY), pl.BlockSpec(memory_space=pl.ANY), _rows(tm, 2 * CC),
                  _full((TAPS, CC))],
        out_specs=[_rows(tm, 2 * CC), _full((TAPS + 1, CC)), _full((8, 2 * CC))],
        out_shape=[_sds((S, 2 * CC), _MXU), _sds((TAPS + 1, CC), F32), _sds((8, 2 * CC), F32)],
        scratch_shapes=[pltpu.VMEM((S + 2 * CONV_HALO, CC), F32), pltpu.VMEM((S + 2 * CONV_HALO, CC), F32),
                        pltpu.VMEM((TAPS, 8, CC), F32), pltpu.SemaphoreType.DMA((2,))],
        compiler_params=_params(48),
    )(dy, vg, uglu, w_dw)


def _in_proj_bwd(duglu, dqr, dkr, dv, cosq, sinq, x, dx1, g_mix, winT, tm=512):
    S = x.shape[0]
    NA = ATT + 2 * KVC

    def body(dug_ref, dq_ref, dk_ref, dv_ref, c_ref, s_ref, x_ref, dx1_ref, g_ref, w_ref,
             dx_ref, du_ref, dg_ref, dbin_ref):
        i = pl.program_id(0)
        _acc_init(i, dg_ref, dbin_ref)
        c = c_ref[...]
        s = s_ref[...]
        datt = jnp.concatenate([_rope_bwd(dq_ref[...], c, s), _rope_bwd(dk_ref[...], c, s), dv_ref[...]], axis=1)
        _acc_row(dbin_ref, jnp.sum(datt, axis=0, keepdims=True))
        du = jnp.concatenate([dug_ref[...], datt.astype(du_ref.dtype)], axis=1)
        du_ref[...] = du
        dh = _nn(du, w_ref[...])
        xv = x_ref[...]
        g = g_ref[...]
        _, r = _rms(xv, g)
        dxn, dg = _rms_bwd(dh, xv, r, g)
        dx_ref[...] = dx1_ref[...] + dxn
        _acc_row(dg_ref, dg)

    return pl.pallas_call(
        body, name="in_proj_bwd", grid=(S // tm,),
        in_specs=[_rows(tm, 2 * CC), _rows(tm, ATT), _rows(tm, KVC), _rows(tm, KVC), _rows(tm, LANES),
                  _rows(tm, LANES), _rows(tm, D), _rows(tm, D), _full((1, D)), _once((INC, D))],
        out_specs=[_rows(tm, D), _rows(tm, INC), _full((8, D)), _full((8, NA))],
        out_shape=[_sds((S, D), F32), _sds((S, INC), _MXU), _sds((8, D), F32), _sds((8, NA), F32)],
        compiler_params=_params(48),
    )(duglu, dqr, dkr, dv, cosq, sinq, x, dx1, g_mix, winT)


def _rope_tables(S):
    pos = jnp.arange(S, dtype=F32)
    inv_freq = ROPE_THETA ** (-jnp.arange(0, HD, 2, dtype=F32) / HD)
    ang = pos[:, None] * inv_freq[None, :]
    cos, sin = jnp.cos(ang), jnp.sin(ang)
    cosq = jnp.tile(jnp.concatenate([cos, cos], axis=1), (1, LANES // HD))
    sinq = jnp.tile(jnp.concatenate([-sin, sin], axis=1), (1, LANES // HD))
    return cosq, sinq


def _local_step(x, mem, tgt, sp, wfull, w_dw_full):
    S = x.shape[0]
    winT, wout, wq, wkvT, wo, wgT, wuT, wd = [wfull[n] for n in W_NAMES]
    cosq, sinq = _rope_tables(S)

    km, vm, memn = _mem_kv_fwd(mem, sp["g_mem_kv"], wkvT)
    h, uglu, vg, qr, kr, vv = _in_proj_fwd(x, sp["g_mix"], winT, sp["b_in"], cosq, sinq)
    y, yc = _conv_fwd(vg, w_dw_full, sp["b_dw"], sp["g_conv_ln"], sp["b_conv_ln"])
    ya = _attn_fwd(qr, kr, vv, sp["attn_sink"])
    x1 = _out_proj_fwd(x, yc, ya, wout, sp["b_out"])
    x2, qm, om = _mem_attn_fwd(x1, sp["g_mem_q"], wq, km, vm, wo)
    gt, up, hf, dx3, dx3b, loss_acc, dg_final = _ffn_loss_fwd(x2, tgt, sp["g_ffn"], sp["g_final"], wgT, wuT, wd)

    dx2, dx2b, dgt, dup, act, dg_ffn = _ffn_bwd(dx3, x2, gt, up, sp["g_ffn"], wd, wgT, wuT)
    g_wgT = _wgrad(dgt, hf, "wgrad_gate")
    g_wuT = _wgrad(dup, hf, "wgrad_up")
    g_wd = _wgrad(act, dx3b, "wgrad_down")

    dx1, dx1b, hq, dqm, dyc, dya, dkm, dvm, dg_mem_q, db_out = _mem_attn_bwd(
        dx2, x1, qm, km, vm, sp["g_mem_q"], wq, wo, wout)
    g_wo = _wgrad(om, dx2b, "wgrad_mem_o")
    g_wq = _wgrad(hq, dqm, "wgrad_mem_q")
    g_wkvT_f, g_wkvT_b, dg_mem_kv = _mem_kv_bwd(dkm, dvm, mem, sp["g_mem_kv"], memn, wkvT)

    g_wout_c = _wgrad(yc, dx1b, "wgrad_out_conv")
    g_wout_a = _wgrad(ya, dx1b, "wgrad_out_attn")
    g_wout = tuple(jnp.concatenate([c, a], axis=0) for c, a in zip(g_wout_c, g_wout_a))

    dqr, dkr, dvv, dsink = _attn_bwd(qr, kr, vv, sp["attn_sink"], dya)
    dy, dg_cln, db_cln, db_dw = _conv_norm_bwd(dyc, y, sp["g_conv_ln"], sp["b_conv_ln"])
    duglu, g_wdw, dbin_glu = _conv_bwd(dy, vg, uglu, w_dw_full)
    grad_x, du, dg_mix, dbin_att = _in_proj_bwd(duglu, dqr, dkr, dvv, cosq, sinq, x, dx1, sp["g_mix"], winT)
    g_winT = _wgrad(du, h, "wgrad_in")

    small = {
        "loss": loss_acc[0:1, :],
        "g_mix": dg_mix[0:1], "b_in": jnp.concatenate([dbin_glu[0:1], dbin_att[0:1]], axis=1),
        "b_dw": db_dw[0:1], "g_conv_ln": dg_cln[0:1], "b_conv_ln": db_cln[0:1],
        "attn_sink": dsink[:, 0].reshape(1, NQ), "b_out": db_out[0:1], "g_mem_q": dg_mem_q[0:1],
        "g_mem_kv": dg_mem_kv[0:1], "g_ffn": dg_ffn[0:1], "g_final": dg_final[0:1],
    }
    wgrads = [g_winT, g_wout, g_wq, (g_wkvT_f, g_wkvT_b), g_wo, g_wgT, g_wuT, g_wd]
    return grad_x, small, wgrads, g_wdw


def _place():
    return lax.axis_index("x"), lax.axis_index("y"), lax.axis_index("c")


def _hbm_specs(n):
    return [pl.BlockSpec(memory_space=pl.ANY)] * n


def _all_gather(blocks):
    n = len(blocks)

    def body(*refs):
        ins, outs = refs[:n], refs[n:2 * n]
        send_sems, recv_sems, local_sems = refs[2 * n:]
        x, y, c = _place()
        me, sibling = (x, y, c), (x, y, 1 - c)
        chips = [(1 - x, y), (x, 1 - y), (1 - x, 1 - y)]

        def slot(k, p):
            return outs[k].at[4 * p[0] + 2 * p[1] + p[2]]

        def copy(k, j, block, to, src=None):
            return pltpu.make_async_remote_copy(
                src_ref=slot(k, block) if src is None else src, dst_ref=slot(k, block),
                send_sem=send_sems.at[7 * k + j], recv_sem=recv_sems.at[7 * k + j],
                device_id=to, device_id_type=MESH)

        mine = [pltpu.make_async_copy(ins[k], slot(k, me), local_sems.at[k]) for k in range(n)]
        for cp in mine:
            cp.start()
        first = []
        for k in range(n):
            first.append(copy(k, 0, me, sibling, src=ins[k]))
            first += [copy(k, 1 + j, me, (*chip, c), src=ins[k]) for j, chip in enumerate(chips)]
        for cp in first:
            cp.start()
        passed = []
        for j, chip in enumerate(chips):
            for k in range(n):
                copy(k, 1 + j, (*chip, c), me).wait_recv()
                fwd = copy(k, 4 + j, (*chip, c), sibling)
                fwd.start()
                passed.append(fwd)
        for k in range(n):
            copy(k, 0, sibling, me).wait_recv()
        for j, chip in enumerate(chips):
            for k in range(n):
                copy(k, 4 + j, (*chip, 1 - c), me).wait_recv()
        for cp in first + passed:
            cp.wait_send()
        for cp in mine:
            cp.wait()

    return pl.pallas_call(
        body, name="all_gather_weights",
        in_specs=_hbm_specs(n), out_specs=_hbm_specs(n),
        out_shape=[_sds((NDEV,) + b.shape, b.dtype) for b in blocks],
        scratch_shapes=[pltpu.SemaphoreType.DMA((7 * n,)), pltpu.SemaphoreType.DMA((7 * n,)),
                        pltpu.SemaphoreType.DMA((n,))],
    )(*blocks)


def _sibling_exchange(gbs):
    n = len(gbs)

    def body(*refs):
        ins, outs = refs[:n], refs[n:2 * n]
        send_sems, recv_sems = refs[2 * n:]
        x, y, c = _place()
        copies = []
        for k in range(n):
            for s in range(4):
                copies.append(pltpu.make_async_remote_copy(
                    src_ref=ins[k].at[2 * s + (1 - c)], dst_ref=outs[k].at[s],
                    send_sem=send_sems.at[4 * k + s], recv_sem=recv_sems.at[4 * k + s],
                    device_id=(x, y, 1 - c), device_id_type=MESH))
        for cp in copies:
            cp.start()
        for cp in copies:
            cp.wait_recv()
        for cp in copies:
            cp.wait_send()

    return pl.pallas_call(
        body, name="grad_sibling_exchange",
        in_specs=_hbm_specs(n), out_specs=_hbm_specs(n),
        out_shape=[_sds((4,) + g.shape[1:], g.dtype) for g in gbs],
        scratch_shapes=[pltpu.SemaphoreType.DMA((4 * n,)), pltpu.SemaphoreType.DMA((4 * n,))],
    )(*gbs)


def _pair_sum(gfs, ts, c_idx):
    n = len(gfs)
    halves = 2

    def body(c_ref, *refs):
        for k in range(n):
            refs[2 * n + k][...] = (refs[k][...] + refs[n + k][...].astype(F32)).astype(jnp.bfloat16)

    in_specs, out_specs, out_shape = [], [], []
    for g in gfs:
        rh = g.shape[1] // halves
        in_specs.append(pl.BlockSpec((None, rh, D), lambda s, h, c_ref: (2 * s + c_ref[0], h, 0)))
    for g in gfs:
        rh = g.shape[1] // halves
        in_specs.append(pl.BlockSpec((None, rh, D), lambda s, h, c_ref: (s, h, 0)))
        out_specs.append(pl.BlockSpec((None, rh, D), lambda s, h, c_ref: (s, h, 0)))
        out_shape.append(_sds((4, g.shape[1], D), jnp.bfloat16))
    return pl.pallas_call(
        body, name="grad_pair_sum",
        grid_spec=pltpu.PrefetchScalarGridSpec(num_scalar_prefetch=1, grid=(4, halves), in_specs=in_specs,
                                               out_specs=out_specs),
        out_shape=out_shape,
        compiler_params=pltpu.CompilerParams(dimension_semantics=("arbitrary", "arbitrary"),
                                             vmem_limit_bytes=48 * 2 ** 20),
    )(c_idx, *gfs, *ts)


def _chip_exchange(pbs, pack, gdw):
    n = len(pbs)

    def body(*refs):
        ins, pack_ref, gdw_ref = refs[:n], refs[n], refs[n + 1]
        outs, packs_ref, gdws_ref = refs[n + 2:2 * n + 2], refs[2 * n + 2], refs[2 * n + 3]
        send_sems, recv_sems, local_sems, s_send, s_recv = refs[2 * n + 4:]
        x, y, c = _place()
        own = 2 * x + y
        me = 4 * x + 2 * y + c
        chips = [(1 - x, y), (x, 1 - y), (1 - x, 1 - y)]
        local = [pltpu.make_async_copy(ins[k].at[own], outs[k].at[own], local_sems.at[k]) for k in range(n)]
        local.append(pltpu.make_async_copy(pack_ref, packs_ref.at[me], local_sems.at[n]))
        local.append(pltpu.make_async_copy(gdw_ref, gdws_ref.at[me], local_sems.at[n + 1]))
        for cp in local:
            cp.start()
        copies = []
        for k in range(n):
            for j, chip in enumerate(chips):
                copies.append(pltpu.make_async_remote_copy(
                    src_ref=ins[k].at[2 * chip[0] + chip[1]], dst_ref=outs[k].at[own],
                    send_sem=send_sems.at[3 * k + j], recv_sem=recv_sems.at[3 * k + j],
                    device_id=(*chip, c), device_id_type=MESH))
        peers = [(x, y, 1 - c)] + [(*chip, cc) for chip in chips for cc in (c, 1 - c)]
        for j, peer in enumerate(peers):
            copies.append(pltpu.make_async_remote_copy(
                src_ref=pack_ref, dst_ref=packs_ref.at[me], send_sem=s_send.at[2 * j], recv_sem=s_recv.at[2 * j],
                device_id=peer, device_id_type=MESH))
            copies.append(pltpu.make_async_remote_copy(
                src_ref=gdw_ref, dst_ref=gdws_ref.at[me], send_sem=s_send.at[2 * j + 1],
                recv_sem=s_recv.at[2 * j + 1], device_id=peer, device_id_type=MESH))
        for cp in copies:
            cp.start()
        for cp in copies:
            cp.wait_recv()
        for cp in copies:
            cp.wait_send()
        for cp in local:
            cp.wait()

    return pl.pallas_call(
        body, name="grad_chip_exchange",
        in_specs=_hbm_specs(n + 2), out_specs=_hbm_specs(n + 2),
        out_shape=[_sds(p.shape, p.dtype) for p in pbs] + [_sds((NDEV,) + pack.shape, F32),
                                                            _sds((NDEV,) + gdw.shape, F32)],
        scratch_shapes=[pltpu.SemaphoreType.DMA((3 * n,)), pltpu.SemaphoreType.DMA((3 * n,)),
                        pltpu.SemaphoreType.DMA((n + 2,)), pltpu.SemaphoreType.DMA((14,)),
                        pltpu.SemaphoreType.DMA((14,))],
    )(*pbs, pack, gdw)


def _chip_sum(qs):
    n = len(qs)
    halves = 2

    def body(*refs):
        for k in range(n):
            q = refs[k]
            refs[n + k][...] = ((q[0].astype(F32) + q[1].astype(F32)) + q[2].astype(F32)) + q[3].astype(F32)

    in_specs = [pl.BlockSpec((4, q.shape[1] // halves, D), lambda h: (0, h, 0)) for q in qs]
    out_specs = [pl.BlockSpec((q.shape[1] // halves, D), lambda h: (h, 0)) for q in qs]
    return pl.pallas_call(
        body, name="grad_chip_sum", grid=(halves,),
        in_specs=in_specs, out_specs=out_specs,
        out_shape=[_sds((q.shape[1], D), F32) for q in qs],
        compiler_params=_params(48),
    )(*qs)


def _adamw_math(w, g, m, v):
    m = ADAM_B1 * m + (1.0 - ADAM_B1) * g
    v = ADAM_B2 * v + (1.0 - ADAM_B2) * (g * g)
    m_hat = m / (1.0 - ADAM_B1 ** ADAM_STEP)
    v_hat = v / (1.0 - ADAM_B2 ** ADAM_STEP)
    delta = -ADAM_LR * (m_hat / (jnp.sqrt(v_hat) + ADAM_EPS) + ADAM_WD * w)
    return delta, m, v


def _adamw(w, g, m, v, name):
    R, C = w.shape
    rc = 256 if R % 256 == 0 and R > 256 else R

    def body(w_ref, g_ref, m_ref, v_ref, d_ref, nm_ref, nv_ref):
        d, nm, nv = _adamw_math(w_ref[...], g_ref[...], m_ref[...], v_ref[...])
        d_ref[...] = d
        nm_ref[...] = nm
        nv_ref[...] = nv

    return pl.pallas_call(
        body, name=name, grid=(R // rc,),
        in_specs=[_rows(rc, C)] * 4, out_specs=[_rows(rc, C)] * 3,
        out_shape=[_sds((R, C), F32)] * 3,
        compiler_params=_params(40),
    )(w, g, m, v)


PACK = (("loss", LANES, 1), ("g_mix", D, D), ("b_in", INC, INC), ("b_dw", CC, CC), ("g_conv_ln", CC, CC),
        ("b_conv_ln", CC, CC), ("attn_sink", LANES, NQ), ("b_out", D, D), ("g_mem_q", D, D), ("g_mem_kv", D, D),
        ("g_ffn", D, D), ("g_final", D, D))
PACK_LANES = sum(p[1] for p in PACK)


def _small_update(packs, gdws, ws, ms, vs):
    names = [p[0] for p in PACK[1:]]
    n = len(names)

    def body(*refs):
        packs_ref, gdws_ref = refs[0], refs[1]
        w_refs, m_refs, v_refs = refs[2:2 + n], refs[2 + n:2 + 2 * n], refs[2 + 2 * n:2 + 3 * n]
        outs = refs[2 + 3 * n:]
        tot = packs_ref[0]
        gdw = gdws_ref[0]
        for s in range(1, NDEV):
            tot = tot + packs_ref[s]
            gdw = gdw + gdws_ref[s]
        outs[0][...] = tot[:, 0:LANES]
        outs[1][...] = gdw
        off = LANES
        for k, (_, lanes, used) in enumerate(PACK[1:]):
            g = tot[:, off:off + used]
            off += lanes
            d, nm, nv = _adamw_math(w_refs[k][...], g, m_refs[k][...], v_refs[k][...])
            o = outs[2 + 4 * k:6 + 4 * k]
            o[0][...] = g
            o[1][...] = d
            o[2][...] = nm
            o[3][...] = nv

    def full(a):
        nd = a.ndim
        return pl.BlockSpec(a.shape, lambda i: (0,) * nd)

    ins = [packs, gdws, *ws, *ms, *vs]
    out_shape = [_sds((1, LANES), F32), _sds(gdws.shape[1:], F32)]
    for w in ws:
        out_shape += [_sds(w.shape, F32)] * 4
    return pl.pallas_call(
        body, name="small_update", grid=(1,),
        in_specs=[full(a) for a in ins], out_specs=[full(o) for o in out_shape], out_shape=out_shape,
        compiler_params=_params(40),
    )(*ins)


def kernel(x, mem, g_mix, w_in, b_in, w_dw, b_dw, g_conv_ln, b_conv_ln, attn_sink, w_out, b_out, g_mem_q, g_mem_kv, w_mem_q, w_mem_kv, w_mem_o, g_ffn, w_gate, w_up, w_down, g_final, loss_target, m_g_mix, m_w_in, m_b_in, m_w_dw, m_b_dw, m_g_conv_ln, m_b_conv_ln, m_attn_sink, m_w_out, m_b_out, m_g_mem_q, m_g_mem_kv, m_w_mem_q, m_w_mem_kv, m_w_mem_o, m_g_ffn, m_w_gate, m_w_up, m_w_down, m_g_final, v_g_mix, v_w_in, v_b_in, v_w_dw, v_b_dw, v_g_conv_ln, v_b_conv_ln, v_attn_sink, v_w_out, v_b_out, v_g_mem_q, v_g_mem_kv, v_w_mem_q, v_w_mem_kv, v_w_mem_o, v_g_ffn, v_w_gate, v_w_up, v_w_down, v_g_final):
    given = dict(locals())
    bf = jnp.bfloat16
    xi, yi, ci = _place()
    me = 4 * xi + 2 * yi + ci

    transposed = {"w_inT": w_in, "w_mem_kvT": w_mem_kv, "w_gateT": w_gate, "w_upT": w_up}
    plain = {"w_out": w_out, "w_mem_q": w_mem_q, "w_mem_o": w_mem_o, "w_down": w_down}
    shards = [(transposed[n][0].T if n in transposed else plain[n][0]).astype(bf) for n in W_NAMES]
    dw_shard = jnp.pad(w_dw[0], ((0, 1), (0, 0)))
    gathered = _all_gather(shards + [dw_shard])
    wfull = {n: g.reshape(NDEV * r, D) for n, g, r in zip(W_NAMES, gathered[:-1], W_ROWS)}
    w_dw_full = jnp.transpose(gathered[-1], (1, 0, 2)).reshape(TAPS + 1, CC)[:TAPS]

    sp = {n: given[n] for n in ("g_mix", "b_in", "b_dw", "g_conv_ln", "b_conv_ln", "attn_sink", "b_out", "g_mem_q",
                                "g_mem_kv", "g_ffn")}
    sp["g_final"] = g_final.reshape(1, D)
    grad_x, small, wgrads, g_wdw = _local_step(x[0], mem[0], loss_target[0], sp, wfull, w_dw_full)

    gfs = [g[0].reshape(NDEV, r, D) for g, r in zip(wgrads, W_ROWS)]
    gbs = [g[1].reshape(NDEV, r, D) for g, r in zip(wgrads, W_ROWS)]
    ts = _sibling_exchange(gbs)
    pbs = _pair_sum(gfs, ts, ci.astype(jnp.int32).reshape(1))
    pack = jnp.concatenate(
        [jnp.pad(small[name], ((0, 0), (0, lanes - small[name].shape[1]))) for name, lanes, _ in PACK], axis=1)
    exchanged = _chip_exchange(pbs, pack, g_wdw)
    qs, packs, gdws = exchanged[:-2], exchanged[-2], exchanged[-1]
    gsh = dict(zip(W_NAMES, _chip_sum(qs)))

    grads, deltas, new_m, new_v = {}, {}, {}, {}
    names_small = [p[0] for p in PACK[1:]]

    def as_row(a):
        return a.reshape(1, -1)

    res = _small_update(packs, gdws, [as_row(given[n]) for n in names_small],
                        [as_row(given["m_" + n]) for n in names_small],
                        [as_row(given["v_" + n]) for n in names_small])
    loss = res[0][0, 0]
    gdw_sum = res[1]
    for k, n in enumerate(names_small):
        shape = given[n].shape
        g, d, nm, nv = res[2 + 4 * k:6 + 4 * k]
        grads[n], deltas[n], new_m[n], new_v[n] = (g.reshape(shape), d.reshape(shape), nm.reshape(shape),
                                                   nv.reshape(shape))

    big = {"w_in": gsh["w_inT"].T, "w_out": gsh["w_out"], "w_mem_q": gsh["w_mem_q"], "w_mem_kv": gsh["w_mem_kvT"].T,
           "w_mem_o": gsh["w_mem_o"], "w_gate": gsh["w_gateT"].T, "w_up": gsh["w_upT"].T, "w_down": gsh["w_down"],
           "w_dw": lax.dynamic_slice(gdw_sum, (0, me * (CC // NDEV)), (TAPS, CC // NDEV))}
    for n, g in big.items():
        d, nm, nv = _adamw(given[n][0], g, given["m_" + n][0], given["v_" + n][0], "adamw_" + n)
        grads[n], deltas[n], new_m[n], new_v[n] = g[None], d[None], nm[None], nv[None]

    order = ("g_mix", "w_in", "b_in", "w_dw", "b_dw", "g_conv_ln", "b_conv_ln", "attn_sink", "w_out", "b_out",
             "g_mem_q", "g_mem_kv", "w_mem_q", "w_mem_kv", "w_mem_o", "g_ffn", "w_gate", "w_up", "w_down", "g_final")
    return (loss, grad_x[None], *[grads[n] for n in order], *[deltas[n] for n in order],
            *[new_m[n] for n in order], *[new_v[n] for n in order])
```

```python
import functools

import jax
import jax.numpy as jnp
from jax import lax
from jax.experimental import pallas as pl
from jax.experimental.pallas import tpu as pltpu

D = 1024
CC = 512
HD = 64
NQ = 8
NKV = 2
GRP = NQ // NKV
ATT = NQ * HD
KVC = NKV * HD
BLK = 128
MH = 4
MHD = D // MH
F = 2816
FCH = F // 2
INC = 2 * CC + ATT + 2 * KVC
TAPS = 31
EPS = 1e-6
ROPE_THETA = 10000.0
SCALE = HD ** -0.5
MSCALE = MHD ** -0.5
LANES = 128
NDEV = 8

ADAM_LR = 0.001
ADAM_B1 = 0.9
ADAM_B2 = 0.999
ADAM_EPS = 1e-08
ADAM_WD = 0.01
ADAM_STEP = 10

F32 = jnp.float32
_MXU = jnp.bfloat16
MESH = pl.DeviceIdType.MESH

W_NAMES = ("w_inT", "w_out", "w_mem_q", "w_mem_kvT", "w_mem_o", "w_gateT", "w_upT", "w_down")
W_ROWS = (INC // NDEV, D // NDEV, D // NDEV, 2 * D // NDEV, D // NDEV, F // NDEV, F // NDEV, F // NDEV)


def _mm(a, b, ca, cb):
    return lax.dot_general(a.astype(_MXU), b.astype(_MXU), (((ca,), (cb,)), ((), ())),
                           preferred_element_type=F32)


def _nn(a, b):
    return _mm(a, b, 1, 0)


def _nt(a, b):
    return _mm(a, b, 1, 1)


def _tn(a, b):
    return _mm(a, b, 0, 0)


def _sigmoid(x):
    return 1.0 / (1.0 + jnp.exp(-x))


def _rms(x, g):
    r = lax.rsqrt(jnp.mean(x * x, axis=-1, keepdims=True) + EPS)
    return x * r * g, r


def _rms_bwd(dy, x, r, g):
    xh = x * r
    dg = jnp.sum(dy * xh, axis=0, keepdims=True)
    dxh = dy * g
    dx = r * (dxh - xh * jnp.mean(dxh * xh, axis=-1, keepdims=True))
    return dx, dg


def _rope(t, c, s):
    n = t.shape[1]
    reps = n // LANES
    if reps > 1:
        c = jnp.tile(c, (1, reps))
        s = jnp.tile(s, (1, reps))
    lane = lax.broadcasted_iota(jnp.int32, t.shape, 1)
    first = (lane & (HD - 1)) < (HD // 2)
    partner = jnp.where(first, pltpu.roll(t, n - HD // 2, 1), pltpu.roll(t, HD // 2, 1))
    return t * c + partner * s


def _rope_bwd(dt, c, s):
    n = dt.shape[1]
    reps = n // LANES
    if reps > 1:
        c = jnp.tile(c, (1, reps))
        s = jnp.tile(s, (1, reps))
    lane = lax.broadcasted_iota(jnp.int32, dt.shape, 1)
    first = (lane & (HD - 1)) < (HD // 2)
    ds = dt * s
    partner = jnp.where(first, pltpu.roll(ds, n - HD // 2, 1), pltpu.roll(ds, HD // 2, 1))
    return dt * c + partner


def _acc_init(i, *refs):
    @pl.when(i == 0)
    def _():
        for r in refs:
            r[...] = jnp.zeros(r.shape, r.dtype)


def _acc_row(ref, v):
    ref[...] += jnp.broadcast_to(v, ref.shape)


def _rows(tm, n):
    return pl.BlockSpec((tm, n), lambda i: (i, 0))


def _full(shape):
    nd = len(shape)
    return pl.BlockSpec(shape, lambda i: (0,) * nd)


def _once(shape):
    nd = len(shape)
    return pl.BlockSpec(shape, lambda i: (0,) * nd, pipeline_mode=pl.Buffered(1))


def _params(vmem_mb):
    return pltpu.CompilerParams(dimension_semantics=("arbitrary",), vmem_limit_bytes=vmem_mb * 2 ** 20)


def _sds(shape, dtype):
    return jax.ShapeDtypeStruct(shape, dtype)


def _mem_kv_fwd(mem, g_kv, wkvT):
    M = mem.shape[0]

    def body(mem_ref, g_ref, w_ref, km_ref, vm_ref, mn_ref):
        y, _ = _rms(mem_ref[...], g_ref[...])
        kv = _nt(y, w_ref[...])
        km_ref[...] = kv[:, :D].astype(km_ref.dtype)
        vm_ref[...] = kv[:, D:].astype(vm_ref.dtype)
        mn_ref[...] = y.astype(mn_ref.dtype)

    return pl.pallas_call(
        body, name="mem_kv_fwd", grid=(1,),
        in_specs=[_full((M, D)), _full((1, D)), _full((2 * D, D))],
        out_specs=[_full((M, D))] * 3,
        out_shape=[_sds((M, D), _MXU)] * 3,
        compiler_params=_params(40),
    )(mem, g_kv, wkvT)


def _in_proj_fwd(x, g_mix, winT, b_in, cosq, sinq, tm=512):
    S = x.shape[0]

    def body(x_ref, g_ref, w_ref, b_ref, c_ref, s_ref, h_ref, uglu_ref, vg_ref, q_ref, k_ref, v_ref):
        h, _ = _rms(x_ref[...], g_ref[...])
        hb = h.astype(h_ref.dtype)
        h_ref[...] = hb
        u = _nt(hb, w_ref[...]) + b_ref[...]
        uglu_ref[...] = u[:, :2 * CC]
        vg_ref[...] = u[:, :CC] * _sigmoid(u[:, CC:2 * CC])
        c = c_ref[...]
        s = s_ref[...]
        q_ref[...] = _rope(u[:, 2 * CC:2 * CC + ATT], c, s).astype(q_ref.dtype)
        k_ref[...] = _rope(u[:, 2 * CC + ATT:2 * CC + ATT + KVC], c, s).astype(k_ref.dtype)
        v_ref[...] = u[:, 2 * CC + ATT + KVC:].astype(v_ref.dtype)

    return pl.pallas_call(
        body, name="in_proj_fwd", grid=(S // tm,),
        in_specs=[_rows(tm, D), _full((1, D)), _full((INC, D)), _full((1, INC)), _rows(tm, LANES), _rows(tm, LANES)],
        out_specs=[_rows(tm, D), _rows(tm, 2 * CC), _rows(tm, CC), _rows(tm, ATT), _rows(tm, KVC), _rows(tm, KVC)],
        out_shape=[_sds((S, D), _MXU), _sds((S, 2 * CC), F32), _sds((S, CC), F32), _sds((S, ATT), _MXU),
                   _sds((S, KVC), _MXU), _sds((S, KVC), _MXU)],
        compiler_params=_params(48),
    )(x, g_mix, winT, b_in, cosq, sinq)


CONV_HALO = 16


def _dwconv_tile(pad_ref, t0, tm, w_ref, flip):
    cols = []
    for cl in range(CC // LANES):
        lanes = slice(cl * LANES, (cl + 1) * LANES)
        blk = pad_ref[pl.ds(t0, tm + 2 * CONV_HALO), lanes]
        acc = jnp.zeros((tm, LANES), F32)
        for b in range(8):
            sh = blk[b:b + tm + 24, :]
            for a in range(4):
                k = 8 * a + b - 1
                if 0 <= k < TAPS:
                    kk = TAPS - 1 - k if flip else k
                    acc = acc + sh[8 * a:8 * a + tm, :] * w_ref[kk:kk + 1, lanes]
        cols.append(acc)
    return jnp.concatenate(cols, axis=1)


def _fill_padded(i, src_hbm, pad_ref, sem, S):
    @pl.when(i == 0)
    def _():
        zeros = jnp.zeros((CONV_HALO, pad_ref.shape[1]), pad_ref.dtype)
        pad_ref[0:CONV_HALO, :] = zeros
        pad_ref[CONV_HALO + S:2 * CONV_HALO + S, :] = zeros
        cp = pltpu.make_async_copy(src_hbm, pad_ref.at[pl.ds(CONV_HALO, S), :], sem)
        cp.start()
        cp.wait()


def _ln_silu(y, g, b):
    mu = jnp.mean(y, axis=-1, keepdims=True)
    yc = y - mu
    var = jnp.mean(yc * yc, axis=-1, keepdims=True)
    rstd = lax.rsqrt(var + EPS)
    yh = yc * rstd
    z = yh * g + b
    return yh, rstd, z


def _conv_fwd(vg, w_dw, b_dw, ln_g, ln_b, tm=128):
    S = vg.shape[0]

    def body(vg_hbm, w_ref, b_ref, g_ref, bb_ref, y_ref, yc_ref, pad_ref, sem):
        i = pl.program_id(0)
        _fill_padded(i, vg_hbm, pad_ref, sem, S)
        t0 = pl.multiple_of(i * tm, tm)
        y = _dwconv_tile(pad_ref, t0, tm, w_ref, False) + b_ref[...]
        y_ref[...] = y
        _, _, z = _ln_silu(y, g_ref[...], bb_ref[...])
        yc_ref[...] = (z * _sigmoid(z)).astype(yc_ref.dtype)

    return pl.pallas_call(
        body, name="conv_fwd", grid=(S // tm,),
        in_specs=[pl.BlockSpec(memory_space=pl.ANY), _full((TAPS, CC)), _full((1, CC)), _full((1, CC)), _full((1, CC))],
        out_specs=[_rows(tm, CC), _rows(tm, CC)],
        out_shape=[_sds((S, CC), F32), _sds((S, CC), _MXU)],
        scratch_shapes=[pltpu.VMEM((S + 2 * CONV_HALO, CC), F32), pltpu.SemaphoreType.DMA],
        compiler_params=_params(40),
    )(vg, w_dw, b_dw, ln_g, ln_b)


def _attn_window(i, S):
    start = jnp.clip((i - 1) * BLK, 0, S - 3 * BLK)
    return pl.multiple_of(start, BLK)


def _attn_mask(i, start):
    rows = lax.broadcasted_iota(jnp.int32, (GRP * BLK, 3 * BLK), 0)
    cols = lax.broadcasted_iota(jnp.int32, (GRP * BLK, 3 * BLK), 1)
    rel = (start + cols) - (i * BLK + (rows & (BLK - 1)))
    return jnp.abs(rel) <= BLK


def _stack_heads(t, hk):
    return jnp.concatenate([t[:, (hk * GRP + g) * HD:(hk * GRP + g + 1) * HD] for g in range(GRP)], axis=0)


def _attn_probs(qs, kh, sk_ref, hk, mask):
    s = _nt(qs, kh) * SCALE
    s = jnp.where(mask, s, -1e30)
    sk = jnp.concatenate(
        [jnp.broadcast_to(sk_ref[0:1, hk * GRP + g:hk * GRP + g + 1], (BLK, 1)) for g in range(GRP)], axis=0)
    m = jnp.maximum(jnp.max(s, axis=-1, keepdims=True), sk)
    e = jnp.exp(s - m)
    es = jnp.exp(sk - m)
    den = jnp.sum(e, axis=-1, keepdims=True) + es
    return e / den, es / den


def _attn_fwd(qr, kr, vv, sink):
    S = qr.shape[0]

    def body(q_ref, k_ref, v_ref, sk_ref, o_ref):
        i = pl.program_id(0)
        start = _attn_window(i, S)
        kb = k_ref[pl.ds(start, 3 * BLK), :]
        vb = v_ref[pl.ds(start, 3 * BLK), :]
        q = q_ref[...]
        mask = _attn_mask(i, start)
        outs = []
        for hk in range(NKV):
            p, _ = _attn_probs(_stack_heads(q, hk), kb[:, hk * HD:(hk + 1) * HD], sk_ref, hk, mask)
            o = _nn(p, vb[:, hk * HD:(hk + 1) * HD])
            outs += [o[g * BLK:(g + 1) * BLK, :] for g in range(GRP)]
        o_ref[...] = jnp.concatenate(outs, axis=1).astype(o_ref.dtype)

    return pl.pallas_call(
        body, name="attn_fwd", grid=(S // BLK,),
        in_specs=[_rows(BLK, ATT), _full((S, KVC)), _full((S, KVC)), _full((1, NQ))],
        out_specs=_rows(BLK, ATT),
        out_shape=_sds((S, ATT), _MXU),
        compiler_params=_params(40),
    )(qr, kr, vv, sink)


def _out_proj_fwd(x, yc, ya, wout, b_out, tm=512):
    S = x.shape[0]

    def body(x_ref, yc_ref, ya_ref, w_ref, b_ref, x1_ref):
        x1_ref[...] = (x_ref[...] + _nn(yc_ref[...], w_ref[0:CC, :]) + _nn(ya_ref[...], w_ref[CC:2 * CC, :])
                       + b_ref[...])

    return pl.pallas_call(
        body, name="out_proj_fwd", grid=(S // tm,),
        in_specs=[_rows(tm, D), _rows(tm, CC), _rows(tm, ATT), _full((D, D)), _full((1, D))],
        out_specs=_rows(tm, D),
        out_shape=_sds((S, D), F32),
        compiler_params=_params(40),
    )(x, yc, ya, wout, b_out)


def _softmax(s):
    m = jnp.max(s, axis=-1, keepdims=True)
    e = jnp.exp(s - m)
    return e / jnp.sum(e, axis=-1, keepdims=True)


def _mem_attn_fwd(x1, g_q, wq, km, vm, wo, tm=512):
    S = x1.shape[0]
    M = km.shape[0]

    def body(x1_ref, g_ref, wq_ref, km_ref, vm_ref, wo_ref, x2_ref, qm_ref, om_ref):
        x1v = x1_ref[...]
        hq, _ = _rms(x1v, g_ref[...])
        qm = _nn(hq, wq_ref[...]).astype(qm_ref.dtype)
        qm_ref[...] = qm
        oms = []
        for h in range(MH):
            sl = slice(h * MHD, (h + 1) * MHD)
            p = _softmax(_nt(qm[:, sl], km_ref[:, sl]) * MSCALE)
            oms.append(_nn(p, vm_ref[:, sl]))
        om = jnp.concatenate(oms, axis=1).astype(om_ref.dtype)
        om_ref[...] = om
        x2_ref[...] = x1v + _nn(om, wo_ref[...])

    return pl.pallas_call(
        body, name="mem_attn_fwd", grid=(S // tm,),
        in_specs=[_rows(tm, D), _full((1, D)), _full((D, D)), _full((M, D)), _full((M, D)), _full((D, D))],
        out_specs=[_rows(tm, D)] * 3,
        out_shape=[_sds((S, D), F32), _sds((S, D), _MXU), _sds((S, D), _MXU)],
        compiler_params=_params(48),
    )(x1, g_q, wq, km, vm, wo)


def _ffn_loss_fwd(x2, tgt, g_ffn, g_final, wgT, wuT, wd, tm=256):
    S = x2.shape[0]

    def body(x2_ref, t_ref, gf_ref, gl_ref, wg_ref, wu_ref, wd_ref,
             gt_ref, up_ref, hf_ref, dx3_ref, dx3b_ref, loss_ref, dgl_ref):
        i = pl.program_id(0)
        _acc_init(i, loss_ref, dgl_ref)
        x2v = x2_ref[...]
        hf, _ = _rms(x2v, gf_ref[...])
        hb = hf.astype(hf_ref.dtype)
        hf_ref[...] = hb
        x3 = x2v
        for c in range(F // FCH):
            sl = slice(c * FCH, (c + 1) * FCH)
            gt = _nt(hb, wg_ref[sl, :])
            up = _nt(hb, wu_ref[sl, :])
            gt_ref[:, sl] = gt.astype(gt_ref.dtype)
            up_ref[:, sl] = up.astype(up_ref.dtype)
            x3 = x3 + _nn(gt * _sigmoid(gt) * up, wd_ref[sl, :])
        gl = gl_ref[...]
        y, r = _rms(x3, gl)
        diff = y - t_ref[...]
        _acc_row(loss_ref, jnp.sum(jnp.sum(diff * diff, axis=-1, keepdims=True), axis=0, keepdims=True) * (0.5 / D))
        dx3, dgl = _rms_bwd(diff * (1.0 / D), x3, r, gl)
        dx3_ref[...] = dx3
        dx3b_ref[...] = dx3.astype(dx3b_ref.dtype)
        _acc_row(dgl_ref, dgl)

    return pl.pallas_call(
        body, name="ffn_loss_fwd", grid=(S // tm,),
        in_specs=[_rows(tm, D), _rows(tm, D), _full((1, D)), _full((1, D)),
                  _once((F, D)), _once((F, D)), _once((F, D))],
        out_specs=[_rows(tm, F), _rows(tm, F), _rows(tm, D), _rows(tm, D), _rows(tm, D),
                   _full((8, LANES)), _full((8, D))],
        out_shape=[_sds((S, F), _MXU), _sds((S, F), _MXU), _sds((S, D), _MXU), _sds((S, D), F32),
                   _sds((S, D), _MXU), _sds((8, LANES), F32), _sds((8, D), F32)],
        compiler_params=_params(56),
    )(x2, tgt, g_ffn, g_final, wgT, wuT, wd)


def _ffn_bwd(dx3, x2, gt, up, g_ffn, wd, wgT, wuT, tm=256):
    S = x2.shape[0]

    def body(dx3_ref, x2_ref, gt_ref, up_ref, g_ref, wd_ref, wg_ref, wu_ref,
             dx2_ref, dx2b_ref, dgt_ref, dup_ref, act_ref, dg_ref):
        i = pl.program_id(0)
        _acc_init(i, dg_ref)
        dx3v = dx3_ref[...]
        db = dx3v.astype(_MXU)
        dhf = jnp.zeros((tm, D), F32)
        for c in range(F // FCH):
            sl = slice(c * FCH, (c + 1) * FCH)
            gtv = gt_ref[:, sl].astype(F32)
            upv = up_ref[:, sl].astype(F32)
            dact = _nt(db, wd_ref[sl, :])
            sig = _sigmoid(gtv)
            silu = gtv * sig
            dup = (dact * silu).astype(dup_ref.dtype)
            dgt = (dact * upv * (sig * (1.0 + gtv * (1.0 - sig)))).astype(dgt_ref.dtype)
            act_ref[:, sl] = (silu * upv).astype(act_ref.dtype)
            dgt_ref[:, sl] = dgt
            dup_ref[:, sl] = dup
            dhf = dhf + _nn(dgt, wg_ref[sl, :]) + _nn(dup, wu_ref[sl, :])
        x2v = x2_ref[...]
        g = g_ref[...]
        _, r = _rms(x2v, g)
        dxn, dg = _rms_bwd(dhf, x2v, r, g)
        dx2 = dx3v + dxn
        dx2_ref[...] = dx2
        dx2b_ref[...] = dx2.astype(dx2b_ref.dtype)
        _acc_row(dg_ref, dg)

    return pl.pallas_call(
        body, name="ffn_bwd", grid=(S // tm,),
        in_specs=[_rows(tm, D), _rows(tm, D), _rows(tm, F), _rows(tm, F), _full((1, D)),
                  _once((F, D)), _once((F, D)), _once((F, D))],
        out_specs=[_rows(tm, D), _rows(tm, D), _rows(tm, F), _rows(tm, F), _rows(tm, F), _full((8, D))],
        out_shape=[_sds((S, D), F32), _sds((S, D), _MXU), _sds((S, F), _MXU), _sds((S, F), _MXU),
                   _sds((S, F), _MXU), _sds((8, D), F32)],
        compiler_params=_params(56),
    )(dx3, x2, gt, up, g_ffn, wd, wgT, wuT)


def _wgrad(a, b, name, rc=256):
    S, R = a.shape
    N = b.shape[1]

    def body(a_ref, b_ref, o_ref, ob_ref):
        g = _tn(a_ref[...], b_ref[...])
        o_ref[...] = g
        ob_ref[...] = g.astype(ob_ref.dtype)

    return pl.pallas_call(
        body, name=name, grid=(R // rc,),
        in_specs=[pl.BlockSpec((S, rc), lambda j: (0, j)), _once((S, N))],
        out_specs=[_rows(rc, N), _rows(rc, N)],
        out_shape=[_sds((R, N), F32), _sds((R, N), jnp.bfloat16)],
        compiler_params=_params(48),
    )(a, b)


def _mem_attn_bwd(dx2, x1, qm, km, vm, g_q, wq, wo, wout, tm=512):
    S = x1.shape[0]
    M = km.shape[0]

    def body(dx2_ref, x1_ref, qm_ref, km_ref, vm_ref, g_ref, wq_ref, wo_ref, wout_ref,
             dx1_ref, dx1b_ref, hq_ref, dqm_ref, dyc_ref, dya_ref, dkm_ref, dvm_ref, dg_ref, dbo_ref):
        i = pl.program_id(0)
        _acc_init(i, dkm_ref, dvm_ref, dg_ref, dbo_ref)
        dx2v = dx2_ref[...]
        dom = _nt(dx2v, wo_ref[...]).astype(_MXU)
        dqs = []
        for h in range(MH):
            sl = slice(h * MHD, (h + 1) * MHD)
            qh = qm_ref[:, sl]
            kh = km_ref[:, sl]
            p = _softmax(_nt(qh, kh) * MSCALE)
            dp = _nt(dom[:, sl], vm_ref[:, sl])
            ds = (p * (dp - jnp.sum(dp * p, axis=-1, keepdims=True))).astype(_MXU)
            dqs.append(_nn(ds, kh) * MSCALE)
            dkm_ref[:, sl] += _tn(ds, qh) * MSCALE
            dvm_ref[:, sl] += _tn(p, dom[:, sl])
        dqm = jnp.concatenate(dqs, axis=1).astype(dqm_ref.dtype)
        dqm_ref[...] = dqm
        dhq = _nt(dqm, wq_ref[...])
        x1v = x1_ref[...]
        g = g_ref[...]
        hq, r = _rms(x1v, g)
        hq_ref[...] = hq.astype(hq_ref.dtype)
        dxn, dg = _rms_bwd(dhq, x1v, r, g)
        dx1 = dx2v + dxn
        dx1_ref[...] = dx1
        dx1b = dx1.astype(dx1b_ref.dtype)
        dx1b_ref[...] = dx1b
        dym = _nt(dx1b, wout_ref[...])
        dyc_ref[...] = dym[:, :CC]
        dya_ref[...] = dym[:, CC:].astype(dya_ref.dtype)
        _acc_row(dg_ref, dg)
        _acc_row(dbo_ref, jnp.sum(dx1, axis=0, keepdims=True))

    return pl.pallas_call(
        body, name="mem_attn_bwd", grid=(S // tm,),
        in_specs=[_rows(tm, D), _rows(tm, D), _rows(tm, D), _full((M, D)), _full((M, D)), _full((1, D)),
                  _once((D, D)), _once((D, D)), _once((D, D))],
        out_specs=[_rows(tm, D), _rows(tm, D), _rows(tm, D), _rows(tm, D), _rows(tm, CC), _rows(tm, ATT),
                   _full((M, D)), _full((M, D)), _full((8, D)), _full((8, D))],
        out_shape=[_sds((S, D), F32), _sds((S, D), _MXU), _sds((S, D), _MXU), _sds((S, D), _MXU),
                   _sds((S, CC), F32), _sds((S, ATT), _MXU),
                   _sds((M, D), F32), _sds((M, D), F32), _sds((8, D), F32), _sds((8, D), F32)],
        compiler_params=_params(56),
    )(dx2, x1, qm, km, vm, g_q, wq, wo, wout)


def _mem_kv_bwd(dkm, dvm, mem, g_kv, memn, wkvT):
    M = mem.shape[0]

    def body(dkm_ref, dvm_ref, mem_ref, g_ref, mn_ref, w_ref, gw_ref, gwb_ref, dg_ref):
        dkv = jnp.concatenate([dkm_ref[...], dvm_ref[...]], axis=1).astype(_MXU)
        gw = _tn(dkv, mn_ref[...])
        gw_ref[...] = gw
        gwb_ref[...] = gw.astype(gwb_ref.dtype)
        dmn = _nn(dkv, w_ref[...])
        memv = mem_ref[...]
        g = g_ref[...]
        _, r = _rms(memv, g)
        _, dg = _rms_bwd(dmn, memv, r, g)
        dg_ref[...] = jnp.broadcast_to(dg, dg_ref.shape)

    return pl.pallas_call(
        body, name="mem_kv_bwd", grid=(1,),
        in_specs=[_full((M, D)), _full((M, D)), _full((M, D)), _full((1, D)), _full((M, D)), _full((2 * D, D))],
        out_specs=[_full((2 * D, D)), _full((2 * D, D)), _full((8, D))],
        out_shape=[_sds((2 * D, D), F32), _sds((2 * D, D), jnp.bfloat16), _sds((8, D), F32)],
        compiler_params=_params(48),
    )(dkm, dvm, mem, g_kv, memn, wkvT)


def _attn_bwd(qr, kr, vv, sink, dya):
    S = qr.shape[0]

    def body(q_ref, k_ref, v_ref, sk_ref, do_ref, dq_ref, dk_ref, dv_ref, dsk_ref):
        i = pl.program_id(0)
        _acc_init(i, dk_ref, dv_ref, dsk_ref)
        start = _attn_window(i, S)
        kb = k_ref[pl.ds(start, 3 * BLK), :]
        vb = v_ref[pl.ds(start, 3 * BLK), :]
        q = q_ref[...]
        do = do_ref[...]
        mask = _attn_mask(i, start)
        dqs, dks, dvs = [], [], []
        for hk in range(NKV):
            hs = slice(hk * HD, (hk + 1) * HD)
            qs = _stack_heads(q, hk)
            dos = _stack_heads(do, hk)
            p, ps = _attn_probs(qs, kb[:, hs], sk_ref, hk, mask)
            dp = _nt(dos, vb[:, hs])
            delta = jnp.sum(dp * p, axis=-1, keepdims=True)
            ds = (p * (dp - delta)).astype(_MXU)
            dq = _nn(ds, kb[:, hs]) * SCALE
            dqs += [dq[g * BLK:(g + 1) * BLK, :] for g in range(GRP)]
            dks.append(_tn(ds, qs) * SCALE)
            dvs.append(_tn(p, dos))
            dsk = ps * delta
            for g in range(GRP):
                h = hk * GRP + g
                tot = jnp.sum(dsk[g * BLK:(g + 1) * BLK, :], axis=0, keepdims=True)
                dsk_ref[h:h + 1, :] -= jnp.broadcast_to(tot, (1, LANES))
        dq_ref[...] = jnp.concatenate(dqs, axis=1)
        dk_ref[pl.ds(start, 3 * BLK), :] += jnp.concatenate(dks, axis=1)
        dv_ref[pl.ds(start, 3 * BLK), :] += jnp.concatenate(dvs, axis=1)

    return pl.pallas_call(
        body, name="attn_bwd", grid=(S // BLK,),
        in_specs=[_rows(BLK, ATT), _full((S, KVC)), _full((S, KVC)), _full((1, NQ)), _rows(BLK, ATT)],
        out_specs=[_rows(BLK, ATT), _full((S, KVC)), _full((S, KVC)), _full((8, LANES))],
        out_shape=[_sds((S, ATT), F32), _sds((S, KVC), F32), _sds((S, KVC), F32), _sds((8, LANES), F32)],
        compiler_params=_params(40),
    )(qr, kr, vv, sink, dya)


def _conv_norm_bwd(dyc, y, ln_g, ln_b, tm=512):
    S = y.shape[0]

    def body(dyc_ref, y_ref, g_ref, b_ref, dy_ref, dg_ref, db_ref, dbdw_ref):
        i = pl.program_id(0)
        _acc_init(i, dg_ref, db_ref, dbdw_ref)
        g = g_ref[...]
        yh, rstd, z = _ln_silu(y_ref[...], g, b_ref[...])
        sig = _sigmoid(z)
        dz = dyc_ref[...] * (sig * (1.0 + z * (1.0 - sig)))
        dyh = dz * g
        dy = rstd * (dyh - jnp.mean(dyh, axis=-1, keepdims=True)
                     - yh * jnp.mean(dyh * yh, axis=-1, keepdims=True))
        dy_ref[...] = dy
        _acc_row(dg_ref, jnp.sum(dz * yh, axis=0, keepdims=True))
        _acc_row(db_ref, jnp.sum(dz, axis=0, keepdims=True))
        _acc_row(dbdw_ref, jnp.sum(dy, axis=0, keepdims=True))

    return pl.pallas_call(
        body, name="conv_norm_bwd", grid=(S // tm,),
        in_specs=[_rows(tm, CC), _rows(tm, CC), _full((1, CC)), _full((1, CC))],
        out_specs=[_rows(tm, CC), _full((8, CC)), _full((8, CC)), _full((8, CC))],
        out_shape=[_sds((S, CC), F32), _sds((8, CC), F32), _sds((8, CC), F32), _sds((8, CC), F32)],
        compiler_params=_params(40),
    )(dyc, y, ln_g, ln_b)


def _conv_bwd(dy, vg, uglu, w_dw, tm=128):
    S = dy.shape[0]

    def body(dy_hbm, vg_hbm, uglu_ref, w_ref, du_ref, gw_ref, dbin_ref, dyp_ref, vgp_ref, gacc_ref, sems):
        i = pl.program_id(0)
        _acc_init(i, gacc_ref, dbin_ref)
        _fill_padded(i, dy_hbm, dyp_ref, sems.at[0], S)
        _fill_padded(i, vg_hbm, vgp_ref, sems.at[1], S)
        t0 = pl.multiple_of(i * tm, tm)
        dvg = _dwconv_tile(dyp_ref, t0, tm, w_ref, True)
        for cl in range(CC // LANES):
            lanes = slice(cl * LANES, (cl + 1) * LANES)
            dyt = dyp_ref[pl.ds(t0 + CONV_HALO, tm), lanes]
            blk = vgp_ref[pl.ds(t0, tm + 2 * CONV_HALO), lanes]
            for b in range(8):
                sh = blk[b:b + tm + 24, :]
                for a in range(4):
                    k = 8 * a + b - 1
                    if 0 <= k < TAPS:
                        prod = dyt * sh[8 * a:8 * a + tm, :]
                        gacc_ref[k, :, lanes] += jnp.sum(prod.reshape(tm // 8, 8, LANES), axis=0)
        a_v = uglu_ref[:, :CC]
        sg = _sigmoid(uglu_ref[:, CC:])
        da = dvg * sg
        dgate = dvg * a_v * sg * (1.0 - sg)
        du = jnp.concatenate([da, dgate], axis=1)
        du_ref[...] = du.astype(du_ref.dtype)
        _acc_row(dbin_ref, jnp.sum(du, axis=0, keepdims=True))

        @pl.when(i == pl.num_programs(0) - 1)
        def _():
            gw_ref[...] = jnp.zeros(gw_ref.shape, F32)
            for k in range(TAPS):
                gw_ref[k:k + 1, :] = jnp.sum(gacc_ref[k], axis=0, keepdims=True)

    return pl.pallas_call(
        body, name="conv_bwd", grid=(S // tm,),
        in_specs=[pl.BlockSpec(memory_space=pl.ANY), pl.BlockSpec(memory_space=pl.ANY), _rows(tm, 2 * CC),
                  _full((TAPS, CC))],
        out_specs=[_rows(tm, 2 * CC), _full((TAPS + 1, CC)), _full((8, 2 * CC))],
        out_shape=[_sds((S, 2 * CC), _MXU), _sds((TAPS + 1, CC), F32), _sds((8, 2 * CC), F32)],
        scratch_shapes=[pltpu.VMEM((S + 2 * CONV_HALO, CC), F32), pltpu.VMEM((S + 2 * CONV_HALO, CC), F32),
                        pltpu.VMEM((TAPS, 8, CC), F32), pltpu.SemaphoreType.DMA((2,))],
        compiler_params=_params(48),
    )(dy, vg, uglu, w_dw)


def _in_proj_bwd(duglu, dqr, dkr, dv, cosq, sinq, x, dx1, g_mix, winT, tm=512):
    S = x.shape[0]
    NA = ATT + 2 * KVC

    def body(dug_ref, dq_ref, dk_ref, dv_ref, c_ref, s_ref, x_ref, dx1_ref, g_ref, w_ref,
             dx_ref, du_ref, dg_ref, dbin_ref):
        i = pl.program_id(0)
        _acc_init(i, dg_ref, dbin_ref)
        c = c_ref[...]
        s = s_ref[...]
        datt = jnp.concatenate([_rope_bwd(dq_ref[...], c, s), _rope_bwd(dk_ref[...], c, s), dv_ref[...]], axis=1)
        _acc_row(dbin_ref, jnp.sum(datt, axis=0, keepdims=True))
        du = jnp.concatenate([dug_ref[...], datt.astype(du_ref.dtype)], axis=1)
        du_ref[...] = du
        dh = _nn(du, w_ref[...])
        xv = x_ref[...]
        g = g_ref[...]
        _, r = _rms(xv, g)
        dxn, dg = _rms_bwd(dh, xv, r, g)
        dx_ref[...] = dx1_ref[...] + dxn
        _acc_row(dg_ref, dg)

    return pl.pallas_call(
        body, name="in_proj_bwd", grid=(S // tm,),
        in_specs=[_rows(tm, 2 * CC), _rows(tm, ATT), _rows(tm, KVC), _rows(tm, KVC), _rows(tm, LANES),
                  _rows(tm, LANES), _rows(tm, D), _rows(tm, D), _full((1, D)), _once((INC, D))],
        out_specs=[_rows(tm, D), _rows(tm, INC), _full((8, D)), _full((8, NA))],
        out_shape=[_sds((S, D), F32), _sds((S, INC), _MXU), _sds((8, D), F32), _sds((8, NA), F32)],
        compiler_params=_params(48),
    )(duglu, dqr, dkr, dv, cosq, sinq, x, dx1, g_mix, winT)


def _rope_tables(S):
    pos = jnp.arange(S, dtype=F32)
    inv_freq = ROPE_THETA ** (-jnp.arange(0, HD, 2, dtype=F32) / HD)
    ang = pos[:, None] * inv_freq[None, :]
    cos, sin = jnp.cos(ang), jnp.sin(ang)
    cosq = jnp.tile(jnp.concatenate([cos, cos], axis=1), (1, LANES // HD))
    sinq = jnp.tile(jnp.concatenate([-sin, sin], axis=1), (1, LANES // HD))
    return cosq, sinq


GATHER_GROUPS = (("w_inT", "w_dw"), ("w_out", "w_mem_kvT", "w_mem_q", "w_mem_o"), ("w_gateT", "w_upT", "w_down"))


def _local_step(x, mem, tgt, sp, gather):
    S = x.shape[0]
    cosq, sinq = _rope_tables(S)

    gather.forward(0, x)
    w0 = gather.finish(0, x)
    winT, w_dw_full = w0["w_inT"], w0["w_dw"]
    h, uglu, vg, qr, kr, vv = _in_proj_fwd(x, sp["g_mix"], winT, sp["b_in"], cosq, sinq)
    y, yc = _conv_fwd(vg, w_dw_full, sp["b_dw"], sp["g_conv_ln"], sp["b_conv_ln"])
    gather.forward(1, yc)
    ya = _attn_fwd(qr, kr, vv, sp["attn_sink"])
    w1 = gather.finish(1, ya)
    gather.forward(2, ya)
    wout, wkvT, wq, wo = w1["w_out"], w1["w_mem_kvT"], w1["w_mem_q"], w1["w_mem_o"]
    km, vm, memn = _mem_kv_fwd(mem, sp["g_mem_kv"], wkvT)
    x1 = _out_proj_fwd(x, yc, ya, wout, sp["b_out"])
    x2, qm, om = _mem_attn_fwd(x1, sp["g_mem_q"], wq, km, vm, wo)
    w2 = gather.finish(2, om)
    wgT, wuT, wd = w2["w_gateT"], w2["w_upT"], w2["w_down"]
    gt, up, hf, dx3, dx3b, loss_acc, dg_final = _ffn_loss_fwd(x2, tgt, sp["g_ffn"], sp["g_final"], wgT, wuT, wd)

    dx2, dx2b, dgt, dup, act, dg_ffn = _ffn_bwd(dx3, x2, gt, up, sp["g_ffn"], wd, wgT, wuT)
    g_wgT = _wgrad(dgt, hf, "wgrad_gate")
    g_wuT = _wgrad(dup, hf, "wgrad_up")
    g_wd = _wgrad(act, dx3b, "wgrad_down")

    dx1, dx1b, hq, dqm, dyc, dya, dkm, dvm, dg_mem_q, db_out = _mem_attn_bwd(
        dx2, x1, qm, km, vm, sp["g_mem_q"], wq, wo, wout)
    g_wo = _wgrad(om, dx2b, "wgrad_mem_o")
    g_wq = _wgrad(hq, dqm, "wgrad_mem_q")
    g_wkvT_f, g_wkvT_b, dg_mem_kv = _mem_kv_bwd(dkm, dvm, mem, sp["g_mem_kv"], memn, wkvT)

    g_wout_c = _wgrad(yc, dx1b, "wgrad_out_conv")
    g_wout_a = _wgrad(ya, dx1b, "wgrad_out_attn")
    g_wout = tuple(jnp.concatenate([c, a], axis=0) for c, a in zip(g_wout_c, g_wout_a))

    dqr, dkr, dvv, dsink = _attn_bwd(qr, kr, vv, sp["attn_sink"], dya)
    dy, dg_cln, db_cln, db_dw = _conv_norm_bwd(dyc, y, sp["g_conv_ln"], sp["b_conv_ln"])
    duglu, g_wdw, dbin_glu = _conv_bwd(dy, vg, uglu, w_dw_full)
    grad_x, du, dg_mix, dbin_att = _in_proj_bwd(duglu, dqr, dkr, dvv, cosq, sinq, x, dx1, sp["g_mix"], winT)
    g_winT = _wgrad(du, h, "wgrad_in")

    small = {
        "loss": loss_acc[0:1, :],
        "g_mix": dg_mix[0:1], "b_in": jnp.concatenate([dbin_glu[0:1], dbin_att[0:1]], axis=1),
        "b_dw": db_dw[0:1], "g_conv_ln": dg_cln[0:1], "b_conv_ln": db_cln[0:1],
        "attn_sink": dsink[:, 0].reshape(1, NQ), "b_out": db_out[0:1], "g_mem_q": dg_mem_q[0:1],
        "g_mem_kv": dg_mem_kv[0:1], "g_ffn": dg_ffn[0:1], "g_final": dg_final[0:1],
    }
    wgrads = [g_winT, g_wout, g_wq, (g_wkvT_f, g_wkvT_b), g_wo, g_wgT, g_wuT, g_wd]
    return grad_x, small, wgrads, g_wdw


def _place():
    return lax.axis_index("x"), lax.axis_index("y"), lax.axis_index("c")


def _hbm_specs(n):
    return [pl.BlockSpec(memory_space=pl.ANY)] * n


_HBM = pl.BlockSpec(memory_space=pltpu.HBM)
_SEM = pl.BlockSpec(memory_space=pltpu.SEMAPHORE)
_ANY = pl.BlockSpec(memory_space=pl.ANY)
_EFFECT = pltpu.SideEffectType.DATAFLOW_SIDE_EFFECTING


def _dev_index(p):
    return 4 * p[0] + 2 * p[1] + p[2]


def _gather_start(groups):
    sizes = [len(g) for g in groups]
    flat = [z for g in groups for z in g]
    n = len(flat)

    def body(*refs):
        ins = refs[:n]
        sems = refs[2 * n:]
        x, y, c = _place()
        me = _dev_index((x, y, c))
        peers = [(x, y, 1 - c), (1 - x, y, c), (x, 1 - y, c), (1 - x, 1 - y, c)]
        base = 0
        for gi, size in enumerate(sizes):
            send_sems, recv_sems = sems[2 * gi], sems[2 * gi + 1]
            for k in range(size):
                zone = ins[base + k]
                for j, peer in enumerate(peers):
                    pltpu.make_async_remote_copy(
                        src_ref=zone.at[me], dst_ref=zone.at[me], send_sem=send_sems.at[4 * k + j],
                        recv_sem=recv_sems.at[4 * k + j], device_id=peer, device_id_type=MESH).start()
            base += size

    out_shape = [pltpu.HBM(z.shape, z.dtype) for z in flat]
    out_specs = [_HBM] * n
    for size in sizes:
        out_shape += [pltpu.SemaphoreType.DMA((4 * size,))] * 2
        out_specs += [_SEM, _SEM]
    res = pl.pallas_call(
        body, name="gather_start", in_specs=[_HBM] * n, out_specs=out_specs, out_shape=out_shape,
        input_output_aliases={k: k for k in range(n)},
        compiler_params=pltpu.CompilerParams(has_side_effects=_EFFECT),
    )(*[pltpu.with_memory_space_constraint(z, pltpu.HBM) for z in flat])
    out, base = [], 0
    for gi, size in enumerate(sizes):
        out.append((list(res[base:base + size]), res[n + 2 * gi], res[n + 2 * gi + 1]))
        base += size
    return out


def _gather_forward(zones, send_sems, recv_sems, after, collective_id, name):
    n = len(zones)

    def body(*refs):
        ins = refs[:n]
        send_ref, recv_ref = refs[n], refs[n + 1]
        fsend, frecv = refs[2 * n + 3], refs[2 * n + 4]
        x, y, c = _place()
        sibling = (x, y, 1 - c)
        barrier = pltpu.get_barrier_semaphore()
        pl.semaphore_signal(barrier, inc=1, device_id=sibling, device_id_type=MESH)
        pl.semaphore_wait(barrier, 1)
        chips = [(1 - x, y), (x, 1 - y), (1 - x, 1 - y)]
        for j, chip in enumerate(chips):
            for k in range(n):
                slot = ins[k].at[_dev_index((*chip, c))]
                pltpu.make_async_remote_copy(
                    src_ref=slot, dst_ref=slot, send_sem=send_ref.at[4 * k + 1 + j],
                    recv_sem=recv_ref.at[4 * k + 1 + j], device_id=sibling, device_id_type=MESH).wait_recv()
                pltpu.make_async_remote_copy(
                    src_ref=slot, dst_ref=slot, send_sem=fsend.at[3 * k + j], recv_sem=frecv.at[3 * k + j],
                    device_id=sibling, device_id_type=MESH).start()
        for k in range(n):
            own = ins[k].at[_dev_index((x, y, c))]
            theirs = ins[k].at[_dev_index(sibling)]
            pltpu.make_async_remote_copy(
                src_ref=theirs, dst_ref=theirs, send_sem=send_ref.at[4 * k], recv_sem=recv_ref.at[4 * k],
                device_id=sibling, device_id_type=MESH).wait_recv()
            for j in range(4):
                pltpu.make_async_remote_copy(
                    src_ref=own, dst_ref=own, send_sem=send_ref.at[4 * k + j], recv_sem=recv_ref.at[4 * k + j],
                    device_id=sibling, device_id_type=MESH).wait_send()

    res = pl.pallas_call(
        body, name=name, in_specs=[_HBM] * n + [_SEM, _SEM, _ANY],
        out_specs=[_HBM] * n + [_SEM, _SEM],
        out_shape=[pltpu.HBM(z.shape, z.dtype) for z in zones] + [pltpu.SemaphoreType.DMA((3 * n,))] * 2,
        input_output_aliases={k: k for k in range(n)},
        compiler_params=pltpu.CompilerParams(has_side_effects=_EFFECT, collective_id=collective_id),
    )(*zones, send_sems, recv_sems, after)
    return list(res[:n]), res[n], res[n + 1]


def _gather_finish(zones, fsend, frecv, after, name):
    n = len(zones)

    def body(*refs):
        ins = refs[:n]
        fsend_ref, frecv_ref = refs[n], refs[n + 1]
        x, y, c = _place()
        sibling = (x, y, 1 - c)
        chips = [(1 - x, y), (x, 1 - y), (1 - x, 1 - y)]
        for j, chip in enumerate(chips):
            for k in range(n):
                mine = ins[k].at[_dev_index((*chip, c))]
                theirs = ins[k].at[_dev_index((*chip, 1 - c))]
                pltpu.make_async_remote_copy(
                    src_ref=theirs, dst_ref=theirs, send_sem=fsend_ref.at[3 * k + j],
                    recv_sem=frecv_ref.at[3 * k + j], device_id=sibling, device_id_type=MESH).wait_recv()
                pltpu.make_async_remote_copy(
                    src_ref=mine, dst_ref=mine, send_sem=fsend_ref.at[3 * k + j],
                    recv_sem=frecv_ref.at[3 * k + j], device_id=sibling, device_id_type=MESH).wait_send()

    res = pl.pallas_call(
        body, name=name, in_specs=[_HBM] * n + [_SEM, _SEM, _ANY], out_specs=[_HBM] * n,
        out_shape=[pltpu.HBM(z.shape, z.dtype) for z in zones],
        input_output_aliases={k: k for k in range(n)},
        compiler_params=pltpu.CompilerParams(has_side_effects=_EFFECT),
    )(*zones, fsend, frecv, after)
    return list(res)


class _Gather:
    def __init__(self, started):
        self._phase = list(started)

    def forward(self, gi, after):
        zones, send_sems, recv_sems = self._phase[gi]
        self._phase[gi] = _gather_forward(zones, send_sems, recv_sems, after, gi + 1, "gather_forward_%d" % gi)

    def finish(self, gi, after):
        zones, fsend, frecv = self._phase[gi]
        out = {}
        for name, z in zip(GATHER_GROUPS[gi], _gather_finish(zones, fsend, frecv, after, "gather_finish_%d" % gi)):
            if name == "w_dw":
                out[name] = jnp.transpose(z, (1, 0, 2)).reshape(TAPS + 1, CC)[:TAPS]
            else:
                out[name] = z.reshape(NDEV * z.shape[1], D)
        return out


def _sibling_exchange(gbs):
    n = len(gbs)

    def body(*refs):
        ins, outs = refs[:n], refs[n:2 * n]
        send_sems, recv_sems = refs[2 * n:]
        x, y, c = _place()
        copies = []
        for k in range(n):
            for s in range(4):
                copies.append(pltpu.make_async_remote_copy(
                    src_ref=ins[k].at[2 * s + (1 - c)], dst_ref=outs[k].at[s],
                    send_sem=send_sems.at[4 * k + s], recv_sem=recv_sems.at[4 * k + s],
                    device_id=(x, y, 1 - c), device_id_type=MESH))
        for cp in copies:
            cp.start()
        for cp in copies:
            cp.wait_recv()
        for cp in copies:
            cp.wait_send()

    return pl.pallas_call(
        body, name="grad_sibling_exchange",
        in_specs=_hbm_specs(n), out_specs=_hbm_specs(n),
        out_shape=[_sds((4,) + g.shape[1:], g.dtype) for g in gbs],
        scratch_shapes=[pltpu.SemaphoreType.DMA((4 * n,)), pltpu.SemaphoreType.DMA((4 * n,))],
    )(*gbs)


def _pair_sum(gfs, ts, c_idx):
    n = len(gfs)
    halves = 2

    def body(c_ref, *refs):
        for k in range(n):
            refs[2 * n + k][...] = (refs[k][...] + refs[n + k][...].astype(F32)).astype(jnp.bfloat16)

    in_specs, out_specs, out_shape = [], [], []
    for g in gfs:
        rh = g.shape[1] // halves
        in_specs.append(pl.BlockSpec((None, rh, D), lambda s, h, c_ref: (2 * s + c_ref[0], h, 0)))
    for g in gfs:
        rh = g.shape[1] // halves
        in_specs.append(pl.BlockSpec((None, rh, D), lambda s, h, c_ref: (s, h, 0)))
        out_specs.append(pl.BlockSpec((None, rh, D), lambda s, h, c_ref: (s, h, 0)))
        out_shape.append(_sds((4, g.shape[1], D), jnp.bfloat16))
    return pl.pallas_call(
        body, name="grad_pair_sum",
        grid_spec=pltpu.PrefetchScalarGridSpec(num_scalar_prefetch=1, grid=(4, halves), in_specs=in_specs,
                                               out_specs=out_specs),
        out_shape=out_shape,
        compiler_params=pltpu.CompilerParams(dimension_semantics=("arbitrary", "arbitrary"),
                                             vmem_limit_bytes=48 * 2 ** 20),
    )(c_idx, *gfs, *ts)


def _chip_exchange(pbs, pack, gdw):
    n = len(pbs)

    def body(*refs):
        ins, pack_ref, gdw_ref = refs[:n], refs[n], refs[n + 1]
        outs, packs_ref, gdws_ref = refs[n + 2:2 * n + 2], refs[2 * n + 2], refs[2 * n + 3]
        send_sems, recv_sems, local_sems, s_send, s_recv = refs[2 * n + 4:]
        x, y, c = _place()
        own = 2 * x + y
        me = 4 * x + 2 * y + c
        chips = [(1 - x, y), (x, 1 - y), (1 - x, 1 - y)]
        local = [pltpu.make_async_copy(ins[k].at[own], outs[k].at[own], local_sems.at[k]) for k in range(n)]
        local.append(pltpu.make_async_copy(pack_ref, packs_ref.at[me], local_sems.at[n]))
        local.append(pltpu.make_async_copy(gdw_ref, gdws_ref.at[me], local_sems.at[n + 1]))
        for cp in local:
            cp.start()
        copies = []
        for k in range(n):
            for j, chip in enumerate(chips):
                copies.append(pltpu.make_async_remote_copy(
                    src_ref=ins[k].at[2 * chip[0] + chip[1]], dst_ref=outs[k].at[own],
                    send_sem=send_sems.at[3 * k + j], recv_sem=recv_sems.at[3 * k + j],
                    device_id=(*chip, c), device_id_type=MESH))
        peers = [(x, y, 1 - c)] + [(*chip, cc) for chip in chips for cc in (c, 1 - c)]
        for j, peer in enumerate(peers):
            copies.append(pltpu.make_async_remote_copy(
                src_ref=pack_ref, dst_ref=packs_ref.at[me], send_sem=s_send.at[2 * j], recv_sem=s_recv.at[2 * j],
                device_id=peer, device_id_type=MESH))
            copies.append(pltpu.make_async_remote_copy(
                src_ref=gdw_ref, dst_ref=gdws_ref.at[me], send_sem=s_send.at[2 * j + 1],
                recv_sem=s_recv.at[2 * j + 1], device_id=peer, device_id_type=MESH))
        for cp in copies:
            cp.start()
        for cp in copies:
            cp.wait_recv()
        for cp in copies:
            cp.wait_send()
        for cp in local:
            cp.wait()

    return pl.pallas_call(
        body, name="grad_chip_exchange",
        in_specs=_hbm_specs(n + 2), out_specs=_hbm_specs(n + 2),
        out_shape=[_sds(p.shape, p.dtype) for p in pbs] + [_sds((NDEV,) + pack.shape, F32),
                                                            _sds((NDEV,) + gdw.shape, F32)],
        scratch_shapes=[pltpu.SemaphoreType.DMA((3 * n,)), pltpu.SemaphoreType.DMA((3 * n,)),
                        pltpu.SemaphoreType.DMA((n + 2,)), pltpu.SemaphoreType.DMA((14,)),
                        pltpu.SemaphoreType.DMA((14,))],
    )(*pbs, pack, gdw)


def _chip_sum(qs):
    n = len(qs)
    halves = 2

    def body(*refs):
        for k in range(n):
            q = refs[k]
            refs[n + k][...] = ((q[0].astype(F32) + q[1].astype(F32)) + q[2].astype(F32)) + q[3].astype(F32)

    in_specs = [pl.BlockSpec((4, q.shape[1] // halves, D), lambda h: (0, h, 0)) for q in qs]
    out_specs = [pl.BlockSpec((q.shape[1] // halves, D), lambda h: (h, 0)) for q in qs]
    return pl.pallas_call(
        body, name="grad_chip_sum", grid=(halves,),
        in_specs=in_specs, out_specs=out_specs,
        out_shape=[_sds((q.shape[1], D), F32) for q in qs],
        compiler_params=_params(48),
    )(*qs)


def _adamw_math(w, g, m, v):
    m = ADAM_B1 * m + (1.0 - ADAM_B1) * g
    v = ADAM_B2 * v + (1.0 - ADAM_B2) * (g * g)
    m_hat = m / (1.0 - ADAM_B1 ** ADAM_STEP)
    v_hat = v / (1.0 - ADAM_B2 ** ADAM_STEP)
    delta = -ADAM_LR * (m_hat / (jnp.sqrt(v_hat) + ADAM_EPS) + ADAM_WD * w)
    return delta, m, v


def _adamw(w, g, m, v, name):
    R, C = w.shape
    rc = 256 if R % 256 == 0 and R > 256 else R

    def body(w_ref, g_ref, m_ref, v_ref, d_ref, nm_ref, nv_ref):
        d, nm, nv = _adamw_math(w_ref[...], g_ref[...], m_ref[...], v_ref[...])
        d_ref[...] = d
        nm_ref[...] = nm
        nv_ref[...] = nv

    return pl.pallas_call(
        body, name=name, grid=(R // rc,),
        in_specs=[_rows(rc, C)] * 4, out_specs=[_rows(rc, C)] * 3,
        out_shape=[_sds((R, C), F32)] * 3,
        compiler_params=_params(40),
    )(w, g, m, v)


PACK = (("loss", LANES, 1), ("g_mix", D, D), ("b_in", INC, INC), ("b_dw", CC, CC), ("g_conv_ln", CC, CC),
        ("b_conv_ln", CC, CC), ("attn_sink", LANES, NQ), ("b_out", D, D), ("g_mem_q", D, D), ("g_mem_kv", D, D),
        ("g_ffn", D, D), ("g_final", D, D))
PACK_LANES = sum(p[1] for p in PACK)


def _small_update(packs, gdws, ws, ms, vs):
    names = [p[0] for p in PACK[1:]]
    n = len(names)

    def body(*refs):
        packs_ref, gdws_ref = refs[0], refs[1]
        w_refs, m_refs, v_refs = refs[2:2 + n], refs[2 + n:2 + 2 * n], refs[2 + 2 * n:2 + 3 * n]
        outs = refs[2 + 3 * n:]
        tot = packs_ref[0]
        gdw = gdws_ref[0]
        for s in range(1, NDEV):
            tot = tot + packs_ref[s]
            gdw = gdw + gdws_ref[s]
        outs[0][...] = tot[:, 0:LANES]
        outs[1][...] = gdw
        off = LANES
        for k, (_, lanes, used) in enumerate(PACK[1:]):
            g = tot[:, off:off + used]
            off += lanes
            d, nm, nv = _adamw_math(w_refs[k][...], g, m_refs[k][...], v_refs[k][...])
            o = outs[2 + 4 * k:6 + 4 * k]
            o[0][...] = g
            o[1][...] = d
            o[2][...] = nm
            o[3][...] = nv

    def full(a):
        nd = a.ndim
        return pl.BlockSpec(a.shape, lambda i: (0,) * nd)

    ins = [packs, gdws, *ws, *ms, *vs]
    out_shape = [_sds((1, LANES), F32), _sds(gdws.shape[1:], F32)]
    for w in ws:
        out_shape += [_sds(w.shape, F32)] * 4
    return pl.pallas_call(
        body, name="small_update", grid=(1,),
        in_specs=[full(a) for a in ins], out_specs=[full(o) for o in out_shape], out_shape=out_shape,
        compiler_params=_params(40),
    )(*ins)


def kernel(x, mem, g_mix, w_in, b_in, w_dw, b_dw, g_conv_ln, b_conv_ln, attn_sink, w_out, b_out, g_mem_q, g_mem_kv, w_mem_q, w_mem_kv, w_mem_o, g_ffn, w_gate, w_up, w_down, g_final, loss_target, m_g_mix, m_w_in, m_b_in, m_w_dw, m_b_dw, m_g_conv_ln, m_b_conv_ln, m_attn_sink, m_w_out, m_b_out, m_g_mem_q, m_g_mem_kv, m_w_mem_q, m_w_mem_kv, m_w_mem_o, m_g_ffn, m_w_gate, m_w_up, m_w_down, m_g_final, v_g_mix, v_w_in, v_b_in, v_w_dw, v_b_dw, v_g_conv_ln, v_b_conv_ln, v_attn_sink, v_w_out, v_b_out, v_g_mem_q, v_g_mem_kv, v_w_mem_q, v_w_mem_kv, v_w_mem_o, v_g_ffn, v_w_gate, v_w_up, v_w_down, v_g_final):
    given = dict(locals())
    bf = jnp.bfloat16
    xi, yi, ci = _place()
    me = 4 * xi + 2 * yi + ci

    transposed = {"w_inT": w_in, "w_mem_kvT": w_mem_kv, "w_gateT": w_gate, "w_upT": w_up}
    plain = {"w_out": w_out, "w_mem_q": w_mem_q, "w_mem_o": w_mem_o, "w_down": w_down}
    blocks = {n: (transposed[n][0].T if n in transposed else plain[n][0]).astype(bf) for n in W_NAMES}
    blocks["w_dw"] = jnp.pad(w_dw[0], ((0, 1), (0, 0)))

    def landing_zone(b):
        zone = lax.empty((NDEV,) + b.shape, b.dtype)
        return lax.dynamic_update_slice(zone, b[None], (me,) + (0,) * b.ndim)

    gather = _Gather(_gather_start([[landing_zone(blocks[n]) for n in names] for names in GATHER_GROUPS]))

    sp = {n: given[n] for n in ("g_mix", "b_in", "b_dw", "g_conv_ln", "b_conv_ln", "attn_sink", "b_out", "g_mem_q",
                                "g_mem_kv", "g_ffn")}
    sp["g_final"] = g_final.reshape(1, D)
    grad_x, small, wgrads, g_wdw = _local_step(x[0], mem[0], loss_target[0], sp, gather)

    gfs = [g[0].reshape(NDEV, r, D) for g, r in zip(wgrads, W_ROWS)]
    gbs = [g[1].reshape(NDEV, r, D) for g, r in zip(wgrads, W_ROWS)]
    ts = _sibling_exchange(gbs)
    pbs = _pair_sum(gfs, ts, ci.astype(jnp.int32).reshape(1))
    pack = jnp.concatenate(
        [jnp.pad(small[name], ((0, 0), (0, lanes - small[name].shape[1]))) for name, lanes, _ in PACK], axis=1)
    exchanged = _chip_exchange(pbs, pack, g_wdw)
    qs, packs, gdws = exchanged[:-2], exchanged[-2], exchanged[-1]
    gsh = dict(zip(W_NAMES, _chip_sum(qs)))

    grads, deltas, new_m, new_v = {}, {}, {}, {}
    names_small = [p[0] for p in PACK[1:]]

    def as_row(a):
        return a.reshape(1, -1)

    res = _small_update(packs, gdws, [as_row(given[n]) for n in names_small],
                        [as_row(given["m_" + n]) for n in names_small],
                        [as_row(given["v_" + n]) for n in names_small])
    loss = res[0][0, 0]
    gdw_sum = res[1]
    for k, n in enumerate(names_small):
        shape = given[n].shape
        g, d, nm, nv = res[2 + 4 * k:6 + 4 * k]
        grads[n], deltas[n], new_m[n], new_v[n] = (g.reshape(shape), d.reshape(shape), nm.reshape(shape),
                                                   nv.reshape(shape))

    big = {"w_out": gsh["w_out"], "w_mem_q": gsh["w_mem_q"], "w_mem_kv": gsh["w_mem_kvT"].T,
           "w_mem_o": gsh["w_mem_o"], "w_down": gsh["w_down"],
           "w_dw": lax.dynamic_slice(gdw_sum, (0, me * (CC // NDEV)), (TAPS, CC // NDEV))}
    for n, g in big.items():
        d, nm, nv = _adamw(given[n][0], g, given["m_" + n][0], given["v_" + n][0], "adamw_" + n)
        grads[n], deltas[n], new_m[n], new_v[n] = g[None], d[None], nm[None], nv[None]
    for n in ("w_in", "w_gate", "w_up"):
        g = gsh[n + "T"]
        d, nm, nv = _adamw(given[n][0].T, g, given["m_" + n][0].T, given["v_" + n][0].T, "adamw_" + n)
        grads[n], deltas[n], new_m[n], new_v[n] = g.T[None], d.T[None], nm.T[None], nv.T[None]

    order = ("g_mix", "w_in", "b_in", "w_dw", "b_dw", "g_conv_ln", "b_conv_ln", "attn_sink", "w_out", "b_out",
             "g_mem_q", "g_mem_kv", "w_mem_q", "w_mem_kv", "w_mem_o", "g_ffn", "w_gate", "w_up", "w_down", "g_final")
    return (loss, grad_x[None], *[grads[n] for n in order], *[deltas[n] for n in order],
            *[new_m[n] for n in order], *[new_v[n] for n in order])
```

```python
import functools

import jax
import jax.numpy as jnp
from jax import lax
from jax.experimental import pallas as pl
from jax.experimental.pallas import tpu as pltpu

D = 1024
CC = 512
HD = 64
NQ = 8
NKV = 2
GRP = NQ // NKV
ATT = NQ * HD
KVC = NKV * HD
BLK = 128
MH = 4
MHD = D // MH
F = 2816
FCH = F // 2
INC = 2 * CC + ATT + 2 * KVC
TAPS = 31
EPS = 1e-6
ROPE_THETA = 10000.0
SCALE = HD ** -0.5
MSCALE = MHD ** -0.5
LANES = 128
NDEV = 8

ADAM_LR = 0.001
ADAM_B1 = 0.9
ADAM_B2 = 0.999
ADAM_EPS = 1e-08
ADAM_WD = 0.01
ADAM_STEP = 10

F32 = jnp.float32
_MXU = jnp.bfloat16
MESH = pl.DeviceIdType.MESH

W_NAMES = ("w_inT", "w_out", "w_mem_q", "w_mem_kvT", "w_mem_o", "w_gateT", "w_upT", "w_down")
W_ROWS = (INC // NDEV, D // NDEV, D // NDEV, 2 * D // NDEV, D // NDEV, F // NDEV, F // NDEV, F // NDEV)


def _mm(a, b, ca, cb):
    return lax.dot_general(a.astype(_MXU), b.astype(_MXU), (((ca,), (cb,)), ((), ())),
                           preferred_element_type=F32)


def _nn(a, b):
    return _mm(a, b, 1, 0)


def _nt(a, b):
    return _mm(a, b, 1, 1)


def _tn(a, b):
    return _mm(a, b, 0, 0)


def _sigmoid(x):
    return 1.0 / (1.0 + jnp.exp(-x))


def _rms(x, g):
    r = lax.rsqrt(jnp.mean(x * x, axis=-1, keepdims=True) + EPS)
    return x * r * g, r


def _rms_bwd(dy, x, r, g):
    xh = x * r
    dg = jnp.sum(dy * xh, axis=0, keepdims=True)
    dxh = dy * g
    dx = r * (dxh - xh * jnp.mean(dxh * xh, axis=-1, keepdims=True))
    return dx, dg


def _rope(t, c, s):
    n = t.shape[1]
    reps = n // LANES
    if reps > 1:
        c = jnp.tile(c, (1, reps))
        s = jnp.tile(s, (1, reps))
    lane = lax.broadcasted_iota(jnp.int32, t.shape, 1)
    first = (lane & (HD - 1)) < (HD // 2)
    partner = jnp.where(first, pltpu.roll(t, n - HD // 2, 1), pltpu.roll(t, HD // 2, 1))
    return t * c + partner * s


def _rope_bwd(dt, c, s):
    n = dt.shape[1]
    reps = n // LANES
    if reps > 1:
        c = jnp.tile(c, (1, reps))
        s = jnp.tile(s, (1, reps))
    lane = lax.broadcasted_iota(jnp.int32, dt.shape, 1)
    first = (lane & (HD - 1)) < (HD // 2)
    ds = dt * s
    partner = jnp.where(first, pltpu.roll(ds, n - HD // 2, 1), pltpu.roll(ds, HD // 2, 1))
    return dt * c + partner


def _acc_init(i, *refs):
    @pl.when(i == 0)
    def _():
        for r in refs:
            r[...] = jnp.zeros(r.shape, r.dtype)


def _acc_row(ref, v):
    ref[...] += jnp.broadcast_to(v, ref.shape)


def _rows(tm, n):
    return pl.BlockSpec((tm, n), lambda i: (i, 0))


def _full(shape):
    nd = len(shape)
    return pl.BlockSpec(shape, lambda i: (0,) * nd)


def _once(shape):
    nd = len(shape)
    return pl.BlockSpec(shape, lambda i: (0,) * nd, pipeline_mode=pl.Buffered(1))


def _params(vmem_mb):
    return pltpu.CompilerParams(dimension_semantics=("arbitrary",), vmem_limit_bytes=vmem_mb * 2 ** 20)


def _sds(shape, dtype):
    return jax.ShapeDtypeStruct(shape, dtype)


def _mem_kv_fwd(mem, g_kv, wkvT):
    M = mem.shape[0]

    def body(mem_ref, g_ref, w_ref, km_ref, vm_ref, mn_ref):
        y, _ = _rms(mem_ref[...], g_ref[...])
        kv = _nt(y, w_ref[...])
        km_ref[...] = kv[:, :D].astype(km_ref.dtype)
        vm_ref[...] = kv[:, D:].astype(vm_ref.dtype)
        mn_ref[...] = y.astype(mn_ref.dtype)

    return pl.pallas_call(
        body, name="mem_kv_fwd", grid=(1,),
        in_specs=[_full((M, D)), _full((1, D)), _full((2 * D, D))],
        out_specs=[_full((M, D))] * 3,
        out_shape=[_sds((M, D), _MXU)] * 3,
        compiler_params=_params(40),
    )(mem, g_kv, wkvT)


def _in_proj_fwd(x, g_mix, winT, b_in, cosq, sinq, tm=512):
    S = x.shape[0]

    def body(x_ref, g_ref, w_ref, b_ref, c_ref, s_ref, h_ref, uglu_ref, vg_ref, q_ref, k_ref, v_ref):
        h, _ = _rms(x_ref[...], g_ref[...])
        hb = h.astype(h_ref.dtype)
        h_ref[...] = hb
        u = _nt(hb, w_ref[...]) + b_ref[...]
        uglu_ref[...] = u[:, :2 * CC]
        vg_ref[...] = u[:, :CC] * _sigmoid(u[:, CC:2 * CC])
        c = c_ref[...]
        s = s_ref[...]
        q_ref[...] = _rope(u[:, 2 * CC:2 * CC + ATT], c, s).astype(q_ref.dtype)
        k_ref[...] = _rope(u[:, 2 * CC + ATT:2 * CC + ATT + KVC], c, s).astype(k_ref.dtype)
        v_ref[...] = u[:, 2 * CC + ATT + KVC:].astype(v_ref.dtype)

    return pl.pallas_call(
        body, name="in_proj_fwd", grid=(S // tm,),
        in_specs=[_rows(tm, D), _full((1, D)), _full((INC, D)), _full((1, INC)), _rows(tm, LANES), _rows(tm, LANES)],
        out_specs=[_rows(tm, D), _rows(tm, 2 * CC), _rows(tm, CC), _rows(tm, ATT), _rows(tm, KVC), _rows(tm, KVC)],
        out_shape=[_sds((S, D), _MXU), _sds((S, 2 * CC), F32), _sds((S, CC), F32), _sds((S, ATT), _MXU),
                   _sds((S, KVC), _MXU), _sds((S, KVC), _MXU)],
        compiler_params=_params(48),
    )(x, g_mix, winT, b_in, cosq, sinq)


CONV_HALO = 16


def _dwconv_tile(pad_ref, t0, tm, w_ref, flip):
    cols = []
    for cl in range(CC // LANES):
        lanes = slice(cl * LANES, (cl + 1) * LANES)
        blk = pad_ref[pl.ds(t0, tm + 2 * CONV_HALO), lanes]
        acc = jnp.zeros((tm, LANES), F32)
        for b in range(8):
            sh = blk[b:b + tm + 24, :]
            for a in range(4):
                k = 8 * a + b - 1
                if 0 <= k < TAPS:
                    kk = TAPS - 1 - k if flip else k
                    acc = acc + sh[8 * a:8 * a + tm, :] * w_ref[kk:kk + 1, lanes]
        cols.append(acc)
    return jnp.concatenate(cols, axis=1)


def _fill_padded(i, src_hbm, pad_ref, sem, S):
    @pl.when(i == 0)
    def _():
        zeros = jnp.zeros((CONV_HALO, pad_ref.shape[1]), pad_ref.dtype)
        pad_ref[0:CONV_HALO, :] = zeros
        pad_ref[CONV_HALO + S:2 * CONV_HALO + S, :] = zeros
        cp = pltpu.make_async_copy(src_hbm, pad_ref.at[pl.ds(CONV_HALO, S), :], sem)
        cp.start()
        cp.wait()


def _ln_silu(y, g, b):
    mu = jnp.mean(y, axis=-1, keepdims=True)
    yc = y - mu
    var = jnp.mean(yc * yc, axis=-1, keepdims=True)
    rstd = lax.rsqrt(var + EPS)
    yh = yc * rstd
    z = yh * g + b
    return yh, rstd, z


def _conv_fwd(vg, w_dw, b_dw, ln_g, ln_b, tm=128):
    S = vg.shape[0]

    def body(vg_hbm, w_ref, b_ref, g_ref, bb_ref, y_ref, yc_ref, pad_ref, sem):
        i = pl.program_id(0)
        _fill_padded(i, vg_hbm, pad_ref, sem, S)
        t0 = pl.multiple_of(i * tm, tm)
        y = _dwconv_tile(pad_ref, t0, tm, w_ref, False) + b_ref[...]
        y_ref[...] = y
        _, _, z = _ln_silu(y, g_ref[...], bb_ref[...])
        yc_ref[...] = (z * _sigmoid(z)).astype(yc_ref.dtype)

    return pl.pallas_call(
        body, name="conv_fwd", grid=(S // tm,),
        in_specs=[pl.BlockSpec(memory_space=pl.ANY), _full((TAPS, CC)), _full((1, CC)), _full((1, CC)), _full((1, CC))],
        out_specs=[_rows(tm, CC), _rows(tm, CC)],
        out_shape=[_sds((S, CC), F32), _sds((S, CC), _MXU)],
        scratch_shapes=[pltpu.VMEM((S + 2 * CONV_HALO, CC), F32), pltpu.SemaphoreType.DMA],
        compiler_params=_params(40),
    )(vg, w_dw, b_dw, ln_g, ln_b)


def _attn_window(i, S):
    start = jnp.clip((i - 1) * BLK, 0, S - 3 * BLK)
    return pl.multiple_of(start, BLK)


def _attn_mask(i, start):
    rows = lax.broadcasted_iota(jnp.int32, (GRP * BLK, 3 * BLK), 0)
    cols = lax.broadcasted_iota(jnp.int32, (GRP * BLK, 3 * BLK), 1)
    rel = (start + cols) - (i * BLK + (rows & (BLK - 1)))
    return jnp.abs(rel) <= BLK


def _stack_heads(t, hk):
    return jnp.concatenate([t[:, (hk * GRP + g) * HD:(hk * GRP + g + 1) * HD] for g in range(GRP)], axis=0)


def _attn_probs(qs, kh, sk_ref, hk, mask):
    s = _nt(qs, kh) * SCALE
    s = jnp.where(mask, s, -1e30)
    sk = jnp.concatenate(
        [jnp.broadcast_to(sk_ref[0:1, hk * GRP + g:hk * GRP + g + 1], (BLK, 1)) for g in range(GRP)], axis=0)
    m = jnp.maximum(jnp.max(s, axis=-1, keepdims=True), sk)
    e = jnp.exp(s - m)
    es = jnp.exp(sk - m)
    inv = 1.0 / (jnp.sum(e, axis=-1, keepdims=True) + es)
    return e * inv, es * inv


ATTN_BLOCKS = 4


def _attn_fwd(qr, kr, vv, sink):
    S = qr.shape[0]
    tq = ATTN_BLOCKS * BLK

    def body(q_ref, k_ref, v_ref, sk_ref, o_ref):
        for b in range(ATTN_BLOCKS):
            i = pl.program_id(0) * ATTN_BLOCKS + b
            rows = slice(b * BLK, (b + 1) * BLK)
            start = _attn_window(i, S)
            kb = k_ref[pl.ds(start, 3 * BLK), :]
            vb = v_ref[pl.ds(start, 3 * BLK), :]
            q = q_ref[rows, :]
            mask = _attn_mask(i, start)
            outs = []
            for hk in range(NKV):
                p, _ = _attn_probs(_stack_heads(q, hk), kb[:, hk * HD:(hk + 1) * HD], sk_ref, hk, mask)
                o = _nn(p, vb[:, hk * HD:(hk + 1) * HD])
                outs += [o[g * BLK:(g + 1) * BLK, :] for g in range(GRP)]
            o_ref[rows, :] = jnp.concatenate(outs, axis=1).astype(o_ref.dtype)

    return pl.pallas_call(
        body, name="attn_fwd", grid=(S // tq,),
        in_specs=[_rows(tq, ATT), _full((S, KVC)), _full((S, KVC)), _full((1, NQ))],
        out_specs=_rows(tq, ATT),
        out_shape=_sds((S, ATT), _MXU),
        compiler_params=_params(40),
    )(qr, kr, vv, sink)


def _out_proj_fwd(x, yc, ya, wout, b_out, tm=512):
    S = x.shape[0]

    def body(x_ref, yc_ref, ya_ref, w_ref, b_ref, x1_ref):
        x1_ref[...] = (x_ref[...] + _nn(yc_ref[...], w_ref[0:CC, :]) + _nn(ya_ref[...], w_ref[CC:2 * CC, :])
                       + b_ref[...])

    return pl.pallas_call(
        body, name="out_proj_fwd", grid=(S // tm,),
        in_specs=[_rows(tm, D), _rows(tm, CC), _rows(tm, ATT), _full((D, D)), _full((1, D))],
        out_specs=_rows(tm, D),
        out_shape=_sds((S, D), F32),
        compiler_params=_params(40),
    )(x, yc, ya, wout, b_out)


def _softmax(s):
    m = jnp.max(s, axis=-1, keepdims=True)
    e = jnp.exp(s - m)
    return e * (1.0 / jnp.sum(e, axis=-1, keepdims=True))


def _mem_attn_fwd(x1, g_q, wq, km, vm, wo, tm=512):
    S = x1.shape[0]
    M = km.shape[0]

    def body(x1_ref, g_ref, wq_ref, km_ref, vm_ref, wo_ref, x2_ref, qm_ref, om_ref):
        x1v = x1_ref[...]
        hq, _ = _rms(x1v, g_ref[...])
        qm = _nn(hq, wq_ref[...]).astype(qm_ref.dtype)
        qm_ref[...] = qm
        oms = []
        for h in range(MH):
            sl = slice(h * MHD, (h + 1) * MHD)
            p = _softmax(_nt(qm[:, sl], km_ref[:, sl]) * MSCALE)
            oms.append(_nn(p, vm_ref[:, sl]))
        om = jnp.concatenate(oms, axis=1).astype(om_ref.dtype)
        om_ref[...] = om
        x2_ref[...] = x1v + _nn(om, wo_ref[...])

    return pl.pallas_call(
        body, name="mem_attn_fwd", grid=(S // tm,),
        in_specs=[_rows(tm, D), _full((1, D)), _full((D, D)), _full((M, D)), _full((M, D)), _full((D, D))],
        out_specs=[_rows(tm, D)] * 3,
        out_shape=[_sds((S, D), F32), _sds((S, D), _MXU), _sds((S, D), _MXU)],
        compiler_params=_params(48),
    )(x1, g_q, wq, km, vm, wo)


def _ffn_loss_fwd(x2, tgt, g_ffn, g_final, wgT, wuT, wd, tm=256):
    S = x2.shape[0]

    def body(x2_ref, t_ref, gf_ref, gl_ref, wg_ref, wu_ref, wd_ref,
             gt_ref, up_ref, hf_ref, dx3_ref, dx3b_ref, loss_ref, dgl_ref):
        i = pl.program_id(0)
        _acc_init(i, loss_ref, dgl_ref)
        x2v = x2_ref[...]
        hf, _ = _rms(x2v, gf_ref[...])
        hb = hf.astype(hf_ref.dtype)
        hf_ref[...] = hb
        x3 = x2v
        for c in range(F // FCH):
            sl = slice(c * FCH, (c + 1) * FCH)
            gt = _nt(hb, wg_ref[sl, :])
            up = _nt(hb, wu_ref[sl, :])
            gt_ref[:, sl] = gt.astype(gt_ref.dtype)
            up_ref[:, sl] = up.astype(up_ref.dtype)
            x3 = x3 + _nn(gt * _sigmoid(gt) * up, wd_ref[sl, :])
        gl = gl_ref[...]
        y, r = _rms(x3, gl)
        diff = y - t_ref[...]
        _acc_row(loss_ref, jnp.sum(jnp.sum(diff * diff, axis=-1, keepdims=True), axis=0, keepdims=True) * (0.5 / D))
        dx3, dgl = _rms_bwd(diff * (1.0 / D), x3, r, gl)
        dx3_ref[...] = dx3
        dx3b_ref[...] = dx3.astype(dx3b_ref.dtype)
        _acc_row(dgl_ref, dgl)

    return pl.pallas_call(
        body, name="ffn_loss_fwd", grid=(S // tm,),
        in_specs=[_rows(tm, D), _rows(tm, D), _full((1, D)), _full((1, D)),
                  _once((F, D)), _once((F, D)), _once((F, D))],
        out_specs=[_rows(tm, F), _rows(tm, F), _rows(tm, D), _rows(tm, D), _rows(tm, D),
                   _full((8, LANES)), _full((8, D))],
        out_shape=[_sds((S, F), _MXU), _sds((S, F), _MXU), _sds((S, D), _MXU), _sds((S, D), F32),
                   _sds((S, D), _MXU), _sds((8, LANES), F32), _sds((8, D), F32)],
        compiler_params=_params(56),
    )(x2, tgt, g_ffn, g_final, wgT, wuT, wd)


def _ffn_bwd(dx3, x2, gt, up, g_ffn, wd, wgT, wuT, tm=256):
    S = x2.shape[0]

    def body(dx3_ref, x2_ref, gt_ref, up_ref, g_ref, wd_ref, wg_ref, wu_ref,
             dx2_ref, dx2b_ref, dgt_ref, dup_ref, act_ref, dg_ref):
        i = pl.program_id(0)
        _acc_init(i, dg_ref)
        dx3v = dx3_ref[...]
        db = dx3v.astype(_MXU)
        dhf = jnp.zeros((tm, D), F32)
        for c in range(F // FCH):
            sl = slice(c * FCH, (c + 1) * FCH)
            gtv = gt_ref[:, sl].astype(F32)
            upv = up_ref[:, sl].astype(F32)
            dact = _nt(db, wd_ref[sl, :])
            sig = _sigmoid(gtv)
            silu = gtv * sig
            dup = (dact * silu).astype(dup_ref.dtype)
            dgt = (dact * upv * (sig * (1.0 + gtv * (1.0 - sig)))).astype(dgt_ref.dtype)
            act_ref[:, sl] = (silu * upv).astype(act_ref.dtype)
            dgt_ref[:, sl] = dgt
            dup_ref[:, sl] = dup
            dhf = dhf + _nn(dgt, wg_ref[sl, :]) + _nn(dup, wu_ref[sl, :])
        x2v = x2_ref[...]
        g = g_ref[...]
        _, r = _rms(x2v, g)
        dxn, dg = _rms_bwd(dhf, x2v, r, g)
        dx2 = dx3v + dxn
        dx2_ref[...] = dx2
        dx2b_ref[...] = dx2.astype(dx2b_ref.dtype)
        _acc_row(dg_ref, dg)

    return pl.pallas_call(
        body, name="ffn_bwd", grid=(S // tm,),
        in_specs=[_rows(tm, D), _rows(tm, D), _rows(tm, F), _rows(tm, F), _full((1, D)),
                  _once((F, D)), _once((F, D)), _once((F, D))],
        out_specs=[_rows(tm, D), _rows(tm, D), _rows(tm, F), _rows(tm, F), _rows(tm, F), _full((8, D))],
        out_shape=[_sds((S, D), F32), _sds((S, D), _MXU), _sds((S, F), _MXU), _sds((S, F), _MXU),
                   _sds((S, F), _MXU), _sds((8, D), F32)],
        compiler_params=_params(56),
    )(dx3, x2, gt, up, g_ffn, wd, wgT, wuT)


_AFTER = pl.BlockSpec(memory_space=pl.ANY)


def _wgrad(a, b, name, after, rc=256):
    S, R = a.shape
    N = b.shape[1]

    def body(a_ref, b_ref, after_ref, o_ref, ob_ref):
        g = _tn(a_ref[...], b_ref[...])
        o_ref[...] = g
        ob_ref[...] = g.astype(ob_ref.dtype)

    return pl.pallas_call(
        body, name=name, grid=(R // rc,),
        in_specs=[pl.BlockSpec((S, rc), lambda j: (0, j)), _once((S, N)), _AFTER],
        out_specs=[_rows(rc, N), _rows(rc, N)],
        out_shape=[_sds((R, N), F32), _sds((R, N), jnp.bfloat16)],
        compiler_params=_params(48),
    )(a, b, after)


def _mem_attn_bwd(dx2, x1, qm, km, vm, g_q, wq, wo, wout, after, tm=512):
    S = x1.shape[0]
    M = km.shape[0]

    def body(dx2_ref, x1_ref, qm_ref, km_ref, vm_ref, g_ref, wq_ref, wo_ref, wout_ref, after_ref,
             dx1_ref, dx1b_ref, hq_ref, dqm_ref, dyc_ref, dya_ref, dkm_ref, dvm_ref, dg_ref, dbo_ref):
        i = pl.program_id(0)
        _acc_init(i, dkm_ref, dvm_ref, dg_ref, dbo_ref)
        dx2v = dx2_ref[...]
        dom = _nt(dx2v, wo_ref[...]).astype(_MXU)
        dqs = []
        for h in range(MH):
            sl = slice(h * MHD, (h + 1) * MHD)
            qh = qm_ref[:, sl]
            kh = km_ref[:, sl]
            p = _softmax(_nt(qh, kh) * MSCALE)
            dp = _nt(dom[:, sl], vm_ref[:, sl])
            ds = (p * (dp - jnp.sum(dp * p, axis=-1, keepdims=True))).astype(_MXU)
            dqs.append(_nn(ds, kh) * MSCALE)
            dkm_ref[:, sl] += _tn(ds, qh) * MSCALE
            dvm_ref[:, sl] += _tn(p, dom[:, sl])
        dqm = jnp.concatenate(dqs, axis=1).astype(dqm_ref.dtype)
        dqm_ref[...] = dqm
        dhq = _nt(dqm, wq_ref[...])
        x1v = x1_ref[...]
        g = g_ref[...]
        hq, r = _rms(x1v, g)
        hq_ref[...] = hq.astype(hq_ref.dtype)
        dxn, dg = _rms_bwd(dhq, x1v, r, g)
        dx1 = dx2v + dxn
        dx1_ref[...] = dx1
        dx1b = dx1.astype(dx1b_ref.dtype)
        dx1b_ref[...] = dx1b
        dym = _nt(dx1b, wout_ref[...])
        dyc_ref[...] = dym[:, :CC]
        dya_ref[...] = dym[:, CC:].astype(dya_ref.dtype)
        _acc_row(dg_ref, dg)
        _acc_row(dbo_ref, jnp.sum(dx1, axis=0, keepdims=True))

    return pl.pallas_call(
        body, name="mem_attn_bwd", grid=(S // tm,),
        in_specs=[_rows(tm, D), _rows(tm, D), _rows(tm, D), _full((M, D)), _full((M, D)), _full((1, D)),
                  _once((D, D)), _once((D, D)), _once((D, D)), _AFTER],
        out_specs=[_rows(tm, D), _rows(tm, D), _rows(tm, D), _rows(tm, D), _rows(tm, CC), _rows(tm, ATT),
                   _full((M, D)), _full((M, D)), _full((8, D)), _full((8, D))],
        out_shape=[_sds((S, D), F32), _sds((S, D), _MXU), _sds((S, D), _MXU), _sds((S, D), _MXU),
                   _sds((S, CC), F32), _sds((S, ATT), _MXU),
                   _sds((M, D), F32), _sds((M, D), F32), _sds((8, D), F32), _sds((8, D), F32)],
        compiler_params=_params(56),
    )(dx2, x1, qm, km, vm, g_q, wq, wo, wout, after)


def _mem_kv_bwd(dkm, dvm, mem, g_kv, memn, wkvT):
    M = mem.shape[0]

    def body(dkm_ref, dvm_ref, mem_ref, g_ref, mn_ref, w_ref, gw_ref, gwb_ref, dg_ref):
        dkv = jnp.concatenate([dkm_ref[...], dvm_ref[...]], axis=1).astype(_MXU)
        gw = _tn(dkv, mn_ref[...])
        gw_ref[...] = gw
        gwb_ref[...] = gw.astype(gwb_ref.dtype)
        dmn = _nn(dkv, w_ref[...])
        memv = mem_ref[...]
        g = g_ref[...]
        _, r = _rms(memv, g)
        _, dg = _rms_bwd(dmn, memv, r, g)
        dg_ref[...] = jnp.broadcast_to(dg, dg_ref.shape)

    return pl.pallas_call(
        body, name="mem_kv_bwd", grid=(1,),
        in_specs=[_full((M, D)), _full((M, D)), _full((M, D)), _full((1, D)), _full((M, D)), _full((2 * D, D))],
        out_specs=[_full((2 * D, D)), _full((2 * D, D)), _full((8, D))],
        out_shape=[_sds((2 * D, D), F32), _sds((2 * D, D), jnp.bfloat16), _sds((8, D), F32)],
        compiler_params=_params(48),
    )(dkm, dvm, mem, g_kv, memn, wkvT)


def _attn_bwd(qr, kr, vv, sink, dya, after):
    S = qr.shape[0]
    tq = ATTN_BLOCKS * BLK

    def body(q_ref, k_ref, v_ref, sk_ref, do_ref, after_ref, dq_ref, dk_ref, dv_ref, dsk_ref):
        _acc_init(pl.program_id(0), dk_ref, dv_ref, dsk_ref)
        for b in range(ATTN_BLOCKS):
            i = pl.program_id(0) * ATTN_BLOCKS + b
            rows = slice(b * BLK, (b + 1) * BLK)
            start = _attn_window(i, S)
            kb = k_ref[pl.ds(start, 3 * BLK), :]
            vb = v_ref[pl.ds(start, 3 * BLK), :]
            q = q_ref[rows, :]
            do = do_ref[rows, :]
            mask = _attn_mask(i, start)
            dqs, dks, dvs = [], [], []
            for hk in range(NKV):
                hs = slice(hk * HD, (hk + 1) * HD)
                qs = _stack_heads(q, hk)
                dos = _stack_heads(do, hk)
                p, ps = _attn_probs(qs, kb[:, hs], sk_ref, hk, mask)
                dp = _nt(dos, vb[:, hs])
                delta = jnp.sum(dp * p, axis=-1, keepdims=True)
                ds = (p * (dp - delta)).astype(_MXU)
                dq = _nn(ds, kb[:, hs]) * SCALE
                dqs += [dq[g * BLK:(g + 1) * BLK, :] for g in range(GRP)]
                dks.append(_tn(ds, qs) * SCALE)
                dvs.append(_tn(p, dos))
                dsk = ps * delta
                for g in range(GRP):
                    h = hk * GRP + g
                    tot = jnp.sum(dsk[g * BLK:(g + 1) * BLK, :], axis=0, keepdims=True)
                    dsk_ref[h:h + 1, :] -= jnp.broadcast_to(tot, (1, LANES))
            dq_ref[rows, :] = jnp.concatenate(dqs, axis=1)
            dk_ref[pl.ds(start, 3 * BLK), :] += jnp.concatenate(dks, axis=1)
            dv_ref[pl.ds(start, 3 * BLK), :] += jnp.concatenate(dvs, axis=1)

    return pl.pallas_call(
        body, name="attn_bwd", grid=(S // tq,),
        in_specs=[_rows(tq, ATT), _full((S, KVC)), _full((S, KVC)), _full((1, NQ)), _rows(tq, ATT), _AFTER],
        out_specs=[_rows(tq, ATT), _full((S, KVC)), _full((S, KVC)), _full((8, LANES))],
        out_shape=[_sds((S, ATT), F32), _sds((S, KVC), F32), _sds((S, KVC), F32), _sds((8, LANES), F32)],
        compiler_params=_params(40),
    )(qr, kr, vv, sink, dya, after)


def _conv_norm_bwd(dyc, y, ln_g, ln_b, after, tm=512):
    S = y.shape[0]

    def body(dyc_ref, y_ref, g_ref, b_ref, after_ref, dy_ref, dg_ref, db_ref, dbdw_ref):
        i = pl.program_id(0)
        _acc_init(i, dg_ref, db_ref, dbdw_ref)
        g = g_ref[...]
        yh, rstd, z = _ln_silu(y_ref[...], g, b_ref[...])
        sig = _sigmoid(z)
        dz = dyc_ref[...] * (sig * (1.0 + z * (1.0 - sig)))
        dyh = dz * g
        dy = rstd * (dyh - jnp.mean(dyh, axis=-1, keepdims=True)
                     - yh * jnp.mean(dyh * yh, axis=-1, keepdims=True))
        dy_ref[...] = dy
        _acc_row(dg_ref, jnp.sum(dz * yh, axis=0, keepdims=True))
        _acc_row(db_ref, jnp.sum(dz, axis=0, keepdims=True))
        _acc_row(dbdw_ref, jnp.sum(dy, axis=0, keepdims=True))

    return pl.pallas_call(
        body, name="conv_norm_bwd", grid=(S // tm,),
        in_specs=[_rows(tm, CC), _rows(tm, CC), _full((1, CC)), _full((1, CC)), _AFTER],
        out_specs=[_rows(tm, CC), _full((8, CC)), _full((8, CC)), _full((8, CC))],
        out_shape=[_sds((S, CC), F32), _sds((8, CC), F32), _sds((8, CC), F32), _sds((8, CC), F32)],
        compiler_params=_params(40),
    )(dyc, y, ln_g, ln_b, after)


def _conv_bwd(dy, vg, uglu, w_dw, tm=128):
    S = dy.shape[0]

    def body(dy_hbm, vg_hbm, uglu_ref, w_ref, du_ref, gw_ref, dbin_ref, dyp_ref, vgp_ref, gacc_ref, sems):
        i = pl.program_id(0)
        _acc_init(i, gacc_ref, dbin_ref)
        _fill_padded(i, dy_hbm, dyp_ref, sems.at[0], S)
        _fill_padded(i, vg_hbm, vgp_ref, sems.at[1], S)
        t0 = pl.multiple_of(i * tm, tm)
        dvg = _dwconv_tile(dyp_ref, t0, tm, w_ref, True)
        for cl in range(CC // LANES):
            lanes = slice(cl * LANES, (cl + 1) * LANES)
            dyt = dyp_ref[pl.ds(t0 + CONV_HALO, tm), lanes]
            blk = vgp_ref[pl.ds(t0, tm + 2 * CONV_HALO), lanes]
            for b in range(8):
                sh = blk[b:b + tm + 24, :]
                for a in range(4):
                    k = 8 * a + b - 1
                    if 0 <= k < TAPS:
                        prod = dyt * sh[8 * a:8 * a + tm, :]
                        gacc_ref[k, :, lanes] += jnp.sum(prod.reshape(tm // 8, 8, LANES), axis=0)
        a_v = uglu_ref[:, :CC]
        sg = _sigmoid(uglu_ref[:, CC:])
        da = dvg * sg
        dgate = dvg * a_v * sg * (1.0 - sg)
        du = jnp.concatenate([da, dgate], axis=1)
        du_ref[...] = du.astype(du_ref.dtype)
        _acc_row(dbin_ref, jnp.sum(du, axis=0, keepdims=True))

        @pl.when(i == pl.num_programs(0) - 1)
        def _():
            gw_ref[...] = jnp.zeros(gw_ref.shape, F32)
            for k in range(TAPS):
                gw_ref[k:k + 1, :] = jnp.sum(gacc_ref[k], axis=0, keepdims=True)

    return pl.pallas_call(
        body, name="conv_bwd", grid=(S // tm,),
        in_specs=[pl.BlockSpec(memory_space=pl.ANY), pl.BlockSpec(memory_space=pl.ANY), _rows(tm, 2 * CC),
                  _full((TAPS, CC))],
        out_specs=[_rows(tm, 2 * CC), _full((TAPS + 1, CC)), _full((8, 2 * CC))],
        out_shape=[_sds((S, 2 * CC), _MXU), _sds((TAPS + 1, CC), F32), _sds((8, 2 * CC), F32)],
        scratch_shapes=[pltpu.VMEM((S + 2 * CONV_HALO, CC), F32), pltpu.VMEM((S + 2 * CONV_HALO, CC), F32),
                        pltpu.VMEM((TAPS, 8, CC), F32), pltpu.SemaphoreType.DMA((2,))],
        compiler_params=_params(48),
    )(dy, vg, uglu, w_dw)


def _in_proj_bwd(duglu, dqr, dkr, dv, cosq, sinq, x, dx1, g_mix, winT, tm=512):
    S = x.shape[0]
    NA = ATT + 2 * KVC

    def body(dug_ref, dq_ref, dk_ref, dv_ref, c_ref, s_ref, x_ref, dx1_ref, g_ref, w_ref,
             dx_ref, du_ref, dg_ref, dbin_ref):
        i = pl.program_id(0)
        _acc_init(i, dg_ref, dbin_ref)
        c = c_ref[...]
        s = s_ref[...]
        datt = jnp.concatenate([_rope_bwd(dq_ref[...], c, s), _rope_bwd(dk_ref[...], c, s), dv_ref[...]], axis=1)
        _acc_row(dbin_ref, jnp.sum(datt, axis=0, keepdims=True))
        du = jnp.concatenate([dug_ref[...], datt.astype(du_ref.dtype)], axis=1)
        du_ref[...] = du
        dh = _nn(du, w_ref[...])
        xv = x_ref[...]
        g = g_ref[...]
        _, r = _rms(xv, g)
        dxn, dg = _rms_bwd(dh, xv, r, g)
        dx_ref[...] = dx1_ref[...] + dxn
        _acc_row(dg_ref, dg)

    return pl.pallas_call(
        body, name="in_proj_bwd", grid=(S // tm,),
        in_specs=[_rows(tm, 2 * CC), _rows(tm, ATT), _rows(tm, KVC), _rows(tm, KVC), _rows(tm, LANES),
                  _rows(tm, LANES), _rows(tm, D), _rows(tm, D), _full((1, D)), _once((INC, D))],
        out_specs=[_rows(tm, D), _rows(tm, INC), _full((8, D)), _full((8, NA))],
        out_shape=[_sds((S, D), F32), _sds((S, INC), _MXU), _sds((8, D), F32), _sds((8, NA), F32)],
        compiler_params=_params(48),
    )(duglu, dqr, dkr, dv, cosq, sinq, x, dx1, g_mix, winT)


def _rope_tables(S):
    pos = jnp.arange(S, dtype=F32)
    inv_freq = ROPE_THETA ** (-jnp.arange(0, HD, 2, dtype=F32) / HD)
    ang = pos[:, None] * inv_freq[None, :]
    cos, sin = jnp.cos(ang), jnp.sin(ang)
    cosq = jnp.tile(jnp.concatenate([cos, cos], axis=1), (1, LANES // HD))
    sinq = jnp.tile(jnp.concatenate([-sin, sin], axis=1), (1, LANES // HD))
    return cosq, sinq


GATHER_GROUPS = (("w_inT", "w_dw"), ("w_out", "w_mem_kvT", "w_mem_q", "w_mem_o"), ("w_gateT", "w_upT", "w_down"))


def _local_step(x, mem, tgt, sp, gather, reduce):
    S = x.shape[0]
    cosq, sinq = _rope_tables(S)

    gather.forward(0, x)
    w0 = gather.finish(0, x)
    winT, w_dw_full = w0["w_inT"], w0["w_dw"]
    h, uglu, vg, qr, kr, vv = _in_proj_fwd(x, sp["g_mix"], winT, sp["b_in"], cosq, sinq)
    y, yc = _conv_fwd(vg, w_dw_full, sp["b_dw"], sp["g_conv_ln"], sp["b_conv_ln"])
    gather.forward(1, yc)
    ya = _attn_fwd(qr, kr, vv, sp["attn_sink"])
    w1 = gather.finish(1, ya)
    gather.forward(2, ya)
    wout, wkvT, wq, wo = w1["w_out"], w1["w_mem_kvT"], w1["w_mem_q"], w1["w_mem_o"]
    km, vm, memn = _mem_kv_fwd(mem, sp["g_mem_kv"], wkvT)
    x1 = _out_proj_fwd(x, yc, ya, wout, sp["b_out"])
    x2, qm, om = _mem_attn_fwd(x1, sp["g_mem_q"], wq, km, vm, wo)
    w2 = gather.finish(2, om)
    wgT, wuT, wd = w2["w_gateT"], w2["w_upT"], w2["w_down"]
    gt, up, hf, dx3, dx3b, loss_acc, dg_final = _ffn_loss_fwd(x2, tgt, sp["g_ffn"], sp["g_final"], wgT, wuT, wd)

    dx2, dx2b, dgt, dup, act, dg_ffn = _ffn_bwd(dx3, x2, gt, up, sp["g_ffn"], wd, wgT, wuT)
    token = reduce.start(0, [_wgrad(dgt, hf, "wgrad_gate", dx2), _wgrad(dup, hf, "wgrad_up", dx2),
                             _wgrad(act, dx3b, "wgrad_down", dx2)])

    dx1, dx1b, hq, dqm, dyc, dya, dkm, dvm, dg_mem_q, db_out = _mem_attn_bwd(
        dx2, x1, qm, km, vm, sp["g_mem_q"], wq, wo, wout, token)
    token = reduce.middle(0, dx1)
    g_wo = _wgrad(om, dx2b, "wgrad_mem_o", token)
    g_wq = _wgrad(hq, dqm, "wgrad_mem_q", token)
    g_wkvT_f, g_wkvT_b, dg_mem_kv = _mem_kv_bwd(dkm, dvm, mem, sp["g_mem_kv"], memn, wkvT)
    g_wout_c = _wgrad(yc, dx1b, "wgrad_out_conv", token)
    g_wout_a = _wgrad(ya, dx1b, "wgrad_out_attn", token)
    g_wout = tuple(jnp.concatenate([c, a], axis=0) for c, a in zip(g_wout_c, g_wout_a))
    token = reduce.start(1, [g_wout, g_wq, (g_wkvT_f, g_wkvT_b), g_wo])

    dqr, dkr, dvv, dsink = _attn_bwd(qr, kr, vv, sp["attn_sink"], dya, token)
    token = reduce.middle(1, dqr)
    dy, dg_cln, db_cln, db_dw = _conv_norm_bwd(dyc, y, sp["g_conv_ln"], sp["b_conv_ln"], token)
    duglu, g_wdw, dbin_glu = _conv_bwd(dy, vg, uglu, w_dw_full)
    reduce.finish(0, duglu)
    grad_x, du, dg_mix, dbin_att = _in_proj_bwd(duglu, dqr, dkr, dvv, cosq, sinq, x, dx1, sp["g_mix"], winT)
    token = reduce.start(2, [_wgrad(du, h, "wgrad_in", grad_x)])

    small = {
        "loss": loss_acc[0:1, :],
        "g_mix": dg_mix[0:1], "b_in": jnp.concatenate([dbin_glu[0:1], dbin_att[0:1]], axis=1),
        "b_dw": db_dw[0:1], "g_conv_ln": dg_cln[0:1], "b_conv_ln": db_cln[0:1],
        "attn_sink": dsink[:, 0].reshape(1, NQ), "b_out": db_out[0:1], "g_mem_q": dg_mem_q[0:1],
        "g_mem_kv": dg_mem_kv[0:1], "g_ffn": dg_ffn[0:1], "g_final": dg_final[0:1],
    }
    reduce.middle(2, token, reduce.small_zones(small, g_wdw))
    reduce.finish(1, grad_x)
    reduce.finish(2, grad_x)
    return grad_x


def _place():
    return lax.axis_index("x"), lax.axis_index("y"), lax.axis_index("c")


def _hbm_specs(n):
    return [pl.BlockSpec(memory_space=pl.ANY)] * n


_HBM = pl.BlockSpec(memory_space=pltpu.HBM)
_SEM = pl.BlockSpec(memory_space=pltpu.SEMAPHORE)
_ANY = pl.BlockSpec(memory_space=pl.ANY)
_EFFECT = pltpu.SideEffectType.DATAFLOW_SIDE_EFFECTING
_VMEM = pl.BlockSpec(memory_space=pltpu.VMEM)
_TOKEN = jax.ShapeDtypeStruct((8, LANES), F32)


def _dev_index(p):
    return 4 * p[0] + 2 * p[1] + p[2]


def _gather_start(groups):
    sizes = [len(g) for g in groups]
    flat = [z for g in groups for z in g]
    n = len(flat)

    def body(*refs):
        ins = refs[:n]
        sems = refs[2 * n:]
        x, y, c = _place()
        me = _dev_index((x, y, c))
        peers = [(x, y, 1 - c), (1 - x, y, c), (x, 1 - y, c), (1 - x, 1 - y, c)]
        base = 0
        for gi, size in enumerate(sizes):
            send_sems, recv_sems = sems[2 * gi], sems[2 * gi + 1]
            for k in range(size):
                zone = ins[base + k]
                for j, peer in enumerate(peers):
                    pltpu.make_async_remote_copy(
                        src_ref=zone.at[me], dst_ref=zone.at[me], send_sem=send_sems.at[4 * k + j],
                        recv_sem=recv_sems.at[4 * k + j], device_id=peer, device_id_type=MESH).start()
            base += size

    out_shape = [pltpu.HBM(z.shape, z.dtype) for z in flat]
    out_specs = [_HBM] * n
    for size in sizes:
        out_shape += [pltpu.SemaphoreType.DMA((4 * size,))] * 2
        out_specs += [_SEM, _SEM]
    res = pl.pallas_call(
        body, name="gather_start", in_specs=[_HBM] * n, out_specs=out_specs, out_shape=out_shape,
        input_output_aliases={k: k for k in range(n)},
        compiler_params=pltpu.CompilerParams(has_side_effects=_EFFECT),
    )(*[pltpu.with_memory_space_constraint(z, pltpu.HBM) for z in flat])
    out, base = [], 0
    for gi, size in enumerate(sizes):
        out.append((list(res[base:base + size]), res[n + 2 * gi], res[n + 2 * gi + 1]))
        base += size
    return out


def _gather_forward(zones, send_sems, recv_sems, after, collective_id, name):
    n = len(zones)

    def body(*refs):
        ins = refs[:n]
        send_ref, recv_ref = refs[n], refs[n + 1]
        fsend, frecv = refs[2 * n + 3], refs[2 * n + 4]
        x, y, c = _place()
        sibling = (x, y, 1 - c)
        barrier = pltpu.get_barrier_semaphore()
        pl.semaphore_signal(barrier, inc=1, device_id=sibling, device_id_type=MESH)
        pl.semaphore_wait(barrier, 1)
        chips = [(1 - x, y), (x, 1 - y), (1 - x, 1 - y)]
        for j, chip in enumerate(chips):
            for k in range(n):
                slot = ins[k].at[_dev_index((*chip, c))]
                pltpu.make_async_remote_copy(
                    src_ref=slot, dst_ref=slot, send_sem=send_ref.at[4 * k + 1 + j],
                    recv_sem=recv_ref.at[4 * k + 1 + j], device_id=sibling, device_id_type=MESH).wait_recv()
                pltpu.make_async_remote_copy(
                    src_ref=slot, dst_ref=slot, send_sem=fsend.at[3 * k + j], recv_sem=frecv.at[3 * k + j],
                    device_id=sibling, device_id_type=MESH).start()
        for k in range(n):
            own = ins[k].at[_dev_index((x, y, c))]
            theirs = ins[k].at[_dev_index(sibling)]
            pltpu.make_async_remote_copy(
                src_ref=theirs, dst_ref=theirs, send_sem=send_ref.at[4 * k], recv_sem=recv_ref.at[4 * k],
                device_id=sibling, device_id_type=MESH).wait_recv()
            for j in range(4):
                pltpu.make_async_remote_copy(
                    src_ref=own, dst_ref=own, send_sem=send_ref.at[4 * k + j], recv_sem=recv_ref.at[4 * k + j],
                    device_id=sibling, device_id_type=MESH).wait_send()

    res = pl.pallas_call(
        body, name=name, in_specs=[_HBM] * n + [_SEM, _SEM, _ANY],
        out_specs=[_HBM] * n + [_SEM, _SEM],
        out_shape=[pltpu.HBM(z.shape, z.dtype) for z in zones] + [pltpu.SemaphoreType.DMA((3 * n,))] * 2,
        input_output_aliases={k: k for k in range(n)},
        compiler_params=pltpu.CompilerParams(has_side_effects=_EFFECT, collective_id=collective_id),
    )(*zones, send_sems, recv_sems, after)
    return list(res[:n]), res[n], res[n + 1]


def _gather_finish(zones, fsend, frecv, after, name):
    n = len(zones)

    def body(*refs):
        ins = refs[:n]
        fsend_ref, frecv_ref = refs[n], refs[n + 1]
        x, y, c = _place()
        sibling = (x, y, 1 - c)
        chips = [(1 - x, y), (x, 1 - y), (1 - x, 1 - y)]
        for j, chip in enumerate(chips):
            for k in range(n):
                mine = ins[k].at[_dev_index((*chip, c))]
                theirs = ins[k].at[_dev_index((*chip, 1 - c))]
                pltpu.make_async_remote_copy(
                    src_ref=theirs, dst_ref=theirs, send_sem=fsend_ref.at[3 * k + j],
                    recv_sem=frecv_ref.at[3 * k + j], device_id=sibling, device_id_type=MESH).wait_recv()
                pltpu.make_async_remote_copy(
                    src_ref=mine, dst_ref=mine, send_sem=fsend_ref.at[3 * k + j],
                    recv_sem=frecv_ref.at[3 * k + j], device_id=sibling, device_id_type=MESH).wait_send()

    res = pl.pallas_call(
        body, name=name, in_specs=[_HBM] * n + [_SEM, _SEM, _ANY], out_specs=[_HBM] * n,
        out_shape=[pltpu.HBM(z.shape, z.dtype) for z in zones],
        input_output_aliases={k: k for k in range(n)},
        compiler_params=pltpu.CompilerParams(has_side_effects=_EFFECT),
    )(*zones, fsend, frecv, after)
    return list(res)


class _Gather:
    def __init__(self, started):
        self._phase = list(started)

    def forward(self, gi, after):
        zones, send_sems, recv_sems = self._phase[gi]
        self._phase[gi] = _gather_forward(zones, send_sems, recv_sems, after, gi + 1, "gather_forward_%d" % gi)

    def finish(self, gi, after):
        zones, fsend, frecv = self._phase[gi]
        out = {}
        for name, z in zip(GATHER_GROUPS[gi], _gather_finish(zones, fsend, frecv, after, "gather_finish_%d" % gi)):
            if name == "w_dw":
                out[name] = jnp.transpose(z, (1, 0, 2)).reshape(TAPS + 1, CC)[:TAPS]
            else:
                out[name] = z.reshape(NDEV * z.shape[1], D)
        return out


def _handshake(peers):
    barrier = pltpu.get_barrier_semaphore()
    for peer in peers:
        pl.semaphore_signal(barrier, inc=1, device_id=peer, device_id_type=MESH)
    pl.semaphore_wait(barrier, len(peers))


def _sibling_copies(srcs, lands, send_sems, recv_sems):
    x, y, c = _place()
    return [pltpu.make_async_remote_copy(
        src_ref=srcs[k].at[2 * s + (1 - c)], dst_ref=lands[k].at[s], send_sem=send_sems.at[4 * k + s],
        recv_sem=recv_sems.at[4 * k + s], device_id=(x, y, 1 - c), device_id_type=MESH)
        for k in range(len(srcs)) for s in range(4)]


def _sibling_start(gbs, collective_id, name):
    n = len(gbs)
    lands = [lax.empty((4,) + g.shape[1:], g.dtype) for g in gbs]

    def body(*refs):
        x, y, c = _place()
        _handshake([(x, y, 1 - c)])
        for cp in _sibling_copies(refs[:n], refs[n:2 * n], refs[4 * n], refs[4 * n + 1]):
            cp.start()
        refs[4 * n + 2][...] = jnp.zeros(_TOKEN.shape, _TOKEN.dtype)

    res = pl.pallas_call(
        body, name=name, in_specs=[_HBM] * (2 * n), out_specs=[_HBM] * (2 * n) + [_SEM, _SEM, _VMEM],
        out_shape=[pltpu.HBM(a.shape, a.dtype) for a in gbs + lands] + [pltpu.SemaphoreType.DMA((4 * n,))] * 2
        + [_TOKEN],
        input_output_aliases={k: k for k in range(2 * n)},
        compiler_params=pltpu.CompilerParams(has_side_effects=_EFFECT, collective_id=collective_id),
    )(*[pltpu.with_memory_space_constraint(a, pltpu.HBM) for a in gbs + lands])
    return list(res[:n]), list(res[n:2 * n]), res[2 * n], res[2 * n + 1], res[2 * n + 2]


def _sibling_wait(gbs, lands, send_sems, recv_sems, after, name):
    n = len(gbs)

    def body(*refs):
        for cp in _sibling_copies(refs[:n], refs[n:2 * n], refs[2 * n], refs[2 * n + 1]):
            cp.wait_recv()
            cp.wait_send()

    res = pl.pallas_call(
        body, name=name, in_specs=[_HBM] * (2 * n) + [_SEM, _SEM, _ANY], out_specs=[_HBM] * (2 * n),
        out_shape=[pltpu.HBM(a.shape, a.dtype) for a in gbs + lands],
        input_output_aliases={k: k for k in range(2 * n)},
        compiler_params=pltpu.CompilerParams(has_side_effects=_EFFECT),
    )(*gbs, *lands, send_sems, recv_sems, after)
    return list(res[n:])


def _pair_sum(gfs, ts, c_idx, name):
    n = len(gfs)
    halves = 2

    def body(c_ref, *refs):
        for k in range(n):
            refs[2 * n + k][...] = (refs[k][...] + refs[n + k][...].astype(F32)).astype(jnp.bfloat16)

    in_specs, out_specs, out_shape = [], [], []
    for g in gfs:
        rh = g.shape[1] // halves
        in_specs.append(pl.BlockSpec((None, rh, D), lambda s, h, c_ref: (2 * s + c_ref[0], h, 0)))
    for g in gfs:
        rh = g.shape[1] // halves
        in_specs.append(pl.BlockSpec((None, rh, D), lambda s, h, c_ref: (s, h, 0)))
        out_specs.append(pl.BlockSpec((None, rh, D), lambda s, h, c_ref: (s, h, 0)))
        out_shape.append(_sds((4, g.shape[1], D), jnp.bfloat16))
    return pl.pallas_call(
        body, name=name,
        grid_spec=pltpu.PrefetchScalarGridSpec(num_scalar_prefetch=1, grid=(4, halves), in_specs=in_specs,
                                               out_specs=out_specs),
        out_shape=out_shape,
        compiler_params=pltpu.CompilerParams(dimension_semantics=("arbitrary", "arbitrary"),
                                             vmem_limit_bytes=48 * 2 ** 20),
    )(c_idx, *gfs, *ts)


def _chip_copies(pbs, lands, smalls, send_sems, recv_sems):
    x, y, c = _place()
    own = 2 * x + y
    me = _dev_index((x, y, c))
    chips = [(1 - x, y), (x, 1 - y), (1 - x, 1 - y)]
    n = len(pbs)
    copies = [pltpu.make_async_remote_copy(
        src_ref=pbs[k].at[2 * chip[0] + chip[1]], dst_ref=lands[k].at[own], send_sem=send_sems.at[3 * k + j],
        recv_sem=recv_sems.at[3 * k + j], device_id=(*chip, c), device_id_type=MESH)
        for k in range(n) for j, chip in enumerate(chips)]
    peers = [(x, y, 1 - c)] + [(*chip, cc) for chip in chips for cc in (c, 1 - c)]
    copies += [pltpu.make_async_remote_copy(
        src_ref=smalls[t].at[me], dst_ref=smalls[t].at[me], send_sem=send_sems.at[3 * n + 7 * t + j],
        recv_sem=recv_sems.at[3 * n + 7 * t + j], device_id=peer, device_id_type=MESH)
        for t in range(len(smalls)) for j, peer in enumerate(peers)]
    return copies


def _chip_start(pbs, smalls, collective_id, name):
    n, m = len(pbs), len(smalls)
    lands = [lax.empty(p.shape, p.dtype) for p in pbs]
    arrays = list(pbs) + lands + list(smalls)
    na = len(arrays)
    nsem = 3 * n + 7 * m

    def body(*refs):
        x, y, c = _place()
        chips = [(1 - x, y), (x, 1 - y), (1 - x, 1 - y)]
        peers = [(*chip, c) for chip in chips]
        if m:
            peers += [(x, y, 1 - c)] + [(*chip, 1 - c) for chip in chips]
        _handshake(peers)
        for cp in _chip_copies(refs[:n], refs[n:2 * n], refs[2 * n:na], refs[2 * na], refs[2 * na + 1]):
            cp.start()
        refs[2 * na + 2][...] = jnp.zeros(_TOKEN.shape, _TOKEN.dtype)

    res = pl.pallas_call(
        body, name=name, in_specs=[_HBM] * na, out_specs=[_HBM] * na + [_SEM, _SEM, _VMEM],
        out_shape=[pltpu.HBM(a.shape, a.dtype) for a in arrays] + [pltpu.SemaphoreType.DMA((nsem,))] * 2 + [_TOKEN],
        input_output_aliases={k: k for k in range(na)},
        compiler_params=pltpu.CompilerParams(has_side_effects=_EFFECT, collective_id=collective_id),
    )(*[pltpu.with_memory_space_constraint(a, pltpu.HBM) for a in arrays])
    return (list(res[:n]), list(res[n:2 * n]), list(res[2 * n:na]), res[na], res[na + 1]), res[na + 2]


def _chip_wait(pbs, lands, smalls, send_sems, recv_sems, after, name):
    n = len(pbs)
    arrays = list(pbs) + list(lands) + list(smalls)
    na = len(arrays)

    def body(*refs):
        for cp in _chip_copies(refs[:n], refs[n:2 * n], refs[2 * n:na], refs[na], refs[na + 1]):
            cp.wait_recv()
            cp.wait_send()

    res = pl.pallas_call(
        body, name=name, in_specs=[_HBM] * na + [_SEM, _SEM, _ANY], out_specs=[_HBM] * na,
        out_shape=[pltpu.HBM(a.shape, a.dtype) for a in arrays],
        input_output_aliases={k: k for k in range(na)},
        compiler_params=pltpu.CompilerParams(has_side_effects=_EFFECT),
    )(*arrays, send_sems, recv_sems, after)
    return list(res[:n]), list(res[n:2 * n]), list(res[2 * n:])


def _chip_sum(pbs, qs, own_idx, name):
    n = len(pbs)
    halves = 2

    def body(o_ref, *refs):
        for k in range(n):
            acc = refs[4 * k][...].astype(F32)
            for j in range(1, 4):
                acc = acc + refs[4 * k + j][...].astype(F32)
            refs[4 * n + k][...] = acc

    def other(j):
        return lambda h, o: ((o[0] + j) % 4, h, 0)

    in_specs, ins = [], []
    for p, q in zip(pbs, qs):
        rh = p.shape[1] // halves
        in_specs.append(pl.BlockSpec((None, rh, D), lambda h, o: (o[0], h, 0)))
        ins.append(p)
        for j in range(1, 4):
            in_specs.append(pl.BlockSpec((None, rh, D), other(j)))
            ins.append(q)
    out_specs = [pl.BlockSpec((p.shape[1] // halves, D), lambda h, o: (h, 0)) for p in pbs]
    return pl.pallas_call(
        body, name=name,
        grid_spec=pltpu.PrefetchScalarGridSpec(num_scalar_prefetch=1, grid=(halves,), in_specs=in_specs,
                                               out_specs=out_specs),
        out_shape=[_sds((p.shape[1], D), F32) for p in pbs],
        compiler_params=pltpu.CompilerParams(dimension_semantics=("arbitrary",), vmem_limit_bytes=48 * 2 ** 20),
    )(own_idx, *ins)


REDUCE_GROUPS = (("w_gateT", "w_upT", "w_down"), ("w_out", "w_mem_q", "w_mem_kvT", "w_mem_o"), ("w_inT",))


class _Reduce:
    def __init__(self):
        x, y, c = _place()
        self._c = c.astype(jnp.int32).reshape(1)
        self._own = (2 * x + y).astype(jnp.int32).reshape(1)
        self._me = _dev_index((x, y, c))
        self._state = {}
        self.shards = {}
        self.smalls = None

    def start(self, gi, wgrads):
        gfs = [g[0].reshape(NDEV, g[0].shape[0] // NDEV, D) for g in wgrads]
        gbs = [g[1].reshape(NDEV, g[1].shape[0] // NDEV, D) for g in wgrads]
        gbs, lands, send_sems, recv_sems, token = _sibling_start(gbs, 4 + gi, "reduce_sibling_start_%d" % gi)
        self._state[gi] = (gfs, gbs, lands, send_sems, recv_sems)
        return token

    def middle(self, gi, after, smalls=()):
        gfs, gbs, lands, send_sems, recv_sems = self._state[gi]
        ts = _sibling_wait(gbs, lands, send_sems, recv_sems, after, "reduce_sibling_wait_%d" % gi)
        pbs = _pair_sum(gfs, ts, self._c, "reduce_pair_sum_%d" % gi)
        self._state[gi], token = _chip_start(pbs, list(smalls), 7 + gi, "reduce_chip_start_%d" % gi)
        return token

    def finish(self, gi, after):
        pbs, lands, smalls, send_sems, recv_sems = self._state[gi]
        pbs, qs, smalls = _chip_wait(pbs, lands, smalls, send_sems, recv_sems, after, "reduce_chip_wait_%d" % gi)
        self.shards.update(zip(REDUCE_GROUPS[gi], _chip_sum(pbs, qs, self._own, "reduce_chip_sum_%d" % gi)))
        if smalls:
            self.smalls = smalls

    def small_zones(self, small, g_wdw):
        pack = jnp.concatenate(
            [jnp.pad(small[name], ((0, 0), (0, lanes - small[name].shape[1]))) for name, lanes, _ in PACK], axis=1)
        zones = []
        for a in (pack, g_wdw):
            zone = lax.empty((NDEV,) + a.shape, a.dtype)
            zones.append(lax.dynamic_update_slice(zone, a[None], (self._me,) + (0,) * a.ndim))
        return zones


def _adamw_math(w, g, m, v):
    m = ADAM_B1 * m + (1.0 - ADAM_B1) * g
    v = ADAM_B2 * v + (1.0 - ADAM_B2) * (g * g)
    m_hat = m / (1.0 - ADAM_B1 ** ADAM_STEP)
    v_hat = v / (1.0 - ADAM_B2 ** ADAM_STEP)
    delta = -ADAM_LR * (m_hat / (jnp.sqrt(v_hat) + ADAM_EPS) + ADAM_WD * w)
    return delta, m, v


def _adamw(w, g, m, v, name):
    R, C = w.shape
    rc = 256 if R % 256 == 0 and R > 256 else R

    def body(w_ref, g_ref, m_ref, v_ref, d_ref, nm_ref, nv_ref):
        d, nm, nv = _adamw_math(w_ref[...], g_ref[...], m_ref[...], v_ref[...])
        d_ref[...] = d
        nm_ref[...] = nm
        nv_ref[...] = nv

    return pl.pallas_call(
        body, name=name, grid=(R // rc,),
        in_specs=[_rows(rc, C)] * 4, out_specs=[_rows(rc, C)] * 3,
        out_shape=[_sds((R, C), F32)] * 3,
        compiler_params=_params(40),
    )(w, g, m, v)


PACK = (("loss", LANES, 1), ("g_mix", D, D), ("b_in", INC, INC), ("b_dw", CC, CC), ("g_conv_ln", CC, CC),
        ("b_conv_ln", CC, CC), ("attn_sink", LANES, NQ), ("b_out", D, D), ("g_mem_q", D, D), ("g_mem_kv", D, D),
        ("g_ffn", D, D), ("g_final", D, D))
PACK_LANES = sum(p[1] for p in PACK)


def _small_update(packs, gdws, ws, ms, vs):
    names = [p[0] for p in PACK[1:]]
    n = len(names)

    def body(*refs):
        packs_ref, gdws_ref = refs[0], refs[1]
        w_refs, m_refs, v_refs = refs[2:2 + n], refs[2 + n:2 + 2 * n], refs[2 + 2 * n:2 + 3 * n]
        outs = refs[2 + 3 * n:]
        tot = packs_ref[0]
        gdw = gdws_ref[0]
        for s in range(1, NDEV):
            tot = tot + packs_ref[s]
            gdw = gdw + gdws_ref[s]
        outs[0][...] = tot[:, 0:LANES]
        outs[1][...] = gdw
        off = LANES
        for k, (_, lanes, used) in enumerate(PACK[1:]):
            g = tot[:, off:off + used]
            off += lanes
            d, nm, nv = _adamw_math(w_refs[k][...], g, m_refs[k][...], v_refs[k][...])
            o = outs[2 + 4 * k:6 + 4 * k]
            o[0][...] = g
            o[1][...] = d
            o[2][...] = nm
            o[3][...] = nv

    def full(a):
        nd = a.ndim
        return pl.BlockSpec(a.shape, lambda i: (0,) * nd)

    ins = [packs, gdws, *ws, *ms, *vs]
    out_shape = [_sds((1, LANES), F32), _sds(gdws.shape[1:], F32)]
    for w in ws:
        out_shape += [_sds(w.shape, F32)] * 4
    return pl.pallas_call(
        body, name="small_update", grid=(1,),
        in_specs=[full(a) for a in ins], out_specs=[full(o) for o in out_shape], out_shape=out_shape,
        compiler_params=_params(40),
    )(*ins)


def kernel(x, mem, g_mix, w_in, b_in, w_dw, b_dw, g_conv_ln, b_conv_ln, attn_sink, w_out, b_out, g_mem_q, g_mem_kv, w_mem_q, w_mem_kv, w_mem_o, g_ffn, w_gate, w_up, w_down, g_final, loss_target, m_g_mix, m_w_in, m_b_in, m_w_dw, m_b_dw, m_g_conv_ln, m_b_conv_ln, m_attn_sink, m_w_out, m_b_out, m_g_mem_q, m_g_mem_kv, m_w_mem_q, m_w_mem_kv, m_w_mem_o, m_g_ffn, m_w_gate, m_w_up, m_w_down, m_g_final, v_g_mix, v_w_in, v_b_in, v_w_dw, v_b_dw, v_g_conv_ln, v_b_conv_ln, v_attn_sink, v_w_out, v_b_out, v_g_mem_q, v_g_mem_kv, v_w_mem_q, v_w_mem_kv, v_w_mem_o, v_g_ffn, v_w_gate, v_w_up, v_w_down, v_g_final):
    given = dict(locals())
    bf = jnp.bfloat16
    xi, yi, ci = _place()
    me = 4 * xi + 2 * yi + ci

    transposed = {"w_inT": w_in, "w_mem_kvT": w_mem_kv, "w_gateT": w_gate, "w_upT": w_up}
    plain = {"w_out": w_out, "w_mem_q": w_mem_q, "w_mem_o": w_mem_o, "w_down": w_down}
    blocks = {n: (transposed[n][0].T if n in transposed else plain[n][0]).astype(bf) for n in W_NAMES}
    blocks["w_dw"] = jnp.pad(w_dw[0], ((0, 1), (0, 0)))

    def landing_zone(b):
        zone = lax.empty((NDEV,) + b.shape, b.dtype)
        return lax.dynamic_update_slice(zone, b[None], (me,) + (0,) * b.ndim)

    gather = _Gather(_gather_start([[landing_zone(blocks[n]) for n in names] for names in GATHER_GROUPS]))

    sp = {n: given[n] for n in ("g_mix", "b_in", "b_dw", "g_conv_ln", "b_conv_ln", "attn_sink", "b_out", "g_mem_q",
                                "g_mem_kv", "g_ffn")}
    sp["g_final"] = g_final.reshape(1, D)
    reduce = _Reduce()
    grad_x = _local_step(x[0], mem[0], loss_target[0], sp, gather, reduce)
    gsh = reduce.shards
    packs, gdws = reduce.smalls

    grads, deltas, new_m, new_v = {}, {}, {}, {}
    names_small = [p[0] for p in PACK[1:]]

    def as_row(a):
        return a.reshape(1, -1)

    res = _small_update(packs, gdws, [as_row(given[n]) for n in names_small],
                        [as_row(given["m_" + n]) for n in names_small],
                        [as_row(given["v_" + n]) for n in names_small])
    loss = res[0][0, 0]
    gdw_sum = res[1]
    for k, n in enumerate(names_small):
        shape = given[n].shape
        g, d, nm, nv = res[2 + 4 * k:6 + 4 * k]
        grads[n], deltas[n], new_m[n], new_v[n] = (g.reshape(shape), d.reshape(shape), nm.reshape(shape),
                                                   nv.reshape(shape))

    big = {"w_out": gsh["w_out"], "w_mem_q": gsh["w_mem_q"], "w_mem_kv": gsh["w_mem_kvT"].T,
           "w_mem_o": gsh["w_mem_o"], "w_down": gsh["w_down"],
           "w_dw": lax.dynamic_slice(gdw_sum, (0, me * (CC // NDEV)), (TAPS, CC // NDEV))}
    for n, g in big.items():
        d, nm, nv = _adamw(given[n][0], g, given["m_" + n][0], given["v_" + n][0], "adamw_" + n)
        grads[n], deltas[n], new_m[n], new_v[n] = g[None], d[None], nm[None], nv[None]
    for n in ("w_in", "w_gate", "w_up"):
        g = gsh[n + "T"]
        d, nm, nv = _adamw(given[n][0].T, g, given["m_" + n][0].T, given["v_" + n][0].T, "adamw_" + n)
        grads[n], deltas[n], new_m[n], new_v[n] = g.T[None], d.T[None], nm.T[None], nv.T[None]

    order = ("g_mix", "w_in", "b_in", "w_dw", "b_dw", "g_conv_ln", "b_conv_ln", "attn_sink", "w_out", "b_out",
             "g_mem_q", "g_mem_kv", "w_mem_q", "w_mem_kv", "w_mem_o", "g_ffn", "w_gate", "w_up", "w_down", "g_final")
    return (loss, grad_x[None], *[grads[n] for n in order], *[deltas[n] for n in order],
            *[new_m[n] for n in order], *[new_v[n] for n in order])
```

```python
import functools

import jax
import jax.numpy as jnp
from jax import lax
from jax.experimental import pallas as pl
from jax.experimental.pallas import tpu as pltpu

D = 1024
CC = 512
HD = 64
NQ = 8
NKV = 2
GRP = NQ // NKV
ATT = NQ * HD
KVC = NKV * HD
BLK = 128
MH = 4
MHD = D // MH
F = 2816
FCH = F // 2
INC = 2 * CC + ATT + 2 * KVC
TAPS = 31
EPS = 1e-6
ROPE_THETA = 10000.0
SCALE = HD ** -0.5
MSCALE = MHD ** -0.5
LANES = 128
NDEV = 8

ADAM_LR = 0.001
ADAM_B1 = 0.9
ADAM_B2 = 0.999
ADAM_EPS = 1e-08
ADAM_WD = 0.01
ADAM_STEP = 10

F32 = jnp.float32
_MXU = jnp.bfloat16
MESH = pl.DeviceIdType.MESH

W_NAMES = ("w_inT", "w_out", "w_mem_q", "w_mem_kvT", "w_mem_o", "w_gateT", "w_upT", "w_down")
W_ROWS = (INC // NDEV, D // NDEV, D // NDEV, 2 * D // NDEV, D // NDEV, F // NDEV, F // NDEV, F // NDEV)


def _mm(a, b, ca, cb):
    return lax.dot_general(a.astype(_MXU), b.astype(_MXU), (((ca,), (cb,)), ((), ())),
                           preferred_element_type=F32)


def _nn(a, b):
    return _mm(a, b, 1, 0)


def _nt(a, b):
    return _mm(a, b, 1, 1)


def _tn(a, b):
    return _mm(a, b, 0, 0)


def _sigmoid(x):
    return 1.0 / (1.0 + jnp.exp(-x))


def _rms(x, g):
    r = lax.rsqrt(jnp.mean(x * x, axis=-1, keepdims=True) + EPS)
    return x * r * g, r


def _rms_bwd(dy, x, r, g):
    xh = x * r
    dg = jnp.sum(dy * xh, axis=0, keepdims=True)
    dxh = dy * g
    dx = r * (dxh - xh * jnp.mean(dxh * xh, axis=-1, keepdims=True))
    return dx, dg


def _rope(t, c, s):
    n = t.shape[1]
    reps = n // LANES
    if reps > 1:
        c = jnp.tile(c, (1, reps))
        s = jnp.tile(s, (1, reps))
    lane = lax.broadcasted_iota(jnp.int32, t.shape, 1)
    first = (lane & (HD - 1)) < (HD // 2)
    partner = jnp.where(first, pltpu.roll(t, n - HD // 2, 1), pltpu.roll(t, HD // 2, 1))
    return t * c + partner * s


def _rope_bwd(dt, c, s):
    n = dt.shape[1]
    reps = n // LANES
    if reps > 1:
        c = jnp.tile(c, (1, reps))
        s = jnp.tile(s, (1, reps))
    lane = lax.broadcasted_iota(jnp.int32, dt.shape, 1)
    first = (lane & (HD - 1)) < (HD // 2)
    ds = dt * s
    partner = jnp.where(first, pltpu.roll(ds, n - HD // 2, 1), pltpu.roll(ds, HD // 2, 1))
    return dt * c + partner


def _acc_init(i, *refs):
    @pl.when(i == 0)
    def _():
        for r in refs:
            r[...] = jnp.zeros(r.shape, r.dtype)


def _acc_row(ref, v):
    ref[...] += jnp.broadcast_to(v, ref.shape)


def _rows(tm, n):
    return pl.BlockSpec((tm, n), lambda i: (i, 0))


def _full(shape):
    nd = len(shape)
    return pl.BlockSpec(shape, lambda i: (0,) * nd)


def _once(shape):
    nd = len(shape)
    return pl.BlockSpec(shape, lambda i: (0,) * nd, pipeline_mode=pl.Buffered(1))


def _params(vmem_mb):
    return pltpu.CompilerParams(dimension_semantics=("arbitrary",), vmem_limit_bytes=vmem_mb * 2 ** 20)


def _sds(shape, dtype):
    return jax.ShapeDtypeStruct(shape, dtype)


def _mem_kv_fwd(mem, g_kv, wkvT):
    M = mem.shape[0]

    def body(mem_ref, g_ref, w_ref, km_ref, vm_ref, mn_ref):
        y, _ = _rms(mem_ref[...], g_ref[...])
        kv = _nt(y, w_ref[...])
        km_ref[...] = kv[:, :D].astype(km_ref.dtype)
        vm_ref[...] = kv[:, D:].astype(vm_ref.dtype)
        mn_ref[...] = y.astype(mn_ref.dtype)

    return pl.pallas_call(
        body, name="mem_kv_fwd", grid=(1,),
        in_specs=[_full((M, D)), _full((1, D)), _full((2 * D, D))],
        out_specs=[_full((M, D))] * 3,
        out_shape=[_sds((M, D), _MXU)] * 3,
        compiler_params=_params(40),
    )(mem, g_kv, wkvT)


def _in_proj_fwd(x, g_mix, winT, b_in, cosq, sinq, tm=512):
    S = x.shape[0]

    def body(x_ref, g_ref, w_ref, b_ref, c_ref, s_ref, h_ref, uglu_ref, vg_ref, q_ref, k_ref, v_ref):
        h, _ = _rms(x_ref[...], g_ref[...])
        hb = h.astype(h_ref.dtype)
        h_ref[...] = hb
        u = _nt(hb, w_ref[...]) + b_ref[...]
        uglu_ref[...] = u[:, :2 * CC]
        vg_ref[...] = u[:, :CC] * _sigmoid(u[:, CC:2 * CC])
        c = c_ref[...]
        s = s_ref[...]
        q_ref[...] = _rope(u[:, 2 * CC:2 * CC + ATT], c, s).astype(q_ref.dtype)
        k_ref[...] = _rope(u[:, 2 * CC + ATT:2 * CC + ATT + KVC], c, s).astype(k_ref.dtype)
        v_ref[...] = u[:, 2 * CC + ATT + KVC:].astype(v_ref.dtype)

    return pl.pallas_call(
        body, name="in_proj_fwd", grid=(S // tm,),
        in_specs=[_rows(tm, D), _full((1, D)), _full((INC, D)), _full((1, INC)), _rows(tm, LANES), _rows(tm, LANES)],
        out_specs=[_rows(tm, D), _rows(tm, 2 * CC), _rows(tm, CC), _rows(tm, ATT), _rows(tm, KVC), _rows(tm, KVC)],
        out_shape=[_sds((S, D), _MXU), _sds((S, 2 * CC), F32), _sds((S, CC), F32), _sds((S, ATT), _MXU),
                   _sds((S, KVC), _MXU), _sds((S, KVC), _MXU)],
        compiler_params=_params(48),
    )(x, g_mix, winT, b_in, cosq, sinq)


CONV_HALO = 16


def _dwconv_tile(pad_ref, t0, tm, w_ref, flip):
    cols = []
    for cl in range(CC // LANES):
        lanes = slice(cl * LANES, (cl + 1) * LANES)
        blk = pad_ref[pl.ds(t0, tm + 2 * CONV_HALO), lanes]
        acc = jnp.zeros((tm, LANES), F32)
        for b in range(8):
            sh = blk[b:b + tm + 24, :]
            for a in range(4):
                k = 8 * a + b - 1
                if 0 <= k < TAPS:
                    kk = TAPS - 1 - k if flip else k
                    acc = acc + sh[8 * a:8 * a + tm, :] * w_ref[kk:kk + 1, lanes]
        cols.append(acc)
    return jnp.concatenate(cols, axis=1)


def _fill_padded(i, src_hbm, pad_ref, sem, S):
    @pl.when(i == 0)
    def _():
        zeros = jnp.zeros((CONV_HALO, pad_ref.shape[1]), pad_ref.dtype)
        pad_ref[0:CONV_HALO, :] = zeros
        pad_ref[CONV_HALO + S:2 * CONV_HALO + S, :] = zeros
        cp = pltpu.make_async_copy(src_hbm, pad_ref.at[pl.ds(CONV_HALO, S), :], sem)
        cp.start()
        cp.wait()


def _ln_silu(y, g, b):
    mu = jnp.mean(y, axis=-1, keepdims=True)
    yc = y - mu
    var = jnp.mean(yc * yc, axis=-1, keepdims=True)
    rstd = lax.rsqrt(var + EPS)
    yh = yc * rstd
    z = yh * g + b
    return yh, rstd, z


def _conv_fwd(vg, w_dw, b_dw, ln_g, ln_b, tm=128):
    S = vg.shape[0]

    def body(vg_hbm, w_ref, b_ref, g_ref, bb_ref, y_ref, yc_ref, pad_ref, sem):
        i = pl.program_id(0)
        _fill_padded(i, vg_hbm, pad_ref, sem, S)
        t0 = pl.multiple_of(i * tm, tm)
        y = _dwconv_tile(pad_ref, t0, tm, w_ref, False) + b_ref[...]
        y_ref[...] = y
        _, _, z = _ln_silu(y, g_ref[...], bb_ref[...])
        yc_ref[...] = (z * _sigmoid(z)).astype(yc_ref.dtype)

    return pl.pallas_call(
        body, name="conv_fwd", grid=(S // tm,),
        in_specs=[pl.BlockSpec(memory_space=pl.ANY), _full((TAPS, CC)), _full((1, CC)), _full((1, CC)), _full((1, CC))],
        out_specs=[_rows(tm, CC), _rows(tm, CC)],
        out_shape=[_sds((S, CC), F32), _sds((S, CC), _MXU)],
        scratch_shapes=[pltpu.VMEM((S + 2 * CONV_HALO, CC), F32), pltpu.SemaphoreType.DMA],
        compiler_params=_params(40),
    )(vg, w_dw, b_dw, ln_g, ln_b)


def _attn_window(i, S):
    start = jnp.clip((i - 1) * BLK, 0, S - 3 * BLK)
    return pl.multiple_of(start, BLK)


def _attn_mask(i, start):
    keys = lax.broadcasted_iota(jnp.int32, (3 * BLK, GRP * BLK), 0)
    cols = lax.broadcasted_iota(jnp.int32, (3 * BLK, GRP * BLK), 1)
    rel = (start + keys) - (i * BLK + (cols & (BLK - 1)))
    return jnp.abs(rel) <= BLK


def _stack_heads(t, hk):
    return jnp.concatenate([t[:, (hk * GRP + g) * HD:(hk * GRP + g + 1) * HD] for g in range(GRP)], axis=0)


def _unstack_heads_t(t):
    cols = []
    for p in range(GRP // 2):
        pair = jnp.concatenate([t[:, (2 * p) * BLK:(2 * p + 1) * BLK], t[:, (2 * p + 1) * BLK:(2 * p + 2) * BLK]],
                               axis=0)
        cols.append(pair.T)
    return jnp.concatenate(cols, axis=1)


def _attn_probs(qs, kh, sk_ref, hk, mask):
    s = _nt(kh, qs) * SCALE
    s = jnp.where(mask, s, -1e30)
    sk = jnp.concatenate(
        [jnp.broadcast_to(sk_ref[0:1, hk * GRP + g:hk * GRP + g + 1], (1, BLK)) for g in range(GRP)], axis=1)
    m = jnp.maximum(jnp.max(s, axis=0, keepdims=True), sk)
    e = jnp.exp(s - m)
    es = jnp.exp(sk - m)
    inv = 1.0 / (jnp.sum(e, axis=0, keepdims=True) + es)
    return e, inv, es


ATTN_BLOCKS = 4


def _attn_fwd(qr, kr, vv, sink):
    S = qr.shape[0]
    tq = ATTN_BLOCKS * BLK

    def body(q_ref, k_ref, v_ref, sk_ref, o_ref):
        for b in range(ATTN_BLOCKS):
            i = pl.program_id(0) * ATTN_BLOCKS + b
            rows = slice(b * BLK, (b + 1) * BLK)
            start = _attn_window(i, S)
            kb = k_ref[pl.ds(start, 3 * BLK), :]
            vb = v_ref[pl.ds(start, 3 * BLK), :]
            q = q_ref[rows, :]
            mask = _attn_mask(i, start)
            outs = []
            for hk in range(NKV):
                hs = slice(hk * HD, (hk + 1) * HD)
                e, inv, _ = _attn_probs(_stack_heads(q, hk), kb[:, hs], sk_ref, hk, mask)
                outs.append(_unstack_heads_t(_tn(vb[:, hs], e) * inv))
            o_ref[rows, :] = jnp.concatenate(outs, axis=1).astype(o_ref.dtype)

    return pl.pallas_call(
        body, name="attn_fwd", grid=(S // tq,),
        in_specs=[_rows(tq, ATT), _full((S, KVC)), _full((S, KVC)), _full((1, NQ))],
        out_specs=_rows(tq, ATT),
        out_shape=_sds((S, ATT), _MXU),
        compiler_params=_params(40),
    )(qr, kr, vv, sink)


def _out_proj_fwd(x, yc, ya, wout, b_out, tm=512):
    S = x.shape[0]

    def body(x_ref, yc_ref, ya_ref, w_ref, b_ref, x1_ref):
        x1_ref[...] = (x_ref[...] + _nn(yc_ref[...], w_ref[0:CC, :]) + _nn(ya_ref[...], w_ref[CC:2 * CC, :])
                       + b_ref[...])

    return pl.pallas_call(
        body, name="out_proj_fwd", grid=(S // tm,),
        in_specs=[_rows(tm, D), _rows(tm, CC), _rows(tm, ATT), _full((D, D)), _full((1, D))],
        out_specs=_rows(tm, D),
        out_shape=_sds((S, D), F32),
        compiler_params=_params(40),
    )(x, yc, ya, wout, b_out)


def _softmax(s):
    m = jnp.max(s, axis=-1, keepdims=True)
    e = jnp.exp(s - m)
    return e * (1.0 / jnp.sum(e, axis=-1, keepdims=True))


def _mem_attn_fwd(x1, g_q, wq, km, vm, wo, tm=512):
    S = x1.shape[0]
    M = km.shape[0]

    def body(x1_ref, g_ref, wq_ref, km_ref, vm_ref, wo_ref, x2_ref, qm_ref, om_ref):
        x1v = x1_ref[...]
        hq, _ = _rms(x1v, g_ref[...])
        qm = _nn(hq, wq_ref[...]).astype(qm_ref.dtype)
        qm_ref[...] = qm
        oms = []
        for h in range(MH):
            sl = slice(h * MHD, (h + 1) * MHD)
            p = _softmax(_nt(qm[:, sl], km_ref[:, sl]) * MSCALE)
            oms.append(_nn(p, vm_ref[:, sl]))
        om = jnp.concatenate(oms, axis=1).astype(om_ref.dtype)
        om_ref[...] = om
        x2_ref[...] = x1v + _nn(om, wo_ref[...])

    return pl.pallas_call(
        body, name="mem_attn_fwd", grid=(S // tm,),
        in_specs=[_rows(tm, D), _full((1, D)), _full((D, D)), _full((M, D)), _full((M, D)), _full((D, D))],
        out_specs=[_rows(tm, D)] * 3,
        out_shape=[_sds((S, D), F32), _sds((S, D), _MXU), _sds((S, D), _MXU)],
        compiler_params=_params(48),
    )(x1, g_q, wq, km, vm, wo)


def _ffn_loss_fwd(x2, tgt, g_ffn, g_final, wgT, wuT, wd, tm=256):
    S = x2.shape[0]

    def body(x2_ref, t_ref, gf_ref, gl_ref, wg_ref, wu_ref, wd_ref,
             gt_ref, up_ref, hf_ref, dx3_ref, dx3b_ref, loss_ref, dgl_ref):
        i = pl.program_id(0)
        _acc_init(i, loss_ref, dgl_ref)
        x2v = x2_ref[...]
        hf, _ = _rms(x2v, gf_ref[...])
        hb = hf.astype(hf_ref.dtype)
        hf_ref[...] = hb
        x3 = x2v
        for c in range(F // FCH):
            sl = slice(c * FCH, (c + 1) * FCH)
            gt = _nt(hb, wg_ref[sl, :])
            up = _nt(hb, wu_ref[sl, :])
            gt_ref[:, sl] = gt.astype(gt_ref.dtype)
            up_ref[:, sl] = up.astype(up_ref.dtype)
            x3 = x3 + _nn(gt * _sigmoid(gt) * up, wd_ref[sl, :])
        gl = gl_ref[...]
        y, r = _rms(x3, gl)
        diff = y - t_ref[...]
        _acc_row(loss_ref, jnp.sum(jnp.sum(diff * diff, axis=-1, keepdims=True), axis=0, keepdims=True) * (0.5 / D))
        dx3, dgl = _rms_bwd(diff * (1.0 / D), x3, r, gl)
        dx3_ref[...] = dx3
        dx3b_ref[...] = dx3.astype(dx3b_ref.dtype)
        _acc_row(dgl_ref, dgl)

    return pl.pallas_call(
        body, name="ffn_loss_fwd", grid=(S // tm,),
        in_specs=[_rows(tm, D), _rows(tm, D), _full((1, D)), _full((1, D)),
                  _once((F, D)), _once((F, D)), _once((F, D))],
        out_specs=[_rows(tm, F), _rows(tm, F), _rows(tm, D), _rows(tm, D), _rows(tm, D),
                   _full((8, LANES)), _full((8, D))],
        out_shape=[_sds((S, F), _MXU), _sds((S, F), _MXU), _sds((S, D), _MXU), _sds((S, D), F32),
                   _sds((S, D), _MXU), _sds((8, LANES), F32), _sds((8, D), F32)],
        compiler_params=_params(56),
    )(x2, tgt, g_ffn, g_final, wgT, wuT, wd)


def _ffn_bwd(dx3, x2, gt, up, g_ffn, wd, wgT, wuT, tm=256):
    S = x2.shape[0]

    def body(dx3_ref, x2_ref, gt_ref, up_ref, g_ref, wd_ref, wg_ref, wu_ref,
             dx2_ref, dx2b_ref, dgt_ref, dup_ref, act_ref, dg_ref):
        i = pl.program_id(0)
        _acc_init(i, dg_ref)
        dx3v = dx3_ref[...]
        db = dx3v.astype(_MXU)
        dhf = jnp.zeros((tm, D), F32)
        for c in range(F // FCH):
            sl = slice(c * FCH, (c + 1) * FCH)
            gtv = gt_ref[:, sl].astype(F32)
            upv = up_ref[:, sl].astype(F32)
            dact = _nt(db, wd_ref[sl, :])
            sig = _sigmoid(gtv)
            silu = gtv * sig
            dup = (dact * silu).astype(dup_ref.dtype)
            dgt = (dact * upv * (sig * (1.0 + gtv * (1.0 - sig)))).astype(dgt_ref.dtype)
            act_ref[:, sl] = (silu * upv).astype(act_ref.dtype)
            dgt_ref[:, sl] = dgt
            dup_ref[:, sl] = dup
            dhf = dhf + _nn(dgt, wg_ref[sl, :]) + _nn(dup, wu_ref[sl, :])
        x2v = x2_ref[...]
        g = g_ref[...]
        _, r = _rms(x2v, g)
        dxn, dg = _rms_bwd(dhf, x2v, r, g)
        dx2 = dx3v + dxn
        dx2_ref[...] = dx2
        dx2b_ref[...] = dx2.astype(dx2b_ref.dtype)
        _acc_row(dg_ref, dg)

    return pl.pallas_call(
        body, name="ffn_bwd", grid=(S // tm,),
        in_specs=[_rows(tm, D), _rows(tm, D), _rows(tm, F), _rows(tm, F), _full((1, D)),
                  _once((F, D)), _once((F, D)), _once((F, D))],
        out_specs=[_rows(tm, D), _rows(tm, D), _rows(tm, F), _rows(tm, F), _rows(tm, F), _full((8, D))],
        out_shape=[_sds((S, D), F32), _sds((S, D), _MXU), _sds((S, F), _MXU), _sds((S, F), _MXU),
                   _sds((S, F), _MXU), _sds((8, D), F32)],
        compiler_params=_params(56),
    )(dx3, x2, gt, up, g_ffn, wd, wgT, wuT)


_AFTER = pl.BlockSpec(memory_space=pl.ANY)


def _wgrad(a, b, name, after, rc=256):
    S, R = a.shape
    N = b.shape[1]

    def body(a_ref, b_ref, after_ref, o_ref, ob_ref):
        g = _tn(a_ref[...], b_ref[...])
        o_ref[...] = g
        ob_ref[...] = g.astype(ob_ref.dtype)

    return pl.pallas_call(
        body, name=name, grid=(R // rc,),
        in_specs=[pl.BlockSpec((S, rc), lambda j: (0, j)), _once((S, N)), _AFTER],
        out_specs=[_rows(rc, N), _rows(rc, N)],
        out_shape=[_sds((R, N), F32), _sds((R, N), jnp.bfloat16)],
        compiler_params=_params(48),
    )(a, b, after)


def _mem_attn_bwd(dx2, x1, qm, km, vm, g_q, wq, wo, wout, after, tm=512):
    S = x1.shape[0]
    M = km.shape[0]

    def body(dx2_ref, x1_ref, qm_ref, km_ref, vm_ref, g_ref, wq_ref, wo_ref, wout_ref, after_ref,
             dx1_ref, dx1b_ref, hq_ref, dqm_ref, dyc_ref, dya_ref, dkm_ref, dvm_ref, dg_ref, dbo_ref):
        i = pl.program_id(0)
        _acc_init(i, dkm_ref, dvm_ref, dg_ref, dbo_ref)
        dx2v = dx2_ref[...]
        dom = _nt(dx2v, wo_ref[...]).astype(_MXU)
        dqs = []
        for h in range(MH):
            sl = slice(h * MHD, (h + 1) * MHD)
            qh = qm_ref[:, sl]
            kh = km_ref[:, sl]
            p = _softmax(_nt(qh, kh) * MSCALE)
            dp = _nt(dom[:, sl], vm_ref[:, sl])
            ds = (p * (dp - jnp.sum(dp * p, axis=-1, keepdims=True))).astype(_MXU)
            dqs.append(_nn(ds, kh) * MSCALE)
            dkm_ref[:, sl] += _tn(ds, qh) * MSCALE
            dvm_ref[:, sl] += _tn(p, dom[:, sl])
        dqm = jnp.concatenate(dqs, axis=1).astype(dqm_ref.dtype)
        dqm_ref[...] = dqm
        dhq = _nt(dqm, wq_ref[...])
        x1v = x1_ref[...]
        g = g_ref[...]
        hq, r = _rms(x1v, g)
        hq_ref[...] = hq.astype(hq_ref.dtype)
        dxn, dg = _rms_bwd(dhq, x1v, r, g)
        dx1 = dx2v + dxn
        dx1_ref[...] = dx1
        dx1b = dx1.astype(dx1b_ref.dtype)
        dx1b_ref[...] = dx1b
        dym = _nt(dx1b, wout_ref[...])
        dyc_ref[...] = dym[:, :CC]
        dya_ref[...] = dym[:, CC:].astype(dya_ref.dtype)
        _acc_row(dg_ref, dg)
        _acc_row(dbo_ref, jnp.sum(dx1, axis=0, keepdims=True))

    return pl.pallas_call(
        body, name="mem_attn_bwd", grid=(S // tm,),
        in_specs=[_rows(tm, D), _rows(tm, D), _rows(tm, D), _full((M, D)), _full((M, D)), _full((1, D)),
                  _once((D, D)), _once((D, D)), _once((D, D)), _AFTER],
        out_specs=[_rows(tm, D), _rows(tm, D), _rows(tm, D), _rows(tm, D), _rows(tm, CC), _rows(tm, ATT),
                   _full((M, D)), _full((M, D)), _full((8, D)), _full((8, D))],
        out_shape=[_sds((S, D), F32), _sds((S, D), _MXU), _sds((S, D), _MXU), _sds((S, D), _MXU),
                   _sds((S, CC), F32), _sds((S, ATT), _MXU),
                   _sds((M, D), F32), _sds((M, D), F32), _sds((8, D), F32), _sds((8, D), F32)],
        compiler_params=_params(56),
    )(dx2, x1, qm, km, vm, g_q, wq, wo, wout, after)


def _mem_kv_bwd(dkm, dvm, mem, g_kv, memn, wkvT):
    M = mem.shape[0]

    def body(dkm_ref, dvm_ref, mem_ref, g_ref, mn_ref, w_ref, gw_ref, gwb_ref, dg_ref):
        dkv = jnp.concatenate([dkm_ref[...], dvm_ref[...]], axis=1).astype(_MXU)
        gw = _tn(dkv, mn_ref[...])
        gw_ref[...] = gw
        gwb_ref[...] = gw.astype(gwb_ref.dtype)
        dmn = _nn(dkv, w_ref[...])
        memv = mem_ref[...]
        g = g_ref[...]
        _, r = _rms(memv, g)
        _, dg = _rms_bwd(dmn, memv, r, g)
        dg_ref[...] = jnp.broadcast_to(dg, dg_ref.shape)

    return pl.pallas_call(
        body, name="mem_kv_bwd", grid=(1,),
        in_specs=[_full((M, D)), _full((M, D)), _full((M, D)), _full((1, D)), _full((M, D)), _full((2 * D, D))],
        out_specs=[_full((2 * D, D)), _full((2 * D, D)), _full((8, D))],
        out_shape=[_sds((2 * D, D), F32), _sds((2 * D, D), jnp.bfloat16), _sds((8, D), F32)],
        compiler_params=_params(48),
    )(dkm, dvm, mem, g_kv, memn, wkvT)


def _attn_bwd(qr, kr, vv, sink, dya, after):
    S = qr.shape[0]
    tq = ATTN_BLOCKS * BLK

    def body(q_ref, k_ref, v_ref, sk_ref, do_ref, after_ref, dq_ref, dk_ref, dv_ref, dsk_ref, sink_acc):
        _acc_init(pl.program_id(0), dk_ref, dv_ref, sink_acc)
        for b in range(ATTN_BLOCKS):
            i = pl.program_id(0) * ATTN_BLOCKS + b
            rows = slice(b * BLK, (b + 1) * BLK)
            start = _attn_window(i, S)
            kb = k_ref[pl.ds(start, 3 * BLK), :]
            vb = v_ref[pl.ds(start, 3 * BLK), :]
            q = q_ref[rows, :]
            do = do_ref[rows, :]
            mask = _attn_mask(i, start)
            dqs, dks, dvs = [], [], []
            for hk in range(NKV):
                hs = slice(hk * HD, (hk + 1) * HD)
                qs = _stack_heads(q, hk)
                dos = _stack_heads(do, hk)
                e, inv, es = _attn_probs(qs, kb[:, hs], sk_ref, hk, mask)
                p = e * inv
                dp = _nt(vb[:, hs], dos)
                delta = jnp.sum(dp * p, axis=0, keepdims=True)
                ds = (p * (dp - delta)).astype(_MXU)
                dqs.append(_unstack_heads_t(_tn(kb[:, hs], ds) * SCALE))
                dks.append(_nn(ds, qs) * SCALE)
                dvs.append(_nn(p, dos))
                sink_acc[hk:hk + 1, :] -= es * inv * delta
            dq_ref[rows, :] = jnp.concatenate(dqs, axis=1)
            dk_ref[pl.ds(start, 3 * BLK), :] += jnp.concatenate(dks, axis=1)
            dv_ref[pl.ds(start, 3 * BLK), :] += jnp.concatenate(dvs, axis=1)

        @pl.when(pl.program_id(0) == pl.num_programs(0) - 1)
        def _():
            for h in range(NQ):
                tot = jnp.sum(sink_acc[h // GRP:h // GRP + 1, (h % GRP) * BLK:(h % GRP + 1) * BLK], axis=1,
                              keepdims=True)
                dsk_ref[h:h + 1, :] = jnp.broadcast_to(tot, (1, LANES))

    return pl.pallas_call(
        body, name="attn_bwd", grid=(S // tq,),
        in_specs=[_rows(tq, ATT), _full((S, KVC)), _full((S, KVC)), _full((1, NQ)), _rows(tq, ATT), _AFTER],
        out_specs=[_rows(tq, ATT), _full((S, KVC)), _full((S, KVC)), _full((8, LANES))],
        out_shape=[_sds((S, ATT), F32), _sds((S, KVC), F32), _sds((S, KVC), F32), _sds((8, LANES), F32)],
        scratch_shapes=[pltpu.VMEM((8, GRP * BLK), F32)],
        compiler_params=_params(40),
    )(qr, kr, vv, sink, dya, after)


def _conv_norm_bwd(dyc, y, ln_g, ln_b, after, tm=512):
    S = y.shape[0]

    def body(dyc_ref, y_ref, g_ref, b_ref, after_ref, dy_ref, dg_ref, db_ref, dbdw_ref):
        i = pl.program_id(0)
        _acc_init(i, dg_ref, db_ref, dbdw_ref)
        g = g_ref[...]
        yh, rstd, z = _ln_silu(y_ref[...], g, b_ref[...])
        sig = _sigmoid(z)
        dz = dyc_ref[...] * (sig * (1.0 + z * (1.0 - sig)))
        dyh = dz * g
        dy = rstd * (dyh - jnp.mean(dyh, axis=-1, keepdims=True)
                     - yh * jnp.mean(dyh * yh, axis=-1, keepdims=True))
        dy_ref[...] = dy
        _acc_row(dg_ref, jnp.sum(dz * yh, axis=0, keepdims=True))
        _acc_row(db_ref, jnp.sum(dz, axis=0, keepdims=True))
        _acc_row(dbdw_ref, jnp.sum(dy, axis=0, keepdims=True))

    return pl.pallas_call(
        body, name="conv_norm_bwd", grid=(S // tm,),
        in_specs=[_rows(tm, CC), _rows(tm, CC), _full((1, CC)), _full((1, CC)), _AFTER],
        out_specs=[_rows(tm, CC), _full((8, CC)), _full((8, CC)), _full((8, CC))],
        out_shape=[_sds((S, CC), F32), _sds((8, CC), F32), _sds((8, CC), F32), _sds((8, CC), F32)],
        compiler_params=_params(40),
    )(dyc, y, ln_g, ln_b, after)


def _conv_bwd(dy, vg, uglu, w_dw, tm=128):
    S = dy.shape[0]

    def body(dy_hbm, vg_hbm, uglu_ref, w_ref, du_ref, gw_ref, dbin_ref, dyp_ref, vgp_ref, gacc_ref, sems):
        i = pl.program_id(0)
        _acc_init(i, gacc_ref, dbin_ref)
        _fill_padded(i, dy_hbm, dyp_ref, sems.at[0], S)
        _fill_padded(i, vg_hbm, vgp_ref, sems.at[1], S)
        t0 = pl.multiple_of(i * tm, tm)
        dvg = _dwconv_tile(dyp_ref, t0, tm, w_ref, True)
        for cl in range(CC // LANES):
            lanes = slice(cl * LANES, (cl + 1) * LANES)
            dyt = dyp_ref[pl.ds(t0 + CONV_HALO, tm), lanes]
            blk = vgp_ref[pl.ds(t0, tm + 2 * CONV_HALO), lanes]
            for b in range(8):
                sh = blk[b:b + tm + 24, :]
                for a in range(4):
                    k = 8 * a + b - 1
                    if 0 <= k < TAPS:
                        prod = dyt * sh[8 * a:8 * a + tm, :]
                        gacc_ref[k, :, lanes] += jnp.sum(prod.reshape(tm // 8, 8, LANES), axis=0)
        a_v = uglu_ref[:, :CC]
        sg = _sigmoid(uglu_ref[:, CC:])
        da = dvg * sg
        dgate = dvg * a_v * sg * (1.0 - sg)
        du = jnp.concatenate([da, dgate], axis=1)
        du_ref[...] = du.astype(du_ref.dtype)
        _acc_row(dbin_ref, jnp.sum(du, axis=0, keepdims=True))

        @pl.when(i == pl.num_programs(0) - 1)
        def _():
            gw_ref[...] = jnp.zeros(gw_ref.shape, F32)
            for k in range(TAPS):
                gw_ref[k:k + 1, :] = jnp.sum(gacc_ref[k], axis=0, keepdims=True)

    return pl.pallas_call(
        body, name="conv_bwd", grid=(S // tm,),
        in_specs=[pl.BlockSpec(memory_space=pl.ANY), pl.BlockSpec(memory_space=pl.ANY), _rows(tm, 2 * CC),
                  _full((TAPS, CC))],
        out_specs=[_rows(tm, 2 * CC), _full((TAPS + 1, CC)), _full((8, 2 * CC))],
        out_shape=[_sds((S, 2 * CC), _MXU), _sds((TAPS + 1, CC), F32), _sds((8, 2 * CC), F32)],
        scratch_shapes=[pltpu.VMEM((S + 2 * CONV_HALO, CC), F32), pltpu.VMEM((S + 2 * CONV_HALO, CC), F32),
                        pltpu.VMEM((TAPS, 8, CC), F32), pltpu.SemaphoreType.DMA((2,))],
        compiler_params=_params(48),
    )(dy, vg, uglu, w_dw)


def _in_proj_bwd(duglu, dqr, dkr, dv, cosq, sinq, x, dx1, g_mix, winT, tm=512):
    S = x.shape[0]
    NA = ATT + 2 * KVC

    def body(dug_ref, dq_ref, dk_ref, dv_ref, c_ref, s_ref, x_ref, dx1_ref, g_ref, w_ref,
             dx_ref, du_ref, dg_ref, dbin_ref):
        i = pl.program_id(0)
        _acc_init(i, dg_ref, dbin_ref)
        c = c_ref[...]
        s = s_ref[...]
        datt = jnp.concatenate([_rope_bwd(dq_ref[...], c, s), _rope_bwd(dk_ref[...], c, s), dv_ref[...]], axis=1)
        _acc_row(dbin_ref, jnp.sum(datt, axis=0, keepdims=True))
        du = jnp.concatenate([dug_ref[...], datt.astype(du_ref.dtype)], axis=1)
        du_ref[...] = du
        dh = _nn(du, w_ref[...])
        xv = x_ref[...]
        g = g_ref[...]
        _, r = _rms(xv, g)
        dxn, dg = _rms_bwd(dh, xv, r, g)
        dx_ref[...] = dx1_ref[...] + dxn
        _acc_row(dg_ref, dg)

    return pl.pallas_call(
        body, name="in_proj_bwd", grid=(S // tm,),
        in_specs=[_rows(tm, 2 * CC), _rows(tm, ATT), _rows(tm, KVC), _rows(tm, KVC), _rows(tm, LANES),
                  _rows(tm, LANES), _rows(tm, D), _rows(tm, D), _full((1, D)), _once((INC, D))],
        out_specs=[_rows(tm, D), _rows(tm, INC), _full((8, D)), _full((8, NA))],
        out_shape=[_sds((S, D), F32), _sds((S, INC), _MXU), _sds((8, D), F32), _sds((8, NA), F32)],
        compiler_params=_params(48),
    )(duglu, dqr, dkr, dv, cosq, sinq, x, dx1, g_mix, winT)


def _rope_tables(S):
    pos = jnp.arange(S, dtype=F32)
    inv_freq = ROPE_THETA ** (-jnp.arange(0, HD, 2, dtype=F32) / HD)
    ang = pos[:, None] * inv_freq[None, :]
    cos, sin = jnp.cos(ang), jnp.sin(ang)
    cosq = jnp.tile(jnp.concatenate([cos, cos], axis=1), (1, LANES // HD))
    sinq = jnp.tile(jnp.concatenate([-sin, sin], axis=1), (1, LANES // HD))
    return cosq, sinq


GATHER_GROUPS = (("w_inT", "w_dw"), ("w_out", "w_mem_kvT", "w_mem_q", "w_mem_o"), ("w_gateT", "w_upT", "w_down"))


def _local_step(x, mem, tgt, sp, gather, reduce):
    S = x.shape[0]
    cosq, sinq = _rope_tables(S)

    gather.forward(0, x)
    w0 = gather.finish(0, x)
    winT, w_dw_full = w0["w_inT"], w0["w_dw"]
    h, uglu, vg, qr, kr, vv = _in_proj_fwd(x, sp["g_mix"], winT, sp["b_in"], cosq, sinq)
    y, yc = _conv_fwd(vg, w_dw_full, sp["b_dw"], sp["g_conv_ln"], sp["b_conv_ln"])
    gather.forward(1, yc)
    ya = _attn_fwd(qr, kr, vv, sp["attn_sink"])
    w1 = gather.finish(1, ya)
    gather.forward(2, ya)
    wout, wkvT, wq, wo = w1["w_out"], w1["w_mem_kvT"], w1["w_mem_q"], w1["w_mem_o"]
    km, vm, memn = _mem_kv_fwd(mem, sp["g_mem_kv"], wkvT)
    x1 = _out_proj_fwd(x, yc, ya, wout, sp["b_out"])
    x2, qm, om = _mem_attn_fwd(x1, sp["g_mem_q"], wq, km, vm, wo)
    w2 = gather.finish(2, om)
    wgT, wuT, wd = w2["w_gateT"], w2["w_upT"], w2["w_down"]
    gt, up, hf, dx3, dx3b, loss_acc, dg_final = _ffn_loss_fwd(x2, tgt, sp["g_ffn"], sp["g_final"], wgT, wuT, wd)

    dx2, dx2b, dgt, dup, act, dg_ffn = _ffn_bwd(dx3, x2, gt, up, sp["g_ffn"], wd, wgT, wuT)
    token = reduce.start(0, [_wgrad(dgt, hf, "wgrad_gate", dx2), _wgrad(dup, hf, "wgrad_up", dx2),
                             _wgrad(act, dx3b, "wgrad_down", dx2)])

    dx1, dx1b, hq, dqm, dyc, dya, dkm, dvm, dg_mem_q, db_out = _mem_attn_bwd(
        dx2, x1, qm, km, vm, sp["g_mem_q"], wq, wo, wout, token)
    token = reduce.middle(0, dx1)
    g_wo = _wgrad(om, dx2b, "wgrad_mem_o", token)
    g_wq = _wgrad(hq, dqm, "wgrad_mem_q", token)
    g_wkvT_f, g_wkvT_b, dg_mem_kv = _mem_kv_bwd(dkm, dvm, mem, sp["g_mem_kv"], memn, wkvT)
    g_wout_c = _wgrad(yc, dx1b, "wgrad_out_conv", token)
    g_wout_a = _wgrad(ya, dx1b, "wgrad_out_attn", token)
    g_wout = tuple(jnp.concatenate([c, a], axis=0) for c, a in zip(g_wout_c, g_wout_a))
    token = reduce.start(1, [g_wout, g_wq, (g_wkvT_f, g_wkvT_b), g_wo])

    dqr, dkr, dvv, dsink = _attn_bwd(qr, kr, vv, sp["attn_sink"], dya, token)
    token = reduce.middle(1, dqr)
    dy, dg_cln, db_cln, db_dw = _conv_norm_bwd(dyc, y, sp["g_conv_ln"], sp["b_conv_ln"], token)
    duglu, g_wdw, dbin_glu = _conv_bwd(dy, vg, uglu, w_dw_full)
    reduce.finish(0, duglu)
    grad_x, du, dg_mix, dbin_att = _in_proj_bwd(duglu, dqr, dkr, dvv, cosq, sinq, x, dx1, sp["g_mix"], winT)
    token = reduce.start(2, [_wgrad(du, h, "wgrad_in", grad_x)])

    small = {
        "loss": loss_acc[0:1, :],
        "g_mix": dg_mix[0:1], "b_in": jnp.concatenate([dbin_glu[0:1], dbin_att[0:1]], axis=1),
        "b_dw": db_dw[0:1], "g_conv_ln": dg_cln[0:1], "b_conv_ln": db_cln[0:1],
        "attn_sink": dsink[:, 0].reshape(1, NQ), "b_out": db_out[0:1], "g_mem_q": dg_mem_q[0:1],
        "g_mem_kv": dg_mem_kv[0:1], "g_ffn": dg_ffn[0:1], "g_final": dg_final[0:1],
    }
    reduce.middle(2, token, reduce.small_zones(small, g_wdw))
    reduce.finish(1, grad_x)
    reduce.finish(2, grad_x)
    return grad_x


def _place():
    return lax.axis_index("x"), lax.axis_index("y"), lax.axis_index("c")


def _hbm_specs(n):
    return [pl.BlockSpec(memory_space=pl.ANY)] * n


_HBM = pl.BlockSpec(memory_space=pltpu.HBM)
_SEM = pl.BlockSpec(memory_space=pltpu.SEMAPHORE)
_ANY = pl.BlockSpec(memory_space=pl.ANY)
_EFFECT = pltpu.SideEffectType.DATAFLOW_SIDE_EFFECTING
_VMEM = pl.BlockSpec(memory_space=pltpu.VMEM)
_TOKEN = jax.ShapeDtypeStruct((8, LANES), F32)


def _dev_index(p):
    return 4 * p[0] + 2 * p[1] + p[2]


def _gather_start(groups):
    sizes = [len(g) for g in groups]
    flat = [z for g in groups for z in g]
    n = len(flat)

    def body(*refs):
        ins = refs[:n]
        sems = refs[2 * n:]
        x, y, c = _place()
        me = _dev_index((x, y, c))
        peers = [(x, y, 1 - c), (1 - x, y, c), (x, 1 - y, c), (1 - x, 1 - y, c)]
        base = 0
        for gi, size in enumerate(sizes):
            send_sems, recv_sems = sems[2 * gi], sems[2 * gi + 1]
            for k in range(size):
                zone = ins[base + k]
                for j, peer in enumerate(peers):
                    pltpu.make_async_remote_copy(
                        src_ref=zone.at[me], dst_ref=zone.at[me], send_sem=send_sems.at[4 * k + j],
                        recv_sem=recv_sems.at[4 * k + j], device_id=peer, device_id_type=MESH).start()
            base += size

    out_shape = [pltpu.HBM(z.shape, z.dtype) for z in flat]
    out_specs = [_HBM] * n
    for size in sizes:
        out_shape += [pltpu.SemaphoreType.DMA((4 * size,))] * 2
        out_specs += [_SEM, _SEM]
    res = pl.pallas_call(
        body, name="gather_start", in_specs=[_HBM] * n, out_specs=out_specs, out_shape=out_shape,
        input_output_aliases={k: k for k in range(n)},
        compiler_params=pltpu.CompilerParams(has_side_effects=_EFFECT),
    )(*[pltpu.with_memory_space_constraint(z, pltpu.HBM) for z in flat])
    out, base = [], 0
    for gi, size in enumerate(sizes):
        out.append((list(res[base:base + size]), res[n + 2 * gi], res[n + 2 * gi + 1]))
        base += size
    return out


def _gather_forward(zones, send_sems, recv_sems, after, collective_id, name):
    n = len(zones)

    def body(*refs):
        ins = refs[:n]
        send_ref, recv_ref = refs[n], refs[n + 1]
        fsend, frecv = refs[2 * n + 3], refs[2 * n + 4]
        x, y, c = _place()
        sibling = (x, y, 1 - c)
        barrier = pltpu.get_barrier_semaphore()
        pl.semaphore_signal(barrier, inc=1, device_id=sibling, device_id_type=MESH)
        pl.semaphore_wait(barrier, 1)
        chips = [(1 - x, y), (x, 1 - y), (1 - x, 1 - y)]
        for j, chip in enumerate(chips):
            for k in range(n):
                slot = ins[k].at[_dev_index((*chip, c))]
                pltpu.make_async_remote_copy(
                    src_ref=slot, dst_ref=slot, send_sem=send_ref.at[4 * k + 1 + j],
                    recv_sem=recv_ref.at[4 * k + 1 + j], device_id=sibling, device_id_type=MESH).wait_recv()
                pltpu.make_async_remote_copy(
                    src_ref=slot, dst_ref=slot, send_sem=fsend.at[3 * k + j], recv_sem=frecv.at[3 * k + j],
                    device_id=sibling, device_id_type=MESH).start()
        for k in range(n):
            own = ins[k].at[_dev_index((x, y, c))]
            theirs = ins[k].at[_dev_index(sibling)]
            pltpu.make_async_remote_copy(
                src_ref=theirs, dst_ref=theirs, send_sem=send_ref.at[4 * k], recv_sem=recv_ref.at[4 * k],
                device_id=sibling, device_id_type=MESH).wait_recv()
            for j in range(4):
                pltpu.make_async_remote_copy(
                    src_ref=own, dst_ref=own, send_sem=send_ref.at[4 * k + j], recv_sem=recv_ref.at[4 * k + j],
                    device_id=sibling, device_id_type=MESH).wait_send()

    res = pl.pallas_call(
        body, name=name, in_specs=[_HBM] * n + [_SEM, _SEM, _ANY],
        out_specs=[_HBM] * n + [_SEM, _SEM],
        out_shape=[pltpu.HBM(z.shape, z.dtype) for z in zones] + [pltpu.SemaphoreType.DMA((3 * n,))] * 2,
        input_output_aliases={k: k for k in range(n)},
        compiler_params=pltpu.CompilerParams(has_side_effects=_EFFECT, collective_id=collective_id),
    )(*zones, send_sems, recv_sems, after)
    return list(res[:n]), res[n], res[n + 1]


def _gather_finish(zones, fsend, frecv, after, name):
    n = len(zones)

    def body(*refs):
        ins = refs[:n]
        fsend_ref, frecv_ref = refs[n], refs[n + 1]
        x, y, c = _place()
        sibling = (x, y, 1 - c)
        chips = [(1 - x, y), (x, 1 - y), (1 - x, 1 - y)]
        for j, chip in enumerate(chips):
            for k in range(n):
                mine = ins[k].at[_dev_index((*chip, c))]
                theirs = ins[k].at[_dev_index((*chip, 1 - c))]
                pltpu.make_async_remote_copy(
                    src_ref=theirs, dst_ref=theirs, send_sem=fsend_ref.at[3 * k + j],
                    recv_sem=frecv_ref.at[3 * k + j], device_id=sibling, device_id_type=MESH).wait_recv()
                pltpu.make_async_remote_copy(
                    src_ref=mine, dst_ref=mine, send_sem=fsend_ref.at[3 * k + j],
                    recv_sem=frecv_ref.at[3 * k + j], device_id=sibling, device_id_type=MESH).wait_send()

    res = pl.pallas_call(
        body, name=name, in_specs=[_HBM] * n + [_SEM, _SEM, _ANY], out_specs=[_HBM] * n,
        out_shape=[pltpu.HBM(z.shape, z.dtype) for z in zones],
        input_output_aliases={k: k for k in range(n)},
        compiler_params=pltpu.CompilerParams(has_side_effects=_EFFECT),
    )(*zones, fsend, frecv, after)
    return list(res)


class _Gather:
    def __init__(self, started):
        self._phase = list(started)

    def forward(self, gi, after):
        zones, send_sems, recv_sems = self._phase[gi]
        self._phase[gi] = _gather_forward(zones, send_sems, recv_sems, after, gi + 1, "gather_forward_%d" % gi)

    def finish(self, gi, after):
        zones, fsend, frecv = self._phase[gi]
        out = {}
        for name, z in zip(GATHER_GROUPS[gi], _gather_finish(zones, fsend, frecv, after, "gather_finish_%d" % gi)):
            if name == "w_dw":
                out[name] = jnp.transpose(z, (1, 0, 2)).reshape(TAPS + 1, CC)[:TAPS]
            else:
                out[name] = z.reshape(NDEV * z.shape[1], D)
        return out


def _handshake(peers):
    barrier = pltpu.get_barrier_semaphore()
    for peer in peers:
        pl.semaphore_signal(barrier, inc=1, device_id=peer, device_id_type=MESH)
    pl.semaphore_wait(barrier, len(peers))


def _sibling_copies(srcs, lands, send_sems, recv_sems):
    x, y, c = _place()
    return [pltpu.make_async_remote_copy(
        src_ref=srcs[k].at[2 * s + (1 - c)], dst_ref=lands[k].at[s], send_sem=send_sems.at[4 * k + s],
        recv_sem=recv_sems.at[4 * k + s], device_id=(x, y, 1 - c), device_id_type=MESH)
        for k in range(len(srcs)) for s in range(4)]


def _sibling_start(gbs, collective_id, name):
    n = len(gbs)
    lands = [lax.empty((4,) + g.shape[1:], g.dtype) for g in gbs]

    def body(*refs):
        x, y, c = _place()
        _handshake([(x, y, 1 - c)])
        for cp in _sibling_copies(refs[:n], refs[n:2 * n], refs[4 * n], refs[4 * n + 1]):
            cp.start()
        refs[4 * n + 2][...] = jnp.zeros(_TOKEN.shape, _TOKEN.dtype)

    res = pl.pallas_call(
        body, name=name, in_specs=[_HBM] * (2 * n), out_specs=[_HBM] * (2 * n) + [_SEM, _SEM, _VMEM],
        out_shape=[pltpu.HBM(a.shape, a.dtype) for a in gbs + lands] + [pltpu.SemaphoreType.DMA((4 * n,))] * 2
        + [_TOKEN],
        input_output_aliases={k: k for k in range(2 * n)},
        compiler_params=pltpu.CompilerParams(has_side_effects=_EFFECT, collective_id=collective_id),
    )(*[pltpu.with_memory_space_constraint(a, pltpu.HBM) for a in gbs + lands])
    return list(res[:n]), list(res[n:2 * n]), res[2 * n], res[2 * n + 1], res[2 * n + 2]


def _sibling_wait(gbs, lands, send_sems, recv_sems, after, name):
    n = len(gbs)

    def body(*refs):
        for cp in _sibling_copies(refs[:n], refs[n:2 * n], refs[2 * n], refs[2 * n + 1]):
            cp.wait_recv()
            cp.wait_send()

    res = pl.pallas_call(
        body, name=name, in_specs=[_HBM] * (2 * n) + [_SEM, _SEM, _ANY], out_specs=[_HBM] * (2 * n),
        out_shape=[pltpu.HBM(a.shape, a.dtype) for a in gbs + lands],
        input_output_aliases={k: k for k in range(2 * n)},
        compiler_params=pltpu.CompilerParams(has_side_effects=_EFFECT),
    )(*gbs, *lands, send_sems, recv_sems, after)
    return list(res[n:])


def _pair_sum(gfs, ts, c_idx, name):
    n = len(gfs)
    halves = 2

    def body(c_ref, *refs):
        for k in range(n):
            refs[2 * n + k][...] = (refs[k][...] + refs[n + k][...].astype(F32)).astype(jnp.bfloat16)

    in_specs, out_specs, out_shape = [], [], []
    for g in gfs:
        rh = g.shape[1] // halves
        in_specs.append(pl.BlockSpec((None, rh, D), lambda s, h, c_ref: (2 * s + c_ref[0], h, 0)))
    for g in gfs:
        rh = g.shape[1] // halves
        in_specs.append(pl.BlockSpec((None, rh, D), lambda s, h, c_ref: (s, h, 0)))
        out_specs.append(pl.BlockSpec((None, rh, D), lambda s, h, c_ref: (s, h, 0)))
        out_shape.append(_sds((4, g.shape[1], D), jnp.bfloat16))
    return pl.pallas_call(
        body, name=name,
        grid_spec=pltpu.PrefetchScalarGridSpec(num_scalar_prefetch=1, grid=(4, halves), in_specs=in_specs,
                                               out_specs=out_specs),
        out_shape=out_shape,
        compiler_params=pltpu.CompilerParams(dimension_semantics=("arbitrary", "arbitrary"),
                                             vmem_limit_bytes=48 * 2 ** 20),
    )(c_idx, *gfs, *ts)


def _chip_copies(pbs, lands, smalls, send_sems, recv_sems):
    x, y, c = _place()
    own = 2 * x + y
    me = _dev_index((x, y, c))
    chips = [(1 - x, y), (x, 1 - y), (1 - x, 1 - y)]
    n = len(pbs)
    copies = [pltpu.make_async_remote_copy(
        src_ref=pbs[k].at[2 * chip[0] + chip[1]], dst_ref=lands[k].at[own], send_sem=send_sems.at[3 * k + j],
        recv_sem=recv_sems.at[3 * k + j], device_id=(*chip, c), device_id_type=MESH)
        for k in range(n) for j, chip in enumerate(chips)]
    peers = [(x, y, 1 - c)] + [(*chip, cc) for chip in chips for cc in (c, 1 - c)]
    copies += [pltpu.make_async_remote_copy(
        src_ref=smalls[t].at[me], dst_ref=smalls[t].at[me], send_sem=send_sems.at[3 * n + 7 * t + j],
        recv_sem=recv_sems.at[3 * n + 7 * t + j], device_id=peer, device_id_type=MESH)
        for t in range(len(smalls)) for j, peer in enumerate(peers)]
    return copies


def _chip_start(pbs, smalls, collective_id, name):
    n, m = len(pbs), len(smalls)
    lands = [lax.empty(p.shape, p.dtype) for p in pbs]
    arrays = list(pbs) + lands + list(smalls)
    na = len(arrays)
    nsem = 3 * n + 7 * m

    def body(*refs):
        x, y, c = _place()
        chips = [(1 - x, y), (x, 1 - y), (1 - x, 1 - y)]
        peers = [(*chip, c) for chip in chips]
        if m:
            peers += [(x, y, 1 - c)] + [(*chip, 1 - c) for chip in chips]
        _handshake(peers)
        for cp in _chip_copies(refs[:n], refs[n:2 * n], refs[2 * n:na], refs[2 * na], refs[2 * na + 1]):
            cp.start()
        refs[2 * na + 2][...] = jnp.zeros(_TOKEN.shape, _TOKEN.dtype)

    res = pl.pallas_call(
        body, name=name, in_specs=[_HBM] * na, out_specs=[_HBM] * na + [_SEM, _SEM, _VMEM],
        out_shape=[pltpu.HBM(a.shape, a.dtype) for a in arrays] + [pltpu.SemaphoreType.DMA((nsem,))] * 2 + [_TOKEN],
        input_output_aliases={k: k for k in range(na)},
        compiler_params=pltpu.CompilerParams(has_side_effects=_EFFECT, collective_id=collective_id),
    )(*[pltpu.with_memory_space_constraint(a, pltpu.HBM) for a in arrays])
    return (list(res[:n]), list(res[n:2 * n]), list(res[2 * n:na]), res[na], res[na + 1]), res[na + 2]


def _chip_wait(pbs, lands, smalls, send_sems, recv_sems, after, name):
    n = len(pbs)
    arrays = list(pbs) + list(lands) + list(smalls)
    na = len(arrays)

    def body(*refs):
        for cp in _chip_copies(refs[:n], refs[n:2 * n], refs[2 * n:na], refs[na], refs[na + 1]):
            cp.wait_recv()
            cp.wait_send()

    res = pl.pallas_call(
        body, name=name, in_specs=[_HBM] * na + [_SEM, _SEM, _ANY], out_specs=[_HBM] * na,
        out_shape=[pltpu.HBM(a.shape, a.dtype) for a in arrays],
        input_output_aliases={k: k for k in range(na)},
        compiler_params=pltpu.CompilerParams(has_side_effects=_EFFECT),
    )(*arrays, send_sems, recv_sems, after)
    return list(res[:n]), list(res[n:2 * n]), list(res[2 * n:])


def _chip_sum(pbs, qs, own_idx, name):
    n = len(pbs)
    halves = 2

    def body(o_ref, *refs):
        for k in range(n):
            acc = refs[4 * k][...].astype(F32)
            for j in range(1, 4):
                acc = acc + refs[4 * k + j][...].astype(F32)
            refs[4 * n + k][...] = acc

    def other(j):
        return lambda h, o: ((o[0] + j) % 4, h, 0)

    in_specs, ins = [], []
    for p, q in zip(pbs, qs):
        rh = p.shape[1] // halves
        in_specs.append(pl.BlockSpec((None, rh, D), lambda h, o: (o[0], h, 0)))
        ins.append(p)
        for j in range(1, 4):
            in_specs.append(pl.BlockSpec((None, rh, D), other(j)))
            ins.append(q)
    out_specs = [pl.BlockSpec((p.shape[1] // halves, D), lambda h, o: (h, 0)) for p in pbs]
    return pl.pallas_call(
        body, name=name,
        grid_spec=pltpu.PrefetchScalarGridSpec(num_scalar_prefetch=1, grid=(halves,), in_specs=in_specs,
                                               out_specs=out_specs),
        out_shape=[_sds((p.shape[1], D), F32) for p in pbs],
        compiler_params=pltpu.CompilerParams(dimension_semantics=("arbitrary",), vmem_limit_bytes=48 * 2 ** 20),
    )(own_idx, *ins)


REDUCE_GROUPS = (("w_gateT", "w_upT", "w_down"), ("w_out", "w_mem_q", "w_mem_kvT", "w_mem_o"), ("w_inT",))


class _Reduce:
    def __init__(self):
        x, y, c = _place()
        self._c = c.astype(jnp.int32).reshape(1)
        self._own = (2 * x + y).astype(jnp.int32).reshape(1)
        self._me = _dev_index((x, y, c))
        self._state = {}
        self.shards = {}
        self.smalls = None

    def start(self, gi, wgrads):
        gfs = [g[0].reshape(NDEV, g[0].shape[0] // NDEV, D) for g in wgrads]
        gbs = [g[1].reshape(NDEV, g[1].shape[0] // NDEV, D) for g in wgrads]
        gbs, lands, send_sems, recv_sems, token = _sibling_start(gbs, 4 + gi, "reduce_sibling_start_%d" % gi)
        self._state[gi] = (gfs, gbs, lands, send_sems, recv_sems)
        return token

    def middle(self, gi, after, smalls=()):
        gfs, gbs, lands, send_sems, recv_sems = self._state[gi]
        ts = _sibling_wait(gbs, lands, send_sems, recv_sems, after, "reduce_sibling_wait_%d" % gi)
        pbs = _pair_sum(gfs, ts, self._c, "reduce_pair_sum_%d" % gi)
        self._state[gi], token = _chip_start(pbs, list(smalls), 7 + gi, "reduce_chip_start_%d" % gi)
        return token

    def finish(self, gi, after):
        pbs, lands, smalls, send_sems, recv_sems = self._state[gi]
        pbs, qs, smalls = _chip_wait(pbs, lands, smalls, send_sems, recv_sems, after, "reduce_chip_wait_%d" % gi)
        self.shards.update(zip(REDUCE_GROUPS[gi], _chip_sum(pbs, qs, self._own, "reduce_chip_sum_%d" % gi)))
        if smalls:
            self.smalls = smalls

    def small_zones(self, small, g_wdw):
        pack = jnp.concatenate(
            [jnp.pad(small[name], ((0, 0), (0, lanes - small[name].shape[1]))) for name, lanes, _ in PACK], axis=1)
        zones = []
        for a in (pack, g_wdw):
            zone = lax.empty((NDEV,) + a.shape, a.dtype)
            zones.append(lax.dynamic_update_slice(zone, a[None], (self._me,) + (0,) * a.ndim))
        return zones


def _adamw_math(w, g, m, v):
    m = ADAM_B1 * m + (1.0 - ADAM_B1) * g
    v = ADAM_B2 * v + (1.0 - ADAM_B2) * (g * g)
    m_hat = m / (1.0 - ADAM_B1 ** ADAM_STEP)
    v_hat = v / (1.0 - ADAM_B2 ** ADAM_STEP)
    delta = -ADAM_LR * (m_hat / (jnp.sqrt(v_hat) + ADAM_EPS) + ADAM_WD * w)
    return delta, m, v


def _adamw(w, g, m, v, name):
    R, C = w.shape
    rc = 256 if R % 256 == 0 and R > 256 else R

    def body(w_ref, g_ref, m_ref, v_ref, d_ref, nm_ref, nv_ref):
        d, nm, nv = _adamw_math(w_ref[...], g_ref[...], m_ref[...], v_ref[...])
        d_ref[...] = d
        nm_ref[...] = nm
        nv_ref[...] = nv

    return pl.pallas_call(
        body, name=name, grid=(R // rc,),
        in_specs=[_rows(rc, C)] * 4, out_specs=[_rows(rc, C)] * 3,
        out_shape=[_sds((R, C), F32)] * 3,
        compiler_params=_params(40),
    )(w, g, m, v)


PACK = (("loss", LANES, 1), ("g_mix", D, D), ("b_in", INC, INC), ("b_dw", CC, CC), ("g_conv_ln", CC, CC),
        ("b_conv_ln", CC, CC), ("attn_sink", LANES, NQ), ("b_out", D, D), ("g_mem_q", D, D), ("g_mem_kv", D, D),
        ("g_ffn", D, D), ("g_final", D, D))
PACK_LANES = sum(p[1] for p in PACK)


def _small_update(packs, gdws, ws, ms, vs):
    names = [p[0] for p in PACK[1:]]
    n = len(names)

    def body(*refs):
        packs_ref, gdws_ref = refs[0], refs[1]
        w_refs, m_refs, v_refs = refs[2:2 + n], refs[2 + n:2 + 2 * n], refs[2 + 2 * n:2 + 3 * n]
        outs = refs[2 + 3 * n:]
        tot = packs_ref[0]
        gdw = gdws_ref[0]
        for s in range(1, NDEV):
            tot = tot + packs_ref[s]
            gdw = gdw + gdws_ref[s]
        outs[0][...] = tot[:, 0:LANES]
        outs[1][...] = gdw
        off = LANES
        for k, (_, lanes, used) in enumerate(PACK[1:]):
            g = tot[:, off:off + used]
            off += lanes
            d, nm, nv = _adamw_math(w_refs[k][...], g, m_refs[k][...], v_refs[k][...])
            o = outs[2 + 4 * k:6 + 4 * k]
            o[0][...] = g
            o[1][...] = d
            o[2][...] = nm
            o[3][...] = nv

    def full(a):
        nd = a.ndim
        return pl.BlockSpec(a.shape, lambda i: (0,) * nd)

    ins = [packs, gdws, *ws, *ms, *vs]
    out_shape = [_sds((1, LANES), F32), _sds(gdws.shape[1:], F32)]
    for w in ws:
        out_shape += [_sds(w.shape, F32)] * 4
    return pl.pallas_call(
        body, name="small_update", grid=(1,),
        in_specs=[full(a) for a in ins], out_specs=[full(o) for o in out_shape], out_shape=out_shape,
        compiler_params=_params(40),
    )(*ins)


def kernel(x, mem, g_mix, w_in, b_in, w_dw, b_dw, g_conv_ln, b_conv_ln, attn_sink, w_out, b_out, g_mem_q, g_mem_kv, w_mem_q, w_mem_kv, w_mem_o, g_ffn, w_gate, w_up, w_down, g_final, loss_target, m_g_mix, m_w_in, m_b_in, m_w_dw, m_b_dw, m_g_conv_ln, m_b_conv_ln, m_attn_sink, m_w_out, m_b_out, m_g_mem_q, m_g_mem_kv, m_w_mem_q, m_w_mem_kv, m_w_mem_o, m_g_ffn, m_w_gate, m_w_up, m_w_down, m_g_final, v_g_mix, v_w_in, v_b_in, v_w_dw, v_b_dw, v_g_conv_ln, v_b_conv_ln, v_attn_sink, v_w_out, v_b_out, v_g_mem_q, v_g_mem_kv, v_w_mem_q, v_w_mem_kv, v_w_mem_o, v_g_ffn, v_w_gate, v_w_up, v_w_down, v_g_final):
    given = dict(locals())
    bf = jnp.bfloat16
    xi, yi, ci = _place()
    me = 4 * xi + 2 * yi + ci

    transposed = {"w_inT": w_in, "w_mem_kvT": w_mem_kv, "w_gateT": w_gate, "w_upT": w_up}
    plain = {"w_out": w_out, "w_mem_q": w_mem_q, "w_mem_o": w_mem_o, "w_down": w_down}
    blocks = {n: (transposed[n][0].T if n in transposed else plain[n][0]).astype(bf) for n in W_NAMES}
    blocks["w_dw"] = jnp.pad(w_dw[0], ((0, 1), (0, 0)))

    def landing_zone(b):
        zone = lax.empty((NDEV,) + b.shape, b.dtype)
        return lax.dynamic_update_slice(zone, b[None], (me,) + (0,) * b.ndim)

    gather = _Gather(_gather_start([[landing_zone(blocks[n]) for n in names] for names in GATHER_GROUPS]))

    sp = {n: given[n] for n in ("g_mix", "b_in", "b_dw", "g_conv_ln", "b_conv_ln", "attn_sink", "b_out", "g_mem_q",
                                "g_mem_kv", "g_ffn")}
    sp["g_final"] = g_final.reshape(1, D)
    reduce = _Reduce()
    grad_x = _local_step(x[0], mem[0], loss_target[0], sp, gather, reduce)
    gsh = reduce.shards
    packs, gdws = reduce.smalls

    grads, deltas, new_m, new_v = {}, {}, {}, {}
    names_small = [p[0] for p in PACK[1:]]

    def as_row(a):
        return a.reshape(1, -1)

    res = _small_update(packs, gdws, [as_row(given[n]) for n in names_small],
                        [as_row(given["m_" + n]) for n in names_small],
                        [as_row(given["v_" + n]) for n in names_small])
    loss = res[0][0, 0]
    gdw_sum = res[1]
    for k, n in enumerate(names_small):
        shape = given[n].shape
        g, d, nm, nv = res[2 + 4 * k:6 + 4 * k]
        grads[n], deltas[n], new_m[n], new_v[n] = (g.reshape(shape), d.reshape(shape), nm.reshape(shape),
                                                   nv.reshape(shape))

    big = {"w_out": gsh["w_out"], "w_mem_q": gsh["w_mem_q"], "w_mem_kv": gsh["w_mem_kvT"].T,
           "w_mem_o": gsh["w_mem_o"], "w_down": gsh["w_down"],
           "w_dw": lax.dynamic_slice(gdw_sum, (0, me * (CC // NDEV)), (TAPS, CC // NDEV))}
    for n, g in big.items():
        d, nm, nv = _adamw(given[n][0], g, given["m_" + n][0], given["v_" + n][0], "adamw_" + n)
        grads[n], deltas[n], new_m[n], new_v[n] = g[None], d[None], nm[None], nv[None]
    for n in ("w_in", "w_gate", "w_up"):
        g = gsh[n + "T"]
        d, nm, nv = _adamw(given[n][0].T, g, given["m_" + n][0].T, given["v_" + n][0].T, "adamw_" + n)
        grads[n], deltas[n], new_m[n], new_v[n] = g.T[None], d.T[None], nm.T[None], nv.T[None]

    order = ("g_mix", "w_in", "b_in", "w_dw", "b_dw", "g_conv_ln", "b_conv_ln", "attn_sink", "w_out", "b_out",
             "g_mem_q", "g_mem_kv", "w_mem_q", "w_mem_kv", "w_mem_o", "g_ffn", "w_gate", "w_up", "w_down", "g_final")
    return (loss, grad_x[None], *[grads[n] for n in order], *[deltas[n] for n in order],
            *[new_m[n] for n in order], *[new_v[n] for n in order])
```

```python
import functools

import jax
import jax.numpy as jnp
from jax import lax
from jax.experimental import pallas as pl
from jax.experimental.pallas import tpu as pltpu

D = 1024
CC = 512
HD = 64
NQ = 8
NKV = 2
GRP = NQ // NKV
ATT = NQ * HD
KVC = NKV * HD
BLK = 128
MH = 4
MHD = D // MH
F = 2816
FCH = F // 2
INC = 2 * CC + ATT + 2 * KVC
TAPS = 31
EPS = 1e-6
ROPE_THETA = 10000.0
SCALE = HD ** -0.5
MSCALE = MHD ** -0.5
LANES = 128
NDEV = 8

ADAM_LR = 0.001
ADAM_B1 = 0.9
ADAM_B2 = 0.999
ADAM_EPS = 1e-08
ADAM_WD = 0.01
ADAM_STEP = 10

F32 = jnp.float32
_MXU = jnp.bfloat16
MESH = pl.DeviceIdType.MESH

W_NAMES = ("w_inT", "w_out", "w_mem_q", "w_mem_kvT", "w_mem_o", "w_gateT", "w_upT", "w_down")
W_ROWS = (INC // NDEV, D // NDEV, D // NDEV, 2 * D // NDEV, D // NDEV, F // NDEV, F // NDEV, F // NDEV)


def _mm(a, b, ca, cb):
    return lax.dot_general(a.astype(_MXU), b.astype(_MXU), (((ca,), (cb,)), ((), ())),
                           preferred_element_type=F32)


def _nn(a, b):
    return _mm(a, b, 1, 0)


def _nt(a, b):
    return _mm(a, b, 1, 1)


def _tn(a, b):
    return _mm(a, b, 0, 0)


def _sigmoid(x):
    return 1.0 / (1.0 + jnp.exp(-x))


def _rms(x, g):
    r = lax.rsqrt(jnp.mean(x * x, axis=-1, keepdims=True) + EPS)
    return x * r * g, r


def _rms_bwd(dy, x, r, g):
    xh = x * r
    dg = jnp.sum(dy * xh, axis=0, keepdims=True)
    dxh = dy * g
    dx = r * (dxh - xh * jnp.mean(dxh * xh, axis=-1, keepdims=True))
    return dx, dg


def _rope(t, c, s):
    n = t.shape[1]
    reps = n // LANES
    if reps > 1:
        c = jnp.tile(c, (1, reps))
        s = jnp.tile(s, (1, reps))
    lane = lax.broadcasted_iota(jnp.int32, t.shape, 1)
    first = (lane & (HD - 1)) < (HD // 2)
    partner = jnp.where(first, pltpu.roll(t, n - HD // 2, 1), pltpu.roll(t, HD // 2, 1))
    return t * c + partner * s


def _rope_bwd(dt, c, s):
    n = dt.shape[1]
    reps = n // LANES
    if reps > 1:
        c = jnp.tile(c, (1, reps))
        s = jnp.tile(s, (1, reps))
    lane = lax.broadcasted_iota(jnp.int32, dt.shape, 1)
    first = (lane & (HD - 1)) < (HD // 2)
    ds = dt * s
    partner = jnp.where(first, pltpu.roll(ds, n - HD // 2, 1), pltpu.roll(ds, HD // 2, 1))
    return dt * c + partner


def _acc_init(i, *refs):
    @pl.when(i == 0)
    def _():
        for r in refs:
            r[...] = jnp.zeros(r.shape, r.dtype)


def _acc_row(ref, v):
    ref[...] += jnp.broadcast_to(v, ref.shape)


def _rows(tm, n):
    return pl.BlockSpec((tm, n), lambda i: (i, 0))


def _full(shape):
    nd = len(shape)
    return pl.BlockSpec(shape, lambda i: (0,) * nd)


def _once(shape):
    nd = len(shape)
    return pl.BlockSpec(shape, lambda i: (0,) * nd, pipeline_mode=pl.Buffered(1))


def _params(vmem_mb):
    return pltpu.CompilerParams(dimension_semantics=("arbitrary",), vmem_limit_bytes=vmem_mb * 2 ** 20)


def _sds(shape, dtype):
    return pltpu.HBM(shape, dtype)


def _hbm_call(*args, **kwargs):
    call = pl.pallas_call(*args, **kwargs)
    return lambda *ops: call(*[pltpu.with_memory_space_constraint(o, pltpu.HBM) for o in ops])


def _mem_kv_fwd(mem, g_kv, wkvT):
    M = mem.shape[0]

    def body(mem_ref, g_ref, w_ref, km_ref, vm_ref, mn_ref):
        y, _ = _rms(mem_ref[...], g_ref[...])
        kv = _nt(y, w_ref[...])
        km_ref[...] = kv[:, :D].astype(km_ref.dtype)
        vm_ref[...] = kv[:, D:].astype(vm_ref.dtype)
        mn_ref[...] = y.astype(mn_ref.dtype)

    return _hbm_call(
        body, name="mem_kv_fwd", grid=(1,),
        in_specs=[_full((M, D)), _full((1, D)), _full((2 * D, D))],
        out_specs=[_full((M, D))] * 3,
        out_shape=[_sds((M, D), _MXU)] * 3,
        compiler_params=_params(40),
    )(mem, g_kv, wkvT)


def _in_proj_fwd(x, g_mix, winT, b_in, cosq, sinq, tm=512):
    S = x.shape[0]

    def body(x_ref, g_ref, w_ref, b_ref, c_ref, s_ref, h_ref, uglu_ref, vg_ref, q_ref, k_ref, v_ref):
        h, _ = _rms(x_ref[...], g_ref[...])
        hb = h.astype(h_ref.dtype)
        h_ref[...] = hb
        u = _nt(hb, w_ref[...]) + b_ref[...]
        uglu_ref[...] = u[:, :2 * CC]
        vg_ref[...] = u[:, :CC] * _sigmoid(u[:, CC:2 * CC])
        c = c_ref[...]
        s = s_ref[...]
        q_ref[...] = _rope(u[:, 2 * CC:2 * CC + ATT], c, s).astype(q_ref.dtype)
        k_ref[...] = _rope(u[:, 2 * CC + ATT:2 * CC + ATT + KVC], c, s).astype(k_ref.dtype)
        v_ref[...] = u[:, 2 * CC + ATT + KVC:].astype(v_ref.dtype)

    return _hbm_call(
        body, name="in_proj_fwd", grid=(S // tm,),
        in_specs=[_rows(tm, D), _full((1, D)), _full((INC, D)), _full((1, INC)), _rows(tm, LANES), _rows(tm, LANES)],
        out_specs=[_rows(tm, D), _rows(tm, 2 * CC), _rows(tm, CC), _rows(tm, ATT), _rows(tm, KVC), _rows(tm, KVC)],
        out_shape=[_sds((S, D), _MXU), _sds((S, 2 * CC), F32), _sds((S, CC), F32), _sds((S, ATT), _MXU),
                   _sds((S, KVC), _MXU), _sds((S, KVC), _MXU)],
        compiler_params=_params(48),
    )(x, g_mix, winT, b_in, cosq, sinq)


CONV_HALO = 16


def _dwconv_tile(pad_ref, t0, tm, w_ref, flip):
    cols = []
    for cl in range(CC // LANES):
        lanes = slice(cl * LANES, (cl + 1) * LANES)
        blk = pad_ref[pl.ds(t0, tm + 2 * CONV_HALO), lanes]
        acc = jnp.zeros((tm, LANES), F32)
        for b in range(8):
            sh = blk[b:b + tm + 24, :]
            for a in range(4):
                k = 8 * a + b - 1
                if 0 <= k < TAPS:
                    kk = TAPS - 1 - k if flip else k
                    acc = acc + sh[8 * a:8 * a + tm, :] * w_ref[kk:kk + 1, lanes]
        cols.append(acc)
    return jnp.concatenate(cols, axis=1)


def _fill_padded(i, src_hbm, pad_ref, sem, S):
    @pl.when(i == 0)
    def _():
        zeros = jnp.zeros((CONV_HALO, pad_ref.shape[1]), pad_ref.dtype)
        pad_ref[0:CONV_HALO, :] = zeros
        pad_ref[CONV_HALO + S:2 * CONV_HALO + S, :] = zeros
        cp = pltpu.make_async_copy(src_hbm, pad_ref.at[pl.ds(CONV_HALO, S), :], sem)
        cp.start()
        cp.wait()


def _ln_silu(y, g, b):
    mu = jnp.mean(y, axis=-1, keepdims=True)
    yc = y - mu
    var = jnp.mean(yc * yc, axis=-1, keepdims=True)
    rstd = lax.rsqrt(var + EPS)
    yh = yc * rstd
    z = yh * g + b
    return yh, rstd, z


def _conv_fwd(vg, w_dw, b_dw, ln_g, ln_b, tm=128):
    S = vg.shape[0]

    def body(vg_hbm, w_ref, b_ref, g_ref, bb_ref, y_ref, yc_ref, pad_ref, sem):
        i = pl.program_id(0)
        _fill_padded(i, vg_hbm, pad_ref, sem, S)
        t0 = pl.multiple_of(i * tm, tm)
        y = _dwconv_tile(pad_ref, t0, tm, w_ref, False) + b_ref[...]
        y_ref[...] = y
        _, _, z = _ln_silu(y, g_ref[...], bb_ref[...])
        yc_ref[...] = (z * _sigmoid(z)).astype(yc_ref.dtype)

    return _hbm_call(
        body, name="conv_fwd", grid=(S // tm,),
        in_specs=[pl.BlockSpec(memory_space=pl.ANY), _full((TAPS, CC)), _full((1, CC)), _full((1, CC)), _full((1, CC))],
        out_specs=[_rows(tm, CC), _rows(tm, CC)],
        out_shape=[_sds((S, CC), F32), _sds((S, CC), _MXU)],
        scratch_shapes=[pltpu.VMEM((S + 2 * CONV_HALO, CC), F32), pltpu.SemaphoreType.DMA],
        compiler_params=_params(40),
    )(vg, w_dw, b_dw, ln_g, ln_b)


def _attn_window(i, S):
    start = jnp.clip((i - 1) * BLK, 0, S - 3 * BLK)
    return pl.multiple_of(start, BLK)


def _attn_mask(i, start):
    keys = lax.broadcasted_iota(jnp.int32, (3 * BLK, GRP * BLK), 0)
    cols = lax.broadcasted_iota(jnp.int32, (3 * BLK, GRP * BLK), 1)
    rel = (start + keys) - (i * BLK + (cols & (BLK - 1)))
    return jnp.abs(rel) <= BLK


def _stack_heads(t, hk):
    return jnp.concatenate([t[:, (hk * GRP + g) * HD:(hk * GRP + g + 1) * HD] for g in range(GRP)], axis=0)


def _unstack_heads_t(t):
    cols = []
    for p in range(GRP // 2):
        pair = jnp.concatenate([t[:, (2 * p) * BLK:(2 * p + 1) * BLK], t[:, (2 * p + 1) * BLK:(2 * p + 2) * BLK]],
                               axis=0)
        cols.append(pair.T)
    return jnp.concatenate(cols, axis=1)


def _attn_probs(qs, kh, sk_ref, hk, mask):
    s = _nt(kh, qs) * SCALE
    s = jnp.where(mask, s, -1e30)
    sk = jnp.concatenate(
        [jnp.broadcast_to(sk_ref[0:1, hk * GRP + g:hk * GRP + g + 1], (1, BLK)) for g in range(GRP)], axis=1)
    m = jnp.maximum(jnp.max(s, axis=0, keepdims=True), sk)
    e = jnp.exp(s - m)
    es = jnp.exp(sk - m)
    inv = 1.0 / (jnp.sum(e, axis=0, keepdims=True) + es)
    return e, inv, es


ATTN_BLOCKS = 4


def _attn_fwd(qr, kr, vv, sink):
    S = qr.shape[0]
    tq = ATTN_BLOCKS * BLK

    def body(q_ref, k_ref, v_ref, sk_ref, o_ref):
        for b in range(ATTN_BLOCKS):
            i = pl.program_id(0) * ATTN_BLOCKS + b
            rows = slice(b * BLK, (b + 1) * BLK)
            start = _attn_window(i, S)
            kb = k_ref[pl.ds(start, 3 * BLK), :]
            vb = v_ref[pl.ds(start, 3 * BLK), :]
            q = q_ref[rows, :]
            mask = _attn_mask(i, start)
            outs = []
            for hk in range(NKV):
                hs = slice(hk * HD, (hk + 1) * HD)
                e, inv, _ = _attn_probs(_stack_heads(q, hk), kb[:, hs], sk_ref, hk, mask)
                outs.append(_unstack_heads_t(_tn(vb[:, hs], e) * inv))
            o_ref[rows, :] = jnp.concatenate(outs, axis=1).astype(o_ref.dtype)

    return _hbm_call(
        body, name="attn_fwd", grid=(S // tq,),
        in_specs=[_rows(tq, ATT), _full((S, KVC)), _full((S, KVC)), _full((1, NQ))],
        out_specs=_rows(tq, ATT),
        out_shape=_sds((S, ATT), _MXU),
        compiler_params=_params(40),
    )(qr, kr, vv, sink)


def _out_proj_fwd(x, yc, ya, wout, b_out, tm=512):
    S = x.shape[0]

    def body(x_ref, yc_ref, ya_ref, w_ref, b_ref, x1_ref):
        x1_ref[...] = (x_ref[...] + _nn(yc_ref[...], w_ref[0:CC, :]) + _nn(ya_ref[...], w_ref[CC:2 * CC, :])
                       + b_ref[...])

    return _hbm_call(
        body, name="out_proj_fwd", grid=(S // tm,),
        in_specs=[_rows(tm, D), _rows(tm, CC), _rows(tm, ATT), _full((D, D)), _full((1, D))],
        out_specs=_rows(tm, D),
        out_shape=_sds((S, D), F32),
        compiler_params=_params(40),
    )(x, yc, ya, wout, b_out)


def _softmax(s):
    m = jnp.max(s, axis=-1, keepdims=True)
    e = jnp.exp(s - m)
    return e * (1.0 / jnp.sum(e, axis=-1, keepdims=True))


def _mem_attn_fwd(x1, g_q, wq, km, vm, wo, tm=512):
    S = x1.shape[0]
    M = km.shape[0]

    def body(x1_ref, g_ref, wq_ref, km_ref, vm_ref, wo_ref, x2_ref, qm_ref, om_ref):
        x1v = x1_ref[...]
        hq, _ = _rms(x1v, g_ref[...])
        qm = _nn(hq, wq_ref[...]).astype(qm_ref.dtype)
        qm_ref[...] = qm
        oms = []
        for h in range(MH):
            sl = slice(h * MHD, (h + 1) * MHD)
            p = _softmax(_nt(qm[:, sl], km_ref[:, sl]) * MSCALE)
            oms.append(_nn(p, vm_ref[:, sl]))
        om = jnp.concatenate(oms, axis=1).astype(om_ref.dtype)
        om_ref[...] = om
        x2_ref[...] = x1v + _nn(om, wo_ref[...])

    return _hbm_call(
        body, name="mem_attn_fwd", grid=(S // tm,),
        in_specs=[_rows(tm, D), _full((1, D)), _full((D, D)), _full((M, D)), _full((M, D)), _full((D, D))],
        out_specs=[_rows(tm, D)] * 3,
        out_shape=[_sds((S, D), F32), _sds((S, D), _MXU), _sds((S, D), _MXU)],
        compiler_params=_params(48),
    )(x1, g_q, wq, km, vm, wo)


def _ffn_loss_fwd(x2, tgt, g_ffn, g_final, wgT, wuT, wd, tm=256):
    S = x2.shape[0]

    def body(x2_ref, t_ref, gf_ref, gl_ref, wg_ref, wu_ref, wd_ref,
             gt_ref, up_ref, hf_ref, dx3_ref, dx3b_ref, loss_ref, dgl_ref):
        i = pl.program_id(0)
        _acc_init(i, loss_ref, dgl_ref)
        x2v = x2_ref[...]
        hf, _ = _rms(x2v, gf_ref[...])
        hb = hf.astype(hf_ref.dtype)
        hf_ref[...] = hb
        x3 = x2v
        for c in range(F // FCH):
            sl = slice(c * FCH, (c + 1) * FCH)
            gt = _nt(hb, wg_ref[sl, :])
            up = _nt(hb, wu_ref[sl, :])
            gt_ref[:, sl] = gt.astype(gt_ref.dtype)
            up_ref[:, sl] = up.astype(up_ref.dtype)
            x3 = x3 + _nn(gt * _sigmoid(gt) * up, wd_ref[sl, :])
        gl = gl_ref[...]
        y, r = _rms(x3, gl)
        diff = y - t_ref[...]
        _acc_row(loss_ref, jnp.sum(jnp.sum(diff * diff, axis=-1, keepdims=True), axis=0, keepdims=True) * (0.5 / D))
        dx3, dgl = _rms_bwd(diff * (1.0 / D), x3, r, gl)
        dx3_ref[...] = dx3
        dx3b_ref[...] = dx3.astype(dx3b_ref.dtype)
        _acc_row(dgl_ref, dgl)

    return _hbm_call(
        body, name="ffn_loss_fwd", grid=(S // tm,),
        in_specs=[_rows(tm, D), _rows(tm, D), _full((1, D)), _full((1, D)),
                  _once((F, D)), _once((F, D)), _once((F, D))],
        out_specs=[_rows(tm, F), _rows(tm, F), _rows(tm, D), _rows(tm, D), _rows(tm, D),
                   _full((8, LANES)), _full((8, D))],
        out_shape=[_sds((S, F), _MXU), _sds((S, F), _MXU), _sds((S, D), _MXU), _sds((S, D), F32),
                   _sds((S, D), _MXU), _sds((8, LANES), F32), _sds((8, D), F32)],
        compiler_params=_params(56),
    )(x2, tgt, g_ffn, g_final, wgT, wuT, wd)


def _ffn_bwd(dx3, x2, gt, up, g_ffn, wd, wgT, wuT, tm=256):
    S = x2.shape[0]

    def body(dx3_ref, x2_ref, gt_ref, up_ref, g_ref, wd_ref, wg_ref, wu_ref,
             dx2_ref, dx2b_ref, dgt_ref, dup_ref, act_ref, dg_ref):
        i = pl.program_id(0)
        _acc_init(i, dg_ref)
        dx3v = dx3_ref[...]
        db = dx3v.astype(_MXU)
        dhf = jnp.zeros((tm, D), F32)
        for c in range(F // FCH):
            sl = slice(c * FCH, (c + 1) * FCH)
            gtv = gt_ref[:, sl].astype(F32)
            upv = up_ref[:, sl].astype(F32)
            dact = _nt(db, wd_ref[sl, :])
            sig = _sigmoid(gtv)
            silu = gtv * sig
            dup = (dact * silu).astype(dup_ref.dtype)
            dgt = (dact * upv * (sig * (1.0 + gtv * (1.0 - sig)))).astype(dgt_ref.dtype)
            act_ref[:, sl] = (silu * upv).astype(act_ref.dtype)
            dgt_ref[:, sl] = dgt
            dup_ref[:, sl] = dup
            dhf = dhf + _nn(dgt, wg_ref[sl, :]) + _nn(dup, wu_ref[sl, :])
        x2v = x2_ref[...]
        g = g_ref[...]
        _, r = _rms(x2v, g)
        dxn, dg = _rms_bwd(dhf, x2v, r, g)
        dx2 = dx3v + dxn
        dx2_ref[...] = dx2
        dx2b_ref[...] = dx2.astype(dx2b_ref.dtype)
        _acc_row(dg_ref, dg)

    return _hbm_call(
        body, name="ffn_bwd", grid=(S // tm,),
        in_specs=[_rows(tm, D), _rows(tm, D), _rows(tm, F), _rows(tm, F), _full((1, D)),
                  _once((F, D)), _once((F, D)), _once((F, D))],
        out_specs=[_rows(tm, D), _rows(tm, D), _rows(tm, F), _rows(tm, F), _rows(tm, F), _full((8, D))],
        out_shape=[_sds((S, D), F32), _sds((S, D), _MXU), _sds((S, F), _MXU), _sds((S, F), _MXU),
                   _sds((S, F), _MXU), _sds((8, D), F32)],
        compiler_params=_params(56),
    )(dx3, x2, gt, up, g_ffn, wd, wgT, wuT)


_AFTER = pl.BlockSpec(memory_space=pl.ANY)


def _wgrad(a, b, name, after, rc=256):
    S, R = a.shape
    N = b.shape[1]

    def body(a_ref, b_ref, after_ref, o_ref, ob_ref):
        g = _tn(a_ref[...], b_ref[...])
        o_ref[...] = g
        ob_ref[...] = g.astype(ob_ref.dtype)

    return _hbm_call(
        body, name=name, grid=(R // rc,),
        in_specs=[pl.BlockSpec((S, rc), lambda j: (0, j)), _once((S, N)), _AFTER],
        out_specs=[_rows(rc, N), _rows(rc, N)],
        out_shape=[_sds((R, N), F32), _sds((R, N), jnp.bfloat16)],
        compiler_params=_params(48),
    )(a, b, after)


def _mem_attn_bwd(dx2, x1, qm, km, vm, g_q, wq, wo, wout, after, tm=512):
    S = x1.shape[0]
    M = km.shape[0]

    def body(dx2_ref, x1_ref, qm_ref, km_ref, vm_ref, g_ref, wq_ref, wo_ref, wout_ref, after_ref,
             dx1_ref, dx1b_ref, hq_ref, dqm_ref, dyc_ref, dya_ref, dkm_ref, dvm_ref, dg_ref, dbo_ref):
        i = pl.program_id(0)
        _acc_init(i, dkm_ref, dvm_ref, dg_ref, dbo_ref)
        dx2v = dx2_ref[...]
        dom = _nt(dx2v, wo_ref[...]).astype(_MXU)
        dqs = []
        for h in range(MH):
            sl = slice(h * MHD, (h + 1) * MHD)
            qh = qm_ref[:, sl]
            kh = km_ref[:, sl]
            p = _softmax(_nt(qh, kh) * MSCALE)
            dp = _nt(dom[:, sl], vm_ref[:, sl])
            ds = (p * (dp - jnp.sum(dp * p, axis=-1, keepdims=True))).astype(_MXU)
            dqs.append(_nn(ds, kh) * MSCALE)
            dkm_ref[:, sl] += _tn(ds, qh) * MSCALE
            dvm_ref[:, sl] += _tn(p, dom[:, sl])
        dqm = jnp.concatenate(dqs, axis=1).astype(dqm_ref.dtype)
        dqm_ref[...] = dqm
        dhq = _nt(dqm, wq_ref[...])
        x1v = x1_ref[...]
        g = g_ref[...]
        hq, r = _rms(x1v, g)
        hq_ref[...] = hq.astype(hq_ref.dtype)
        dxn, dg = _rms_bwd(dhq, x1v, r, g)
        dx1 = dx2v + dxn
        dx1_ref[...] = dx1
        dx1b = dx1.astype(dx1b_ref.dtype)
        dx1b_ref[...] = dx1b
        dym = _nt(dx1b, wout_ref[...])
        dyc_ref[...] = dym[:, :CC]
        dya_ref[...] = dym[:, CC:].astype(dya_ref.dtype)
        _acc_row(dg_ref, dg)
        _acc_row(dbo_ref, jnp.sum(dx1, axis=0, keepdims=True))

    return _hbm_call(
        body, name="mem_attn_bwd", grid=(S // tm,),
        in_specs=[_rows(tm, D), _rows(tm, D), _rows(tm, D), _full((M, D)), _full((M, D)), _full((1, D)),
                  _once((D, D)), _once((D, D)), _once((D, D)), _AFTER],
        out_specs=[_rows(tm, D), _rows(tm, D), _rows(tm, D), _rows(tm, D), _rows(tm, CC), _rows(tm, ATT),
                   _full((M, D)), _full((M, D)), _full((8, D)), _full((8, D))],
        out_shape=[_sds((S, D), F32), _sds((S, D), _MXU), _sds((S, D), _MXU), _sds((S, D), _MXU),
                   _sds((S, CC), F32), _sds((S, ATT), _MXU),
                   _sds((M, D), F32), _sds((M, D), F32), _sds((8, D), F32), _sds((8, D), F32)],
        compiler_params=_params(56),
    )(dx2, x1, qm, km, vm, g_q, wq, wo, wout, after)


def _mem_kv_bwd(dkm, dvm, mem, g_kv, memn, wkvT):
    M = mem.shape[0]

    def body(dkm_ref, dvm_ref, mem_ref, g_ref, mn_ref, w_ref, gw_ref, gwb_ref, dg_ref):
        dkv = jnp.concatenate([dkm_ref[...], dvm_ref[...]], axis=1).astype(_MXU)
        gw = _tn(dkv, mn_ref[...])
        gw_ref[...] = gw
        gwb_ref[...] = gw.astype(gwb_ref.dtype)
        dmn = _nn(dkv, w_ref[...])
        memv = mem_ref[...]
        g = g_ref[...]
        _, r = _rms(memv, g)
        _, dg = _rms_bwd(dmn, memv, r, g)
        dg_ref[...] = jnp.broadcast_to(dg, dg_ref.shape)

    return _hbm_call(
        body, name="mem_kv_bwd", grid=(1,),
        in_specs=[_full((M, D)), _full((M, D)), _full((M, D)), _full((1, D)), _full((M, D)), _full((2 * D, D))],
        out_specs=[_full((2 * D, D)), _full((2 * D, D)), _full((8, D))],
        out_shape=[_sds((2 * D, D), F32), _sds((2 * D, D), jnp.bfloat16), _sds((8, D), F32)],
        compiler_params=_params(48),
    )(dkm, dvm, mem, g_kv, memn, wkvT)


def _attn_bwd(qr, kr, vv, sink, dya, after):
    S = qr.shape[0]
    tq = ATTN_BLOCKS * BLK

    def body(q_ref, k_ref, v_ref, sk_ref, do_ref, after_ref, dq_ref, dk_ref, dv_ref, dsk_ref, sink_acc):
        _acc_init(pl.program_id(0), dk_ref, dv_ref, sink_acc)
        for b in range(ATTN_BLOCKS):
            i = pl.program_id(0) * ATTN_BLOCKS + b
            rows = slice(b * BLK, (b + 1) * BLK)
            start = _attn_window(i, S)
            kb = k_ref[pl.ds(start, 3 * BLK), :]
            vb = v_ref[pl.ds(start, 3 * BLK), :]
            q = q_ref[rows, :]
            do = do_ref[rows, :]
            mask = _attn_mask(i, start)
            dqs, dks, dvs = [], [], []
            for hk in range(NKV):
                hs = slice(hk * HD, (hk + 1) * HD)
                qs = _stack_heads(q, hk)
                dos = _stack_heads(do, hk)
                e, inv, es = _attn_probs(qs, kb[:, hs], sk_ref, hk, mask)
                p = e * inv
                dp = _nt(vb[:, hs], dos)
                delta = jnp.sum(dp * p, axis=0, keepdims=True)
                ds = (p * (dp - delta)).astype(_MXU)
                dqs.append(_unstack_heads_t(_tn(kb[:, hs], ds) * SCALE))
                dks.append(_nn(ds, qs) * SCALE)
                dvs.append(_nn(p, dos))
                sink_acc[hk:hk + 1, :] -= es * inv * delta
            dq_ref[rows, :] = jnp.concatenate(dqs, axis=1)
            dk_ref[pl.ds(start, 3 * BLK), :] += jnp.concatenate(dks, axis=1)
            dv_ref[pl.ds(start, 3 * BLK), :] += jnp.concatenate(dvs, axis=1)

        @pl.when(pl.program_id(0) == pl.num_programs(0) - 1)
        def _():
            for h in range(NQ):
                tot = jnp.sum(sink_acc[h // GRP:h // GRP + 1, (h % GRP) * BLK:(h % GRP + 1) * BLK], axis=1,
                              keepdims=True)
                dsk_ref[h:h + 1, :] = jnp.broadcast_to(tot, (1, LANES))

    return _hbm_call(
        body, name="attn_bwd", grid=(S // tq,),
        in_specs=[_rows(tq, ATT), _full((S, KVC)), _full((S, KVC)), _full((1, NQ)), _rows(tq, ATT), _AFTER],
        out_specs=[_rows(tq, ATT), _full((S, KVC)), _full((S, KVC)), _full((8, LANES))],
        out_shape=[_sds((S, ATT), F32), _sds((S, KVC), F32), _sds((S, KVC), F32), _sds((8, LANES), F32)],
        scratch_shapes=[pltpu.VMEM((8, GRP * BLK), F32)],
        compiler_params=_params(40),
    )(qr, kr, vv, sink, dya, after)


def _conv_norm_bwd(dyc, y, ln_g, ln_b, after, tm=512):
    S = y.shape[0]

    def body(dyc_ref, y_ref, g_ref, b_ref, after_ref, dy_ref, dg_ref, db_ref, dbdw_ref):
        i = pl.program_id(0)
        _acc_init(i, dg_ref, db_ref, dbdw_ref)
        g = g_ref[...]
        yh, rstd, z = _ln_silu(y_ref[...], g, b_ref[...])
        sig = _sigmoid(z)
        dz = dyc_ref[...] * (sig * (1.0 + z * (1.0 - sig)))
        dyh = dz * g
        dy = rstd * (dyh - jnp.mean(dyh, axis=-1, keepdims=True)
                     - yh * jnp.mean(dyh * yh, axis=-1, keepdims=True))
        dy_ref[...] = dy
        _acc_row(dg_ref, jnp.sum(dz * yh, axis=0, keepdims=True))
        _acc_row(db_ref, jnp.sum(dz, axis=0, keepdims=True))
        _acc_row(dbdw_ref, jnp.sum(dy, axis=0, keepdims=True))

    return _hbm_call(
        body, name="conv_norm_bwd", grid=(S // tm,),
        in_specs=[_rows(tm, CC), _rows(tm, CC), _full((1, CC)), _full((1, CC)), _AFTER],
        out_specs=[_rows(tm, CC), _full((8, CC)), _full((8, CC)), _full((8, CC))],
        out_shape=[_sds((S, CC), F32), _sds((8, CC), F32), _sds((8, CC), F32), _sds((8, CC), F32)],
        compiler_params=_params(40),
    )(dyc, y, ln_g, ln_b, after)


def _conv_bwd(dy, vg, uglu, w_dw, tm=128):
    S = dy.shape[0]

    def body(dy_hbm, vg_hbm, uglu_ref, w_ref, du_ref, gw_ref, dbin_ref, dyp_ref, vgp_ref, gacc_ref, sems):
        i = pl.program_id(0)
        _acc_init(i, gacc_ref, dbin_ref)
        _fill_padded(i, dy_hbm, dyp_ref, sems.at[0], S)
        _fill_padded(i, vg_hbm, vgp_ref, sems.at[1], S)
        t0 = pl.multiple_of(i * tm, tm)
        dvg = _dwconv_tile(dyp_ref, t0, tm, w_ref, True)
        for cl in range(CC // LANES):
            lanes = slice(cl * LANES, (cl + 1) * LANES)
            dyt = dyp_ref[pl.ds(t0 + CONV_HALO, tm), lanes]
            blk = vgp_ref[pl.ds(t0, tm + 2 * CONV_HALO), lanes]
            for b in range(8):
                sh = blk[b:b + tm + 24, :]
                for a in range(4):
                    k = 8 * a + b - 1
                    if 0 <= k < TAPS:
                        prod = dyt * sh[8 * a:8 * a + tm, :]
                        gacc_ref[k, :, lanes] += jnp.sum(prod.reshape(tm // 8, 8, LANES), axis=0)
        a_v = uglu_ref[:, :CC]
        sg = _sigmoid(uglu_ref[:, CC:])
        da = dvg * sg
        dgate = dvg * a_v * sg * (1.0 - sg)
        du = jnp.concatenate([da, dgate], axis=1)
        du_ref[...] = du.astype(du_ref.dtype)
        _acc_row(dbin_ref, jnp.sum(du, axis=0, keepdims=True))

        @pl.when(i == pl.num_programs(0) - 1)
        def _():
            gw_ref[...] = jnp.zeros(gw_ref.shape, F32)
            for k in range(TAPS):
                gw_ref[k:k + 1, :] = jnp.sum(gacc_ref[k], axis=0, keepdims=True)

    return _hbm_call(
        body, name="conv_bwd", grid=(S // tm,),
        in_specs=[pl.BlockSpec(memory_space=pl.ANY), pl.BlockSpec(memory_space=pl.ANY), _rows(tm, 2 * CC),
                  _full((TAPS, CC))],
        out_specs=[_rows(tm, 2 * CC), _full((TAPS + 1, CC)), _full((8, 2 * CC))],
        out_shape=[_sds((S, 2 * CC), _MXU), _sds((TAPS + 1, CC), F32), _sds((8, 2 * CC), F32)],
        scratch_shapes=[pltpu.VMEM((S + 2 * CONV_HALO, CC), F32), pltpu.VMEM((S + 2 * CONV_HALO, CC), F32),
                        pltpu.VMEM((TAPS, 8, CC), F32), pltpu.SemaphoreType.DMA((2,))],
        compiler_params=_params(48),
    )(dy, vg, uglu, w_dw)


def _in_proj_bwd(duglu, dqr, dkr, dv, cosq, sinq, x, dx1, g_mix, winT, tm=512):
    S = x.shape[0]
    NA = ATT + 2 * KVC

    def body(dug_ref, dq_ref, dk_ref, dv_ref, c_ref, s_ref, x_ref, dx1_ref, g_ref, w_ref,
             dx_ref, du_ref, dg_ref, dbin_ref):
        i = pl.program_id(0)
        _acc_init(i, dg_ref, dbin_ref)
        c = c_ref[...]
        s = s_ref[...]
        datt = jnp.concatenate([_rope_bwd(dq_ref[...], c, s), _rope_bwd(dk_ref[...], c, s), dv_ref[...]], axis=1)
        _acc_row(dbin_ref, jnp.sum(datt, axis=0, keepdims=True))
        du = jnp.concatenate([dug_ref[...], datt.astype(du_ref.dtype)], axis=1)
        du_ref[...] = du
        dh = _nn(du, w_ref[...])
        xv = x_ref[...]
        g = g_ref[...]
        _, r = _rms(xv, g)
        dxn, dg = _rms_bwd(dh, xv, r, g)
        dx_ref[...] = dx1_ref[...] + dxn
        _acc_row(dg_ref, dg)

    return _hbm_call(
        body, name="in_proj_bwd", grid=(S // tm,),
        in_specs=[_rows(tm, 2 * CC), _rows(tm, ATT), _rows(tm, KVC), _rows(tm, KVC), _rows(tm, LANES),
                  _rows(tm, LANES), _rows(tm, D), _rows(tm, D), _full((1, D)), _once((INC, D))],
        out_specs=[_rows(tm, D), _rows(tm, INC), _full((8, D)), _full((8, NA))],
        out_shape=[_sds((S, D), F32), _sds((S, INC), _MXU), _sds((8, D), F32), _sds((8, NA), F32)],
        compiler_params=_params(48),
    )(duglu, dqr, dkr, dv, cosq, sinq, x, dx1, g_mix, winT)


def _rope_tables(S):
    pos = jnp.arange(S, dtype=F32)
    inv_freq = ROPE_THETA ** (-jnp.arange(0, HD, 2, dtype=F32) / HD)
    ang = pos[:, None] * inv_freq[None, :]
    cos, sin = jnp.cos(ang), jnp.sin(ang)
    cosq = jnp.tile(jnp.concatenate([cos, cos], axis=1), (1, LANES // HD))
    sinq = jnp.tile(jnp.concatenate([-sin, sin], axis=1), (1, LANES // HD))
    return cosq, sinq


GATHER_GROUPS = (("w_inT", "w_dw"), ("w_out", "w_mem_kvT", "w_mem_q", "w_mem_o"), ("w_gateT", "w_upT", "w_down"))


def _local_step(x, mem, tgt, sp, gather, reduce):
    S = x.shape[0]
    cosq, sinq = _rope_tables(S)

    gather.forward(0, x)
    w0 = gather.finish(0, x)
    winT, w_dw_full = w0["w_inT"], w0["w_dw"]
    h, uglu, vg, qr, kr, vv = _in_proj_fwd(x, sp["g_mix"], winT, sp["b_in"], cosq, sinq)
    ya = _attn_fwd(qr, kr, vv, sp["attn_sink"])
    gather.forward(1, ya)
    y, yc = _conv_fwd(vg, w_dw_full, sp["b_dw"], sp["g_conv_ln"], sp["b_conv_ln"])
    w1 = gather.finish(1, yc)
    wout, wkvT, wq, wo = w1["w_out"], w1["w_mem_kvT"], w1["w_mem_q"], w1["w_mem_o"]
    km, vm, memn = _mem_kv_fwd(mem, sp["g_mem_kv"], wkvT)
    x1 = _out_proj_fwd(x, yc, ya, wout, sp["b_out"])
    gather.forward(2, x1)
    x2, qm, om = _mem_attn_fwd(x1, sp["g_mem_q"], wq, km, vm, wo)
    w2 = gather.finish(2, om)
    wgT, wuT, wd = w2["w_gateT"], w2["w_upT"], w2["w_down"]
    gt, up, hf, dx3, dx3b, loss_acc, dg_final = _ffn_loss_fwd(x2, tgt, sp["g_ffn"], sp["g_final"], wgT, wuT, wd)

    dx2, dx2b, dgt, dup, act, dg_ffn = _ffn_bwd(dx3, x2, gt, up, sp["g_ffn"], wd, wgT, wuT)
    token = reduce.start(0, [_wgrad(dgt, hf, "wgrad_gate", dx2), _wgrad(dup, hf, "wgrad_up", dx2),
                             _wgrad(act, dx3b, "wgrad_down", dx2)])

    dx1, dx1b, hq, dqm, dyc, dya, dkm, dvm, dg_mem_q, db_out = _mem_attn_bwd(
        dx2, x1, qm, km, vm, sp["g_mem_q"], wq, wo, wout, token)
    token = reduce.middle(0, dx1)
    g_wo = _wgrad(om, dx2b, "wgrad_mem_o", token)
    g_wq = _wgrad(hq, dqm, "wgrad_mem_q", token)
    g_wkvT_f, g_wkvT_b, dg_mem_kv = _mem_kv_bwd(dkm, dvm, mem, sp["g_mem_kv"], memn, wkvT)
    g_wout_c = _wgrad(yc, dx1b, "wgrad_out_conv", token)
    g_wout_a = _wgrad(ya, dx1b, "wgrad_out_attn", token)
    g_wout = tuple(jnp.concatenate([c, a], axis=0) for c, a in zip(g_wout_c, g_wout_a))
    token = reduce.start(1, [g_wout, g_wq, (g_wkvT_f, g_wkvT_b), g_wo])

    dqr, dkr, dvv, dsink = _attn_bwd(qr, kr, vv, sp["attn_sink"], dya, token)
    token = reduce.middle(1, dqr)
    dy, dg_cln, db_cln, db_dw = _conv_norm_bwd(dyc, y, sp["g_conv_ln"], sp["b_conv_ln"], token)
    duglu, g_wdw, dbin_glu = _conv_bwd(dy, vg, uglu, w_dw_full)
    grad_x, du, dg_mix, dbin_att = _in_proj_bwd(duglu, dqr, dkr, dvv, cosq, sinq, x, dx1, sp["g_mix"], winT)
    token = reduce.start(2, [_wgrad(du, h, "wgrad_in", grad_x)])

    small = {
        "loss": loss_acc[0:1, :],
        "g_mix": dg_mix[0:1], "b_in": jnp.concatenate([dbin_glu[0:1], dbin_att[0:1]], axis=1),
        "b_dw": db_dw[0:1], "g_conv_ln": dg_cln[0:1], "b_conv_ln": db_cln[0:1],
        "attn_sink": dsink[:, 0].reshape(1, NQ), "b_out": db_out[0:1], "g_mem_q": dg_mem_q[0:1],
        "g_mem_kv": dg_mem_kv[0:1], "g_ffn": dg_ffn[0:1], "g_final": dg_final[0:1],
    }
    token = reduce.middle(2, token, reduce.small_zones(small, g_wdw))
    reduce.finish(0, token)
    reduce.finish(1, token)
    return grad_x


def _place():
    return lax.axis_index("x"), lax.axis_index("y"), lax.axis_index("c")


def _hbm_specs(n):
    return [pl.BlockSpec(memory_space=pl.ANY)] * n


_HBM = pl.BlockSpec(memory_space=pltpu.HBM)
_SEM = pl.BlockSpec(memory_space=pltpu.SEMAPHORE)
_ANY = pl.BlockSpec(memory_space=pl.ANY)
_EFFECT = pltpu.SideEffectType.DATAFLOW_SIDE_EFFECTING
_VMEM = pl.BlockSpec(memory_space=pltpu.VMEM)
_TOKEN = jax.ShapeDtypeStruct((8, LANES), F32)


def _dev_index(p):
    return 4 * p[0] + 2 * p[1] + p[2]


def _gather_start(groups):
    sizes = [len(g) for g in groups]
    flat = [z for g in groups for z in g]
    n = len(flat)

    def body(*refs):
        ins = refs[:n]
        sems = refs[2 * n:]
        x, y, c = _place()
        me = _dev_index((x, y, c))
        peers = [(x, y, 1 - c), (1 - x, y, c), (x, 1 - y, c), (1 - x, 1 - y, c)]
        base = 0
        for gi, size in enumerate(sizes):
            send_sems, recv_sems = sems[2 * gi], sems[2 * gi + 1]
            for k in range(size):
                zone = ins[base + k]
                for j, peer in enumerate(peers):
                    pltpu.make_async_remote_copy(
                        src_ref=zone.at[me], dst_ref=zone.at[me], send_sem=send_sems.at[4 * k + j],
                        recv_sem=recv_sems.at[4 * k + j], device_id=peer, device_id_type=MESH).start()
            base += size

    out_shape = [pltpu.HBM(z.shape, z.dtype) for z in flat]
    out_specs = [_HBM] * n
    for size in sizes:
        out_shape += [pltpu.SemaphoreType.DMA((4 * size,))] * 2
        out_specs += [_SEM, _SEM]
    res = pl.pallas_call(
        body, name="gather_start", in_specs=[_HBM] * n, out_specs=out_specs, out_shape=out_shape,
        input_output_aliases={k: k for k in range(n)},
        compiler_params=pltpu.CompilerParams(has_side_effects=_EFFECT),
    )(*[pltpu.with_memory_space_constraint(z, pltpu.HBM) for z in flat])
    out, base = [], 0
    for gi, size in enumerate(sizes):
        out.append((list(res[base:base + size]), res[n + 2 * gi], res[n + 2 * gi + 1]))
        base += size
    return out


def _gather_forward(zones, send_sems, recv_sems, after, collective_id, name):
    n = len(zones)

    def body(*refs):
        ins = refs[:n]
        send_ref, recv_ref = refs[n], refs[n + 1]
        fsend, frecv = refs[2 * n + 3], refs[2 * n + 4]
        x, y, c = _place()
        sibling = (x, y, 1 - c)
        barrier = pltpu.get_barrier_semaphore()
        pl.semaphore_signal(barrier, inc=1, device_id=sibling, device_id_type=MESH)
        pl.semaphore_wait(barrier, 1)
        chips = [(1 - x, y), (x, 1 - y), (1 - x, 1 - y)]
        for j, chip in enumerate(chips):
            for k in range(n):
                slot = ins[k].at[_dev_index((*chip, c))]
                pltpu.make_async_remote_copy(
                    src_ref=slot, dst_ref=slot, send_sem=send_ref.at[4 * k + 1 + j],
                    recv_sem=recv_ref.at[4 * k + 1 + j], device_id=sibling, device_id_type=MESH).wait_recv()
                pltpu.make_async_remote_copy(
                    src_ref=slot, dst_ref=slot, send_sem=fsend.at[3 * k + j], recv_sem=frecv.at[3 * k + j],
                    device_id=sibling, device_id_type=MESH).start()
        for k in range(n):
            own = ins[k].at[_dev_index((x, y, c))]
            theirs = ins[k].at[_dev_index(sibling)]
            pltpu.make_async_remote_copy(
                src_ref=theirs, dst_ref=theirs, send_sem=send_ref.at[4 * k], recv_sem=recv_ref.at[4 * k],
                device_id=sibling, device_id_type=MESH).wait_recv()
            for j in range(4):
                pltpu.make_async_remote_copy(
                    src_ref=own, dst_ref=own, send_sem=send_ref.at[4 * k + j], recv_sem=recv_ref.at[4 * k + j],
                    device_id=sibling, device_id_type=MESH).wait_send()

    res = pl.pallas_call(
        body, name=name, in_specs=[_HBM] * n + [_SEM, _SEM, _ANY],
        out_specs=[_HBM] * n + [_SEM, _SEM],
        out_shape=[pltpu.HBM(z.shape, z.dtype) for z in zones] + [pltpu.SemaphoreType.DMA((3 * n,))] * 2,
        input_output_aliases={k: k for k in range(n)},
        compiler_params=pltpu.CompilerParams(has_side_effects=_EFFECT, collective_id=collective_id),
    )(*zones, send_sems, recv_sems, after)
    return list(res[:n]), res[n], res[n + 1]


def _gather_finish(zones, fsend, frecv, after, name):
    n = len(zones)

    def body(*refs):
        ins = refs[:n]
        fsend_ref, frecv_ref = refs[n], refs[n + 1]
        x, y, c = _place()
        sibling = (x, y, 1 - c)
        chips = [(1 - x, y), (x, 1 - y), (1 - x, 1 - y)]
        for j, chip in enumerate(chips):
            for k in range(n):
                mine = ins[k].at[_dev_index((*chip, c))]
                theirs = ins[k].at[_dev_index((*chip, 1 - c))]
                pltpu.make_async_remote_copy(
                    src_ref=theirs, dst_ref=theirs, send_sem=fsend_ref.at[3 * k + j],
                    recv_sem=frecv_ref.at[3 * k + j], device_id=sibling, device_id_type=MESH).wait_recv()
                pltpu.make_async_remote_copy(
                    src_ref=mine, dst_ref=mine, send_sem=fsend_ref.at[3 * k + j],
                    recv_sem=frecv_ref.at[3 * k + j], device_id=sibling, device_id_type=MESH).wait_send()

    res = pl.pallas_call(
        body, name=name, in_specs=[_HBM] * n + [_SEM, _SEM, _ANY], out_specs=[_HBM] * n,
        out_shape=[pltpu.HBM(z.shape, z.dtype) for z in zones],
        input_output_aliases={k: k for k in range(n)},
        compiler_params=pltpu.CompilerParams(has_side_effects=_EFFECT),
    )(*zones, fsend, frecv, after)
    return list(res)


class _Gather:
    def __init__(self, started):
        self._phase = list(started)

    def forward(self, gi, after):
        zones, send_sems, recv_sems = self._phase[gi]
        self._phase[gi] = _gather_forward(zones, send_sems, recv_sems, after, gi + 1, "gather_forward_%d" % gi)

    def finish(self, gi, after):
        zones, fsend, frecv = self._phase[gi]
        out = {}
        for name, z in zip(GATHER_GROUPS[gi], _gather_finish(zones, fsend, frecv, after, "gather_finish_%d" % gi)):
            if name == "w_dw":
                out[name] = jnp.transpose(z, (1, 0, 2)).reshape(TAPS + 1, CC)[:TAPS]
            else:
                out[name] = z.reshape(NDEV * z.shape[1], D)
        return out


def _handshake(peers):
    barrier = pltpu.get_barrier_semaphore()
    for peer in peers:
        pl.semaphore_signal(barrier, inc=1, device_id=peer, device_id_type=MESH)
    pl.semaphore_wait(barrier, len(peers))


def _sibling_copies(srcs, lands, send_sems, recv_sems):
    x, y, c = _place()
    return [pltpu.make_async_remote_copy(
        src_ref=srcs[k].at[2 * s + (1 - c)], dst_ref=lands[k].at[s], send_sem=send_sems.at[4 * k + s],
        recv_sem=recv_sems.at[4 * k + s], device_id=(x, y, 1 - c), device_id_type=MESH)
        for k in range(len(srcs)) for s in range(4)]


def _sibling_start(gbs, collective_id, name):
    n = len(gbs)
    lands = [lax.empty((4,) + g.shape[1:], g.dtype) for g in gbs]

    def body(*refs):
        x, y, c = _place()
        _handshake([(x, y, 1 - c)])
        for cp in _sibling_copies(refs[:n], refs[n:2 * n], refs[4 * n], refs[4 * n + 1]):
            cp.start()
        refs[4 * n + 2][...] = jnp.zeros(_TOKEN.shape, _TOKEN.dtype)

    res = pl.pallas_call(
        body, name=name, in_specs=[_HBM] * (2 * n), out_specs=[_HBM] * (2 * n) + [_SEM, _SEM, _VMEM],
        out_shape=[pltpu.HBM(a.shape, a.dtype) for a in gbs + lands] + [pltpu.SemaphoreType.DMA((4 * n,))] * 2
        + [_TOKEN],
        input_output_aliases={k: k for k in range(2 * n)},
        compiler_params=pltpu.CompilerParams(has_side_effects=_EFFECT, collective_id=collective_id),
    )(*[pltpu.with_memory_space_constraint(a, pltpu.HBM) for a in gbs + lands])
    return list(res[:n]), list(res[n:2 * n]), res[2 * n], res[2 * n + 1], res[2 * n + 2]


def _sibling_wait(gbs, lands, send_sems, recv_sems, after, name):
    n = len(gbs)

    def body(*refs):
        for cp in _sibling_copies(refs[:n], refs[n:2 * n], refs[2 * n], refs[2 * n + 1]):
            cp.wait_recv()
            cp.wait_send()

    res = pl.pallas_call(
        body, name=name, in_specs=[_HBM] * (2 * n) + [_SEM, _SEM, _ANY], out_specs=[_HBM] * (2 * n),
        out_shape=[pltpu.HBM(a.shape, a.dtype) for a in gbs + lands],
        input_output_aliases={k: k for k in range(2 * n)},
        compiler_params=pltpu.CompilerParams(has_side_effects=_EFFECT),
    )(*gbs, *lands, send_sems, recv_sems, after)
    return list(res[n:])


def _pair_sum(gfs, ts, c_idx, name):
    n = len(gfs)
    halves = 2

    def body(c_ref, *refs):
        for k in range(n):
            refs[2 * n + k][...] = (refs[k][...] + refs[n + k][...].astype(F32)).astype(jnp.bfloat16)

    in_specs, out_specs, out_shape = [], [], []
    for g in gfs:
        rh = g.shape[1] // halves
        in_specs.append(pl.BlockSpec((None, rh, D), lambda s, h, c_ref: (2 * s + c_ref[0], h, 0)))
    for g in gfs:
        rh = g.shape[1] // halves
        in_specs.append(pl.BlockSpec((None, rh, D), lambda s, h, c_ref: (s, h, 0)))
        out_specs.append(pl.BlockSpec((None, rh, D), lambda s, h, c_ref: (s, h, 0)))
        out_shape.append(_sds((4, g.shape[1], D), jnp.bfloat16))
    return pl.pallas_call(
        body, name=name,
        grid_spec=pltpu.PrefetchScalarGridSpec(num_scalar_prefetch=1, grid=(4, halves), in_specs=in_specs,
                                               out_specs=out_specs),
        out_shape=out_shape,
        compiler_params=pltpu.CompilerParams(dimension_semantics=("arbitrary", "arbitrary"),
                                             vmem_limit_bytes=48 * 2 ** 20),
    )(c_idx, *gfs, *ts)


def _chip_copies(pbs, lands, smalls, send_sems, recv_sems):
    x, y, c = _place()
    own = 2 * x + y
    me = _dev_index((x, y, c))
    chips = [(1 - x, y), (x, 1 - y), (1 - x, 1 - y)]
    n = len(pbs)
    copies = [pltpu.make_async_remote_copy(
        src_ref=pbs[k].at[2 * chip[0] + chip[1]], dst_ref=lands[k].at[own], send_sem=send_sems.at[3 * k + j],
        recv_sem=recv_sems.at[3 * k + j], device_id=(*chip, c), device_id_type=MESH)
        for k in range(n) for j, chip in enumerate(chips)]
    peers = [(x, y, 1 - c)] + [(*chip, cc) for chip in chips for cc in (c, 1 - c)]
    copies += [pltpu.make_async_remote_copy(
        src_ref=smalls[t].at[me], dst_ref=smalls[t].at[me], send_sem=send_sems.at[3 * n + 7 * t + j],
        recv_sem=recv_sems.at[3 * n + 7 * t + j], device_id=peer, device_id_type=MESH)
        for t in range(len(smalls)) for j, peer in enumerate(peers)]
    return copies


def _chip_start(pbs, smalls, collective_id, name):
    n, m = len(pbs), len(smalls)
    lands = [lax.empty(p.shape, p.dtype) for p in pbs]
    arrays = list(pbs) + lands + list(smalls)
    na = len(arrays)
    nsem = 3 * n + 7 * m

    def body(*refs):
        x, y, c = _place()
        chips = [(1 - x, y), (x, 1 - y), (1 - x, 1 - y)]
        peers = [(*chip, c) for chip in chips]
        if m:
            peers += [(x, y, 1 - c)] + [(*chip, 1 - c) for chip in chips]
        _handshake(peers)
        for cp in _chip_copies(refs[:n], refs[n:2 * n], refs[2 * n:na], refs[2 * na], refs[2 * na + 1]):
            cp.start()
        refs[2 * na + 2][...] = jnp.zeros(_TOKEN.shape, _TOKEN.dtype)

    res = pl.pallas_call(
        body, name=name, in_specs=[_HBM] * na, out_specs=[_HBM] * na + [_SEM, _SEM, _VMEM],
        out_shape=[pltpu.HBM(a.shape, a.dtype) for a in arrays] + [pltpu.SemaphoreType.DMA((nsem,))] * 2 + [_TOKEN],
        input_output_aliases={k: k for k in range(na)},
        compiler_params=pltpu.CompilerParams(has_side_effects=_EFFECT, collective_id=collective_id),
    )(*[pltpu.with_memory_space_constraint(a, pltpu.HBM) for a in arrays])
    return (list(res[:n]), list(res[n:2 * n]), list(res[2 * n:na]), res[na], res[na + 1]), res[na + 2]


def _chip_wait(pbs, lands, smalls, send_sems, recv_sems, after, name):
    n = len(pbs)
    arrays = list(pbs) + list(lands) + list(smalls)
    na = len(arrays)

    def body(*refs):
        for cp in _chip_copies(refs[:n], refs[n:2 * n], refs[2 * n:na], refs[na], refs[na + 1]):
            cp.wait_recv()
            cp.wait_send()

    res = pl.pallas_call(
        body, name=name, in_specs=[_HBM] * na + [_SEM, _SEM, _ANY], out_specs=[_HBM] * na,
        out_shape=[pltpu.HBM(a.shape, a.dtype) for a in arrays],
        input_output_aliases={k: k for k in range(na)},
        compiler_params=pltpu.CompilerParams(has_side_effects=_EFFECT),
    )(*arrays, send_sems, recv_sems, after)
    return list(res[:n]), list(res[n:2 * n]), list(res[2 * n:])


def _chip_sum(pbs, qs, own_idx, name):
    n = len(pbs)
    halves = 2

    def body(o_ref, *refs):
        for k in range(n):
            acc = refs[4 * k][...].astype(F32)
            for j in range(1, 4):
                acc = acc + refs[4 * k + j][...].astype(F32)
            refs[4 * n + k][...] = acc

    def other(j):
        return lambda h, o: ((o[0] + j) % 4, h, 0)

    in_specs, ins = [], []
    for p, q in zip(pbs, qs):
        rh = p.shape[1] // halves
        in_specs.append(pl.BlockSpec((None, rh, D), lambda h, o: (o[0], h, 0)))
        ins.append(p)
        for j in range(1, 4):
            in_specs.append(pl.BlockSpec((None, rh, D), other(j)))
            ins.append(q)
    out_specs = [pl.BlockSpec((p.shape[1] // halves, D), lambda h, o: (h, 0)) for p in pbs]
    return pl.pallas_call(
        body, name=name,
        grid_spec=pltpu.PrefetchScalarGridSpec(num_scalar_prefetch=1, grid=(halves,), in_specs=in_specs,
                                               out_specs=out_specs),
        out_shape=[_sds((p.shape[1], D), F32) for p in pbs],
        compiler_params=pltpu.CompilerParams(dimension_semantics=("arbitrary",), vmem_limit_bytes=48 * 2 ** 20),
    )(own_idx, *ins)


REDUCE_GROUPS = (("w_gateT", "w_upT", "w_down"), ("w_out", "w_mem_q", "w_mem_kvT", "w_mem_o"), ("w_inT",))


class _Reduce:
    def __init__(self):
        x, y, c = _place()
        self._c = c.astype(jnp.int32).reshape(1)
        self._own = (2 * x + y).astype(jnp.int32).reshape(1)
        self._me = _dev_index((x, y, c))
        self._state = {}
        self.shards = {}
        self.smalls = None

    def start(self, gi, wgrads):
        gfs = [g[0].reshape(NDEV, g[0].shape[0] // NDEV, D) for g in wgrads]
        gbs = [g[1].reshape(NDEV, g[1].shape[0] // NDEV, D) for g in wgrads]
        gbs, lands, send_sems, recv_sems, token = _sibling_start(gbs, 4 + gi, "reduce_sibling_start_%d" % gi)
        self._state[gi] = (gfs, gbs, lands, send_sems, recv_sems)
        return token

    def middle(self, gi, after, smalls=()):
        gfs, gbs, lands, send_sems, recv_sems = self._state[gi]
        ts = _sibling_wait(gbs, lands, send_sems, recv_sems, after, "reduce_sibling_wait_%d" % gi)
        pbs = _pair_sum(gfs, ts, self._c, "reduce_pair_sum_%d" % gi)
        self._state[gi], token = _chip_start(pbs, list(smalls), 7 + gi, "reduce_chip_start_%d" % gi)
        return token

    def finish(self, gi, after):
        pbs, lands, smalls, send_sems, recv_sems = self._state[gi]
        pbs, qs, smalls = _chip_wait(pbs, lands, smalls, send_sems, recv_sems, after, "reduce_chip_wait_%d" % gi)
        self.shards.update(zip(REDUCE_GROUPS[gi], _chip_sum(pbs, qs, self._own, "reduce_chip_sum_%d" % gi)))
        if smalls:
            self.smalls = smalls

    def small_zones(self, small, g_wdw):
        pack = jnp.concatenate(
            [jnp.pad(small[name], ((0, 0), (0, lanes - small[name].shape[1]))) for name, lanes, _ in PACK], axis=1)
        zones = []
        for a in (pack, g_wdw):
            zone = lax.empty((NDEV,) + a.shape, a.dtype)
            zones.append(lax.dynamic_update_slice(zone, a[None], (self._me,) + (0,) * a.ndim))
        return zones


def _adamw_math(w, g, m, v):
    m = ADAM_B1 * m + (1.0 - ADAM_B1) * g
    v = ADAM_B2 * v + (1.0 - ADAM_B2) * (g * g)
    m_hat = m / (1.0 - ADAM_B1 ** ADAM_STEP)
    v_hat = v / (1.0 - ADAM_B2 ** ADAM_STEP)
    delta = -ADAM_LR * (m_hat / (jnp.sqrt(v_hat) + ADAM_EPS) + ADAM_WD * w)
    return delta, m, v


def _adamw(w, g, m, v, name):
    R, C = w.shape
    rc = 256 if R % 256 == 0 and R > 256 else R

    def body(w_ref, g_ref, m_ref, v_ref, d_ref, nm_ref, nv_ref):
        d, nm, nv = _adamw_math(w_ref[...], g_ref[...], m_ref[...], v_ref[...])
        d_ref[...] = d
        nm_ref[...] = nm
        nv_ref[...] = nv

    return _hbm_call(
        body, name=name, grid=(R // rc,),
        in_specs=[_rows(rc, C)] * 4, out_specs=[_rows(rc, C)] * 3,
        out_shape=[_sds((R, C), F32)] * 3,
        compiler_params=_params(40),
    )(w, g, m, v)


PACK = (("loss", LANES, 1), ("g_mix", D, D), ("b_in", INC, INC), ("b_dw", CC, CC), ("g_conv_ln", CC, CC),
        ("b_conv_ln", CC, CC), ("attn_sink", LANES, NQ), ("b_out", D, D), ("g_mem_q", D, D), ("g_mem_kv", D, D),
        ("g_ffn", D, D), ("g_final", D, D))
PACK_LANES = sum(p[1] for p in PACK)


def _small_update(packs, gdws, ws, ms, vs):
    names = [p[0] for p in PACK[1:]]
    n = len(names)

    def body(*refs):
        packs_ref, gdws_ref = refs[0], refs[1]
        w_refs, m_refs, v_refs = refs[2:2 + n], refs[2 + n:2 + 2 * n], refs[2 + 2 * n:2 + 3 * n]
        outs = refs[2 + 3 * n:]
        tot = packs_ref[0]
        gdw = gdws_ref[0]
        for s in range(1, NDEV):
            tot = tot + packs_ref[s]
            gdw = gdw + gdws_ref[s]
        outs[0][...] = tot[:, 0:LANES]
        outs[1][...] = gdw
        off = LANES
        for k, (_, lanes, used) in enumerate(PACK[1:]):
            g = tot[:, off:off + used]
            off += lanes
            d, nm, nv = _adamw_math(w_refs[k][...], g, m_refs[k][...], v_refs[k][...])
            o = outs[2 + 4 * k:6 + 4 * k]
            o[0][...] = g
            o[1][...] = d
            o[2][...] = nm
            o[3][...] = nv

    def full(a):
        nd = len(a.shape)
        return pl.BlockSpec(a.shape, lambda i: (0,) * nd)

    ins = [packs, gdws, *ws, *ms, *vs]
    out_shape = [_sds((1, LANES), F32), _sds(gdws.shape[1:], F32)]
    for w in ws:
        out_shape += [_sds(w.shape, F32)] * 4
    return pl.pallas_call(
        body, name="small_update", grid=(1,),
        in_specs=[full(a) for a in ins], out_specs=[full(o) for o in out_shape], out_shape=out_shape,
        compiler_params=_params(40),
    )(*ins)


def kernel(x, mem, g_mix, w_in, b_in, w_dw, b_dw, g_conv_ln, b_conv_ln, attn_sink, w_out, b_out, g_mem_q, g_mem_kv, w_mem_q, w_mem_kv, w_mem_o, g_ffn, w_gate, w_up, w_down, g_final, loss_target, m_g_mix, m_w_in, m_b_in, m_w_dw, m_b_dw, m_g_conv_ln, m_b_conv_ln, m_attn_sink, m_w_out, m_b_out, m_g_mem_q, m_g_mem_kv, m_w_mem_q, m_w_mem_kv, m_w_mem_o, m_g_ffn, m_w_gate, m_w_up, m_w_down, m_g_final, v_g_mix, v_w_in, v_b_in, v_w_dw, v_b_dw, v_g_conv_ln, v_b_conv_ln, v_attn_sink, v_w_out, v_b_out, v_g_mem_q, v_g_mem_kv, v_w_mem_q, v_w_mem_kv, v_w_mem_o, v_g_ffn, v_w_gate, v_w_up, v_w_down, v_g_final):
    given = dict(locals())
    bf = jnp.bfloat16
    xi, yi, ci = _place()
    me = 4 * xi + 2 * yi + ci

    transposed = {"w_inT": w_in, "w_mem_kvT": w_mem_kv, "w_gateT": w_gate, "w_upT": w_up}
    plain = {"w_out": w_out, "w_mem_q": w_mem_q, "w_mem_o": w_mem_o, "w_down": w_down}
    blocks = {n: (transposed[n][0].T if n in transposed else plain[n][0]).astype(bf) for n in W_NAMES}
    blocks["w_dw"] = jnp.pad(w_dw[0], ((0, 1), (0, 0)))

    def landing_zone(b):
        zone = lax.empty((NDEV,) + b.shape, b.dtype)
        return lax.dynamic_update_slice(zone, b[None], (me,) + (0,) * b.ndim)

    gather = _Gather(_gather_start([[landing_zone(blocks[n]) for n in names] for names in GATHER_GROUPS]))

    sp = {n: given[n] for n in ("g_mix", "b_in", "b_dw", "g_conv_ln", "b_conv_ln", "attn_sink", "b_out", "g_mem_q",
                                "g_mem_kv", "g_ffn")}
    sp["g_final"] = g_final.reshape(1, D)
    reduce = _Reduce()
    grad_x = _local_step(x[0], mem[0], loss_target[0], sp, gather, reduce)
    gsh = reduce.shards

    grads, deltas, new_m, new_v = {}, {}, {}, {}

    def update(n, g, transposed):
        w, m, v = given[n][0], given["m_" + n][0], given["v_" + n][0]
        if transposed:
            d, nm, nv = _adamw(w.T, g, m.T, v.T, "adamw_" + n)
            grads[n], deltas[n], new_m[n], new_v[n] = g.T[None], d.T[None], nm.T[None], nv.T[None]
        else:
            d, nm, nv = _adamw(w, g, m, v, "adamw_" + n)
            grads[n], deltas[n], new_m[n], new_v[n] = g[None], d[None], nm[None], nv[None]
        return d

    for n in ("w_gate", "w_up"):
        update(n, gsh[n + "T"], True)
    for n in ("w_down", "w_out", "w_mem_q", "w_mem_o"):
        update(n, gsh[n], False)
    last = update("w_mem_kv", gsh["w_mem_kvT"].T, False)
    reduce.finish(2, last)
    update("w_in", reduce.shards["w_inT"], True)

    packs, gdws = reduce.smalls
    names_small = [p[0] for p in PACK[1:]]

    def as_row(a):
        return a.reshape(1, -1)

    res = _small_update(packs, gdws, [as_row(given[n]) for n in names_small],
                        [as_row(given["m_" + n]) for n in names_small],
                        [as_row(given["v_" + n]) for n in names_small])
    loss = res[0][0, 0]
    for k, n in enumerate(names_small):
        shape = given[n].shape
        g, d, nm, nv = res[2 + 4 * k:6 + 4 * k]
        grads[n], deltas[n], new_m[n], new_v[n] = (g.reshape(shape), d.reshape(shape), nm.reshape(shape),
                                                   nv.reshape(shape))
    update("w_dw", lax.dynamic_slice(res[1], (0, me * (CC // NDEV)), (TAPS, CC // NDEV)), False)

    order = ("g_mix", "w_in", "b_in", "w_dw", "b_dw", "g_conv_ln", "b_conv_ln", "attn_sink", "w_out", "b_out",
             "g_mem_q", "g_mem_kv", "w_mem_q", "w_mem_kv", "w_mem_o", "g_ffn", "w_gate", "w_up", "w_down", "g_final")
    return (loss, grad_x[None], *[grads[n] for n in order], *[deltas[n] for n in order],
            *[new_m[n] for n in order], *[new_v[n] for n in order])
```

```python
import functools

import jax
import jax.numpy as jnp
from jax import lax
from jax.experimental import pallas as pl
from jax.experimental.pallas import tpu as pltpu

D = 1024
CC = 512
HD = 64
NQ = 8
NKV = 2
GRP = NQ // NKV
ATT = NQ * HD
KVC = NKV * HD
BLK = 128
MH = 4
MHD = D // MH
F = 2816
FCH = F // 2
INC = 2 * CC + ATT + 2 * KVC
TAPS = 31
EPS = 1e-6
ROPE_THETA = 10000.0
SCALE = HD ** -0.5
MSCALE = MHD ** -0.5
LANES = 128
NDEV = 8

ADAM_LR = 0.001
ADAM_B1 = 0.9
ADAM_B2 = 0.999
ADAM_EPS = 1e-08
ADAM_WD = 0.01
ADAM_STEP = 10

F32 = jnp.float32
_MXU = jnp.bfloat16
MESH = pl.DeviceIdType.MESH

W_NAMES = ("w_inT", "w_out", "w_mem_q", "w_mem_kvT", "w_mem_o", "w_gateT", "w_upT", "w_down")
W_ROWS = (INC // NDEV, D // NDEV, D // NDEV, 2 * D // NDEV, D // NDEV, F // NDEV, F // NDEV, F // NDEV)


def _mm(a, b, ca, cb):
    return lax.dot_general(a.astype(_MXU), b.astype(_MXU), (((ca,), (cb,)), ((), ())),
                           preferred_element_type=F32)


def _nn(a, b):
    return _mm(a, b, 1, 0)


def _nt(a, b):
    return _mm(a, b, 1, 1)


def _tn(a, b):
    return _mm(a, b, 0, 0)


def _sigmoid(x):
    return 1.0 / (1.0 + jnp.exp(-x))


def _rms(x, g):
    r = lax.rsqrt(jnp.mean(x * x, axis=-1, keepdims=True) + EPS)
    return x * r * g, r


def _rms_bwd(dy, x, r, g):
    xh = x * r
    dg = jnp.sum(dy * xh, axis=0, keepdims=True)
    dxh = dy * g
    dx = r * (dxh - xh * jnp.mean(dxh * xh, axis=-1, keepdims=True))
    return dx, dg


def _rope(t, c, s):
    n = t.shape[1]
    reps = n // LANES
    if reps > 1:
        c = jnp.tile(c, (1, reps))
        s = jnp.tile(s, (1, reps))
    lane = lax.broadcasted_iota(jnp.int32, t.shape, 1)
    first = (lane & (HD - 1)) < (HD // 2)
    partner = jnp.where(first, pltpu.roll(t, n - HD // 2, 1), pltpu.roll(t, HD // 2, 1))
    return t * c + partner * s


def _rope_bwd(dt, c, s):
    n = dt.shape[1]
    reps = n // LANES
    if reps > 1:
        c = jnp.tile(c, (1, reps))
        s = jnp.tile(s, (1, reps))
    lane = lax.broadcasted_iota(jnp.int32, dt.shape, 1)
    first = (lane & (HD - 1)) < (HD // 2)
    ds = dt * s
    partner = jnp.where(first, pltpu.roll(ds, n - HD // 2, 1), pltpu.roll(ds, HD // 2, 1))
    return dt * c + partner


def _acc_init(i, *refs):
    @pl.when(i == 0)
    def _():
        for r in refs:
            r[...] = jnp.zeros(r.shape, r.dtype)


def _acc_row(ref, v):
    ref[...] += jnp.broadcast_to(v, ref.shape)


def _rows(tm, n):
    return pl.BlockSpec((tm, n), lambda i: (i, 0))


def _full(shape):
    nd = len(shape)
    return pl.BlockSpec(shape, lambda i: (0,) * nd)


def _once(shape):
    nd = len(shape)
    return pl.BlockSpec(shape, lambda i: (0,) * nd, pipeline_mode=pl.Buffered(1))


def _params(vmem_mb):
    return pltpu.CompilerParams(dimension_semantics=("arbitrary",), vmem_limit_bytes=vmem_mb * 2 ** 20)


def _sds(shape, dtype):
    return pltpu.HBM(shape, dtype)


def _hbm_call(*args, **kwargs):
    call = pl.pallas_call(*args, **kwargs)
    return lambda *ops: call(*[pltpu.with_memory_space_constraint(o, pltpu.HBM) for o in ops])


def _mem_kv_fwd(mem, g_kv, wkvT):
    M = mem.shape[0]

    def body(mem_ref, g_ref, w_ref, km_ref, vm_ref, mn_ref):
        y, _ = _rms(mem_ref[...], g_ref[...])
        kv = _nt(y, w_ref[...])
        km_ref[...] = kv[:, :D].astype(km_ref.dtype)
        vm_ref[...] = kv[:, D:].astype(vm_ref.dtype)
        mn_ref[...] = y.astype(mn_ref.dtype)

    return _hbm_call(
        body, name="mem_kv_fwd", grid=(1,),
        in_specs=[_full((M, D)), _full((1, D)), _full((2 * D, D))],
        out_specs=[_full((M, D))] * 3,
        out_shape=[_sds((M, D), _MXU)] * 3,
        compiler_params=_params(40),
    )(mem, g_kv, wkvT)


def _in_proj_fwd(x, g_mix, winT, b_in, cosq, sinq, tm=512):
    S = x.shape[0]

    def body(x_ref, g_ref, w_ref, b_ref, c_ref, s_ref, h_ref, uglu_ref, vg_ref, q_ref, k_ref, v_ref):
        h, _ = _rms(x_ref[...], g_ref[...])
        hb = h.astype(h_ref.dtype)
        h_ref[...] = hb
        u = _nt(hb, w_ref[...]) + b_ref[...]
        uglu_ref[...] = u[:, :2 * CC]
        vg_ref[...] = u[:, :CC] * _sigmoid(u[:, CC:2 * CC])
        c = c_ref[...]
        s = s_ref[...]
        q_ref[...] = _rope(u[:, 2 * CC:2 * CC + ATT], c, s).astype(q_ref.dtype)
        k_ref[...] = _rope(u[:, 2 * CC + ATT:2 * CC + ATT + KVC], c, s).astype(k_ref.dtype)
        v_ref[...] = u[:, 2 * CC + ATT + KVC:].astype(v_ref.dtype)

    return _hbm_call(
        body, name="in_proj_fwd", grid=(S // tm,),
        in_specs=[_rows(tm, D), _full((1, D)), _full((INC, D)), _full((1, INC)), _rows(tm, LANES), _rows(tm, LANES)],
        out_specs=[_rows(tm, D), _rows(tm, 2 * CC), _rows(tm, CC), _rows(tm, ATT), _rows(tm, KVC), _rows(tm, KVC)],
        out_shape=[_sds((S, D), _MXU), _sds((S, 2 * CC), F32), _sds((S, CC), F32), _sds((S, ATT), _MXU),
                   _sds((S, KVC), _MXU), _sds((S, KVC), _MXU)],
        compiler_params=_params(48),
    )(x, g_mix, winT, b_in, cosq, sinq)


CONV_HALO = 16


CONV_TILE = 256


def _shift_scratch(tm, sets):
    return pltpu.VMEM((sets, CC // LANES, 8, tm + 24, LANES), F32)


def _shifted_copies(pad_ref, sh_ref, t0, tm):
    for cl in range(CC // LANES):
        blk = pad_ref[pl.ds(t0, tm + 2 * CONV_HALO), cl * LANES:(cl + 1) * LANES]
        for b in range(8):
            sh_ref[cl, b, :, :] = blk[b:b + tm + 24, :]


def _conv_taps():
    return [(8 * a + b - 1, b, 8 * a) for b in range(8) for a in range(4) if 0 <= 8 * a + b - 1 < TAPS]


CONV_ROWS = 64
SUBLANES = 8


def _dwconv_tile(sh_ref, tm, w_ref, flip, out_ref):
    for cl in range(CC // LANES):
        lanes = slice(cl * LANES, (cl + 1) * LANES)

        def chunk(c, carry):
            r = pl.multiple_of(c * CONV_ROWS, CONV_ROWS)
            accs = [None] * 4
            for n, (k, b, off) in enumerate(_conv_taps()):
                kk = TAPS - 1 - k if flip else k
                term = sh_ref[cl, b, pl.ds(off + r, CONV_ROWS), :] * w_ref[kk:kk + 1, lanes]
                accs[n % 4] = term if accs[n % 4] is None else accs[n % 4] + term
            out_ref[pl.ds(r, CONV_ROWS), lanes] = (accs[0] + accs[1]) + (accs[2] + accs[3])
            return carry

        lax.fori_loop(0, tm // CONV_ROWS, chunk, 0)


def _fill_padded(i, src_hbm, pad_ref, sem, S):
    @pl.when(i == 0)
    def _():
        zeros = jnp.zeros((CONV_HALO, pad_ref.shape[1]), pad_ref.dtype)
        pad_ref[0:CONV_HALO, :] = zeros
        pad_ref[CONV_HALO + S:2 * CONV_HALO + S, :] = zeros
        cp = pltpu.make_async_copy(src_hbm, pad_ref.at[pl.ds(CONV_HALO, S), :], sem)
        cp.start()
        cp.wait()


def _ln_silu(y, g, b):
    mu = jnp.mean(y, axis=-1, keepdims=True)
    yc = y - mu
    var = jnp.mean(yc * yc, axis=-1, keepdims=True)
    rstd = lax.rsqrt(var + EPS)
    yh = yc * rstd
    z = yh * g + b
    return yh, rstd, z


def _conv_fwd(vg, w_dw, b_dw, ln_g, ln_b):
    S = vg.shape[0]
    tm = min(CONV_TILE, S)

    def body(vg_hbm, w_ref, b_ref, g_ref, bb_ref, y_ref, yc_ref, pad_ref, sh_ref, sem):
        i = pl.program_id(0)
        _fill_padded(i, vg_hbm, pad_ref, sem, S)
        _shifted_copies(pad_ref, sh_ref.at[0], pl.multiple_of(i * tm, tm), tm)
        _dwconv_tile(sh_ref.at[0], tm, w_ref, False, y_ref)
        y = y_ref[...] + b_ref[...]
        y_ref[...] = y
        _, _, z = _ln_silu(y, g_ref[...], bb_ref[...])
        yc_ref[...] = (z * _sigmoid(z)).astype(yc_ref.dtype)

    return _hbm_call(
        body, name="conv_fwd", grid=(S // tm,),
        in_specs=[pl.BlockSpec(memory_space=pl.ANY), _full((TAPS, CC)), _full((1, CC)), _full((1, CC)), _full((1, CC))],
        out_specs=[_rows(tm, CC), _rows(tm, CC)],
        out_shape=[_sds((S, CC), F32), _sds((S, CC), _MXU)],
        scratch_shapes=[pltpu.VMEM((S + 2 * CONV_HALO, CC), F32), _shift_scratch(tm, 1), pltpu.SemaphoreType.DMA],
        compiler_params=_params(40),
    )(vg, w_dw, b_dw, ln_g, ln_b)


def _attn_window(i, S):
    start = jnp.clip((i - 1) * BLK, 0, S - 3 * BLK)
    return pl.multiple_of(start, BLK)


def _attn_mask(i, start):
    keys = lax.broadcasted_iota(jnp.int32, (3 * BLK, GRP * BLK), 0)
    cols = lax.broadcasted_iota(jnp.int32, (3 * BLK, GRP * BLK), 1)
    rel = (start + keys) - (i * BLK + (cols & (BLK - 1)))
    return jnp.abs(rel) <= BLK


def _stack_heads(t, hk):
    return jnp.concatenate([t[:, (hk * GRP + g) * HD:(hk * GRP + g + 1) * HD] for g in range(GRP)], axis=0)


def _unstack_heads_t(t):
    cols = []
    for p in range(GRP // 2):
        pair = jnp.concatenate([t[:, (2 * p) * BLK:(2 * p + 1) * BLK], t[:, (2 * p + 1) * BLK:(2 * p + 2) * BLK]],
                               axis=0)
        cols.append(pair.T)
    return jnp.concatenate(cols, axis=1)


def _attn_probs(qs, kh, sk_ref, hk, mask):
    s = _nt(kh, qs) * SCALE
    s = jnp.where(mask, s, -1e30)
    sk = jnp.concatenate(
        [jnp.broadcast_to(sk_ref[0:1, hk * GRP + g:hk * GRP + g + 1], (1, BLK)) for g in range(GRP)], axis=1)
    m = jnp.maximum(jnp.max(s, axis=0, keepdims=True), sk)
    e = jnp.exp(s - m)
    es = jnp.exp(sk - m)
    inv = 1.0 / (jnp.sum(e, axis=0, keepdims=True) + es)
    return e, inv, es


ATTN_BLOCKS = 4


def _attn_fwd(qr, kr, vv, sink):
    S = qr.shape[0]
    tq = ATTN_BLOCKS * BLK

    def body(q_ref, k_ref, v_ref, sk_ref, o_ref):
        for b in range(ATTN_BLOCKS):
            i = pl.program_id(0) * ATTN_BLOCKS + b
            rows = slice(b * BLK, (b + 1) * BLK)
            start = _attn_window(i, S)
            kb = k_ref[pl.ds(start, 3 * BLK), :]
            vb = v_ref[pl.ds(start, 3 * BLK), :]
            q = q_ref[rows, :]
            mask = _attn_mask(i, start)
            outs = []
            for hk in range(NKV):
                hs = slice(hk * HD, (hk + 1) * HD)
                e, inv, _ = _attn_probs(_stack_heads(q, hk), kb[:, hs], sk_ref, hk, mask)
                outs.append(_unstack_heads_t(_tn(vb[:, hs], e) * inv))
            o_ref[rows, :] = jnp.concatenate(outs, axis=1).astype(o_ref.dtype)

    return _hbm_call(
        body, name="attn_fwd", grid=(S // tq,),
        in_specs=[_rows(tq, ATT), _full((S, KVC)), _full((S, KVC)), _full((1, NQ))],
        out_specs=_rows(tq, ATT),
        out_shape=_sds((S, ATT), _MXU),
        compiler_params=_params(40),
    )(qr, kr, vv, sink)


def _out_proj_fwd(x, yc, ya, wout, b_out, tm=512):
    S = x.shape[0]

    def body(x_ref, yc_ref, ya_ref, w_ref, b_ref, x1_ref):
        x1_ref[...] = (x_ref[...] + _nn(yc_ref[...], w_ref[0:CC, :]) + _nn(ya_ref[...], w_ref[CC:2 * CC, :])
                       + b_ref[...])

    return _hbm_call(
        body, name="out_proj_fwd", grid=(S // tm,),
        in_specs=[_rows(tm, D), _rows(tm, CC), _rows(tm, ATT), _full((D, D)), _full((1, D))],
        out_specs=_rows(tm, D),
        out_shape=_sds((S, D), F32),
        compiler_params=_params(40),
    )(x, yc, ya, wout, b_out)


def _softmax(s):
    m = jnp.max(s, axis=-1, keepdims=True)
    e = jnp.exp(s - m)
    return e * (1.0 / jnp.sum(e, axis=-1, keepdims=True))


def _mem_attn_fwd(x1, g_q, wq, km, vm, wo, tm=512):
    S = x1.shape[0]
    M = km.shape[0]

    def body(x1_ref, g_ref, wq_ref, km_ref, vm_ref, wo_ref, x2_ref, qm_ref, om_ref):
        x1v = x1_ref[...]
        hq, _ = _rms(x1v, g_ref[...])
        qm = _nn(hq, wq_ref[...]).astype(qm_ref.dtype)
        qm_ref[...] = qm
        oms = []
        for h in range(MH):
            sl = slice(h * MHD, (h + 1) * MHD)
            p = _softmax(_nt(qm[:, sl], km_ref[:, sl]) * MSCALE)
            oms.append(_nn(p, vm_ref[:, sl]))
        om = jnp.concatenate(oms, axis=1).astype(om_ref.dtype)
        om_ref[...] = om
        x2_ref[...] = x1v + _nn(om, wo_ref[...])

    return _hbm_call(
        body, name="mem_attn_fwd", grid=(S // tm,),
        in_specs=[_rows(tm, D), _full((1, D)), _full((D, D)), _full((M, D)), _full((M, D)), _full((D, D))],
        out_specs=[_rows(tm, D)] * 3,
        out_shape=[_sds((S, D), F32), _sds((S, D), _MXU), _sds((S, D), _MXU)],
        compiler_params=_params(48),
    )(x1, g_q, wq, km, vm, wo)


def _ffn_loss_fwd(x2, tgt, g_ffn, g_final, wgT, wuT, wd, tm=256):
    S = x2.shape[0]

    def body(x2_ref, t_ref, gf_ref, gl_ref, wg_ref, wu_ref, wd_ref,
             gt_ref, up_ref, hf_ref, dx3_ref, dx3b_ref, loss_ref, dgl_ref):
        i = pl.program_id(0)
        _acc_init(i, loss_ref, dgl_ref)
        x2v = x2_ref[...]
        hf, _ = _rms(x2v, gf_ref[...])
        hb = hf.astype(hf_ref.dtype)
        hf_ref[...] = hb
        x3 = x2v
        for c in range(F // FCH):
            sl = slice(c * FCH, (c + 1) * FCH)
            gt = _nt(hb, wg_ref[sl, :])
            up = _nt(hb, wu_ref[sl, :])
            gt_ref[:, sl] = gt.astype(gt_ref.dtype)
            up_ref[:, sl] = up.astype(up_ref.dtype)
            x3 = x3 + _nn(gt * _sigmoid(gt) * up, wd_ref[sl, :])
        gl = gl_ref[...]
        y, r = _rms(x3, gl)
        diff = y - t_ref[...]
        _acc_row(loss_ref, jnp.sum(jnp.sum(diff * diff, axis=-1, keepdims=True), axis=0, keepdims=True) * (0.5 / D))
        dx3, dgl = _rms_bwd(diff * (1.0 / D), x3, r, gl)
        dx3_ref[...] = dx3
        dx3b_ref[...] = dx3.astype(dx3b_ref.dtype)
        _acc_row(dgl_ref, dgl)

    return _hbm_call(
        body, name="ffn_loss_fwd", grid=(S // tm,),
        in_specs=[_rows(tm, D), _rows(tm, D), _full((1, D)), _full((1, D)),
                  _once((F, D)), _once((F, D)), _once((F, D))],
        out_specs=[_rows(tm, F), _rows(tm, F), _rows(tm, D), _rows(tm, D), _rows(tm, D),
                   _full((8, LANES)), _full((8, D))],
        out_shape=[_sds((S, F), _MXU), _sds((S, F), _MXU), _sds((S, D), _MXU), _sds((S, D), F32),
                   _sds((S, D), _MXU), _sds((8, LANES), F32), _sds((8, D), F32)],
        compiler_params=_params(56),
    )(x2, tgt, g_ffn, g_final, wgT, wuT, wd)


def _ffn_bwd(dx3, x2, gt, up, g_ffn, wd, wgT, wuT, tm=256):
    S = x2.shape[0]

    def body(dx3_ref, x2_ref, gt_ref, up_ref, g_ref, wd_ref, wg_ref, wu_ref,
             dx2_ref, dx2b_ref, dgt_ref, dup_ref, act_ref, dg_ref):
        i = pl.program_id(0)
        _acc_init(i, dg_ref)
        dx3v = dx3_ref[...]
        db = dx3v.astype(_MXU)
        dhf = jnp.zeros((tm, D), F32)
        for c in range(F // FCH):
            sl = slice(c * FCH, (c + 1) * FCH)
            gtv = gt_ref[:, sl].astype(F32)
            upv = up_ref[:, sl].astype(F32)
            dact = _nt(db, wd_ref[sl, :])
            sig = _sigmoid(gtv)
            silu = gtv * sig
            dup = (dact * silu).astype(dup_ref.dtype)
            dgt = (dact * upv * (sig * (1.0 + gtv * (1.0 - sig)))).astype(dgt_ref.dtype)
            act_ref[:, sl] = (silu * upv).astype(act_ref.dtype)
            dgt_ref[:, sl] = dgt
            dup_ref[:, sl] = dup
            dhf = dhf + _nn(dgt, wg_ref[sl, :]) + _nn(dup, wu_ref[sl, :])
        x2v = x2_ref[...]
        g = g_ref[...]
        _, r = _rms(x2v, g)
        dxn, dg = _rms_bwd(dhf, x2v, r, g)
        dx2 = dx3v + dxn
        dx2_ref[...] = dx2
        dx2b_ref[...] = dx2.astype(dx2b_ref.dtype)
        _acc_row(dg_ref, dg)

    return _hbm_call(
        body, name="ffn_bwd", grid=(S // tm,),
        in_specs=[_rows(tm, D), _rows(tm, D), _rows(tm, F), _rows(tm, F), _full((1, D)),
                  _once((F, D)), _once((F, D)), _once((F, D))],
        out_specs=[_rows(tm, D), _rows(tm, D), _rows(tm, F), _rows(tm, F), _rows(tm, F), _full((8, D))],
        out_shape=[_sds((S, D), F32), _sds((S, D), _MXU), _sds((S, F), _MXU), _sds((S, F), _MXU),
                   _sds((S, F), _MXU), _sds((8, D), F32)],
        compiler_params=_params(56),
    )(dx3, x2, gt, up, g_ffn, wd, wgT, wuT)


_AFTER = pl.BlockSpec(memory_space=pl.ANY)


def _wgrad(a, b, name, after, rc=256):
    S, R = a.shape
    N = b.shape[1]

    def body(a_ref, b_ref, after_ref, o_ref, ob_ref):
        g = _tn(a_ref[...], b_ref[...])
        o_ref[...] = g
        ob_ref[...] = g.astype(ob_ref.dtype)

    return _hbm_call(
        body, name=name, grid=(R // rc,),
        in_specs=[pl.BlockSpec((S, rc), lambda j: (0, j)), _once((S, N)), _AFTER],
        out_specs=[_rows(rc, N), _rows(rc, N)],
        out_shape=[_sds((R, N), F32), _sds((R, N), jnp.bfloat16)],
        compiler_params=_params(48),
    )(a, b, after)


def _mem_attn_bwd(dx2, x1, qm, km, vm, g_q, wq, wo, wout, after, tm=512):
    S = x1.shape[0]
    M = km.shape[0]

    def body(dx2_ref, x1_ref, qm_ref, km_ref, vm_ref, g_ref, wq_ref, wo_ref, wout_ref, after_ref,
             dx1_ref, dx1b_ref, hq_ref, dqm_ref, dyc_ref, dya_ref, dkm_ref, dvm_ref, dg_ref, dbo_ref):
        i = pl.program_id(0)
        _acc_init(i, dkm_ref, dvm_ref, dg_ref, dbo_ref)
        dx2v = dx2_ref[...]
        dom = _nt(dx2v, wo_ref[...]).astype(_MXU)
        dqs = []
        for h in range(MH):
            sl = slice(h * MHD, (h + 1) * MHD)
            qh = qm_ref[:, sl]
            kh = km_ref[:, sl]
            p = _softmax(_nt(qh, kh) * MSCALE)
            dp = _nt(dom[:, sl], vm_ref[:, sl])
            ds = (p * (dp - jnp.sum(dp * p, axis=-1, keepdims=True))).astype(_MXU)
            dqs.append(_nn(ds, kh) * MSCALE)
            dkm_ref[:, sl] += _tn(ds, qh) * MSCALE
            dvm_ref[:, sl] += _tn(p, dom[:, sl])
        dqm = jnp.concatenate(dqs, axis=1).astype(dqm_ref.dtype)
        dqm_ref[...] = dqm
        dhq = _nt(dqm, wq_ref[...])
        x1v = x1_ref[...]
        g = g_ref[...]
        hq, r = _rms(x1v, g)
        hq_ref[...] = hq.astype(hq_ref.dtype)
        dxn, dg = _rms_bwd(dhq, x1v, r, g)
        dx1 = dx2v + dxn
        dx1_ref[...] = dx1
        dx1b = dx1.astype(dx1b_ref.dtype)
        dx1b_ref[...] = dx1b
        dym = _nt(dx1b, wout_ref[...])
        dyc_ref[...] = dym[:, :CC]
        dya_ref[...] = dym[:, CC:].astype(dya_ref.dtype)
        _acc_row(dg_ref, dg)
        _acc_row(dbo_ref, jnp.sum(dx1, axis=0, keepdims=True))

    return _hbm_call(
        body, name="mem_attn_bwd", grid=(S // tm,),
        in_specs=[_rows(tm, D), _rows(tm, D), _rows(tm, D), _full((M, D)), _full((M, D)), _full((1, D)),
                  _once((D, D)), _once((D, D)), _once((D, D)), _AFTER],
        out_specs=[_rows(tm, D), _rows(tm, D), _rows(tm, D), _rows(tm, D), _rows(tm, CC), _rows(tm, ATT),
                   _full((M, D)), _full((M, D)), _full((8, D)), _full((8, D))],
        out_shape=[_sds((S, D), F32), _sds((S, D), _MXU), _sds((S, D), _MXU), _sds((S, D), _MXU),
                   _sds((S, CC), F32), _sds((S, ATT), _MXU),
                   _sds((M, D), F32), _sds((M, D), F32), _sds((8, D), F32), _sds((8, D), F32)],
        compiler_params=_params(56),
    )(dx2, x1, qm, km, vm, g_q, wq, wo, wout, after)


def _mem_kv_bwd(dkm, dvm, mem, g_kv, memn, wkvT):
    M = mem.shape[0]

    def body(dkm_ref, dvm_ref, mem_ref, g_ref, mn_ref, w_ref, gw_ref, gwb_ref, dg_ref):
        dkv = jnp.concatenate([dkm_ref[...], dvm_ref[...]], axis=1).astype(_MXU)
        gw = _tn(dkv, mn_ref[...])
        gw_ref[...] = gw
        gwb_ref[...] = gw.astype(gwb_ref.dtype)
        dmn = _nn(dkv, w_ref[...])
        memv = mem_ref[...]
        g = g_ref[...]
        _, r = _rms(memv, g)
        _, dg = _rms_bwd(dmn, memv, r, g)
        dg_ref[...] = jnp.broadcast_to(dg, dg_ref.shape)

    return _hbm_call(
        body, name="mem_kv_bwd", grid=(1,),
        in_specs=[_full((M, D)), _full((M, D)), _full((M, D)), _full((1, D)), _full((M, D)), _full((2 * D, D))],
        out_specs=[_full((2 * D, D)), _full((2 * D, D)), _full((8, D))],
        out_shape=[_sds((2 * D, D), F32), _sds((2 * D, D), jnp.bfloat16), _sds((8, D), F32)],
        compiler_params=_params(48),
    )(dkm, dvm, mem, g_kv, memn, wkvT)


def _attn_bwd(qr, kr, vv, sink, dya, after):
    S = qr.shape[0]
    tq = ATTN_BLOCKS * BLK

    def body(q_ref, k_ref, v_ref, sk_ref, do_ref, after_ref, dq_ref, dk_ref, dv_ref, dsk_ref, sink_acc):
        _acc_init(pl.program_id(0), dk_ref, dv_ref, sink_acc)
        for b in range(ATTN_BLOCKS):
            i = pl.program_id(0) * ATTN_BLOCKS + b
            rows = slice(b * BLK, (b + 1) * BLK)
            start = _attn_window(i, S)
            kb = k_ref[pl.ds(start, 3 * BLK), :]
            vb = v_ref[pl.ds(start, 3 * BLK), :]
            q = q_ref[rows, :]
            do = do_ref[rows, :]
            mask = _attn_mask(i, start)
            dqs, dks, dvs = [], [], []
            for hk in range(NKV):
                hs = slice(hk * HD, (hk + 1) * HD)
                qs = _stack_heads(q, hk)
                dos = _stack_heads(do, hk)
                e, inv, es = _attn_probs(qs, kb[:, hs], sk_ref, hk, mask)
                p = e * inv
                dp = _nt(vb[:, hs], dos)
                delta = jnp.sum(dp * p, axis=0, keepdims=True)
                ds = (p * (dp - delta)).astype(_MXU)
                dqs.append(_unstack_heads_t(_tn(kb[:, hs], ds) * SCALE))
                dks.append(_nn(ds, qs) * SCALE)
                dvs.append(_nn(p, dos))
                sink_acc[hk:hk + 1, :] -= es * inv * delta
            dq_ref[rows, :] = jnp.concatenate(dqs, axis=1)
            dk_ref[pl.ds(start, 3 * BLK), :] += jnp.concatenate(dks, axis=1)
            dv_ref[pl.ds(start, 3 * BLK), :] += jnp.concatenate(dvs, axis=1)

        @pl.when(pl.program_id(0) == pl.num_programs(0) - 1)
        def _():
            for h in range(NQ):
                tot = jnp.sum(sink_acc[h // GRP:h // GRP + 1, (h % GRP) * BLK:(h % GRP + 1) * BLK], axis=1,
                              keepdims=True)
                dsk_ref[h:h + 1, :] = jnp.broadcast_to(tot, (1, LANES))

    return _hbm_call(
        body, name="attn_bwd", grid=(S // tq,),
        in_specs=[_rows(tq, ATT), _full((S, KVC)), _full((S, KVC)), _full((1, NQ)), _rows(tq, ATT), _AFTER],
        out_specs=[_rows(tq, ATT), _full((S, KVC)), _full((S, KVC)), _full((8, LANES))],
        out_shape=[_sds((S, ATT), F32), _sds((S, KVC), F32), _sds((S, KVC), F32), _sds((8, LANES), F32)],
        scratch_shapes=[pltpu.VMEM((8, GRP * BLK), F32)],
        compiler_params=_params(40),
    )(qr, kr, vv, sink, dya, after)


def _conv_norm_bwd(dyc, y, ln_g, ln_b, after, tm=512):
    S = y.shape[0]

    def body(dyc_ref, y_ref, g_ref, b_ref, after_ref, dy_ref, dg_ref, db_ref, dbdw_ref):
        i = pl.program_id(0)
        _acc_init(i, dg_ref, db_ref, dbdw_ref)
        g = g_ref[...]
        yh, rstd, z = _ln_silu(y_ref[...], g, b_ref[...])
        sig = _sigmoid(z)
        dz = dyc_ref[...] * (sig * (1.0 + z * (1.0 - sig)))
        dyh = dz * g
        dy = rstd * (dyh - jnp.mean(dyh, axis=-1, keepdims=True)
                     - yh * jnp.mean(dyh * yh, axis=-1, keepdims=True))
        dy_ref[...] = dy
        _acc_row(dg_ref, jnp.sum(dz * yh, axis=0, keepdims=True))
        _acc_row(db_ref, jnp.sum(dz, axis=0, keepdims=True))
        _acc_row(dbdw_ref, jnp.sum(dy, axis=0, keepdims=True))

    return _hbm_call(
        body, name="conv_norm_bwd", grid=(S // tm,),
        in_specs=[_rows(tm, CC), _rows(tm, CC), _full((1, CC)), _full((1, CC)), _AFTER],
        out_specs=[_rows(tm, CC), _full((8, CC)), _full((8, CC)), _full((8, CC))],
        out_shape=[_sds((S, CC), F32), _sds((8, CC), F32), _sds((8, CC), F32), _sds((8, CC), F32)],
        compiler_params=_params(40),
    )(dyc, y, ln_g, ln_b, after)


def _conv_bwd(dy, vg, uglu, w_dw):
    S = dy.shape[0]
    tm = min(CONV_TILE, S)

    def body(dy_hbm, vg_hbm, uglu_ref, w_ref, du_ref, gw_ref, dbin_ref, dyp_ref, vgp_ref, gacc_ref, sh_ref, dvg_ref,
             sems):
        i = pl.program_id(0)
        _acc_init(i, gacc_ref, dbin_ref)
        _fill_padded(i, dy_hbm, dyp_ref, sems.at[0], S)
        _fill_padded(i, vg_hbm, vgp_ref, sems.at[1], S)
        t0 = pl.multiple_of(i * tm, tm)
        _shifted_copies(dyp_ref, sh_ref.at[0], t0, tm)
        _shifted_copies(vgp_ref, sh_ref.at[1], t0, tm)
        _dwconv_tile(sh_ref.at[0], tm, w_ref, True, dvg_ref)
        dvg = dvg_ref[...]
        sub = CONV_ROWS
        for cl in range(CC // LANES):
            lanes = slice(cl * LANES, (cl + 1) * LANES)
            for k, b, off in _conv_taps():
                part = jnp.zeros((8, LANES), F32)
                for r in range(0, tm, sub):
                    prod = (sh_ref[0, cl, 0, CONV_HALO + r:CONV_HALO + r + sub, :]
                            * sh_ref[1, cl, b, off + r:off + r + sub, :])
                    part = part + jnp.sum(prod.reshape(sub // 8, 8, LANES), axis=0)
                gacc_ref[k, :, lanes] += part
        a_v = uglu_ref[:, :CC]
        sg = _sigmoid(uglu_ref[:, CC:])
        da = dvg * sg
        dgate = dvg * a_v * sg * (1.0 - sg)
        du = jnp.concatenate([da, dgate], axis=1)
        du_ref[...] = du.astype(du_ref.dtype)
        _acc_row(dbin_ref, jnp.sum(du, axis=0, keepdims=True))

        @pl.when(i == pl.num_programs(0) - 1)
        def _():
            gw_ref[...] = jnp.zeros(gw_ref.shape, F32)
            for k in range(TAPS):
                gw_ref[k:k + 1, :] = jnp.sum(gacc_ref[k], axis=0, keepdims=True)

    return _hbm_call(
        body, name="conv_bwd", grid=(S // tm,),
        in_specs=[pl.BlockSpec(memory_space=pl.ANY), pl.BlockSpec(memory_space=pl.ANY), _rows(tm, 2 * CC),
                  _full((TAPS, CC))],
        out_specs=[_rows(tm, 2 * CC), _full((TAPS + 1, CC)), _full((8, 2 * CC))],
        out_shape=[_sds((S, 2 * CC), _MXU), _sds((TAPS + 1, CC), F32), _sds((8, 2 * CC), F32)],
        scratch_shapes=[pltpu.VMEM((S + 2 * CONV_HALO, CC), F32), pltpu.VMEM((S + 2 * CONV_HALO, CC), F32),
                        pltpu.VMEM((TAPS, 8, CC), F32), _shift_scratch(tm, 2), pltpu.VMEM((tm, CC), F32),
                        pltpu.SemaphoreType.DMA((2,))],
        compiler_params=_params(56),
    )(dy, vg, uglu, w_dw)


def _in_proj_bwd(duglu, dqr, dkr, dv, cosq, sinq, x, dx1, g_mix, winT, tm=512):
    S = x.shape[0]
    NA = ATT + 2 * KVC

    def body(dug_ref, dq_ref, dk_ref, dv_ref, c_ref, s_ref, x_ref, dx1_ref, g_ref, w_ref,
             dx_ref, du_ref, dg_ref, dbin_ref):
        i = pl.program_id(0)
        _acc_init(i, dg_ref, dbin_ref)
        c = c_ref[...]
        s = s_ref[...]
        datt = jnp.concatenate([_rope_bwd(dq_ref[...], c, s), _rope_bwd(dk_ref[...], c, s), dv_ref[...]], axis=1)
        _acc_row(dbin_ref, jnp.sum(datt, axis=0, keepdims=True))
        du = jnp.concatenate([dug_ref[...], datt.astype(du_ref.dtype)], axis=1)
        du_ref[...] = du
        dh = _nn(du, w_ref[...])
        xv = x_ref[...]
        g = g_ref[...]
        _, r = _rms(xv, g)
        dxn, dg = _rms_bwd(dh, xv, r, g)
        dx_ref[...] = dx1_ref[...] + dxn
        _acc_row(dg_ref, dg)

    return _hbm_call(
        body, name="in_proj_bwd", grid=(S // tm,),
        in_specs=[_rows(tm, 2 * CC), _rows(tm, ATT), _rows(tm, KVC), _rows(tm, KVC), _rows(tm, LANES),
                  _rows(tm, LANES), _rows(tm, D), _rows(tm, D), _full((1, D)), _once((INC, D))],
        out_specs=[_rows(tm, D), _rows(tm, INC), _full((8, D)), _full((8, NA))],
        out_shape=[_sds((S, D), F32), _sds((S, INC), _MXU), _sds((8, D), F32), _sds((8, NA), F32)],
        compiler_params=_params(48),
    )(duglu, dqr, dkr, dv, cosq, sinq, x, dx1, g_mix, winT)


def _rope_tables(S):
    pos = jnp.arange(S, dtype=F32)
    inv_freq = ROPE_THETA ** (-jnp.arange(0, HD, 2, dtype=F32) / HD)
    ang = pos[:, None] * inv_freq[None, :]
    cos, sin = jnp.cos(ang), jnp.sin(ang)
    cosq = jnp.tile(jnp.concatenate([cos, cos], axis=1), (1, LANES // HD))
    sinq = jnp.tile(jnp.concatenate([-sin, sin], axis=1), (1, LANES // HD))
    return cosq, sinq


GATHER_GROUPS = (("w_inT", "w_dw"), ("w_out", "w_mem_kvT", "w_mem_q", "w_mem_o"), ("w_gateT", "w_upT", "w_down"))


def _local_step(x, mem, tgt, sp, gather, reduce):
    S = x.shape[0]
    cosq, sinq = _rope_tables(S)

    gather.forward(0, x)
    w0 = gather.finish(0, x)
    winT, w_dw_full = w0["w_inT"], w0["w_dw"]
    h, uglu, vg, qr, kr, vv = _in_proj_fwd(x, sp["g_mix"], winT, sp["b_in"], cosq, sinq)
    ya = _attn_fwd(qr, kr, vv, sp["attn_sink"])
    gather.forward(1, ya)
    y, yc = _conv_fwd(vg, w_dw_full, sp["b_dw"], sp["g_conv_ln"], sp["b_conv_ln"])
    w1 = gather.finish(1, yc)
    wout, wkvT, wq, wo = w1["w_out"], w1["w_mem_kvT"], w1["w_mem_q"], w1["w_mem_o"]
    km, vm, memn = _mem_kv_fwd(mem, sp["g_mem_kv"], wkvT)
    x1 = _out_proj_fwd(x, yc, ya, wout, sp["b_out"])
    gather.forward(2, x1)
    x2, qm, om = _mem_attn_fwd(x1, sp["g_mem_q"], wq, km, vm, wo)
    w2 = gather.finish(2, om)
    wgT, wuT, wd = w2["w_gateT"], w2["w_upT"], w2["w_down"]
    gt, up, hf, dx3, dx3b, loss_acc, dg_final = _ffn_loss_fwd(x2, tgt, sp["g_ffn"], sp["g_final"], wgT, wuT, wd)

    dx2, dx2b, dgt, dup, act, dg_ffn = _ffn_bwd(dx3, x2, gt, up, sp["g_ffn"], wd, wgT, wuT)
    token = reduce.start(0, [_wgrad(dgt, hf, "wgrad_gate", dx2), _wgrad(dup, hf, "wgrad_up", dx2),
                             _wgrad(act, dx3b, "wgrad_down", dx2)])

    dx1, dx1b, hq, dqm, dyc, dya, dkm, dvm, dg_mem_q, db_out = _mem_attn_bwd(
        dx2, x1, qm, km, vm, sp["g_mem_q"], wq, wo, wout, token)
    token = reduce.middle(0, dx1)
    g_wo = _wgrad(om, dx2b, "wgrad_mem_o", token)
    g_wq = _wgrad(hq, dqm, "wgrad_mem_q", token)
    g_wkvT_f, g_wkvT_b, dg_mem_kv = _mem_kv_bwd(dkm, dvm, mem, sp["g_mem_kv"], memn, wkvT)
    g_wout_c = _wgrad(yc, dx1b, "wgrad_out_conv", token)
    g_wout_a = _wgrad(ya, dx1b, "wgrad_out_attn", token)
    g_wout = tuple(jnp.concatenate([c, a], axis=0) for c, a in zip(g_wout_c, g_wout_a))
    token = reduce.start(1, [g_wout, g_wq, (g_wkvT_f, g_wkvT_b), g_wo])

    dqr, dkr, dvv, dsink = _attn_bwd(qr, kr, vv, sp["attn_sink"], dya, token)
    token = reduce.middle(1, dqr)
    dy, dg_cln, db_cln, db_dw = _conv_norm_bwd(dyc, y, sp["g_conv_ln"], sp["b_conv_ln"], token)
    duglu, g_wdw, dbin_glu = _conv_bwd(dy, vg, uglu, w_dw_full)
    grad_x, du, dg_mix, dbin_att = _in_proj_bwd(duglu, dqr, dkr, dvv, cosq, sinq, x, dx1, sp["g_mix"], winT)
    token = reduce.start(2, [_wgrad(du, h, "wgrad_in", grad_x)])

    small = {
        "loss": loss_acc[0:1, :],
        "g_mix": dg_mix[0:1], "b_in": jnp.concatenate([dbin_glu[0:1], dbin_att[0:1]], axis=1),
        "b_dw": db_dw[0:1], "g_conv_ln": dg_cln[0:1], "b_conv_ln": db_cln[0:1],
        "attn_sink": dsink[:, 0].reshape(1, NQ), "b_out": db_out[0:1], "g_mem_q": dg_mem_q[0:1],
        "g_mem_kv": dg_mem_kv[0:1], "g_ffn": dg_ffn[0:1], "g_final": dg_final[0:1],
    }
    token = reduce.middle(2, token, reduce.small_zones(small, g_wdw))
    reduce.finish(0, token)
    reduce.finish(1, token)
    return grad_x


def _place():
    return lax.axis_index("x"), lax.axis_index("y"), lax.axis_index("c")


def _hbm_specs(n):
    return [pl.BlockSpec(memory_space=pl.ANY)] * n


_HBM = pl.BlockSpec(memory_space=pltpu.HBM)
_SEM = pl.BlockSpec(memory_space=pltpu.SEMAPHORE)
_ANY = pl.BlockSpec(memory_space=pl.ANY)
_EFFECT = pltpu.SideEffectType.DATAFLOW_SIDE_EFFECTING
_VMEM = pl.BlockSpec(memory_space=pltpu.VMEM)
_TOKEN = jax.ShapeDtypeStruct((8, LANES), F32)


def _dev_index(p):
    return 4 * p[0] + 2 * p[1] + p[2]


def _gather_start(groups):
    sizes = [len(g) for g in groups]
    flat = [z for g in groups for z in g]
    n = len(flat)

    def body(*refs):
        ins = refs[:n]
        sems = refs[2 * n:]
        x, y, c = _place()
        me = _dev_index((x, y, c))
        peers = [(x, y, 1 - c), (1 - x, y, c), (x, 1 - y, c), (1 - x, 1 - y, c)]
        base = 0
        for gi, size in enumerate(sizes):
            send_sems, recv_sems = sems[2 * gi], sems[2 * gi + 1]
            for k in range(size):
                zone = ins[base + k]
                for j, peer in enumerate(peers):
                    pltpu.make_async_remote_copy(
                        src_ref=zone.at[me], dst_ref=zone.at[me], send_sem=send_sems.at[4 * k + j],
                        recv_sem=recv_sems.at[4 * k + j], device_id=peer, device_id_type=MESH).start()
            base += size

    out_shape = [pltpu.HBM(z.shape, z.dtype) for z in flat]
    out_specs = [_HBM] * n
    for size in sizes:
        out_shape += [pltpu.SemaphoreType.DMA((4 * size,))] * 2
        out_specs += [_SEM, _SEM]
    res = pl.pallas_call(
        body, name="gather_start", in_specs=[_HBM] * n, out_specs=out_specs, out_shape=out_shape,
        input_output_aliases={k: k for k in range(n)},
        compiler_params=pltpu.CompilerParams(has_side_effects=_EFFECT),
    )(*[pltpu.with_memory_space_constraint(z, pltpu.HBM) for z in flat])
    out, base = [], 0
    for gi, size in enumerate(sizes):
        out.append((list(res[base:base + size]), res[n + 2 * gi], res[n + 2 * gi + 1]))
        base += size
    return out


def _gather_forward(zones, send_sems, recv_sems, after, collective_id, name):
    n = len(zones)

    def body(*refs):
        ins = refs[:n]
        send_ref, recv_ref = refs[n], refs[n + 1]
        fsend, frecv = refs[2 * n + 3], refs[2 * n + 4]
        x, y, c = _place()
        sibling = (x, y, 1 - c)
        barrier = pltpu.get_barrier_semaphore()
        pl.semaphore_signal(barrier, inc=1, device_id=sibling, device_id_type=MESH)
        pl.semaphore_wait(barrier, 1)
        chips = [(1 - x, y), (x, 1 - y), (1 - x, 1 - y)]
        for j, chip in enumerate(chips):
            for k in range(n):
                slot = ins[k].at[_dev_index((*chip, c))]
                pltpu.make_async_remote_copy(
                    src_ref=slot, dst_ref=slot, send_sem=send_ref.at[4 * k + 1 + j],
                    recv_sem=recv_ref.at[4 * k + 1 + j], device_id=sibling, device_id_type=MESH).wait_recv()
                pltpu.make_async_remote_copy(
                    src_ref=slot, dst_ref=slot, send_sem=fsend.at[3 * k + j], recv_sem=frecv.at[3 * k + j],
                    device_id=sibling, device_id_type=MESH).start()
        for k in range(n):
            own = ins[k].at[_dev_index((x, y, c))]
            theirs = ins[k].at[_dev_index(sibling)]
            pltpu.make_async_remote_copy(
                src_ref=theirs, dst_ref=theirs, send_sem=send_ref.at[4 * k], recv_sem=recv_ref.at[4 * k],
                device_id=sibling, device_id_type=MESH).wait_recv()
            for j in range(4):
                pltpu.make_async_remote_copy(
                    src_ref=own, dst_ref=own, send_sem=send_ref.at[4 * k + j], recv_sem=recv_ref.at[4 * k + j],
                    device_id=sibling, device_id_type=MESH).wait_send()

    res = pl.pallas_call(
        body, name=name, in_specs=[_HBM] * n + [_SEM, _SEM, _ANY],
        out_specs=[_HBM] * n + [_SEM, _SEM],
        out_shape=[pltpu.HBM(z.shape, z.dtype) for z in zones] + [pltpu.SemaphoreType.DMA((3 * n,))] * 2,
        input_output_aliases={k: k for k in range(n)},
        compiler_params=pltpu.CompilerParams(has_side_effects=_EFFECT, collective_id=collective_id),
    )(*zones, send_sems, recv_sems, after)
    return list(res[:n]), res[n], res[n + 1]


def _gather_finish(zones, fsend, frecv, after, name):
    n = len(zones)

    def body(*refs):
        ins = refs[:n]
        fsend_ref, frecv_ref = refs[n], refs[n + 1]
        x, y, c = _place()
        sibling = (x, y, 1 - c)
        chips = [(1 - x, y), (x, 1 - y), (1 - x, 1 - y)]
        for j, chip in enumerate(chips):
            for k in range(n):
                mine = ins[k].at[_dev_index((*chip, c))]
                theirs = ins[k].at[_dev_index((*chip, 1 - c))]
                pltpu.make_async_remote_copy(
                    src_ref=theirs, dst_ref=theirs, send_sem=fsend_ref.at[3 * k + j],
                    recv_sem=frecv_ref.at[3 * k + j], device_id=sibling, device_id_type=MESH).wait_recv()
                pltpu.make_async_remote_copy(
                    src_ref=mine, dst_ref=mine, send_sem=fsend_ref.at[3 * k + j],
                    recv_sem=frecv_ref.at[3 * k + j], device_id=sibling, device_id_type=MESH).wait_send()

    res = pl.pallas_call(
        body, name=name, in_specs=[_HBM] * n + [_SEM, _SEM, _ANY], out_specs=[_HBM] * n,
        out_shape=[pltpu.HBM(z.shape, z.dtype) for z in zones],
        input_output_aliases={k: k for k in range(n)},
        compiler_params=pltpu.CompilerParams(has_side_effects=_EFFECT),
    )(*zones, fsend, frecv, after)
    return list(res)


class _Gather:
    def __init__(self, started):
        self._phase = list(started)

    def forward(self, gi, after):
        zones, send_sems, recv_sems = self._phase[gi]
        self._phase[gi] = _gather_forward(zones, send_sems, recv_sems, after, gi + 1, "gather_forward_%d" % gi)

    def finish(self, gi, after):
        zones, fsend, frecv = self._phase[gi]
        out = {}
        for name, z in zip(GATHER_GROUPS[gi], _gather_finish(zones, fsend, frecv, after, "gather_finish_%d" % gi)):
            if name == "w_dw":
                out[name] = jnp.transpose(z, (1, 0, 2)).reshape(TAPS + 1, CC)[:TAPS]
            else:
                out[name] = z.reshape(NDEV * z.shape[1], D)
        return out


def _handshake(peers):
    barrier = pltpu.get_barrier_semaphore()
    for peer in peers:
        pl.semaphore_signal(barrier, inc=1, device_id=peer, device_id_type=MESH)
    pl.semaphore_wait(barrier, len(peers))


def _sibling_copies(srcs, lands, send_sems, recv_sems):
    x, y, c = _place()
    return [pltpu.make_async_remote_copy(
        src_ref=srcs[k].at[2 * s + (1 - c)], dst_ref=lands[k].at[s], send_sem=send_sems.at[4 * k + s],
        recv_sem=recv_sems.at[4 * k + s], device_id=(x, y, 1 - c), device_id_type=MESH)
        for k in range(len(srcs)) for s in range(4)]


def _sibling_start(gbs, collective_id, name):
    n = len(gbs)
    lands = [lax.empty((4,) + g.shape[1:], g.dtype) for g in gbs]

    def body(*refs):
        x, y, c = _place()
        _handshake([(x, y, 1 - c)])
        for cp in _sibling_copies(refs[:n], refs[n:2 * n], refs[4 * n], refs[4 * n + 1]):
            cp.start()
        refs[4 * n + 2][...] = jnp.zeros(_TOKEN.shape, _TOKEN.dtype)

    res = pl.pallas_call(
        body, name=name, in_specs=[_HBM] * (2 * n), out_specs=[_HBM] * (2 * n) + [_SEM, _SEM, _VMEM],
        out_shape=[pltpu.HBM(a.shape, a.dtype) for a in gbs + lands] + [pltpu.SemaphoreType.DMA((4 * n,))] * 2
        + [_TOKEN],
        input_output_aliases={k: k for k in range(2 * n)},
        compiler_params=pltpu.CompilerParams(has_side_effects=_EFFECT, collective_id=collective_id),
    )(*[pltpu.with_memory_space_constraint(a, pltpu.HBM) for a in gbs + lands])
    return list(res[:n]), list(res[n:2 * n]), res[2 * n], res[2 * n + 1], res[2 * n + 2]


def _sibling_wait(gbs, lands, send_sems, recv_sems, after, name):
    n = len(gbs)

    def body(*refs):
        for cp in _sibling_copies(refs[:n], refs[n:2 * n], refs[2 * n], refs[2 * n + 1]):
            cp.wait_recv()
            cp.wait_send()

    res = pl.pallas_call(
        body, name=name, in_specs=[_HBM] * (2 * n) + [_SEM, _SEM, _ANY], out_specs=[_HBM] * (2 * n),
        out_shape=[pltpu.HBM(a.shape, a.dtype) for a in gbs + lands],
        input_output_aliases={k: k for k in range(2 * n)},
        compiler_params=pltpu.CompilerParams(has_side_effects=_EFFECT),
    )(*gbs, *lands, send_sems, recv_sems, after)
    return list(res[n:])


def _pair_sum(gfs, ts, c_idx, name):
    n = len(gfs)
    halves = 2

    def body(c_ref, *refs):
        for k in range(n):
            refs[2 * n + k][...] = (refs[k][...] + refs[n + k][...].astype(F32)).astype(jnp.bfloat16)

    in_specs, out_specs, out_shape = [], [], []
    for g in gfs:
        rh = g.shape[1] // halves
        in_specs.append(pl.BlockSpec((None, rh, D), lambda s, h, c_ref: (2 * s + c_ref[0], h, 0)))
    for g in gfs:
        rh = g.shape[1] // halves
        in_specs.append(pl.BlockSpec((None, rh, D), lambda s, h, c_ref: (s, h, 0)))
        out_specs.append(pl.BlockSpec((None, rh, D), lambda s, h, c_ref: (s, h, 0)))
        out_shape.append(_sds((4, g.shape[1], D), jnp.bfloat16))
    return pl.pallas_call(
        body, name=name,
        grid_spec=pltpu.PrefetchScalarGridSpec(num_scalar_prefetch=1, grid=(4, halves), in_specs=in_specs,
                                               out_specs=out_specs),
        out_shape=out_shape,
        compiler_params=pltpu.CompilerParams(dimension_semantics=("arbitrary", "arbitrary"),
                                             vmem_limit_bytes=48 * 2 ** 20),
    )(c_idx, *gfs, *ts)


def _chip_copies(pbs, lands, smalls, send_sems, recv_sems):
    x, y, c = _place()
    own = 2 * x + y
    me = _dev_index((x, y, c))
    chips = [(1 - x, y), (x, 1 - y), (1 - x, 1 - y)]
    n = len(pbs)
    copies = [pltpu.make_async_remote_copy(
        src_ref=pbs[k].at[2 * chip[0] + chip[1]], dst_ref=lands[k].at[own], send_sem=send_sems.at[3 * k + j],
        recv_sem=recv_sems.at[3 * k + j], device_id=(*chip, c), device_id_type=MESH)
        for k in range(n) for j, chip in enumerate(chips)]
    peers = [(x, y, 1 - c)] + [(*chip, cc) for chip in chips for cc in (c, 1 - c)]
    copies += [pltpu.make_async_remote_copy(
        src_ref=smalls[t].at[me], dst_ref=smalls[t].at[me], send_sem=send_sems.at[3 * n + 7 * t + j],
        recv_sem=recv_sems.at[3 * n + 7 * t + j], device_id=peer, device_id_type=MESH)
        for t in range(len(smalls)) for j, peer in enumerate(peers)]
    return copies


def _chip_start(pbs, smalls, collective_id, name):
    n, m = len(pbs), len(smalls)
    lands = [lax.empty(p.shape, p.dtype) for p in pbs]
    arrays = list(pbs) + lands + list(smalls)
    na = len(arrays)
    nsem = 3 * n + 7 * m

    def body(*refs):
        x, y, c = _place()
        chips = [(1 - x, y), (x, 1 - y), (1 - x, 1 - y)]
        peers = [(*chip, c) for chip in chips]
        if m:
            peers += [(x, y, 1 - c)] + [(*chip, 1 - c) for chip in chips]
        _handshake(peers)
        for cp in _chip_copies(refs[:n], refs[n:2 * n], refs[2 * n:na], refs[2 * na], refs[2 * na + 1]):
            cp.start()
        refs[2 * na + 2][...] = jnp.zeros(_TOKEN.shape, _TOKEN.dtype)

    res = pl.pallas_call(
        body, name=name, in_specs=[_HBM] * na, out_specs=[_HBM] * na + [_SEM, _SEM, _VMEM],
        out_shape=[pltpu.HBM(a.shape, a.dtype) for a in arrays] + [pltpu.SemaphoreType.DMA((nsem,))] * 2 + [_TOKEN],
        input_output_aliases={k: k for k in range(na)},
        compiler_params=pltpu.CompilerParams(has_side_effects=_EFFECT, collective_id=collective_id),
    )(*[pltpu.with_memory_space_constraint(a, pltpu.HBM) for a in arrays])
    return (list(res[:n]), list(res[n:2 * n]), list(res[2 * n:na]), res[na], res[na + 1]), res[na + 2]


def _chip_wait(pbs, lands, smalls, send_sems, recv_sems, after, name):
    n = len(pbs)
    arrays = list(pbs) + list(lands) + list(smalls)
    na = len(arrays)

    def body(*refs):
        for cp in _chip_copies(refs[:n], refs[n:2 * n], refs[2 * n:na], refs[na], refs[na + 1]):
            cp.wait_recv()
            cp.wait_send()

    res = pl.pallas_call(
        body, name=name, in_specs=[_HBM] * na + [_SEM, _SEM, _ANY], out_specs=[_HBM] * na,
        out_shape=[pltpu.HBM(a.shape, a.dtype) for a in arrays],
        input_output_aliases={k: k for k in range(na)},
        compiler_params=pltpu.CompilerParams(has_side_effects=_EFFECT),
    )(*arrays, send_sems, recv_sems, after)
    return list(res[:n]), list(res[n:2 * n]), list(res[2 * n:])


def _chip_sum(pbs, qs, own_idx, name):
    n = len(pbs)
    halves = 2

    def body(o_ref, *refs):
        for k in range(n):
            acc = refs[4 * k][...].astype(F32)
            for j in range(1, 4):
                acc = acc + refs[4 * k + j][...].astype(F32)
            refs[4 * n + k][...] = acc

    def other(j):
        return lambda h, o: ((o[0] + j) % 4, h, 0)

    in_specs, ins = [], []
    for p, q in zip(pbs, qs):
        rh = p.shape[1] // halves
        in_specs.append(pl.BlockSpec((None, rh, D), lambda h, o: (o[0], h, 0)))
        ins.append(p)
        for j in range(1, 4):
            in_specs.append(pl.BlockSpec((None, rh, D), other(j)))
            ins.append(q)
    out_specs = [pl.BlockSpec((p.shape[1] // halves, D), lambda h, o: (h, 0)) for p in pbs]
    return pl.pallas_call(
        body, name=name,
        grid_spec=pltpu.PrefetchScalarGridSpec(num_scalar_prefetch=1, grid=(halves,), in_specs=in_specs,
                                               out_specs=out_specs),
        out_shape=[_sds((p.shape[1], D), F32) for p in pbs],
        compiler_params=pltpu.CompilerParams(dimension_semantics=("arbitrary",), vmem_limit_bytes=48 * 2 ** 20),
    )(own_idx, *ins)


REDUCE_GROUPS = (("w_gateT", "w_upT", "w_down"), ("w_out", "w_mem_q", "w_mem_kvT", "w_mem_o"), ("w_inT",))


class _Reduce:
    def __init__(self):
        x, y, c = _place()
        self._c = c.astype(jnp.int32).reshape(1)
        self._own = (2 * x + y).astype(jnp.int32).reshape(1)
        self._me = _dev_index((x, y, c))
        self._state = {}
        self.shards = {}
        self.smalls = None

    def start(self, gi, wgrads):
        gfs = [g[0].reshape(NDEV, g[0].shape[0] // NDEV, D) for g in wgrads]
        gbs = [g[1].reshape(NDEV, g[1].shape[0] // NDEV, D) for g in wgrads]
        gbs, lands, send_sems, recv_sems, token = _sibling_start(gbs, 4 + gi, "reduce_sibling_start_%d" % gi)
        self._state[gi] = (gfs, gbs, lands, send_sems, recv_sems)
        return token

    def middle(self, gi, after, smalls=()):
        gfs, gbs, lands, send_sems, recv_sems = self._state[gi]
        ts = _sibling_wait(gbs, lands, send_sems, recv_sems, after, "reduce_sibling_wait_%d" % gi)
        pbs = _pair_sum(gfs, ts, self._c, "reduce_pair_sum_%d" % gi)
        self._state[gi], token = _chip_start(pbs, list(smalls), 7 + gi, "reduce_chip_start_%d" % gi)
        return token

    def finish(self, gi, after):
        pbs, lands, smalls, send_sems, recv_sems = self._state[gi]
        pbs, qs, smalls = _chip_wait(pbs, lands, smalls, send_sems, recv_sems, after, "reduce_chip_wait_%d" % gi)
        self.shards.update(zip(REDUCE_GROUPS[gi], _chip_sum(pbs, qs, self._own, "reduce_chip_sum_%d" % gi)))
        if smalls:
            self.smalls = smalls

    def small_zones(self, small, g_wdw):
        pack = jnp.concatenate(
            [jnp.pad(small[name], ((0, 0), (0, lanes - small[name].shape[1]))) for name, lanes, _ in PACK], axis=1)
        zones = []
        for a in (pack, g_wdw):
            zone = lax.empty((NDEV,) + a.shape, a.dtype)
            zones.append(lax.dynamic_update_slice(zone, a[None], (self._me,) + (0,) * a.ndim))
        return zones


def _adamw_math(w, g, m, v):
    m = ADAM_B1 * m + (1.0 - ADAM_B1) * g
    v = ADAM_B2 * v + (1.0 - ADAM_B2) * (g * g)
    m_hat = m / (1.0 - ADAM_B1 ** ADAM_STEP)
    v_hat = v / (1.0 - ADAM_B2 ** ADAM_STEP)
    delta = -ADAM_LR * (m_hat / (jnp.sqrt(v_hat) + ADAM_EPS) + ADAM_WD * w)
    return delta, m, v


def _adamw(w, g, m, v, name):
    R, C = w.shape
    rc = 256 if R % 256 == 0 and R > 256 else R

    def body(w_ref, g_ref, m_ref, v_ref, d_ref, nm_ref, nv_ref):
        d, nm, nv = _adamw_math(w_ref[...], g_ref[...], m_ref[...], v_ref[...])
        d_ref[...] = d
        nm_ref[...] = nm
        nv_ref[...] = nv

    return _hbm_call(
        body, name=name, grid=(R // rc,),
        in_specs=[_rows(rc, C)] * 4, out_specs=[_rows(rc, C)] * 3,
        out_shape=[_sds((R, C), F32)] * 3,
        compiler_params=_params(40),
    )(w, g, m, v)


PACK = (("loss", LANES, 1), ("g_mix", D, D), ("b_in", INC, INC), ("b_dw", CC, CC), ("g_conv_ln", CC, CC),
        ("b_conv_ln", CC, CC), ("attn_sink", LANES, NQ), ("b_out", D, D), ("g_mem_q", D, D), ("g_mem_kv", D, D),
        ("g_ffn", D, D), ("g_final", D, D))
PACK_LANES = sum(p[1] for p in PACK)


def _small_update(packs, gdws, ws, ms, vs):
    names = [p[0] for p in PACK[1:]]
    n = len(names)

    def body(*refs):
        packs_ref, gdws_ref = refs[0], refs[1]
        w_refs, m_refs, v_refs = refs[2:2 + n], refs[2 + n:2 + 2 * n], refs[2 + 2 * n:2 + 3 * n]
        outs = refs[2 + 3 * n:]
        tot = packs_ref[0]
        gdw = gdws_ref[0]
        for s in range(1, NDEV):
            tot = tot + packs_ref[s]
            gdw = gdw + gdws_ref[s]
        outs[0][...] = tot[:, 0:LANES]
        outs[1][...] = gdw
        off = LANES
        for k, (_, lanes, used) in enumerate(PACK[1:]):
            g = tot[:, off:off + used]
            off += lanes
            d, nm, nv = _adamw_math(w_refs[k][...], g, m_refs[k][...], v_refs[k][...])
            o = outs[2 + 4 * k:6 + 4 * k]
            o[0][...] = g
            o[1][...] = d
            o[2][...] = nm
            o[3][...] = nv

    def full(a):
        nd = len(a.shape)
        return pl.BlockSpec(a.shape, lambda i: (0,) * nd)

    ins = [packs, gdws, *ws, *ms, *vs]
    out_shape = [_sds((1, LANES), F32), _sds(gdws.shape[1:], F32)]
    for w in ws:
        out_shape += [_sds(w.shape, F32)] * 4
    return pl.pallas_call(
        body, name="small_update", grid=(1,),
        in_specs=[full(a) for a in ins], out_specs=[full(o) for o in out_shape], out_shape=out_shape,
        compiler_params=_params(40),
    )(*ins)


def kernel(x, mem, g_mix, w_in, b_in, w_dw, b_dw, g_conv_ln, b_conv_ln, attn_sink, w_out, b_out, g_mem_q, g_mem_kv, w_mem_q, w_mem_kv, w_mem_o, g_ffn, w_gate, w_up, w_down, g_final, loss_target, m_g_mix, m_w_in, m_b_in, m_w_dw, m_b_dw, m_g_conv_ln, m_b_conv_ln, m_attn_sink, m_w_out, m_b_out, m_g_mem_q, m_g_mem_kv, m_w_mem_q, m_w_mem_kv, m_w_mem_o, m_g_ffn, m_w_gate, m_w_up, m_w_down, m_g_final, v_g_mix, v_w_in, v_b_in, v_w_dw, v_b_dw, v_g_conv_ln, v_b_conv_ln, v_attn_sink, v_w_out, v_b_out, v_g_mem_q, v_g_mem_kv, v_w_mem_q, v_w_mem_kv, v_w_mem_o, v_g_ffn, v_w_gate, v_w_up, v_w_down, v_g_final):
    given = dict(locals())
    bf = jnp.bfloat16
    xi, yi, ci = _place()
    me = 4 * xi + 2 * yi + ci

    transposed = {"w_inT": w_in, "w_mem_kvT": w_mem_kv, "w_gateT": w_gate, "w_upT": w_up}
    plain = {"w_out": w_out, "w_mem_q": w_mem_q, "w_mem_o": w_mem_o, "w_down": w_down}
    blocks = {n: (transposed[n][0].T if n in transposed else plain[n][0]).astype(bf) for n in W_NAMES}
    blocks["w_dw"] = jnp.pad(w_dw[0], ((0, 1), (0, 0)))

    def landing_zone(b):
        zone = lax.empty((NDEV,) + b.shape, b.dtype)
        return lax.dynamic_update_slice(zone, b[None], (me,) + (0,) * b.ndim)

    gather = _Gather(_gather_start([[landing_zone(blocks[n]) for n in names] for names in GATHER_GROUPS]))

    sp = {n: given[n] for n in ("g_mix", "b_in", "b_dw", "g_conv_ln", "b_conv_ln", "attn_sink", "b_out", "g_mem_q",
                                "g_mem_kv", "g_ffn")}
    sp["g_final"] = g_final.reshape(1, D)
    reduce = _Reduce()
    grad_x = _local_step(x[0], mem[0], loss_target[0], sp, gather, reduce)
    gsh = reduce.shards

    grads, deltas, new_m, new_v = {}, {}, {}, {}

    def update(n, g, transposed):
        w, m, v = given[n][0], given["m_" + n][0], given["v_" + n][0]
        if transposed:
            d, nm, nv = _adamw(w.T, g, m.T, v.T, "adamw_" + n)
            grads[n], deltas[n], new_m[n], new_v[n] = g.T[None], d.T[None], nm.T[None], nv.T[None]
        else:
            d, nm, nv = _adamw(w, g, m, v, "adamw_" + n)
            grads[n], deltas[n], new_m[n], new_v[n] = g[None], d[None], nm[None], nv[None]
        return d

    for n in ("w_gate", "w_up"):
        update(n, gsh[n + "T"], True)
    for n in ("w_down", "w_out", "w_mem_q", "w_mem_o"):
        update(n, gsh[n], False)
    last = update("w_mem_kv", gsh["w_mem_kvT"].T, False)
    reduce.finish(2, last)
    update("w_in", reduce.shards["w_inT"], True)

    packs, gdws = reduce.smalls
    names_small = [p[0] for p in PACK[1:]]

    def as_row(a):
        return a.reshape(1, -1)

    res = _small_update(packs, gdws, [as_row(given[n]) for n in names_small],
                        [as_row(given["m_" + n]) for n in names_small],
                        [as_row(given["v_" + n]) for n in names_small])
    loss = res[0][0, 0]
    for k, n in enumerate(names_small):
        shape = given[n].shape
        g, d, nm, nv = res[2 + 4 * k:6 + 4 * k]
        grads[n], deltas[n], new_m[n], new_v[n] = (g.reshape(shape), d.reshape(shape), nm.reshape(shape),
                                                   nv.reshape(shape))
    update("w_dw", lax.dynamic_slice(res[1], (0, me * (CC // NDEV)), (TAPS, CC // NDEV)), False)

    order = ("g_mix", "w_in", "b_in", "w_dw", "b_dw", "g_conv_ln", "b_conv_ln", "attn_sink", "w_out", "b_out",
             "g_mem_q", "g_mem_kv", "w_mem_q", "w_mem_kv", "w_mem_o", "g_ffn", "w_gate", "w_up", "w_down", "g_final")
    return (loss, grad_x[None], *[grads[n] for n in order], *[deltas[n] for n in order],
            *[new_m[n] for n in order], *[new_v[n] for n in order])
```

```python
import functools

import jax
import jax.numpy as jnp
from jax import lax
from jax.experimental import pallas as pl
from jax.experimental.pallas import tpu as pltpu

D = 1024
CC = 512
HD = 64
NQ = 8
NKV = 2
GRP = NQ // NKV
ATT = NQ * HD
KVC = NKV * HD
BLK = 128
MH = 4
MHD = D // MH
F = 2816
FCH = 256
FCH_BWD = F // 2
INC = 2 * CC + ATT + 2 * KVC
TAPS = 31
EPS = 1e-6
ROPE_THETA = 10000.0
SCALE = HD ** -0.5
MSCALE = MHD ** -0.5
LANES = 128
NDEV = 8

ADAM_LR = 0.001
ADAM_B1 = 0.9
ADAM_B2 = 0.999
ADAM_EPS = 1e-08
ADAM_WD = 0.01
ADAM_STEP = 10

F32 = jnp.float32
_MXU = jnp.bfloat16
MESH = pl.DeviceIdType.MESH

W_NAMES = ("w_inT", "w_out", "w_mem_q", "w_mem_kvT", "w_mem_o", "w_gateT", "w_upT", "w_down")
W_ROWS = (INC // NDEV, D // NDEV, D // NDEV, 2 * D // NDEV, D // NDEV, F // NDEV, F // NDEV, F // NDEV)


def _mm(a, b, ca, cb):
    return lax.dot_general(a.astype(_MXU), b.astype(_MXU), (((ca,), (cb,)), ((), ())),
                           preferred_element_type=F32)


def _nn(a, b):
    return _mm(a, b, 1, 0)


def _nt(a, b):
    return _mm(a, b, 1, 1)


def _tn(a, b):
    return _mm(a, b, 0, 0)


def _sigmoid(x):
    return 1.0 / (1.0 + jnp.exp(-x))


def _rms(x, g):
    r = lax.rsqrt(jnp.mean(x * x, axis=-1, keepdims=True) + EPS)
    return x * r * g, r


def _rms_bwd(dy, x, r, g):
    xh = x * r
    dg = jnp.sum(dy * xh, axis=0, keepdims=True)
    dxh = dy * g
    dx = r * (dxh - xh * jnp.mean(dxh * xh, axis=-1, keepdims=True))
    return dx, dg


def _rope(t, c, s):
    n = t.shape[1]
    reps = n // LANES
    if reps > 1:
        c = jnp.tile(c, (1, reps))
        s = jnp.tile(s, (1, reps))
    lane = lax.broadcasted_iota(jnp.int32, t.shape, 1)
    first = (lane & (HD - 1)) < (HD // 2)
    partner = jnp.where(first, pltpu.roll(t, n - HD // 2, 1), pltpu.roll(t, HD // 2, 1))
    return t * c + partner * s


def _rope_bwd(dt, c, s):
    n = dt.shape[1]
    reps = n // LANES
    if reps > 1:
        c = jnp.tile(c, (1, reps))
        s = jnp.tile(s, (1, reps))
    lane = lax.broadcasted_iota(jnp.int32, dt.shape, 1)
    first = (lane & (HD - 1)) < (HD // 2)
    ds = dt * s
    partner = jnp.where(first, pltpu.roll(ds, n - HD // 2, 1), pltpu.roll(ds, HD // 2, 1))
    return dt * c + partner


def _acc_init(i, *refs):
    @pl.when(i == 0)
    def _():
        for r in refs:
            r[...] = jnp.zeros(r.shape, r.dtype)


def _acc_row(ref, v):
    ref[...] += jnp.broadcast_to(v, ref.shape)


def _rows(tm, n):
    return pl.BlockSpec((tm, n), lambda i: (i, 0))


def _full(shape):
    nd = len(shape)
    return pl.BlockSpec(shape, lambda i: (0,) * nd)


def _once(shape):
    nd = len(shape)
    return pl.BlockSpec(shape, lambda i: (0,) * nd, pipeline_mode=pl.Buffered(1))


def _params(vmem_mb):
    return pltpu.CompilerParams(dimension_semantics=("arbitrary",), vmem_limit_bytes=vmem_mb * 2 ** 20)


def _sds(shape, dtype):
    return pltpu.HBM(shape, dtype)


def _hbm_call(*args, **kwargs):
    call = pl.pallas_call(*args, **kwargs)
    return lambda *ops: call(*[pltpu.with_memory_space_constraint(o, pltpu.HBM) for o in ops])


def _mem_kv_fwd(mem, g_kv, wkvT):
    M = mem.shape[0]

    def body(mem_ref, g_ref, w_ref, km_ref, vm_ref, mn_ref):
        y, _ = _rms(mem_ref[...], g_ref[...])
        kv = _nt(y, w_ref[...])
        km_ref[...] = kv[:, :D].astype(km_ref.dtype)
        vm_ref[...] = kv[:, D:].astype(vm_ref.dtype)
        mn_ref[...] = y.astype(mn_ref.dtype)

    return _hbm_call(
        body, name="mem_kv_fwd", grid=(1,),
        in_specs=[_full((M, D)), _full((1, D)), _full((2 * D, D))],
        out_specs=[_full((M, D))] * 3,
        out_shape=[_sds((M, D), _MXU)] * 3,
        compiler_params=_params(40),
    )(mem, g_kv, wkvT)


def _in_proj_fwd(x, g_mix, winT, b_in, cosq, sinq, tm=512):
    S = x.shape[0]

    def body(x_ref, g_ref, w_ref, b_ref, c_ref, s_ref, h_ref, uglu_ref, vg_ref, q_ref, k_ref, v_ref):
        h, _ = _rms(x_ref[...], g_ref[...])
        hb = h.astype(h_ref.dtype)
        h_ref[...] = hb
        u = _nt(hb, w_ref[...]) + b_ref[...]
        uglu_ref[...] = u[:, :2 * CC]
        vg_ref[...] = u[:, :CC] * _sigmoid(u[:, CC:2 * CC])
        c = c_ref[...]
        s = s_ref[...]
        q_ref[...] = _rope(u[:, 2 * CC:2 * CC + ATT], c, s).astype(q_ref.dtype)
        k_ref[...] = _rope(u[:, 2 * CC + ATT:2 * CC + ATT + KVC], c, s).astype(k_ref.dtype)
        v_ref[...] = u[:, 2 * CC + ATT + KVC:].astype(v_ref.dtype)

    return _hbm_call(
        body, name="in_proj_fwd", grid=(S // tm,),
        in_specs=[_rows(tm, D), _full((1, D)), _full((INC, D)), _full((1, INC)), _rows(tm, LANES), _rows(tm, LANES)],
        out_specs=[_rows(tm, D), _rows(tm, 2 * CC), _rows(tm, CC), _rows(tm, ATT), _rows(tm, KVC), _rows(tm, KVC)],
        out_shape=[_sds((S, D), _MXU), _sds((S, 2 * CC), F32), _sds((S, CC), F32), _sds((S, ATT), _MXU),
                   _sds((S, KVC), _MXU), _sds((S, KVC), _MXU)],
        compiler_params=_params(48),
    )(x, g_mix, winT, b_in, cosq, sinq)


CONV_HALO = 16


CONV_TILE = 256


def _shift_scratch(tm, sets):
    return pltpu.VMEM((sets, CC // LANES, 8, tm + 24, LANES), F32)


def _shifted_copies(pad_ref, sh_ref, t0, tm):
    for cl in range(CC // LANES):
        blk = pad_ref[pl.ds(t0, tm + 2 * CONV_HALO), cl * LANES:(cl + 1) * LANES]
        for b in range(8):
            sh_ref[cl, b, :, :] = blk[b:b + tm + 24, :]


def _conv_taps():
    return [(8 * a + b - 1, b, 8 * a) for b in range(8) for a in range(4) if 0 <= 8 * a + b - 1 < TAPS]


CONV_ROWS = 64
SUBLANES = 8


def _dwconv_tile(sh_ref, tm, w_ref, flip, out_ref):
    for cl in range(CC // LANES):
        lanes = slice(cl * LANES, (cl + 1) * LANES)

        def chunk(c, carry):
            r = pl.multiple_of(c * CONV_ROWS, CONV_ROWS)
            accs = [None] * 4
            for n, (k, b, off) in enumerate(_conv_taps()):
                kk = TAPS - 1 - k if flip else k
                term = sh_ref[cl, b, pl.ds(off + r, CONV_ROWS), :] * w_ref[kk:kk + 1, lanes]
                accs[n % 4] = term if accs[n % 4] is None else accs[n % 4] + term
            out_ref[pl.ds(r, CONV_ROWS), lanes] = (accs[0] + accs[1]) + (accs[2] + accs[3])
            return carry

        lax.fori_loop(0, tm // CONV_ROWS, chunk, 0)


def _fill_padded(i, src_hbm, pad_ref, sem, S):
    @pl.when(i == 0)
    def _():
        zeros = jnp.zeros((CONV_HALO, pad_ref.shape[1]), pad_ref.dtype)
        pad_ref[0:CONV_HALO, :] = zeros
        pad_ref[CONV_HALO + S:2 * CONV_HALO + S, :] = zeros
        cp = pltpu.make_async_copy(src_hbm, pad_ref.at[pl.ds(CONV_HALO, S), :], sem)
        cp.start()
        cp.wait()


def _ln_silu(y, g, b):
    mu = jnp.mean(y, axis=-1, keepdims=True)
    yc = y - mu
    var = jnp.mean(yc * yc, axis=-1, keepdims=True)
    rstd = lax.rsqrt(var + EPS)
    yh = yc * rstd
    z = yh * g + b
    return yh, rstd, z


def _conv_fwd(vg, w_dw, b_dw, ln_g, ln_b):
    S = vg.shape[0]
    tm = min(CONV_TILE, S)

    def body(vg_hbm, w_ref, b_ref, g_ref, bb_ref, y_ref, yc_ref, pad_ref, sh_ref, sem):
        i = pl.program_id(0)
        _fill_padded(i, vg_hbm, pad_ref, sem, S)
        _shifted_copies(pad_ref, sh_ref.at[0], pl.multiple_of(i * tm, tm), tm)
        _dwconv_tile(sh_ref.at[0], tm, w_ref, False, y_ref)
        y = y_ref[...] + b_ref[...]
        y_ref[...] = y
        _, _, z = _ln_silu(y, g_ref[...], bb_ref[...])
        yc_ref[...] = (z * _sigmoid(z)).astype(yc_ref.dtype)

    return _hbm_call(
        body, name="conv_fwd", grid=(S // tm,),
        in_specs=[pl.BlockSpec(memory_space=pl.ANY), _full((TAPS, CC)), _full((1, CC)), _full((1, CC)), _full((1, CC))],
        out_specs=[_rows(tm, CC), _rows(tm, CC)],
        out_shape=[_sds((S, CC), F32), _sds((S, CC), _MXU)],
        scratch_shapes=[pltpu.VMEM((S + 2 * CONV_HALO, CC), F32), _shift_scratch(tm, 1), pltpu.SemaphoreType.DMA],
        compiler_params=_params(40),
    )(vg, w_dw, b_dw, ln_g, ln_b)


def _attn_window(i, S):
    start = jnp.clip((i - 1) * BLK, 0, S - 3 * BLK)
    return pl.multiple_of(start, BLK)


def _attn_mask(i, start):
    keys = lax.broadcasted_iota(jnp.int32, (3 * BLK, GRP * BLK), 0)
    cols = lax.broadcasted_iota(jnp.int32, (3 * BLK, GRP * BLK), 1)
    rel = (start + keys) - (i * BLK + (cols & (BLK - 1)))
    return jnp.abs(rel) <= BLK


def _stack_heads(t, hk):
    return jnp.concatenate([t[:, (hk * GRP + g) * HD:(hk * GRP + g + 1) * HD] for g in range(GRP)], axis=0)


def _unstack_heads_t(t):
    cols = []
    for p in range(GRP // 2):
        pair = jnp.concatenate([t[:, (2 * p) * BLK:(2 * p + 1) * BLK], t[:, (2 * p + 1) * BLK:(2 * p + 2) * BLK]],
                               axis=0)
        cols.append(pair.T)
    return jnp.concatenate(cols, axis=1)


def _attn_probs(qs, kh, sk_ref, hk, mask):
    s = _nt(kh, qs) * SCALE
    s = jnp.where(mask, s, -1e30)
    sk = jnp.concatenate(
        [jnp.broadcast_to(sk_ref[0:1, hk * GRP + g:hk * GRP + g + 1], (1, BLK)) for g in range(GRP)], axis=1)
    m = jnp.maximum(jnp.max(s, axis=0, keepdims=True), sk)
    e = jnp.exp(s - m)
    es = jnp.exp(sk - m)
    inv = 1.0 / (jnp.sum(e, axis=0, keepdims=True) + es)
    return e, inv, es


ATTN_BLOCKS = 4


def _attn_fwd(qr, kr, vv, sink):
    S = qr.shape[0]
    tq = ATTN_BLOCKS * BLK

    def body(q_ref, k_ref, v_ref, sk_ref, o_ref):
        for b in range(ATTN_BLOCKS):
            i = pl.program_id(0) * ATTN_BLOCKS + b
            rows = slice(b * BLK, (b + 1) * BLK)
            start = _attn_window(i, S)
            kb = k_ref[pl.ds(start, 3 * BLK), :]
            vb = v_ref[pl.ds(start, 3 * BLK), :]
            q = q_ref[rows, :]
            mask = _attn_mask(i, start)
            outs = []
            for hk in range(NKV):
                hs = slice(hk * HD, (hk + 1) * HD)
                e, inv, _ = _attn_probs(_stack_heads(q, hk), kb[:, hs], sk_ref, hk, mask)
                outs.append(_unstack_heads_t(_tn(vb[:, hs], e) * inv))
            o_ref[rows, :] = jnp.concatenate(outs, axis=1).astype(o_ref.dtype)

    return _hbm_call(
        body, name="attn_fwd", grid=(S // tq,),
        in_specs=[_rows(tq, ATT), _full((S, KVC)), _full((S, KVC)), _full((1, NQ))],
        out_specs=_rows(tq, ATT),
        out_shape=_sds((S, ATT), _MXU),
        compiler_params=_params(40),
    )(qr, kr, vv, sink)


def _out_proj_fwd(x, yc, ya, wout, b_out, tm=512):
    S = x.shape[0]

    def body(x_ref, yc_ref, ya_ref, w_ref, b_ref, x1_ref, ymix_ref):
        ymix = jnp.concatenate([yc_ref[...], ya_ref[...]], axis=1)
        ymix_ref[...] = ymix
        x1_ref[...] = x_ref[...] + _nn(ymix, w_ref[...]) + b_ref[...]

    return _hbm_call(
        body, name="out_proj_fwd", grid=(S // tm,),
        in_specs=[_rows(tm, D), _rows(tm, CC), _rows(tm, ATT), _full((D, D)), _full((1, D))],
        out_specs=[_rows(tm, D), _rows(tm, D)],
        out_shape=[_sds((S, D), F32), _sds((S, D), _MXU)],
        compiler_params=_params(40),
    )(x, yc, ya, wout, b_out)


def _softmax_t(s):
    m = jnp.max(s, axis=0, keepdims=True)
    e = jnp.exp(s - m)
    return e, 1.0 / jnp.sum(e, axis=0, keepdims=True)


def _softmax(s):
    m = jnp.max(s, axis=-1, keepdims=True)
    e = jnp.exp(s - m)
    return e * (1.0 / jnp.sum(e, axis=-1, keepdims=True))


def _mem_attn_fwd(x1, g_q, wq, km, vm, wo, tm=512):
    S = x1.shape[0]
    M = km.shape[0]

    def body(x1_ref, g_ref, wq_ref, km_ref, vm_ref, wo_ref, x2_ref, qm_ref, om_ref):
        x1v = x1_ref[...]
        hq, _ = _rms(x1v, g_ref[...])
        qm = _nn(hq, wq_ref[...]).astype(qm_ref.dtype)
        qm_ref[...] = qm
        oms = []
        for h in range(MH):
            sl = slice(h * MHD, (h + 1) * MHD)
            p = _softmax(_nt(qm[:, sl], km_ref[:, sl]) * MSCALE)
            oms.append(_nn(p, vm_ref[:, sl]))
        om = jnp.concatenate(oms, axis=1).astype(om_ref.dtype)
        om_ref[...] = om
        x2_ref[...] = x1v + _nn(om, wo_ref[...])

    return _hbm_call(
        body, name="mem_attn_fwd", grid=(S // tm,),
        in_specs=[_rows(tm, D), _full((1, D)), _full((D, D)), _full((M, D)), _full((M, D)), _full((D, D))],
        out_specs=[_rows(tm, D)] * 3,
        out_shape=[_sds((S, D), F32), _sds((S, D), _MXU), _sds((S, D), _MXU)],
        compiler_params=_params(48),
    )(x1, g_q, wq, km, vm, wo)


def _ffn_loss_fwd(x2, tgt, g_ffn, g_final, wgT, wuT, wd, tm=512):
    S = x2.shape[0]

    def body(x2_ref, t_ref, gf_ref, gl_ref, wg_ref, wu_ref, wd_ref,
             gt_ref, up_ref, hf_ref, dx3_ref, dx3b_ref, loss_ref, dgl_ref):
        i = pl.program_id(0)
        _acc_init(i, loss_ref, dgl_ref)
        x2v = x2_ref[...]
        hf, _ = _rms(x2v, gf_ref[...])
        hb = hf.astype(hf_ref.dtype)
        hf_ref[...] = hb
        x3 = x2v
        for c in range(F // FCH):
            sl = slice(c * FCH, (c + 1) * FCH)
            gt = _nt(hb, wg_ref[sl, :])
            up = _nt(hb, wu_ref[sl, :])
            gt_ref[:, sl] = gt.astype(gt_ref.dtype)
            up_ref[:, sl] = up.astype(up_ref.dtype)
            x3 = x3 + _nn(gt * _sigmoid(gt) * up, wd_ref[sl, :])
        gl = gl_ref[...]
        y, r = _rms(x3, gl)
        diff = y - t_ref[...]
        _acc_row(loss_ref, jnp.sum(jnp.sum(diff * diff, axis=-1, keepdims=True), axis=0, keepdims=True) * (0.5 / D))
        dx3, dgl = _rms_bwd(diff * (1.0 / D), x3, r, gl)
        dx3_ref[...] = dx3
        dx3b_ref[...] = dx3.astype(dx3b_ref.dtype)
        _acc_row(dgl_ref, dgl)

    return _hbm_call(
        body, name="ffn_loss_fwd", grid=(S // tm,),
        in_specs=[_rows(tm, D), _rows(tm, D), _full((1, D)), _full((1, D)),
                  _once((F, D)), _once((F, D)), _once((F, D))],
        out_specs=[_rows(tm, F), _rows(tm, F), _rows(tm, D), _rows(tm, D), _rows(tm, D),
                   _full((8, LANES)), _full((8, D))],
        out_shape=[_sds((S, F), _MXU), _sds((S, F), _MXU), _sds((S, D), _MXU), _sds((S, D), F32),
                   _sds((S, D), _MXU), _sds((8, LANES), F32), _sds((8, D), F32)],
        compiler_params=_params(56),
    )(x2, tgt, g_ffn, g_final, wgT, wuT, wd)


def _ffn_bwd(dx3, x2, gt, up, g_ffn, wd, wgT, wuT, tm=256):
    S = x2.shape[0]

    def body(dx3_ref, x2_ref, gt_ref, up_ref, g_ref, wd_ref, wg_ref, wu_ref,
             dx2_ref, dx2b_ref, dgt_ref, dup_ref, act_ref, dg_ref):
        i = pl.program_id(0)
        _acc_init(i, dg_ref)
        dx3v = dx3_ref[...]
        db = dx3v.astype(_MXU)
        dhf = jnp.zeros((tm, D), F32)
        for c in range(F // FCH_BWD):
            sl = slice(c * FCH_BWD, (c + 1) * FCH_BWD)
            gtv = gt_ref[:, sl].astype(F32)
            upv = up_ref[:, sl].astype(F32)
            dact = _nt(db, wd_ref[sl, :])
            sig = _sigmoid(gtv)
            silu = gtv * sig
            dup = (dact * silu).astype(dup_ref.dtype)
            dgt = (dact * upv * (sig * (1.0 + gtv * (1.0 - sig)))).astype(dgt_ref.dtype)
            act_ref[:, sl] = (silu * upv).astype(act_ref.dtype)
            dgt_ref[:, sl] = dgt
            dup_ref[:, sl] = dup
            dhf = dhf + _nn(dgt, wg_ref[sl, :]) + _nn(dup, wu_ref[sl, :])
        x2v = x2_ref[...]
        g = g_ref[...]
        _, r = _rms(x2v, g)
        dxn, dg = _rms_bwd(dhf, x2v, r, g)
        dx2 = dx3v + dxn
        dx2_ref[...] = dx2
        dx2b_ref[...] = dx2.astype(dx2b_ref.dtype)
        _acc_row(dg_ref, dg)

    return _hbm_call(
        body, name="ffn_bwd", grid=(S // tm,),
        in_specs=[_rows(tm, D), _rows(tm, D), _rows(tm, F), _rows(tm, F), _full((1, D)),
                  _once((F, D)), _once((F, D)), _once((F, D))],
        out_specs=[_rows(tm, D), _rows(tm, D), _rows(tm, F), _rows(tm, F), _rows(tm, F), _full((8, D))],
        out_shape=[_sds((S, D), F32), _sds((S, D), _MXU), _sds((S, F), _MXU), _sds((S, F), _MXU),
                   _sds((S, F), _MXU), _sds((8, D), F32)],
        compiler_params=_params(56),
    )(dx3, x2, gt, up, g_ffn, wd, wgT, wuT)


_AFTER = pl.BlockSpec(memory_space=pl.ANY)


def _wgrad(a, b, name, after, rc=256):
    S, R = a.shape
    N = b.shape[1]

    def body(a_ref, b_ref, after_ref, o_ref, ob_ref):
        g = _tn(a_ref[...], b_ref[...])
        o_ref[...] = g
        ob_ref[...] = g.astype(ob_ref.dtype)

    return _hbm_call(
        body, name=name, grid=(R // rc,),
        in_specs=[pl.BlockSpec((S, rc), lambda j: (0, j)), _once((S, N)), _AFTER],
        out_specs=[_rows(rc, N), _rows(rc, N)],
        out_shape=[_sds((R, N), F32), _sds((R, N), jnp.bfloat16)],
        compiler_params=_params(56),
    )(a, b, after)


def _mem_attn_bwd(dx2, x1, qm, km, vm, g_q, wq, wo, wout, after, tm=512):
    S = x1.shape[0]
    M = km.shape[0]

    def body(dx2_ref, x1_ref, qm_ref, km_ref, vm_ref, g_ref, wq_ref, wo_ref, wout_ref, after_ref,
             dx1_ref, dx1b_ref, hq_ref, dqm_ref, dyc_ref, dya_ref, dkm_ref, dvm_ref, dg_ref, dbo_ref):
        i = pl.program_id(0)
        _acc_init(i, dkm_ref, dvm_ref, dg_ref, dbo_ref)
        dx2v = dx2_ref[...]
        domt = _nt(wo_ref[...], dx2v).astype(_MXU)
        dqs = []
        for h in range(MH):
            sl = slice(h * MHD, (h + 1) * MHD)
            qh = qm_ref[:, sl]
            kh = km_ref[:, sl]
            e, inv = _softmax_t(_nt(kh, qh) * MSCALE)
            p = e * inv
            dp = _nn(vm_ref[:, sl], domt[sl, :])
            ds = (p * (dp - jnp.sum(dp * p, axis=0, keepdims=True))).astype(_MXU)
            dqs.append(_tn(ds, kh) * MSCALE)
            dkm_ref[:, sl] += _nn(ds, qh) * MSCALE
            dvm_ref[:, sl] += _nt(p, domt[sl, :])
        dqm = jnp.concatenate(dqs, axis=1).astype(dqm_ref.dtype)
        dqm_ref[...] = dqm
        dhq = _nt(dqm, wq_ref[...])
        x1v = x1_ref[...]
        g = g_ref[...]
        hq, r = _rms(x1v, g)
        hq_ref[...] = hq.astype(hq_ref.dtype)
        dxn, dg = _rms_bwd(dhq, x1v, r, g)
        dx1 = dx2v + dxn
        dx1_ref[...] = dx1
        dx1b = dx1.astype(dx1b_ref.dtype)
        dx1b_ref[...] = dx1b
        dym = _nt(dx1b, wout_ref[...])
        dyc_ref[...] = dym[:, :CC]
        dya_ref[...] = dym[:, CC:].astype(dya_ref.dtype)
        _acc_row(dg_ref, dg)
        _acc_row(dbo_ref, jnp.sum(dx1, axis=0, keepdims=True))

    return _hbm_call(
        body, name="mem_attn_bwd", grid=(S // tm,),
        in_specs=[_rows(tm, D), _rows(tm, D), _rows(tm, D), _full((M, D)), _full((M, D)), _full((1, D)),
                  _once((D, D)), _once((D, D)), _once((D, D)), _AFTER],
        out_specs=[_rows(tm, D), _rows(tm, D), _rows(tm, D), _rows(tm, D), _rows(tm, CC), _rows(tm, ATT),
                   _full((M, D)), _full((M, D)), _full((8, D)), _full((8, D))],
        out_shape=[_sds((S, D), F32), _sds((S, D), _MXU), _sds((S, D), _MXU), _sds((S, D), _MXU),
                   _sds((S, CC), F32), _sds((S, ATT), _MXU),
                   _sds((M, D), F32), _sds((M, D), F32), _sds((8, D), F32), _sds((8, D), F32)],
        compiler_params=_params(56),
    )(dx2, x1, qm, km, vm, g_q, wq, wo, wout, after)


def _mem_kv_bwd(dkm, dvm, mem, g_kv, memn, wkvT):
    M = mem.shape[0]

    def body(dkm_ref, dvm_ref, mem_ref, g_ref, mn_ref, w_ref, gw_ref, gwb_ref, dg_ref):
        dkv = jnp.concatenate([dkm_ref[...], dvm_ref[...]], axis=1).astype(_MXU)
        gw = _tn(dkv, mn_ref[...])
        gw_ref[...] = gw
        gwb_ref[...] = gw.astype(gwb_ref.dtype)
        dmn = _nn(dkv, w_ref[...])
        memv = mem_ref[...]
        g = g_ref[...]
        _, r = _rms(memv, g)
        _, dg = _rms_bwd(dmn, memv, r, g)
        dg_ref[...] = jnp.broadcast_to(dg, dg_ref.shape)

    return _hbm_call(
        body, name="mem_kv_bwd", grid=(1,),
        in_specs=[_full((M, D)), _full((M, D)), _full((M, D)), _full((1, D)), _full((M, D)), _full((2 * D, D))],
        out_specs=[_full((2 * D, D)), _full((2 * D, D)), _full((8, D))],
        out_shape=[_sds((2 * D, D), F32), _sds((2 * D, D), jnp.bfloat16), _sds((8, D), F32)],
        compiler_params=_params(48),
    )(dkm, dvm, mem, g_kv, memn, wkvT)


def _attn_bwd(qr, kr, vv, sink, dya, after):
    S = qr.shape[0]
    tq = ATTN_BLOCKS * BLK

    def body(q_ref, k_ref, v_ref, sk_ref, do_ref, after_ref, dq_ref, dk_ref, dv_ref, dsk_ref, sink_acc):
        _acc_init(pl.program_id(0), dk_ref, dv_ref, sink_acc)
        for b in range(ATTN_BLOCKS):
            i = pl.program_id(0) * ATTN_BLOCKS + b
            rows = slice(b * BLK, (b + 1) * BLK)
            start = _attn_window(i, S)
            kb = k_ref[pl.ds(start, 3 * BLK), :]
            vb = v_ref[pl.ds(start, 3 * BLK), :]
            q = q_ref[rows, :]
            do = do_ref[rows, :]
            mask = _attn_mask(i, start)
            dqs, dks, dvs = [], [], []
            for hk in range(NKV):
                hs = slice(hk * HD, (hk + 1) * HD)
                qs = _stack_heads(q, hk)
                dos = _stack_heads(do, hk)
                e, inv, es = _attn_probs(qs, kb[:, hs], sk_ref, hk, mask)
                p = e * inv
                dp = _nt(vb[:, hs], dos)
                delta = jnp.sum(dp * p, axis=0, keepdims=True)
                ds = (p * (dp - delta)).astype(_MXU)
                dqs.append(_unstack_heads_t(_tn(kb[:, hs], ds) * SCALE))
                dks.append(_nn(ds, qs) * SCALE)
                dvs.append(_nn(p, dos))
                sink_acc[hk:hk + 1, :] -= es * inv * delta
            dq_ref[rows, :] = jnp.concatenate(dqs, axis=1)
            dk_ref[pl.ds(start, 3 * BLK), :] += jnp.concatenate(dks, axis=1)
            dv_ref[pl.ds(start, 3 * BLK), :] += jnp.concatenate(dvs, axis=1)

        @pl.when(pl.program_id(0) == pl.num_programs(0) - 1)
        def _():
            for h in range(NQ):
                tot = jnp.sum(sink_acc[h // GRP:h // GRP + 1, (h % GRP) * BLK:(h % GRP + 1) * BLK], axis=1,
                              keepdims=True)
                dsk_ref[h:h + 1, :] = jnp.broadcast_to(tot, (1, LANES))

    return _hbm_call(
        body, name="attn_bwd", grid=(S // tq,),
        in_specs=[_rows(tq, ATT), _full((S, KVC)), _full((S, KVC)), _full((1, NQ)), _rows(tq, ATT), _AFTER],
        out_specs=[_rows(tq, ATT), _full((S, KVC)), _full((S, KVC)), _full((8, LANES))],
        out_shape=[_sds((S, ATT), F32), _sds((S, KVC), F32), _sds((S, KVC), F32), _sds((8, LANES), F32)],
        scratch_shapes=[pltpu.VMEM((8, GRP * BLK), F32)],
        compiler_params=_params(40),
    )(qr, kr, vv, sink, dya, after)


def _conv_norm_bwd(dyc, y, ln_g, ln_b, after, tm=512):
    S = y.shape[0]

    def body(dyc_ref, y_ref, g_ref, b_ref, after_ref, dy_ref, dg_ref, db_ref, dbdw_ref):
        i = pl.program_id(0)
        _acc_init(i, dg_ref, db_ref, dbdw_ref)
        g = g_ref[...]
        yh, rstd, z = _ln_silu(y_ref[...], g, b_ref[...])
        sig = _sigmoid(z)
        dz = dyc_ref[...] * (sig * (1.0 + z * (1.0 - sig)))
        dyh = dz * g
        dy = rstd * (dyh - jnp.mean(dyh, axis=-1, keepdims=True)
                     - yh * jnp.mean(dyh * yh, axis=-1, keepdims=True))
        dy_ref[...] = dy
        _acc_row(dg_ref, jnp.sum(dz * yh, axis=0, keepdims=True))
        _acc_row(db_ref, jnp.sum(dz, axis=0, keepdims=True))
        _acc_row(dbdw_ref, jnp.sum(dy, axis=0, keepdims=True))

    return _hbm_call(
        body, name="conv_norm_bwd", grid=(S // tm,),
        in_specs=[_rows(tm, CC), _rows(tm, CC), _full((1, CC)), _full((1, CC)), _AFTER],
        out_specs=[_rows(tm, CC), _full((8, CC)), _full((8, CC)), _full((8, CC))],
        out_shape=[_sds((S, CC), F32), _sds((8, CC), F32), _sds((8, CC), F32), _sds((8, CC), F32)],
        compiler_params=_params(40),
    )(dyc, y, ln_g, ln_b, after)


def _conv_bwd(dy, vg, uglu, w_dw):
    S = dy.shape[0]
    tm = min(CONV_TILE, S)

    def body(dy_hbm, vg_hbm, uglu_ref, w_ref, du_ref, gw_ref, dbin_ref, dyp_ref, vgp_ref, gacc_ref, sh_ref, dvg_ref,
             sems):
        i = pl.program_id(0)
        _acc_init(i, gacc_ref, dbin_ref)
        _fill_padded(i, dy_hbm, dyp_ref, sems.at[0], S)
        _fill_padded(i, vg_hbm, vgp_ref, sems.at[1], S)
        t0 = pl.multiple_of(i * tm, tm)
        _shifted_copies(dyp_ref, sh_ref.at[0], t0, tm)
        _shifted_copies(vgp_ref, sh_ref.at[1], t0, tm)
        _dwconv_tile(sh_ref.at[0], tm, w_ref, True, dvg_ref)
        dvg = dvg_ref[...]
        sub = CONV_ROWS
        for cl in range(CC // LANES):
            lanes = slice(cl * LANES, (cl + 1) * LANES)
            for k, b, off in _conv_taps():
                part = jnp.zeros((8, LANES), F32)
                for r in range(0, tm, sub):
                    prod = (sh_ref[0, cl, 0, CONV_HALO + r:CONV_HALO + r + sub, :]
                            * sh_ref[1, cl, b, off + r:off + r + sub, :])
                    part = part + jnp.sum(prod.reshape(sub // 8, 8, LANES), axis=0)
                gacc_ref[k, :, lanes] += part
        a_v = uglu_ref[:, :CC]
        sg = _sigmoid(uglu_ref[:, CC:])
        da = dvg * sg
        dgate = dvg * a_v * sg * (1.0 - sg)
        du = jnp.concatenate([da, dgate], axis=1)
        du_ref[...] = du.astype(du_ref.dtype)
        _acc_row(dbin_ref, jnp.sum(du, axis=0, keepdims=True))

        @pl.when(i == pl.num_programs(0) - 1)
        def _():
            gw_ref[...] = jnp.zeros(gw_ref.shape, F32)
            for k in range(TAPS):
                gw_ref[k:k + 1, :] = jnp.sum(gacc_ref[k], axis=0, keepdims=True)

    return _hbm_call(
        body, name="conv_bwd", grid=(S // tm,),
        in_specs=[pl.BlockSpec(memory_space=pl.ANY), pl.BlockSpec(memory_space=pl.ANY), _rows(tm, 2 * CC),
                  _full((TAPS, CC))],
        out_specs=[_rows(tm, 2 * CC), _full((TAPS + 1, CC)), _full((8, 2 * CC))],
        out_shape=[_sds((S, 2 * CC), _MXU), _sds((TAPS + 1, CC), F32), _sds((8, 2 * CC), F32)],
        scratch_shapes=[pltpu.VMEM((S + 2 * CONV_HALO, CC), F32), pltpu.VMEM((S + 2 * CONV_HALO, CC), F32),
                        pltpu.VMEM((TAPS, 8, CC), F32), _shift_scratch(tm, 2), pltpu.VMEM((tm, CC), F32),
                        pltpu.SemaphoreType.DMA((2,))],
        compiler_params=_params(56),
    )(dy, vg, uglu, w_dw)


def _in_proj_bwd(duglu, dqr, dkr, dv, cosq, sinq, x, dx1, g_mix, winT, tm=512):
    S = x.shape[0]
    NA = ATT + 2 * KVC

    def body(dug_ref, dq_ref, dk_ref, dv_ref, c_ref, s_ref, x_ref, dx1_ref, g_ref, w_ref,
             dx_ref, du_ref, dg_ref, dbin_ref):
        i = pl.program_id(0)
        _acc_init(i, dg_ref, dbin_ref)
        c = c_ref[...]
        s = s_ref[...]
        datt = jnp.concatenate([_rope_bwd(dq_ref[...], c, s), _rope_bwd(dk_ref[...], c, s), dv_ref[...]], axis=1)
        _acc_row(dbin_ref, jnp.sum(datt, axis=0, keepdims=True))
        du = jnp.concatenate([dug_ref[...], datt.astype(du_ref.dtype)], axis=1)
        du_ref[...] = du
        dh = _nn(du, w_ref[...])
        xv = x_ref[...]
        g = g_ref[...]
        _, r = _rms(xv, g)
        dxn, dg = _rms_bwd(dh, xv, r, g)
        dx_ref[...] = dx1_ref[...] + dxn
        _acc_row(dg_ref, dg)

    return _hbm_call(
        body, name="in_proj_bwd", grid=(S // tm,),
        in_specs=[_rows(tm, 2 * CC), _rows(tm, ATT), _rows(tm, KVC), _rows(tm, KVC), _rows(tm, LANES),
                  _rows(tm, LANES), _rows(tm, D), _rows(tm, D), _full((1, D)), _once((INC, D))],
        out_specs=[_rows(tm, D), _rows(tm, INC), _full((8, D)), _full((8, NA))],
        out_shape=[_sds((S, D), F32), _sds((S, INC), _MXU), _sds((8, D), F32), _sds((8, NA), F32)],
        compiler_params=_params(48),
    )(duglu, dqr, dkr, dv, cosq, sinq, x, dx1, g_mix, winT)


def _rope_tables(S):
    pos = jnp.arange(S, dtype=F32)
    inv_freq = ROPE_THETA ** (-jnp.arange(0, HD, 2, dtype=F32) / HD)
    ang = pos[:, None] * inv_freq[None, :]
    cos, sin = jnp.cos(ang), jnp.sin(ang)
    cosq = jnp.tile(jnp.concatenate([cos, cos], axis=1), (1, LANES // HD))
    sinq = jnp.tile(jnp.concatenate([-sin, sin], axis=1), (1, LANES // HD))
    return cosq, sinq


GATHER_GROUPS = (("w_inT", "w_dw"), ("w_out", "w_mem_kvT", "w_mem_q", "w_mem_o"), ("w_gateT", "w_upT", "w_down"))


def _local_step(x, mem, tgt, sp, gather, reduce):
    S = x.shape[0]
    cosq, sinq = _rope_tables(S)

    gather.forward(0, cosq)
    w0 = gather.finish(0, sinq)
    winT, w_dw_full = w0["w_inT"], w0["w_dw"]
    h, uglu, vg, qr, kr, vv = _in_proj_fwd(x, sp["g_mix"], winT, sp["b_in"], cosq, sinq)
    ya = _attn_fwd(qr, kr, vv, sp["attn_sink"])
    gather.forward(1, ya)
    y, yc = _conv_fwd(vg, w_dw_full, sp["b_dw"], sp["g_conv_ln"], sp["b_conv_ln"])
    w1 = gather.finish(1, yc)
    wout, wkvT, wq, wo = w1["w_out"], w1["w_mem_kvT"], w1["w_mem_q"], w1["w_mem_o"]
    km, vm, memn = _mem_kv_fwd(mem, sp["g_mem_kv"], wkvT)
    x1, ymix = _out_proj_fwd(x, yc, ya, wout, sp["b_out"])
    gather.forward(2, x1)
    x2, qm, om = _mem_attn_fwd(x1, sp["g_mem_q"], wq, km, vm, wo)
    w2 = gather.finish(2, om)
    wgT, wuT, wd = w2["w_gateT"], w2["w_upT"], w2["w_down"]
    gt, up, hf, dx3, dx3b, loss_acc, dg_final = _ffn_loss_fwd(x2, tgt, sp["g_ffn"], sp["g_final"], wgT, wuT, wd)

    dx2, dx2b, dgt, dup, act, dg_ffn = _ffn_bwd(dx3, x2, gt, up, sp["g_ffn"], wd, wgT, wuT)
    token = reduce.start(0, [_wgrad(dgt, hf, "wgrad_gate", dx2), _wgrad(dup, hf, "wgrad_up", dx2),
                             _wgrad(act, dx3b, "wgrad_down", dx2)])

    dx1, dx1b, hq, dqm, dyc, dya, dkm, dvm, dg_mem_q, db_out = _mem_attn_bwd(
        dx2, x1, qm, km, vm, sp["g_mem_q"], wq, wo, wout, token)
    token = reduce.middle(0, dx1)
    g_wo = _wgrad(om, dx2b, "wgrad_mem_o", token)
    g_wq = _wgrad(hq, dqm, "wgrad_mem_q", token)
    g_wkvT_f, g_wkvT_b, dg_mem_kv = _mem_kv_bwd(dkm, dvm, mem, sp["g_mem_kv"], memn, wkvT)
    g_wout = _wgrad(ymix, dx1b, "wgrad_out", token)
    token = reduce.start(1, [g_wout, g_wq, (g_wkvT_f, g_wkvT_b), g_wo])

    dqr, dkr, dvv, dsink = _attn_bwd(qr, kr, vv, sp["attn_sink"], dya, token)
    token = reduce.middle(1, dqr)
    dy, dg_cln, db_cln, db_dw = _conv_norm_bwd(dyc, y, sp["g_conv_ln"], sp["b_conv_ln"], token)
    duglu, g_wdw, dbin_glu = _conv_bwd(dy, vg, uglu, w_dw_full)
    grad_x, du, dg_mix, dbin_att = _in_proj_bwd(duglu, dqr, dkr, dvv, cosq, sinq, x, dx1, sp["g_mix"], winT)
    token = reduce.start(2, [_wgrad(du, h, "wgrad_in", grad_x)])

    small = {
        "loss": loss_acc[0:1, :],
        "g_mix": dg_mix[0:1], "b_in": jnp.concatenate([dbin_glu[0:1], dbin_att[0:1]], axis=1),
        "b_dw": db_dw[0:1], "g_conv_ln": dg_cln[0:1], "b_conv_ln": db_cln[0:1],
        "attn_sink": dsink[:, 0].reshape(1, NQ), "b_out": db_out[0:1], "g_mem_q": dg_mem_q[0:1],
        "g_mem_kv": dg_mem_kv[0:1], "g_ffn": dg_ffn[0:1], "g_final": dg_final[0:1],
    }
    token = reduce.middle(2, token, reduce.small_zones(small, g_wdw))
    reduce.finish(0, token)
    reduce.finish(1, token)
    return grad_x


def _place():
    return lax.axis_index("x"), lax.axis_index("y"), lax.axis_index("c")


def _hbm_specs(n):
    return [pl.BlockSpec(memory_space=pl.ANY)] * n


_HBM = pl.BlockSpec(memory_space=pltpu.HBM)
_SEM = pl.BlockSpec(memory_space=pltpu.SEMAPHORE)
_ANY = pl.BlockSpec(memory_space=pl.ANY)
_EFFECT = pltpu.SideEffectType.DATAFLOW_SIDE_EFFECTING
_VMEM = pl.BlockSpec(memory_space=pltpu.VMEM)
_TOKEN = jax.ShapeDtypeStruct((8, LANES), F32)


def _dev_index(p):
    return 4 * p[0] + 2 * p[1] + p[2]


def _gather_start(groups):
    sizes = [len(g) for g in groups]
    flat = [z for g in groups for z in g]
    n = len(flat)

    def body(*refs):
        ins = refs[:n]
        sems = refs[2 * n:]
        x, y, c = _place()
        me = _dev_index((x, y, c))
        peers = [(x, y, 1 - c), (1 - x, y, c), (x, 1 - y, c), (1 - x, 1 - y, c)]
        base = 0
        for gi, size in enumerate(sizes):
            send_sems, recv_sems = sems[2 * gi], sems[2 * gi + 1]
            for k in range(size):
                zone = ins[base + k]
                for j, peer in enumerate(peers):
                    pltpu.make_async_remote_copy(
                        src_ref=zone.at[me], dst_ref=zone.at[me], send_sem=send_sems.at[4 * k + j],
                        recv_sem=recv_sems.at[4 * k + j], device_id=peer, device_id_type=MESH).start()
            base += size

    out_shape = [pltpu.HBM(z.shape, z.dtype) for z in flat]
    out_specs = [_HBM] * n
    for size in sizes:
        out_shape += [pltpu.SemaphoreType.DMA((4 * size,))] * 2
        out_specs += [_SEM, _SEM]
    res = pl.pallas_call(
        body, name="gather_start", in_specs=[_HBM] * n, out_specs=out_specs, out_shape=out_shape,
        input_output_aliases={k: k for k in range(n)},
        compiler_params=pltpu.CompilerParams(has_side_effects=_EFFECT),
    )(*[pltpu.with_memory_space_constraint(z, pltpu.HBM) for z in flat])
    out, base = [], 0
    for gi, size in enumerate(sizes):
        out.append((list(res[base:base + size]), res[n + 2 * gi], res[n + 2 * gi + 1]))
        base += size
    return out


def _gather_forward(zones, send_sems, recv_sems, after, collective_id, name):
    n = len(zones)

    def body(*refs):
        ins = refs[:n]
        send_ref, recv_ref = refs[n], refs[n + 1]
        fsend, frecv = refs[2 * n + 3], refs[2 * n + 4]
        x, y, c = _place()
        sibling = (x, y, 1 - c)
        barrier = pltpu.get_barrier_semaphore()
        pl.semaphore_signal(barrier, inc=1, device_id=sibling, device_id_type=MESH)
        pl.semaphore_wait(barrier, 1)
        chips = [(1 - x, y), (x, 1 - y), (1 - x, 1 - y)]
        for j, chip in enumerate(chips):
            for k in range(n):
                slot = ins[k].at[_dev_index((*chip, c))]
                pltpu.make_async_remote_copy(
                    src_ref=slot, dst_ref=slot, send_sem=send_ref.at[4 * k + 1 + j],
                    recv_sem=recv_ref.at[4 * k + 1 + j], device_id=sibling, device_id_type=MESH).wait_recv()
                pltpu.make_async_remote_copy(
                    src_ref=slot, dst_ref=slot, send_sem=fsend.at[3 * k + j], recv_sem=frecv.at[3 * k + j],
                    device_id=sibling, device_id_type=MESH).start()
        for k in range(n):
            own = ins[k].at[_dev_index((x, y, c))]
            theirs = ins[k].at[_dev_index(sibling)]
            pltpu.make_async_remote_copy(
                src_ref=theirs, dst_ref=theirs, send_sem=send_ref.at[4 * k], recv_sem=recv_ref.at[4 * k],
                device_id=sibling, device_id_type=MESH).wait_recv()
            for j in range(4):
                pltpu.make_async_remote_copy(
                    src_ref=own, dst_ref=own, send_sem=send_ref.at[4 * k + j], recv_sem=recv_ref.at[4 * k + j],
                    device_id=sibling, device_id_type=MESH).wait_send()

    res = pl.pallas_call(
        body, name=name, in_specs=[_HBM] * n + [_SEM, _SEM, _ANY],
        out_specs=[_HBM] * n + [_SEM, _SEM],
        out_shape=[pltpu.HBM(z.shape, z.dtype) for z in zones] + [pltpu.SemaphoreType.DMA((3 * n,))] * 2,
        input_output_aliases={k: k for k in range(n)},
        compiler_params=pltpu.CompilerParams(has_side_effects=_EFFECT, collective_id=collective_id),
    )(*zones, send_sems, recv_sems, after)
    return list(res[:n]), res[n], res[n + 1]


def _gather_finish(zones, fsend, frecv, after, name):
    n = len(zones)

    def body(*refs):
        ins = refs[:n]
        fsend_ref, frecv_ref = refs[n], refs[n + 1]
        x, y, c = _place()
        sibling = (x, y, 1 - c)
        chips = [(1 - x, y), (x, 1 - y), (1 - x, 1 - y)]
        for j, chip in enumerate(chips):
            for k in range(n):
                mine = ins[k].at[_dev_index((*chip, c))]
                theirs = ins[k].at[_dev_index((*chip, 1 - c))]
                pltpu.make_async_remote_copy(
                    src_ref=theirs, dst_ref=theirs, send_sem=fsend_ref.at[3 * k + j],
                    recv_sem=frecv_ref.at[3 * k + j], device_id=sibling, device_id_type=MESH).wait_recv()
                pltpu.make_async_remote_copy(
                    src_ref=mine, dst_ref=mine, send_sem=fsend_ref.at[3 * k + j],
                    recv_sem=frecv_ref.at[3 * k + j], device_id=sibling, device_id_type=MESH).wait_send()

    res = pl.pallas_call(
        body, name=name, in_specs=[_HBM] * n + [_SEM, _SEM, _ANY], out_specs=[_HBM] * n,
        out_shape=[pltpu.HBM(z.shape, z.dtype) for z in zones],
        input_output_aliases={k: k for k in range(n)},
        compiler_params=pltpu.CompilerParams(has_side_effects=_EFFECT),
    )(*zones, fsend, frecv, after)
    return list(res)


class _Gather:
    def __init__(self, started):
        self._phase = list(started)

    def forward(self, gi, after):
        zones, send_sems, recv_sems = self._phase[gi]
        self._phase[gi] = _gather_forward(zones, send_sems, recv_sems, after, gi + 1, "gather_forward_%d" % gi)

    def finish(self, gi, after):
        zones, fsend, frecv = self._phase[gi]
        out = {}
        for name, z in zip(GATHER_GROUPS[gi], _gather_finish(zones, fsend, frecv, after, "gather_finish_%d" % gi)):
            if name == "w_dw":
                out[name] = jnp.transpose(z, (1, 0, 2)).reshape(TAPS + 1, CC)[:TAPS]
            else:
                out[name] = z.reshape(NDEV * z.shape[1], D)
        return out


def _handshake(peers):
    barrier = pltpu.get_barrier_semaphore()
    for peer in peers:
        pl.semaphore_signal(barrier, inc=1, device_id=peer, device_id_type=MESH)
    pl.semaphore_wait(barrier, len(peers))


def _sibling_copies(srcs, lands, send_sems, recv_sems):
    x, y, c = _place()
    return [pltpu.make_async_remote_copy(
        src_ref=srcs[k].at[2 * s + (1 - c)], dst_ref=lands[k].at[s], send_sem=send_sems.at[4 * k + s],
        recv_sem=recv_sems.at[4 * k + s], device_id=(x, y, 1 - c), device_id_type=MESH)
        for k in range(len(srcs)) for s in range(4)]


def _sibling_start(gbs, collective_id, name):
    n = len(gbs)
    lands = [lax.empty((4,) + g.shape[1:], g.dtype) for g in gbs]

    def body(*refs):
        x, y, c = _place()
        _handshake([(x, y, 1 - c)])
        for cp in _sibling_copies(refs[:n], refs[n:2 * n], refs[4 * n], refs[4 * n + 1]):
            cp.start()
        refs[4 * n + 2][...] = jnp.zeros(_TOKEN.shape, _TOKEN.dtype)

    res = pl.pallas_call(
        body, name=name, in_specs=[_HBM] * (2 * n), out_specs=[_HBM] * (2 * n) + [_SEM, _SEM, _VMEM],
        out_shape=[pltpu.HBM(a.shape, a.dtype) for a in gbs + lands] + [pltpu.SemaphoreType.DMA((4 * n,))] * 2
        + [_TOKEN],
        input_output_aliases={k: k for k in range(2 * n)},
        compiler_params=pltpu.CompilerParams(has_side_effects=_EFFECT, collective_id=collective_id),
    )(*[pltpu.with_memory_space_constraint(a, pltpu.HBM) for a in gbs + lands])
    return list(res[:n]), list(res[n:2 * n]), res[2 * n], res[2 * n + 1], res[2 * n + 2]


def _sibling_wait(gbs, lands, send_sems, recv_sems, after, name):
    n = len(gbs)

    def body(*refs):
        for cp in _sibling_copies(refs[:n], refs[n:2 * n], refs[2 * n], refs[2 * n + 1]):
            cp.wait_recv()
            cp.wait_send()

    res = pl.pallas_call(
        body, name=name, in_specs=[_HBM] * (2 * n) + [_SEM, _SEM, _ANY], out_specs=[_HBM] * (2 * n),
        out_shape=[pltpu.HBM(a.shape, a.dtype) for a in gbs + lands],
        input_output_aliases={k: k for k in range(2 * n)},
        compiler_params=pltpu.CompilerParams(has_side_effects=_EFFECT),
    )(*gbs, *lands, send_sems, recv_sems, after)
    return list(res[n:])


def _pair_sum(gfs, ts, c_idx, name):
    n = len(gfs)
    halves = 2

    def body(c_ref, *refs):
        for k in range(n):
            refs[2 * n + k][...] = (refs[k][...] + refs[n + k][...].astype(F32)).astype(jnp.bfloat16)

    in_specs, out_specs, out_shape = [], [], []
    for g in gfs:
        rh = g.shape[1] // halves
        in_specs.append(pl.BlockSpec((None, rh, D), lambda s, h, c_ref: (2 * s + c_ref[0], h, 0)))
    for g in gfs:
        rh = g.shape[1] // halves
        in_specs.append(pl.BlockSpec((None, rh, D), lambda s, h, c_ref: (s, h, 0)))
        out_specs.append(pl.BlockSpec((None, rh, D), lambda s, h, c_ref: (s, h, 0)))
        out_shape.append(_sds((4, g.shape[1], D), jnp.bfloat16))
    return pl.pallas_call(
        body, name=name,
        grid_spec=pltpu.PrefetchScalarGridSpec(num_scalar_prefetch=1, grid=(4, halves), in_specs=in_specs,
                                               out_specs=out_specs),
        out_shape=out_shape,
        compiler_params=pltpu.CompilerParams(dimension_semantics=("arbitrary", "arbitrary"),
                                             vmem_limit_bytes=48 * 2 ** 20),
    )(c_idx, *gfs, *ts)


def _chip_copies(pbs, lands, smalls, send_sems, recv_sems):
    x, y, c = _place()
    own = 2 * x + y
    me = _dev_index((x, y, c))
    chips = [(1 - x, y), (x, 1 - y), (1 - x, 1 - y)]
    n = len(pbs)
    copies = [pltpu.make_async_remote_copy(
        src_ref=pbs[k].at[2 * chip[0] + chip[1]], dst_ref=lands[k].at[own], send_sem=send_sems.at[3 * k + j],
        recv_sem=recv_sems.at[3 * k + j], device_id=(*chip, c), device_id_type=MESH)
        for k in range(n) for j, chip in enumerate(chips)]
    peers = [(x, y, 1 - c)] + [(*chip, cc) for chip in chips for cc in (c, 1 - c)]
    copies += [pltpu.make_async_remote_copy(
        src_ref=smalls[t].at[me], dst_ref=smalls[t].at[me], send_sem=send_sems.at[3 * n + 7 * t + j],
        recv_sem=recv_sems.at[3 * n + 7 * t + j], device_id=peer, device_id_type=MESH)
        for t in range(len(smalls)) for j, peer in enumerate(peers)]
    return copies


def _chip_start(pbs, smalls, collective_id, name):
    n, m = len(pbs), len(smalls)
    lands = [lax.empty(p.shape, p.dtype) for p in pbs]
    arrays = list(pbs) + lands + list(smalls)
    na = len(arrays)
    nsem = 3 * n + 7 * m

    def body(*refs):
        x, y, c = _place()
        chips = [(1 - x, y), (x, 1 - y), (1 - x, 1 - y)]
        peers = [(*chip, c) for chip in chips]
        if m:
            peers += [(x, y, 1 - c)] + [(*chip, 1 - c) for chip in chips]
        _handshake(peers)
        for cp in _chip_copies(refs[:n], refs[n:2 * n], refs[2 * n:na], refs[2 * na], refs[2 * na + 1]):
            cp.start()
        refs[2 * na + 2][...] = jnp.zeros(_TOKEN.shape, _TOKEN.dtype)

    res = pl.pallas_call(
        body, name=name, in_specs=[_HBM] * na, out_specs=[_HBM] * na + [_SEM, _SEM, _VMEM],
        out_shape=[pltpu.HBM(a.shape, a.dtype) for a in arrays] + [pltpu.SemaphoreType.DMA((nsem,))] * 2 + [_TOKEN],
        input_output_aliases={k: k for k in range(na)},
        compiler_params=pltpu.CompilerParams(has_side_effects=_EFFECT, collective_id=collective_id),
    )(*[pltpu.with_memory_space_constraint(a, pltpu.HBM) for a in arrays])
    return (list(res[:n]), list(res[n:2 * n]), list(res[2 * n:na]), res[na], res[na + 1]), res[na + 2]


def _chip_wait(pbs, lands, smalls, send_sems, recv_sems, after, name):
    n = len(pbs)
    arrays = list(pbs) + list(lands) + list(smalls)
    na = len(arrays)

    def body(*refs):
        for cp in _chip_copies(refs[:n], refs[n:2 * n], refs[2 * n:na], refs[na], refs[na + 1]):
            cp.wait_recv()
            cp.wait_send()

    res = pl.pallas_call(
        body, name=name, in_specs=[_HBM] * na + [_SEM, _SEM, _ANY], out_specs=[_HBM] * na,
        out_shape=[pltpu.HBM(a.shape, a.dtype) for a in arrays],
        input_output_aliases={k: k for k in range(na)},
        compiler_params=pltpu.CompilerParams(has_side_effects=_EFFECT),
    )(*arrays, send_sems, recv_sems, after)
    return list(res[:n]), list(res[n:2 * n]), list(res[2 * n:])


def _chip_sum(pbs, qs, own_idx, states, name):
    n = len(pbs)
    halves = 2
    n_in = 4 * n + 3 * sum(s is not None for s in states)

    def body(o_ref, *refs):
        nxt_in, nxt_out = 4 * n, n_in
        for k in range(n):
            acc = refs[4 * k][...].astype(F32)
            for j in range(1, 4):
                acc = acc + refs[4 * k + j][...].astype(F32)
            refs[nxt_out][...] = acc
            nxt_out += 1
            if states[k] is not None:
                w_ref, m_ref, v_ref = refs[nxt_in:nxt_in + 3]
                nxt_in += 3
                d, nm, nv = _adamw_math(w_ref[...], acc, m_ref[...], v_ref[...])
                refs[nxt_out][...] = d
                refs[nxt_out + 1][...] = nm
                refs[nxt_out + 2][...] = nv
                nxt_out += 3

    def other(j):
        return lambda h, o: ((o[0] + j) % 4, h, 0)

    def rows(r):
        return pl.BlockSpec((r // halves, D), lambda h, o: (h, 0))

    in_specs, ins = [], []
    for p, q in zip(pbs, qs):
        rh = p.shape[1] // halves
        in_specs.append(pl.BlockSpec((None, rh, D), lambda h, o: (o[0], h, 0)))
        ins.append(p)
        for j in range(1, 4):
            in_specs.append(pl.BlockSpec((None, rh, D), other(j)))
            ins.append(q)
    out_specs, out_shape, counts = [], [], []
    for p, s in zip(pbs, states):
        r = p.shape[1]
        if s is not None:
            in_specs += [rows(r)] * 3
            ins += list(s)
        counts.append(1 if s is None else 4)
        out_specs += [rows(r)] * counts[-1]
        out_shape += [_sds((r, D), F32)] * counts[-1]
    res = pl.pallas_call(
        body, name=name,
        grid_spec=pltpu.PrefetchScalarGridSpec(num_scalar_prefetch=1, grid=(halves,), in_specs=in_specs,
                                               out_specs=out_specs),
        out_shape=out_shape,
        compiler_params=pltpu.CompilerParams(dimension_semantics=("arbitrary",), vmem_limit_bytes=56 * 2 ** 20),
    )(own_idx, *ins)
    out, base = [], 0
    for c in counts:
        out.append(tuple(res[base:base + c]))
        base += c
    return out


REDUCE_GROUPS = (("w_gateT", "w_upT", "w_down"), ("w_out", "w_mem_q", "w_mem_kvT", "w_mem_o"), ("w_inT",))


class _Reduce:
    def __init__(self, states):
        x, y, c = _place()
        self._c = c.astype(jnp.int32).reshape(1)
        self._own = (2 * x + y).astype(jnp.int32).reshape(1)
        self._me = _dev_index((x, y, c))
        self._state = {}
        self._adam = states
        self.results = {}
        self.smalls = None

    def start(self, gi, wgrads):
        gfs = [g[0].reshape(NDEV, g[0].shape[0] // NDEV, D) for g in wgrads]
        gbs = [g[1].reshape(NDEV, g[1].shape[0] // NDEV, D) for g in wgrads]
        gbs, lands, send_sems, recv_sems, token = _sibling_start(gbs, 4 + gi, "reduce_sibling_start_%d" % gi)
        self._state[gi] = (gfs, gbs, lands, send_sems, recv_sems)
        return token

    def middle(self, gi, after, smalls=()):
        gfs, gbs, lands, send_sems, recv_sems = self._state[gi]
        ts = _sibling_wait(gbs, lands, send_sems, recv_sems, after, "reduce_sibling_wait_%d" % gi)
        pbs = _pair_sum(gfs, ts, self._c, "reduce_pair_sum_%d" % gi)
        self._state[gi], token = _chip_start(pbs, list(smalls), 7 + gi, "reduce_chip_start_%d" % gi)
        return token

    def finish(self, gi, after):
        pbs, lands, smalls, send_sems, recv_sems = self._state[gi]
        pbs, qs, smalls = _chip_wait(pbs, lands, smalls, send_sems, recv_sems, after, "reduce_chip_wait_%d" % gi)
        names = REDUCE_GROUPS[gi]
        states = [self._adam.get(n) for n in names]
        self.results.update(zip(names, _chip_sum(pbs, qs, self._own, states, "reduce_update_%d" % gi)))
        if smalls:
            self.smalls = smalls

    def small_zones(self, small, g_wdw):
        pack = jnp.concatenate(
            [jnp.pad(small[name], ((0, 0), (0, lanes - small[name].shape[1]))) for name, lanes, _ in PACK], axis=1)
        zones = []
        for a in (pack, g_wdw):
            zone = lax.empty((NDEV,) + a.shape, a.dtype)
            zones.append(lax.dynamic_update_slice(zone, a[None], (self._me,) + (0,) * a.ndim))
        return zones


def _adamw_math(w, g, m, v):
    m = ADAM_B1 * m + (1.0 - ADAM_B1) * g
    v = ADAM_B2 * v + (1.0 - ADAM_B2) * (g * g)
    m_hat = m / (1.0 - ADAM_B1 ** ADAM_STEP)
    v_hat = v / (1.0 - ADAM_B2 ** ADAM_STEP)
    delta = -ADAM_LR * (m_hat / (jnp.sqrt(v_hat) + ADAM_EPS) + ADAM_WD * w)
    return delta, m, v


def _adamw(w, g, m, v, name):
    R, C = w.shape
    rc = 256 if R % 256 == 0 and R > 256 else R

    def body(w_ref, g_ref, m_ref, v_ref, d_ref, nm_ref, nv_ref):
        d, nm, nv = _adamw_math(w_ref[...], g_ref[...], m_ref[...], v_ref[...])
        d_ref[...] = d
        nm_ref[...] = nm
        nv_ref[...] = nv

    return _hbm_call(
        body, name=name, grid=(R // rc,),
        in_specs=[_rows(rc, C)] * 4, out_specs=[_rows(rc, C)] * 3,
        out_shape=[_sds((R, C), F32)] * 3,
        compiler_params=_params(40),
    )(w, g, m, v)


PACK = (("loss", LANES, 1), ("g_mix", D, D), ("b_in", INC, INC), ("b_dw", CC, CC), ("g_conv_ln", CC, CC),
        ("b_conv_ln", CC, CC), ("attn_sink", LANES, NQ), ("b_out", D, D), ("g_mem_q", D, D), ("g_mem_kv", D, D),
        ("g_ffn", D, D), ("g_final", D, D))
PACK_LANES = sum(p[1] for p in PACK)


def _small_update(packs, gdws, ws, ms, vs):
    names = [p[0] for p in PACK[1:]]
    n = len(names)

    def body(*refs):
        packs_ref, gdws_ref = refs[0], refs[1]
        w_refs, m_refs, v_refs = refs[2:2 + n], refs[2 + n:2 + 2 * n], refs[2 + 2 * n:2 + 3 * n]
        outs = refs[2 + 3 * n:]
        tot = packs_ref[0]
        gdw = gdws_ref[0]
        for s in range(1, NDEV):
            tot = tot + packs_ref[s]
            gdw = gdw + gdws_ref[s]
        outs[0][...] = tot[:, 0:LANES]
        outs[1][...] = gdw
        off = LANES
        for k, (_, lanes, used) in enumerate(PACK[1:]):
            g = tot[:, off:off + used]
            off += lanes
            d, nm, nv = _adamw_math(w_refs[k][...], g, m_refs[k][...], v_refs[k][...])
            o = outs[2 + 4 * k:6 + 4 * k]
            o[0][...] = g
            o[1][...] = d
            o[2][...] = nm
            o[3][...] = nv

    def full(a):
        nd = len(a.shape)
        return pl.BlockSpec(a.shape, lambda i: (0,) * nd)

    ins = [packs, gdws, *ws, *ms, *vs]
    out_shape = [_sds((1, LANES), F32), _sds(gdws.shape[1:], F32)]
    for w in ws:
        out_shape += [_sds(w.shape, F32)] * 4
    return pl.pallas_call(
        body, name="small_update", grid=(1,),
        in_specs=[full(a) for a in ins], out_specs=[full(o) for o in out_shape], out_shape=out_shape,
        compiler_params=_params(40),
    )(*ins)


def kernel(x, mem, g_mix, w_in, b_in, w_dw, b_dw, g_conv_ln, b_conv_ln, attn_sink, w_out, b_out, g_mem_q, g_mem_kv, w_mem_q, w_mem_kv, w_mem_o, g_ffn, w_gate, w_up, w_down, g_final, loss_target, m_g_mix, m_w_in, m_b_in, m_w_dw, m_b_dw, m_g_conv_ln, m_b_conv_ln, m_attn_sink, m_w_out, m_b_out, m_g_mem_q, m_g_mem_kv, m_w_mem_q, m_w_mem_kv, m_w_mem_o, m_g_ffn, m_w_gate, m_w_up, m_w_down, m_g_final, v_g_mix, v_w_in, v_b_in, v_w_dw, v_b_dw, v_g_conv_ln, v_b_conv_ln, v_attn_sink, v_w_out, v_b_out, v_g_mem_q, v_g_mem_kv, v_w_mem_q, v_w_mem_kv, v_w_mem_o, v_g_ffn, v_w_gate, v_w_up, v_w_down, v_g_final):
    given = dict(locals())
    bf = jnp.bfloat16
    xi, yi, ci = _place()
    me = 4 * xi + 2 * yi + ci

    transposed = {"w_inT": w_in, "w_mem_kvT": w_mem_kv, "w_gateT": w_gate, "w_upT": w_up}
    plain = {"w_out": w_out, "w_mem_q": w_mem_q, "w_mem_o": w_mem_o, "w_down": w_down}
    blocks = {n: (transposed[n][0].T if n in transposed else plain[n][0]).astype(bf) for n in W_NAMES}
    blocks["w_dw"] = jnp.pad(w_dw[0], ((0, 1), (0, 0)))

    def landing_zone(b):
        zone = lax.empty((NDEV,) + b.shape, b.dtype)
        return lax.dynamic_update_slice(zone, b[None], (me,) + (0,) * b.ndim)

    gather = _Gather(_gather_start([[landing_zone(blocks[n]) for n in names] for names in GATHER_GROUPS]))

    sp = {n: given[n] for n in ("g_mix", "b_in", "b_dw", "g_conv_ln", "b_conv_ln", "attn_sink", "b_out", "g_mem_q",
                                "g_mem_kv", "g_ffn")}
    sp["g_final"] = g_final.reshape(1, D)
    frames = {"w_inT": ("w_in", True), "w_gateT": ("w_gate", True), "w_upT": ("w_up", True),
              "w_out": ("w_out", False), "w_mem_q": ("w_mem_q", False), "w_mem_o": ("w_mem_o", False),
              "w_down": ("w_down", False)}
    states = {}
    for frame, (n, transposed) in frames.items():
        wmv = (given[n][0], given["m_" + n][0], given["v_" + n][0])
        states[frame] = tuple(a.T for a in wmv) if transposed else wmv
    reduce = _Reduce(states)
    grad_x = _local_step(x[0], mem[0], loss_target[0], sp, gather, reduce)
    reduce.finish(2, reduce.results["w_out"][1])

    grads, deltas, new_m, new_v = {}, {}, {}, {}
    for frame, (n, transposed) in frames.items():
        outs = [a.T if transposed else a for a in reduce.results[frame]]
        grads[n], deltas[n], new_m[n], new_v[n] = [a[None] for a in outs]

    def update(n, g):
        d, nm, nv = _adamw(given[n][0], g, given["m_" + n][0], given["v_" + n][0], "adamw_" + n)
        grads[n], deltas[n], new_m[n], new_v[n] = g[None], d[None], nm[None], nv[None]

    update("w_mem_kv", reduce.results["w_mem_kvT"][0].T)

    packs, gdws = reduce.smalls
    names_small = [p[0] for p in PACK[1:]]

    def as_row(a):
        return a.reshape(1, -1)

    res = _small_update(packs, gdws, [as_row(given[n]) for n in names_small],
                        [as_row(given["m_" + n]) for n in names_small],
                        [as_row(given["v_" + n]) for n in names_small])
    loss = res[0][0, 0]
    for k, n in enumerate(names_small):
        shape = given[n].shape
        g, d, nm, nv = res[2 + 4 * k:6 + 4 * k]
        grads[n], deltas[n], new_m[n], new_v[n] = (g.reshape(shape), d.reshape(shape), nm.reshape(shape),
                                                   nv.reshape(shape))
    update("w_dw", lax.dynamic_slice(res[1], (0, me * (CC // NDEV)), (TAPS, CC // NDEV)))

    order = ("g_mix", "w_in", "b_in", "w_dw", "b_dw", "g_conv_ln", "b_conv_ln", "attn_sink", "w_out", "b_out",
             "g_mem_q", "g_mem_kv", "w_mem_q", "w_mem_kv", "w_mem_o", "g_ffn", "w_gate", "w_up", "w_down", "g_final")
    return (loss, grad_x[None], *[grads[n] for n in order], *[deltas[n] for n in order],
            *[new_m[n] for n in order], *[new_v[n] for n in order])
```

```python
import functools

import jax
import jax.numpy as jnp
from jax import lax
from jax.experimental import pallas as pl
from jax.experimental.pallas import tpu as pltpu

D = 1024
CC = 512
HD = 64
NQ = 8
NKV = 2
GRP = NQ // NKV
ATT = NQ * HD
KVC = NKV * HD
BLK = 128
MH = 4
MHD = D // MH
F = 2816
FCH = 256
FCH_BWD = F // 2
INC = 2 * CC + ATT + 2 * KVC
TAPS = 31
EPS = 1e-6
ROPE_THETA = 10000.0
SCALE = HD ** -0.5
MSCALE = MHD ** -0.5
LANES = 128
NDEV = 8

ADAM_LR = 0.001
ADAM_B1 = 0.9
ADAM_B2 = 0.999
ADAM_EPS = 1e-08
ADAM_WD = 0.01
ADAM_STEP = 10

F32 = jnp.float32
_MXU = jnp.bfloat16
MESH = pl.DeviceIdType.MESH

W_NAMES = ("w_inT", "w_out", "w_mem_q", "w_mem_kvT", "w_mem_o", "w_gateT", "w_upT", "w_down")
W_ROWS = (INC // NDEV, D // NDEV, D // NDEV, 2 * D // NDEV, D // NDEV, F // NDEV, F // NDEV, F // NDEV)


def _mm(a, b, ca, cb):
    return lax.dot_general(a.astype(_MXU), b.astype(_MXU), (((ca,), (cb,)), ((), ())),
                           preferred_element_type=F32)


def _nn(a, b):
    return _mm(a, b, 1, 0)


def _nt(a, b):
    return _mm(a, b, 1, 1)


def _tn(a, b):
    return _mm(a, b, 0, 0)


def _sigmoid(x):
    return 1.0 / (1.0 + jnp.exp(-x))


def _rms(x, g):
    r = lax.rsqrt(jnp.mean(x * x, axis=-1, keepdims=True) + EPS)
    return x * r * g, r


def _rms_bwd(dy, x, r, g):
    xh = x * r
    dg = jnp.sum(dy * xh, axis=0, keepdims=True)
    dxh = dy * g
    dx = r * (dxh - xh * jnp.mean(dxh * xh, axis=-1, keepdims=True))
    return dx, dg


def _rope(t, c, s):
    n = t.shape[1]
    reps = n // LANES
    if reps > 1:
        c = jnp.tile(c, (1, reps))
        s = jnp.tile(s, (1, reps))
    lane = lax.broadcasted_iota(jnp.int32, t.shape, 1)
    first = (lane & (HD - 1)) < (HD // 2)
    partner = jnp.where(first, pltpu.roll(t, n - HD // 2, 1), pltpu.roll(t, HD // 2, 1))
    return t * c + partner * s


def _rope_bwd(dt, c, s):
    n = dt.shape[1]
    reps = n // LANES
    if reps > 1:
        c = jnp.tile(c, (1, reps))
        s = jnp.tile(s, (1, reps))
    lane = lax.broadcasted_iota(jnp.int32, dt.shape, 1)
    first = (lane & (HD - 1)) < (HD // 2)
    ds = dt * s
    partner = jnp.where(first, pltpu.roll(ds, n - HD // 2, 1), pltpu.roll(ds, HD // 2, 1))
    return dt * c + partner


def _acc_init(i, *refs):
    @pl.when(i == 0)
    def _():
        for r in refs:
            r[...] = jnp.zeros(r.shape, r.dtype)


def _acc_row(ref, v):
    ref[...] += jnp.broadcast_to(v, ref.shape)


def _rows(tm, n):
    return pl.BlockSpec((tm, n), lambda i: (i, 0))


def _full(shape):
    nd = len(shape)
    return pl.BlockSpec(shape, lambda i: (0,) * nd)


def _once(shape):
    nd = len(shape)
    return pl.BlockSpec(shape, lambda i: (0,) * nd, pipeline_mode=pl.Buffered(1))


def _params(vmem_mb):
    return pltpu.CompilerParams(dimension_semantics=("arbitrary",), vmem_limit_bytes=vmem_mb * 2 ** 20)


def _sds(shape, dtype):
    return pltpu.HBM(shape, dtype)


def _hbm_call(*args, **kwargs):
    call = pl.pallas_call(*args, **kwargs)
    return lambda *ops: call(*[pltpu.with_memory_space_constraint(o, pltpu.HBM) for o in ops])


def _mem_kv_fwd(mem, g_kv, wkvT):
    M = mem.shape[0]

    def body(mem_ref, g_ref, w_ref, km_ref, vm_ref, mn_ref):
        y, _ = _rms(mem_ref[...], g_ref[...])
        kv = _nt(y, w_ref[...])
        km_ref[...] = kv[:, :D].astype(km_ref.dtype)
        vm_ref[...] = kv[:, D:].astype(vm_ref.dtype)
        mn_ref[...] = y.astype(mn_ref.dtype)

    return _hbm_call(
        body, name="mem_kv_fwd", grid=(1,),
        in_specs=[_full((M, D)), _full((1, D)), _full((2 * D, D))],
        out_specs=[_full((M, D))] * 3,
        out_shape=[_sds((M, D), _MXU)] * 3,
        compiler_params=_params(40),
    )(mem, g_kv, wkvT)


def _in_proj_fwd(x, g_mix, winT, b_in, cosq, sinq, tm=512):
    S = x.shape[0]

    def body(x_ref, g_ref, w_ref, b_ref, c_ref, s_ref, h_ref, uglu_ref, vg_ref, q_ref, k_ref, v_ref):
        h, _ = _rms(x_ref[...], g_ref[...])
        hb = h.astype(h_ref.dtype)
        h_ref[...] = hb
        u = _nt(hb, w_ref[...]) + b_ref[...]
        uglu_ref[...] = u[:, :2 * CC]
        vg_ref[...] = u[:, :CC] * _sigmoid(u[:, CC:2 * CC])
        c = c_ref[...]
        s = s_ref[...]
        q_ref[...] = _rope(u[:, 2 * CC:2 * CC + ATT], c, s).astype(q_ref.dtype)
        k_ref[...] = _rope(u[:, 2 * CC + ATT:2 * CC + ATT + KVC], c, s).astype(k_ref.dtype)
        v_ref[...] = u[:, 2 * CC + ATT + KVC:].astype(v_ref.dtype)

    return _hbm_call(
        body, name="in_proj_fwd", grid=(S // tm,),
        in_specs=[_rows(tm, D), _full((1, D)), _full((INC, D)), _full((1, INC)), _rows(tm, LANES), _rows(tm, LANES)],
        out_specs=[_rows(tm, D), _rows(tm, 2 * CC), _rows(tm, CC), _rows(tm, ATT), _rows(tm, KVC), _rows(tm, KVC)],
        out_shape=[_sds((S, D), _MXU), _sds((S, 2 * CC), F32), _sds((S, CC), F32), _sds((S, ATT), _MXU),
                   _sds((S, KVC), _MXU), _sds((S, KVC), _MXU)],
        compiler_params=_params(48),
    )(x, g_mix, winT, b_in, cosq, sinq)


CONV_HALO = 16


CONV_TILE = 256


def _shift_scratch(tm, sets):
    return pltpu.VMEM((sets, CC // LANES, 8, tm + 24, LANES), F32)


def _shifted_copies(pad_ref, sh_ref, t0, tm):
    for cl in range(CC // LANES):
        blk = pad_ref[pl.ds(t0, tm + 2 * CONV_HALO), cl * LANES:(cl + 1) * LANES]
        for b in range(8):
            sh_ref[cl, b, :, :] = blk[b:b + tm + 24, :]


def _conv_taps():
    return [(8 * a + b - 1, b, 8 * a) for b in range(8) for a in range(4) if 0 <= 8 * a + b - 1 < TAPS]


CONV_ROWS = 64
SUBLANES = 8


def _dwconv_tile(sh_ref, tm, w_ref, flip, out_ref):
    for cl in range(CC // LANES):
        lanes = slice(cl * LANES, (cl + 1) * LANES)

        def chunk(c, carry):
            r = pl.multiple_of(c * CONV_ROWS, CONV_ROWS)
            accs = [None] * 4
            for n, (k, b, off) in enumerate(_conv_taps()):
                kk = TAPS - 1 - k if flip else k
                term = sh_ref[cl, b, pl.ds(off + r, CONV_ROWS), :] * w_ref[kk:kk + 1, lanes]
                accs[n % 4] = term if accs[n % 4] is None else accs[n % 4] + term
            out_ref[pl.ds(r, CONV_ROWS), lanes] = (accs[0] + accs[1]) + (accs[2] + accs[3])
            return carry

        lax.fori_loop(0, tm // CONV_ROWS, chunk, 0)


def _fill_padded(i, src_hbm, pad_ref, sem, S):
    @pl.when(i == 0)
    def _():
        zeros = jnp.zeros((CONV_HALO, pad_ref.shape[1]), pad_ref.dtype)
        pad_ref[0:CONV_HALO, :] = zeros
        pad_ref[CONV_HALO + S:2 * CONV_HALO + S, :] = zeros
        cp = pltpu.make_async_copy(src_hbm, pad_ref.at[pl.ds(CONV_HALO, S), :], sem)
        cp.start()
        cp.wait()


def _ln_silu(y, g, b):
    mu = jnp.mean(y, axis=-1, keepdims=True)
    yc = y - mu
    var = jnp.mean(yc * yc, axis=-1, keepdims=True)
    rstd = lax.rsqrt(var + EPS)
    yh = yc * rstd
    z = yh * g + b
    return yh, rstd, z


def _conv_fwd(vg, w_dw, b_dw, ln_g, ln_b):
    S = vg.shape[0]
    tm = min(CONV_TILE, S)

    def body(vg_hbm, w_ref, b_ref, g_ref, bb_ref, y_ref, yc_ref, pad_ref, sh_ref, sem):
        i = pl.program_id(0)
        _fill_padded(i, vg_hbm, pad_ref, sem, S)
        _shifted_copies(pad_ref, sh_ref.at[0], pl.multiple_of(i * tm, tm), tm)
        _dwconv_tile(sh_ref.at[0], tm, w_ref, False, y_ref)
        y = y_ref[...] + b_ref[...]
        y_ref[...] = y
        _, _, z = _ln_silu(y, g_ref[...], bb_ref[...])
        yc_ref[...] = (z * _sigmoid(z)).astype(yc_ref.dtype)

    return _hbm_call(
        body, name="conv_fwd", grid=(S // tm,),
        in_specs=[pl.BlockSpec(memory_space=pl.ANY), _full((TAPS, CC)), _full((1, CC)), _full((1, CC)), _full((1, CC))],
        out_specs=[_rows(tm, CC), _rows(tm, CC)],
        out_shape=[_sds((S, CC), F32), _sds((S, CC), _MXU)],
        scratch_shapes=[pltpu.VMEM((S + 2 * CONV_HALO, CC), F32), _shift_scratch(tm, 1), pltpu.SemaphoreType.DMA],
        compiler_params=_params(40),
    )(vg, w_dw, b_dw, ln_g, ln_b)


def _attn_window(i, S):
    start = jnp.clip((i - 1) * BLK, 0, S - 3 * BLK)
    return pl.multiple_of(start, BLK)


def _attn_mask(i, start):
    keys = lax.broadcasted_iota(jnp.int32, (3 * BLK, GRP * BLK), 0)
    cols = lax.broadcasted_iota(jnp.int32, (3 * BLK, GRP * BLK), 1)
    rel = (start + keys) - (i * BLK + (cols & (BLK - 1)))
    return jnp.abs(rel) <= BLK


def _stack_heads(t, hk):
    return jnp.concatenate([t[:, (hk * GRP + g) * HD:(hk * GRP + g + 1) * HD] for g in range(GRP)], axis=0)


def _unstack_heads_t(t):
    cols = []
    for p in range(GRP // 2):
        pair = jnp.concatenate([t[:, (2 * p) * BLK:(2 * p + 1) * BLK], t[:, (2 * p + 1) * BLK:(2 * p + 2) * BLK]],
                               axis=0)
        cols.append(pair.T)
    return jnp.concatenate(cols, axis=1)


def _attn_probs(qs, kh, sk_ref, hk, mask):
    s = _nt(kh, qs) * SCALE
    s = jnp.where(mask, s, -1e30)
    sk = jnp.concatenate(
        [jnp.broadcast_to(sk_ref[0:1, hk * GRP + g:hk * GRP + g + 1], (1, BLK)) for g in range(GRP)], axis=1)
    m = jnp.maximum(jnp.max(s, axis=0, keepdims=True), sk)
    e = jnp.exp(s - m)
    es = jnp.exp(sk - m)
    inv = 1.0 / (jnp.sum(e, axis=0, keepdims=True) + es)
    return e, inv, es


ATTN_BLOCKS = 4


def _attn_fwd(qr, kr, vv, sink):
    S = qr.shape[0]
    tq = ATTN_BLOCKS * BLK

    def body(q_ref, k_ref, v_ref, sk_ref, o_ref):
        for b in range(ATTN_BLOCKS):
            i = pl.program_id(0) * ATTN_BLOCKS + b
            rows = slice(b * BLK, (b + 1) * BLK)
            start = _attn_window(i, S)
            kb = k_ref[pl.ds(start, 3 * BLK), :]
            vb = v_ref[pl.ds(start, 3 * BLK), :]
            q = q_ref[rows, :]
            mask = _attn_mask(i, start)
            outs = []
            for hk in range(NKV):
                hs = slice(hk * HD, (hk + 1) * HD)
                e, inv, _ = _attn_probs(_stack_heads(q, hk), kb[:, hs], sk_ref, hk, mask)
                outs.append(_unstack_heads_t(_tn(vb[:, hs], e) * inv))
            o_ref[rows, :] = jnp.concatenate(outs, axis=1).astype(o_ref.dtype)

    return _hbm_call(
        body, name="attn_fwd", grid=(S // tq,),
        in_specs=[_rows(tq, ATT), _full((S, KVC)), _full((S, KVC)), _full((1, NQ))],
        out_specs=_rows(tq, ATT),
        out_shape=_sds((S, ATT), _MXU),
        compiler_params=_params(40),
    )(qr, kr, vv, sink)


def _out_proj_fwd(x, yc, ya, wout, b_out, tm=512):
    S = x.shape[0]

    def body(x_ref, yc_ref, ya_ref, w_ref, b_ref, x1_ref, ymix_ref):
        ymix = jnp.concatenate([yc_ref[...], ya_ref[...]], axis=1)
        ymix_ref[...] = ymix
        x1_ref[...] = x_ref[...] + _nn(ymix, w_ref[...]) + b_ref[...]

    return _hbm_call(
        body, name="out_proj_fwd", grid=(S // tm,),
        in_specs=[_rows(tm, D), _rows(tm, CC), _rows(tm, ATT), _full((D, D)), _full((1, D))],
        out_specs=[_rows(tm, D), _rows(tm, D)],
        out_shape=[_sds((S, D), F32), _sds((S, D), _MXU)],
        compiler_params=_params(40),
    )(x, yc, ya, wout, b_out)


def _softmax_t(s):
    m = jnp.max(s, axis=0, keepdims=True)
    e = jnp.exp(s - m)
    return e, 1.0 / jnp.sum(e, axis=0, keepdims=True)


def _softmax(s):
    m = jnp.max(s, axis=-1, keepdims=True)
    e = jnp.exp(s - m)
    return e * (1.0 / jnp.sum(e, axis=-1, keepdims=True))


def _mem_attn_fwd(x1, g_q, wq, km, vm, wo, tm=512):
    S = x1.shape[0]
    M = km.shape[0]

    def body(x1_ref, g_ref, wq_ref, km_ref, vm_ref, wo_ref, x2_ref, qm_ref, om_ref):
        x1v = x1_ref[...]
        hq, _ = _rms(x1v, g_ref[...])
        qm = _nn(hq, wq_ref[...]).astype(qm_ref.dtype)
        qm_ref[...] = qm
        oms = []
        for h in range(MH):
            sl = slice(h * MHD, (h + 1) * MHD)
            p = _softmax(_nt(qm[:, sl], km_ref[:, sl]) * MSCALE)
            oms.append(_nn(p, vm_ref[:, sl]))
        om = jnp.concatenate(oms, axis=1).astype(om_ref.dtype)
        om_ref[...] = om
        x2_ref[...] = x1v + _nn(om, wo_ref[...])

    return _hbm_call(
        body, name="mem_attn_fwd", grid=(S // tm,),
        in_specs=[_rows(tm, D), _full((1, D)), _full((D, D)), _full((M, D)), _full((M, D)), _full((D, D))],
        out_specs=[_rows(tm, D)] * 3,
        out_shape=[_sds((S, D), F32), _sds((S, D), _MXU), _sds((S, D), _MXU)],
        compiler_params=_params(48),
    )(x1, g_q, wq, km, vm, wo)


def _ffn_loss_fwd(x2, tgt, g_ffn, g_final, wgT, wuT, wd, tm=512):
    S = x2.shape[0]

    def body(x2_ref, t_ref, gf_ref, gl_ref, wg_ref, wu_ref, wd_ref,
             gt_ref, up_ref, hf_ref, dx3_ref, dx3b_ref, loss_ref, dgl_ref):
        i = pl.program_id(0)
        _acc_init(i, loss_ref, dgl_ref)
        x2v = x2_ref[...]
        hf, _ = _rms(x2v, gf_ref[...])
        hb = hf.astype(hf_ref.dtype)
        hf_ref[...] = hb
        x3 = x2v
        for c in range(F // FCH):
            sl = slice(c * FCH, (c + 1) * FCH)
            gt = _nt(hb, wg_ref[sl, :])
            up = _nt(hb, wu_ref[sl, :])
            gt_ref[:, sl] = gt.astype(gt_ref.dtype)
            up_ref[:, sl] = up.astype(up_ref.dtype)
            x3 = x3 + _nn(gt * _sigmoid(gt) * up, wd_ref[sl, :])
        gl = gl_ref[...]
        y, r = _rms(x3, gl)
        diff = y - t_ref[...]
        _acc_row(loss_ref, jnp.sum(jnp.sum(diff * diff, axis=-1, keepdims=True), axis=0, keepdims=True) * (0.5 / D))
        dx3, dgl = _rms_bwd(diff * (1.0 / D), x3, r, gl)
        dx3_ref[...] = dx3
        dx3b_ref[...] = dx3.astype(dx3b_ref.dtype)
        _acc_row(dgl_ref, dgl)

    return _hbm_call(
        body, name="ffn_loss_fwd", grid=(S // tm,),
        in_specs=[_rows(tm, D), _rows(tm, D), _full((1, D)), _full((1, D)),
                  _once((F, D)), _once((F, D)), _once((F, D))],
        out_specs=[_rows(tm, F), _rows(tm, F), _rows(tm, D), _rows(tm, D), _rows(tm, D),
                   _full((8, LANES)), _full((8, D))],
        out_shape=[_sds((S, F), _MXU), _sds((S, F), _MXU), _sds((S, D), _MXU), _sds((S, D), F32),
                   _sds((S, D), _MXU), _sds((8, LANES), F32), _sds((8, D), F32)],
        compiler_params=_params(56),
    )(x2, tgt, g_ffn, g_final, wgT, wuT, wd)


def _ffn_bwd(dx3, x2, gt, up, g_ffn, wd, wgT, wuT, tm=256):
    S = x2.shape[0]

    def body(dx3_ref, x2_ref, gt_ref, up_ref, g_ref, wd_ref, wg_ref, wu_ref,
             dx2_ref, dx2b_ref, dgt_ref, dup_ref, act_ref, dg_ref):
        i = pl.program_id(0)
        _acc_init(i, dg_ref)
        dx3v = dx3_ref[...]
        db = dx3v.astype(_MXU)
        dhf = jnp.zeros((tm, D), F32)
        for c in range(F // FCH_BWD):
            sl = slice(c * FCH_BWD, (c + 1) * FCH_BWD)
            gtv = gt_ref[:, sl].astype(F32)
            upv = up_ref[:, sl].astype(F32)
            dact = _nt(db, wd_ref[sl, :])
            sig = _sigmoid(gtv)
            silu = gtv * sig
            dup = (dact * silu).astype(dup_ref.dtype)
            dgt = (dact * upv * (sig * (1.0 + gtv * (1.0 - sig)))).astype(dgt_ref.dtype)
            act_ref[:, sl] = (silu * upv).astype(act_ref.dtype)
            dgt_ref[:, sl] = dgt
            dup_ref[:, sl] = dup
            dhf = dhf + _nn(dgt, wg_ref[sl, :]) + _nn(dup, wu_ref[sl, :])
        x2v = x2_ref[...]
        g = g_ref[...]
        _, r = _rms(x2v, g)
        dxn, dg = _rms_bwd(dhf, x2v, r, g)
        dx2 = dx3v + dxn
        dx2_ref[...] = dx2
        dx2b_ref[...] = dx2.astype(dx2b_ref.dtype)
        _acc_row(dg_ref, dg)

    return _hbm_call(
        body, name="ffn_bwd", grid=(S // tm,),
        in_specs=[_rows(tm, D), _rows(tm, D), _rows(tm, F), _rows(tm, F), _full((1, D)),
                  _once((F, D)), _once((F, D)), _once((F, D))],
        out_specs=[_rows(tm, D), _rows(tm, D), _rows(tm, F), _rows(tm, F), _rows(tm, F), _full((8, D))],
        out_shape=[_sds((S, D), F32), _sds((S, D), _MXU), _sds((S, F), _MXU), _sds((S, F), _MXU),
                   _sds((S, F), _MXU), _sds((8, D), F32)],
        compiler_params=_params(56),
    )(dx3, x2, gt, up, g_ffn, wd, wgT, wuT)


_AFTER = pl.BlockSpec(memory_space=pl.ANY)


def _wgrad(pairs, name, after, rc=256):
    S, R = pairs[0][0].shape
    N = pairs[0][1].shape[1]
    rights = []
    for _, b in pairs:
        if not any(b is r for r in rights):
            rights.append(b)
    which = [next(i for i, r in enumerate(rights) if r is b) for _, b in pairs]
    n, nb = len(pairs), len(rights)

    def body(*refs):
        for k in range(n):
            refs[n + nb + 1 + k][...] = _tn(refs[k][...], refs[n + which[k]][...]).astype(jnp.bfloat16)

    res = _hbm_call(
        body, name=name, grid=(R // rc,),
        in_specs=[pl.BlockSpec((S, rc), lambda j: (0, j))] * n + [_once((S, N))] * nb + [_AFTER],
        out_specs=[_rows(rc, N)] * n,
        out_shape=[_sds((R, N), jnp.bfloat16)] * n,
        compiler_params=_params(56),
    )(*[a for a, _ in pairs], *rights, after)
    return list(res)


def _mem_attn_bwd(dx2, x1, qm, km, vm, g_q, wq, wo, wout, after, tm=512):
    S = x1.shape[0]
    M = km.shape[0]

    def body(dx2_ref, x1_ref, qm_ref, km_ref, vm_ref, g_ref, wq_ref, wo_ref, wout_ref, after_ref,
             dx1_ref, dx1b_ref, hq_ref, dqm_ref, dyc_ref, dya_ref, dkm_ref, dvm_ref, dg_ref, dbo_ref):
        i = pl.program_id(0)
        _acc_init(i, dkm_ref, dvm_ref, dg_ref, dbo_ref)
        dx2v = dx2_ref[...]
        domt = _nt(wo_ref[...], dx2v).astype(_MXU)
        dqs = []
        for h in range(MH):
            sl = slice(h * MHD, (h + 1) * MHD)
            qh = qm_ref[:, sl]
            kh = km_ref[:, sl]
            e, inv = _softmax_t(_nt(kh, qh) * MSCALE)
            p = e * inv
            dp = _nn(vm_ref[:, sl], domt[sl, :])
            ds = (p * (dp - jnp.sum(dp * p, axis=0, keepdims=True))).astype(_MXU)
            dqs.append(_tn(ds, kh) * MSCALE)
            dkm_ref[:, sl] += _nn(ds, qh) * MSCALE
            dvm_ref[:, sl] += _nt(p, domt[sl, :])
        dqm = jnp.concatenate(dqs, axis=1).astype(dqm_ref.dtype)
        dqm_ref[...] = dqm
        dhq = _nt(dqm, wq_ref[...])
        x1v = x1_ref[...]
        g = g_ref[...]
        hq, r = _rms(x1v, g)
        hq_ref[...] = hq.astype(hq_ref.dtype)
        dxn, dg = _rms_bwd(dhq, x1v, r, g)
        dx1 = dx2v + dxn
        dx1_ref[...] = dx1
        dx1b = dx1.astype(dx1b_ref.dtype)
        dx1b_ref[...] = dx1b
        dym = _nt(dx1b, wout_ref[...])
        dyc_ref[...] = dym[:, :CC]
        dya_ref[...] = dym[:, CC:].astype(dya_ref.dtype)
        _acc_row(dg_ref, dg)
        _acc_row(dbo_ref, jnp.sum(dx1, axis=0, keepdims=True))

    return _hbm_call(
        body, name="mem_attn_bwd", grid=(S // tm,),
        in_specs=[_rows(tm, D), _rows(tm, D), _rows(tm, D), _full((M, D)), _full((M, D)), _full((1, D)),
                  _once((D, D)), _once((D, D)), _once((D, D)), _AFTER],
        out_specs=[_rows(tm, D), _rows(tm, D), _rows(tm, D), _rows(tm, D), _rows(tm, CC), _rows(tm, ATT),
                   _full((M, D)), _full((M, D)), _full((8, D)), _full((8, D))],
        out_shape=[_sds((S, D), F32), _sds((S, D), _MXU), _sds((S, D), _MXU), _sds((S, D), _MXU),
                   _sds((S, CC), F32), _sds((S, ATT), _MXU),
                   _sds((M, D), F32), _sds((M, D), F32), _sds((8, D), F32), _sds((8, D), F32)],
        compiler_params=_params(56),
    )(dx2, x1, qm, km, vm, g_q, wq, wo, wout, after)


def _mem_kv_bwd(dkm, dvm, mem, g_kv, memn, wkvT):
    M = mem.shape[0]

    def body(dkm_ref, dvm_ref, mem_ref, g_ref, mn_ref, w_ref, gw_ref, dg_ref):
        dkv = jnp.concatenate([dkm_ref[...], dvm_ref[...]], axis=1).astype(_MXU)
        gw_ref[...] = _tn(dkv, mn_ref[...]).astype(gw_ref.dtype)
        dmn = _nn(dkv, w_ref[...])
        memv = mem_ref[...]
        g = g_ref[...]
        _, r = _rms(memv, g)
        _, dg = _rms_bwd(dmn, memv, r, g)
        dg_ref[...] = jnp.broadcast_to(dg, dg_ref.shape)

    return _hbm_call(
        body, name="mem_kv_bwd", grid=(1,),
        in_specs=[_full((M, D)), _full((M, D)), _full((M, D)), _full((1, D)), _full((M, D)), _full((2 * D, D))],
        out_specs=[_full((2 * D, D)), _full((8, D))],
        out_shape=[_sds((2 * D, D), jnp.bfloat16), _sds((8, D), F32)],
        compiler_params=_params(48),
    )(dkm, dvm, mem, g_kv, memn, wkvT)


def _attn_bwd(qr, kr, vv, sink, dya, after):
    S = qr.shape[0]
    tq = ATTN_BLOCKS * BLK

    def body(q_ref, k_ref, v_ref, sk_ref, do_ref, after_ref, dq_ref, dk_ref, dv_ref, dsk_ref, sink_acc):
        _acc_init(pl.program_id(0), dk_ref, dv_ref, sink_acc)
        for b in range(ATTN_BLOCKS):
            i = pl.program_id(0) * ATTN_BLOCKS + b
            rows = slice(b * BLK, (b + 1) * BLK)
            start = _attn_window(i, S)
            kb = k_ref[pl.ds(start, 3 * BLK), :]
            vb = v_ref[pl.ds(start, 3 * BLK), :]
            q = q_ref[rows, :]
            do = do_ref[rows, :]
            mask = _attn_mask(i, start)
            dqs, dks, dvs = [], [], []
            for hk in range(NKV):
                hs = slice(hk * HD, (hk + 1) * HD)
                qs = _stack_heads(q, hk)
                dos = _stack_heads(do, hk)
                e, inv, es = _attn_probs(qs, kb[:, hs], sk_ref, hk, mask)
                p = e * inv
                dp = _nt(vb[:, hs], dos)
                delta = jnp.sum(dp * p, axis=0, keepdims=True)
                ds = (p * (dp - delta)).astype(_MXU)
                dqs.append(_unstack_heads_t(_tn(kb[:, hs], ds) * SCALE))
                dks.append(_nn(ds, qs) * SCALE)
                dvs.append(_nn(p, dos))
                sink_acc[hk:hk + 1, :] -= es * inv * delta
            dq_ref[rows, :] = jnp.concatenate(dqs, axis=1)
            dk_ref[pl.ds(start, 3 * BLK), :] += jnp.concatenate(dks, axis=1)
            dv_ref[pl.ds(start, 3 * BLK), :] += jnp.concatenate(dvs, axis=1)

        @pl.when(pl.program_id(0) == pl.num_programs(0) - 1)
        def _():
            for h in range(NQ):
                tot = jnp.sum(sink_acc[h // GRP:h // GRP + 1, (h % GRP) * BLK:(h % GRP + 1) * BLK], axis=1,
                              keepdims=True)
                dsk_ref[h:h + 1, :] = jnp.broadcast_to(tot, (1, LANES))

    return _hbm_call(
        body, name="attn_bwd", grid=(S // tq,),
        in_specs=[_rows(tq, ATT), _full((S, KVC)), _full((S, KVC)), _full((1, NQ)), _rows(tq, ATT), _AFTER],
        out_specs=[_rows(tq, ATT), _full((S, KVC)), _full((S, KVC)), _full((8, LANES))],
        out_shape=[_sds((S, ATT), F32), _sds((S, KVC), F32), _sds((S, KVC), F32), _sds((8, LANES), F32)],
        scratch_shapes=[pltpu.VMEM((8, GRP * BLK), F32)],
        compiler_params=_params(40),
    )(qr, kr, vv, sink, dya, after)


def _conv_norm_bwd(dyc, y, ln_g, ln_b, after, tm=512):
    S = y.shape[0]

    def body(dyc_ref, y_ref, g_ref, b_ref, after_ref, dy_ref, dg_ref, db_ref, dbdw_ref):
        i = pl.program_id(0)
        _acc_init(i, dg_ref, db_ref, dbdw_ref)
        g = g_ref[...]
        yh, rstd, z = _ln_silu(y_ref[...], g, b_ref[...])
        sig = _sigmoid(z)
        dz = dyc_ref[...] * (sig * (1.0 + z * (1.0 - sig)))
        dyh = dz * g
        dy = rstd * (dyh - jnp.mean(dyh, axis=-1, keepdims=True)
                     - yh * jnp.mean(dyh * yh, axis=-1, keepdims=True))
        dy_ref[...] = dy
        _acc_row(dg_ref, jnp.sum(dz * yh, axis=0, keepdims=True))
        _acc_row(db_ref, jnp.sum(dz, axis=0, keepdims=True))
        _acc_row(dbdw_ref, jnp.sum(dy, axis=0, keepdims=True))

    return _hbm_call(
        body, name="conv_norm_bwd", grid=(S // tm,),
        in_specs=[_rows(tm, CC), _rows(tm, CC), _full((1, CC)), _full((1, CC)), _AFTER],
        out_specs=[_rows(tm, CC), _full((8, CC)), _full((8, CC)), _full((8, CC))],
        out_shape=[_sds((S, CC), F32), _sds((8, CC), F32), _sds((8, CC), F32), _sds((8, CC), F32)],
        compiler_params=_params(40),
    )(dyc, y, ln_g, ln_b, after)


def _conv_bwd(dy, vg, uglu, w_dw):
    S = dy.shape[0]
    tm = min(CONV_TILE, S)

    def body(dy_hbm, vg_hbm, uglu_ref, w_ref, du_ref, gw_ref, dbin_ref, dyp_ref, vgp_ref, gacc_ref, sh_ref, dvg_ref,
             sems):
        i = pl.program_id(0)
        _acc_init(i, gacc_ref, dbin_ref)
        _fill_padded(i, dy_hbm, dyp_ref, sems.at[0], S)
        _fill_padded(i, vg_hbm, vgp_ref, sems.at[1], S)
        t0 = pl.multiple_of(i * tm, tm)
        _shifted_copies(dyp_ref, sh_ref.at[0], t0, tm)
        _shifted_copies(vgp_ref, sh_ref.at[1], t0, tm)
        _dwconv_tile(sh_ref.at[0], tm, w_ref, True, dvg_ref)
        dvg = dvg_ref[...]
        sub = CONV_ROWS
        for cl in range(CC // LANES):
            lanes = slice(cl * LANES, (cl + 1) * LANES)
            for k, b, off in _conv_taps():
                part = jnp.zeros((8, LANES), F32)
                for r in range(0, tm, sub):
                    prod = (sh_ref[0, cl, 0, CONV_HALO + r:CONV_HALO + r + sub, :]
                            * sh_ref[1, cl, b, off + r:off + r + sub, :])
                    part = part + jnp.sum(prod.reshape(sub // 8, 8, LANES), axis=0)
                gacc_ref[k, :, lanes] += part
        a_v = uglu_ref[:, :CC]
        sg = _sigmoid(uglu_ref[:, CC:])
        da = dvg * sg
        dgate = dvg * a_v * sg * (1.0 - sg)
        du = jnp.concatenate([da, dgate], axis=1)
        du_ref[...] = du.astype(du_ref.dtype)
        _acc_row(dbin_ref, jnp.sum(du, axis=0, keepdims=True))

        @pl.when(i == pl.num_programs(0) - 1)
        def _():
            gw_ref[...] = jnp.zeros(gw_ref.shape, F32)
            for k in range(TAPS):
                gw_ref[k:k + 1, :] = jnp.sum(gacc_ref[k], axis=0, keepdims=True)

    return _hbm_call(
        body, name="conv_bwd", grid=(S // tm,),
        in_specs=[pl.BlockSpec(memory_space=pl.ANY), pl.BlockSpec(memory_space=pl.ANY), _rows(tm, 2 * CC),
                  _full((TAPS, CC))],
        out_specs=[_rows(tm, 2 * CC), _full((TAPS + 1, CC)), _full((8, 2 * CC))],
        out_shape=[_sds((S, 2 * CC), _MXU), _sds((TAPS + 1, CC), F32), _sds((8, 2 * CC), F32)],
        scratch_shapes=[pltpu.VMEM((S + 2 * CONV_HALO, CC), F32), pltpu.VMEM((S + 2 * CONV_HALO, CC), F32),
                        pltpu.VMEM((TAPS, 8, CC), F32), _shift_scratch(tm, 2), pltpu.VMEM((tm, CC), F32),
                        pltpu.SemaphoreType.DMA((2,))],
        compiler_params=_params(56),
    )(dy, vg, uglu, w_dw)


def _in_proj_bwd(duglu, dqr, dkr, dv, cosq, sinq, x, dx1, g_mix, winT, tm=512):
    S = x.shape[0]
    NA = ATT + 2 * KVC

    def body(dug_ref, dq_ref, dk_ref, dv_ref, c_ref, s_ref, x_ref, dx1_ref, g_ref, w_ref,
             dx_ref, du_ref, dg_ref, dbin_ref):
        i = pl.program_id(0)
        _acc_init(i, dg_ref, dbin_ref)
        c = c_ref[...]
        s = s_ref[...]
        datt = jnp.concatenate([_rope_bwd(dq_ref[...], c, s), _rope_bwd(dk_ref[...], c, s), dv_ref[...]], axis=1)
        _acc_row(dbin_ref, jnp.sum(datt, axis=0, keepdims=True))
        du = jnp.concatenate([dug_ref[...], datt.astype(du_ref.dtype)], axis=1)
        du_ref[...] = du
        dh = _nn(du, w_ref[...])
        xv = x_ref[...]
        g = g_ref[...]
        _, r = _rms(xv, g)
        dxn, dg = _rms_bwd(dh, xv, r, g)
        dx_ref[...] = dx1_ref[...] + dxn
        _acc_row(dg_ref, dg)

    return _hbm_call(
        body, name="in_proj_bwd", grid=(S // tm,),
        in_specs=[_rows(tm, 2 * CC), _rows(tm, ATT), _rows(tm, KVC), _rows(tm, KVC), _rows(tm, LANES),
                  _rows(tm, LANES), _rows(tm, D), _rows(tm, D), _full((1, D)), _once((INC, D))],
        out_specs=[_rows(tm, D), _rows(tm, INC), _full((8, D)), _full((8, NA))],
        out_shape=[_sds((S, D), F32), _sds((S, INC), _MXU), _sds((8, D), F32), _sds((8, NA), F32)],
        compiler_params=_params(48),
    )(duglu, dqr, dkr, dv, cosq, sinq, x, dx1, g_mix, winT)


def _rope_tables(S):
    pos = jnp.arange(S, dtype=F32)
    inv_freq = ROPE_THETA ** (-jnp.arange(0, HD, 2, dtype=F32) / HD)
    ang = pos[:, None] * inv_freq[None, :]
    cos, sin = jnp.cos(ang), jnp.sin(ang)
    cosq = jnp.tile(jnp.concatenate([cos, cos], axis=1), (1, LANES // HD))
    sinq = jnp.tile(jnp.concatenate([-sin, sin], axis=1), (1, LANES // HD))
    return cosq, sinq


GATHER_GROUPS = (("w_inT", "w_dw"), ("w_out", "w_mem_kvT", "w_mem_q", "w_mem_o"), ("w_gateT", "w_upT", "w_down"))


def _local_step(x, mem, tgt, sp, gather, reduce):
    S = x.shape[0]
    cosq, sinq = _rope_tables(S)

    gather.forward(0, cosq)
    w0 = gather.finish(0, sinq)
    winT, w_dw_full = w0["w_inT"], w0["w_dw"]
    h, uglu, vg, qr, kr, vv = _in_proj_fwd(x, sp["g_mix"], winT, sp["b_in"], cosq, sinq)
    ya = _attn_fwd(qr, kr, vv, sp["attn_sink"])
    gather.forward(1, ya)
    y, yc = _conv_fwd(vg, w_dw_full, sp["b_dw"], sp["g_conv_ln"], sp["b_conv_ln"])
    w1 = gather.finish(1, yc)
    wout, wkvT, wq, wo = w1["w_out"], w1["w_mem_kvT"], w1["w_mem_q"], w1["w_mem_o"]
    km, vm, memn = _mem_kv_fwd(mem, sp["g_mem_kv"], wkvT)
    x1, ymix = _out_proj_fwd(x, yc, ya, wout, sp["b_out"])
    gather.forward(2, x1)
    x2, qm, om = _mem_attn_fwd(x1, sp["g_mem_q"], wq, km, vm, wo)
    w2 = gather.finish(2, om)
    wgT, wuT, wd = w2["w_gateT"], w2["w_upT"], w2["w_down"]
    gt, up, hf, dx3, dx3b, loss_acc, dg_final = _ffn_loss_fwd(x2, tgt, sp["g_ffn"], sp["g_final"], wgT, wuT, wd)

    dx2, dx2b, dgt, dup, act, dg_ffn = _ffn_bwd(dx3, x2, gt, up, sp["g_ffn"], wd, wgT, wuT)
    token = reduce.start(0, _wgrad([(dgt, hf), (dup, hf), (act, dx3b)], "wgrad_ffn", dx2))

    dx1, dx1b, hq, dqm, dyc, dya, dkm, dvm, dg_mem_q, db_out = _mem_attn_bwd(
        dx2, x1, qm, km, vm, sp["g_mem_q"], wq, wo, wout, token)
    token = reduce.middle(0, dx1)
    g_wout, g_wq, g_wo = _wgrad([(ymix, dx1b), (hq, dqm), (om, dx2b)], "wgrad_mix_mem", token)
    g_wkvT, dg_mem_kv = _mem_kv_bwd(dkm, dvm, mem, sp["g_mem_kv"], memn, wkvT)
    token = reduce.start(1, [g_wout, g_wq, g_wkvT, g_wo])

    dqr, dkr, dvv, dsink = _attn_bwd(qr, kr, vv, sp["attn_sink"], dya, token)
    token = reduce.middle(1, dqr)
    dy, dg_cln, db_cln, db_dw = _conv_norm_bwd(dyc, y, sp["g_conv_ln"], sp["b_conv_ln"], token)
    duglu, g_wdw, dbin_glu = _conv_bwd(dy, vg, uglu, w_dw_full)
    grad_x, du, dg_mix, dbin_att = _in_proj_bwd(duglu, dqr, dkr, dvv, cosq, sinq, x, dx1, sp["g_mix"], winT)
    token = reduce.start(2, _wgrad([(du, h)], "wgrad_in", grad_x))

    small = {
        "loss": loss_acc[0:1, :],
        "g_mix": dg_mix[0:1], "b_in": jnp.concatenate([dbin_glu[0:1], dbin_att[0:1]], axis=1),
        "b_dw": db_dw[0:1], "g_conv_ln": dg_cln[0:1], "b_conv_ln": db_cln[0:1],
        "attn_sink": dsink[:, 0].reshape(1, NQ), "b_out": db_out[0:1], "g_mem_q": dg_mem_q[0:1],
        "g_mem_kv": dg_mem_kv[0:1], "g_ffn": dg_ffn[0:1], "g_final": dg_final[0:1],
    }
    token = reduce.middle(2, token, reduce.small_zones(small, g_wdw))
    reduce.finish(0, token)
    reduce.finish(1, token)
    return grad_x


def _place():
    return lax.axis_index("x"), lax.axis_index("y"), lax.axis_index("c")


def _hbm_specs(n):
    return [pl.BlockSpec(memory_space=pl.ANY)] * n


_HBM = pl.BlockSpec(memory_space=pltpu.HBM)
_SEM = pl.BlockSpec(memory_space=pltpu.SEMAPHORE)
_ANY = pl.BlockSpec(memory_space=pl.ANY)
_EFFECT = pltpu.SideEffectType.DATAFLOW_SIDE_EFFECTING
_VMEM = pl.BlockSpec(memory_space=pltpu.VMEM)
_TOKEN = jax.ShapeDtypeStruct((8, LANES), F32)


def _dev_index(p):
    return 4 * p[0] + 2 * p[1] + p[2]


def _gather_start(groups):
    sizes = [len(g) for g in groups]
    flat = [z for g in groups for z in g]
    n = len(flat)

    def body(*refs):
        ins = refs[:n]
        sems = refs[2 * n:]
        x, y, c = _place()
        me = _dev_index((x, y, c))
        peers = [(x, y, 1 - c), (1 - x, y, c), (x, 1 - y, c), (1 - x, 1 - y, c)]
        base = 0
        for gi, size in enumerate(sizes):
            send_sems, recv_sems = sems[2 * gi], sems[2 * gi + 1]
            for k in range(size):
                zone = ins[base + k]
                for j, peer in enumerate(peers):
                    pltpu.make_async_remote_copy(
                        src_ref=zone.at[me], dst_ref=zone.at[me], send_sem=send_sems.at[4 * k + j],
                        recv_sem=recv_sems.at[4 * k + j], device_id=peer, device_id_type=MESH).start()
            base += size

    out_shape = [pltpu.HBM(z.shape, z.dtype) for z in flat]
    out_specs = [_HBM] * n
    for size in sizes:
        out_shape += [pltpu.SemaphoreType.DMA((4 * size,))] * 2
        out_specs += [_SEM, _SEM]
    res = pl.pallas_call(
        body, name="gather_start", in_specs=[_HBM] * n, out_specs=out_specs, out_shape=out_shape,
        input_output_aliases={k: k for k in range(n)},
        compiler_params=pltpu.CompilerParams(has_side_effects=_EFFECT),
    )(*[pltpu.with_memory_space_constraint(z, pltpu.HBM) for z in flat])
    out, base = [], 0
    for gi, size in enumerate(sizes):
        out.append((list(res[base:base + size]), res[n + 2 * gi], res[n + 2 * gi + 1]))
        base += size
    return out


def _gather_forward(zones, send_sems, recv_sems, after, collective_id, name):
    n = len(zones)

    def body(*refs):
        ins = refs[:n]
        send_ref, recv_ref = refs[n], refs[n + 1]
        fsend, frecv = refs[2 * n + 3], refs[2 * n + 4]
        x, y, c = _place()
        sibling = (x, y, 1 - c)
        barrier = pltpu.get_barrier_semaphore()
        pl.semaphore_signal(barrier, inc=1, device_id=sibling, device_id_type=MESH)
        pl.semaphore_wait(barrier, 1)
        chips = [(1 - x, y), (x, 1 - y), (1 - x, 1 - y)]
        for j, chip in enumerate(chips):
            for k in range(n):
                slot = ins[k].at[_dev_index((*chip, c))]
                pltpu.make_async_remote_copy(
                    src_ref=slot, dst_ref=slot, send_sem=send_ref.at[4 * k + 1 + j],
                    recv_sem=recv_ref.at[4 * k + 1 + j], device_id=sibling, device_id_type=MESH).wait_recv()
                pltpu.make_async_remote_copy(
                    src_ref=slot, dst_ref=slot, send_sem=fsend.at[3 * k + j], recv_sem=frecv.at[3 * k + j],
                    device_id=sibling, device_id_type=MESH).start()
        for k in range(n):
            own = ins[k].at[_dev_index((x, y, c))]
            theirs = ins[k].at[_dev_index(sibling)]
            pltpu.make_async_remote_copy(
                src_ref=theirs, dst_ref=theirs, send_sem=send_ref.at[4 * k], recv_sem=recv_ref.at[4 * k],
                device_id=sibling, device_id_type=MESH).wait_recv()
            for j in range(4):
                pltpu.make_async_remote_copy(
                    src_ref=own, dst_ref=own, send_sem=send_ref.at[4 * k + j], recv_sem=recv_ref.at[4 * k + j],
                    device_id=sibling, device_id_type=MESH).wait_send()

    res = pl.pallas_call(
        body, name=name, in_specs=[_HBM] * n + [_SEM, _SEM, _ANY],
        out_specs=[_HBM] * n + [_SEM, _SEM],
        out_shape=[pltpu.HBM(z.shape, z.dtype) for z in zones] + [pltpu.SemaphoreType.DMA((3 * n,))] * 2,
        input_output_aliases={k: k for k in range(n)},
        compiler_params=pltpu.CompilerParams(has_side_effects=_EFFECT, collective_id=collective_id),
    )(*zones, send_sems, recv_sems, after)
    return list(res[:n]), res[n], res[n + 1]


def _gather_finish(zones, fsend, frecv, after, name):
    n = len(zones)

    def body(*refs):
        ins = refs[:n]
        fsend_ref, frecv_ref = refs[n], refs[n + 1]
        x, y, c = _place()
        sibling = (x, y, 1 - c)
        chips = [(1 - x, y), (x, 1 - y), (1 - x, 1 - y)]
        for j, chip in enumerate(chips):
            for k in range(n):
                mine = ins[k].at[_dev_index((*chip, c))]
                theirs = ins[k].at[_dev_index((*chip, 1 - c))]
                pltpu.make_async_remote_copy(
                    src_ref=theirs, dst_ref=theirs, send_sem=fsend_ref.at[3 * k + j],
                    recv_sem=frecv_ref.at[3 * k + j], device_id=sibling, device_id_type=MESH).wait_recv()
                pltpu.make_async_remote_copy(
                    src_ref=mine, dst_ref=mine, send_sem=fsend_ref.at[3 * k + j],
                    recv_sem=frecv_ref.at[3 * k + j], device_id=sibling, device_id_type=MESH).wait_send()

    res = pl.pallas_call(
        body, name=name, in_specs=[_HBM] * n + [_SEM, _SEM, _ANY], out_specs=[_HBM] * n,
        out_shape=[pltpu.HBM(z.shape, z.dtype) for z in zones],
        input_output_aliases={k: k for k in range(n)},
        compiler_params=pltpu.CompilerParams(has_side_effects=_EFFECT),
    )(*zones, fsend, frecv, after)
    return list(res)


class _Gather:
    def __init__(self, started):
        self._phase = list(started)

    def forward(self, gi, after):
        zones, send_sems, recv_sems = self._phase[gi]
        self._phase[gi] = _gather_forward(zones, send_sems, recv_sems, after, gi + 1, "gather_forward_%d" % gi)

    def finish(self, gi, after):
        zones, fsend, frecv = self._phase[gi]
        out = {}
        for name, z in zip(GATHER_GROUPS[gi], _gather_finish(zones, fsend, frecv, after, "gather_finish_%d" % gi)):
            if name == "w_dw":
                out[name] = jnp.transpose(z, (1, 0, 2)).reshape(TAPS + 1, CC)[:TAPS]
            else:
                out[name] = z.reshape(NDEV * z.shape[1], D)
        return out


def _handshake(peers):
    barrier = pltpu.get_barrier_semaphore()
    for peer in peers:
        pl.semaphore_signal(barrier, inc=1, device_id=peer, device_id_type=MESH)
    pl.semaphore_wait(barrier, len(peers))


def _sibling_copies(srcs, lands, send_sems, recv_sems):
    x, y, c = _place()
    return [pltpu.make_async_remote_copy(
        src_ref=srcs[k].at[2 * s + (1 - c)], dst_ref=lands[k].at[s], send_sem=send_sems.at[4 * k + s],
        recv_sem=recv_sems.at[4 * k + s], device_id=(x, y, 1 - c), device_id_type=MESH)
        for k in range(len(srcs)) for s in range(4)]


def _sibling_start(gbs, collective_id, name):
    n = len(gbs)
    lands = [lax.empty((4,) + g.shape[1:], g.dtype) for g in gbs]

    def body(*refs):
        x, y, c = _place()
        _handshake([(x, y, 1 - c)])
        for cp in _sibling_copies(refs[:n], refs[n:2 * n], refs[4 * n], refs[4 * n + 1]):
            cp.start()
        refs[4 * n + 2][...] = jnp.zeros(_TOKEN.shape, _TOKEN.dtype)

    res = pl.pallas_call(
        body, name=name, in_specs=[_HBM] * (2 * n), out_specs=[_HBM] * (2 * n) + [_SEM, _SEM, _VMEM],
        out_shape=[pltpu.HBM(a.shape, a.dtype) for a in gbs + lands] + [pltpu.SemaphoreType.DMA((4 * n,))] * 2
        + [_TOKEN],
        input_output_aliases={k: k for k in range(2 * n)},
        compiler_params=pltpu.CompilerParams(has_side_effects=_EFFECT, collective_id=collective_id),
    )(*[pltpu.with_memory_space_constraint(a, pltpu.HBM) for a in gbs + lands])
    return list(res[:n]), list(res[n:2 * n]), res[2 * n], res[2 * n + 1], res[2 * n + 2]


def _sibling_wait(gbs, lands, send_sems, recv_sems, after, name):
    n = len(gbs)

    def body(*refs):
        for cp in _sibling_copies(refs[:n], refs[n:2 * n], refs[2 * n], refs[2 * n + 1]):
            cp.wait_recv()
            cp.wait_send()

    res = pl.pallas_call(
        body, name=name, in_specs=[_HBM] * (2 * n) + [_SEM, _SEM, _ANY], out_specs=[_HBM] * (2 * n),
        out_shape=[pltpu.HBM(a.shape, a.dtype) for a in gbs + lands],
        input_output_aliases={k: k for k in range(2 * n)},
        compiler_params=pltpu.CompilerParams(has_side_effects=_EFFECT),
    )(*gbs, *lands, send_sems, recv_sems, after)
    return list(res[:n]), list(res[n:])


def _pair_sum(gfs, ts, c_idx, name):
    n = len(gfs)
    halves = 2

    def body(c_ref, *refs):
        for k in range(n):
            refs[2 * n + k][...] = (refs[k][...].astype(F32) + refs[n + k][...].astype(F32)).astype(jnp.bfloat16)

    in_specs, out_specs, out_shape = [], [], []
    for g in gfs:
        rh = g.shape[1] // halves
        in_specs.append(pl.BlockSpec((None, rh, D), lambda s, h, c_ref: (2 * s + c_ref[0], h, 0)))
    for g in gfs:
        rh = g.shape[1] // halves
        in_specs.append(pl.BlockSpec((None, rh, D), lambda s, h, c_ref: (s, h, 0)))
        out_specs.append(pl.BlockSpec((None, rh, D), lambda s, h, c_ref: (s, h, 0)))
        out_shape.append(_sds((4, g.shape[1], D), jnp.bfloat16))
    return pl.pallas_call(
        body, name=name,
        grid_spec=pltpu.PrefetchScalarGridSpec(num_scalar_prefetch=1, grid=(4, halves), in_specs=in_specs,
                                               out_specs=out_specs),
        out_shape=out_shape,
        compiler_params=pltpu.CompilerParams(dimension_semantics=("arbitrary", "arbitrary"),
                                             vmem_limit_bytes=48 * 2 ** 20),
    )(c_idx, *gfs, *ts)


def _chip_copies(pbs, lands, smalls, send_sems, recv_sems):
    x, y, c = _place()
    own = 2 * x + y
    me = _dev_index((x, y, c))
    chips = [(1 - x, y), (x, 1 - y), (1 - x, 1 - y)]
    n = len(pbs)
    copies = [pltpu.make_async_remote_copy(
        src_ref=pbs[k].at[2 * chip[0] + chip[1]], dst_ref=lands[k].at[own], send_sem=send_sems.at[3 * k + j],
        recv_sem=recv_sems.at[3 * k + j], device_id=(*chip, c), device_id_type=MESH)
        for k in range(n) for j, chip in enumerate(chips)]
    peers = [(x, y, 1 - c)] + [(*chip, cc) for chip in chips for cc in (c, 1 - c)]
    copies += [pltpu.make_async_remote_copy(
        src_ref=smalls[t].at[me], dst_ref=smalls[t].at[me], send_sem=send_sems.at[3 * n + 7 * t + j],
        recv_sem=recv_sems.at[3 * n + 7 * t + j], device_id=peer, device_id_type=MESH)
        for t in range(len(smalls)) for j, peer in enumerate(peers)]
    return copies


def _chip_start(pbs, smalls, collective_id, name):
    n, m = len(pbs), len(smalls)
    lands = [lax.empty(p.shape, p.dtype) for p in pbs]
    arrays = list(pbs) + lands + list(smalls)
    na = len(arrays)
    nsem = 3 * n + 7 * m

    def body(*refs):
        x, y, c = _place()
        chips = [(1 - x, y), (x, 1 - y), (1 - x, 1 - y)]
        peers = [(*chip, c) for chip in chips]
        if m:
            peers += [(x, y, 1 - c)] + [(*chip, 1 - c) for chip in chips]
        _handshake(peers)
        for cp in _chip_copies(refs[:n], refs[n:2 * n], refs[2 * n:na], refs[2 * na], refs[2 * na + 1]):
            cp.start()
        refs[2 * na + 2][...] = jnp.zeros(_TOKEN.shape, _TOKEN.dtype)

    res = pl.pallas_call(
        body, name=name, in_specs=[_HBM] * na, out_specs=[_HBM] * na + [_SEM, _SEM, _VMEM],
        out_shape=[pltpu.HBM(a.shape, a.dtype) for a in arrays] + [pltpu.SemaphoreType.DMA((nsem,))] * 2 + [_TOKEN],
        input_output_aliases={k: k for k in range(na)},
        compiler_params=pltpu.CompilerParams(has_side_effects=_EFFECT, collective_id=collective_id),
    )(*[pltpu.with_memory_space_constraint(a, pltpu.HBM) for a in arrays])
    return (list(res[:n]), list(res[n:2 * n]), list(res[2 * n:na]), res[na], res[na + 1]), res[na + 2]


def _chip_wait(pbs, lands, smalls, send_sems, recv_sems, after, name):
    n = len(pbs)
    arrays = list(pbs) + list(lands) + list(smalls)
    na = len(arrays)

    def body(*refs):
        for cp in _chip_copies(refs[:n], refs[n:2 * n], refs[2 * n:na], refs[na], refs[na + 1]):
            cp.wait_recv()
            cp.wait_send()

    res = pl.pallas_call(
        body, name=name, in_specs=[_HBM] * na + [_SEM, _SEM, _ANY], out_specs=[_HBM] * na,
        out_shape=[pltpu.HBM(a.shape, a.dtype) for a in arrays],
        input_output_aliases={k: k for k in range(na)},
        compiler_params=pltpu.CompilerParams(has_side_effects=_EFFECT),
    )(*arrays, send_sems, recv_sems, after)
    return list(res[:n]), list(res[n:2 * n]), list(res[2 * n:])


def _chip_sum(pbs, qs, own_idx, states, name):
    n = len(pbs)
    halves = 2
    n_in = 4 * n + 3 * sum(s is not None for s in states)

    def body(o_ref, *refs):
        nxt_in, nxt_out = 4 * n, n_in
        for k in range(n):
            acc = refs[4 * k][...].astype(F32)
            for j in range(1, 4):
                acc = acc + refs[4 * k + j][...].astype(F32)
            refs[nxt_out][...] = acc
            nxt_out += 1
            if states[k] is not None:
                w_ref, m_ref, v_ref = refs[nxt_in:nxt_in + 3]
                nxt_in += 3
                d, nm, nv = _adamw_math(w_ref[...], acc, m_ref[...], v_ref[...])
                refs[nxt_out][...] = d
                refs[nxt_out + 1][...] = nm
                refs[nxt_out + 2][...] = nv
                nxt_out += 3

    def other(j):
        return lambda h, o: ((o[0] + j) % 4, h, 0)

    def rows(r):
        return pl.BlockSpec((r // halves, D), lambda h, o: (h, 0))

    in_specs, ins = [], []
    for p, q in zip(pbs, qs):
        rh = p.shape[1] // halves
        in_specs.append(pl.BlockSpec((None, rh, D), lambda h, o: (o[0], h, 0)))
        ins.append(p)
        for j in range(1, 4):
            in_specs.append(pl.BlockSpec((None, rh, D), other(j)))
            ins.append(q)
    out_specs, out_shape, counts = [], [], []
    for p, s in zip(pbs, states):
        r = p.shape[1]
        if s is not None:
            in_specs += [rows(r)] * 3
            ins += list(s)
        counts.append(1 if s is None else 4)
        out_specs += [rows(r)] * counts[-1]
        out_shape += [_sds((r, D), F32)] * counts[-1]
    res = pl.pallas_call(
        body, name=name,
        grid_spec=pltpu.PrefetchScalarGridSpec(num_scalar_prefetch=1, grid=(halves,), in_specs=in_specs,
                                               out_specs=out_specs),
        out_shape=out_shape,
        compiler_params=pltpu.CompilerParams(dimension_semantics=("arbitrary",), vmem_limit_bytes=56 * 2 ** 20),
    )(own_idx, *ins)
    out, base = [], 0
    for c in counts:
        out.append(tuple(res[base:base + c]))
        base += c
    return out


REDUCE_GROUPS = (("w_gateT", "w_upT", "w_down"), ("w_out", "w_mem_q", "w_mem_kvT", "w_mem_o"), ("w_inT",))


class _Reduce:
    def __init__(self, states):
        x, y, c = _place()
        self._c = c.astype(jnp.int32).reshape(1)
        self._own = (2 * x + y).astype(jnp.int32).reshape(1)
        self._me = _dev_index((x, y, c))
        self._state = {}
        self._adam = states
        self.results = {}
        self.smalls = None

    def start(self, gi, wgrads):
        gbs = [g.reshape(NDEV, g.shape[0] // NDEV, D) for g in wgrads]
        gbs, lands, send_sems, recv_sems, token = _sibling_start(gbs, 4 + gi, "reduce_sibling_start_%d" % gi)
        self._state[gi] = (gbs, lands, send_sems, recv_sems)
        return token

    def middle(self, gi, after, smalls=()):
        gbs, lands, send_sems, recv_sems = self._state[gi]
        gbs, ts = _sibling_wait(gbs, lands, send_sems, recv_sems, after, "reduce_sibling_wait_%d" % gi)
        pbs = _pair_sum(gbs, ts, self._c, "reduce_pair_sum_%d" % gi)
        self._state[gi], token = _chip_start(pbs, list(smalls), 7 + gi, "reduce_chip_start_%d" % gi)
        return token

    def finish(self, gi, after):
        pbs, lands, smalls, send_sems, recv_sems = self._state[gi]
        pbs, qs, smalls = _chip_wait(pbs, lands, smalls, send_sems, recv_sems, after, "reduce_chip_wait_%d" % gi)
        names = REDUCE_GROUPS[gi]
        states = [self._adam.get(n) for n in names]
        self.results.update(zip(names, _chip_sum(pbs, qs, self._own, states, "reduce_update_%d" % gi)))
        if smalls:
            self.smalls = smalls

    def small_zones(self, small, g_wdw):
        pack = jnp.concatenate(
            [jnp.pad(small[name], ((0, 0), (0, lanes - small[name].shape[1]))) for name, lanes, _ in PACK], axis=1)
        zones = []
        for a in (pack, g_wdw):
            zone = lax.empty((NDEV,) + a.shape, a.dtype)
            zones.append(lax.dynamic_update_slice(zone, a[None], (self._me,) + (0,) * a.ndim))
        return zones


def _adamw_math(w, g, m, v):
    m = ADAM_B1 * m + (1.0 - ADAM_B1) * g
    v = ADAM_B2 * v + (1.0 - ADAM_B2) * (g * g)
    m_hat = m / (1.0 - ADAM_B1 ** ADAM_STEP)
    v_hat = v / (1.0 - ADAM_B2 ** ADAM_STEP)
    delta = -ADAM_LR * (m_hat / (jnp.sqrt(v_hat) + ADAM_EPS) + ADAM_WD * w)
    return delta, m, v


def _adamw(w, g, m, v, name):
    R, C = w.shape
    rc = 256 if R % 256 == 0 and R > 256 else R

    def body(w_ref, g_ref, m_ref, v_ref, d_ref, nm_ref, nv_ref):
        d, nm, nv = _adamw_math(w_ref[...], g_ref[...], m_ref[...], v_ref[...])
        d_ref[...] = d
        nm_ref[...] = nm
        nv_ref[...] = nv

    return _hbm_call(
        body, name=name, grid=(R // rc,),
        in_specs=[_rows(rc, C)] * 4, out_specs=[_rows(rc, C)] * 3,
        out_shape=[_sds((R, C), F32)] * 3,
        compiler_params=_params(40),
    )(w, g, m, v)


PACK = (("loss", LANES, 1), ("g_mix", D, D), ("b_in", INC, INC), ("b_dw", CC, CC), ("g_conv_ln", CC, CC),
        ("b_conv_ln", CC, CC), ("attn_sink", LANES, NQ), ("b_out", D, D), ("g_mem_q", D, D), ("g_mem_kv", D, D),
        ("g_ffn", D, D), ("g_final", D, D))
PACK_LANES = sum(p[1] for p in PACK)


def _small_update(packs, gdws, ws, ms, vs):
    names = [p[0] for p in PACK[1:]]
    n = len(names)

    def body(*refs):
        packs_ref, gdws_ref = refs[0], refs[1]
        w_refs, m_refs, v_refs = refs[2:2 + n], refs[2 + n:2 + 2 * n], refs[2 + 2 * n:2 + 3 * n]
        outs = refs[2 + 3 * n:]
        tot = packs_ref[0]
        gdw = gdws_ref[0]
        for s in range(1, NDEV):
            tot = tot + packs_ref[s]
            gdw = gdw + gdws_ref[s]
        outs[0][...] = tot[:, 0:LANES]
        outs[1][...] = gdw
        off = LANES
        for k, (_, lanes, used) in enumerate(PACK[1:]):
            g = tot[:, off:off + used]
            off += lanes
            d, nm, nv = _adamw_math(w_refs[k][...], g, m_refs[k][...], v_refs[k][...])
            o = outs[2 + 4 * k:6 + 4 * k]
            o[0][...] = g
            o[1][...] = d
            o[2][...] = nm
            o[3][...] = nv

    def full(a):
        nd = len(a.shape)
        return pl.BlockSpec(a.shape, lambda i: (0,) * nd)

    ins = [packs, gdws, *ws, *ms, *vs]
    out_shape = [_sds((1, LANES), F32), _sds(gdws.shape[1:], F32)]
    for w in ws:
        out_shape += [_sds(w.shape, F32)] * 4
    return pl.pallas_call(
        body, name="small_update", grid=(1,),
        in_specs=[full(a) for a in ins], out_specs=[full(o) for o in out_shape], out_shape=out_shape,
        compiler_params=_params(40),
    )(*ins)


def kernel(x, mem, g_mix, w_in, b_in, w_dw, b_dw, g_conv_ln, b_conv_ln, attn_sink, w_out, b_out, g_mem_q, g_mem_kv, w_mem_q, w_mem_kv, w_mem_o, g_ffn, w_gate, w_up, w_down, g_final, loss_target, m_g_mix, m_w_in, m_b_in, m_w_dw, m_b_dw, m_g_conv_ln, m_b_conv_ln, m_attn_sink, m_w_out, m_b_out, m_g_mem_q, m_g_mem_kv, m_w_mem_q, m_w_mem_kv, m_w_mem_o, m_g_ffn, m_w_gate, m_w_up, m_w_down, m_g_final, v_g_mix, v_w_in, v_b_in, v_w_dw, v_b_dw, v_g_conv_ln, v_b_conv_ln, v_attn_sink, v_w_out, v_b_out, v_g_mem_q, v_g_mem_kv, v_w_mem_q, v_w_mem_kv, v_w_mem_o, v_g_ffn, v_w_gate, v_w_up, v_w_down, v_g_final):
    given = dict(locals())
    bf = jnp.bfloat16
    xi, yi, ci = _place()
    me = 4 * xi + 2 * yi + ci

    transposed = {"w_inT": w_in, "w_mem_kvT": w_mem_kv, "w_gateT": w_gate, "w_upT": w_up}
    plain = {"w_out": w_out, "w_mem_q": w_mem_q, "w_mem_o": w_mem_o, "w_down": w_down}
    blocks = {n: (transposed[n][0].T if n in transposed else plain[n][0]).astype(bf) for n in W_NAMES}
    blocks["w_dw"] = jnp.pad(w_dw[0], ((0, 1), (0, 0)))

    def landing_zone(b):
        zone = lax.empty((NDEV,) + b.shape, b.dtype)
        return lax.dynamic_update_slice(zone, b[None], (me,) + (0,) * b.ndim)

    gather = _Gather(_gather_start([[landing_zone(blocks[n]) for n in names] for names in GATHER_GROUPS]))

    sp = {n: given[n] for n in ("g_mix", "b_in", "b_dw", "g_conv_ln", "b_conv_ln", "attn_sink", "b_out", "g_mem_q",
                                "g_mem_kv", "g_ffn")}
    sp["g_final"] = g_final.reshape(1, D)
    frames = {"w_inT": ("w_in", True), "w_gateT": ("w_gate", True), "w_upT": ("w_up", True),
              "w_out": ("w_out", False), "w_mem_q": ("w_mem_q", False), "w_mem_o": ("w_mem_o", False),
              "w_down": ("w_down", False)}
    states = {}
    for frame, (n, transposed) in frames.items():
        wmv = (given[n][0], given["m_" + n][0], given["v_" + n][0])
        states[frame] = tuple(a.T for a in wmv) if transposed else wmv
    reduce = _Reduce(states)
    grad_x = _local_step(x[0], mem[0], loss_target[0], sp, gather, reduce)
    reduce.finish(2, reduce.results["w_out"][1])

    grads, deltas, new_m, new_v = {}, {}, {}, {}
    for frame, (n, transposed) in frames.items():
        outs = [a.T if transposed else a for a in reduce.results[frame]]
        grads[n], deltas[n], new_m[n], new_v[n] = [a[None] for a in outs]

    def update(n, g):
        d, nm, nv = _adamw(given[n][0], g, given["m_" + n][0], given["v_" + n][0], "adamw_" + n)
        grads[n], deltas[n], new_m[n], new_v[n] = g[None], d[None], nm[None], nv[None]

    update("w_mem_kv", reduce.results["w_mem_kvT"][0].T)

    packs, gdws = reduce.smalls
    names_small = [p[0] for p in PACK[1:]]

    def as_row(a):
        return a.reshape(1, -1)

    res = _small_update(packs, gdws, [as_row(given[n]) for n in names_small],
                        [as_row(given["m_" + n]) for n in names_small],
                        [as_row(given["v_" + n]) for n in names_small])
    loss = res[0][0, 0]
    for k, n in enumerate(names_small):
        shape = given[n].shape
        g, d, nm, nv = res[2 + 4 * k:6 + 4 * k]
        grads[n], deltas[n], new_m[n], new_v[n] = (g.reshape(shape), d.reshape(shape), nm.reshape(shape),
                                                   nv.reshape(shape))
    update("w_dw", lax.dynamic_slice(res[1], (0, me * (CC // NDEV)), (TAPS, CC // NDEV)))

    order = ("g_mix", "w_in", "b_in", "w_dw", "b_dw", "g_conv_ln", "b_conv_ln", "attn_sink", "w_out", "b_out",
             "g_mem_q", "g_mem_kv", "w_mem_q", "w_mem_kv", "w_mem_o", "g_ffn", "w_gate", "w_up", "w_down", "g_final")
    return (loss, grad_x[None], *[grads[n] for n in order], *[deltas[n] for n in order],
            *[new_m[n] for n in order], *[new_v[n] for n in order])
```

```python
import functools

import jax
import jax.numpy as jnp
from jax import lax
from jax.experimental import pallas as pl
from jax.experimental.pallas import tpu as pltpu

D = 1024
CC = 512
HD = 64
NQ = 8
NKV = 2
GRP = NQ // NKV
ATT = NQ * HD
KVC = NKV * HD
BLK = 128
MH = 4
MHD = D // MH
F = 2816
FCH = 256
FCH_BWD = F // 2
INC = 2 * CC + ATT + 2 * KVC
TAPS = 31
EPS = 1e-6
ROPE_THETA = 10000.0
SCALE = HD ** -0.5
MSCALE = MHD ** -0.5
LANES = 128
NDEV = 8

ADAM_LR = 0.001
ADAM_B1 = 0.9
ADAM_B2 = 0.999
ADAM_EPS = 1e-08
ADAM_WD = 0.01
ADAM_STEP = 10

F32 = jnp.float32
_MXU = jnp.bfloat16
MESH = pl.DeviceIdType.MESH

W_NAMES = ("w_inT", "w_out", "w_mem_q", "w_mem_kvT", "w_mem_o", "w_gateT", "w_upT", "w_down")
W_ROWS = (INC // NDEV, D // NDEV, D // NDEV, 2 * D // NDEV, D // NDEV, F // NDEV, F // NDEV, F // NDEV)


def _mm(a, b, ca, cb):
    return lax.dot_general(a.astype(_MXU), b.astype(_MXU), (((ca,), (cb,)), ((), ())),
                           preferred_element_type=F32)


def _nn(a, b):
    return _mm(a, b, 1, 0)


def _nt(a, b):
    return _mm(a, b, 1, 1)


def _tn(a, b):
    return _mm(a, b, 0, 0)


def _sigmoid(x):
    return 1.0 / (1.0 + jnp.exp(-x))


def _rms(x, g):
    r = lax.rsqrt(jnp.mean(x * x, axis=-1, keepdims=True) + EPS)
    return x * r * g, r


def _rms_bwd(dy, x, r, g):
    xh = x * r
    dg = jnp.sum(dy * xh, axis=0, keepdims=True)
    dxh = dy * g
    dx = r * (dxh - xh * jnp.mean(dxh * xh, axis=-1, keepdims=True))
    return dx, dg


def _rope(t, c, s):
    n = t.shape[1]
    reps = n // LANES
    if reps > 1:
        c = jnp.tile(c, (1, reps))
        s = jnp.tile(s, (1, reps))
    lane = lax.broadcasted_iota(jnp.int32, t.shape, 1)
    first = (lane & (HD - 1)) < (HD // 2)
    partner = jnp.where(first, pltpu.roll(t, n - HD // 2, 1), pltpu.roll(t, HD // 2, 1))
    return t * c + partner * s


def _rope_bwd(dt, c, s):
    n = dt.shape[1]
    reps = n // LANES
    if reps > 1:
        c = jnp.tile(c, (1, reps))
        s = jnp.tile(s, (1, reps))
    lane = lax.broadcasted_iota(jnp.int32, dt.shape, 1)
    first = (lane & (HD - 1)) < (HD // 2)
    ds = dt * s
    partner = jnp.where(first, pltpu.roll(ds, n - HD // 2, 1), pltpu.roll(ds, HD // 2, 1))
    return dt * c + partner


def _acc_init(i, *refs):
    @pl.when(i == 0)
    def _():
        for r in refs:
            r[...] = jnp.zeros(r.shape, r.dtype)


def _acc_row(ref, v):
    ref[...] += jnp.broadcast_to(v, ref.shape)


def _rows(tm, n):
    return pl.BlockSpec((tm, n), lambda i: (i, 0))


def _full(shape):
    nd = len(shape)
    return pl.BlockSpec(shape, lambda i: (0,) * nd)


def _once(shape):
    nd = len(shape)
    return pl.BlockSpec(shape, lambda i: (0,) * nd, pipeline_mode=pl.Buffered(1))


def _params(vmem_mb):
    return pltpu.CompilerParams(dimension_semantics=("arbitrary",), vmem_limit_bytes=vmem_mb * 2 ** 20)


def _sds(shape, dtype):
    return pltpu.HBM(shape, dtype)


def _hbm_call(*args, **kwargs):
    call = pl.pallas_call(*args, **kwargs)
    return lambda *ops: call(*[pltpu.with_memory_space_constraint(o, pltpu.HBM) for o in ops])


def _mem_kv_fwd(mem, g_kv, wkvT):
    M = mem.shape[0]

    def body(mem_ref, g_ref, w_ref, km_ref, vm_ref, mn_ref):
        y, _ = _rms(mem_ref[...], g_ref[...])
        kv = _nt(y, w_ref[...])
        km_ref[...] = kv[:, :D].astype(km_ref.dtype)
        vm_ref[...] = kv[:, D:].astype(vm_ref.dtype)
        mn_ref[...] = y.astype(mn_ref.dtype)

    return _hbm_call(
        body, name="mem_kv_fwd", grid=(1,),
        in_specs=[_full((M, D)), _full((1, D)), _full((2 * D, D))],
        out_specs=[_full((M, D))] * 3,
        out_shape=[_sds((M, D), _MXU)] * 3,
        compiler_params=_params(40),
    )(mem, g_kv, wkvT)


def _in_proj_fwd(x, g_mix, winT, b_in, cosq, sinq, tm=512):
    S = x.shape[0]

    def body(x_ref, g_ref, w_ref, b_ref, c_ref, s_ref, h_ref, uglu_ref, vg_ref, q_ref, k_ref, v_ref):
        h, _ = _rms(x_ref[...], g_ref[...])
        hb = h.astype(h_ref.dtype)
        h_ref[...] = hb
        u = _nt(hb, w_ref[...]) + b_ref[...]
        uglu_ref[...] = u[:, :2 * CC]
        vg_ref[...] = u[:, :CC] * _sigmoid(u[:, CC:2 * CC])
        c = c_ref[...]
        s = s_ref[...]
        q_ref[...] = _rope(u[:, 2 * CC:2 * CC + ATT], c, s).astype(q_ref.dtype)
        k_ref[...] = _rope(u[:, 2 * CC + ATT:2 * CC + ATT + KVC], c, s).astype(k_ref.dtype)
        v_ref[...] = u[:, 2 * CC + ATT + KVC:].astype(v_ref.dtype)

    return _hbm_call(
        body, name="in_proj_fwd", grid=(S // tm,),
        in_specs=[_rows(tm, D), _full((1, D)), _full((INC, D)), _full((1, INC)), _rows(tm, LANES), _rows(tm, LANES)],
        out_specs=[_rows(tm, D), _rows(tm, 2 * CC), _rows(tm, CC), _rows(tm, ATT), _rows(tm, KVC), _rows(tm, KVC)],
        out_shape=[_sds((S, D), _MXU), _sds((S, 2 * CC), F32), _sds((S, CC), F32), _sds((S, ATT), _MXU),
                   _sds((S, KVC), _MXU), _sds((S, KVC), _MXU)],
        compiler_params=_params(48),
    )(x, g_mix, winT, b_in, cosq, sinq)


CONV_HALO = 16


CONV_TILE = 256


def _shift_scratch(tm, sets):
    return pltpu.VMEM((sets, CC // LANES, 8, tm + 24, LANES), F32)


def _shifted_copies(pad_ref, sh_ref, t0, tm):
    for cl in range(CC // LANES):
        blk = pad_ref[pl.ds(t0, tm + 2 * CONV_HALO), cl * LANES:(cl + 1) * LANES]
        for b in range(8):
            sh_ref[cl, b, :, :] = blk[b:b + tm + 24, :]


def _conv_taps():
    return [(8 * a + b - 1, b, 8 * a) for b in range(8) for a in range(4) if 0 <= 8 * a + b - 1 < TAPS]


CONV_ROWS = 64
SUBLANES = 8


def _dwconv_tile(sh_ref, tm, w_ref, flip, out_ref):
    for cl in range(CC // LANES):
        lanes = slice(cl * LANES, (cl + 1) * LANES)

        def chunk(c, carry):
            r = pl.multiple_of(c * CONV_ROWS, CONV_ROWS)
            accs = [None] * 4
            for n, (k, b, off) in enumerate(_conv_taps()):
                kk = TAPS - 1 - k if flip else k
                term = sh_ref[cl, b, pl.ds(off + r, CONV_ROWS), :] * w_ref[kk:kk + 1, lanes]
                accs[n % 4] = term if accs[n % 4] is None else accs[n % 4] + term
            out_ref[pl.ds(r, CONV_ROWS), lanes] = (accs[0] + accs[1]) + (accs[2] + accs[3])
            return carry

        lax.fori_loop(0, tm // CONV_ROWS, chunk, 0)


def _fill_padded(i, src_hbm, pad_ref, sem, S):
    @pl.when(i == 0)
    def _():
        zeros = jnp.zeros((CONV_HALO, pad_ref.shape[1]), pad_ref.dtype)
        pad_ref[0:CONV_HALO, :] = zeros
        pad_ref[CONV_HALO + S:2 * CONV_HALO + S, :] = zeros
        cp = pltpu.make_async_copy(src_hbm, pad_ref.at[pl.ds(CONV_HALO, S), :], sem)
        cp.start()
        cp.wait()


def _ln_silu(y, g, b):
    mu = jnp.mean(y, axis=-1, keepdims=True)
    yc = y - mu
    var = jnp.mean(yc * yc, axis=-1, keepdims=True)
    rstd = lax.rsqrt(var + EPS)
    yh = yc * rstd
    z = yh * g + b
    return yh, rstd, z


def _conv_fwd(vg, w_dw, b_dw, ln_g, ln_b):
    S = vg.shape[0]
    tm = min(CONV_TILE, S)

    def body(vg_hbm, w_ref, b_ref, g_ref, bb_ref, y_ref, yc_ref, pad_ref, sh_ref, sem):
        i = pl.program_id(0)
        _fill_padded(i, vg_hbm, pad_ref, sem, S)
        _shifted_copies(pad_ref, sh_ref.at[0], pl.multiple_of(i * tm, tm), tm)
        _dwconv_tile(sh_ref.at[0], tm, w_ref, False, y_ref)
        y = y_ref[...] + b_ref[...]
        y_ref[...] = y
        _, _, z = _ln_silu(y, g_ref[...], bb_ref[...])
        yc_ref[...] = (z * _sigmoid(z)).astype(yc_ref.dtype)

    return _hbm_call(
        body, name="conv_fwd", grid=(S // tm,),
        in_specs=[pl.BlockSpec(memory_space=pl.ANY), _full((TAPS, CC)), _full((1, CC)), _full((1, CC)), _full((1, CC))],
        out_specs=[_rows(tm, CC), _rows(tm, CC)],
        out_shape=[_sds((S, CC), F32), _sds((S, CC), _MXU)],
        scratch_shapes=[pltpu.VMEM((S + 2 * CONV_HALO, CC), F32), _shift_scratch(tm, 1), pltpu.SemaphoreType.DMA],
        compiler_params=_params(40),
    )(vg, w_dw, b_dw, ln_g, ln_b)


def _attn_window(i, S):
    start = jnp.clip((i - 1) * BLK, 0, S - 3 * BLK)
    return pl.multiple_of(start, BLK)


def _attn_mask(i, start):
    keys = lax.broadcasted_iota(jnp.int32, (3 * BLK, GRP * BLK), 0)
    cols = lax.broadcasted_iota(jnp.int32, (3 * BLK, GRP * BLK), 1)
    rel = (start + keys) - (i * BLK + (cols & (BLK - 1)))
    return jnp.abs(rel) <= BLK


def _stack_heads(t, hk):
    return jnp.concatenate([t[:, (hk * GRP + g) * HD:(hk * GRP + g + 1) * HD] for g in range(GRP)], axis=0)


def _unstack_heads_t(t):
    cols = []
    for p in range(GRP // 2):
        pair = jnp.concatenate([t[:, (2 * p) * BLK:(2 * p + 1) * BLK], t[:, (2 * p + 1) * BLK:(2 * p + 2) * BLK]],
                               axis=0)
        cols.append(pair.T)
    return jnp.concatenate(cols, axis=1)


def _attn_probs(qs, kh, sk_ref, hk, mask):
    s = _nt(kh, qs) * SCALE
    s = jnp.where(mask, s, -1e30)
    sk = jnp.concatenate(
        [jnp.broadcast_to(sk_ref[0:1, hk * GRP + g:hk * GRP + g + 1], (1, BLK)) for g in range(GRP)], axis=1)
    m = jnp.maximum(jnp.max(s, axis=0, keepdims=True), sk)
    e = jnp.exp(s - m)
    es = jnp.exp(sk - m)
    inv = 1.0 / (jnp.sum(e, axis=0, keepdims=True) + es)
    return e, inv, es


ATTN_BLOCKS = 4


def _attn_fwd(qr, kr, vv, sink):
    S = qr.shape[0]
    tq = ATTN_BLOCKS * BLK

    def body(q_ref, k_ref, v_ref, sk_ref, o_ref):
        for b in range(ATTN_BLOCKS):
            i = pl.program_id(0) * ATTN_BLOCKS + b
            rows = slice(b * BLK, (b + 1) * BLK)
            start = _attn_window(i, S)
            kb = k_ref[pl.ds(start, 3 * BLK), :]
            vb = v_ref[pl.ds(start, 3 * BLK), :]
            q = q_ref[rows, :]
            mask = _attn_mask(i, start)
            outs = []
            for hk in range(NKV):
                hs = slice(hk * HD, (hk + 1) * HD)
                e, inv, _ = _attn_probs(_stack_heads(q, hk), kb[:, hs], sk_ref, hk, mask)
                outs.append(_unstack_heads_t(_tn(vb[:, hs], e) * inv))
            o_ref[rows, :] = jnp.concatenate(outs, axis=1).astype(o_ref.dtype)

    return _hbm_call(
        body, name="attn_fwd", grid=(S // tq,),
        in_specs=[_rows(tq, ATT), _full((S, KVC)), _full((S, KVC)), _full((1, NQ))],
        out_specs=_rows(tq, ATT),
        out_shape=_sds((S, ATT), _MXU),
        compiler_params=_params(40),
    )(qr, kr, vv, sink)


def _out_proj_fwd(x, yc, ya, wout, b_out, tm=512):
    S = x.shape[0]

    def body(x_ref, yc_ref, ya_ref, w_ref, b_ref, x1_ref, ymix_ref):
        ymix = jnp.concatenate([yc_ref[...], ya_ref[...]], axis=1)
        ymix_ref[...] = ymix
        x1_ref[...] = x_ref[...] + _nn(ymix, w_ref[...]) + b_ref[...]

    return _hbm_call(
        body, name="out_proj_fwd", grid=(S // tm,),
        in_specs=[_rows(tm, D), _rows(tm, CC), _rows(tm, ATT), _full((D, D)), _full((1, D))],
        out_specs=[_rows(tm, D), _rows(tm, D)],
        out_shape=[_sds((S, D), F32), _sds((S, D), _MXU)],
        compiler_params=_params(40),
    )(x, yc, ya, wout, b_out)


def _softmax_t(s):
    m = jnp.max(s, axis=0, keepdims=True)
    e = jnp.exp(s - m)
    return e, 1.0 / jnp.sum(e, axis=0, keepdims=True)


def _softmax(s):
    m = jnp.max(s, axis=-1, keepdims=True)
    e = jnp.exp(s - m)
    return e * (1.0 / jnp.sum(e, axis=-1, keepdims=True))


def _mem_attn_fwd(x1, g_q, wq, km, vm, wo, tm=512):
    S = x1.shape[0]
    M = km.shape[0]

    def body(x1_ref, g_ref, wq_ref, km_ref, vm_ref, wo_ref, x2_ref, qm_ref, om_ref):
        x1v = x1_ref[...]
        hq, _ = _rms(x1v, g_ref[...])
        qm = _nn(hq, wq_ref[...]).astype(qm_ref.dtype)
        qm_ref[...] = qm
        oms = []
        for h in range(MH):
            sl = slice(h * MHD, (h + 1) * MHD)
            p = _softmax(_nt(qm[:, sl], km_ref[:, sl]) * MSCALE)
            oms.append(_nn(p, vm_ref[:, sl]))
        om = jnp.concatenate(oms, axis=1).astype(om_ref.dtype)
        om_ref[...] = om
        x2_ref[...] = x1v + _nn(om, wo_ref[...])

    return _hbm_call(
        body, name="mem_attn_fwd", grid=(S // tm,),
        in_specs=[_rows(tm, D), _full((1, D)), _full((D, D)), _full((M, D)), _full((M, D)), _full((D, D))],
        out_specs=[_rows(tm, D)] * 3,
        out_shape=[_sds((S, D), F32), _sds((S, D), _MXU), _sds((S, D), _MXU)],
        compiler_params=_params(48),
    )(x1, g_q, wq, km, vm, wo)


def _ffn_loss_fwd(x2, tgt, g_ffn, g_final, wgT, wuT, wd, tm=512):
    S = x2.shape[0]

    def body(x2_ref, t_ref, gf_ref, gl_ref, wg_ref, wu_ref, wd_ref,
             gt_ref, up_ref, hf_ref, dx3_ref, dx3b_ref, loss_ref, dgl_ref):
        i = pl.program_id(0)
        _acc_init(i, loss_ref, dgl_ref)
        x2v = x2_ref[...]
        hf, _ = _rms(x2v, gf_ref[...])
        hb = hf.astype(hf_ref.dtype)
        hf_ref[...] = hb
        x3 = x2v
        for c in range(F // FCH):
            sl = slice(c * FCH, (c + 1) * FCH)
            gt = _nt(hb, wg_ref[sl, :])
            up = _nt(hb, wu_ref[sl, :])
            gt_ref[:, sl] = gt.astype(gt_ref.dtype)
            up_ref[:, sl] = up.astype(up_ref.dtype)
            x3 = x3 + _nn(gt * _sigmoid(gt) * up, wd_ref[sl, :])
        gl = gl_ref[...]
        y, r = _rms(x3, gl)
        diff = y - t_ref[...]
        _acc_row(loss_ref, jnp.sum(jnp.sum(diff * diff, axis=-1, keepdims=True), axis=0, keepdims=True) * (0.5 / D))
        dx3, dgl = _rms_bwd(diff * (1.0 / D), x3, r, gl)
        dx3_ref[...] = dx3
        dx3b_ref[...] = dx3.astype(dx3b_ref.dtype)
        _acc_row(dgl_ref, dgl)

    return _hbm_call(
        body, name="ffn_loss_fwd", grid=(S // tm,),
        in_specs=[_rows(tm, D), _rows(tm, D), _full((1, D)), _full((1, D)),
                  _once((F, D)), _once((F, D)), _once((F, D))],
        out_specs=[_rows(tm, F), _rows(tm, F), _rows(tm, D), _rows(tm, D), _rows(tm, D),
                   _full((8, LANES)), _full((8, D))],
        out_shape=[_sds((S, F), _MXU), _sds((S, F), _MXU), _sds((S, D), _MXU), _sds((S, D), F32),
                   _sds((S, D), _MXU), _sds((8, LANES), F32), _sds((8, D), F32)],
        compiler_params=_params(56),
    )(x2, tgt, g_ffn, g_final, wgT, wuT, wd)


def _ffn_bwd(dx3, x2, gt, up, g_ffn, wd, wgT, wuT, tm=256):
    S = x2.shape[0]

    def body(dx3_ref, x2_ref, gt_ref, up_ref, g_ref, wd_ref, wg_ref, wu_ref,
             dx2_ref, dx2b_ref, dgt_ref, dup_ref, act_ref, dg_ref):
        i = pl.program_id(0)
        _acc_init(i, dg_ref)
        dx3v = dx3_ref[...]
        db = dx3v.astype(_MXU)
        dhf = jnp.zeros((tm, D), F32)
        for c in range(F // FCH_BWD):
            sl = slice(c * FCH_BWD, (c + 1) * FCH_BWD)
            gtv = gt_ref[:, sl].astype(F32)
            upv = up_ref[:, sl].astype(F32)
            dact = _nt(db, wd_ref[sl, :])
            sig = _sigmoid(gtv)
            silu = gtv * sig
            dup = (dact * silu).astype(dup_ref.dtype)
            dgt = (dact * upv * (sig * (1.0 + gtv * (1.0 - sig)))).astype(dgt_ref.dtype)
            act_ref[:, sl] = (silu * upv).astype(act_ref.dtype)
            dgt_ref[:, sl] = dgt
            dup_ref[:, sl] = dup
            dhf = dhf + _nn(dgt, wg_ref[sl, :]) + _nn(dup, wu_ref[sl, :])
        x2v = x2_ref[...]
        g = g_ref[...]
        _, r = _rms(x2v, g)
        dxn, dg = _rms_bwd(dhf, x2v, r, g)
        dx2 = dx3v + dxn
        dx2_ref[...] = dx2
        dx2b_ref[...] = dx2.astype(dx2b_ref.dtype)
        _acc_row(dg_ref, dg)

    return _hbm_call(
        body, name="ffn_bwd", grid=(S // tm,),
        in_specs=[_rows(tm, D), _rows(tm, D), _rows(tm, F), _rows(tm, F), _full((1, D)),
                  _once((F, D)), _once((F, D)), _once((F, D))],
        out_specs=[_rows(tm, D), _rows(tm, D), _rows(tm, F), _rows(tm, F), _rows(tm, F), _full((8, D))],
        out_shape=[_sds((S, D), F32), _sds((S, D), _MXU), _sds((S, F), _MXU), _sds((S, F), _MXU),
                   _sds((S, F), _MXU), _sds((8, D), F32)],
        compiler_params=_params(56),
    )(dx3, x2, gt, up, g_ffn, wd, wgT, wuT)


_AFTER = pl.BlockSpec(memory_space=pl.ANY)


def _wgrad(pairs, name, after, rc=256):
    S, R = pairs[0][0].shape
    N = pairs[0][1].shape[1]
    rights = []
    for _, b in pairs:
        if not any(b is r for r in rights):
            rights.append(b)
    which = [next(i for i, r in enumerate(rights) if r is b) for _, b in pairs]
    n, nb = len(pairs), len(rights)

    def body(*refs):
        for k in range(n):
            refs[n + nb + 1 + k][...] = _tn(refs[k][...], refs[n + which[k]][...]).astype(jnp.bfloat16)

    res = _hbm_call(
        body, name=name, grid=(R // rc,),
        in_specs=[pl.BlockSpec((S, rc), lambda j: (0, j))] * n + [_once((S, N))] * nb + [_AFTER],
        out_specs=[_rows(rc, N)] * n,
        out_shape=[_sds((R, N), jnp.bfloat16)] * n,
        compiler_params=_params(56),
    )(*[a for a, _ in pairs], *rights, after)
    return list(res)


def _mem_attn_bwd(dx2, x1, qm, km, vm, g_q, wq, wo, wout, after, tm=512):
    S = x1.shape[0]
    M = km.shape[0]

    def body(dx2_ref, x1_ref, qm_ref, km_ref, vm_ref, g_ref, wq_ref, wo_ref, wout_ref, after_ref,
             dx1_ref, dx1b_ref, hq_ref, dqm_ref, dyc_ref, dya_ref, dkm_ref, dvm_ref, dg_ref, dbo_ref):
        i = pl.program_id(0)
        _acc_init(i, dkm_ref, dvm_ref, dg_ref, dbo_ref)
        dx2v = dx2_ref[...]
        domt = _nt(wo_ref[...], dx2v).astype(_MXU)
        dqs = []
        for h in range(MH):
            sl = slice(h * MHD, (h + 1) * MHD)
            qh = qm_ref[:, sl]
            kh = km_ref[:, sl]
            e, inv = _softmax_t(_nt(kh, qh) * MSCALE)
            p = e * inv
            dp = _nn(vm_ref[:, sl], domt[sl, :])
            ds = (p * (dp - jnp.sum(dp * p, axis=0, keepdims=True))).astype(_MXU)
            dqs.append(_tn(ds, kh) * MSCALE)
            dkm_ref[:, sl] += _nn(ds, qh) * MSCALE
            dvm_ref[:, sl] += _nt(p, domt[sl, :])
        dqm = jnp.concatenate(dqs, axis=1).astype(dqm_ref.dtype)
        dqm_ref[...] = dqm
        dhq = _nt(dqm, wq_ref[...])
        x1v = x1_ref[...]
        g = g_ref[...]
        hq, r = _rms(x1v, g)
        hq_ref[...] = hq.astype(hq_ref.dtype)
        dxn, dg = _rms_bwd(dhq, x1v, r, g)
        dx1 = dx2v + dxn
        dx1_ref[...] = dx1
        dx1b = dx1.astype(dx1b_ref.dtype)
        dx1b_ref[...] = dx1b
        dym = _nt(dx1b, wout_ref[...])
        dyc_ref[...] = dym[:, :CC]
        dya_ref[...] = dym[:, CC:].astype(dya_ref.dtype)
        _acc_row(dg_ref, dg)
        _acc_row(dbo_ref, jnp.sum(dx1, axis=0, keepdims=True))

    return _hbm_call(
        body, name="mem_attn_bwd", grid=(S // tm,),
        in_specs=[_rows(tm, D), _rows(tm, D), _rows(tm, D), _full((M, D)), _full((M, D)), _full((1, D)),
                  _once((D, D)), _once((D, D)), _once((D, D)), _AFTER],
        out_specs=[_rows(tm, D), _rows(tm, D), _rows(tm, D), _rows(tm, D), _rows(tm, CC), _rows(tm, ATT),
                   _full((M, D)), _full((M, D)), _full((8, D)), _full((8, D))],
        out_shape=[_sds((S, D), F32), _sds((S, D), _MXU), _sds((S, D), _MXU), _sds((S, D), _MXU),
                   _sds((S, CC), F32), _sds((S, ATT), _MXU),
                   _sds((M, D), F32), _sds((M, D), F32), _sds((8, D), F32), _sds((8, D), F32)],
        compiler_params=_params(56),
    )(dx2, x1, qm, km, vm, g_q, wq, wo, wout, after)


def _mem_kv_bwd(dkm, dvm, mem, g_kv, memn, wkvT):
    M = mem.shape[0]

    def body(dkm_ref, dvm_ref, mem_ref, g_ref, mn_ref, w_ref, gw_ref, dg_ref):
        dkv = jnp.concatenate([dkm_ref[...], dvm_ref[...]], axis=1).astype(_MXU)
        gw_ref[...] = _tn(dkv, mn_ref[...]).astype(gw_ref.dtype)
        dmn = _nn(dkv, w_ref[...])
        memv = mem_ref[...]
        g = g_ref[...]
        _, r = _rms(memv, g)
        _, dg = _rms_bwd(dmn, memv, r, g)
        dg_ref[...] = jnp.broadcast_to(dg, dg_ref.shape)

    return _hbm_call(
        body, name="mem_kv_bwd", grid=(1,),
        in_specs=[_full((M, D)), _full((M, D)), _full((M, D)), _full((1, D)), _full((M, D)), _full((2 * D, D))],
        out_specs=[_full((2 * D, D)), _full((8, D))],
        out_shape=[_sds((2 * D, D), jnp.bfloat16), _sds((8, D), F32)],
        compiler_params=_params(48),
    )(dkm, dvm, mem, g_kv, memn, wkvT)


def _attn_bwd(qr, kr, vv, sink, dya, after):
    S = qr.shape[0]
    tq = ATTN_BLOCKS * BLK

    def body(q_ref, k_ref, v_ref, sk_ref, do_ref, after_ref, dq_ref, dk_ref, dv_ref, dsk_ref, sink_acc):
        _acc_init(pl.program_id(0), dk_ref, dv_ref, sink_acc)
        for b in range(ATTN_BLOCKS):
            i = pl.program_id(0) * ATTN_BLOCKS + b
            rows = slice(b * BLK, (b + 1) * BLK)
            start = _attn_window(i, S)
            kb = k_ref[pl.ds(start, 3 * BLK), :]
            vb = v_ref[pl.ds(start, 3 * BLK), :]
            q = q_ref[rows, :]
            do = do_ref[rows, :]
            mask = _attn_mask(i, start)
            dqs, dks, dvs = [], [], []
            for hk in range(NKV):
                hs = slice(hk * HD, (hk + 1) * HD)
                qs = _stack_heads(q, hk)
                dos = _stack_heads(do, hk)
                e, inv, es = _attn_probs(qs, kb[:, hs], sk_ref, hk, mask)
                p = e * inv
                dp = _nt(vb[:, hs], dos)
                delta = jnp.sum(dp * p, axis=0, keepdims=True)
                ds = (p * (dp - delta)).astype(_MXU)
                dqs.append(_unstack_heads_t(_tn(kb[:, hs], ds) * SCALE))
                dks.append(_nn(ds, qs) * SCALE)
                dvs.append(_nn(p, dos))
                sink_acc[hk:hk + 1, :] -= es * inv * delta
            dq_ref[rows, :] = jnp.concatenate(dqs, axis=1)
            dk_ref[pl.ds(start, 3 * BLK), :] += jnp.concatenate(dks, axis=1)
            dv_ref[pl.ds(start, 3 * BLK), :] += jnp.concatenate(dvs, axis=1)

        @pl.when(pl.program_id(0) == pl.num_programs(0) - 1)
        def _():
            for h in range(NQ):
                tot = jnp.sum(sink_acc[h // GRP:h // GRP + 1, (h % GRP) * BLK:(h % GRP + 1) * BLK], axis=1,
                              keepdims=True)
                dsk_ref[h:h + 1, :] = jnp.broadcast_to(tot, (1, LANES))

    return _hbm_call(
        body, name="attn_bwd", grid=(S // tq,),
        in_specs=[_rows(tq, ATT), _full((S, KVC)), _full((S, KVC)), _full((1, NQ)), _rows(tq, ATT), _AFTER],
        out_specs=[_rows(tq, ATT), _full((S, KVC)), _full((S, KVC)), _full((8, LANES))],
        out_shape=[_sds((S, ATT), F32), _sds((S, KVC), F32), _sds((S, KVC), F32), _sds((8, LANES), F32)],
        scratch_shapes=[pltpu.VMEM((8, GRP * BLK), F32)],
        compiler_params=_params(40),
    )(qr, kr, vv, sink, dya, after)


def _conv_norm_bwd(dyc, y, ln_g, ln_b, after, tm=512):
    S = y.shape[0]

    def body(dyc_ref, y_ref, g_ref, b_ref, after_ref, dy_ref, dg_ref, db_ref, dbdw_ref):
        i = pl.program_id(0)
        _acc_init(i, dg_ref, db_ref, dbdw_ref)
        g = g_ref[...]
        yh, rstd, z = _ln_silu(y_ref[...], g, b_ref[...])
        sig = _sigmoid(z)
        dz = dyc_ref[...] * (sig * (1.0 + z * (1.0 - sig)))
        dyh = dz * g
        dy = rstd * (dyh - jnp.mean(dyh, axis=-1, keepdims=True)
                     - yh * jnp.mean(dyh * yh, axis=-1, keepdims=True))
        dy_ref[...] = dy
        _acc_row(dg_ref, jnp.sum(dz * yh, axis=0, keepdims=True))
        _acc_row(db_ref, jnp.sum(dz, axis=0, keepdims=True))
        _acc_row(dbdw_ref, jnp.sum(dy, axis=0, keepdims=True))

    return _hbm_call(
        body, name="conv_norm_bwd", grid=(S // tm,),
        in_specs=[_rows(tm, CC), _rows(tm, CC), _full((1, CC)), _full((1, CC)), _AFTER],
        out_specs=[_rows(tm, CC), _full((8, CC)), _full((8, CC)), _full((8, CC))],
        out_shape=[_sds((S, CC), F32), _sds((8, CC), F32), _sds((8, CC), F32), _sds((8, CC), F32)],
        compiler_params=_params(40),
    )(dyc, y, ln_g, ln_b, after)


def _conv_bwd(dy, vg, uglu, w_dw):
    S = dy.shape[0]
    tm = min(CONV_TILE, S)

    def body(dy_hbm, vg_hbm, uglu_ref, w_ref, du_ref, gw_ref, dbin_ref, dyp_ref, vgp_ref, gacc_ref, sh_ref, dvg_ref,
             sems):
        i = pl.program_id(0)
        _acc_init(i, gacc_ref, dbin_ref)
        _fill_padded(i, dy_hbm, dyp_ref, sems.at[0], S)
        _fill_padded(i, vg_hbm, vgp_ref, sems.at[1], S)
        t0 = pl.multiple_of(i * tm, tm)
        _shifted_copies(dyp_ref, sh_ref.at[0], t0, tm)
        _shifted_copies(vgp_ref, sh_ref.at[1], t0, tm)
        _dwconv_tile(sh_ref.at[0], tm, w_ref, True, dvg_ref)
        dvg = dvg_ref[...]
        sub = CONV_ROWS
        for cl in range(CC // LANES):
            lanes = slice(cl * LANES, (cl + 1) * LANES)
            for k, b, off in _conv_taps():
                part = jnp.zeros((8, LANES), F32)
                for r in range(0, tm, sub):
                    prod = (sh_ref[0, cl, 0, CONV_HALO + r:CONV_HALO + r + sub, :]
                            * sh_ref[1, cl, b, off + r:off + r + sub, :])
                    part = part + jnp.sum(prod.reshape(sub // 8, 8, LANES), axis=0)
                gacc_ref[k, :, lanes] += part
        a_v = uglu_ref[:, :CC]
        sg = _sigmoid(uglu_ref[:, CC:])
        da = dvg * sg
        dgate = dvg * a_v * sg * (1.0 - sg)
        du = jnp.concatenate([da, dgate], axis=1)
        du_ref[...] = du.astype(du_ref.dtype)
        _acc_row(dbin_ref, jnp.sum(du, axis=0, keepdims=True))

        @pl.when(i == pl.num_programs(0) - 1)
        def _():
            gw_ref[...] = jnp.zeros(gw_ref.shape, F32)
            for k in range(TAPS):
                gw_ref[k:k + 1, :] = jnp.sum(gacc_ref[k], axis=0, keepdims=True)

    return _hbm_call(
        body, name="conv_bwd", grid=(S // tm,),
        in_specs=[pl.BlockSpec(memory_space=pl.ANY), pl.BlockSpec(memory_space=pl.ANY), _rows(tm, 2 * CC),
                  _full((TAPS, CC))],
        out_specs=[_rows(tm, 2 * CC), _full((TAPS + 1, CC)), _full((8, 2 * CC))],
        out_shape=[_sds((S, 2 * CC), _MXU), _sds((TAPS + 1, CC), F32), _sds((8, 2 * CC), F32)],
        scratch_shapes=[pltpu.VMEM((S + 2 * CONV_HALO, CC), F32), pltpu.VMEM((S + 2 * CONV_HALO, CC), F32),
                        pltpu.VMEM((TAPS, 8, CC), F32), _shift_scratch(tm, 2), pltpu.VMEM((tm, CC), F32),
                        pltpu.SemaphoreType.DMA((2,))],
        compiler_params=_params(56),
    )(dy, vg, uglu, w_dw)


def _in_proj_bwd(duglu, dqr, dkr, dv, cosq, sinq, x, dx1, g_mix, winT, tm=512):
    S = x.shape[0]
    NA = ATT + 2 * KVC

    def body(dug_ref, dq_ref, dk_ref, dv_ref, c_ref, s_ref, x_ref, dx1_ref, g_ref, w_ref,
             dx_ref, du_ref, dg_ref, dbin_ref):
        i = pl.program_id(0)
        _acc_init(i, dg_ref, dbin_ref)
        c = c_ref[...]
        s = s_ref[...]
        datt = jnp.concatenate([_rope_bwd(dq_ref[...], c, s), _rope_bwd(dk_ref[...], c, s), dv_ref[...]], axis=1)
        _acc_row(dbin_ref, jnp.sum(datt, axis=0, keepdims=True))
        du = jnp.concatenate([dug_ref[...], datt.astype(du_ref.dtype)], axis=1)
        du_ref[...] = du
        dh = _nn(du, w_ref[...])
        xv = x_ref[...]
        g = g_ref[...]
        _, r = _rms(xv, g)
        dxn, dg = _rms_bwd(dh, xv, r, g)
        dx_ref[...] = dx1_ref[...] + dxn
        _acc_row(dg_ref, dg)

    return _hbm_call(
        body, name="in_proj_bwd", grid=(S // tm,),
        in_specs=[_rows(tm, 2 * CC), _rows(tm, ATT), _rows(tm, KVC), _rows(tm, KVC), _rows(tm, LANES),
                  _rows(tm, LANES), _rows(tm, D), _rows(tm, D), _full((1, D)), _once((INC, D))],
        out_specs=[_rows(tm, D), _rows(tm, INC), _full((8, D)), _full((8, NA))],
        out_shape=[_sds((S, D), F32), _sds((S, INC), _MXU), _sds((8, D), F32), _sds((8, NA), F32)],
        compiler_params=_params(48),
    )(duglu, dqr, dkr, dv, cosq, sinq, x, dx1, g_mix, winT)


def _rope_tables(S):
    pos = jnp.arange(S, dtype=F32)
    inv_freq = ROPE_THETA ** (-jnp.arange(0, HD, 2, dtype=F32) / HD)
    ang = pos[:, None] * inv_freq[None, :]
    cos, sin = jnp.cos(ang), jnp.sin(ang)
    cosq = jnp.tile(jnp.concatenate([cos, cos], axis=1), (1, LANES // HD))
    sinq = jnp.tile(jnp.concatenate([-sin, sin], axis=1), (1, LANES // HD))
    return cosq, sinq


GATHER_GROUPS = (("w_inT", "w_dw"), ("w_out", "w_mem_kvT", "w_mem_q", "w_mem_o"), ("w_gateT", "w_upT"), ("w_down",))
ID_GATHER, ID_SIBLING, ID_CHIP = 1, 5, 8


def _local_step(x, mem, tgt, sp, gather, reduce):
    S = x.shape[0]
    cosq, sinq = _rope_tables(S)

    gather.forward(0, cosq)
    w0 = gather.finish(0, sinq)
    winT, w_dw_full = w0["w_inT"], w0["w_dw"]
    h, uglu, vg, qr, kr, vv = _in_proj_fwd(x, sp["g_mix"], winT, sp["b_in"], cosq, sinq)
    ya = _attn_fwd(qr, kr, vv, sp["attn_sink"])
    gather.forward(1, ya)
    y, yc = _conv_fwd(vg, w_dw_full, sp["b_dw"], sp["g_conv_ln"], sp["b_conv_ln"])
    w1 = gather.finish(1, yc)
    wout, wkvT, wq, wo = w1["w_out"], w1["w_mem_kvT"], w1["w_mem_q"], w1["w_mem_o"]
    km, vm, memn = _mem_kv_fwd(mem, sp["g_mem_kv"], wkvT)
    x1, ymix = _out_proj_fwd(x, yc, ya, wout, sp["b_out"])
    gather.forward(2, x1)
    x2, qm, om = _mem_attn_fwd(x1, sp["g_mem_q"], wq, km, vm, wo)
    w2 = gather.finish(2, om)
    gather.forward(3, om)
    wgT, wuT, wd = w2["w_gateT"], w2["w_upT"], gather.finish(3, om)["w_down"]
    gt, up, hf, dx3, dx3b, loss_acc, dg_final = _ffn_loss_fwd(x2, tgt, sp["g_ffn"], sp["g_final"], wgT, wuT, wd)

    dx2, dx2b, dgt, dup, act, dg_ffn = _ffn_bwd(dx3, x2, gt, up, sp["g_ffn"], wd, wgT, wuT)
    token = reduce.start(0, _wgrad([(dgt, hf), (dup, hf), (act, dx3b)], "wgrad_ffn", dx2))

    dx1, dx1b, hq, dqm, dyc, dya, dkm, dvm, dg_mem_q, db_out = _mem_attn_bwd(
        dx2, x1, qm, km, vm, sp["g_mem_q"], wq, wo, wout, token)
    token = reduce.middle(0, dx1)
    g_wout, g_wq, g_wo = _wgrad([(ymix, dx1b), (hq, dqm), (om, dx2b)], "wgrad_mix_mem", token)
    g_wkvT, dg_mem_kv = _mem_kv_bwd(dkm, dvm, mem, sp["g_mem_kv"], memn, wkvT)
    token = reduce.start(1, [g_wout, g_wq, g_wkvT, g_wo])

    dqr, dkr, dvv, dsink = _attn_bwd(qr, kr, vv, sp["attn_sink"], dya, token)
    token = reduce.middle(1, dqr)
    dy, dg_cln, db_cln, db_dw = _conv_norm_bwd(dyc, y, sp["g_conv_ln"], sp["b_conv_ln"], token)
    duglu, g_wdw, dbin_glu = _conv_bwd(dy, vg, uglu, w_dw_full)
    grad_x, du, dg_mix, dbin_att = _in_proj_bwd(duglu, dqr, dkr, dvv, cosq, sinq, x, dx1, sp["g_mix"], winT)
    token = reduce.start(2, _wgrad([(du, h)], "wgrad_in", grad_x))

    small = {
        "loss": loss_acc[0:1, :],
        "g_mix": dg_mix[0:1], "b_in": jnp.concatenate([dbin_glu[0:1], dbin_att[0:1]], axis=1),
        "b_dw": db_dw[0:1], "g_conv_ln": dg_cln[0:1], "b_conv_ln": db_cln[0:1],
        "attn_sink": dsink[:, 0].reshape(1, NQ), "b_out": db_out[0:1], "g_mem_q": dg_mem_q[0:1],
        "g_mem_kv": dg_mem_kv[0:1], "g_ffn": dg_ffn[0:1], "g_final": dg_final[0:1],
    }
    token = reduce.middle(2, token, reduce.small_zones(small, g_wdw))
    reduce.finish(0, token)
    reduce.finish(1, token)
    return grad_x


def _place():
    return lax.axis_index("x"), lax.axis_index("y"), lax.axis_index("c")


def _hbm_specs(n):
    return [pl.BlockSpec(memory_space=pl.ANY)] * n


_HBM = pl.BlockSpec(memory_space=pltpu.HBM)
_SEM = pl.BlockSpec(memory_space=pltpu.SEMAPHORE)
_ANY = pl.BlockSpec(memory_space=pl.ANY)
_EFFECT = pltpu.SideEffectType.DATAFLOW_SIDE_EFFECTING
_VMEM = pl.BlockSpec(memory_space=pltpu.VMEM)
_TOKEN = jax.ShapeDtypeStruct((8, LANES), F32)


def _dev_index(p):
    return 4 * p[0] + 2 * p[1] + p[2]


def _gather_start(groups):
    sizes = [len(g) for g in groups]
    flat = [z for g in groups for z in g]
    n = len(flat)

    def body(*refs):
        ins = refs[:n]
        sems = refs[2 * n:]
        x, y, c = _place()
        me = _dev_index((x, y, c))
        peers = [(x, y, 1 - c), (1 - x, y, c), (x, 1 - y, c), (1 - x, 1 - y, c)]
        base = 0
        for gi, size in enumerate(sizes):
            send_sems, recv_sems = sems[2 * gi], sems[2 * gi + 1]
            for k in range(size):
                zone = ins[base + k]
                for j, peer in enumerate(peers):
                    pltpu.make_async_remote_copy(
                        src_ref=zone.at[me], dst_ref=zone.at[me], send_sem=send_sems.at[4 * k + j],
                        recv_sem=recv_sems.at[4 * k + j], device_id=peer, device_id_type=MESH).start()
            base += size

    out_shape = [pltpu.HBM(z.shape, z.dtype) for z in flat]
    out_specs = [_HBM] * n
    for size in sizes:
        out_shape += [pltpu.SemaphoreType.DMA((4 * size,))] * 2
        out_specs += [_SEM, _SEM]
    res = pl.pallas_call(
        body, name="gather_start", in_specs=[_HBM] * n, out_specs=out_specs, out_shape=out_shape,
        input_output_aliases={k: k for k in range(n)},
        compiler_params=pltpu.CompilerParams(has_side_effects=_EFFECT),
    )(*[pltpu.with_memory_space_constraint(z, pltpu.HBM) for z in flat])
    out, base = [], 0
    for gi, size in enumerate(sizes):
        out.append((list(res[base:base + size]), res[n + 2 * gi], res[n + 2 * gi + 1]))
        base += size
    return out


def _gather_forward(zones, send_sems, recv_sems, after, collective_id, name):
    n = len(zones)

    def body(*refs):
        ins = refs[:n]
        send_ref, recv_ref = refs[n], refs[n + 1]
        fsend, frecv = refs[2 * n + 3], refs[2 * n + 4]
        x, y, c = _place()
        sibling = (x, y, 1 - c)
        barrier = pltpu.get_barrier_semaphore()
        pl.semaphore_signal(barrier, inc=1, device_id=sibling, device_id_type=MESH)
        pl.semaphore_wait(barrier, 1)
        chips = [(1 - x, y), (x, 1 - y), (1 - x, 1 - y)]
        for j, chip in enumerate(chips):
            for k in range(n):
                slot = ins[k].at[_dev_index((*chip, c))]
                pltpu.make_async_remote_copy(
                    src_ref=slot, dst_ref=slot, send_sem=send_ref.at[4 * k + 1 + j],
                    recv_sem=recv_ref.at[4 * k + 1 + j], device_id=sibling, device_id_type=MESH).wait_recv()
                pltpu.make_async_remote_copy(
                    src_ref=slot, dst_ref=slot, send_sem=fsend.at[3 * k + j], recv_sem=frecv.at[3 * k + j],
                    device_id=sibling, device_id_type=MESH).start()
        for k in range(n):
            own = ins[k].at[_dev_index((x, y, c))]
            theirs = ins[k].at[_dev_index(sibling)]
            pltpu.make_async_remote_copy(
                src_ref=theirs, dst_ref=theirs, send_sem=send_ref.at[4 * k], recv_sem=recv_ref.at[4 * k],
                device_id=sibling, device_id_type=MESH).wait_recv()
            for j in range(4):
                pltpu.make_async_remote_copy(
                    src_ref=own, dst_ref=own, send_sem=send_ref.at[4 * k + j], recv_sem=recv_ref.at[4 * k + j],
                    device_id=sibling, device_id_type=MESH).wait_send()

    res = pl.pallas_call(
        body, name=name, in_specs=[_HBM] * n + [_SEM, _SEM, _ANY],
        out_specs=[_HBM] * n + [_SEM, _SEM],
        out_shape=[pltpu.HBM(z.shape, z.dtype) for z in zones] + [pltpu.SemaphoreType.DMA((3 * n,))] * 2,
        input_output_aliases={k: k for k in range(n)},
        compiler_params=pltpu.CompilerParams(has_side_effects=_EFFECT, collective_id=collective_id),
    )(*zones, send_sems, recv_sems, after)
    return list(res[:n]), res[n], res[n + 1]


def _gather_finish(zones, fsend, frecv, after, name):
    n = len(zones)

    def body(*refs):
        ins = refs[:n]
        fsend_ref, frecv_ref = refs[n], refs[n + 1]
        x, y, c = _place()
        sibling = (x, y, 1 - c)
        chips = [(1 - x, y), (x, 1 - y), (1 - x, 1 - y)]
        for j, chip in enumerate(chips):
            for k in range(n):
                mine = ins[k].at[_dev_index((*chip, c))]
                theirs = ins[k].at[_dev_index((*chip, 1 - c))]
                pltpu.make_async_remote_copy(
                    src_ref=theirs, dst_ref=theirs, send_sem=fsend_ref.at[3 * k + j],
                    recv_sem=frecv_ref.at[3 * k + j], device_id=sibling, device_id_type=MESH).wait_recv()
                pltpu.make_async_remote_copy(
                    src_ref=mine, dst_ref=mine, send_sem=fsend_ref.at[3 * k + j],
                    recv_sem=frecv_ref.at[3 * k + j], device_id=sibling, device_id_type=MESH).wait_send()

    res = pl.pallas_call(
        body, name=name, in_specs=[_HBM] * n + [_SEM, _SEM, _ANY], out_specs=[_HBM] * n,
        out_shape=[pltpu.HBM(z.shape, z.dtype) for z in zones],
        input_output_aliases={k: k for k in range(n)},
        compiler_params=pltpu.CompilerParams(has_side_effects=_EFFECT),
    )(*zones, fsend, frecv, after)
    return list(res)


class _Gather:
    def __init__(self, started):
        self._phase = list(started)

    def forward(self, gi, after):
        zones, send_sems, recv_sems = self._phase[gi]
        self._phase[gi] = _gather_forward(zones, send_sems, recv_sems, after, ID_GATHER + gi,
                                           "gather_forward_%d" % gi)

    def finish(self, gi, after):
        zones, fsend, frecv = self._phase[gi]
        out = {}
        for name, z in zip(GATHER_GROUPS[gi], _gather_finish(zones, fsend, frecv, after, "gather_finish_%d" % gi)):
            if name == "w_dw":
                out[name] = jnp.transpose(z, (1, 0, 2)).reshape(TAPS + 1, CC)[:TAPS]
            else:
                out[name] = z.reshape(NDEV * z.shape[1], D)
        return out


def _handshake(peers):
    barrier = pltpu.get_barrier_semaphore()
    for peer in peers:
        pl.semaphore_signal(barrier, inc=1, device_id=peer, device_id_type=MESH)
    pl.semaphore_wait(barrier, len(peers))


def _sibling_copies(srcs, lands, send_sems, recv_sems):
    x, y, c = _place()
    return [pltpu.make_async_remote_copy(
        src_ref=srcs[k].at[2 * s + (1 - c)], dst_ref=lands[k].at[s], send_sem=send_sems.at[4 * k + s],
        recv_sem=recv_sems.at[4 * k + s], device_id=(x, y, 1 - c), device_id_type=MESH)
        for k in range(len(srcs)) for s in range(4)]


def _sibling_start(gbs, collective_id, name):
    n = len(gbs)
    lands = [lax.empty((4,) + g.shape[1:], g.dtype) for g in gbs]

    def body(*refs):
        x, y, c = _place()
        _handshake([(x, y, 1 - c)])
        for cp in _sibling_copies(refs[:n], refs[n:2 * n], refs[4 * n], refs[4 * n + 1]):
            cp.start()
        refs[4 * n + 2][...] = jnp.zeros(_TOKEN.shape, _TOKEN.dtype)

    res = pl.pallas_call(
        body, name=name, in_specs=[_HBM] * (2 * n), out_specs=[_HBM] * (2 * n) + [_SEM, _SEM, _VMEM],
        out_shape=[pltpu.HBM(a.shape, a.dtype) for a in gbs + lands] + [pltpu.SemaphoreType.DMA((4 * n,))] * 2
        + [_TOKEN],
        input_output_aliases={k: k for k in range(2 * n)},
        compiler_params=pltpu.CompilerParams(has_side_effects=_EFFECT, collective_id=collective_id),
    )(*[pltpu.with_memory_space_constraint(a, pltpu.HBM) for a in gbs + lands])
    return list(res[:n]), list(res[n:2 * n]), res[2 * n], res[2 * n + 1], res[2 * n + 2]


def _sibling_wait(gbs, lands, send_sems, recv_sems, after, name):
    n = len(gbs)

    def body(*refs):
        for cp in _sibling_copies(refs[:n], refs[n:2 * n], refs[2 * n], refs[2 * n + 1]):
            cp.wait_recv()
            cp.wait_send()

    res = pl.pallas_call(
        body, name=name, in_specs=[_HBM] * (2 * n) + [_SEM, _SEM, _ANY], out_specs=[_HBM] * (2 * n),
        out_shape=[pltpu.HBM(a.shape, a.dtype) for a in gbs + lands],
        input_output_aliases={k: k for k in range(2 * n)},
        compiler_params=pltpu.CompilerParams(has_side_effects=_EFFECT),
    )(*gbs, *lands, send_sems, recv_sems, after)
    return list(res[:n]), list(res[n:])


def _pair_sum(gfs, ts, c_idx, name):
    n = len(gfs)
    halves = 2

    def body(c_ref, *refs):
        for k in range(n):
            refs[2 * n + k][...] = (refs[k][...].astype(F32) + refs[n + k][...].astype(F32)).astype(jnp.bfloat16)

    in_specs, out_specs, out_shape = [], [], []
    for g in gfs:
        rh = g.shape[1] // halves
        in_specs.append(pl.BlockSpec((None, rh, D), lambda s, h, c_ref: (2 * s + c_ref[0], h, 0)))
    for g in gfs:
        rh = g.shape[1] // halves
        in_specs.append(pl.BlockSpec((None, rh, D), lambda s, h, c_ref: (s, h, 0)))
        out_specs.append(pl.BlockSpec((None, rh, D), lambda s, h, c_ref: (s, h, 0)))
        out_shape.append(_sds((4, g.shape[1], D), jnp.bfloat16))
    return pl.pallas_call(
        body, name=name,
        grid_spec=pltpu.PrefetchScalarGridSpec(num_scalar_prefetch=1, grid=(4, halves), in_specs=in_specs,
                                               out_specs=out_specs),
        out_shape=out_shape,
        compiler_params=pltpu.CompilerParams(dimension_semantics=("arbitrary", "arbitrary"),
                                             vmem_limit_bytes=48 * 2 ** 20),
    )(c_idx, *gfs, *ts)


def _chip_copies(pbs, lands, smalls, send_sems, recv_sems):
    x, y, c = _place()
    own = 2 * x + y
    me = _dev_index((x, y, c))
    chips = [(1 - x, y), (x, 1 - y), (1 - x, 1 - y)]
    n = len(pbs)
    copies = [pltpu.make_async_remote_copy(
        src_ref=pbs[k].at[2 * chip[0] + chip[1]], dst_ref=lands[k].at[own], send_sem=send_sems.at[3 * k + j],
        recv_sem=recv_sems.at[3 * k + j], device_id=(*chip, c), device_id_type=MESH)
        for k in range(n) for j, chip in enumerate(chips)]
    peers = [(x, y, 1 - c)] + [(*chip, cc) for chip in chips for cc in (c, 1 - c)]
    copies += [pltpu.make_async_remote_copy(
        src_ref=smalls[t].at[me], dst_ref=smalls[t].at[me], send_sem=send_sems.at[3 * n + 7 * t + j],
        recv_sem=recv_sems.at[3 * n + 7 * t + j], device_id=peer, device_id_type=MESH)
        for t in range(len(smalls)) for j, peer in enumerate(peers)]
    return copies


def _chip_start(pbs, smalls, collective_id, name):
    n, m = len(pbs), len(smalls)
    lands = [lax.empty(p.shape, p.dtype) for p in pbs]
    arrays = list(pbs) + lands + list(smalls)
    na = len(arrays)
    nsem = 3 * n + 7 * m

    def body(*refs):
        x, y, c = _place()
        chips = [(1 - x, y), (x, 1 - y), (1 - x, 1 - y)]
        peers = [(*chip, c) for chip in chips]
        if m:
            peers += [(x, y, 1 - c)] + [(*chip, 1 - c) for chip in chips]
        _handshake(peers)
        for cp in _chip_copies(refs[:n], refs[n:2 * n], refs[2 * n:na], refs[2 * na], refs[2 * na + 1]):
            cp.start()
        refs[2 * na + 2][...] = jnp.zeros(_TOKEN.shape, _TOKEN.dtype)

    res = pl.pallas_call(
        body, name=name, in_specs=[_HBM] * na, out_specs=[_HBM] * na + [_SEM, _SEM, _VMEM],
        out_shape=[pltpu.HBM(a.shape, a.dtype) for a in arrays] + [pltpu.SemaphoreType.DMA((nsem,))] * 2 + [_TOKEN],
        input_output_aliases={k: k for k in range(na)},
        compiler_params=pltpu.CompilerParams(has_side_effects=_EFFECT, collective_id=collective_id),
    )(*[pltpu.with_memory_space_constraint(a, pltpu.HBM) for a in arrays])
    return (list(res[:n]), list(res[n:2 * n]), list(res[2 * n:na]), res[na], res[na + 1]), res[na + 2]


def _chip_wait(pbs, lands, smalls, send_sems, recv_sems, after, name):
    n = len(pbs)
    arrays = list(pbs) + list(lands) + list(smalls)
    na = len(arrays)

    def body(*refs):
        for cp in _chip_copies(refs[:n], refs[n:2 * n], refs[2 * n:na], refs[na], refs[na + 1]):
            cp.wait_recv()
            cp.wait_send()

    res = pl.pallas_call(
        body, name=name, in_specs=[_HBM] * na + [_SEM, _SEM, _ANY], out_specs=[_HBM] * na,
        out_shape=[pltpu.HBM(a.shape, a.dtype) for a in arrays],
        input_output_aliases={k: k for k in range(na)},
        compiler_params=pltpu.CompilerParams(has_side_effects=_EFFECT),
    )(*arrays, send_sems, recv_sems, after)
    return list(res[:n]), list(res[n:2 * n]), list(res[2 * n:])


def _chip_sum(pbs, qs, own_idx, states, name):
    n = len(pbs)
    halves = 2
    n_in = 4 * n + 3 * sum(s is not None for s in states)

    def body(o_ref, *refs):
        nxt_in, nxt_out = 4 * n, n_in
        for k in range(n):
            acc = refs[4 * k][...].astype(F32)
            for j in range(1, 4):
                acc = acc + refs[4 * k + j][...].astype(F32)
            refs[nxt_out][...] = acc
            nxt_out += 1
            if states[k] is not None:
                w_ref, m_ref, v_ref = refs[nxt_in:nxt_in + 3]
                nxt_in += 3
                d, nm, nv = _adamw_math(w_ref[...], acc, m_ref[...], v_ref[...])
                refs[nxt_out][...] = d
                refs[nxt_out + 1][...] = nm
                refs[nxt_out + 2][...] = nv
                nxt_out += 3

    def other(j):
        return lambda h, o: ((o[0] + j) % 4, h, 0)

    def rows(r):
        return pl.BlockSpec((r // halves, D), lambda h, o: (h, 0))

    in_specs, ins = [], []
    for p, q in zip(pbs, qs):
        rh = p.shape[1] // halves
        in_specs.append(pl.BlockSpec((None, rh, D), lambda h, o: (o[0], h, 0)))
        ins.append(p)
        for j in range(1, 4):
            in_specs.append(pl.BlockSpec((None, rh, D), other(j)))
            ins.append(q)
    out_specs, out_shape, counts = [], [], []
    for p, s in zip(pbs, states):
        r = p.shape[1]
        if s is not None:
            in_specs += [rows(r)] * 3
            ins += list(s)
        counts.append(1 if s is None else 4)
        out_specs += [rows(r)] * counts[-1]
        out_shape += [_sds((r, D), F32)] * counts[-1]
    res = pl.pallas_call(
        body, name=name,
        grid_spec=pltpu.PrefetchScalarGridSpec(num_scalar_prefetch=1, grid=(halves,), in_specs=in_specs,
                                               out_specs=out_specs),
        out_shape=out_shape,
        compiler_params=pltpu.CompilerParams(dimension_semantics=("arbitrary",), vmem_limit_bytes=56 * 2 ** 20),
    )(own_idx, *ins)
    out, base = [], 0
    for c in counts:
        out.append(tuple(res[base:base + c]))
        base += c
    return out


REDUCE_GROUPS = (("w_gateT", "w_upT", "w_down"), ("w_out", "w_mem_q", "w_mem_kvT", "w_mem_o"), ("w_inT",))


class _Reduce:
    def __init__(self, states):
        x, y, c = _place()
        self._c = c.astype(jnp.int32).reshape(1)
        self._own = (2 * x + y).astype(jnp.int32).reshape(1)
        self._me = _dev_index((x, y, c))
        self._state = {}
        self._adam = states
        self.results = {}
        self.smalls = None

    def start(self, gi, wgrads):
        gbs = [g.reshape(NDEV, g.shape[0] // NDEV, D) for g in wgrads]
        gbs, lands, send_sems, recv_sems, token = _sibling_start(gbs, ID_SIBLING + gi, "reduce_sibling_start_%d" % gi)
        self._state[gi] = (gbs, lands, send_sems, recv_sems)
        return token

    def middle(self, gi, after, smalls=()):
        gbs, lands, send_sems, recv_sems = self._state[gi]
        gbs, ts = _sibling_wait(gbs, lands, send_sems, recv_sems, after, "reduce_sibling_wait_%d" % gi)
        pbs = _pair_sum(gbs, ts, self._c, "reduce_pair_sum_%d" % gi)
        self._state[gi], token = _chip_start(pbs, list(smalls), ID_CHIP + gi, "reduce_chip_start_%d" % gi)
        return token

    def finish(self, gi, after):
        pbs, lands, smalls, send_sems, recv_sems = self._state[gi]
        pbs, qs, smalls = _chip_wait(pbs, lands, smalls, send_sems, recv_sems, after, "reduce_chip_wait_%d" % gi)
        names = REDUCE_GROUPS[gi]
        states = [self._adam.get(n) for n in names]
        self.results.update(zip(names, _chip_sum(pbs, qs, self._own, states, "reduce_update_%d" % gi)))
        if smalls:
            self.smalls = smalls

    def small_zones(self, small, g_wdw):
        pack = jnp.concatenate(
            [jnp.pad(small[name], ((0, 0), (0, lanes - small[name].shape[1]))) for name, lanes, _ in PACK], axis=1)
        zones = []
        for a in (pack, g_wdw):
            zone = lax.empty((NDEV,) + a.shape, a.dtype)
            zones.append(lax.dynamic_update_slice(zone, a[None], (self._me,) + (0,) * a.ndim))
        return zones


def _adamw_math(w, g, m, v):
    m = ADAM_B1 * m + (1.0 - ADAM_B1) * g
    v = ADAM_B2 * v + (1.0 - ADAM_B2) * (g * g)
    m_hat = m / (1.0 - ADAM_B1 ** ADAM_STEP)
    v_hat = v / (1.0 - ADAM_B2 ** ADAM_STEP)
    delta = -ADAM_LR * (m_hat / (jnp.sqrt(v_hat) + ADAM_EPS) + ADAM_WD * w)
    return delta, m, v


def _adamw(w, g, m, v, name):
    R, C = w.shape
    rc = 256 if R % 256 == 0 and R > 256 else R

    def body(w_ref, g_ref, m_ref, v_ref, d_ref, nm_ref, nv_ref):
        d, nm, nv = _adamw_math(w_ref[...], g_ref[...], m_ref[...], v_ref[...])
        d_ref[...] = d
        nm_ref[...] = nm
        nv_ref[...] = nv

    return _hbm_call(
        body, name=name, grid=(R // rc,),
        in_specs=[_rows(rc, C)] * 4, out_specs=[_rows(rc, C)] * 3,
        out_shape=[_sds((R, C), F32)] * 3,
        compiler_params=_params(40),
    )(w, g, m, v)


PACK = (("loss", LANES, 1), ("g_mix", D, D), ("b_in", INC, INC), ("b_dw", CC, CC), ("g_conv_ln", CC, CC),
        ("b_conv_ln", CC, CC), ("attn_sink", LANES, NQ), ("b_out", D, D), ("g_mem_q", D, D), ("g_mem_kv", D, D),
        ("g_ffn", D, D), ("g_final", D, D))
PACK_LANES = sum(p[1] for p in PACK)


def _small_update(packs, gdws, ws, ms, vs):
    names = [p[0] for p in PACK[1:]]
    n = len(names)

    def body(*refs):
        packs_ref, gdws_ref = refs[0], refs[1]
        w_refs, m_refs, v_refs = refs[2:2 + n], refs[2 + n:2 + 2 * n], refs[2 + 2 * n:2 + 3 * n]
        outs = refs[2 + 3 * n:]
        tot = packs_ref[0]
        gdw = gdws_ref[0]
        for s in range(1, NDEV):
            tot = tot + packs_ref[s]
            gdw = gdw + gdws_ref[s]
        outs[0][...] = tot[:, 0:LANES]
        outs[1][...] = gdw
        off = LANES
        for k, (_, lanes, used) in enumerate(PACK[1:]):
            g = tot[:, off:off + used]
            off += lanes
            d, nm, nv = _adamw_math(w_refs[k][...], g, m_refs[k][...], v_refs[k][...])
            o = outs[2 + 4 * k:6 + 4 * k]
            o[0][...] = g
            o[1][...] = d
            o[2][...] = nm
            o[3][...] = nv

    def full(a):
        nd = len(a.shape)
        return pl.BlockSpec(a.shape, lambda i: (0,) * nd)

    ins = [packs, gdws, *ws, *ms, *vs]
    out_shape = [_sds((1, LANES), F32), _sds(gdws.shape[1:], F32)]
    for w in ws:
        out_shape += [_sds(w.shape, F32)] * 4
    return pl.pallas_call(
        body, name="small_update", grid=(1,),
        in_specs=[full(a) for a in ins], out_specs=[full(o) for o in out_shape], out_shape=out_shape,
        compiler_params=_params(40),
    )(*ins)


def kernel(x, mem, g_mix, w_in, b_in, w_dw, b_dw, g_conv_ln, b_conv_ln, attn_sink, w_out, b_out, g_mem_q, g_mem_kv, w_mem_q, w_mem_kv, w_mem_o, g_ffn, w_gate, w_up, w_down, g_final, loss_target, m_g_mix, m_w_in, m_b_in, m_w_dw, m_b_dw, m_g_conv_ln, m_b_conv_ln, m_attn_sink, m_w_out, m_b_out, m_g_mem_q, m_g_mem_kv, m_w_mem_q, m_w_mem_kv, m_w_mem_o, m_g_ffn, m_w_gate, m_w_up, m_w_down, m_g_final, v_g_mix, v_w_in, v_b_in, v_w_dw, v_b_dw, v_g_conv_ln, v_b_conv_ln, v_attn_sink, v_w_out, v_b_out, v_g_mem_q, v_g_mem_kv, v_w_mem_q, v_w_mem_kv, v_w_mem_o, v_g_ffn, v_w_gate, v_w_up, v_w_down, v_g_final):
    given = dict(locals())
    bf = jnp.bfloat16
    xi, yi, ci = _place()
    me = 4 * xi + 2 * yi + ci

    transposed = {"w_inT": w_in, "w_mem_kvT": w_mem_kv, "w_gateT": w_gate, "w_upT": w_up}
    plain = {"w_out": w_out, "w_mem_q": w_mem_q, "w_mem_o": w_mem_o, "w_down": w_down}
    blocks = {n: (transposed[n][0].T if n in transposed else plain[n][0]).astype(bf) for n in W_NAMES}
    blocks["w_dw"] = jnp.pad(w_dw[0], ((0, 1), (0, 0)))

    def landing_zone(b):
        zone = lax.empty((NDEV,) + b.shape, b.dtype)
        return lax.dynamic_update_slice(zone, b[None], (me,) + (0,) * b.ndim)

    gather = _Gather(_gather_start([[landing_zone(blocks[n]) for n in names] for names in GATHER_GROUPS]))

    sp = {n: given[n] for n in ("g_mix", "b_in", "b_dw", "g_conv_ln", "b_conv_ln", "attn_sink", "b_out", "g_mem_q",
                                "g_mem_kv", "g_ffn")}
    sp["g_final"] = g_final.reshape(1, D)
    frames = {"w_inT": ("w_in", True), "w_gateT": ("w_gate", True), "w_upT": ("w_up", True),
              "w_out": ("w_out", False), "w_mem_q": ("w_mem_q", False), "w_mem_o": ("w_mem_o", False),
              "w_down": ("w_down", False)}
    states = {}
    for frame, (n, transposed) in frames.items():
        wmv = (given[n][0], given["m_" + n][0], given["v_" + n][0])
        states[frame] = tuple(a.T for a in wmv) if transposed else wmv
    reduce = _Reduce(states)
    grad_x = _local_step(x[0], mem[0], loss_target[0], sp, gather, reduce)
    reduce.finish(2, reduce.results["w_out"][1])

    grads, deltas, new_m, new_v = {}, {}, {}, {}
    for frame, (n, transposed) in frames.items():
        outs = [a.T if transposed else a for a in reduce.results[frame]]
        grads[n], deltas[n], new_m[n], new_v[n] = [a[None] for a in outs]

    def update(n, g):
        d, nm, nv = _adamw(given[n][0], g, given["m_" + n][0], given["v_" + n][0], "adamw_" + n)
        grads[n], deltas[n], new_m[n], new_v[n] = g[None], d[None], nm[None], nv[None]

    update("w_mem_kv", reduce.results["w_mem_kvT"][0].T)

    packs, gdws = reduce.smalls
    names_small = [p[0] for p in PACK[1:]]

    def as_row(a):
        return a.reshape(1, -1)

    res = _small_update(packs, gdws, [as_row(given[n]) for n in names_small],
                        [as_row(given["m_" + n]) for n in names_small],
                        [as_row(given["v_" + n]) for n in names_small])
    loss = res[0][0, 0]
    for k, n in enumerate(names_small):
        shape = given[n].shape
        g, d, nm, nv = res[2 + 4 * k:6 + 4 * k]
        grads[n], deltas[n], new_m[n], new_v[n] = (g.reshape(shape), d.reshape(shape), nm.reshape(shape),
                                                   nv.reshape(shape))
    update("w_dw", lax.dynamic_slice(res[1], (0, me * (CC // NDEV)), (TAPS, CC // NDEV)))

    order = ("g_mix", "w_in", "b_in", "w_dw", "b_dw", "g_conv_ln", "b_conv_ln", "attn_sink", "w_out", "b_out",
             "g_mem_q", "g_mem_kv", "w_mem_q", "w_mem_kv", "w_mem_o", "g_ffn", "w_gate", "w_up", "w_down", "g_final")
    return (loss, grad_x[None], *[grads[n] for n in order], *[deltas[n] for n in order],
            *[new_m[n] for n in order], *[new_v[n] for n in order])
```

```python
import functools

import jax
import jax.numpy as jnp
from jax import lax
from jax.experimental import pallas as pl
from jax.experimental.pallas import tpu as pltpu

D = 1024
CC = 512
HD = 64
NQ = 8
NKV = 2
GRP = NQ // NKV
ATT = NQ * HD
KVC = NKV * HD
BLK = 128
MH = 4
MHD = D // MH
F = 2816
FCH = 256
FCH_BWD = F // 2
INC = 2 * CC + ATT + 2 * KVC
TAPS = 31
EPS = 1e-6
ROPE_THETA = 10000.0
SCALE = HD ** -0.5
MSCALE = MHD ** -0.5
LANES = 128
NDEV = 8

ADAM_LR = 0.001
ADAM_B1 = 0.9
ADAM_B2 = 0.999
ADAM_EPS = 1e-08
ADAM_WD = 0.01
ADAM_STEP = 10

F32 = jnp.float32
_MXU = jnp.bfloat16
MESH = pl.DeviceIdType.MESH

W_NAMES = ("w_inT", "w_out", "w_mem_q", "w_mem_kvT", "w_mem_o", "w_gateT", "w_upT", "w_down")
W_ROWS = (INC // NDEV, D // NDEV, D // NDEV, 2 * D // NDEV, D // NDEV, F // NDEV, F // NDEV, F // NDEV)


def _mm(a, b, ca, cb):
    return lax.dot_general(a.astype(_MXU), b.astype(_MXU), (((ca,), (cb,)), ((), ())),
                           preferred_element_type=F32)


def _nn(a, b):
    return _mm(a, b, 1, 0)


def _nt(a, b):
    return _mm(a, b, 1, 1)


def _tn(a, b):
    return _mm(a, b, 0, 0)


def _sigmoid(x):
    return 1.0 / (1.0 + jnp.exp(-x))


def _rms(x, g):
    r = lax.rsqrt(jnp.mean(x * x, axis=-1, keepdims=True) + EPS)
    return x * r * g, r


def _rms_bwd(dy, x, r, g):
    xh = x * r
    dg = jnp.sum(dy * xh, axis=0, keepdims=True)
    dxh = dy * g
    dx = r * (dxh - xh * jnp.mean(dxh * xh, axis=-1, keepdims=True))
    return dx, dg


def _rope(t, c, s):
    n = t.shape[1]
    reps = n // LANES
    if reps > 1:
        c = jnp.tile(c, (1, reps))
        s = jnp.tile(s, (1, reps))
    lane = lax.broadcasted_iota(jnp.int32, t.shape, 1)
    first = (lane & (HD - 1)) < (HD // 2)
    partner = jnp.where(first, pltpu.roll(t, n - HD // 2, 1), pltpu.roll(t, HD // 2, 1))
    return t * c + partner * s


def _rope_bwd(dt, c, s):
    n = dt.shape[1]
    reps = n // LANES
    if reps > 1:
        c = jnp.tile(c, (1, reps))
        s = jnp.tile(s, (1, reps))
    lane = lax.broadcasted_iota(jnp.int32, dt.shape, 1)
    first = (lane & (HD - 1)) < (HD // 2)
    ds = dt * s
    partner = jnp.where(first, pltpu.roll(ds, n - HD // 2, 1), pltpu.roll(ds, HD // 2, 1))
    return dt * c + partner


def _acc_init(i, *refs):
    @pl.when(i == 0)
    def _():
        for r in refs:
            r[...] = jnp.zeros(r.shape, r.dtype)


def _acc_row(ref, v):
    ref[...] += jnp.broadcast_to(v, ref.shape)


def _rows(tm, n):
    return pl.BlockSpec((tm, n), lambda i: (i, 0))


def _full(shape):
    nd = len(shape)
    return pl.BlockSpec(shape, lambda i: (0,) * nd)


def _once(shape):
    nd = len(shape)
    return pl.BlockSpec(shape, lambda i: (0,) * nd, pipeline_mode=pl.Buffered(1))


def _params(vmem_mb):
    return pltpu.CompilerParams(dimension_semantics=("arbitrary",), vmem_limit_bytes=vmem_mb * 2 ** 20)


def _sds(shape, dtype):
    return pltpu.HBM(shape, dtype)


def _hbm_call(*args, **kwargs):
    call = pl.pallas_call(*args, **kwargs)
    return lambda *ops: call(*[pltpu.with_memory_space_constraint(o, pltpu.HBM) for o in ops])


def _mem_kv_fwd(mem, g_kv, wkvT):
    M = mem.shape[0]

    def body(mem_ref, g_ref, w_ref, km_ref, vm_ref, mn_ref):
        y, _ = _rms(mem_ref[...], g_ref[...])
        kv = _nt(y, w_ref[...])
        km_ref[...] = kv[:, :D].astype(km_ref.dtype)
        vm_ref[...] = kv[:, D:].astype(vm_ref.dtype)
        mn_ref[...] = y.astype(mn_ref.dtype)

    return _hbm_call(
        body, name="mem_kv_fwd", grid=(1,),
        in_specs=[_full((M, D)), _full((1, D)), _full((2 * D, D))],
        out_specs=[_full((M, D))] * 3,
        out_shape=[_sds((M, D), _MXU)] * 3,
        compiler_params=_params(40),
    )(mem, g_kv, wkvT)


def _in_proj_fwd(x, g_mix, winT, b_in, cosq, sinq, tm=512):
    S = x.shape[0]

    def body(x_ref, g_ref, w_ref, b_ref, c_ref, s_ref, h_ref, uglu_ref, vg_ref, q_ref, k_ref, v_ref):
        h, _ = _rms(x_ref[...], g_ref[...])
        hb = h.astype(h_ref.dtype)
        h_ref[...] = hb
        u = _nt(hb, w_ref[...]) + b_ref[...]
        uglu_ref[...] = u[:, :2 * CC]
        vg_ref[...] = u[:, :CC] * _sigmoid(u[:, CC:2 * CC])
        c = c_ref[...]
        s = s_ref[...]
        q_ref[...] = _rope(u[:, 2 * CC:2 * CC + ATT], c, s).astype(q_ref.dtype)
        k_ref[...] = _rope(u[:, 2 * CC + ATT:2 * CC + ATT + KVC], c, s).astype(k_ref.dtype)
        v_ref[...] = u[:, 2 * CC + ATT + KVC:].astype(v_ref.dtype)

    return _hbm_call(
        body, name="in_proj_fwd", grid=(S // tm,),
        in_specs=[_rows(tm, D), _full((1, D)), _full((INC, D)), _full((1, INC)), _rows(tm, LANES), _rows(tm, LANES)],
        out_specs=[_rows(tm, D), _rows(tm, 2 * CC), _rows(tm, CC), _rows(tm, ATT), _rows(tm, KVC), _rows(tm, KVC)],
        out_shape=[_sds((S, D), _MXU), _sds((S, 2 * CC), F32), _sds((S, CC), F32), _sds((S, ATT), _MXU),
                   _sds((S, KVC), _MXU), _sds((S, KVC), _MXU)],
        compiler_params=_params(48),
    )(x, g_mix, winT, b_in, cosq, sinq)


CONV_HALO = 16


CONV_TILE = 256


def _shift_scratch(tm, sets):
    return pltpu.VMEM((sets, CC // LANES, 8, tm + 24, LANES), F32)


def _shifted_copies(pad_ref, sh_ref, t0, tm):
    for cl in range(CC // LANES):
        blk = pad_ref[pl.ds(t0, tm + 2 * CONV_HALO), cl * LANES:(cl + 1) * LANES]
        for b in range(8):
            sh_ref[cl, b, :, :] = blk[b:b + tm + 24, :]


def _conv_taps():
    return [(8 * a + b - 1, b, 8 * a) for b in range(8) for a in range(4) if 0 <= 8 * a + b - 1 < TAPS]


CONV_ROWS = 64
SUBLANES = 8


def _dwconv_tile(sh_ref, tm, w_ref, flip, out_ref):
    for cl in range(CC // LANES):
        lanes = slice(cl * LANES, (cl + 1) * LANES)

        def chunk(c, carry):
            r = pl.multiple_of(c * CONV_ROWS, CONV_ROWS)
            accs = [None] * 4
            for n, (k, b, off) in enumerate(_conv_taps()):
                kk = TAPS - 1 - k if flip else k
                term = sh_ref[cl, b, pl.ds(off + r, CONV_ROWS), :] * w_ref[kk:kk + 1, lanes]
                accs[n % 4] = term if accs[n % 4] is None else accs[n % 4] + term
            out_ref[pl.ds(r, CONV_ROWS), lanes] = (accs[0] + accs[1]) + (accs[2] + accs[3])
            return carry

        lax.fori_loop(0, tm // CONV_ROWS, chunk, 0)


def _fill_padded(i, src_hbm, pad_ref, sem, S):
    @pl.when(i == 0)
    def _():
        zeros = jnp.zeros((CONV_HALO, pad_ref.shape[1]), pad_ref.dtype)
        pad_ref[0:CONV_HALO, :] = zeros
        pad_ref[CONV_HALO + S:2 * CONV_HALO + S, :] = zeros
        cp = pltpu.make_async_copy(src_hbm, pad_ref.at[pl.ds(CONV_HALO, S), :], sem)
        cp.start()
        cp.wait()


def _ln_silu(y, g, b):
    mu = jnp.mean(y, axis=-1, keepdims=True)
    yc = y - mu
    var = jnp.mean(yc * yc, axis=-1, keepdims=True)
    rstd = lax.rsqrt(var + EPS)
    yh = yc * rstd
    z = yh * g + b
    return yh, rstd, z


def _conv_fwd(vg, w_dw, b_dw, ln_g, ln_b):
    S = vg.shape[0]
    tm = min(CONV_TILE, S)

    def body(vg_hbm, w_ref, b_ref, g_ref, bb_ref, y_ref, yc_ref, pad_ref, sh_ref, sem):
        i = pl.program_id(0)
        _fill_padded(i, vg_hbm, pad_ref, sem, S)
        _shifted_copies(pad_ref, sh_ref.at[0], pl.multiple_of(i * tm, tm), tm)
        _dwconv_tile(sh_ref.at[0], tm, w_ref, False, y_ref)
        y = y_ref[...] + b_ref[...]
        y_ref[...] = y
        _, _, z = _ln_silu(y, g_ref[...], bb_ref[...])
        yc_ref[...] = (z * _sigmoid(z)).astype(yc_ref.dtype)

    return _hbm_call(
        body, name="conv_fwd", grid=(S // tm,),
        in_specs=[pl.BlockSpec(memory_space=pl.ANY), _full((TAPS, CC)), _full((1, CC)), _full((1, CC)), _full((1, CC))],
        out_specs=[_rows(tm, CC), _rows(tm, CC)],
        out_shape=[_sds((S, CC), F32), _sds((S, CC), _MXU)],
        scratch_shapes=[pltpu.VMEM((S + 2 * CONV_HALO, CC), F32), _shift_scratch(tm, 1), pltpu.SemaphoreType.DMA],
        compiler_params=_params(40),
    )(vg, w_dw, b_dw, ln_g, ln_b)


def _attn_window(i, S):
    start = jnp.clip((i - 1) * BLK, 0, S - 3 * BLK)
    return pl.multiple_of(start, BLK)


def _attn_mask(i, start):
    keys = lax.broadcasted_iota(jnp.int32, (3 * BLK, GRP * BLK), 0)
    cols = lax.broadcasted_iota(jnp.int32, (3 * BLK, GRP * BLK), 1)
    rel = (start + keys) - (i * BLK + (cols & (BLK - 1)))
    return jnp.abs(rel) <= BLK


def _stack_heads(t, hk):
    return jnp.concatenate([t[:, (hk * GRP + g) * HD:(hk * GRP + g + 1) * HD] for g in range(GRP)], axis=0)


def _unstack_heads_t(t):
    cols = []
    for p in range(GRP // 2):
        pair = jnp.concatenate([t[:, (2 * p) * BLK:(2 * p + 1) * BLK], t[:, (2 * p + 1) * BLK:(2 * p + 2) * BLK]],
                               axis=0)
        cols.append(pair.T)
    return jnp.concatenate(cols, axis=1)


def _attn_probs(qs, kh, sk_ref, hk, mask):
    s = _nt(kh, qs) * SCALE
    s = jnp.where(mask, s, -1e30)
    sk = jnp.concatenate(
        [jnp.broadcast_to(sk_ref[0:1, hk * GRP + g:hk * GRP + g + 1], (1, BLK)) for g in range(GRP)], axis=1)
    m = jnp.maximum(jnp.max(s, axis=0, keepdims=True), sk)
    e = jnp.exp(s - m)
    es = jnp.exp(sk - m)
    inv = 1.0 / (jnp.sum(e, axis=0, keepdims=True) + es)
    return e, inv, es


ATTN_BLOCKS = 4


def _attn_fwd(qr, kr, vv, sink):
    S = qr.shape[0]
    tq = ATTN_BLOCKS * BLK

    def body(q_ref, k_ref, v_ref, sk_ref, o_ref):
        for b in range(ATTN_BLOCKS):
            i = pl.program_id(0) * ATTN_BLOCKS + b
            rows = slice(b * BLK, (b + 1) * BLK)
            start = _attn_window(i, S)
            kb = k_ref[pl.ds(start, 3 * BLK), :]
            vb = v_ref[pl.ds(start, 3 * BLK), :]
            q = q_ref[rows, :]
            mask = _attn_mask(i, start)
            outs = []
            for hk in range(NKV):
                hs = slice(hk * HD, (hk + 1) * HD)
                e, inv, _ = _attn_probs(_stack_heads(q, hk), kb[:, hs], sk_ref, hk, mask)
                outs.append(_unstack_heads_t(_tn(vb[:, hs], e) * inv))
            o_ref[rows, :] = jnp.concatenate(outs, axis=1).astype(o_ref.dtype)

    return _hbm_call(
        body, name="attn_fwd", grid=(S // tq,),
        in_specs=[_rows(tq, ATT), _full((S, KVC)), _full((S, KVC)), _full((1, NQ))],
        out_specs=_rows(tq, ATT),
        out_shape=_sds((S, ATT), _MXU),
        compiler_params=_params(40),
    )(qr, kr, vv, sink)


def _out_proj_fwd(x, yc, ya, wout, b_out, tm=512):
    S = x.shape[0]

    def body(x_ref, yc_ref, ya_ref, w_ref, b_ref, x1_ref, ymix_ref):
        ymix = jnp.concatenate([yc_ref[...], ya_ref[...]], axis=1)
        ymix_ref[...] = ymix
        x1_ref[...] = x_ref[...] + _nn(ymix, w_ref[...]) + b_ref[...]

    return _hbm_call(
        body, name="out_proj_fwd", grid=(S // tm,),
        in_specs=[_rows(tm, D), _rows(tm, CC), _rows(tm, ATT), _full((D, D)), _full((1, D))],
        out_specs=[_rows(tm, D), _rows(tm, D)],
        out_shape=[_sds((S, D), F32), _sds((S, D), _MXU)],
        compiler_params=_params(40),
    )(x, yc, ya, wout, b_out)


def _softmax_t(s):
    m = jnp.max(s, axis=0, keepdims=True)
    e = jnp.exp(s - m)
    return e, 1.0 / jnp.sum(e, axis=0, keepdims=True)


def _softmax(s):
    m = jnp.max(s, axis=-1, keepdims=True)
    e = jnp.exp(s - m)
    return e * (1.0 / jnp.sum(e, axis=-1, keepdims=True))


def _mem_attn_fwd(x1, g_q, wq, km, vm, wo, after, tm=512):
    S = x1.shape[0]
    M = km.shape[0]

    def body(x1_ref, g_ref, wq_ref, km_ref, vm_ref, wo_ref, after_ref, x2_ref, qm_ref, om_ref):
        x1v = x1_ref[...]
        hq, _ = _rms(x1v, g_ref[...])
        qm = _nn(hq, wq_ref[...]).astype(qm_ref.dtype)
        qm_ref[...] = qm
        oms = []
        for h in range(MH):
            sl = slice(h * MHD, (h + 1) * MHD)
            p = _softmax(_nt(qm[:, sl], km_ref[:, sl]) * MSCALE)
            oms.append(_nn(p, vm_ref[:, sl]))
        om = jnp.concatenate(oms, axis=1).astype(om_ref.dtype)
        om_ref[...] = om
        x2_ref[...] = x1v + _nn(om, wo_ref[...])

    return _hbm_call(
        body, name="mem_attn_fwd", grid=(S // tm,),
        in_specs=[_rows(tm, D), _full((1, D)), _full((D, D)), _full((M, D)), _full((M, D)), _full((D, D)), _AFTER],
        out_specs=[_rows(tm, D)] * 3,
        out_shape=[_sds((S, D), F32), _sds((S, D), _MXU), _sds((S, D), _MXU)],
        compiler_params=_params(48),
    )(x1, g_q, wq, km, vm, wo, after)


def _ffn_loss_fwd(x2, tgt, g_ffn, g_final, wgT, wuT, wd, tm=512):
    S = x2.shape[0]

    def body(x2_ref, t_ref, gf_ref, gl_ref, wg_ref, wu_ref, wd_ref,
             gt_ref, up_ref, hf_ref, dx3_ref, dx3b_ref, loss_ref, dgl_ref):
        i = pl.program_id(0)
        _acc_init(i, loss_ref, dgl_ref)
        x2v = x2_ref[...]
        hf, _ = _rms(x2v, gf_ref[...])
        hb = hf.astype(hf_ref.dtype)
        hf_ref[...] = hb
        x3 = x2v
        for c in range(F // FCH):
            sl = slice(c * FCH, (c + 1) * FCH)
            gt = _nt(hb, wg_ref[sl, :])
            up = _nt(hb, wu_ref[sl, :])
            gt_ref[:, sl] = gt.astype(gt_ref.dtype)
            up_ref[:, sl] = up.astype(up_ref.dtype)
            x3 = x3 + _nn(gt * _sigmoid(gt) * up, wd_ref[sl, :])
        gl = gl_ref[...]
        y, r = _rms(x3, gl)
        diff = y - t_ref[...]
        _acc_row(loss_ref, jnp.sum(jnp.sum(diff * diff, axis=-1, keepdims=True), axis=0, keepdims=True) * (0.5 / D))
        dx3, dgl = _rms_bwd(diff * (1.0 / D), x3, r, gl)
        dx3_ref[...] = dx3
        dx3b_ref[...] = dx3.astype(dx3b_ref.dtype)
        _acc_row(dgl_ref, dgl)

    return _hbm_call(
        body, name="ffn_loss_fwd", grid=(S // tm,),
        in_specs=[_rows(tm, D), _rows(tm, D), _full((1, D)), _full((1, D)),
                  _once((F, D)), _once((F, D)), _once((F, D))],
        out_specs=[_rows(tm, F), _rows(tm, F), _rows(tm, D), _rows(tm, D), _rows(tm, D),
                   _full((8, LANES)), _full((8, D))],
        out_shape=[_sds((S, F), _MXU), _sds((S, F), _MXU), _sds((S, D), _MXU), _sds((S, D), F32),
                   _sds((S, D), _MXU), _sds((8, LANES), F32), _sds((8, D), F32)],
        compiler_params=_params(56),
    )(x2, tgt, g_ffn, g_final, wgT, wuT, wd)


def _ffn_bwd(dx3, x2, gt, up, g_ffn, wd, wgT, wuT, tm=256):
    S = x2.shape[0]

    def body(dx3_ref, x2_ref, gt_ref, up_ref, g_ref, wd_ref, wg_ref, wu_ref,
             dx2_ref, dx2b_ref, dgt_ref, dup_ref, act_ref, dg_ref):
        i = pl.program_id(0)
        _acc_init(i, dg_ref)
        dx3v = dx3_ref[...]
        db = dx3v.astype(_MXU)
        dhf = jnp.zeros((tm, D), F32)
        for c in range(F // FCH_BWD):
            sl = slice(c * FCH_BWD, (c + 1) * FCH_BWD)
            gtv = gt_ref[:, sl].astype(F32)
            upv = up_ref[:, sl].astype(F32)
            dact = _nt(db, wd_ref[sl, :])
            sig = _sigmoid(gtv)
            silu = gtv * sig
            dup = (dact * silu).astype(dup_ref.dtype)
            dgt = (dact * upv * (sig * (1.0 + gtv * (1.0 - sig)))).astype(dgt_ref.dtype)
            act_ref[:, sl] = (silu * upv).astype(act_ref.dtype)
            dgt_ref[:, sl] = dgt
            dup_ref[:, sl] = dup
            dhf = dhf + _nn(dgt, wg_ref[sl, :]) + _nn(dup, wu_ref[sl, :])
        x2v = x2_ref[...]
        g = g_ref[...]
        _, r = _rms(x2v, g)
        dxn, dg = _rms_bwd(dhf, x2v, r, g)
        dx2 = dx3v + dxn
        dx2_ref[...] = dx2
        dx2b_ref[...] = dx2.astype(dx2b_ref.dtype)
        _acc_row(dg_ref, dg)

    return _hbm_call(
        body, name="ffn_bwd", grid=(S // tm,),
        in_specs=[_rows(tm, D), _rows(tm, D), _rows(tm, F), _rows(tm, F), _full((1, D)),
                  _once((F, D)), _once((F, D)), _once((F, D))],
        out_specs=[_rows(tm, D), _rows(tm, D), _rows(tm, F), _rows(tm, F), _rows(tm, F), _full((8, D))],
        out_shape=[_sds((S, D), F32), _sds((S, D), _MXU), _sds((S, F), _MXU), _sds((S, F), _MXU),
                   _sds((S, F), _MXU), _sds((8, D), F32)],
        compiler_params=_params(56),
    )(dx3, x2, gt, up, g_ffn, wd, wgT, wuT)


_AFTER = pl.BlockSpec(memory_space=pl.ANY)


def _wgrad(pairs, name, after, rc=256):
    S, R = pairs[0][0].shape
    N = pairs[0][1].shape[1]
    rights = []
    for _, b in pairs:
        if not any(b is r for r in rights):
            rights.append(b)
    which = [next(i for i, r in enumerate(rights) if r is b) for _, b in pairs]
    n, nb = len(pairs), len(rights)

    def body(*refs):
        for k in range(n):
            refs[n + nb + 1 + k][...] = _tn(refs[k][...], refs[n + which[k]][...]).astype(jnp.bfloat16)

    res = _hbm_call(
        body, name=name, grid=(R // rc,),
        in_specs=[pl.BlockSpec((S, rc), lambda j: (0, j))] * n + [_once((S, N))] * nb + [_AFTER],
        out_specs=[_rows(rc, N)] * n,
        out_shape=[_sds((R, N), jnp.bfloat16)] * n,
        compiler_params=_params(56),
    )(*[a for a, _ in pairs], *rights, after)
    return list(res)


def _mem_attn_bwd(dx2, x1, qm, km, vm, g_q, wq, wo, wout, after, tm=512):
    S = x1.shape[0]
    M = km.shape[0]

    def body(dx2_ref, x1_ref, qm_ref, km_ref, vm_ref, g_ref, wq_ref, wo_ref, wout_ref, after_ref,
             dx1_ref, dx1b_ref, hq_ref, dqm_ref, dyc_ref, dya_ref, dkm_ref, dvm_ref, dg_ref, dbo_ref):
        i = pl.program_id(0)
        _acc_init(i, dkm_ref, dvm_ref, dg_ref, dbo_ref)
        dx2v = dx2_ref[...]
        domt = _nt(wo_ref[...], dx2v).astype(_MXU)
        dqs = []
        for h in range(MH):
            sl = slice(h * MHD, (h + 1) * MHD)
            qh = qm_ref[:, sl]
            kh = km_ref[:, sl]
            e, inv = _softmax_t(_nt(kh, qh) * MSCALE)
            p = e * inv
            dp = _nn(vm_ref[:, sl], domt[sl, :])
            ds = (p * (dp - jnp.sum(dp * p, axis=0, keepdims=True))).astype(_MXU)
            dqs.append(_tn(ds, kh) * MSCALE)
            dkm_ref[:, sl] += _nn(ds, qh) * MSCALE
            dvm_ref[:, sl] += _nt(p, domt[sl, :])
        dqm = jnp.concatenate(dqs, axis=1).astype(dqm_ref.dtype)
        dqm_ref[...] = dqm
        dhq = _nt(dqm, wq_ref[...])
        x1v = x1_ref[...]
        g = g_ref[...]
        hq, r = _rms(x1v, g)
        hq_ref[...] = hq.astype(hq_ref.dtype)
        dxn, dg = _rms_bwd(dhq, x1v, r, g)
        dx1 = dx2v + dxn
        dx1_ref[...] = dx1
        dx1b = dx1.astype(dx1b_ref.dtype)
        dx1b_ref[...] = dx1b
        dym = _nt(dx1b, wout_ref[...])
        dyc_ref[...] = dym[:, :CC]
        dya_ref[...] = dym[:, CC:].astype(dya_ref.dtype)
        _acc_row(dg_ref, dg)
        _acc_row(dbo_ref, jnp.sum(dx1, axis=0, keepdims=True))

    return _hbm_call(
        body, name="mem_attn_bwd", grid=(S // tm,),
        in_specs=[_rows(tm, D), _rows(tm, D), _rows(tm, D), _full((M, D)), _full((M, D)), _full((1, D)),
                  _once((D, D)), _once((D, D)), _once((D, D)), _AFTER],
        out_specs=[_rows(tm, D), _rows(tm, D), _rows(tm, D), _rows(tm, D), _rows(tm, CC), _rows(tm, ATT),
                   _full((M, D)), _full((M, D)), _full((8, D)), _full((8, D))],
        out_shape=[_sds((S, D), F32), _sds((S, D), _MXU), _sds((S, D), _MXU), _sds((S, D), _MXU),
                   _sds((S, CC), F32), _sds((S, ATT), _MXU),
                   _sds((M, D), F32), _sds((M, D), F32), _sds((8, D), F32), _sds((8, D), F32)],
        compiler_params=_params(56),
    )(dx2, x1, qm, km, vm, g_q, wq, wo, wout, after)


def _mem_kv_bwd(dkm, dvm, mem, g_kv, memn, wkvT):
    M = mem.shape[0]

    def body(dkm_ref, dvm_ref, mem_ref, g_ref, mn_ref, w_ref, gw_ref, dg_ref):
        dkv = jnp.concatenate([dkm_ref[...], dvm_ref[...]], axis=1).astype(_MXU)
        gw_ref[...] = _tn(dkv, mn_ref[...]).astype(gw_ref.dtype)
        dmn = _nn(dkv, w_ref[...])
        memv = mem_ref[...]
        g = g_ref[...]
        _, r = _rms(memv, g)
        _, dg = _rms_bwd(dmn, memv, r, g)
        dg_ref[...] = jnp.broadcast_to(dg, dg_ref.shape)

    return _hbm_call(
        body, name="mem_kv_bwd", grid=(1,),
        in_specs=[_full((M, D)), _full((M, D)), _full((M, D)), _full((1, D)), _full((M, D)), _full((2 * D, D))],
        out_specs=[_full((2 * D, D)), _full((8, D))],
        out_shape=[_sds((2 * D, D), jnp.bfloat16), _sds((8, D), F32)],
        compiler_params=_params(48),
    )(dkm, dvm, mem, g_kv, memn, wkvT)


def _attn_bwd(qr, kr, vv, sink, dya, after):
    S = qr.shape[0]
    tq = ATTN_BLOCKS * BLK

    def body(q_ref, k_ref, v_ref, sk_ref, do_ref, after_ref, dq_ref, dk_ref, dv_ref, dsk_ref, sink_acc):
        _acc_init(pl.program_id(0), dk_ref, dv_ref, sink_acc)
        for b in range(ATTN_BLOCKS):
            i = pl.program_id(0) * ATTN_BLOCKS + b
            rows = slice(b * BLK, (b + 1) * BLK)
            start = _attn_window(i, S)
            kb = k_ref[pl.ds(start, 3 * BLK), :]
            vb = v_ref[pl.ds(start, 3 * BLK), :]
            q = q_ref[rows, :]
            do = do_ref[rows, :]
            mask = _attn_mask(i, start)
            dqs, dks, dvs = [], [], []
            for hk in range(NKV):
                hs = slice(hk * HD, (hk + 1) * HD)
                qs = _stack_heads(q, hk)
                dos = _stack_heads(do, hk)
                e, inv, es = _attn_probs(qs, kb[:, hs], sk_ref, hk, mask)
                p = e * inv
                dp = _nt(vb[:, hs], dos)
                delta = jnp.sum(dp * p, axis=0, keepdims=True)
                ds = (p * (dp - delta)).astype(_MXU)
                dqs.append(_unstack_heads_t(_tn(kb[:, hs], ds) * SCALE))
                dks.append(_nn(ds, qs) * SCALE)
                dvs.append(_nn(p, dos))
                sink_acc[hk:hk + 1, :] -= es * inv * delta
            dq_ref[rows, :] = jnp.concatenate(dqs, axis=1)
            dk_ref[pl.ds(start, 3 * BLK), :] += jnp.concatenate(dks, axis=1)
            dv_ref[pl.ds(start, 3 * BLK), :] += jnp.concatenate(dvs, axis=1)

        @pl.when(pl.program_id(0) == pl.num_programs(0) - 1)
        def _():
            for h in range(NQ):
                tot = jnp.sum(sink_acc[h // GRP:h // GRP + 1, (h % GRP) * BLK:(h % GRP + 1) * BLK], axis=1,
                              keepdims=True)
                dsk_ref[h:h + 1, :] = jnp.broadcast_to(tot, (1, LANES))

    return _hbm_call(
        body, name="attn_bwd", grid=(S // tq,),
        in_specs=[_rows(tq, ATT), _full((S, KVC)), _full((S, KVC)), _full((1, NQ)), _rows(tq, ATT), _AFTER],
        out_specs=[_rows(tq, ATT), _full((S, KVC)), _full((S, KVC)), _full((8, LANES))],
        out_shape=[_sds((S, ATT), F32), _sds((S, KVC), F32), _sds((S, KVC), F32), _sds((8, LANES), F32)],
        scratch_shapes=[pltpu.VMEM((8, GRP * BLK), F32)],
        compiler_params=_params(40),
    )(qr, kr, vv, sink, dya, after)


def _conv_norm_bwd(dyc, y, ln_g, ln_b, after, tm=512):
    S = y.shape[0]

    def body(dyc_ref, y_ref, g_ref, b_ref, after_ref, dy_ref, dg_ref, db_ref, dbdw_ref):
        i = pl.program_id(0)
        _acc_init(i, dg_ref, db_ref, dbdw_ref)
        g = g_ref[...]
        yh, rstd, z = _ln_silu(y_ref[...], g, b_ref[...])
        sig = _sigmoid(z)
        dz = dyc_ref[...] * (sig * (1.0 + z * (1.0 - sig)))
        dyh = dz * g
        dy = rstd * (dyh - jnp.mean(dyh, axis=-1, keepdims=True)
                     - yh * jnp.mean(dyh * yh, axis=-1, keepdims=True))
        dy_ref[...] = dy
        _acc_row(dg_ref, jnp.sum(dz * yh, axis=0, keepdims=True))
        _acc_row(db_ref, jnp.sum(dz, axis=0, keepdims=True))
        _acc_row(dbdw_ref, jnp.sum(dy, axis=0, keepdims=True))

    return _hbm_call(
        body, name="conv_norm_bwd", grid=(S // tm,),
        in_specs=[_rows(tm, CC), _rows(tm, CC), _full((1, CC)), _full((1, CC)), _AFTER],
        out_specs=[_rows(tm, CC), _full((8, CC)), _full((8, CC)), _full((8, CC))],
        out_shape=[_sds((S, CC), F32), _sds((8, CC), F32), _sds((8, CC), F32), _sds((8, CC), F32)],
        compiler_params=_params(40),
    )(dyc, y, ln_g, ln_b, after)


def _conv_bwd(dy, vg, uglu, w_dw):
    S = dy.shape[0]
    tm = min(CONV_TILE, S)

    def body(dy_hbm, vg_hbm, uglu_ref, w_ref, du_ref, gw_ref, dbin_ref, dyp_ref, vgp_ref, gacc_ref, sh_ref, dvg_ref,
             sems):
        i = pl.program_id(0)
        _acc_init(i, gacc_ref, dbin_ref)
        _fill_padded(i, dy_hbm, dyp_ref, sems.at[0], S)
        _fill_padded(i, vg_hbm, vgp_ref, sems.at[1], S)
        t0 = pl.multiple_of(i * tm, tm)
        _shifted_copies(dyp_ref, sh_ref.at[0], t0, tm)
        _shifted_copies(vgp_ref, sh_ref.at[1], t0, tm)
        _dwconv_tile(sh_ref.at[0], tm, w_ref, True, dvg_ref)
        dvg = dvg_ref[...]
        sub = CONV_ROWS
        for cl in range(CC // LANES):
            lanes = slice(cl * LANES, (cl + 1) * LANES)
            for k, b, off in _conv_taps():
                part = jnp.zeros((8, LANES), F32)
                for r in range(0, tm, sub):
                    prod = (sh_ref[0, cl, 0, CONV_HALO + r:CONV_HALO + r + sub, :]
                            * sh_ref[1, cl, b, off + r:off + r + sub, :])
                    part = part + jnp.sum(prod.reshape(sub // 8, 8, LANES), axis=0)
                gacc_ref[k, :, lanes] += part
        a_v = uglu_ref[:, :CC]
        sg = _sigmoid(uglu_ref[:, CC:])
        da = dvg * sg
        dgate = dvg * a_v * sg * (1.0 - sg)
        du = jnp.concatenate([da, dgate], axis=1)
        du_ref[...] = du.astype(du_ref.dtype)
        _acc_row(dbin_ref, jnp.sum(du, axis=0, keepdims=True))

        @pl.when(i == pl.num_programs(0) - 1)
        def _():
            gw_ref[...] = jnp.zeros(gw_ref.shape, F32)
            for k in range(TAPS):
                gw_ref[k:k + 1, :] = jnp.sum(gacc_ref[k], axis=0, keepdims=True)

    return _hbm_call(
        body, name="conv_bwd", grid=(S // tm,),
        in_specs=[pl.BlockSpec(memory_space=pl.ANY), pl.BlockSpec(memory_space=pl.ANY), _rows(tm, 2 * CC),
                  _full((TAPS, CC))],
        out_specs=[_rows(tm, 2 * CC), _full((TAPS + 1, CC)), _full((8, 2 * CC))],
        out_shape=[_sds((S, 2 * CC), _MXU), _sds((TAPS + 1, CC), F32), _sds((8, 2 * CC), F32)],
        scratch_shapes=[pltpu.VMEM((S + 2 * CONV_HALO, CC), F32), pltpu.VMEM((S + 2 * CONV_HALO, CC), F32),
                        pltpu.VMEM((TAPS, 8, CC), F32), _shift_scratch(tm, 2), pltpu.VMEM((tm, CC), F32),
                        pltpu.SemaphoreType.DMA((2,))],
        compiler_params=_params(56),
    )(dy, vg, uglu, w_dw)


def _in_proj_bwd(duglu, dqr, dkr, dv, cosq, sinq, x, dx1, g_mix, winT, tm=512):
    S = x.shape[0]
    NA = ATT + 2 * KVC

    def body(dug_ref, dq_ref, dk_ref, dv_ref, c_ref, s_ref, x_ref, dx1_ref, g_ref, w_ref,
             dx_ref, du_ref, dg_ref, dbin_ref):
        i = pl.program_id(0)
        _acc_init(i, dg_ref, dbin_ref)
        c = c_ref[...]
        s = s_ref[...]
        datt = jnp.concatenate([_rope_bwd(dq_ref[...], c, s), _rope_bwd(dk_ref[...], c, s), dv_ref[...]], axis=1)
        _acc_row(dbin_ref, jnp.sum(datt, axis=0, keepdims=True))
        du = jnp.concatenate([dug_ref[...], datt.astype(du_ref.dtype)], axis=1)
        du_ref[...] = du
        dh = _nn(du, w_ref[...])
        xv = x_ref[...]
        g = g_ref[...]
        _, r = _rms(xv, g)
        dxn, dg = _rms_bwd(dh, xv, r, g)
        dx_ref[...] = dx1_ref[...] + dxn
        _acc_row(dg_ref, dg)

    return _hbm_call(
        body, name="in_proj_bwd", grid=(S // tm,),
        in_specs=[_rows(tm, 2 * CC), _rows(tm, ATT), _rows(tm, KVC), _rows(tm, KVC), _rows(tm, LANES),
                  _rows(tm, LANES), _rows(tm, D), _rows(tm, D), _full((1, D)), _once((INC, D))],
        out_specs=[_rows(tm, D), _rows(tm, INC), _full((8, D)), _full((8, NA))],
        out_shape=[_sds((S, D), F32), _sds((S, INC), _MXU), _sds((8, D), F32), _sds((8, NA), F32)],
        compiler_params=_params(48),
    )(duglu, dqr, dkr, dv, cosq, sinq, x, dx1, g_mix, winT)


def _rope_tables(S):
    pos = jnp.arange(S, dtype=F32)
    inv_freq = ROPE_THETA ** (-jnp.arange(0, HD, 2, dtype=F32) / HD)
    ang = pos[:, None] * inv_freq[None, :]
    cos, sin = jnp.cos(ang), jnp.sin(ang)
    cosq = jnp.tile(jnp.concatenate([cos, cos], axis=1), (1, LANES // HD))
    sinq = jnp.tile(jnp.concatenate([-sin, sin], axis=1), (1, LANES // HD))
    return cosq, sinq


GATHER_GROUPS = (("w_inT", "w_dw"), ("w_out", "w_mem_kvT", "w_mem_q", "w_mem_o"), ("w_gateT", "w_upT"), ("w_down",))
ID_GATHER, ID_SIBLING, ID_CHIP = 1, 5, 8


def _local_step(x, mem, tgt, sp, gather, reduce):
    S = x.shape[0]
    cosq, sinq = _rope_tables(S)

    gather.forward(0, cosq)
    w0 = gather.finish(0, sinq)
    winT, w_dw_full = w0["w_inT"], w0["w_dw"]
    h, uglu, vg, qr, kr, vv = _in_proj_fwd(x, sp["g_mix"], winT, sp["b_in"], cosq, sinq)
    ya = _attn_fwd(qr, kr, vv, sp["attn_sink"])
    gather.forward(1, ya)
    y, yc = _conv_fwd(vg, w_dw_full, sp["b_dw"], sp["g_conv_ln"], sp["b_conv_ln"])
    w1 = gather.finish(1, yc)
    wout, wkvT, wq, wo = w1["w_out"], w1["w_mem_kvT"], w1["w_mem_q"], w1["w_mem_o"]
    km, vm, memn = _mem_kv_fwd(mem, sp["g_mem_kv"], wkvT)
    x1, ymix = _out_proj_fwd(x, yc, ya, wout, sp["b_out"])
    token = gather.forward(2, x1)
    x2, qm, om = _mem_attn_fwd(x1, sp["g_mem_q"], wq, km, vm, wo, token)
    w2 = gather.finish(2, om)
    gather.forward(3, om)
    wgT, wuT, wd = w2["w_gateT"], w2["w_upT"], gather.finish(3, om)["w_down"]
    gt, up, hf, dx3, dx3b, loss_acc, dg_final = _ffn_loss_fwd(x2, tgt, sp["g_ffn"], sp["g_final"], wgT, wuT, wd)

    dx2, dx2b, dgt, dup, act, dg_ffn = _ffn_bwd(dx3, x2, gt, up, sp["g_ffn"], wd, wgT, wuT)
    token = reduce.start(0, _wgrad([(dgt, hf), (dup, hf), (act, dx3b)], "wgrad_ffn", dx2))

    dx1, dx1b, hq, dqm, dyc, dya, dkm, dvm, dg_mem_q, db_out = _mem_attn_bwd(
        dx2, x1, qm, km, vm, sp["g_mem_q"], wq, wo, wout, token)
    token = reduce.middle(0, dx1)
    g_wout, g_wq, g_wo = _wgrad([(ymix, dx1b), (hq, dqm), (om, dx2b)], "wgrad_mix_mem", token)
    g_wkvT, dg_mem_kv = _mem_kv_bwd(dkm, dvm, mem, sp["g_mem_kv"], memn, wkvT)
    token = reduce.start(1, [g_wout, g_wq, g_wkvT, g_wo])

    dqr, dkr, dvv, dsink = _attn_bwd(qr, kr, vv, sp["attn_sink"], dya, token)
    token = reduce.middle(1, dqr)
    dy, dg_cln, db_cln, db_dw = _conv_norm_bwd(dyc, y, sp["g_conv_ln"], sp["b_conv_ln"], token)
    duglu, g_wdw, dbin_glu = _conv_bwd(dy, vg, uglu, w_dw_full)
    grad_x, du, dg_mix, dbin_att = _in_proj_bwd(duglu, dqr, dkr, dvv, cosq, sinq, x, dx1, sp["g_mix"], winT)
    token = reduce.start(2, _wgrad([(du, h)], "wgrad_in", grad_x))

    small = {
        "loss": loss_acc[0:1, :],
        "g_mix": dg_mix[0:1], "b_in": jnp.concatenate([dbin_glu[0:1], dbin_att[0:1]], axis=1),
        "b_dw": db_dw[0:1], "g_conv_ln": dg_cln[0:1], "b_conv_ln": db_cln[0:1],
        "attn_sink": dsink[:, 0].reshape(1, NQ), "b_out": db_out[0:1], "g_mem_q": dg_mem_q[0:1],
        "g_mem_kv": dg_mem_kv[0:1], "g_ffn": dg_ffn[0:1], "g_final": dg_final[0:1],
    }
    token = reduce.middle(2, token, reduce.small_zones(small, g_wdw))
    reduce.finish(0, token)
    reduce.finish(1, token)
    return grad_x


def _place():
    return lax.axis_index("x"), lax.axis_index("y"), lax.axis_index("c")


def _hbm_specs(n):
    return [pl.BlockSpec(memory_space=pl.ANY)] * n


_HBM = pl.BlockSpec(memory_space=pltpu.HBM)
_SEM = pl.BlockSpec(memory_space=pltpu.SEMAPHORE)
_ANY = pl.BlockSpec(memory_space=pl.ANY)
_EFFECT = pltpu.SideEffectType.DATAFLOW_SIDE_EFFECTING
_VMEM = pl.BlockSpec(memory_space=pltpu.VMEM)
_TOKEN = jax.ShapeDtypeStruct((8, LANES), F32)


def _dev_index(p):
    return 4 * p[0] + 2 * p[1] + p[2]


def _gather_start(groups):
    sizes = [len(g) for g in groups]
    flat = [z for g in groups for z in g]
    n = len(flat)

    def body(*refs):
        ins = refs[:n]
        sems = refs[2 * n:]
        x, y, c = _place()
        me = _dev_index((x, y, c))
        peers = [(x, y, 1 - c), (1 - x, y, c), (x, 1 - y, c), (1 - x, 1 - y, c)]
        base = 0
        for gi, size in enumerate(sizes):
            send_sems, recv_sems = sems[2 * gi], sems[2 * gi + 1]
            for k in range(size):
                zone = ins[base + k]
                for j, peer in enumerate(peers):
                    pltpu.make_async_remote_copy(
                        src_ref=zone.at[me], dst_ref=zone.at[me], send_sem=send_sems.at[4 * k + j],
                        recv_sem=recv_sems.at[4 * k + j], device_id=peer, device_id_type=MESH).start()
            base += size

    out_shape = [pltpu.HBM(z.shape, z.dtype) for z in flat]
    out_specs = [_HBM] * n
    for size in sizes:
        out_shape += [pltpu.SemaphoreType.DMA((4 * size,))] * 2
        out_specs += [_SEM, _SEM]
    res = pl.pallas_call(
        body, name="gather_start", in_specs=[_HBM] * n, out_specs=out_specs, out_shape=out_shape,
        input_output_aliases={k: k for k in range(n)},
        compiler_params=pltpu.CompilerParams(has_side_effects=_EFFECT),
    )(*[pltpu.with_memory_space_constraint(z, pltpu.HBM) for z in flat])
    out, base = [], 0
    for gi, size in enumerate(sizes):
        out.append((list(res[base:base + size]), res[n + 2 * gi], res[n + 2 * gi + 1]))
        base += size
    return out


def _gather_forward(zones, send_sems, recv_sems, after, collective_id, name):
    n = len(zones)

    def body(*refs):
        ins = refs[:n]
        send_ref, recv_ref = refs[n], refs[n + 1]
        fsend, frecv = refs[2 * n + 3], refs[2 * n + 4]
        x, y, c = _place()
        sibling = (x, y, 1 - c)
        barrier = pltpu.get_barrier_semaphore()
        pl.semaphore_signal(barrier, inc=1, device_id=sibling, device_id_type=MESH)
        pl.semaphore_wait(barrier, 1)
        chips = [(1 - x, y), (x, 1 - y), (1 - x, 1 - y)]
        for j, chip in enumerate(chips):
            for k in range(n):
                slot = ins[k].at[_dev_index((*chip, c))]
                pltpu.make_async_remote_copy(
                    src_ref=slot, dst_ref=slot, send_sem=send_ref.at[4 * k + 1 + j],
                    recv_sem=recv_ref.at[4 * k + 1 + j], device_id=sibling, device_id_type=MESH).wait_recv()
                pltpu.make_async_remote_copy(
                    src_ref=slot, dst_ref=slot, send_sem=fsend.at[3 * k + j], recv_sem=frecv.at[3 * k + j],
                    device_id=sibling, device_id_type=MESH).start()
        for k in range(n):
            own = ins[k].at[_dev_index((x, y, c))]
            theirs = ins[k].at[_dev_index(sibling)]
            pltpu.make_async_remote_copy(
                src_ref=theirs, dst_ref=theirs, send_sem=send_ref.at[4 * k], recv_sem=recv_ref.at[4 * k],
                device_id=sibling, device_id_type=MESH).wait_recv()
            for j in range(4):
                pltpu.make_async_remote_copy(
                    src_ref=own, dst_ref=own, send_sem=send_ref.at[4 * k + j], recv_sem=recv_ref.at[4 * k + j],
                    device_id=sibling, device_id_type=MESH).wait_send()
        refs[2 * n + 5][...] = jnp.zeros(_TOKEN.shape, _TOKEN.dtype)

    res = pl.pallas_call(
        body, name=name, in_specs=[_HBM] * n + [_SEM, _SEM, _ANY],
        out_specs=[_HBM] * n + [_SEM, _SEM, _VMEM],
        out_shape=[pltpu.HBM(z.shape, z.dtype) for z in zones] + [pltpu.SemaphoreType.DMA((3 * n,))] * 2 + [_TOKEN],
        input_output_aliases={k: k for k in range(n)},
        compiler_params=pltpu.CompilerParams(has_side_effects=_EFFECT, collective_id=collective_id),
    )(*zones, send_sems, recv_sems, after)
    return (list(res[:n]), res[n], res[n + 1]), res[n + 2]


def _gather_finish(zones, fsend, frecv, after, name):
    n = len(zones)

    def body(*refs):
        ins = refs[:n]
        fsend_ref, frecv_ref = refs[n], refs[n + 1]
        x, y, c = _place()
        sibling = (x, y, 1 - c)
        chips = [(1 - x, y), (x, 1 - y), (1 - x, 1 - y)]
        for j, chip in enumerate(chips):
            for k in range(n):
                mine = ins[k].at[_dev_index((*chip, c))]
                theirs = ins[k].at[_dev_index((*chip, 1 - c))]
                pltpu.make_async_remote_copy(
                    src_ref=theirs, dst_ref=theirs, send_sem=fsend_ref.at[3 * k + j],
                    recv_sem=frecv_ref.at[3 * k + j], device_id=sibling, device_id_type=MESH).wait_recv()
                pltpu.make_async_remote_copy(
                    src_ref=mine, dst_ref=mine, send_sem=fsend_ref.at[3 * k + j],
                    recv_sem=frecv_ref.at[3 * k + j], device_id=sibling, device_id_type=MESH).wait_send()

    res = pl.pallas_call(
        body, name=name, in_specs=[_HBM] * n + [_SEM, _SEM, _ANY], out_specs=[_HBM] * n,
        out_shape=[pltpu.HBM(z.shape, z.dtype) for z in zones],
        input_output_aliases={k: k for k in range(n)},
        compiler_params=pltpu.CompilerParams(has_side_effects=_EFFECT),
    )(*zones, fsend, frecv, after)
    return list(res)


class _Gather:
    def __init__(self, started):
        self._phase = list(started)

    def forward(self, gi, after):
        zones, send_sems, recv_sems = self._phase[gi]
        self._phase[gi], token = _gather_forward(zones, send_sems, recv_sems, after, ID_GATHER + gi,
                                                  "gather_forward_%d" % gi)
        return token

    def finish(self, gi, after):
        zones, fsend, frecv = self._phase[gi]
        out = {}
        for name, z in zip(GATHER_GROUPS[gi], _gather_finish(zones, fsend, frecv, after, "gather_finish_%d" % gi)):
            if name == "w_dw":
                out[name] = jnp.transpose(z, (1, 0, 2)).reshape(TAPS + 1, CC)[:TAPS]
            else:
                out[name] = z.reshape(NDEV * z.shape[1], D)
        return out


def _handshake(peers):
    barrier = pltpu.get_barrier_semaphore()
    for peer in peers:
        pl.semaphore_signal(barrier, inc=1, device_id=peer, device_id_type=MESH)
    pl.semaphore_wait(barrier, len(peers))


def _sibling_copies(srcs, lands, send_sems, recv_sems):
    x, y, c = _place()
    return [pltpu.make_async_remote_copy(
        src_ref=srcs[k].at[2 * s + (1 - c)], dst_ref=lands[k].at[s], send_sem=send_sems.at[4 * k + s],
        recv_sem=recv_sems.at[4 * k + s], device_id=(x, y, 1 - c), device_id_type=MESH)
        for k in range(len(srcs)) for s in range(4)]


def _sibling_start(gbs, collective_id, name):
    n = len(gbs)
    lands = [lax.empty((4,) + g.shape[1:], g.dtype) for g in gbs]

    def body(*refs):
        x, y, c = _place()
        _handshake([(x, y, 1 - c)])
        for cp in _sibling_copies(refs[:n], refs[n:2 * n], refs[4 * n], refs[4 * n + 1]):
            cp.start()
        refs[4 * n + 2][...] = jnp.zeros(_TOKEN.shape, _TOKEN.dtype)

    res = pl.pallas_call(
        body, name=name, in_specs=[_HBM] * (2 * n), out_specs=[_HBM] * (2 * n) + [_SEM, _SEM, _VMEM],
        out_shape=[pltpu.HBM(a.shape, a.dtype) for a in gbs + lands] + [pltpu.SemaphoreType.DMA((4 * n,))] * 2
        + [_TOKEN],
        input_output_aliases={k: k for k in range(2 * n)},
        compiler_params=pltpu.CompilerParams(has_side_effects=_EFFECT, collective_id=collective_id),
    )(*[pltpu.with_memory_space_constraint(a, pltpu.HBM) for a in gbs + lands])
    return list(res[:n]), list(res[n:2 * n]), res[2 * n], res[2 * n + 1], res[2 * n + 2]


def _sibling_wait(gbs, lands, send_sems, recv_sems, after, name):
    n = len(gbs)

    def body(*refs):
        for cp in _sibling_copies(refs[:n], refs[n:2 * n], refs[2 * n], refs[2 * n + 1]):
            cp.wait_recv()
            cp.wait_send()

    res = pl.pallas_call(
        body, name=name, in_specs=[_HBM] * (2 * n) + [_SEM, _SEM, _ANY], out_specs=[_HBM] * (2 * n),
        out_shape=[pltpu.HBM(a.shape, a.dtype) for a in gbs + lands],
        input_output_aliases={k: k for k in range(2 * n)},
        compiler_params=pltpu.CompilerParams(has_side_effects=_EFFECT),
    )(*gbs, *lands, send_sems, recv_sems, after)
    return list(res[:n]), list(res[n:])


def _pair_sum(gfs, ts, c_idx, name):
    n = len(gfs)
    halves = 2

    def body(c_ref, *refs):
        for k in range(n):
            refs[2 * n + k][...] = (refs[k][...].astype(F32) + refs[n + k][...].astype(F32)).astype(jnp.bfloat16)

    in_specs, out_specs, out_shape = [], [], []
    for g in gfs:
        rh = g.shape[1] // halves
        in_specs.append(pl.BlockSpec((None, rh, D), lambda s, h, c_ref: (2 * s + c_ref[0], h, 0)))
    for g in gfs:
        rh = g.shape[1] // halves
        in_specs.append(pl.BlockSpec((None, rh, D), lambda s, h, c_ref: (s, h, 0)))
        out_specs.append(pl.BlockSpec((None, rh, D), lambda s, h, c_ref: (s, h, 0)))
        out_shape.append(_sds((4, g.shape[1], D), jnp.bfloat16))
    return pl.pallas_call(
        body, name=name,
        grid_spec=pltpu.PrefetchScalarGridSpec(num_scalar_prefetch=1, grid=(4, halves), in_specs=in_specs,
                                               out_specs=out_specs),
        out_shape=out_shape,
        compiler_params=pltpu.CompilerParams(dimension_semantics=("arbitrary", "arbitrary"),
                                             vmem_limit_bytes=48 * 2 ** 20),
    )(c_idx, *gfs, *ts)


def _chip_copies(pbs, lands, smalls, send_sems, recv_sems):
    x, y, c = _place()
    own = 2 * x + y
    me = _dev_index((x, y, c))
    chips = [(1 - x, y), (x, 1 - y), (1 - x, 1 - y)]
    n = len(pbs)
    copies = [pltpu.make_async_remote_copy(
        src_ref=pbs[k].at[2 * chip[0] + chip[1]], dst_ref=lands[k].at[own], send_sem=send_sems.at[3 * k + j],
        recv_sem=recv_sems.at[3 * k + j], device_id=(*chip, c), device_id_type=MESH)
        for k in range(n) for j, chip in enumerate(chips)]
    peers = [(x, y, 1 - c)] + [(*chip, cc) for chip in chips for cc in (c, 1 - c)]
    copies += [pltpu.make_async_remote_copy(
        src_ref=smalls[t].at[me], dst_ref=smalls[t].at[me], send_sem=send_sems.at[3 * n + 7 * t + j],
        recv_sem=recv_sems.at[3 * n + 7 * t + j], device_id=peer, device_id_type=MESH)
        for t in range(len(smalls)) for j, peer in enumerate(peers)]
    return copies


def _chip_start(pbs, smalls, collective_id, name):
    n, m = len(pbs), len(smalls)
    lands = [lax.empty(p.shape, p.dtype) for p in pbs]
    arrays = list(pbs) + lands + list(smalls)
    na = len(arrays)
    nsem = 3 * n + 7 * m

    def body(*refs):
        x, y, c = _place()
        chips = [(1 - x, y), (x, 1 - y), (1 - x, 1 - y)]
        peers = [(*chip, c) for chip in chips]
        if m:
            peers += [(x, y, 1 - c)] + [(*chip, 1 - c) for chip in chips]
        _handshake(peers)
        for cp in _chip_copies(refs[:n], refs[n:2 * n], refs[2 * n:na], refs[2 * na], refs[2 * na + 1]):
            cp.start()
        refs[2 * na + 2][...] = jnp.zeros(_TOKEN.shape, _TOKEN.dtype)

    res = pl.pallas_call(
        body, name=name, in_specs=[_HBM] * na, out_specs=[_HBM] * na + [_SEM, _SEM, _VMEM],
        out_shape=[pltpu.HBM(a.shape, a.dtype) for a in arrays] + [pltpu.SemaphoreType.DMA((nsem,))] * 2 + [_TOKEN],
        input_output_aliases={k: k for k in range(na)},
        compiler_params=pltpu.CompilerParams(has_side_effects=_EFFECT, collective_id=collective_id),
    )(*[pltpu.with_memory_space_constraint(a, pltpu.HBM) for a in arrays])
    return (list(res[:n]), list(res[n:2 * n]), list(res[2 * n:na]), res[na], res[na + 1]), res[na + 2]


def _chip_wait(pbs, lands, smalls, send_sems, recv_sems, after, name):
    n = len(pbs)
    arrays = list(pbs) + list(lands) + list(smalls)
    na = len(arrays)

    def body(*refs):
        for cp in _chip_copies(refs[:n], refs[n:2 * n], refs[2 * n:na], refs[na], refs[na + 1]):
            cp.wait_recv()
            cp.wait_send()

    res = pl.pallas_call(
        body, name=name, in_specs=[_HBM] * na + [_SEM, _SEM, _ANY], out_specs=[_HBM] * na,
        out_shape=[pltpu.HBM(a.shape, a.dtype) for a in arrays],
        input_output_aliases={k: k for k in range(na)},
        compiler_params=pltpu.CompilerParams(has_side_effects=_EFFECT),
    )(*arrays, send_sems, recv_sems, after)
    return list(res[:n]), list(res[n:2 * n]), list(res[2 * n:])


def _chip_sum(pbs, qs, own_idx, states, name):
    n = len(pbs)
    halves = 2
    n_in = 4 * n + 3 * sum(s is not None for s in states)

    def body(o_ref, *refs):
        nxt_in, nxt_out = 4 * n, n_in
        for k in range(n):
            acc = refs[4 * k][...].astype(F32)
            for j in range(1, 4):
                acc = acc + refs[4 * k + j][...].astype(F32)
            refs[nxt_out][...] = acc
            nxt_out += 1
            if states[k] is not None:
                w_ref, m_ref, v_ref = refs[nxt_in:nxt_in + 3]
                nxt_in += 3
                d, nm, nv = _adamw_math(w_ref[...], acc, m_ref[...], v_ref[...])
                refs[nxt_out][...] = d
                refs[nxt_out + 1][...] = nm
                refs[nxt_out + 2][...] = nv
                nxt_out += 3

    def other(j):
        return lambda h, o: ((o[0] + j) % 4, h, 0)

    def rows(r):
        return pl.BlockSpec((r // halves, D), lambda h, o: (h, 0))

    in_specs, ins = [], []
    for p, q in zip(pbs, qs):
        rh = p.shape[1] // halves
        in_specs.append(pl.BlockSpec((None, rh, D), lambda h, o: (o[0], h, 0)))
        ins.append(p)
        for j in range(1, 4):
            in_specs.append(pl.BlockSpec((None, rh, D), other(j)))
            ins.append(q)
    out_specs, out_shape, counts = [], [], []
    for p, s in zip(pbs, states):
        r = p.shape[1]
        if s is not None:
            in_specs += [rows(r)] * 3
            ins += list(s)
        counts.append(1 if s is None else 4)
        out_specs += [rows(r)] * counts[-1]
        out_shape += [_sds((r, D), F32)] * counts[-1]
    res = pl.pallas_call(
        body, name=name,
        grid_spec=pltpu.PrefetchScalarGridSpec(num_scalar_prefetch=1, grid=(halves,), in_specs=in_specs,
                                               out_specs=out_specs),
        out_shape=out_shape,
        compiler_params=pltpu.CompilerParams(dimension_semantics=("arbitrary",), vmem_limit_bytes=56 * 2 ** 20),
    )(own_idx, *ins)
    out, base = [], 0
    for c in counts:
        out.append(tuple(res[base:base + c]))
        base += c
    return out


REDUCE_GROUPS = (("w_gateT", "w_upT", "w_down"), ("w_out", "w_mem_q", "w_mem_kvT", "w_mem_o"), ("w_inT",))


class _Reduce:
    def __init__(self, states):
        x, y, c = _place()
        self._c = c.astype(jnp.int32).reshape(1)
        self._own = (2 * x + y).astype(jnp.int32).reshape(1)
        self._me = _dev_index((x, y, c))
        self._state = {}
        self._adam = states
        self.results = {}
        self.smalls = None

    def start(self, gi, wgrads):
        gbs = [g.reshape(NDEV, g.shape[0] // NDEV, D) for g in wgrads]
        gbs, lands, send_sems, recv_sems, token = _sibling_start(gbs, ID_SIBLING + gi, "reduce_sibling_start_%d" % gi)
        self._state[gi] = (gbs, lands, send_sems, recv_sems)
        return token

    def middle(self, gi, after, smalls=()):
        gbs, lands, send_sems, recv_sems = self._state[gi]
        gbs, ts = _sibling_wait(gbs, lands, send_sems, recv_sems, after, "reduce_sibling_wait_%d" % gi)
        pbs = _pair_sum(gbs, ts, self._c, "reduce_pair_sum_%d" % gi)
        self._state[gi], token = _chip_start(pbs, list(smalls), ID_CHIP + gi, "reduce_chip_start_%d" % gi)
        return token

    def finish(self, gi, after):
        pbs, lands, smalls, send_sems, recv_sems = self._state[gi]
        pbs, qs, smalls = _chip_wait(pbs, lands, smalls, send_sems, recv_sems, after, "reduce_chip_wait_%d" % gi)
        names = REDUCE_GROUPS[gi]
        states = [self._adam.get(n) for n in names]
        self.results.update(zip(names, _chip_sum(pbs, qs, self._own, states, "reduce_update_%d" % gi)))
        if smalls:
            self.smalls = smalls

    def small_zones(self, small, g_wdw):
        pack = jnp.concatenate(
            [jnp.pad(small[name], ((0, 0), (0, lanes - small[name].shape[1]))) for name, lanes, _ in PACK], axis=1)
        zones = []
        for a in (pack, g_wdw):
            zone = lax.empty((NDEV,) + a.shape, a.dtype)
            zones.append(lax.dynamic_update_slice(zone, a[None], (self._me,) + (0,) * a.ndim))
        return zones


def _adamw_math(w, g, m, v):
    m = ADAM_B1 * m + (1.0 - ADAM_B1) * g
    v = ADAM_B2 * v + (1.0 - ADAM_B2) * (g * g)
    m_hat = m / (1.0 - ADAM_B1 ** ADAM_STEP)
    v_hat = v / (1.0 - ADAM_B2 ** ADAM_STEP)
    delta = -ADAM_LR * (m_hat / (jnp.sqrt(v_hat) + ADAM_EPS) + ADAM_WD * w)
    return delta, m, v


def _adamw(w, g, m, v, name):
    R, C = w.shape
    rc = 256 if R % 256 == 0 and R > 256 else R

    def body(w_ref, g_ref, m_ref, v_ref, d_ref, nm_ref, nv_ref):
        d, nm, nv = _adamw_math(w_ref[...], g_ref[...], m_ref[...], v_ref[...])
        d_ref[...] = d
        nm_ref[...] = nm
        nv_ref[...] = nv

    return _hbm_call(
        body, name=name, grid=(R // rc,),
        in_specs=[_rows(rc, C)] * 4, out_specs=[_rows(rc, C)] * 3,
        out_shape=[_sds((R, C), F32)] * 3,
        compiler_params=_params(40),
    )(w, g, m, v)


PACK = (("loss", LANES, 1), ("g_mix", D, D), ("b_in", INC, INC), ("b_dw", CC, CC), ("g_conv_ln", CC, CC),
        ("b_conv_ln", CC, CC), ("attn_sink", LANES, NQ), ("b_out", D, D), ("g_mem_q", D, D), ("g_mem_kv", D, D),
        ("g_ffn", D, D), ("g_final", D, D))
PACK_LANES = sum(p[1] for p in PACK)


def _small_update(packs, gdws, ws, ms, vs):
    names = [p[0] for p in PACK[1:]]
    n = len(names)

    def body(*refs):
        packs_ref, gdws_ref = refs[0], refs[1]
        w_refs, m_refs, v_refs = refs[2:2 + n], refs[2 + n:2 + 2 * n], refs[2 + 2 * n:2 + 3 * n]
        outs = refs[2 + 3 * n:]
        tot = packs_ref[0]
        gdw = gdws_ref[0]
        for s in range(1, NDEV):
            tot = tot + packs_ref[s]
            gdw = gdw + gdws_ref[s]
        outs[0][...] = tot[:, 0:LANES]
        outs[1][...] = gdw
        off = LANES
        for k, (_, lanes, used) in enumerate(PACK[1:]):
            g = tot[:, off:off + used]
            off += lanes
            d, nm, nv = _adamw_math(w_refs[k][...], g, m_refs[k][...], v_refs[k][...])
            o = outs[2 + 4 * k:6 + 4 * k]
            o[0][...] = g
            o[1][...] = d
            o[2][...] = nm
            o[3][...] = nv

    def full(a):
        nd = len(a.shape)
        return pl.BlockSpec(a.shape, lambda i: (0,) * nd)

    ins = [packs, gdws, *ws, *ms, *vs]
    out_shape = [_sds((1, LANES), F32), _sds(gdws.shape[1:], F32)]
    for w in ws:
        out_shape += [_sds(w.shape, F32)] * 4
    return pl.pallas_call(
        body, name="small_update", grid=(1,),
        in_specs=[full(a) for a in ins], out_specs=[full(o) for o in out_shape], out_shape=out_shape,
        compiler_params=_params(40),
    )(*ins)


def kernel(x, mem, g_mix, w_in, b_in, w_dw, b_dw, g_conv_ln, b_conv_ln, attn_sink, w_out, b_out, g_mem_q, g_mem_kv, w_mem_q, w_mem_kv, w_mem_o, g_ffn, w_gate, w_up, w_down, g_final, loss_target, m_g_mix, m_w_in, m_b_in, m_w_dw, m_b_dw, m_g_conv_ln, m_b_conv_ln, m_attn_sink, m_w_out, m_b_out, m_g_mem_q, m_g_mem_kv, m_w_mem_q, m_w_mem_kv, m_w_mem_o, m_g_ffn, m_w_gate, m_w_up, m_w_down, m_g_final, v_g_mix, v_w_in, v_b_in, v_w_dw, v_b_dw, v_g_conv_ln, v_b_conv_ln, v_attn_sink, v_w_out, v_b_out, v_g_mem_q, v_g_mem_kv, v_w_mem_q, v_w_mem_kv, v_w_mem_o, v_g_ffn, v_w_gate, v_w_up, v_w_down, v_g_final):
    given = dict(locals())
    bf = jnp.bfloat16
    xi, yi, ci = _place()
    me = 4 * xi + 2 * yi + ci

    transposed = {"w_inT": w_in, "w_mem_kvT": w_mem_kv, "w_gateT": w_gate, "w_upT": w_up}
    plain = {"w_out": w_out, "w_mem_q": w_mem_q, "w_mem_o": w_mem_o, "w_down": w_down}
    blocks = {n: (transposed[n][0].T if n in transposed else plain[n][0]).astype(bf) for n in W_NAMES}
    blocks["w_dw"] = jnp.pad(w_dw[0], ((0, 1), (0, 0)))

    def landing_zone(b):
        zone = lax.empty((NDEV,) + b.shape, b.dtype)
        return lax.dynamic_update_slice(zone, b[None], (me,) + (0,) * b.ndim)

    gather = _Gather(_gather_start([[landing_zone(blocks[n]) for n in names] for names in GATHER_GROUPS]))

    sp = {n: given[n] for n in ("g_mix", "b_in", "b_dw", "g_conv_ln", "b_conv_ln", "attn_sink", "b_out", "g_mem_q",
                                "g_mem_kv", "g_ffn")}
    sp["g_final"] = g_final.reshape(1, D)
    frames = {"w_inT": ("w_in", True), "w_gateT": ("w_gate", True), "w_upT": ("w_up", True),
              "w_out": ("w_out", False), "w_mem_q": ("w_mem_q", False), "w_mem_o": ("w_mem_o", False),
              "w_down": ("w_down", False)}
    states = {}
    for frame, (n, transposed) in frames.items():
        wmv = (given[n][0], given["m_" + n][0], given["v_" + n][0])
        states[frame] = tuple(a.T for a in wmv) if transposed else wmv
    reduce = _Reduce(states)
    grad_x = _local_step(x[0], mem[0], loss_target[0], sp, gather, reduce)
    reduce.finish(2, reduce.results["w_out"][1])

    grads, deltas, new_m, new_v = {}, {}, {}, {}
    for frame, (n, transposed) in frames.items():
        outs = [a.T if transposed else a for a in reduce.results[frame]]
        grads[n], deltas[n], new_m[n], new_v[n] = [a[None] for a in outs]

    def update(n, g):
        d, nm, nv = _adamw(given[n][0], g, given["m_" + n][0], given["v_" + n][0], "adamw_" + n)
        grads[n], deltas[n], new_m[n], new_v[n] = g[None], d[None], nm[None], nv[None]

    update("w_mem_kv", reduce.results["w_mem_kvT"][0].T)

    packs, gdws = reduce.smalls
    names_small = [p[0] for p in PACK[1:]]

    def as_row(a):
        return a.reshape(1, -1)

    res = _small_update(packs, gdws, [as_row(given[n]) for n in names_small],
                        [as_row(given["m_" + n]) for n in names_small],
                        [as_row(given["v_" + n]) for n in names_small])
    loss = res[0][0, 0]
    for k, n in enumerate(names_small):
        shape = given[n].shape
        g, d, nm, nv = res[2 + 4 * k:6 + 4 * k]
        grads[n], deltas[n], new_m[n], new_v[n] = (g.reshape(shape), d.reshape(shape), nm.reshape(shape),
                                                   nv.reshape(shape))
    update("w_dw", lax.dynamic_slice(res[1], (0, me * (CC // NDEV)), (TAPS, CC // NDEV)))

    order = ("g_mix", "w_in", "b_in", "w_dw", "b_dw", "g_conv_ln", "b_conv_ln", "attn_sink", "w_out", "b_out",
             "g_mem_q", "g_mem_kv", "w_mem_q", "w_mem_kv", "w_mem_o", "g_ffn", "w_gate", "w_up", "w_down", "g_final")
    return (loss, grad_x[None], *[grads[n] for n in order], *[deltas[n] for n in order],
            *[new_m[n] for n in order], *[new_v[n] for n in order])
```

```python
import functools

import jax
import jax.numpy as jnp
from jax import lax
from jax.experimental import pallas as pl
from jax.experimental.pallas import tpu as pltpu

D = 1024
CC = 512
HD = 64
NQ = 8
NKV = 2
GRP = NQ // NKV
ATT = NQ * HD
KVC = NKV * HD
BLK = 128
MH = 4
MHD = D // MH
F = 2816
FCH = 256
FCH_BWD = F // 2
INC = 2 * CC + ATT + 2 * KVC
TAPS = 31
EPS = 1e-6
ROPE_THETA = 10000.0
SCALE = HD ** -0.5
MSCALE = MHD ** -0.5
LANES = 128
NDEV = 8

ADAM_LR = 0.001
ADAM_B1 = 0.9
ADAM_B2 = 0.999
ADAM_EPS = 1e-08
ADAM_WD = 0.01
ADAM_STEP = 10

F32 = jnp.float32
_MXU = jnp.bfloat16
MESH = pl.DeviceIdType.MESH

W_NAMES = ("w_inT", "w_out", "w_mem_q", "w_mem_kvT", "w_mem_o", "w_gateT", "w_upT", "w_down")
W_ROWS = (INC // NDEV, D // NDEV, D // NDEV, 2 * D // NDEV, D // NDEV, F // NDEV, F // NDEV, F // NDEV)


def _mm(a, b, ca, cb):
    return lax.dot_general(a.astype(_MXU), b.astype(_MXU), (((ca,), (cb,)), ((), ())),
                           preferred_element_type=F32)


def _nn(a, b):
    return _mm(a, b, 1, 0)


def _nt(a, b):
    return _mm(a, b, 1, 1)


def _tn(a, b):
    return _mm(a, b, 0, 0)


def _sigmoid(x):
    return 1.0 / (1.0 + jnp.exp(-x))


def _rms(x, g):
    r = lax.rsqrt(jnp.mean(x * x, axis=-1, keepdims=True) + EPS)
    return x * r * g, r


def _rms_bwd(dy, x, r, g):
    xh = x * r
    dg = jnp.sum(dy * xh, axis=0, keepdims=True)
    dxh = dy * g
    dx = r * (dxh - xh * jnp.mean(dxh * xh, axis=-1, keepdims=True))
    return dx, dg


def _rope(t, c, s):
    n = t.shape[1]
    reps = n // LANES
    if reps > 1:
        c = jnp.tile(c, (1, reps))
        s = jnp.tile(s, (1, reps))
    lane = lax.broadcasted_iota(jnp.int32, t.shape, 1)
    first = (lane & (HD - 1)) < (HD // 2)
    partner = jnp.where(first, pltpu.roll(t, n - HD // 2, 1), pltpu.roll(t, HD // 2, 1))
    return t * c + partner * s


def _rope_bwd(dt, c, s):
    n = dt.shape[1]
    reps = n // LANES
    if reps > 1:
        c = jnp.tile(c, (1, reps))
        s = jnp.tile(s, (1, reps))
    lane = lax.broadcasted_iota(jnp.int32, dt.shape, 1)
    first = (lane & (HD - 1)) < (HD // 2)
    ds = dt * s
    partner = jnp.where(first, pltpu.roll(ds, n - HD // 2, 1), pltpu.roll(ds, HD // 2, 1))
    return dt * c + partner


def _acc_init(i, *refs):
    @pl.when(i == 0)
    def _():
        for r in refs:
            r[...] = jnp.zeros(r.shape, r.dtype)


def _acc_row(ref, v):
    ref[...] += jnp.broadcast_to(v, ref.shape)


def _rows(tm, n):
    return pl.BlockSpec((tm, n), lambda i: (i, 0))


def _full(shape):
    nd = len(shape)
    return pl.BlockSpec(shape, lambda i: (0,) * nd)


def _once(shape):
    nd = len(shape)
    return pl.BlockSpec(shape, lambda i: (0,) * nd, pipeline_mode=pl.Buffered(1))


def _params(vmem_mb):
    return pltpu.CompilerParams(dimension_semantics=("arbitrary",), vmem_limit_bytes=vmem_mb * 2 ** 20)


def _sds(shape, dtype):
    return pltpu.HBM(shape, dtype)


def _hbm_call(*args, **kwargs):
    call = pl.pallas_call(*args, **kwargs)
    return lambda *ops: call(*[pltpu.with_memory_space_constraint(o, pltpu.HBM) for o in ops])


def _mem_kv_fwd(mem, g_kv, wkvT):
    M = mem.shape[0]

    def body(mem_ref, g_ref, w_ref, km_ref, vm_ref, mn_ref):
        y, _ = _rms(mem_ref[...], g_ref[...])
        kv = _nt(y, w_ref[...])
        km_ref[...] = kv[:, :D].astype(km_ref.dtype)
        vm_ref[...] = kv[:, D:].astype(vm_ref.dtype)
        mn_ref[...] = y.astype(mn_ref.dtype)

    return _hbm_call(
        body, name="mem_kv_fwd", grid=(1,),
        in_specs=[_full((M, D)), _full((1, D)), _full((2 * D, D))],
        out_specs=[_full((M, D))] * 3,
        out_shape=[_sds((M, D), _MXU)] * 3,
        compiler_params=_params(40),
    )(mem, g_kv, wkvT)


def _in_proj_fwd(x, g_mix, winT, b_in, cosq, sinq, tm=512):
    S = x.shape[0]

    def body(x_ref, g_ref, w_ref, b_ref, c_ref, s_ref, h_ref, uglu_ref, vg_ref, q_ref, k_ref, v_ref):
        h, _ = _rms(x_ref[...], g_ref[...])
        hb = h.astype(h_ref.dtype)
        h_ref[...] = hb
        u = _nt(hb, w_ref[...]) + b_ref[...]
        uglu_ref[...] = u[:, :2 * CC].astype(uglu_ref.dtype)
        vg_ref[...] = u[:, :CC] * _sigmoid(u[:, CC:2 * CC])
        c = c_ref[...]
        s = s_ref[...]
        q_ref[...] = _rope(u[:, 2 * CC:2 * CC + ATT], c, s).astype(q_ref.dtype)
        k_ref[...] = _rope(u[:, 2 * CC + ATT:2 * CC + ATT + KVC], c, s).astype(k_ref.dtype)
        v_ref[...] = u[:, 2 * CC + ATT + KVC:].astype(v_ref.dtype)

    return _hbm_call(
        body, name="in_proj_fwd", grid=(S // tm,),
        in_specs=[_rows(tm, D), _full((1, D)), _full((INC, D)), _full((1, INC)), _rows(tm, LANES), _rows(tm, LANES)],
        out_specs=[_rows(tm, D), _rows(tm, 2 * CC), _rows(tm, CC), _rows(tm, ATT), _rows(tm, KVC), _rows(tm, KVC)],
        out_shape=[_sds((S, D), _MXU), _sds((S, 2 * CC), _MXU), _sds((S, CC), F32), _sds((S, ATT), _MXU),
                   _sds((S, KVC), _MXU), _sds((S, KVC), _MXU)],
        compiler_params=_params(48),
    )(x, g_mix, winT, b_in, cosq, sinq)


CONV_HALO = 16


CONV_TILE = 256


def _shift_scratch(tm, sets):
    return pltpu.VMEM((sets, CC // LANES, 8, tm + 24, LANES), F32)


def _shifted_copies(pad_ref, sh_ref, t0, tm):
    for cl in range(CC // LANES):
        blk = pad_ref[pl.ds(t0, tm + 2 * CONV_HALO), cl * LANES:(cl + 1) * LANES]
        for b in range(8):
            sh_ref[cl, b, :, :] = blk[b:b + tm + 24, :]


def _conv_taps():
    return [(8 * a + b - 1, b, 8 * a) for b in range(8) for a in range(4) if 0 <= 8 * a + b - 1 < TAPS]


CONV_ROWS = 64
SUBLANES = 8


def _dwconv_tile(sh_ref, tm, w_ref, flip, out_ref):
    for cl in range(CC // LANES):
        lanes = slice(cl * LANES, (cl + 1) * LANES)

        def chunk(c, carry):
            r = pl.multiple_of(c * CONV_ROWS, CONV_ROWS)
            accs = [None] * 4
            for n, (k, b, off) in enumerate(_conv_taps()):
                kk = TAPS - 1 - k if flip else k
                term = sh_ref[cl, b, pl.ds(off + r, CONV_ROWS), :] * w_ref[kk:kk + 1, lanes]
                accs[n % 4] = term if accs[n % 4] is None else accs[n % 4] + term
            out_ref[pl.ds(r, CONV_ROWS), lanes] = (accs[0] + accs[1]) + (accs[2] + accs[3])
            return carry

        lax.fori_loop(0, tm // CONV_ROWS, chunk, 0)


def _fill_padded(i, src_hbm, pad_ref, sem, S):
    @pl.when(i == 0)
    def _():
        zeros = jnp.zeros((CONV_HALO, pad_ref.shape[1]), pad_ref.dtype)
        pad_ref[0:CONV_HALO, :] = zeros
        pad_ref[CONV_HALO + S:2 * CONV_HALO + S, :] = zeros
        cp = pltpu.make_async_copy(src_hbm, pad_ref.at[pl.ds(CONV_HALO, S), :], sem)
        cp.start()
        cp.wait()


def _ln_silu(y, g, b):
    mu = jnp.mean(y, axis=-1, keepdims=True)
    yc = y - mu
    var = jnp.mean(yc * yc, axis=-1, keepdims=True)
    rstd = lax.rsqrt(var + EPS)
    yh = yc * rstd
    z = yh * g + b
    return yh, rstd, z


def _conv_fwd(vg, w_dw, b_dw, ln_g, ln_b):
    S = vg.shape[0]
    tm = min(CONV_TILE, S)

    def body(vg_hbm, w_ref, b_ref, g_ref, bb_ref, y_ref, yc_ref, pad_ref, sh_ref, sem):
        i = pl.program_id(0)
        _fill_padded(i, vg_hbm, pad_ref, sem, S)
        _shifted_copies(pad_ref, sh_ref.at[0], pl.multiple_of(i * tm, tm), tm)
        _dwconv_tile(sh_ref.at[0], tm, w_ref, False, y_ref)
        y = y_ref[...] + b_ref[...]
        y_ref[...] = y
        _, _, z = _ln_silu(y, g_ref[...], bb_ref[...])
        yc_ref[...] = (z * _sigmoid(z)).astype(yc_ref.dtype)

    return _hbm_call(
        body, name="conv_fwd", grid=(S // tm,),
        in_specs=[pl.BlockSpec(memory_space=pl.ANY), _full((TAPS, CC)), _full((1, CC)), _full((1, CC)), _full((1, CC))],
        out_specs=[_rows(tm, CC), _rows(tm, CC)],
        out_shape=[_sds((S, CC), F32), _sds((S, CC), _MXU)],
        scratch_shapes=[pltpu.VMEM((S + 2 * CONV_HALO, CC), F32), _shift_scratch(tm, 1), pltpu.SemaphoreType.DMA],
        compiler_params=_params(40),
    )(vg, w_dw, b_dw, ln_g, ln_b)


def _attn_window(i, S):
    start = jnp.clip((i - 1) * BLK, 0, S - 3 * BLK)
    return pl.multiple_of(start, BLK)


def _attn_mask(i, start):
    keys = lax.broadcasted_iota(jnp.int32, (3 * BLK, GRP * BLK), 0)
    cols = lax.broadcasted_iota(jnp.int32, (3 * BLK, GRP * BLK), 1)
    rel = (start + keys) - (i * BLK + (cols & (BLK - 1)))
    return jnp.abs(rel) <= BLK


def _stack_heads(t, hk):
    return jnp.concatenate([t[:, (hk * GRP + g) * HD:(hk * GRP + g + 1) * HD] for g in range(GRP)], axis=0)


def _unstack_heads_t(t):
    cols = []
    for p in range(GRP // 2):
        pair = jnp.concatenate([t[:, (2 * p) * BLK:(2 * p + 1) * BLK], t[:, (2 * p + 1) * BLK:(2 * p + 2) * BLK]],
                               axis=0)
        cols.append(pair.T)
    return jnp.concatenate(cols, axis=1)


def _attn_probs(qs, kh, sk_ref, hk, mask):
    s = _nt(kh, qs) * SCALE
    s = jnp.where(mask, s, -1e30)
    sk = jnp.concatenate(
        [jnp.broadcast_to(sk_ref[0:1, hk * GRP + g:hk * GRP + g + 1], (1, BLK)) for g in range(GRP)], axis=1)
    m = jnp.maximum(jnp.max(s, axis=0, keepdims=True), sk)
    e = jnp.exp(s - m)
    es = jnp.exp(sk - m)
    inv = 1.0 / (jnp.sum(e, axis=0, keepdims=True) + es)
    return e, inv, es


ATTN_BLOCKS = 4


def _attn_fwd(qr, kr, vv, sink):
    S = qr.shape[0]
    tq = ATTN_BLOCKS * BLK

    def body(q_ref, k_ref, v_ref, sk_ref, o_ref):
        for b in range(ATTN_BLOCKS):
            i = pl.program_id(0) * ATTN_BLOCKS + b
            rows = slice(b * BLK, (b + 1) * BLK)
            start = _attn_window(i, S)
            kb = k_ref[pl.ds(start, 3 * BLK), :]
            vb = v_ref[pl.ds(start, 3 * BLK), :]
            q = q_ref[rows, :]
            mask = _attn_mask(i, start)
            outs = []
            for hk in range(NKV):
                hs = slice(hk * HD, (hk + 1) * HD)
                e, inv, _ = _attn_probs(_stack_heads(q, hk), kb[:, hs], sk_ref, hk, mask)
                outs.append(_unstack_heads_t(_tn(vb[:, hs], e) * inv))
            o_ref[rows, :] = jnp.concatenate(outs, axis=1).astype(o_ref.dtype)

    return _hbm_call(
        body, name="attn_fwd", grid=(S // tq,),
        in_specs=[_rows(tq, ATT), _full((S, KVC)), _full((S, KVC)), _full((1, NQ))],
        out_specs=_rows(tq, ATT),
        out_shape=_sds((S, ATT), _MXU),
        compiler_params=_params(40),
    )(qr, kr, vv, sink)


def _out_proj_fwd(x, yc, ya, wout, b_out, tm=512):
    S = x.shape[0]

    def body(x_ref, yc_ref, ya_ref, w_ref, b_ref, x1_ref, ymix_ref):
        ymix = jnp.concatenate([yc_ref[...], ya_ref[...]], axis=1)
        ymix_ref[...] = ymix
        x1_ref[...] = x_ref[...] + _nn(ymix, w_ref[...]) + b_ref[...]

    return _hbm_call(
        body, name="out_proj_fwd", grid=(S // tm,),
        in_specs=[_rows(tm, D), _rows(tm, CC), _rows(tm, ATT), _full((D, D)), _full((1, D))],
        out_specs=[_rows(tm, D), _rows(tm, D)],
        out_shape=[_sds((S, D), F32), _sds((S, D), _MXU)],
        compiler_params=_params(40),
    )(x, yc, ya, wout, b_out)


def _softmax_t(s):
    m = jnp.max(s, axis=0, keepdims=True)
    e = jnp.exp(s - m)
    return e, 1.0 / jnp.sum(e, axis=0, keepdims=True)


def _softmax(s):
    m = jnp.max(s, axis=-1, keepdims=True)
    e = jnp.exp(s - m)
    return e * (1.0 / jnp.sum(e, axis=-1, keepdims=True))


def _mem_attn_fwd(x1, g_q, wq, km, vm, wo, after, tm=512):
    S = x1.shape[0]
    M = km.shape[0]

    def body(x1_ref, g_ref, wq_ref, km_ref, vm_ref, wo_ref, after_ref, x2_ref, qm_ref, om_ref):
        x1v = x1_ref[...]
        hq, _ = _rms(x1v, g_ref[...])
        qm = _nn(hq, wq_ref[...]).astype(qm_ref.dtype)
        qm_ref[...] = qm
        oms = []
        for h in range(MH):
            sl = slice(h * MHD, (h + 1) * MHD)
            p = _softmax(_nt(qm[:, sl], km_ref[:, sl]) * MSCALE)
            oms.append(_nn(p, vm_ref[:, sl]))
        om = jnp.concatenate(oms, axis=1).astype(om_ref.dtype)
        om_ref[...] = om
        x2_ref[...] = x1v + _nn(om, wo_ref[...])

    return _hbm_call(
        body, name="mem_attn_fwd", grid=(S // tm,),
        in_specs=[_rows(tm, D), _full((1, D)), _full((D, D)), _full((M, D)), _full((M, D)), _full((D, D)), _AFTER],
        out_specs=[_rows(tm, D)] * 3,
        out_shape=[_sds((S, D), F32), _sds((S, D), _MXU), _sds((S, D), _MXU)],
        compiler_params=_params(48),
    )(x1, g_q, wq, km, vm, wo, after)


def _ffn_loss_fwd(x2, tgt, g_ffn, g_final, wgT, wuT, wd, tm=512):
    S = x2.shape[0]

    def body(x2_ref, t_ref, gf_ref, gl_ref, wg_ref, wu_ref, wd_ref,
             gt_ref, up_ref, hf_ref, dx3_ref, dx3b_ref, loss_ref, dgl_ref):
        i = pl.program_id(0)
        _acc_init(i, loss_ref, dgl_ref)
        x2v = x2_ref[...]
        hf, _ = _rms(x2v, gf_ref[...])
        hb = hf.astype(hf_ref.dtype)
        hf_ref[...] = hb
        x3 = x2v
        for c in range(F // FCH):
            sl = slice(c * FCH, (c + 1) * FCH)
            gt = _nt(hb, wg_ref[sl, :])
            up = _nt(hb, wu_ref[sl, :])
            gt_ref[:, sl] = gt.astype(gt_ref.dtype)
            up_ref[:, sl] = up.astype(up_ref.dtype)
            x3 = x3 + _nn(gt * _sigmoid(gt) * up, wd_ref[sl, :])
        gl = gl_ref[...]
        y, r = _rms(x3, gl)
        diff = y - t_ref[...]
        _acc_row(loss_ref, jnp.sum(jnp.sum(diff * diff, axis=-1, keepdims=True), axis=0, keepdims=True) * (0.5 / D))
        dx3, dgl = _rms_bwd(diff * (1.0 / D), x3, r, gl)
        dx3_ref[...] = dx3
        dx3b_ref[...] = dx3.astype(dx3b_ref.dtype)
        _acc_row(dgl_ref, dgl)

    return _hbm_call(
        body, name="ffn_loss_fwd", grid=(S // tm,),
        in_specs=[_rows(tm, D), _rows(tm, D), _full((1, D)), _full((1, D)),
                  _once((F, D)), _once((F, D)), _once((F, D))],
        out_specs=[_rows(tm, F), _rows(tm, F), _rows(tm, D), _rows(tm, D), _rows(tm, D),
                   _full((8, LANES)), _full((8, D))],
        out_shape=[_sds((S, F), _MXU), _sds((S, F), _MXU), _sds((S, D), _MXU), _sds((S, D), F32),
                   _sds((S, D), _MXU), _sds((8, LANES), F32), _sds((8, D), F32)],
        compiler_params=_params(56),
    )(x2, tgt, g_ffn, g_final, wgT, wuT, wd)


def _ffn_bwd(dx3, x2, gt, up, g_ffn, wd, wgT, wuT, tm=256):
    S = x2.shape[0]

    def body(dx3_ref, x2_ref, gt_ref, up_ref, g_ref, wd_ref, wg_ref, wu_ref,
             dx2_ref, dx2b_ref, dgt_ref, dup_ref, act_ref, dg_ref):
        i = pl.program_id(0)
        _acc_init(i, dg_ref)
        dx3v = dx3_ref[...]
        db = dx3v.astype(_MXU)
        dhf = jnp.zeros((tm, D), F32)
        for c in range(F // FCH_BWD):
            sl = slice(c * FCH_BWD, (c + 1) * FCH_BWD)
            gtv = gt_ref[:, sl].astype(F32)
            upv = up_ref[:, sl].astype(F32)
            dact = _nt(db, wd_ref[sl, :])
            sig = _sigmoid(gtv)
            silu = gtv * sig
            dup = (dact * silu).astype(dup_ref.dtype)
            dgt = (dact * upv * (sig * (1.0 + gtv * (1.0 - sig)))).astype(dgt_ref.dtype)
            act_ref[:, sl] = (silu * upv).astype(act_ref.dtype)
            dgt_ref[:, sl] = dgt
            dup_ref[:, sl] = dup
            dhf = dhf + _nn(dgt, wg_ref[sl, :]) + _nn(dup, wu_ref[sl, :])
        x2v = x2_ref[...]
        g = g_ref[...]
        _, r = _rms(x2v, g)
        dxn, dg = _rms_bwd(dhf, x2v, r, g)
        dx2 = dx3v + dxn
        dx2_ref[...] = dx2
        dx2b_ref[...] = dx2.astype(dx2b_ref.dtype)
        _acc_row(dg_ref, dg)

    return _hbm_call(
        body, name="ffn_bwd", grid=(S // tm,),
        in_specs=[_rows(tm, D), _rows(tm, D), _rows(tm, F), _rows(tm, F), _full((1, D)),
                  _once((F, D)), _once((F, D)), _once((F, D))],
        out_specs=[_rows(tm, D), _rows(tm, D), _rows(tm, F), _rows(tm, F), _rows(tm, F), _full((8, D))],
        out_shape=[_sds((S, D), F32), _sds((S, D), _MXU), _sds((S, F), _MXU), _sds((S, F), _MXU),
                   _sds((S, F), _MXU), _sds((8, D), F32)],
        compiler_params=_params(56),
    )(dx3, x2, gt, up, g_ffn, wd, wgT, wuT)


_AFTER = pl.BlockSpec(memory_space=pl.ANY)


def _wgrad(pairs, name, after, rc=256):
    S, R = pairs[0][0].shape
    N = pairs[0][1].shape[1]
    rights = []
    for _, b in pairs:
        if not any(b is r for r in rights):
            rights.append(b)
    which = [next(i for i, r in enumerate(rights) if r is b) for _, b in pairs]
    n, nb = len(pairs), len(rights)

    def body(*refs):
        for k in range(n):
            refs[n + nb + 1 + k][...] = _tn(refs[k][...], refs[n + which[k]][...]).astype(jnp.bfloat16)

    res = _hbm_call(
        body, name=name, grid=(R // rc,),
        in_specs=[pl.BlockSpec((S, rc), lambda j: (0, j))] * n + [_once((S, N))] * nb + [_AFTER],
        out_specs=[_rows(rc, N)] * n,
        out_shape=[_sds((R, N), jnp.bfloat16)] * n,
        compiler_params=_params(56),
    )(*[a for a, _ in pairs], *rights, after)
    return list(res)


def _mem_attn_bwd(dx2, x1, qm, km, vm, g_q, wq, wo, wout, after, tm=512):
    S = x1.shape[0]
    M = km.shape[0]

    def body(dx2_ref, x1_ref, qm_ref, km_ref, vm_ref, g_ref, wq_ref, wo_ref, wout_ref, after_ref,
             dx1_ref, dx1b_ref, hq_ref, dqm_ref, dyc_ref, dya_ref, dkm_ref, dvm_ref, dg_ref, dbo_ref):
        i = pl.program_id(0)
        _acc_init(i, dkm_ref, dvm_ref, dg_ref, dbo_ref)
        dx2v = dx2_ref[...]
        domt = _nt(wo_ref[...], dx2v).astype(_MXU)
        dqs = []
        for h in range(MH):
            sl = slice(h * MHD, (h + 1) * MHD)
            qh = qm_ref[:, sl]
            kh = km_ref[:, sl]
            e, inv = _softmax_t(_nt(kh, qh) * MSCALE)
            p = e * inv
            dp = _nn(vm_ref[:, sl], domt[sl, :])
            ds = (p * (dp - jnp.sum(dp * p, axis=0, keepdims=True))).astype(_MXU)
            dqs.append(_tn(ds, kh) * MSCALE)
            dkm_ref[:, sl] += _nn(ds, qh) * MSCALE
            dvm_ref[:, sl] += _nt(p, domt[sl, :])
        dqm = jnp.concatenate(dqs, axis=1).astype(dqm_ref.dtype)
        dqm_ref[...] = dqm
        dhq = _nt(dqm, wq_ref[...])
        x1v = x1_ref[...]
        g = g_ref[...]
        hq, r = _rms(x1v, g)
        hq_ref[...] = hq.astype(hq_ref.dtype)
        dxn, dg = _rms_bwd(dhq, x1v, r, g)
        dx1 = dx2v + dxn
        dx1_ref[...] = dx1
        dx1b = dx1.astype(dx1b_ref.dtype)
        dx1b_ref[...] = dx1b
        dym = _nt(dx1b, wout_ref[...])
        dyc_ref[...] = dym[:, :CC]
        dya_ref[...] = dym[:, CC:].astype(dya_ref.dtype)
        _acc_row(dg_ref, dg)
        _acc_row(dbo_ref, jnp.sum(dx1, axis=0, keepdims=True))

    return _hbm_call(
        body, name="mem_attn_bwd", grid=(S // tm,),
        in_specs=[_rows(tm, D), _rows(tm, D), _rows(tm, D), _full((M, D)), _full((M, D)), _full((1, D)),
                  _once((D, D)), _once((D, D)), _once((D, D)), _AFTER],
        out_specs=[_rows(tm, D), _rows(tm, D), _rows(tm, D), _rows(tm, D), _rows(tm, CC), _rows(tm, ATT),
                   _full((M, D)), _full((M, D)), _full((8, D)), _full((8, D))],
        out_shape=[_sds((S, D), F32), _sds((S, D), _MXU), _sds((S, D), _MXU), _sds((S, D), _MXU),
                   _sds((S, CC), F32), _sds((S, ATT), _MXU),
                   _sds((M, D), F32), _sds((M, D), F32), _sds((8, D), F32), _sds((8, D), F32)],
        compiler_params=_params(56),
    )(dx2, x1, qm, km, vm, g_q, wq, wo, wout, after)


def _mem_kv_bwd(dkm, dvm, mem, g_kv, memn, wkvT):
    M = mem.shape[0]

    def body(dkm_ref, dvm_ref, mem_ref, g_ref, mn_ref, w_ref, gw_ref, dg_ref):
        dkv = jnp.concatenate([dkm_ref[...], dvm_ref[...]], axis=1).astype(_MXU)
        gw_ref[...] = _tn(dkv, mn_ref[...]).astype(gw_ref.dtype)
        dmn = _nn(dkv, w_ref[...])
        memv = mem_ref[...]
        g = g_ref[...]
        _, r = _rms(memv, g)
        _, dg = _rms_bwd(dmn, memv, r, g)
        dg_ref[...] = jnp.broadcast_to(dg, dg_ref.shape)

    return _hbm_call(
        body, name="mem_kv_bwd", grid=(1,),
        in_specs=[_full((M, D)), _full((M, D)), _full((M, D)), _full((1, D)), _full((M, D)), _full((2 * D, D))],
        out_specs=[_full((2 * D, D)), _full((8, D))],
        out_shape=[_sds((2 * D, D), jnp.bfloat16), _sds((8, D), F32)],
        compiler_params=_params(48),
    )(dkm, dvm, mem, g_kv, memn, wkvT)


def _attn_bwd(qr, kr, vv, sink, dya, after):
    S = qr.shape[0]
    tq = ATTN_BLOCKS * BLK

    def body(q_ref, k_ref, v_ref, sk_ref, do_ref, after_ref, dq_ref, dk_ref, dv_ref, dsk_ref, sink_acc):
        _acc_init(pl.program_id(0), dk_ref, dv_ref, sink_acc)
        for b in range(ATTN_BLOCKS):
            i = pl.program_id(0) * ATTN_BLOCKS + b
            rows = slice(b * BLK, (b + 1) * BLK)
            start = _attn_window(i, S)
            kb = k_ref[pl.ds(start, 3 * BLK), :]
            vb = v_ref[pl.ds(start, 3 * BLK), :]
            q = q_ref[rows, :]
            do = do_ref[rows, :]
            mask = _attn_mask(i, start)
            dqs, dks, dvs = [], [], []
            for hk in range(NKV):
                hs = slice(hk * HD, (hk + 1) * HD)
                qs = _stack_heads(q, hk)
                dos = _stack_heads(do, hk)
                e, inv, es = _attn_probs(qs, kb[:, hs], sk_ref, hk, mask)
                p = e * inv
                dp = _nt(vb[:, hs], dos)
                delta = jnp.sum(dp * p, axis=0, keepdims=True)
                ds = (p * (dp - delta)).astype(_MXU)
                dqs.append(_unstack_heads_t(_tn(kb[:, hs], ds) * SCALE))
                dks.append(_nn(ds, qs) * SCALE)
                dvs.append(_nn(p, dos))
                sink_acc[hk:hk + 1, :] -= es * inv * delta
            dq_ref[rows, :] = jnp.concatenate(dqs, axis=1)
            dk_ref[pl.ds(start, 3 * BLK), :] += jnp.concatenate(dks, axis=1)
            dv_ref[pl.ds(start, 3 * BLK), :] += jnp.concatenate(dvs, axis=1)

        @pl.when(pl.program_id(0) == pl.num_programs(0) - 1)
        def _():
            for h in range(NQ):
                tot = jnp.sum(sink_acc[h // GRP:h // GRP + 1, (h % GRP) * BLK:(h % GRP + 1) * BLK], axis=1,
                              keepdims=True)
                dsk_ref[h:h + 1, :] = jnp.broadcast_to(tot, (1, LANES))

    return _hbm_call(
        body, name="attn_bwd", grid=(S // tq,),
        in_specs=[_rows(tq, ATT), _full((S, KVC)), _full((S, KVC)), _full((1, NQ)), _rows(tq, ATT), _AFTER],
        out_specs=[_rows(tq, ATT), _full((S, KVC)), _full((S, KVC)), _full((8, LANES))],
        out_shape=[_sds((S, ATT), F32), _sds((S, KVC), F32), _sds((S, KVC), F32), _sds((8, LANES), F32)],
        scratch_shapes=[pltpu.VMEM((8, GRP * BLK), F32)],
        compiler_params=_params(40),
    )(qr, kr, vv, sink, dya, after)


def _conv_norm_bwd(dyc, y, ln_g, ln_b, after, tm=512):
    S = y.shape[0]

    def body(dyc_ref, y_ref, g_ref, b_ref, after_ref, dy_ref, dg_ref, db_ref, dbdw_ref):
        i = pl.program_id(0)
        _acc_init(i, dg_ref, db_ref, dbdw_ref)
        g = g_ref[...]
        yh, rstd, z = _ln_silu(y_ref[...], g, b_ref[...])
        sig = _sigmoid(z)
        dz = dyc_ref[...] * (sig * (1.0 + z * (1.0 - sig)))
        dyh = dz * g
        dy = rstd * (dyh - jnp.mean(dyh, axis=-1, keepdims=True)
                     - yh * jnp.mean(dyh * yh, axis=-1, keepdims=True))
        dy_ref[...] = dy
        _acc_row(dg_ref, jnp.sum(dz * yh, axis=0, keepdims=True))
        _acc_row(db_ref, jnp.sum(dz, axis=0, keepdims=True))
        _acc_row(dbdw_ref, jnp.sum(dy, axis=0, keepdims=True))

    return _hbm_call(
        body, name="conv_norm_bwd", grid=(S // tm,),
        in_specs=[_rows(tm, CC), _rows(tm, CC), _full((1, CC)), _full((1, CC)), _AFTER],
        out_specs=[_rows(tm, CC), _full((8, CC)), _full((8, CC)), _full((8, CC))],
        out_shape=[_sds((S, CC), F32), _sds((8, CC), F32), _sds((8, CC), F32), _sds((8, CC), F32)],
        compiler_params=_params(40),
    )(dyc, y, ln_g, ln_b, after)


def _conv_bwd(dy, vg, uglu, w_dw):
    S = dy.shape[0]
    tm = min(CONV_TILE, S)

    def body(dy_hbm, vg_hbm, uglu_ref, w_ref, du_ref, gw_ref, dbin_ref, dyp_ref, vgp_ref, gacc_ref, sh_ref, dvg_ref,
             sems):
        i = pl.program_id(0)
        _acc_init(i, gacc_ref, dbin_ref)
        _fill_padded(i, dy_hbm, dyp_ref, sems.at[0], S)
        _fill_padded(i, vg_hbm, vgp_ref, sems.at[1], S)
        t0 = pl.multiple_of(i * tm, tm)
        _shifted_copies(dyp_ref, sh_ref.at[0], t0, tm)
        _shifted_copies(vgp_ref, sh_ref.at[1], t0, tm)
        _dwconv_tile(sh_ref.at[0], tm, w_ref, True, dvg_ref)
        dvg = dvg_ref[...]
        sub = CONV_ROWS
        for cl in range(CC // LANES):
            lanes = slice(cl * LANES, (cl + 1) * LANES)
            for k, b, off in _conv_taps():
                part = jnp.zeros((8, LANES), F32)
                for r in range(0, tm, sub):
                    prod = (sh_ref[0, cl, 0, CONV_HALO + r:CONV_HALO + r + sub, :]
                            * sh_ref[1, cl, b, off + r:off + r + sub, :])
                    part = part + jnp.sum(prod.reshape(sub // 8, 8, LANES), axis=0)
                gacc_ref[k, :, lanes] += part
        a_v = uglu_ref[:, :CC].astype(F32)
        sg = _sigmoid(uglu_ref[:, CC:].astype(F32))
        da = dvg * sg
        dgate = dvg * a_v * sg * (1.0 - sg)
        du = jnp.concatenate([da, dgate], axis=1)
        du_ref[...] = du.astype(du_ref.dtype)
        _acc_row(dbin_ref, jnp.sum(du, axis=0, keepdims=True))

        @pl.when(i == pl.num_programs(0) - 1)
        def _():
            gw_ref[...] = jnp.zeros(gw_ref.shape, F32)
            for k in range(TAPS):
                gw_ref[k:k + 1, :] = jnp.sum(gacc_ref[k], axis=0, keepdims=True)

    return _hbm_call(
        body, name="conv_bwd", grid=(S // tm,),
        in_specs=[pl.BlockSpec(memory_space=pl.ANY), pl.BlockSpec(memory_space=pl.ANY), _rows(tm, 2 * CC),
                  _full((TAPS, CC))],
        out_specs=[_rows(tm, 2 * CC), _full((TAPS + 1, CC)), _full((8, 2 * CC))],
        out_shape=[_sds((S, 2 * CC), _MXU), _sds((TAPS + 1, CC), F32), _sds((8, 2 * CC), F32)],
        scratch_shapes=[pltpu.VMEM((S + 2 * CONV_HALO, CC), F32), pltpu.VMEM((S + 2 * CONV_HALO, CC), F32),
                        pltpu.VMEM((TAPS, 8, CC), F32), _shift_scratch(tm, 2), pltpu.VMEM((tm, CC), F32),
                        pltpu.SemaphoreType.DMA((2,))],
        compiler_params=_params(56),
    )(dy, vg, uglu, w_dw)


def _in_proj_bwd(duglu, dqr, dkr, dv, cosq, sinq, x, dx1, g_mix, winT, tm=512):
    S = x.shape[0]
    NA = ATT + 2 * KVC

    def body(dug_ref, dq_ref, dk_ref, dv_ref, c_ref, s_ref, x_ref, dx1_ref, g_ref, w_ref,
             dx_ref, du_ref, dg_ref, dbin_ref):
        i = pl.program_id(0)
        _acc_init(i, dg_ref, dbin_ref)
        c = c_ref[...]
        s = s_ref[...]
        datt = jnp.concatenate([_rope_bwd(dq_ref[...], c, s), _rope_bwd(dk_ref[...], c, s), dv_ref[...]], axis=1)
        _acc_row(dbin_ref, jnp.sum(datt, axis=0, keepdims=True))
        du = jnp.concatenate([dug_ref[...], datt.astype(du_ref.dtype)], axis=1)
        du_ref[...] = du
        dh = _nn(du, w_ref[...])
        xv = x_ref[...]
        g = g_ref[...]
        _, r = _rms(xv, g)
        dxn, dg = _rms_bwd(dh, xv, r, g)
        dx_ref[...] = dx1_ref[...] + dxn
        _acc_row(dg_ref, dg)

    return _hbm_call(
        body, name="in_proj_bwd", grid=(S // tm,),
        in_specs=[_rows(tm, 2 * CC), _rows(tm, ATT), _rows(tm, KVC), _rows(tm, KVC), _rows(tm, LANES),
                  _rows(tm, LANES), _rows(tm, D), _rows(tm, D), _full((1, D)), _once((INC, D))],
        out_specs=[_rows(tm, D), _rows(tm, INC), _full((8, D)), _full((8, NA))],
        out_shape=[_sds((S, D), F32), _sds((S, INC), _MXU), _sds((8, D), F32), _sds((8, NA), F32)],
        compiler_params=_params(48),
    )(duglu, dqr, dkr, dv, cosq, sinq, x, dx1, g_mix, winT)


def _rope_tables(S):
    pos = jnp.arange(S, dtype=F32)
    inv_freq = ROPE_THETA ** (-jnp.arange(0, HD, 2, dtype=F32) / HD)
    ang = pos[:, None] * inv_freq[None, :]
    cos, sin = jnp.cos(ang), jnp.sin(ang)
    cosq = jnp.tile(jnp.concatenate([cos, cos], axis=1), (1, LANES // HD))
    sinq = jnp.tile(jnp.concatenate([-sin, sin], axis=1), (1, LANES // HD))
    return cosq, sinq


GATHER_GROUPS = (("w_inT", "w_dw"), ("w_out", "w_mem_kvT", "w_mem_q", "w_mem_o"), ("w_gateT", "w_upT"), ("w_down",))
ID_GATHER, ID_SIBLING, ID_CHIP = 1, 5, 8


def _local_step(x, mem, tgt, sp, gather, reduce):
    S = x.shape[0]
    cosq, sinq = _rope_tables(S)

    gather.forward(0, cosq)
    w0 = gather.finish(0, sinq)
    winT, w_dw_full = w0["w_inT"], w0["w_dw"]
    h, uglu, vg, qr, kr, vv = _in_proj_fwd(x, sp["g_mix"], winT, sp["b_in"], cosq, sinq)
    ya = _attn_fwd(qr, kr, vv, sp["attn_sink"])
    gather.forward(1, ya)
    y, yc = _conv_fwd(vg, w_dw_full, sp["b_dw"], sp["g_conv_ln"], sp["b_conv_ln"])
    w1 = gather.finish(1, yc)
    wout, wkvT, wq, wo = w1["w_out"], w1["w_mem_kvT"], w1["w_mem_q"], w1["w_mem_o"]
    km, vm, memn = _mem_kv_fwd(mem, sp["g_mem_kv"], wkvT)
    x1, ymix = _out_proj_fwd(x, yc, ya, wout, sp["b_out"])
    token = gather.forward(2, x1)
    x2, qm, om = _mem_attn_fwd(x1, sp["g_mem_q"], wq, km, vm, wo, token)
    w2 = gather.finish(2, om)
    gather.forward(3, om)
    wgT, wuT, wd = w2["w_gateT"], w2["w_upT"], gather.finish(3, om)["w_down"]
    gt, up, hf, dx3, dx3b, loss_acc, dg_final = _ffn_loss_fwd(x2, tgt, sp["g_ffn"], sp["g_final"], wgT, wuT, wd)

    dx2, dx2b, dgt, dup, act, dg_ffn = _ffn_bwd(dx3, x2, gt, up, sp["g_ffn"], wd, wgT, wuT)
    token = reduce.start(0, _wgrad([(dgt, hf), (dup, hf), (act, dx3b)], "wgrad_ffn", dx2))

    dx1, dx1b, hq, dqm, dyc, dya, dkm, dvm, dg_mem_q, db_out = _mem_attn_bwd(
        dx2, x1, qm, km, vm, sp["g_mem_q"], wq, wo, wout, token)
    token = reduce.middle(0, dx1)
    g_wout, g_wq, g_wo = _wgrad([(ymix, dx1b), (hq, dqm), (om, dx2b)], "wgrad_mix_mem", token)
    g_wkvT, dg_mem_kv = _mem_kv_bwd(dkm, dvm, mem, sp["g_mem_kv"], memn, wkvT)
    token = reduce.start(1, [g_wout, g_wq, g_wkvT, g_wo])

    dqr, dkr, dvv, dsink = _attn_bwd(qr, kr, vv, sp["attn_sink"], dya, token)
    token = reduce.middle(1, dqr)
    dy, dg_cln, db_cln, db_dw = _conv_norm_bwd(dyc, y, sp["g_conv_ln"], sp["b_conv_ln"], token)
    duglu, g_wdw, dbin_glu = _conv_bwd(dy, vg, uglu, w_dw_full)
    grad_x, du, dg_mix, dbin_att = _in_proj_bwd(duglu, dqr, dkr, dvv, cosq, sinq, x, dx1, sp["g_mix"], winT)
    token = reduce.start(2, _wgrad([(du, h)], "wgrad_in", grad_x))

    small = {
        "loss": loss_acc[0:1, :],
        "g_mix": dg_mix[0:1], "b_in": jnp.concatenate([dbin_glu[0:1], dbin_att[0:1]], axis=1),
        "b_dw": db_dw[0:1], "g_conv_ln": dg_cln[0:1], "b_conv_ln": db_cln[0:1],
        "attn_sink": dsink[:, 0].reshape(1, NQ), "b_out": db_out[0:1], "g_mem_q": dg_mem_q[0:1],
        "g_mem_kv": dg_mem_kv[0:1], "g_ffn": dg_ffn[0:1], "g_final": dg_final[0:1],
    }
    token = reduce.middle(2, token, reduce.small_zones(small, g_wdw))
    reduce.finish(0, token)
    reduce.finish(1, token)
    return grad_x


def _place():
    return lax.axis_index("x"), lax.axis_index("y"), lax.axis_index("c")


def _hbm_specs(n):
    return [pl.BlockSpec(memory_space=pl.ANY)] * n


_HBM = pl.BlockSpec(memory_space=pltpu.HBM)
_SEM = pl.BlockSpec(memory_space=pltpu.SEMAPHORE)
_ANY = pl.BlockSpec(memory_space=pl.ANY)
_EFFECT = pltpu.SideEffectType.DATAFLOW_SIDE_EFFECTING
_VMEM = pl.BlockSpec(memory_space=pltpu.VMEM)
_TOKEN = jax.ShapeDtypeStruct((8, LANES), F32)


def _dev_index(p):
    return 4 * p[0] + 2 * p[1] + p[2]


def _landing_zones(shards, me_idx):
    def body(m_ref, *refs):
        n = len(refs) // 2
        for k in range(n):
            refs[n + k][...] = refs[k][...].astype(jnp.bfloat16)

    return pl.pallas_call(
        body, name="landing_zones",
        grid_spec=pltpu.PrefetchScalarGridSpec(
            num_scalar_prefetch=1, grid=(1,),
            in_specs=[pl.BlockSpec(s.shape, lambda i, m: (0, 0)) for s in shards],
            out_specs=[pl.BlockSpec((None,) + s.shape, lambda i, m: (m[0], 0, 0)) for s in shards]),
        out_shape=[pltpu.HBM((NDEV,) + s.shape, jnp.bfloat16) for s in shards],
        compiler_params=_params(40),
    )(me_idx, *shards)


def _gather_start(groups):
    sizes = [len(g) for g in groups]
    flat = [z for g in groups for z in g]
    n = len(flat)

    def body(*refs):
        ins = refs[:n]
        sems = refs[2 * n:]
        x, y, c = _place()
        me = _dev_index((x, y, c))
        peers = [(x, y, 1 - c), (1 - x, y, c), (x, 1 - y, c), (1 - x, 1 - y, c)]
        base = 0
        for gi, size in enumerate(sizes):
            send_sems, recv_sems = sems[2 * gi], sems[2 * gi + 1]
            for k in range(size):
                zone = ins[base + k]
                for j, peer in enumerate(peers):
                    pltpu.make_async_remote_copy(
                        src_ref=zone.at[me], dst_ref=zone.at[me], send_sem=send_sems.at[4 * k + j],
                        recv_sem=recv_sems.at[4 * k + j], device_id=peer, device_id_type=MESH).start()
            base += size

    out_shape = [pltpu.HBM(z.shape, z.dtype) for z in flat]
    out_specs = [_HBM] * n
    for size in sizes:
        out_shape += [pltpu.SemaphoreType.DMA((4 * size,))] * 2
        out_specs += [_SEM, _SEM]
    res = pl.pallas_call(
        body, name="gather_start", in_specs=[_HBM] * n, out_specs=out_specs, out_shape=out_shape,
        input_output_aliases={k: k for k in range(n)},
        compiler_params=pltpu.CompilerParams(has_side_effects=_EFFECT),
    )(*[pltpu.with_memory_space_constraint(z, pltpu.HBM) for z in flat])
    out, base = [], 0
    for gi, size in enumerate(sizes):
        out.append((list(res[base:base + size]), res[n + 2 * gi], res[n + 2 * gi + 1]))
        base += size
    return out


def _gather_forward(zones, send_sems, recv_sems, after, collective_id, name):
    n = len(zones)

    def body(*refs):
        ins = refs[:n]
        send_ref, recv_ref = refs[n], refs[n + 1]
        fsend, frecv = refs[2 * n + 3], refs[2 * n + 4]
        x, y, c = _place()
        sibling = (x, y, 1 - c)
        barrier = pltpu.get_barrier_semaphore()
        pl.semaphore_signal(barrier, inc=1, device_id=sibling, device_id_type=MESH)
        pl.semaphore_wait(barrier, 1)
        chips = [(1 - x, y), (x, 1 - y), (1 - x, 1 - y)]
        for j, chip in enumerate(chips):
            for k in range(n):
                slot = ins[k].at[_dev_index((*chip, c))]
                pltpu.make_async_remote_copy(
                    src_ref=slot, dst_ref=slot, send_sem=send_ref.at[4 * k + 1 + j],
                    recv_sem=recv_ref.at[4 * k + 1 + j], device_id=sibling, device_id_type=MESH).wait_recv()
                pltpu.make_async_remote_copy(
                    src_ref=slot, dst_ref=slot, send_sem=fsend.at[3 * k + j], recv_sem=frecv.at[3 * k + j],
                    device_id=sibling, device_id_type=MESH).start()
        for k in range(n):
            own = ins[k].at[_dev_index((x, y, c))]
            theirs = ins[k].at[_dev_index(sibling)]
            pltpu.make_async_remote_copy(
                src_ref=theirs, dst_ref=theirs, send_sem=send_ref.at[4 * k], recv_sem=recv_ref.at[4 * k],
                device_id=sibling, device_id_type=MESH).wait_recv()
            for j in range(4):
                pltpu.make_async_remote_copy(
                    src_ref=own, dst_ref=own, send_sem=send_ref.at[4 * k + j], recv_sem=recv_ref.at[4 * k + j],
                    device_id=sibling, device_id_type=MESH).wait_send()
        refs[2 * n + 5][...] = jnp.zeros(_TOKEN.shape, _TOKEN.dtype)

    res = pl.pallas_call(
        body, name=name, in_specs=[_HBM] * n + [_SEM, _SEM, _ANY],
        out_specs=[_HBM] * n + [_SEM, _SEM, _VMEM],
        out_shape=[pltpu.HBM(z.shape, z.dtype) for z in zones] + [pltpu.SemaphoreType.DMA((3 * n,))] * 2 + [_TOKEN],
        input_output_aliases={k: k for k in range(n)},
        compiler_params=pltpu.CompilerParams(has_side_effects=_EFFECT, collective_id=collective_id),
    )(*zones, send_sems, recv_sems, after)
    return (list(res[:n]), res[n], res[n + 1]), res[n + 2]


def _gather_finish(zones, fsend, frecv, after, name):
    n = len(zones)

    def body(*refs):
        ins = refs[:n]
        fsend_ref, frecv_ref = refs[n], refs[n + 1]
        x, y, c = _place()
        sibling = (x, y, 1 - c)
        chips = [(1 - x, y), (x, 1 - y), (1 - x, 1 - y)]
        for j, chip in enumerate(chips):
            for k in range(n):
                mine = ins[k].at[_dev_index((*chip, c))]
                theirs = ins[k].at[_dev_index((*chip, 1 - c))]
                pltpu.make_async_remote_copy(
                    src_ref=theirs, dst_ref=theirs, send_sem=fsend_ref.at[3 * k + j],
                    recv_sem=frecv_ref.at[3 * k + j], device_id=sibling, device_id_type=MESH).wait_recv()
                pltpu.make_async_remote_copy(
                    src_ref=mine, dst_ref=mine, send_sem=fsend_ref.at[3 * k + j],
                    recv_sem=frecv_ref.at[3 * k + j], device_id=sibling, device_id_type=MESH).wait_send()

    res = pl.pallas_call(
        body, name=name, in_specs=[_HBM] * n + [_SEM, _SEM, _ANY], out_specs=[_HBM] * n,
        out_shape=[pltpu.HBM(z.shape, z.dtype) for z in zones],
        input_output_aliases={k: k for k in range(n)},
        compiler_params=pltpu.CompilerParams(has_side_effects=_EFFECT),
    )(*zones, fsend, frecv, after)
    return list(res)


class _Gather:
    def __init__(self, started):
        self._phase = list(started)

    def forward(self, gi, after):
        zones, send_sems, recv_sems = self._phase[gi]
        self._phase[gi], token = _gather_forward(zones, send_sems, recv_sems, after, ID_GATHER + gi,
                                                  "gather_forward_%d" % gi)
        return token

    def finish(self, gi, after):
        zones, fsend, frecv = self._phase[gi]
        out = {}
        for name, z in zip(GATHER_GROUPS[gi], _gather_finish(zones, fsend, frecv, after, "gather_finish_%d" % gi)):
            if name == "w_dw":
                out[name] = jnp.transpose(z, (1, 0, 2)).reshape(TAPS + 1, CC)[:TAPS]
            else:
                out[name] = z.reshape(NDEV * z.shape[1], D)
        return out


def _handshake(peers):
    barrier = pltpu.get_barrier_semaphore()
    for peer in peers:
        pl.semaphore_signal(barrier, inc=1, device_id=peer, device_id_type=MESH)
    pl.semaphore_wait(barrier, len(peers))


def _sibling_copies(srcs, lands, send_sems, recv_sems):
    x, y, c = _place()
    return [pltpu.make_async_remote_copy(
        src_ref=srcs[k].at[2 * s + (1 - c)], dst_ref=lands[k].at[s], send_sem=send_sems.at[4 * k + s],
        recv_sem=recv_sems.at[4 * k + s], device_id=(x, y, 1 - c), device_id_type=MESH)
        for k in range(len(srcs)) for s in range(4)]


def _sibling_start(gbs, collective_id, name):
    n = len(gbs)
    lands = [lax.empty((4,) + g.shape[1:], g.dtype) for g in gbs]

    def body(*refs):
        x, y, c = _place()
        _handshake([(x, y, 1 - c)])
        for cp in _sibling_copies(refs[:n], refs[n:2 * n], refs[4 * n], refs[4 * n + 1]):
            cp.start()
        refs[4 * n + 2][...] = jnp.zeros(_TOKEN.shape, _TOKEN.dtype)

    res = pl.pallas_call(
        body, name=name, in_specs=[_HBM] * (2 * n), out_specs=[_HBM] * (2 * n) + [_SEM, _SEM, _VMEM],
        out_shape=[pltpu.HBM(a.shape, a.dtype) for a in gbs + lands] + [pltpu.SemaphoreType.DMA((4 * n,))] * 2
        + [_TOKEN],
        input_output_aliases={k: k for k in range(2 * n)},
        compiler_params=pltpu.CompilerParams(has_side_effects=_EFFECT, collective_id=collective_id),
    )(*[pltpu.with_memory_space_constraint(a, pltpu.HBM) for a in gbs + lands])
    return list(res[:n]), list(res[n:2 * n]), res[2 * n], res[2 * n + 1], res[2 * n + 2]


def _sibling_wait(gbs, lands, send_sems, recv_sems, after, name):
    n = len(gbs)

    def body(*refs):
        for cp in _sibling_copies(refs[:n], refs[n:2 * n], refs[2 * n], refs[2 * n + 1]):
            cp.wait_recv()
            cp.wait_send()

    res = pl.pallas_call(
        body, name=name, in_specs=[_HBM] * (2 * n) + [_SEM, _SEM, _ANY], out_specs=[_HBM] * (2 * n),
        out_shape=[pltpu.HBM(a.shape, a.dtype) for a in gbs + lands],
        input_output_aliases={k: k for k in range(2 * n)},
        compiler_params=pltpu.CompilerParams(has_side_effects=_EFFECT),
    )(*gbs, *lands, send_sems, recv_sems, after)
    return list(res[:n]), list(res[n:])


def _pair_sum(gfs, ts, c_idx, name):
    n = len(gfs)
    halves = 2

    def body(c_ref, *refs):
        for k in range(n):
            refs[2 * n + k][...] = (refs[k][...].astype(F32) + refs[n + k][...].astype(F32)).astype(jnp.bfloat16)

    in_specs, out_specs, out_shape = [], [], []
    for g in gfs:
        rh = g.shape[1] // halves
        in_specs.append(pl.BlockSpec((None, rh, D), lambda s, h, c_ref: (2 * s + c_ref[0], h, 0)))
    for g in gfs:
        rh = g.shape[1] // halves
        in_specs.append(pl.BlockSpec((None, rh, D), lambda s, h, c_ref: (s, h, 0)))
        out_specs.append(pl.BlockSpec((None, rh, D), lambda s, h, c_ref: (s, h, 0)))
        out_shape.append(_sds((4, g.shape[1], D), jnp.bfloat16))
    return pl.pallas_call(
        body, name=name,
        grid_spec=pltpu.PrefetchScalarGridSpec(num_scalar_prefetch=1, grid=(4, halves), in_specs=in_specs,
                                               out_specs=out_specs),
        out_shape=out_shape,
        compiler_params=pltpu.CompilerParams(dimension_semantics=("arbitrary", "arbitrary"),
                                             vmem_limit_bytes=48 * 2 ** 20),
    )(c_idx, *gfs, *ts)


def _chip_copies(pbs, lands, smalls, send_sems, recv_sems):
    x, y, c = _place()
    own = 2 * x + y
    me = _dev_index((x, y, c))
    chips = [(1 - x, y), (x, 1 - y), (1 - x, 1 - y)]
    n = len(pbs)
    copies = [pltpu.make_async_remote_copy(
        src_ref=pbs[k].at[2 * chip[0] + chip[1]], dst_ref=lands[k].at[own], send_sem=send_sems.at[3 * k + j],
        recv_sem=recv_sems.at[3 * k + j], device_id=(*chip, c), device_id_type=MESH)
        for k in range(n) for j, chip in enumerate(chips)]
    peers = [(x, y, 1 - c)] + [(*chip, cc) for chip in chips for cc in (c, 1 - c)]
    copies += [pltpu.make_async_remote_copy(
        src_ref=smalls[t].at[me], dst_ref=smalls[t].at[me], send_sem=send_sems.at[3 * n + 7 * t + j],
        recv_sem=recv_sems.at[3 * n + 7 * t + j], device_id=peer, device_id_type=MESH)
        for t in range(len(smalls)) for j, peer in enumerate(peers)]
    return copies


def _chip_start(pbs, smalls, collective_id, name):
    n, m = len(pbs), len(smalls)
    lands = [lax.empty(p.shape, p.dtype) for p in pbs]
    arrays = list(pbs) + lands + list(smalls)
    na = len(arrays)
    nsem = 3 * n + 7 * m

    def body(*refs):
        x, y, c = _place()
        chips = [(1 - x, y), (x, 1 - y), (1 - x, 1 - y)]
        peers = [(*chip, c) for chip in chips]
        if m:
            peers += [(x, y, 1 - c)] + [(*chip, 1 - c) for chip in chips]
        _handshake(peers)
        for cp in _chip_copies(refs[:n], refs[n:2 * n], refs[2 * n:na], refs[2 * na], refs[2 * na + 1]):
            cp.start()
        refs[2 * na + 2][...] = jnp.zeros(_TOKEN.shape, _TOKEN.dtype)

    res = pl.pallas_call(
        body, name=name, in_specs=[_HBM] * na, out_specs=[_HBM] * na + [_SEM, _SEM, _VMEM],
        out_shape=[pltpu.HBM(a.shape, a.dtype) for a in arrays] + [pltpu.SemaphoreType.DMA((nsem,))] * 2 + [_TOKEN],
        input_output_aliases={k: k for k in range(na)},
        compiler_params=pltpu.CompilerParams(has_side_effects=_EFFECT, collective_id=collective_id),
    )(*[pltpu.with_memory_space_constraint(a, pltpu.HBM) for a in arrays])
    return (list(res[:n]), list(res[n:2 * n]), list(res[2 * n:na]), res[na], res[na + 1]), res[na + 2]


def _chip_wait(pbs, lands, smalls, send_sems, recv_sems, after, name):
    n = len(pbs)
    arrays = list(pbs) + list(lands) + list(smalls)
    na = len(arrays)

    def body(*refs):
        for cp in _chip_copies(refs[:n], refs[n:2 * n], refs[2 * n:na], refs[na], refs[na + 1]):
            cp.wait_recv()
            cp.wait_send()

    res = pl.pallas_call(
        body, name=name, in_specs=[_HBM] * na + [_SEM, _SEM, _ANY], out_specs=[_HBM] * na,
        out_shape=[pltpu.HBM(a.shape, a.dtype) for a in arrays],
        input_output_aliases={k: k for k in range(na)},
        compiler_params=pltpu.CompilerParams(has_side_effects=_EFFECT),
    )(*arrays, send_sems, recv_sems, after)
    return list(res[:n]), list(res[n:2 * n]), list(res[2 * n:])


def _chip_sum(pbs, qs, own_idx, states, name):
    n = len(pbs)
    halves = 2
    n_in = 4 * n + 3 * sum(s is not None for s in states)

    def body(o_ref, *refs):
        nxt_in, nxt_out = 4 * n, n_in
        for k in range(n):
            acc = refs[4 * k][...].astype(F32)
            for j in range(1, 4):
                acc = acc + refs[4 * k + j][...].astype(F32)
            refs[nxt_out][...] = acc
            nxt_out += 1
            if states[k] is not None:
                w_ref, m_ref, v_ref = refs[nxt_in:nxt_in + 3]
                nxt_in += 3
                d, nm, nv = _adamw_math(w_ref[...], acc, m_ref[...], v_ref[...])
                refs[nxt_out][...] = d
                refs[nxt_out + 1][...] = nm
                refs[nxt_out + 2][...] = nv
                nxt_out += 3

    def other(j):
        return lambda h, o: ((o[0] + j) % 4, h, 0)

    def rows(r):
        return pl.BlockSpec((r // halves, D), lambda h, o: (h, 0))

    in_specs, ins = [], []
    for p, q in zip(pbs, qs):
        rh = p.shape[1] // halves
        in_specs.append(pl.BlockSpec((None, rh, D), lambda h, o: (o[0], h, 0)))
        ins.append(p)
        for j in range(1, 4):
            in_specs.append(pl.BlockSpec((None, rh, D), other(j)))
            ins.append(q)
    out_specs, out_shape, counts = [], [], []
    for p, s in zip(pbs, states):
        r = p.shape[1]
        if s is not None:
            in_specs += [rows(r)] * 3
            ins += list(s)
        counts.append(1 if s is None else 4)
        out_specs += [rows(r)] * counts[-1]
        out_shape += [_sds((r, D), F32)] * counts[-1]
    res = pl.pallas_call(
        body, name=name,
        grid_spec=pltpu.PrefetchScalarGridSpec(num_scalar_prefetch=1, grid=(halves,), in_specs=in_specs,
                                               out_specs=out_specs),
        out_shape=out_shape,
        compiler_params=pltpu.CompilerParams(dimension_semantics=("arbitrary",), vmem_limit_bytes=56 * 2 ** 20),
    )(own_idx, *ins)
    out, base = [], 0
    for c in counts:
        out.append(tuple(res[base:base + c]))
        base += c
    return out


REDUCE_GROUPS = (("w_gateT", "w_upT", "w_down"), ("w_out", "w_mem_q", "w_mem_kvT", "w_mem_o"), ("w_inT",))


class _Reduce:
    def __init__(self, states):
        x, y, c = _place()
        self._c = c.astype(jnp.int32).reshape(1)
        self._own = (2 * x + y).astype(jnp.int32).reshape(1)
        self._me = _dev_index((x, y, c))
        self._state = {}
        self._adam = states
        self.results = {}
        self.smalls = None

    def start(self, gi, wgrads):
        gbs = [g.reshape(NDEV, g.shape[0] // NDEV, D) for g in wgrads]
        gbs, lands, send_sems, recv_sems, token = _sibling_start(gbs, ID_SIBLING + gi, "reduce_sibling_start_%d" % gi)
        self._state[gi] = (gbs, lands, send_sems, recv_sems)
        return token

    def middle(self, gi, after, smalls=()):
        gbs, lands, send_sems, recv_sems = self._state[gi]
        gbs, ts = _sibling_wait(gbs, lands, send_sems, recv_sems, after, "reduce_sibling_wait_%d" % gi)
        pbs = _pair_sum(gbs, ts, self._c, "reduce_pair_sum_%d" % gi)
        self._state[gi], token = _chip_start(pbs, list(smalls), ID_CHIP + gi, "reduce_chip_start_%d" % gi)
        return token

    def finish(self, gi, after):
        pbs, lands, smalls, send_sems, recv_sems = self._state[gi]
        pbs, qs, smalls = _chip_wait(pbs, lands, smalls, send_sems, recv_sems, after, "reduce_chip_wait_%d" % gi)
        names = REDUCE_GROUPS[gi]
        states = [self._adam.get(n) for n in names]
        self.results.update(zip(names, _chip_sum(pbs, qs, self._own, states, "reduce_update_%d" % gi)))
        if smalls:
            self.smalls = smalls

    def small_zones(self, small, g_wdw):
        pack = jnp.concatenate(
            [jnp.pad(small[name], ((0, 0), (0, lanes - small[name].shape[1]))) for name, lanes, _ in PACK], axis=1)
        zones = []
        for a in (pack, g_wdw):
            zone = lax.empty((NDEV,) + a.shape, a.dtype)
            zones.append(lax.dynamic_update_slice(zone, a[None], (self._me,) + (0,) * a.ndim))
        return zones


def _adamw_math(w, g, m, v):
    m = ADAM_B1 * m + (1.0 - ADAM_B1) * g
    v = ADAM_B2 * v + (1.0 - ADAM_B2) * (g * g)
    m_hat = m / (1.0 - ADAM_B1 ** ADAM_STEP)
    v_hat = v / (1.0 - ADAM_B2 ** ADAM_STEP)
    delta = -ADAM_LR * (m_hat / (jnp.sqrt(v_hat) + ADAM_EPS) + ADAM_WD * w)
    return delta, m, v


def _adamw(w, g, m, v, name):
    R, C = w.shape
    rc = 256 if R % 256 == 0 and R > 256 else R

    def body(w_ref, g_ref, m_ref, v_ref, d_ref, nm_ref, nv_ref):
        d, nm, nv = _adamw_math(w_ref[...], g_ref[...], m_ref[...], v_ref[...])
        d_ref[...] = d
        nm_ref[...] = nm
        nv_ref[...] = nv

    return _hbm_call(
        body, name=name, grid=(R // rc,),
        in_specs=[_rows(rc, C)] * 4, out_specs=[_rows(rc, C)] * 3,
        out_shape=[_sds((R, C), F32)] * 3,
        compiler_params=_params(40),
    )(w, g, m, v)


PACK = (("loss", LANES, 1), ("g_mix", D, D), ("b_in", INC, INC), ("b_dw", CC, CC), ("g_conv_ln", CC, CC),
        ("b_conv_ln", CC, CC), ("attn_sink", LANES, NQ), ("b_out", D, D), ("g_mem_q", D, D), ("g_mem_kv", D, D),
        ("g_ffn", D, D), ("g_final", D, D))
PACK_LANES = sum(p[1] for p in PACK)


def _small_update(packs, gdws, ws, ms, vs):
    names = [p[0] for p in PACK[1:]]
    n = len(names)

    def body(*refs):
        packs_ref, gdws_ref = refs[0], refs[1]
        w_refs, m_refs, v_refs = refs[2:2 + n], refs[2 + n:2 + 2 * n], refs[2 + 2 * n:2 + 3 * n]
        outs = refs[2 + 3 * n:]
        tot = packs_ref[0]
        gdw = gdws_ref[0]
        for s in range(1, NDEV):
            tot = tot + packs_ref[s]
            gdw = gdw + gdws_ref[s]
        outs[0][...] = tot[:, 0:LANES]
        outs[1][...] = gdw
        off = LANES
        for k, (_, lanes, used) in enumerate(PACK[1:]):
            g = tot[:, off:off + used]
            off += lanes
            d, nm, nv = _adamw_math(w_refs[k][...], g, m_refs[k][...], v_refs[k][...])
            o = outs[2 + 4 * k:6 + 4 * k]
            o[0][...] = g
            o[1][...] = d
            o[2][...] = nm
            o[3][...] = nv

    def full(a):
        nd = len(a.shape)
        return pl.BlockSpec(a.shape, lambda i: (0,) * nd)

    ins = [packs, gdws, *ws, *ms, *vs]
    out_shape = [_sds((1, LANES), F32), _sds(gdws.shape[1:], F32)]
    for w in ws:
        out_shape += [_sds(w.shape, F32)] * 4
    return pl.pallas_call(
        body, name="small_update", grid=(1,),
        in_specs=[full(a) for a in ins], out_specs=[full(o) for o in out_shape], out_shape=out_shape,
        compiler_params=_params(40),
    )(*ins)


def kernel(x, mem, g_mix, w_in, b_in, w_dw, b_dw, g_conv_ln, b_conv_ln, attn_sink, w_out, b_out, g_mem_q, g_mem_kv, w_mem_q, w_mem_kv, w_mem_o, g_ffn, w_gate, w_up, w_down, g_final, loss_target, m_g_mix, m_w_in, m_b_in, m_w_dw, m_b_dw, m_g_conv_ln, m_b_conv_ln, m_attn_sink, m_w_out, m_b_out, m_g_mem_q, m_g_mem_kv, m_w_mem_q, m_w_mem_kv, m_w_mem_o, m_g_ffn, m_w_gate, m_w_up, m_w_down, m_g_final, v_g_mix, v_w_in, v_b_in, v_w_dw, v_b_dw, v_g_conv_ln, v_b_conv_ln, v_attn_sink, v_w_out, v_b_out, v_g_mem_q, v_g_mem_kv, v_w_mem_q, v_w_mem_kv, v_w_mem_o, v_g_ffn, v_w_gate, v_w_up, v_w_down, v_g_final):
    given = dict(locals())
    bf = jnp.bfloat16
    xi, yi, ci = _place()
    me = 4 * xi + 2 * yi + ci

    transposed = {"w_inT": w_in, "w_mem_kvT": w_mem_kv, "w_gateT": w_gate, "w_upT": w_up}
    plain = {"w_out": w_out, "w_mem_q": w_mem_q, "w_mem_o": w_mem_o, "w_down": w_down}
    shards = [transposed[n][0].T if n in transposed else plain[n][0] for n in W_NAMES]
    zones = dict(zip(W_NAMES, _landing_zones(shards, me.astype(jnp.int32).reshape(1))))
    dw = jnp.pad(w_dw[0], ((0, 1), (0, 0)))
    zones["w_dw"] = lax.dynamic_update_slice(lax.empty((NDEV,) + dw.shape, dw.dtype), dw[None], (me, 0, 0))
    gather = _Gather(_gather_start([[zones[n] for n in names] for names in GATHER_GROUPS]))

    sp = {n: given[n] for n in ("g_mix", "b_in", "b_dw", "g_conv_ln", "b_conv_ln", "attn_sink", "b_out", "g_mem_q",
                                "g_mem_kv", "g_ffn")}
    sp["g_final"] = g_final.reshape(1, D)
    frames = {"w_inT": ("w_in", True), "w_gateT": ("w_gate", True), "w_upT": ("w_up", True),
              "w_out": ("w_out", False), "w_mem_q": ("w_mem_q", False), "w_mem_o": ("w_mem_o", False),
              "w_down": ("w_down", False)}
    states = {}
    for frame, (n, transposed) in frames.items():
        wmv = (given[n][0], given["m_" + n][0], given["v_" + n][0])
        states[frame] = tuple(a.T for a in wmv) if transposed else wmv
    reduce = _Reduce(states)
    grad_x = _local_step(x[0], mem[0], loss_target[0], sp, gather, reduce)
    reduce.finish(2, reduce.results["w_out"][1])

    grads, deltas, new_m, new_v = {}, {}, {}, {}
    for frame, (n, transposed) in frames.items():
        outs = [a.T if transposed else a for a in reduce.results[frame]]
        grads[n], deltas[n], new_m[n], new_v[n] = [a[None] for a in outs]

    def update(n, g):
        d, nm, nv = _adamw(given[n][0], g, given["m_" + n][0], given["v_" + n][0], "adamw_" + n)
        grads[n], deltas[n], new_m[n], new_v[n] = g[None], d[None], nm[None], nv[None]

    update("w_mem_kv", reduce.results["w_mem_kvT"][0].T)

    packs, gdws = reduce.smalls
    names_small = [p[0] for p in PACK[1:]]

    def as_row(a):
        return a.reshape(1, -1)

    res = _small_update(packs, gdws, [as_row(given[n]) for n in names_small],
                        [as_row(given["m_" + n]) for n in names_small],
                        [as_row(given["v_" + n]) for n in names_small])
    loss = res[0][0, 0]
    for k, n in enumerate(names_small):
        shape = given[n].shape
        g, d, nm, nv = res[2 + 4 * k:6 + 4 * k]
        grads[n], deltas[n], new_m[n], new_v[n] = (g.reshape(shape), d.reshape(shape), nm.reshape(shape),
                                                   nv.reshape(shape))
    update("w_dw", lax.dynamic_slice(res[1], (0, me * (CC // NDEV)), (TAPS, CC // NDEV)))

    order = ("g_mix", "w_in", "b_in", "w_dw", "b_dw", "g_conv_ln", "b_conv_ln", "attn_sink", "w_out", "b_out",
             "g_mem_q", "g_mem_kv", "w_mem_q", "w_mem_kv", "w_mem_o", "g_ffn", "w_gate", "w_up", "w_down", "g_final")
    return (loss, grad_x[None], *[grads[n] for n in order], *[deltas[n] for n in order],
            *[new_m[n] for n in order], *[new_v[n] for n in order])
```

```python
import functools

import jax
import jax.numpy as jnp
from jax import lax
from jax.experimental import pallas as pl
from jax.experimental.pallas import tpu as pltpu

D = 1024
CC = 512
HD = 64
NQ = 8
NKV = 2
GRP = NQ // NKV
ATT = NQ * HD
KVC = NKV * HD
BLK = 128
MH = 4
MHD = D // MH
F = 2816
FCH = 256
FCH_BWD = F
INC = 2 * CC + ATT + 2 * KVC
TAPS = 31
EPS = 1e-6
ROPE_THETA = 10000.0
SCALE = HD ** -0.5
MSCALE = MHD ** -0.5
LANES = 128
NDEV = 8

ADAM_LR = 0.001
ADAM_B1 = 0.9
ADAM_B2 = 0.999
ADAM_EPS = 1e-08
ADAM_WD = 0.01
ADAM_STEP = 10

F32 = jnp.float32
_MXU = jnp.bfloat16
MESH = pl.DeviceIdType.MESH

W_NAMES = ("w_inT", "w_out", "w_mem_q", "w_mem_kvT", "w_mem_o", "w_gateT", "w_upT", "w_down")
W_ROWS = (INC // NDEV, D // NDEV, D // NDEV, 2 * D // NDEV, D // NDEV, F // NDEV, F // NDEV, F // NDEV)


def _mm(a, b, ca, cb):
    return lax.dot_general(a.astype(_MXU), b.astype(_MXU), (((ca,), (cb,)), ((), ())),
                           preferred_element_type=F32)


def _nn(a, b):
    return _mm(a, b, 1, 0)


def _nt(a, b):
    return _mm(a, b, 1, 1)


def _tn(a, b):
    return _mm(a, b, 0, 0)


def _sigmoid(x):
    return 0.5 * jnp.tanh(0.5 * x) + 0.5


def _rms(x, g):
    r = lax.rsqrt(jnp.mean(x * x, axis=-1, keepdims=True) + EPS)
    return x * r * g, r


def _rms_bwd(dy, x, r, g):
    xh = x * r
    dg = jnp.sum(dy * xh, axis=0, keepdims=True)
    dxh = dy * g
    dx = r * (dxh - xh * jnp.mean(dxh * xh, axis=-1, keepdims=True))
    return dx, dg


def _rope(t, c, s):
    n = t.shape[1]
    reps = n // LANES
    if reps > 1:
        c = jnp.tile(c, (1, reps))
        s = jnp.tile(s, (1, reps))
    lane = lax.broadcasted_iota(jnp.int32, t.shape, 1)
    first = (lane & (HD - 1)) < (HD // 2)
    partner = jnp.where(first, pltpu.roll(t, n - HD // 2, 1), pltpu.roll(t, HD // 2, 1))
    return t * c + partner * s


def _rope_bwd(dt, c, s):
    n = dt.shape[1]
    reps = n // LANES
    if reps > 1:
        c = jnp.tile(c, (1, reps))
        s = jnp.tile(s, (1, reps))
    lane = lax.broadcasted_iota(jnp.int32, dt.shape, 1)
    first = (lane & (HD - 1)) < (HD // 2)
    ds = dt * s
    partner = jnp.where(first, pltpu.roll(ds, n - HD // 2, 1), pltpu.roll(ds, HD // 2, 1))
    return dt * c + partner


def _acc_init(i, *refs):
    @pl.when(i == 0)
    def _():
        for r in refs:
            r[...] = jnp.zeros(r.shape, r.dtype)


def _acc_row(ref, v):
    ref[...] += jnp.broadcast_to(v, ref.shape)


def _rows(tm, n):
    return pl.BlockSpec((tm, n), lambda i: (i, 0))


def _full(shape):
    nd = len(shape)
    return pl.BlockSpec(shape, lambda i: (0,) * nd)


def _once(shape):
    nd = len(shape)
    return pl.BlockSpec(shape, lambda i: (0,) * nd, pipeline_mode=pl.Buffered(1))


def _params(vmem_mb):
    return pltpu.CompilerParams(dimension_semantics=("arbitrary",), vmem_limit_bytes=vmem_mb * 2 ** 20)


def _sds(shape, dtype):
    return pltpu.HBM(shape, dtype)


def _hbm_call(*args, **kwargs):
    call = pl.pallas_call(*args, **kwargs)
    return lambda *ops: call(*[pltpu.with_memory_space_constraint(o, pltpu.HBM) for o in ops])


def _mem_kv_fwd(mem, g_kv, wkvT):
    M = mem.shape[0]

    def body(mem_ref, g_ref, w_ref, km_ref, vm_ref, mn_ref):
        y, _ = _rms(mem_ref[...], g_ref[...])
        kv = _nt(y, w_ref[...])
        km_ref[...] = kv[:, :D].astype(km_ref.dtype)
        vm_ref[...] = kv[:, D:].astype(vm_ref.dtype)
        mn_ref[...] = y.astype(mn_ref.dtype)

    return _hbm_call(
        body, name="mem_kv_fwd", grid=(1,),
        in_specs=[_full((M, D)), _full((1, D)), _full((2 * D, D))],
        out_specs=[_full((M, D))] * 3,
        out_shape=[_sds((M, D), _MXU)] * 3,
        compiler_params=_params(40),
    )(mem, g_kv, wkvT)


def _in_proj_fwd(x, g_mix, winT, b_in, cosq, sinq, tm=512):
    S = x.shape[0]

    def body(x_ref, g_ref, w_ref, b_ref, c_ref, s_ref, h_ref, uglu_ref, vg_ref, q_ref, k_ref, v_ref):
        h, _ = _rms(x_ref[...], g_ref[...])
        hb = h.astype(h_ref.dtype)
        h_ref[...] = hb
        u = _nt(hb, w_ref[...]) + b_ref[...]
        uglu_ref[...] = u[:, :2 * CC].astype(uglu_ref.dtype)
        vg_ref[...] = u[:, :CC] * _sigmoid(u[:, CC:2 * CC])
        c = c_ref[...]
        s = s_ref[...]
        q_ref[...] = (_rope(u[:, 2 * CC:2 * CC + ATT], c, s) * SCALE).astype(q_ref.dtype)
        k_ref[...] = _rope(u[:, 2 * CC + ATT:2 * CC + ATT + KVC], c, s).astype(k_ref.dtype)
        v_ref[...] = u[:, 2 * CC + ATT + KVC:].astype(v_ref.dtype)

    return _hbm_call(
        body, name="in_proj_fwd", grid=(S // tm,),
        in_specs=[_rows(tm, D), _full((1, D)), _full((INC, D)), _full((1, INC)), _rows(tm, LANES), _rows(tm, LANES)],
        out_specs=[_rows(tm, D), _rows(tm, 2 * CC), _rows(tm, CC), _rows(tm, ATT), _rows(tm, KVC), _rows(tm, KVC)],
        out_shape=[_sds((S, D), _MXU), _sds((S, 2 * CC), _MXU), _sds((S, CC), F32), _sds((S, ATT), _MXU),
                   _sds((S, KVC), _MXU), _sds((S, KVC), _MXU)],
        compiler_params=_params(48),
    )(x, g_mix, winT, b_in, cosq, sinq)


CONV_HALO = 16


CONV_TILE = 256


def _shift_scratch(tm, sets):
    return pltpu.VMEM((sets, CC // LANES, 8, tm + 24, LANES), F32)


def _shifted_copies(pad_ref, sh_ref, t0, tm):
    for cl in range(CC // LANES):
        blk = pad_ref[pl.ds(t0, tm + 2 * CONV_HALO), cl * LANES:(cl + 1) * LANES]
        for b in range(8):
            sh_ref[cl, b, :, :] = blk[b:b + tm + 24, :]


def _conv_taps():
    return [(8 * a + b - 1, b, 8 * a) for b in range(8) for a in range(4) if 0 <= 8 * a + b - 1 < TAPS]


CONV_ROWS = 64
SUBLANES = 8


def _dwconv_tile(sh_ref, tm, w_ref, flip, out_ref):
    for cl in range(CC // LANES):
        lanes = slice(cl * LANES, (cl + 1) * LANES)

        def chunk(c, carry):
            r = pl.multiple_of(c * CONV_ROWS, CONV_ROWS)
            accs = [None] * 4
            for n, (k, b, off) in enumerate(_conv_taps()):
                kk = TAPS - 1 - k if flip else k
                term = sh_ref[cl, b, pl.ds(off + r, CONV_ROWS), :] * w_ref[kk:kk + 1, lanes]
                accs[n % 4] = term if accs[n % 4] is None else accs[n % 4] + term
            out_ref[pl.ds(r, CONV_ROWS), lanes] = (accs[0] + accs[1]) + (accs[2] + accs[3])
            return carry

        lax.fori_loop(0, tm // CONV_ROWS, chunk, 0)


def _fill_padded(i, src_hbm, pad_ref, sem, S):
    @pl.when(i == 0)
    def _():
        zeros = jnp.zeros((CONV_HALO, pad_ref.shape[1]), pad_ref.dtype)
        pad_ref[0:CONV_HALO, :] = zeros
        pad_ref[CONV_HALO + S:2 * CONV_HALO + S, :] = zeros
        cp = pltpu.make_async_copy(src_hbm, pad_ref.at[pl.ds(CONV_HALO, S), :], sem)
        cp.start()
        cp.wait()


def _ln_silu(y, g, b):
    mu = jnp.mean(y, axis=-1, keepdims=True)
    yc = y - mu
    var = jnp.mean(yc * yc, axis=-1, keepdims=True)
    rstd = lax.rsqrt(var + EPS)
    yh = yc * rstd
    z = yh * g + b
    return yh, rstd, z


def _conv_fwd(vg, w_dw, b_dw, ln_g, ln_b):
    S = vg.shape[0]
    tm = min(CONV_TILE, S)

    def body(vg_hbm, w_ref, b_ref, g_ref, bb_ref, y_ref, yc_ref, pad_ref, sh_ref, sem):
        i = pl.program_id(0)
        _fill_padded(i, vg_hbm, pad_ref, sem, S)
        _shifted_copies(pad_ref, sh_ref.at[0], pl.multiple_of(i * tm, tm), tm)
        _dwconv_tile(sh_ref.at[0], tm, w_ref, False, y_ref)
        y = y_ref[...] + b_ref[...]
        y_ref[...] = y
        _, _, z = _ln_silu(y, g_ref[...], bb_ref[...])
        yc_ref[...] = (z * _sigmoid(z)).astype(yc_ref.dtype)

    return _hbm_call(
        body, name="conv_fwd", grid=(S // tm,),
        in_specs=[pl.BlockSpec(memory_space=pl.ANY), _full((TAPS, CC)), _full((1, CC)), _full((1, CC)), _full((1, CC))],
        out_specs=[_rows(tm, CC), _rows(tm, CC)],
        out_shape=[_sds((S, CC), F32), _sds((S, CC), _MXU)],
        scratch_shapes=[pltpu.VMEM((S + 2 * CONV_HALO, CC), F32), _shift_scratch(tm, 1), pltpu.SemaphoreType.DMA],
        compiler_params=_params(40),
    )(vg, w_dw, b_dw, ln_g, ln_b)


def _attn_window(i, S):
    start = jnp.clip((i - 1) * BLK, 0, S - 3 * BLK)
    return pl.multiple_of(start, BLK)


ATTN_COLS = GRP * BLK


def _attn_mask_base():
    keys = lax.broadcasted_iota(jnp.int32, (3 * BLK, ATTN_COLS), 0)
    cols = lax.broadcasted_iota(jnp.int32, (3 * BLK, ATTN_COLS), 1)
    return keys - (cols & (BLK - 1))


def _attn_mask(base, i, start):
    return jnp.abs(base + (start - i * BLK)) <= BLK


def _stack_heads(t, hk):
    return jnp.concatenate([t[:, (hk * GRP + g) * HD:(hk * GRP + g + 1) * HD] for g in range(GRP)], axis=0)


def _unstack_heads_t(t):
    cols = []
    for p in range(GRP // 2):
        pair = jnp.concatenate([t[:, (2 * p) * BLK:(2 * p + 1) * BLK], t[:, (2 * p + 1) * BLK:(2 * p + 2) * BLK]],
                               axis=0)
        cols.append(pair.T)
    return jnp.concatenate(cols, axis=1)


def _attn_probs(qs, kh, sk_ref, hk, mask):
    s = jnp.where(mask, _nt(kh, qs), -1e30)
    sk = jnp.concatenate(
        [jnp.broadcast_to(sk_ref[0:1, hk * GRP + g:hk * GRP + g + 1], (1, BLK)) for g in range(GRP)], axis=1)
    m = jnp.maximum(jnp.max(s, axis=0, keepdims=True), sk)
    e = jnp.exp(s - m)
    es = jnp.exp(sk - m)
    inv = 1.0 / (jnp.sum(e, axis=0, keepdims=True) + es)
    return e, inv, es


ATTN_BLOCKS = 4


def _attn_fwd(qr, kr, vv, sink):
    S = qr.shape[0]
    tq = ATTN_BLOCKS * BLK

    def body(q_ref, k_ref, v_ref, sk_ref, o_ref):
        base = _attn_mask_base()
        for b in range(ATTN_BLOCKS):
            i = pl.program_id(0) * ATTN_BLOCKS + b
            rows = slice(b * BLK, (b + 1) * BLK)
            start = _attn_window(i, S)
            kb = k_ref[pl.ds(start, 3 * BLK), :]
            vb = v_ref[pl.ds(start, 3 * BLK), :]
            q = q_ref[rows, :]
            mask = _attn_mask(base, i, start)
            outs = []
            for hk in range(NKV):
                hs = slice(hk * HD, (hk + 1) * HD)
                e, inv, _ = _attn_probs(_stack_heads(q, hk), kb[:, hs], sk_ref, hk, mask)
                outs.append(_unstack_heads_t(_tn(vb[:, hs], e) * inv))
            o_ref[rows, :] = jnp.concatenate(outs, axis=1).astype(o_ref.dtype)

    return _hbm_call(
        body, name="attn_fwd", grid=(S // tq,),
        in_specs=[_rows(tq, ATT), _full((S, KVC)), _full((S, KVC)), _full((1, NQ))],
        out_specs=_rows(tq, ATT),
        out_shape=_sds((S, ATT), _MXU),
        compiler_params=_params(40),
    )(qr, kr, vv, sink)


def _out_proj_fwd(x, yc, ya, wout, b_out, tm=512):
    S = x.shape[0]

    def body(x_ref, yc_ref, ya_ref, w_ref, b_ref, x1_ref, ymix_ref):
        ymix = jnp.concatenate([yc_ref[...], ya_ref[...]], axis=1)
        ymix_ref[...] = ymix
        x1_ref[...] = x_ref[...] + _nn(ymix, w_ref[...]) + b_ref[...]

    return _hbm_call(
        body, name="out_proj_fwd", grid=(S // tm,),
        in_specs=[_rows(tm, D), _rows(tm, CC), _rows(tm, ATT), _full((D, D)), _full((1, D))],
        out_specs=[_rows(tm, D), _rows(tm, D)],
        out_shape=[_sds((S, D), F32), _sds((S, D), _MXU)],
        compiler_params=_params(40),
    )(x, yc, ya, wout, b_out)


def _softmax_t(s):
    m = jnp.max(s, axis=0, keepdims=True)
    e = jnp.exp(s - m)
    return e, 1.0 / jnp.sum(e, axis=0, keepdims=True)


def _softmax(s):
    m = jnp.max(s, axis=-1, keepdims=True)
    e = jnp.exp(s - m)
    return e * (1.0 / jnp.sum(e, axis=-1, keepdims=True))


def _mem_attn_fwd(x1, g_q, wq, km, vm, wo, after, tm=512):
    S = x1.shape[0]
    M = km.shape[0]

    def body(x1_ref, g_ref, wq_ref, km_ref, vm_ref, wo_ref, after_ref, x2_ref, qm_ref, om_ref):
        x1v = x1_ref[...]
        hq, _ = _rms(x1v, g_ref[...])
        qm = (_nn(hq, wq_ref[...]) * MSCALE).astype(qm_ref.dtype)
        qm_ref[...] = qm
        oms = []
        for h in range(MH):
            sl = slice(h * MHD, (h + 1) * MHD)
            p = _softmax(_nt(qm[:, sl], km_ref[:, sl]))
            oms.append(_nn(p, vm_ref[:, sl]))
        om = jnp.concatenate(oms, axis=1).astype(om_ref.dtype)
        om_ref[...] = om
        x2_ref[...] = x1v + _nn(om, wo_ref[...])

    return _hbm_call(
        body, name="mem_attn_fwd", grid=(S // tm,),
        in_specs=[_rows(tm, D), _full((1, D)), _full((D, D)), _full((M, D)), _full((M, D)), _full((D, D)), _AFTER],
        out_specs=[_rows(tm, D)] * 3,
        out_shape=[_sds((S, D), F32), _sds((S, D), _MXU), _sds((S, D), _MXU)],
        compiler_params=_params(48),
    )(x1, g_q, wq, km, vm, wo, after)


def _ffn_loss_fwd(x2, tgt, g_ffn, g_final, wgT, wuT, wd, tm=512):
    S = x2.shape[0]

    def body(x2_ref, t_ref, gf_ref, gl_ref, wg_ref, wu_ref, wd_ref,
             gt_ref, up_ref, hf_ref, dx3_ref, dx3b_ref, loss_ref, dgl_ref):
        i = pl.program_id(0)
        _acc_init(i, loss_ref, dgl_ref)
        x2v = x2_ref[...]
        hf, _ = _rms(x2v, gf_ref[...])
        hb = hf.astype(hf_ref.dtype)
        hf_ref[...] = hb
        x3 = x2v
        for c in range(F // FCH):
            sl = slice(c * FCH, (c + 1) * FCH)
            gt = _nt(hb, wg_ref[sl, :])
            up = _nt(hb, wu_ref[sl, :])
            gt_ref[:, sl] = gt.astype(gt_ref.dtype)
            up_ref[:, sl] = up.astype(up_ref.dtype)
            x3 = x3 + _nn(gt * _sigmoid(gt) * up, wd_ref[sl, :])
        gl = gl_ref[...]
        y, r = _rms(x3, gl)
        diff = y - t_ref[...]
        _acc_row(loss_ref, jnp.sum(jnp.sum(diff * diff, axis=-1, keepdims=True), axis=0, keepdims=True) * (0.5 / D))
        dx3, dgl = _rms_bwd(diff * (1.0 / D), x3, r, gl)
        dx3_ref[...] = dx3
        dx3b_ref[...] = dx3.astype(dx3b_ref.dtype)
        _acc_row(dgl_ref, dgl)

    return _hbm_call(
        body, name="ffn_loss_fwd", grid=(S // tm,),
        in_specs=[_rows(tm, D), _rows(tm, D), _full((1, D)), _full((1, D)),
                  _once((F, D)), _once((F, D)), _once((F, D))],
        out_specs=[_rows(tm, F), _rows(tm, F), _rows(tm, D), _rows(tm, D), _rows(tm, D),
                   _full((8, LANES)), _full((8, D))],
        out_shape=[_sds((S, F), _MXU), _sds((S, F), _MXU), _sds((S, D), _MXU), _sds((S, D), F32),
                   _sds((S, D), _MXU), _sds((8, LANES), F32), _sds((8, D), F32)],
        compiler_params=_params(56),
    )(x2, tgt, g_ffn, g_final, wgT, wuT, wd)


def _ffn_bwd(dx3, x2, gt, up, g_ffn, wd, wgT, wuT, tm=256):
    S = x2.shape[0]

    def body(dx3_ref, x2_ref, gt_ref, up_ref, g_ref, wd_ref, wg_ref, wu_ref,
             dx2_ref, dx2b_ref, dgt_ref, dup_ref, act_ref, dg_ref):
        i = pl.program_id(0)
        _acc_init(i, dg_ref)
        dx3v = dx3_ref[...]
        db = dx3v.astype(_MXU)
        dhf = jnp.zeros((tm, D), F32)
        for c in range(F // FCH_BWD):
            sl = slice(c * FCH_BWD, (c + 1) * FCH_BWD)
            gtv = gt_ref[:, sl].astype(F32)
            upv = up_ref[:, sl].astype(F32)
            dact = _nt(db, wd_ref[sl, :])
            sig = _sigmoid(gtv)
            silu = gtv * sig
            dup = (dact * silu).astype(dup_ref.dtype)
            dgt = (dact * upv * (sig * (1.0 + gtv * (1.0 - sig)))).astype(dgt_ref.dtype)
            act_ref[:, sl] = (silu * upv).astype(act_ref.dtype)
            dgt_ref[:, sl] = dgt
            dup_ref[:, sl] = dup
            dhf = dhf + _nn(dgt, wg_ref[sl, :]) + _nn(dup, wu_ref[sl, :])
        x2v = x2_ref[...]
        g = g_ref[...]
        _, r = _rms(x2v, g)
        dxn, dg = _rms_bwd(dhf, x2v, r, g)
        dx2 = dx3v + dxn
        dx2_ref[...] = dx2
        dx2b_ref[...] = dx2.astype(dx2b_ref.dtype)
        _acc_row(dg_ref, dg)

    return _hbm_call(
        body, name="ffn_bwd", grid=(S // tm,),
        in_specs=[_rows(tm, D), _rows(tm, D), _rows(tm, F), _rows(tm, F), _full((1, D)),
                  _once((F, D)), _once((F, D)), _once((F, D))],
        out_specs=[_rows(tm, D), _rows(tm, D), _rows(tm, F), _rows(tm, F), _rows(tm, F), _full((8, D))],
        out_shape=[_sds((S, D), F32), _sds((S, D), _MXU), _sds((S, F), _MXU), _sds((S, F), _MXU),
                   _sds((S, F), _MXU), _sds((8, D), F32)],
        compiler_params=_params(56),
    )(dx3, x2, gt, up, g_ffn, wd, wgT, wuT)


_AFTER = pl.BlockSpec(memory_space=pl.ANY)


def _wgrad(pairs, name, after, rc=256):
    S, R = pairs[0][0].shape
    N = pairs[0][1].shape[1]
    rights = []
    for _, b in pairs:
        if not any(b is r for r in rights):
            rights.append(b)
    which = [next(i for i, r in enumerate(rights) if r is b) for _, b in pairs]
    n, nb = len(pairs), len(rights)

    def body(*refs):
        for k in range(n):
            refs[n + nb + 1 + k][...] = _tn(refs[k][...], refs[n + which[k]][...]).astype(jnp.bfloat16)

    res = _hbm_call(
        body, name=name, grid=(R // rc,),
        in_specs=[pl.BlockSpec((S, rc), lambda j: (0, j))] * n + [_once((S, N))] * nb + [_AFTER],
        out_specs=[_rows(rc, N)] * n,
        out_shape=[_sds((R, N), jnp.bfloat16)] * n,
        compiler_params=_params(56),
    )(*[a for a, _ in pairs], *rights, after)
    return list(res)


def _mem_attn_bwd(dx2, x1, qm, km, vm, g_q, wq, wo, wout, after, tm=512):
    S = x1.shape[0]
    M = km.shape[0]

    def body(dx2_ref, x1_ref, qm_ref, km_ref, vm_ref, g_ref, wq_ref, wo_ref, wout_ref, after_ref,
             dx1_ref, dx1b_ref, hq_ref, dqm_ref, dyc_ref, dya_ref, dkm_ref, dvm_ref, dg_ref, dbo_ref):
        i = pl.program_id(0)
        _acc_init(i, dkm_ref, dvm_ref, dg_ref, dbo_ref)
        dx2v = dx2_ref[...]
        domt = _nt(wo_ref[...], dx2v).astype(_MXU)
        dqs = []
        for h in range(MH):
            sl = slice(h * MHD, (h + 1) * MHD)
            qh = qm_ref[:, sl]
            kh = km_ref[:, sl]
            e, inv = _softmax_t(_nt(kh, qh))
            p = e * inv
            dp = _nn(vm_ref[:, sl], domt[sl, :])
            ds = (p * (dp - jnp.sum(dp * p, axis=0, keepdims=True))).astype(_MXU)
            dqs.append(_tn(ds, kh) * MSCALE)
            dkm_ref[:, sl] += _nn(ds, qh)
            dvm_ref[:, sl] += _nt(p, domt[sl, :])
        dqm = jnp.concatenate(dqs, axis=1).astype(dqm_ref.dtype)
        dqm_ref[...] = dqm
        dhq = _nt(dqm, wq_ref[...])
        x1v = x1_ref[...]
        g = g_ref[...]
        hq, r = _rms(x1v, g)
        hq_ref[...] = hq.astype(hq_ref.dtype)
        dxn, dg = _rms_bwd(dhq, x1v, r, g)
        dx1 = dx2v + dxn
        dx1_ref[...] = dx1
        dx1b = dx1.astype(dx1b_ref.dtype)
        dx1b_ref[...] = dx1b
        dym = _nt(dx1b, wout_ref[...])
        dyc_ref[...] = dym[:, :CC]
        dya_ref[...] = dym[:, CC:].astype(dya_ref.dtype)
        _acc_row(dg_ref, dg)
        _acc_row(dbo_ref, jnp.sum(dx1, axis=0, keepdims=True))

    return _hbm_call(
        body, name="mem_attn_bwd", grid=(S // tm,),
        in_specs=[_rows(tm, D), _rows(tm, D), _rows(tm, D), _full((M, D)), _full((M, D)), _full((1, D)),
                  _once((D, D)), _once((D, D)), _once((D, D)), _AFTER],
        out_specs=[_rows(tm, D), _rows(tm, D), _rows(tm, D), _rows(tm, D), _rows(tm, CC), _rows(tm, ATT),
                   _full((M, D)), _full((M, D)), _full((8, D)), _full((8, D))],
        out_shape=[_sds((S, D), F32), _sds((S, D), _MXU), _sds((S, D), _MXU), _sds((S, D), _MXU),
                   _sds((S, CC), F32), _sds((S, ATT), _MXU),
                   _sds((M, D), F32), _sds((M, D), F32), _sds((8, D), F32), _sds((8, D), F32)],
        compiler_params=_params(56),
    )(dx2, x1, qm, km, vm, g_q, wq, wo, wout, after)


def _mem_kv_bwd(dkm, dvm, mem, g_kv, memn, wkvT):
    M = mem.shape[0]

    def body(dkm_ref, dvm_ref, mem_ref, g_ref, mn_ref, w_ref, gw_ref, dg_ref):
        dkv = jnp.concatenate([dkm_ref[...], dvm_ref[...]], axis=1).astype(_MXU)
        gw_ref[...] = _tn(dkv, mn_ref[...]).astype(gw_ref.dtype)
        dmn = _nn(dkv, w_ref[...])
        memv = mem_ref[...]
        g = g_ref[...]
        _, r = _rms(memv, g)
        _, dg = _rms_bwd(dmn, memv, r, g)
        dg_ref[...] = jnp.broadcast_to(dg, dg_ref.shape)

    return _hbm_call(
        body, name="mem_kv_bwd", grid=(1,),
        in_specs=[_full((M, D)), _full((M, D)), _full((M, D)), _full((1, D)), _full((M, D)), _full((2 * D, D))],
        out_specs=[_full((2 * D, D)), _full((8, D))],
        out_shape=[_sds((2 * D, D), jnp.bfloat16), _sds((8, D), F32)],
        compiler_params=_params(48),
    )(dkm, dvm, mem, g_kv, memn, wkvT)


def _attn_bwd(qr, kr, vv, sink, dya, after):
    S = qr.shape[0]
    tq = ATTN_BLOCKS * BLK

    def body(q_ref, k_ref, v_ref, sk_ref, do_ref, after_ref, dq_ref, dk_ref, dv_ref, dsk_ref, sink_acc):
        _acc_init(pl.program_id(0), dk_ref, dv_ref, sink_acc)
        base = _attn_mask_base()
        for b in range(ATTN_BLOCKS):
            i = pl.program_id(0) * ATTN_BLOCKS + b
            rows = slice(b * BLK, (b + 1) * BLK)
            start = _attn_window(i, S)
            kb = k_ref[pl.ds(start, 3 * BLK), :]
            vb = v_ref[pl.ds(start, 3 * BLK), :]
            q = q_ref[rows, :]
            do = do_ref[rows, :]
            mask = _attn_mask(base, i, start)
            dqs, dks, dvs = [], [], []
            for hk in range(NKV):
                hs = slice(hk * HD, (hk + 1) * HD)
                qs = _stack_heads(q, hk)
                dos = _stack_heads(do, hk)
                e, inv, es = _attn_probs(qs, kb[:, hs], sk_ref, hk, mask)
                p = e * inv
                dp = _nt(vb[:, hs], dos)
                delta = jnp.sum(dp * p, axis=0, keepdims=True)
                ds = (p * (dp - delta)).astype(_MXU)
                dqs.append(_unstack_heads_t(_tn(kb[:, hs], ds) * SCALE))
                dks.append(_nn(ds, qs))
                dvs.append(_nn(p, dos))
                sink_acc[hk:hk + 1, :] -= es * inv * delta
            dq_ref[rows, :] = jnp.concatenate(dqs, axis=1)
            dk_ref[pl.ds(start, 3 * BLK), :] += jnp.concatenate(dks, axis=1)
            dv_ref[pl.ds(start, 3 * BLK), :] += jnp.concatenate(dvs, axis=1)

        @pl.when(pl.program_id(0) == pl.num_programs(0) - 1)
        def _():
            for h in range(NQ):
                tot = jnp.sum(sink_acc[h // GRP:h // GRP + 1, (h % GRP) * BLK:(h % GRP + 1) * BLK], axis=1,
                              keepdims=True)
                dsk_ref[h:h + 1, :] = jnp.broadcast_to(tot, (1, LANES))

    return _hbm_call(
        body, name="attn_bwd", grid=(S // tq,),
        in_specs=[_rows(tq, ATT), _full((S, KVC)), _full((S, KVC)), _full((1, NQ)), _rows(tq, ATT), _AFTER],
        out_specs=[_rows(tq, ATT), _full((S, KVC)), _full((S, KVC)), _full((8, LANES))],
        out_shape=[_sds((S, ATT), F32), _sds((S, KVC), F32), _sds((S, KVC), F32), _sds((8, LANES), F32)],
        scratch_shapes=[pltpu.VMEM((8, ATTN_COLS), F32)],
        compiler_params=_params(40),
    )(qr, kr, vv, sink, dya, after)


def _conv_norm_bwd(dyc, y, ln_g, ln_b, after, tm=512):
    S = y.shape[0]

    def body(dyc_ref, y_ref, g_ref, b_ref, after_ref, dy_ref, dg_ref, db_ref, dbdw_ref):
        i = pl.program_id(0)
        _acc_init(i, dg_ref, db_ref, dbdw_ref)
        g = g_ref[...]
        yh, rstd, z = _ln_silu(y_ref[...], g, b_ref[...])
        sig = _sigmoid(z)
        dz = dyc_ref[...] * (sig * (1.0 + z * (1.0 - sig)))
        dyh = dz * g
        dy = rstd * (dyh - jnp.mean(dyh, axis=-1, keepdims=True)
                     - yh * jnp.mean(dyh * yh, axis=-1, keepdims=True))
        dy_ref[...] = dy
        _acc_row(dg_ref, jnp.sum(dz * yh, axis=0, keepdims=True))
        _acc_row(db_ref, jnp.sum(dz, axis=0, keepdims=True))
        _acc_row(dbdw_ref, jnp.sum(dy, axis=0, keepdims=True))

    return _hbm_call(
        body, name="conv_norm_bwd", grid=(S // tm,),
        in_specs=[_rows(tm, CC), _rows(tm, CC), _full((1, CC)), _full((1, CC)), _AFTER],
        out_specs=[_rows(tm, CC), _full((8, CC)), _full((8, CC)), _full((8, CC))],
        out_shape=[_sds((S, CC), F32), _sds((8, CC), F32), _sds((8, CC), F32), _sds((8, CC), F32)],
        compiler_params=_params(40),
    )(dyc, y, ln_g, ln_b, after)


def _conv_bwd(dy, vg, uglu, w_dw):
    S = dy.shape[0]
    tm = min(CONV_TILE, S)

    def body(dy_hbm, vg_hbm, uglu_ref, w_ref, du_ref, gw_ref, dbin_ref, dyp_ref, vgp_ref, gacc_ref, sh_ref, dvg_ref,
             sems):
        i = pl.program_id(0)
        _acc_init(i, gacc_ref, dbin_ref)
        _fill_padded(i, dy_hbm, dyp_ref, sems.at[0], S)
        _fill_padded(i, vg_hbm, vgp_ref, sems.at[1], S)
        t0 = pl.multiple_of(i * tm, tm)
        _shifted_copies(dyp_ref, sh_ref.at[0], t0, tm)
        _shifted_copies(vgp_ref, sh_ref.at[1], t0, tm)
        _dwconv_tile(sh_ref.at[0], tm, w_ref, True, dvg_ref)
        dvg = dvg_ref[...]
        sub = CONV_ROWS
        for cl in range(CC // LANES):
            lanes = slice(cl * LANES, (cl + 1) * LANES)
            for k, b, off in _conv_taps():
                part = jnp.zeros((8, LANES), F32)
                for r in range(0, tm, sub):
                    prod = (sh_ref[0, cl, 0, CONV_HALO + r:CONV_HALO + r + sub, :]
                            * sh_ref[1, cl, b, off + r:off + r + sub, :])
                    part = part + jnp.sum(prod.reshape(sub // 8, 8, LANES), axis=0)
                gacc_ref[k, :, lanes] += part
        a_v = uglu_ref[:, :CC].astype(F32)
        sg = _sigmoid(uglu_ref[:, CC:].astype(F32))
        da = dvg * sg
        dgate = dvg * a_v * sg * (1.0 - sg)
        du = jnp.concatenate([da, dgate], axis=1)
        du_ref[...] = du.astype(du_ref.dtype)
        _acc_row(dbin_ref, jnp.sum(du, axis=0, keepdims=True))

        @pl.when(i == pl.num_programs(0) - 1)
        def _():
            gw_ref[...] = jnp.zeros(gw_ref.shape, F32)
            for k in range(TAPS):
                gw_ref[k:k + 1, :] = jnp.sum(gacc_ref[k], axis=0, keepdims=True)

    return _hbm_call(
        body, name="conv_bwd", grid=(S // tm,),
        in_specs=[pl.BlockSpec(memory_space=pl.ANY), pl.BlockSpec(memory_space=pl.ANY), _rows(tm, 2 * CC),
                  _full((TAPS, CC))],
        out_specs=[_rows(tm, 2 * CC), _full((TAPS + 1, CC)), _full((8, 2 * CC))],
        out_shape=[_sds((S, 2 * CC), _MXU), _sds((TAPS + 1, CC), F32), _sds((8, 2 * CC), F32)],
        scratch_shapes=[pltpu.VMEM((S + 2 * CONV_HALO, CC), F32), pltpu.VMEM((S + 2 * CONV_HALO, CC), F32),
                        pltpu.VMEM((TAPS, 8, CC), F32), _shift_scratch(tm, 2), pltpu.VMEM((tm, CC), F32),
                        pltpu.SemaphoreType.DMA((2,))],
        compiler_params=_params(56),
    )(dy, vg, uglu, w_dw)


def _in_proj_bwd(duglu, dqr, dkr, dv, cosq, sinq, x, dx1, g_mix, winT, tm=512):
    S = x.shape[0]
    NA = ATT + 2 * KVC

    def body(dug_ref, dq_ref, dk_ref, dv_ref, c_ref, s_ref, x_ref, dx1_ref, g_ref, w_ref,
             dx_ref, du_ref, dg_ref, dbin_ref):
        i = pl.program_id(0)
        _acc_init(i, dg_ref, dbin_ref)
        c = c_ref[...]
        s = s_ref[...]
        datt = jnp.concatenate([_rope_bwd(dq_ref[...], c, s), _rope_bwd(dk_ref[...], c, s), dv_ref[...]], axis=1)
        _acc_row(dbin_ref, jnp.sum(datt, axis=0, keepdims=True))
        du = jnp.concatenate([dug_ref[...], datt.astype(du_ref.dtype)], axis=1)
        du_ref[...] = du
        dh = _nn(du, w_ref[...])
        xv = x_ref[...]
        g = g_ref[...]
        _, r = _rms(xv, g)
        dxn, dg = _rms_bwd(dh, xv, r, g)
        dx_ref[...] = dx1_ref[...] + dxn
        _acc_row(dg_ref, dg)

    return _hbm_call(
        body, name="in_proj_bwd", grid=(S // tm,),
        in_specs=[_rows(tm, 2 * CC), _rows(tm, ATT), _rows(tm, KVC), _rows(tm, KVC), _rows(tm, LANES),
                  _rows(tm, LANES), _rows(tm, D), _rows(tm, D), _full((1, D)), _once((INC, D))],
        out_specs=[_rows(tm, D), _rows(tm, INC), _full((8, D)), _full((8, NA))],
        out_shape=[_sds((S, D), F32), _sds((S, INC), _MXU), _sds((8, D), F32), _sds((8, NA), F32)],
        compiler_params=_params(48),
    )(duglu, dqr, dkr, dv, cosq, sinq, x, dx1, g_mix, winT)


def _rope_tables(S):
    pos = jnp.arange(S, dtype=F32)
    inv_freq = ROPE_THETA ** (-jnp.arange(0, HD, 2, dtype=F32) / HD)
    ang = pos[:, None] * inv_freq[None, :]
    cos, sin = jnp.cos(ang), jnp.sin(ang)
    cosq = jnp.tile(jnp.concatenate([cos, cos], axis=1), (1, LANES // HD))
    sinq = jnp.tile(jnp.concatenate([-sin, sin], axis=1), (1, LANES // HD))
    return cosq, sinq


GATHER_GROUPS = (("w_inT", "w_dw"), ("w_out", "w_mem_kvT", "w_mem_q", "w_mem_o"), ("w_gateT", "w_upT"), ("w_down",))
ID_GATHER, ID_SIBLING, ID_CHIP = 1, 5, 8


def _local_step(x, mem, tgt, sp, gather, reduce):
    S = x.shape[0]
    cosq, sinq = _rope_tables(S)

    gather.forward(0, cosq)
    w0 = gather.finish(0, sinq)
    winT, w_dw_full = w0["w_inT"], w0["w_dw"]
    h, uglu, vg, qr, kr, vv = _in_proj_fwd(x, sp["g_mix"], winT, sp["b_in"], cosq, sinq)
    ya = _attn_fwd(qr, kr, vv, sp["attn_sink"])
    gather.forward(1, ya)
    y, yc = _conv_fwd(vg, w_dw_full, sp["b_dw"], sp["g_conv_ln"], sp["b_conv_ln"])
    w1 = gather.finish(1, yc)
    wout, wkvT, wq, wo = w1["w_out"], w1["w_mem_kvT"], w1["w_mem_q"], w1["w_mem_o"]
    km, vm, memn = _mem_kv_fwd(mem, sp["g_mem_kv"], wkvT)
    x1, ymix = _out_proj_fwd(x, yc, ya, wout, sp["b_out"])
    token = gather.forward(2, x1)
    x2, qm, om = _mem_attn_fwd(x1, sp["g_mem_q"], wq, km, vm, wo, token)
    w2 = gather.finish(2, om)
    gather.forward(3, om)
    wgT, wuT, wd = w2["w_gateT"], w2["w_upT"], gather.finish(3, om)["w_down"]
    gt, up, hf, dx3, dx3b, loss_acc, dg_final = _ffn_loss_fwd(x2, tgt, sp["g_ffn"], sp["g_final"], wgT, wuT, wd)

    dx2, dx2b, dgt, dup, act, dg_ffn = _ffn_bwd(dx3, x2, gt, up, sp["g_ffn"], wd, wgT, wuT)
    token = reduce.start(0, _wgrad([(dgt, hf), (dup, hf), (act, dx3b)], "wgrad_ffn", dx2))

    dx1, dx1b, hq, dqm, dyc, dya, dkm, dvm, dg_mem_q, db_out = _mem_attn_bwd(
        dx2, x1, qm, km, vm, sp["g_mem_q"], wq, wo, wout, token)
    token = reduce.middle(0, dx1)
    g_wout, g_wq, g_wo = _wgrad([(ymix, dx1b), (hq, dqm), (om, dx2b)], "wgrad_mix_mem", token)
    g_wkvT, dg_mem_kv = _mem_kv_bwd(dkm, dvm, mem, sp["g_mem_kv"], memn, wkvT)
    token = reduce.start(1, [g_wout, g_wq, g_wkvT, g_wo])

    dqr, dkr, dvv, dsink = _attn_bwd(qr, kr, vv, sp["attn_sink"], dya, token)
    token = reduce.middle(1, dqr)
    dy, dg_cln, db_cln, db_dw = _conv_norm_bwd(dyc, y, sp["g_conv_ln"], sp["b_conv_ln"], token)
    duglu, g_wdw, dbin_glu = _conv_bwd(dy, vg, uglu, w_dw_full)
    grad_x, du, dg_mix, dbin_att = _in_proj_bwd(duglu, dqr, dkr, dvv, cosq, sinq, x, dx1, sp["g_mix"], winT)
    token = reduce.start(2, _wgrad([(du, h)], "wgrad_in", grad_x))

    small = {
        "loss": loss_acc[0:1, :],
        "g_mix": dg_mix[0:1], "b_in": jnp.concatenate([dbin_glu[0:1], dbin_att[0:1]], axis=1),
        "b_dw": db_dw[0:1], "g_conv_ln": dg_cln[0:1], "b_conv_ln": db_cln[0:1],
        "attn_sink": dsink[:, 0].reshape(1, NQ), "b_out": db_out[0:1], "g_mem_q": dg_mem_q[0:1],
        "g_mem_kv": dg_mem_kv[0:1], "g_ffn": dg_ffn[0:1], "g_final": dg_final[0:1],
    }
    token = reduce.middle(2, token, reduce.small_zones(small, g_wdw))
    reduce.finish(0, token)
    reduce.finish(1, token)
    return grad_x


def _place():
    return lax.axis_index("x"), lax.axis_index("y"), lax.axis_index("c")


def _hbm_specs(n):
    return [pl.BlockSpec(memory_space=pl.ANY)] * n


_HBM = pl.BlockSpec(memory_space=pltpu.HBM)
_SEM = pl.BlockSpec(memory_space=pltpu.SEMAPHORE)
_ANY = pl.BlockSpec(memory_space=pl.ANY)
_EFFECT = pltpu.SideEffectType.DATAFLOW_SIDE_EFFECTING
_VMEM = pl.BlockSpec(memory_space=pltpu.VMEM)
_TOKEN = jax.ShapeDtypeStruct((8, LANES), F32)


def _dev_index(p):
    return 4 * p[0] + 2 * p[1] + p[2]


def _landing_zones(shards, me_idx):
    def body(m_ref, *refs):
        n = len(refs) // 2
        for k in range(n):
            refs[n + k][...] = refs[k][...].astype(jnp.bfloat16)

    return pl.pallas_call(
        body, name="landing_zones",
        grid_spec=pltpu.PrefetchScalarGridSpec(
            num_scalar_prefetch=1, grid=(1,),
            in_specs=[pl.BlockSpec(s.shape, lambda i, m: (0, 0)) for s in shards],
            out_specs=[pl.BlockSpec((None,) + s.shape, lambda i, m: (m[0], 0, 0)) for s in shards]),
        out_shape=[pltpu.HBM((NDEV,) + s.shape, jnp.bfloat16) for s in shards],
        compiler_params=_params(40),
    )(me_idx, *shards)


def _gather_start(groups):
    sizes = [len(g) for g in groups]
    flat = [z for g in groups for z in g]
    n = len(flat)

    def body(*refs):
        ins = refs[:n]
        sems = refs[2 * n:]
        x, y, c = _place()
        me = _dev_index((x, y, c))
        peers = [(x, y, 1 - c), (1 - x, y, c), (x, 1 - y, c), (1 - x, 1 - y, c)]
        base = 0
        for gi, size in enumerate(sizes):
            send_sems, recv_sems = sems[2 * gi], sems[2 * gi + 1]
            for k in range(size):
                zone = ins[base + k]
                for j, peer in enumerate(peers):
                    pltpu.make_async_remote_copy(
                        src_ref=zone.at[me], dst_ref=zone.at[me], send_sem=send_sems.at[4 * k + j],
                        recv_sem=recv_sems.at[4 * k + j], device_id=peer, device_id_type=MESH).start()
            base += size

    out_shape = [pltpu.HBM(z.shape, z.dtype) for z in flat]
    out_specs = [_HBM] * n
    for size in sizes:
        out_shape += [pltpu.SemaphoreType.DMA((4 * size,))] * 2
        out_specs += [_SEM, _SEM]
    res = pl.pallas_call(
        body, name="gather_start", in_specs=[_HBM] * n, out_specs=out_specs, out_shape=out_shape,
        input_output_aliases={k: k for k in range(n)},
        compiler_params=pltpu.CompilerParams(has_side_effects=_EFFECT),
    )(*[pltpu.with_memory_space_constraint(z, pltpu.HBM) for z in flat])
    out, base = [], 0
    for gi, size in enumerate(sizes):
        out.append((list(res[base:base + size]), res[n + 2 * gi], res[n + 2 * gi + 1]))
        base += size
    return out


def _gather_forward(zones, send_sems, recv_sems, after, collective_id, name):
    n = len(zones)

    def body(*refs):
        ins = refs[:n]
        send_ref, recv_ref = refs[n], refs[n + 1]
        fsend, frecv = refs[2 * n + 3], refs[2 * n + 4]
        x, y, c = _place()
        sibling = (x, y, 1 - c)
        barrier = pltpu.get_barrier_semaphore()
        pl.semaphore_signal(barrier, inc=1, device_id=sibling, device_id_type=MESH)
        pl.semaphore_wait(barrier, 1)
        chips = [(1 - x, y), (x, 1 - y), (1 - x, 1 - y)]
        for j, chip in enumerate(chips):
            for k in range(n):
                slot = ins[k].at[_dev_index((*chip, c))]
                pltpu.make_async_remote_copy(
                    src_ref=slot, dst_ref=slot, send_sem=send_ref.at[4 * k + 1 + j],
                    recv_sem=recv_ref.at[4 * k + 1 + j], device_id=sibling, device_id_type=MESH).wait_recv()
                pltpu.make_async_remote_copy(
                    src_ref=slot, dst_ref=slot, send_sem=fsend.at[3 * k + j], recv_sem=frecv.at[3 * k + j],
                    device_id=sibling, device_id_type=MESH).start()
        for k in range(n):
            own = ins[k].at[_dev_index((x, y, c))]
            theirs = ins[k].at[_dev_index(sibling)]
            pltpu.make_async_remote_copy(
                src_ref=theirs, dst_ref=theirs, send_sem=send_ref.at[4 * k], recv_sem=recv_ref.at[4 * k],
                device_id=sibling, device_id_type=MESH).wait_recv()
            for j in range(4):
                pltpu.make_async_remote_copy(
                    src_ref=own, dst_ref=own, send_sem=send_ref.at[4 * k + j], recv_sem=recv_ref.at[4 * k + j],
                    device_id=sibling, device_id_type=MESH).wait_send()
        refs[2 * n + 5][...] = jnp.zeros(_TOKEN.shape, _TOKEN.dtype)

    res = pl.pallas_call(
        body, name=name, in_specs=[_HBM] * n + [_SEM, _SEM, _ANY],
        out_specs=[_HBM] * n + [_SEM, _SEM, _VMEM],
        out_shape=[pltpu.HBM(z.shape, z.dtype) for z in zones] + [pltpu.SemaphoreType.DMA((3 * n,))] * 2 + [_TOKEN],
        input_output_aliases={k: k for k in range(n)},
        compiler_params=pltpu.CompilerParams(has_side_effects=_EFFECT, collective_id=collective_id),
    )(*zones, send_sems, recv_sems, after)
    return (list(res[:n]), res[n], res[n + 1]), res[n + 2]


def _gather_finish(zones, fsend, frecv, after, name):
    n = len(zones)

    def body(*refs):
        ins = refs[:n]
        fsend_ref, frecv_ref = refs[n], refs[n + 1]
        x, y, c = _place()
        sibling = (x, y, 1 - c)
        chips = [(1 - x, y), (x, 1 - y), (1 - x, 1 - y)]
        for j, chip in enumerate(chips):
            for k in range(n):
                mine = ins[k].at[_dev_index((*chip, c))]
                theirs = ins[k].at[_dev_index((*chip, 1 - c))]
                pltpu.make_async_remote_copy(
                    src_ref=theirs, dst_ref=theirs, send_sem=fsend_ref.at[3 * k + j],
                    recv_sem=frecv_ref.at[3 * k + j], device_id=sibling, device_id_type=MESH).wait_recv()
                pltpu.make_async_remote_copy(
                    src_ref=mine, dst_ref=mine, send_sem=fsend_ref.at[3 * k + j],
                    recv_sem=frecv_ref.at[3 * k + j], device_id=sibling, device_id_type=MESH).wait_send()

    res = pl.pallas_call(
        body, name=name, in_specs=[_HBM] * n + [_SEM, _SEM, _ANY], out_specs=[_HBM] * n,
        out_shape=[pltpu.HBM(z.shape, z.dtype) for z in zones],
        input_output_aliases={k: k for k in range(n)},
        compiler_params=pltpu.CompilerParams(has_side_effects=_EFFECT),
    )(*zones, fsend, frecv, after)
    return list(res)


class _Gather:
    def __init__(self, started):
        self._phase = list(started)

    def forward(self, gi, after):
        zones, send_sems, recv_sems = self._phase[gi]
        self._phase[gi], token = _gather_forward(zones, send_sems, recv_sems, after, ID_GATHER + gi,
                                                  "gather_forward_%d" % gi)
        return token

    def finish(self, gi, after):
        zones, fsend, frecv = self._phase[gi]
        out = {}
        for name, z in zip(GATHER_GROUPS[gi], _gather_finish(zones, fsend, frecv, after, "gather_finish_%d" % gi)):
            if name == "w_dw":
                out[name] = jnp.transpose(z, (1, 0, 2)).reshape(TAPS + 1, CC)[:TAPS]
            else:
                out[name] = z.reshape(NDEV * z.shape[1], D)
        return out


def _handshake(peers):
    barrier = pltpu.get_barrier_semaphore()
    for peer in peers:
        pl.semaphore_signal(barrier, inc=1, device_id=peer, device_id_type=MESH)
    pl.semaphore_wait(barrier, len(peers))


def _sibling_copies(srcs, lands, send_sems, recv_sems):
    x, y, c = _place()
    return [pltpu.make_async_remote_copy(
        src_ref=srcs[k].at[2 * s + (1 - c)], dst_ref=lands[k].at[s], send_sem=send_sems.at[4 * k + s],
        recv_sem=recv_sems.at[4 * k + s], device_id=(x, y, 1 - c), device_id_type=MESH)
        for k in range(len(srcs)) for s in range(4)]


def _sibling_start(gbs, collective_id, name):
    n = len(gbs)
    lands = [lax.empty((4,) + g.shape[1:], g.dtype) for g in gbs]

    def body(*refs):
        x, y, c = _place()
        _handshake([(x, y, 1 - c)])
        for cp in _sibling_copies(refs[:n], refs[n:2 * n], refs[4 * n], refs[4 * n + 1]):
            cp.start()
        refs[4 * n + 2][...] = jnp.zeros(_TOKEN.shape, _TOKEN.dtype)

    res = pl.pallas_call(
        body, name=name, in_specs=[_HBM] * (2 * n), out_specs=[_HBM] * (2 * n) + [_SEM, _SEM, _VMEM],
        out_shape=[pltpu.HBM(a.shape, a.dtype) for a in gbs + lands] + [pltpu.SemaphoreType.DMA((4 * n,))] * 2
        + [_TOKEN],
        input_output_aliases={k: k for k in range(2 * n)},
        compiler_params=pltpu.CompilerParams(has_side_effects=_EFFECT, collective_id=collective_id),
    )(*[pltpu.with_memory_space_constraint(a, pltpu.HBM) for a in gbs + lands])
    return list(res[:n]), list(res[n:2 * n]), res[2 * n], res[2 * n + 1], res[2 * n + 2]


def _sibling_wait(gbs, lands, send_sems, recv_sems, after, name):
    n = len(gbs)

    def body(*refs):
        for cp in _sibling_copies(refs[:n], refs[n:2 * n], refs[2 * n], refs[2 * n + 1]):
            cp.wait_recv()
            cp.wait_send()

    res = pl.pallas_call(
        body, name=name, in_specs=[_HBM] * (2 * n) + [_SEM, _SEM, _ANY], out_specs=[_HBM] * (2 * n),
        out_shape=[pltpu.HBM(a.shape, a.dtype) for a in gbs + lands],
        input_output_aliases={k: k for k in range(2 * n)},
        compiler_params=pltpu.CompilerParams(has_side_effects=_EFFECT),
    )(*gbs, *lands, send_sems, recv_sems, after)
    return list(res[:n]), list(res[n:])


def _pair_sum(gfs, ts, c_idx, name):
    n = len(gfs)
    halves = 2

    def body(c_ref, *refs):
        for k in range(n):
            refs[2 * n + k][...] = (refs[k][...].astype(F32) + refs[n + k][...].astype(F32)).astype(jnp.bfloat16)

    in_specs, out_specs, out_shape = [], [], []
    for g in gfs:
        rh = g.shape[1] // halves
        in_specs.append(pl.BlockSpec((None, rh, D), lambda s, h, c_ref: (2 * s + c_ref[0], h, 0)))
    for g in gfs:
        rh = g.shape[1] // halves
        in_specs.append(pl.BlockSpec((None, rh, D), lambda s, h, c_ref: (s, h, 0)))
        out_specs.append(pl.BlockSpec((None, rh, D), lambda s, h, c_ref: (s, h, 0)))
        out_shape.append(_sds((4, g.shape[1], D), jnp.bfloat16))
    return pl.pallas_call(
        body, name=name,
        grid_spec=pltpu.PrefetchScalarGridSpec(num_scalar_prefetch=1, grid=(4, halves), in_specs=in_specs,
                                               out_specs=out_specs),
        out_shape=out_shape,
        compiler_params=pltpu.CompilerParams(dimension_semantics=("arbitrary", "arbitrary"),
                                             vmem_limit_bytes=48 * 2 ** 20),
    )(c_idx, *gfs, *ts)


def _chip_copies(pbs, lands, smalls, send_sems, recv_sems):
    x, y, c = _place()
    own = 2 * x + y
    me = _dev_index((x, y, c))
    chips = [(1 - x, y), (x, 1 - y), (1 - x, 1 - y)]
    n = len(pbs)
    copies = [pltpu.make_async_remote_copy(
        src_ref=pbs[k].at[2 * chip[0] + chip[1]], dst_ref=lands[k].at[own], send_sem=send_sems.at[3 * k + j],
        recv_sem=recv_sems.at[3 * k + j], device_id=(*chip, c), device_id_type=MESH)
        for k in range(n) for j, chip in enumerate(chips)]
    peers = [(x, y, 1 - c)] + [(*chip, cc) for chip in chips for cc in (c, 1 - c)]
    copies += [pltpu.make_async_remote_copy(
        src_ref=smalls[t].at[me], dst_ref=smalls[t].at[me], send_sem=send_sems.at[3 * n + 7 * t + j],
        recv_sem=recv_sems.at[3 * n + 7 * t + j], device_id=peer, device_id_type=MESH)
        for t in range(len(smalls)) for j, peer in enumerate(peers)]
    return copies


def _chip_start(pbs, smalls, collective_id, name):
    n, m = len(pbs), len(smalls)
    lands = [lax.empty(p.shape, p.dtype) for p in pbs]
    arrays = list(pbs) + lands + list(smalls)
    na = len(arrays)
    nsem = 3 * n + 7 * m

    def body(*refs):
        x, y, c = _place()
        chips = [(1 - x, y), (x, 1 - y), (1 - x, 1 - y)]
        peers = [(*chip, c) for chip in chips]
        if m:
            peers += [(x, y, 1 - c)] + [(*chip, 1 - c) for chip in chips]
        _handshake(peers)
        for cp in _chip_copies(refs[:n], refs[n:2 * n], refs[2 * n:na], refs[2 * na], refs[2 * na + 1]):
            cp.start()
        refs[2 * na + 2][...] = jnp.zeros(_TOKEN.shape, _TOKEN.dtype)

    res = pl.pallas_call(
        body, name=name, in_specs=[_HBM] * na, out_specs=[_HBM] * na + [_SEM, _SEM, _VMEM],
        out_shape=[pltpu.HBM(a.shape, a.dtype) for a in arrays] + [pltpu.SemaphoreType.DMA((nsem,))] * 2 + [_TOKEN],
        input_output_aliases={k: k for k in range(na)},
        compiler_params=pltpu.CompilerParams(has_side_effects=_EFFECT, collective_id=collective_id),
    )(*[pltpu.with_memory_space_constraint(a, pltpu.HBM) for a in arrays])
    return (list(res[:n]), list(res[n:2 * n]), list(res[2 * n:na]), res[na], res[na + 1]), res[na + 2]


def _chip_wait(pbs, lands, smalls, send_sems, recv_sems, after, name):
    n = len(pbs)
    arrays = list(pbs) + list(lands) + list(smalls)
    na = len(arrays)

    def body(*refs):
        for cp in _chip_copies(refs[:n], refs[n:2 * n], refs[2 * n:na], refs[na], refs[na + 1]):
            cp.wait_recv()
            cp.wait_send()

    res = pl.pallas_call(
        body, name=name, in_specs=[_HBM] * na + [_SEM, _SEM, _ANY], out_specs=[_HBM] * na,
        out_shape=[pltpu.HBM(a.shape, a.dtype) for a in arrays],
        input_output_aliases={k: k for k in range(na)},
        compiler_params=pltpu.CompilerParams(has_side_effects=_EFFECT),
    )(*arrays, send_sems, recv_sems, after)
    return list(res[:n]), list(res[n:2 * n]), list(res[2 * n:])


def _chip_sum(pbs, qs, own_idx, states, name):
    n = len(pbs)
    halves = 2
    n_in = 4 * n + 3 * sum(s is not None for s in states)

    def body(o_ref, *refs):
        nxt_in, nxt_out = 4 * n, n_in
        for k in range(n):
            acc = refs[4 * k][...].astype(F32)
            for j in range(1, 4):
                acc = acc + refs[4 * k + j][...].astype(F32)
            refs[nxt_out][...] = acc
            nxt_out += 1
            if states[k] is not None:
                w_ref, m_ref, v_ref = refs[nxt_in:nxt_in + 3]
                nxt_in += 3
                d, nm, nv = _adamw_math(w_ref[...], acc, m_ref[...], v_ref[...])
                refs[nxt_out][...] = d
                refs[nxt_out + 1][...] = nm
                refs[nxt_out + 2][...] = nv
                nxt_out += 3

    def other(j):
        return lambda h, o: ((o[0] + j) % 4, h, 0)

    def rows(r):
        return pl.BlockSpec((r // halves, D), lambda h, o: (h, 0))

    in_specs, ins = [], []
    for p, q in zip(pbs, qs):
        rh = p.shape[1] // halves
        in_specs.append(pl.BlockSpec((None, rh, D), lambda h, o: (o[0], h, 0)))
        ins.append(p)
        for j in range(1, 4):
            in_specs.append(pl.BlockSpec((None, rh, D), other(j)))
            ins.append(q)
    out_specs, out_shape, counts = [], [], []
    for p, s in zip(pbs, states):
        r = p.shape[1]
        if s is not None:
            in_specs += [rows(r)] * 3
            ins += list(s)
        counts.append(1 if s is None else 4)
        out_specs += [rows(r)] * counts[-1]
        out_shape += [_sds((r, D), F32)] * counts[-1]
    res = pl.pallas_call(
        body, name=name,
        grid_spec=pltpu.PrefetchScalarGridSpec(num_scalar_prefetch=1, grid=(halves,), in_specs=in_specs,
                                               out_specs=out_specs),
        out_shape=out_shape,
        compiler_params=pltpu.CompilerParams(dimension_semantics=("arbitrary",), vmem_limit_bytes=56 * 2 ** 20),
    )(own_idx, *ins)
    out, base = [], 0
    for c in counts:
        out.append(tuple(res[base:base + c]))
        base += c
    return out


REDUCE_GROUPS = (("w_gateT", "w_upT", "w_down"), ("w_out", "w_mem_q", "w_mem_kvT", "w_mem_o"), ("w_inT",))


class _Reduce:
    def __init__(self, states):
        x, y, c = _place()
        self._c = c.astype(jnp.int32).reshape(1)
        self._own = (2 * x + y).astype(jnp.int32).reshape(1)
        self._me = _dev_index((x, y, c))
        self._state = {}
        self._adam = states
        self.results = {}
        self.smalls = None

    def start(self, gi, wgrads):
        gbs = [g.reshape(NDEV, g.shape[0] // NDEV, D) for g in wgrads]
        gbs, lands, send_sems, recv_sems, token = _sibling_start(gbs, ID_SIBLING + gi, "reduce_sibling_start_%d" % gi)
        self._state[gi] = (gbs, lands, send_sems, recv_sems)
        return token

    def middle(self, gi, after, smalls=()):
        gbs, lands, send_sems, recv_sems = self._state[gi]
        gbs, ts = _sibling_wait(gbs, lands, send_sems, recv_sems, after, "reduce_sibling_wait_%d" % gi)
        pbs = _pair_sum(gbs, ts, self._c, "reduce_pair_sum_%d" % gi)
        self._state[gi], token = _chip_start(pbs, list(smalls), ID_CHIP + gi, "reduce_chip_start_%d" % gi)
        return token

    def finish(self, gi, after):
        pbs, lands, smalls, send_sems, recv_sems = self._state[gi]
        pbs, qs, smalls = _chip_wait(pbs, lands, smalls, send_sems, recv_sems, after, "reduce_chip_wait_%d" % gi)
        names = REDUCE_GROUPS[gi]
        states = [self._adam.get(n) for n in names]
        self.results.update(zip(names, _chip_sum(pbs, qs, self._own, states, "reduce_update_%d" % gi)))
        if smalls:
            self.smalls = smalls

    def small_zones(self, small, g_wdw):
        pack = jnp.concatenate(
            [jnp.pad(small[name], ((0, 0), (0, lanes - small[name].shape[1]))) for name, lanes, _ in PACK], axis=1)
        zones = []
        for a in (pack, g_wdw):
            zone = lax.empty((NDEV,) + a.shape, a.dtype)
            zones.append(lax.dynamic_update_slice(zone, a[None], (self._me,) + (0,) * a.ndim))
        return zones


def _adamw_math(w, g, m, v):
    m = ADAM_B1 * m + (1.0 - ADAM_B1) * g
    v = ADAM_B2 * v + (1.0 - ADAM_B2) * (g * g)
    m_hat = m / (1.0 - ADAM_B1 ** ADAM_STEP)
    v_hat = v / (1.0 - ADAM_B2 ** ADAM_STEP)
    delta = -ADAM_LR * (m_hat / (jnp.sqrt(v_hat) + ADAM_EPS) + ADAM_WD * w)
    return delta, m, v


def _adamw(w, g, m, v, name):
    R, C = w.shape
    rc = 256 if R % 256 == 0 and R > 256 else R

    def body(w_ref, g_ref, m_ref, v_ref, d_ref, nm_ref, nv_ref):
        d, nm, nv = _adamw_math(w_ref[...], g_ref[...], m_ref[...], v_ref[...])
        d_ref[...] = d
        nm_ref[...] = nm
        nv_ref[...] = nv

    return _hbm_call(
        body, name=name, grid=(R // rc,),
        in_specs=[_rows(rc, C)] * 4, out_specs=[_rows(rc, C)] * 3,
        out_shape=[_sds((R, C), F32)] * 3,
        compiler_params=_params(40),
    )(w, g, m, v)


PACK = (("loss", LANES, 1), ("g_mix", D, D), ("b_in", INC, INC), ("b_dw", CC, CC), ("g_conv_ln", CC, CC),
        ("b_conv_ln", CC, CC), ("attn_sink", LANES, NQ), ("b_out", D, D), ("g_mem_q", D, D), ("g_mem_kv", D, D),
        ("g_ffn", D, D), ("g_final", D, D))
PACK_LANES = sum(p[1] for p in PACK)


def _small_update(packs, gdws, ws, ms, vs):
    names = [p[0] for p in PACK[1:]]
    n = len(names)

    def body(*refs):
        packs_ref, gdws_ref = refs[0], refs[1]
        w_refs, m_refs, v_refs = refs[2:2 + n], refs[2 + n:2 + 2 * n], refs[2 + 2 * n:2 + 3 * n]
        outs = refs[2 + 3 * n:]
        tot = packs_ref[0]
        gdw = gdws_ref[0]
        for s in range(1, NDEV):
            tot = tot + packs_ref[s]
            gdw = gdw + gdws_ref[s]
        outs[0][...] = tot[:, 0:LANES]
        outs[1][...] = gdw
        off = LANES
        for k, (_, lanes, used) in enumerate(PACK[1:]):
            g = tot[:, off:off + used]
            off += lanes
            d, nm, nv = _adamw_math(w_refs[k][...], g, m_refs[k][...], v_refs[k][...])
            o = outs[2 + 4 * k:6 + 4 * k]
            o[0][...] = g
            o[1][...] = d
            o[2][...] = nm
            o[3][...] = nv

    def full(a):
        nd = len(a.shape)
        return pl.BlockSpec(a.shape, lambda i: (0,) * nd)

    ins = [packs, gdws, *ws, *ms, *vs]
    out_shape = [_sds((1, LANES), F32), _sds(gdws.shape[1:], F32)]
    for w in ws:
        out_shape += [_sds(w.shape, F32)] * 4
    return pl.pallas_call(
        body, name="small_update", grid=(1,),
        in_specs=[full(a) for a in ins], out_specs=[full(o) for o in out_shape], out_shape=out_shape,
        compiler_params=_params(40),
    )(*ins)


def kernel(x, mem, g_mix, w_in, b_in, w_dw, b_dw, g_conv_ln, b_conv_ln, attn_sink, w_out, b_out, g_mem_q, g_mem_kv, w_mem_q, w_mem_kv, w_mem_o, g_ffn, w_gate, w_up, w_down, g_final, loss_target, m_g_mix, m_w_in, m_b_in, m_w_dw, m_b_dw, m_g_conv_ln, m_b_conv_ln, m_attn_sink, m_w_out, m_b_out, m_g_mem_q, m_g_mem_kv, m_w_mem_q, m_w_mem_kv, m_w_mem_o, m_g_ffn, m_w_gate, m_w_up, m_w_down, m_g_final, v_g_mix, v_w_in, v_b_in, v_w_dw, v_b_dw, v_g_conv_ln, v_b_conv_ln, v_attn_sink, v_w_out, v_b_out, v_g_mem_q, v_g_mem_kv, v_w_mem_q, v_w_mem_kv, v_w_mem_o, v_g_ffn, v_w_gate, v_w_up, v_w_down, v_g_final):
    given = dict(locals())
    bf = jnp.bfloat16
    xi, yi, ci = _place()
    me = 4 * xi + 2 * yi + ci

    transposed = {"w_inT": w_in, "w_mem_kvT": w_mem_kv, "w_gateT": w_gate, "w_upT": w_up}
    plain = {"w_out": w_out, "w_mem_q": w_mem_q, "w_mem_o": w_mem_o, "w_down": w_down}
    shards = [transposed[n][0].T if n in transposed else plain[n][0] for n in W_NAMES]
    zones = dict(zip(W_NAMES, _landing_zones(shards, me.astype(jnp.int32).reshape(1))))
    dw = jnp.pad(w_dw[0], ((0, 1), (0, 0)))
    zones["w_dw"] = lax.dynamic_update_slice(lax.empty((NDEV,) + dw.shape, dw.dtype), dw[None], (me, 0, 0))
    gather = _Gather(_gather_start([[zones[n] for n in names] for names in GATHER_GROUPS]))

    sp = {n: given[n] for n in ("g_mix", "b_in", "b_dw", "g_conv_ln", "b_conv_ln", "attn_sink", "b_out", "g_mem_q",
                                "g_mem_kv", "g_ffn")}
    sp["g_final"] = g_final.reshape(1, D)
    frames = {"w_inT": ("w_in", True), "w_gateT": ("w_gate", True), "w_upT": ("w_up", True),
              "w_out": ("w_out", False), "w_mem_q": ("w_mem_q", False), "w_mem_o": ("w_mem_o", False),
              "w_down": ("w_down", False)}
    states = {}
    for frame, (n, transposed) in frames.items():
        wmv = (given[n][0], given["m_" + n][0], given["v_" + n][0])
        states[frame] = tuple(a.T for a in wmv) if transposed else wmv
    reduce = _Reduce(states)
    grad_x = _local_step(x[0], mem[0], loss_target[0], sp, gather, reduce)
    reduce.finish(2, reduce.results["w_out"][1])

    grads, deltas, new_m, new_v = {}, {}, {}, {}
    for frame, (n, transposed) in frames.items():
        outs = [a.T if transposed else a for a in reduce.results[frame]]
        grads[n], deltas[n], new_m[n], new_v[n] = [a[None] for a in outs]

    def update(n, g):
        d, nm, nv = _adamw(given[n][0], g, given["m_" + n][0], given["v_" + n][0], "adamw_" + n)
        grads[n], deltas[n], new_m[n], new_v[n] = g[None], d[None], nm[None], nv[None]

    update("w_mem_kv", reduce.results["w_mem_kvT"][0].T)

    packs, gdws = reduce.smalls
    names_small = [p[0] for p in PACK[1:]]

    def as_row(a):
        return a.reshape(1, -1)

    res = _small_update(packs, gdws, [as_row(given[n]) for n in names_small],
                        [as_row(given["m_" + n]) for n in names_small],
                        [as_row(given["v_" + n]) for n in names_small])
    loss = res[0][0, 0]
    for k, n in enumerate(names_small):
        shape = given[n].shape
        g, d, nm, nv = res[2 + 4 * k:6 + 4 * k]
        grads[n], deltas[n], new_m[n], new_v[n] = (g.reshape(shape), d.reshape(shape), nm.reshape(shape),
                                                   nv.reshape(shape))
    update("w_dw", lax.dynamic_slice(res[1], (0, me * (CC // NDEV)), (TAPS, CC // NDEV)))

    order = ("g_mix", "w_in", "b_in", "w_dw", "b_dw", "g_conv_ln", "b_conv_ln", "attn_sink", "w_out", "b_out",
             "g_mem_q", "g_mem_kv", "w_mem_q", "w_mem_kv", "w_mem_o", "g_ffn", "w_gate", "w_up", "w_down", "g_final")
    return (loss, grad_x[None], *[grads[n] for n in order], *[deltas[n] for n in order],
            *[new_m[n] for n in order], *[new_v[n] for n in order])
```

```python
import functools

import jax
import jax.numpy as jnp
from jax import lax
from jax.experimental import pallas as pl
from jax.experimental.pallas import tpu as pltpu

D = 1024
CC = 512
HD = 64
NQ = 8
NKV = 2
GRP = NQ // NKV
ATT = NQ * HD
KVC = NKV * HD
BLK = 128
MH = 4
MHD = D // MH
F = 2816
FCH = 256
FCH_BWD = F
INC = 2 * CC + ATT + 2 * KVC
TAPS = 31
EPS = 1e-6
ROPE_THETA = 10000.0
SCALE = HD ** -0.5
MSCALE = MHD ** -0.5
LANES = 128
NDEV = 8

ADAM_LR = 0.001
ADAM_B1 = 0.9
ADAM_B2 = 0.999
ADAM_EPS = 1e-08
ADAM_WD = 0.01
ADAM_STEP = 10

F32 = jnp.float32
_MXU = jnp.bfloat16
MESH = pl.DeviceIdType.MESH

W_NAMES = ("w_inT", "w_out", "w_mem_q", "w_mem_kvT", "w_mem_o", "w_gateT", "w_upT", "w_down")
W_ROWS = (INC // NDEV, D // NDEV, D // NDEV, 2 * D // NDEV, D // NDEV, F // NDEV, F // NDEV, F // NDEV)


def _mm(a, b, ca, cb):
    return lax.dot_general(a.astype(_MXU), b.astype(_MXU), (((ca,), (cb,)), ((), ())),
                           preferred_element_type=F32)


def _nn(a, b):
    return _mm(a, b, 1, 0)


def _nt(a, b):
    return _mm(a, b, 1, 1)


def _tn(a, b):
    return _mm(a, b, 0, 0)


def _sigmoid(x):
    return 0.5 * jnp.tanh(0.5 * x) + 0.5


def _rms(x, g):
    r = lax.rsqrt(jnp.mean(x * x, axis=-1, keepdims=True) + EPS)
    return x * r * g, r


def _rms_bwd(dy, x, r, g):
    xh = x * r
    dg = jnp.sum(dy * xh, axis=0, keepdims=True)
    dxh = dy * g
    dx = r * (dxh - xh * jnp.mean(dxh * xh, axis=-1, keepdims=True))
    return dx, dg


def _rope(t, c, s):
    n = t.shape[1]
    reps = n // LANES
    if reps > 1:
        c = jnp.tile(c, (1, reps))
        s = jnp.tile(s, (1, reps))
    lane = lax.broadcasted_iota(jnp.int32, t.shape, 1)
    first = (lane & (HD - 1)) < (HD // 2)
    partner = jnp.where(first, pltpu.roll(t, n - HD // 2, 1), pltpu.roll(t, HD // 2, 1))
    return t * c + partner * s


def _rope_bwd(dt, c, s):
    n = dt.shape[1]
    reps = n // LANES
    if reps > 1:
        c = jnp.tile(c, (1, reps))
        s = jnp.tile(s, (1, reps))
    lane = lax.broadcasted_iota(jnp.int32, dt.shape, 1)
    first = (lane & (HD - 1)) < (HD // 2)
    ds = dt * s
    partner = jnp.where(first, pltpu.roll(ds, n - HD // 2, 1), pltpu.roll(ds, HD // 2, 1))
    return dt * c + partner


def _acc_init(i, *refs):
    @pl.when(i == 0)
    def _():
        for r in refs:
            r[...] = jnp.zeros(r.shape, r.dtype)


def _acc_row(ref, v):
    ref[...] += jnp.broadcast_to(v, ref.shape)


def _rows(tm, n):
    return pl.BlockSpec((tm, n), lambda i: (i, 0))


def _full(shape):
    nd = len(shape)
    return pl.BlockSpec(shape, lambda i: (0,) * nd)


def _once(shape):
    nd = len(shape)
    return pl.BlockSpec(shape, lambda i: (0,) * nd, pipeline_mode=pl.Buffered(1))


def _params(vmem_mb):
    return pltpu.CompilerParams(dimension_semantics=("arbitrary",), vmem_limit_bytes=vmem_mb * 2 ** 20)


def _sds(shape, dtype):
    return pltpu.HBM(shape, dtype)


def _hbm_call(*args, **kwargs):
    call = pl.pallas_call(*args, **kwargs)
    return lambda *ops: call(*[pltpu.with_memory_space_constraint(o, pltpu.HBM) for o in ops])


def _mem_kv_fwd(mem, g_kv, wkvT):
    M = mem.shape[0]

    def body(mem_ref, g_ref, w_ref, km_ref, vm_ref, mn_ref):
        y, _ = _rms(mem_ref[...], g_ref[...])
        kv = _nt(y, w_ref[...])
        km_ref[...] = kv[:, :D].astype(km_ref.dtype)
        vm_ref[...] = kv[:, D:].astype(vm_ref.dtype)
        mn_ref[...] = y.astype(mn_ref.dtype)

    return _hbm_call(
        body, name="mem_kv_fwd", grid=(1,),
        in_specs=[_full((M, D)), _full((1, D)), _full((2 * D, D))],
        out_specs=[_full((M, D))] * 3,
        out_shape=[_sds((M, D), _MXU)] * 3,
        compiler_params=_params(40),
    )(mem, g_kv, wkvT)


def _in_proj_fwd(x, g_mix, winT, b_in, cosq, sinq, tm=512):
    S = x.shape[0]

    def body(x_ref, g_ref, w_ref, b_ref, c_ref, s_ref, h_ref, uglu_ref, vg_ref, q_ref, k_ref, v_ref):
        h, _ = _rms(x_ref[...], g_ref[...])
        hb = h.astype(h_ref.dtype)
        h_ref[...] = hb
        u = _nt(hb, w_ref[...]) + b_ref[...]
        uglu_ref[...] = u[:, :2 * CC].astype(uglu_ref.dtype)
        vg_ref[...] = u[:, :CC] * _sigmoid(u[:, CC:2 * CC])
        c = c_ref[...]
        s = s_ref[...]
        q_ref[...] = (_rope(u[:, 2 * CC:2 * CC + ATT], c, s) * SCALE).astype(q_ref.dtype)
        k_ref[...] = _rope(u[:, 2 * CC + ATT:2 * CC + ATT + KVC], c, s).astype(k_ref.dtype)
        v_ref[...] = u[:, 2 * CC + ATT + KVC:].astype(v_ref.dtype)

    return _hbm_call(
        body, name="in_proj_fwd", grid=(S // tm,),
        in_specs=[_rows(tm, D), _full((1, D)), _full((INC, D)), _full((1, INC)), _rows(tm, LANES), _rows(tm, LANES)],
        out_specs=[_rows(tm, D), _rows(tm, 2 * CC), _rows(tm, CC), _rows(tm, ATT), _rows(tm, KVC), _rows(tm, KVC)],
        out_shape=[_sds((S, D), _MXU), _sds((S, 2 * CC), _MXU), _sds((S, CC), F32), _sds((S, ATT), _MXU),
                   _sds((S, KVC), _MXU), _sds((S, KVC), _MXU)],
        compiler_params=_params(48),
    )(x, g_mix, winT, b_in, cosq, sinq)


CONV_HALO = 16


CONV_TILE = 512


def _shift_scratch(tm, sets):
    return pltpu.VMEM((sets, CC // LANES, 8, tm + 24, LANES), F32)


def _shifted_copies(pad_ref, sh_ref, t0, tm):
    for cl in range(CC // LANES):
        blk = pad_ref[pl.ds(t0, tm + 2 * CONV_HALO), cl * LANES:(cl + 1) * LANES]
        for b in range(8):
            sh_ref[cl, b, :, :] = blk[b:b + tm + 24, :]


def _conv_taps():
    return [(8 * a + b - 1, b, 8 * a) for b in range(8) for a in range(4) if 0 <= 8 * a + b - 1 < TAPS]


CONV_ROWS = 64
SUBLANES = 8


def _dwconv_tile(sh_ref, tm, w_ref, flip, out_ref):
    for cl in range(CC // LANES):
        lanes = slice(cl * LANES, (cl + 1) * LANES)

        def chunk(c, carry):
            r = pl.multiple_of(c * CONV_ROWS, CONV_ROWS)
            accs = [None] * 4
            for n, (k, b, off) in enumerate(_conv_taps()):
                kk = TAPS - 1 - k if flip else k
                term = sh_ref[cl, b, pl.ds(off + r, CONV_ROWS), :] * w_ref[kk:kk + 1, lanes]
                accs[n % 4] = term if accs[n % 4] is None else accs[n % 4] + term
            out_ref[pl.ds(r, CONV_ROWS), lanes] = (accs[0] + accs[1]) + (accs[2] + accs[3])
            return carry

        lax.fori_loop(0, tm // CONV_ROWS, chunk, 0)


def _fill_padded(i, src_hbm, pad_ref, sem, S):
    @pl.when(i == 0)
    def _():
        zeros = jnp.zeros((CONV_HALO, pad_ref.shape[1]), pad_ref.dtype)
        pad_ref[0:CONV_HALO, :] = zeros
        pad_ref[CONV_HALO + S:2 * CONV_HALO + S, :] = zeros
        cp = pltpu.make_async_copy(src_hbm, pad_ref.at[pl.ds(CONV_HALO, S), :], sem)
        cp.start()
        cp.wait()


def _ln_silu(y, g, b):
    mu = jnp.mean(y, axis=-1, keepdims=True)
    yc = y - mu
    var = jnp.mean(yc * yc, axis=-1, keepdims=True)
    rstd = lax.rsqrt(var + EPS)
    yh = yc * rstd
    z = yh * g + b
    return yh, rstd, z


def _conv_fwd(vg, w_dw, b_dw, ln_g, ln_b):
    S = vg.shape[0]
    tm = min(CONV_TILE, S)

    def body(vg_hbm, w_ref, b_ref, g_ref, bb_ref, y_ref, yc_ref, pad_ref, sh_ref, sem):
        i = pl.program_id(0)
        _fill_padded(i, vg_hbm, pad_ref, sem, S)
        _shifted_copies(pad_ref, sh_ref.at[0], pl.multiple_of(i * tm, tm), tm)
        _dwconv_tile(sh_ref.at[0], tm, w_ref, False, y_ref)
        y = y_ref[...] + b_ref[...]
        y_ref[...] = y
        _, _, z = _ln_silu(y, g_ref[...], bb_ref[...])
        yc_ref[...] = (z * _sigmoid(z)).astype(yc_ref.dtype)

    return _hbm_call(
        body, name="conv_fwd", grid=(S // tm,),
        in_specs=[pl.BlockSpec(memory_space=pl.ANY), _full((TAPS, CC)), _full((1, CC)), _full((1, CC)), _full((1, CC))],
        out_specs=[_rows(tm, CC), _rows(tm, CC)],
        out_shape=[_sds((S, CC), F32), _sds((S, CC), _MXU)],
        scratch_shapes=[pltpu.VMEM((S + 2 * CONV_HALO, CC), F32), _shift_scratch(tm, 1), pltpu.SemaphoreType.DMA],
        compiler_params=_params(40),
    )(vg, w_dw, b_dw, ln_g, ln_b)


def _attn_window(i, S):
    start = jnp.clip((i - 1) * BLK, 0, S - 3 * BLK)
    return pl.multiple_of(start, BLK)


ATTN_COLS = GRP * BLK


def _attn_mask_base():
    keys = lax.broadcasted_iota(jnp.int32, (3 * BLK, ATTN_COLS), 0)
    cols = lax.broadcasted_iota(jnp.int32, (3 * BLK, ATTN_COLS), 1)
    return keys - (cols & (BLK - 1))


def _attn_mask(base, i, start):
    return jnp.abs(base + (start - i * BLK)) <= BLK


def _stack_heads(t, hk):
    return jnp.concatenate([t[:, (hk * GRP + g) * HD:(hk * GRP + g + 1) * HD] for g in range(GRP)], axis=0)


def _unstack_heads_t(t):
    cols = []
    for p in range(GRP // 2):
        pair = jnp.concatenate([t[:, (2 * p) * BLK:(2 * p + 1) * BLK], t[:, (2 * p + 1) * BLK:(2 * p + 2) * BLK]],
                               axis=0)
        cols.append(pair.T)
    return jnp.concatenate(cols, axis=1)


def _attn_probs(qs, kh, sk_ref, hk, mask):
    s = jnp.where(mask, _nt(kh, qs), -1e30)
    sk = jnp.concatenate(
        [jnp.broadcast_to(sk_ref[0:1, hk * GRP + g:hk * GRP + g + 1], (1, BLK)) for g in range(GRP)], axis=1)
    m = jnp.maximum(jnp.max(s, axis=0, keepdims=True), sk)
    e = jnp.exp(s - m)
    es = jnp.exp(sk - m)
    inv = 1.0 / (jnp.sum(e, axis=0, keepdims=True) + es)
    return e, inv, es


ATTN_BLOCKS = 8


def _attn_fwd(qr, kr, vv, sink):
    S = qr.shape[0]
    tq = ATTN_BLOCKS * BLK

    def body(q_ref, k_ref, v_ref, sk_ref, o_ref):
        base = _attn_mask_base()
        for b in range(ATTN_BLOCKS):
            i = pl.program_id(0) * ATTN_BLOCKS + b
            rows = slice(b * BLK, (b + 1) * BLK)
            start = _attn_window(i, S)
            kb = k_ref[pl.ds(start, 3 * BLK), :]
            vb = v_ref[pl.ds(start, 3 * BLK), :]
            q = q_ref[rows, :]
            mask = _attn_mask(base, i, start)
            outs = []
            for hk in range(NKV):
                hs = slice(hk * HD, (hk + 1) * HD)
                e, inv, _ = _attn_probs(_stack_heads(q, hk), kb[:, hs], sk_ref, hk, mask)
                outs.append(_unstack_heads_t(_tn(vb[:, hs], e) * inv))
            o_ref[rows, :] = jnp.concatenate(outs, axis=1).astype(o_ref.dtype)

    return _hbm_call(
        body, name="attn_fwd", grid=(S // tq,),
        in_specs=[_rows(tq, ATT), _full((S, KVC)), _full((S, KVC)), _full((1, NQ))],
        out_specs=_rows(tq, ATT),
        out_shape=_sds((S, ATT), _MXU),
        compiler_params=_params(40),
    )(qr, kr, vv, sink)


def _out_proj_fwd(x, yc, ya, wout, b_out, tm=512):
    S = x.shape[0]

    def body(x_ref, yc_ref, ya_ref, w_ref, b_ref, x1_ref, ymix_ref):
        ymix = jnp.concatenate([yc_ref[...], ya_ref[...]], axis=1)
        ymix_ref[...] = ymix
        x1_ref[...] = x_ref[...] + _nn(ymix, w_ref[...]) + b_ref[...]

    return _hbm_call(
        body, name="out_proj_fwd", grid=(S // tm,),
        in_specs=[_rows(tm, D), _rows(tm, CC), _rows(tm, ATT), _full((D, D)), _full((1, D))],
        out_specs=[_rows(tm, D), _rows(tm, D)],
        out_shape=[_sds((S, D), F32), _sds((S, D), _MXU)],
        compiler_params=_params(40),
    )(x, yc, ya, wout, b_out)


def _softmax_t(s):
    m = jnp.max(s, axis=0, keepdims=True)
    e = jnp.exp(s - m)
    return e, 1.0 / jnp.sum(e, axis=0, keepdims=True)


def _softmax(s):
    m = jnp.max(s, axis=-1, keepdims=True)
    e = jnp.exp(s - m)
    return e * (1.0 / jnp.sum(e, axis=-1, keepdims=True))


def _mem_attn_fwd(x1, g_q, wq, km, vm, wo, after, tm=512):
    S = x1.shape[0]
    M = km.shape[0]

    def body(x1_ref, g_ref, wq_ref, km_ref, vm_ref, wo_ref, after_ref, x2_ref, qm_ref, om_ref):
        x1v = x1_ref[...]
        hq, _ = _rms(x1v, g_ref[...])
        qm = (_nn(hq, wq_ref[...]) * MSCALE).astype(qm_ref.dtype)
        qm_ref[...] = qm
        oms = []
        for h in range(MH):
            sl = slice(h * MHD, (h + 1) * MHD)
            p = _softmax(_nt(qm[:, sl], km_ref[:, sl]))
            oms.append(_nn(p, vm_ref[:, sl]))
        om = jnp.concatenate(oms, axis=1).astype(om_ref.dtype)
        om_ref[...] = om
        x2_ref[...] = x1v + _nn(om, wo_ref[...])

    return _hbm_call(
        body, name="mem_attn_fwd", grid=(S // tm,),
        in_specs=[_rows(tm, D), _full((1, D)), _full((D, D)), _full((M, D)), _full((M, D)), _full((D, D)), _AFTER],
        out_specs=[_rows(tm, D)] * 3,
        out_shape=[_sds((S, D), F32), _sds((S, D), _MXU), _sds((S, D), _MXU)],
        compiler_params=_params(48),
    )(x1, g_q, wq, km, vm, wo, after)


def _ffn_loss_fwd(x2, tgt, g_ffn, g_final, wgT, wuT, wd, tm=512):
    S = x2.shape[0]

    def body(x2_ref, t_ref, gf_ref, gl_ref, wg_ref, wu_ref, wd_ref,
             gt_ref, up_ref, hf_ref, dx3_ref, dx3b_ref, loss_ref, dgl_ref):
        i = pl.program_id(0)
        _acc_init(i, loss_ref, dgl_ref)
        x2v = x2_ref[...]
        hf, _ = _rms(x2v, gf_ref[...])
        hb = hf.astype(hf_ref.dtype)
        hf_ref[...] = hb
        x3 = x2v
        for c in range(F // FCH):
            sl = slice(c * FCH, (c + 1) * FCH)
            gt = _nt(hb, wg_ref[sl, :])
            up = _nt(hb, wu_ref[sl, :])
            gt_ref[:, sl] = gt.astype(gt_ref.dtype)
            up_ref[:, sl] = up.astype(up_ref.dtype)
            x3 = x3 + _nn(gt * _sigmoid(gt) * up, wd_ref[sl, :])
        gl = gl_ref[...]
        y, r = _rms(x3, gl)
        diff = y - t_ref[...]
        _acc_row(loss_ref, jnp.sum(jnp.sum(diff * diff, axis=-1, keepdims=True), axis=0, keepdims=True) * (0.5 / D))
        dx3, dgl = _rms_bwd(diff * (1.0 / D), x3, r, gl)
        dx3_ref[...] = dx3
        dx3b_ref[...] = dx3.astype(dx3b_ref.dtype)
        _acc_row(dgl_ref, dgl)

    return _hbm_call(
        body, name="ffn_loss_fwd", grid=(S // tm,),
        in_specs=[_rows(tm, D), _rows(tm, D), _full((1, D)), _full((1, D)),
                  _once((F, D)), _once((F, D)), _once((F, D))],
        out_specs=[_rows(tm, F), _rows(tm, F), _rows(tm, D), _rows(tm, D), _rows(tm, D),
                   _full((8, LANES)), _full((8, D))],
        out_shape=[_sds((S, F), _MXU), _sds((S, F), _MXU), _sds((S, D), _MXU), _sds((S, D), F32),
                   _sds((S, D), _MXU), _sds((8, LANES), F32), _sds((8, D), F32)],
        compiler_params=_params(56),
    )(x2, tgt, g_ffn, g_final, wgT, wuT, wd)


def _ffn_bwd(dx3, x2, gt, up, g_ffn, wd, wgT, wuT, tm=256):
    S = x2.shape[0]

    def body(dx3_ref, x2_ref, gt_ref, up_ref, g_ref, wd_ref, wg_ref, wu_ref,
             dx2_ref, dx2b_ref, dgt_ref, dup_ref, act_ref, dg_ref):
        i = pl.program_id(0)
        _acc_init(i, dg_ref)
        dx3v = dx3_ref[...]
        db = dx3v.astype(_MXU)
        dhf = jnp.zeros((tm, D), F32)
        for c in range(F // FCH_BWD):
            sl = slice(c * FCH_BWD, (c + 1) * FCH_BWD)
            gtv = gt_ref[:, sl].astype(F32)
            upv = up_ref[:, sl].astype(F32)
            dact = _nt(db, wd_ref[sl, :])
            sig = _sigmoid(gtv)
            silu = gtv * sig
            dup = (dact * silu).astype(dup_ref.dtype)
            dgt = (dact * upv * (sig * (1.0 + gtv * (1.0 - sig)))).astype(dgt_ref.dtype)
            act_ref[:, sl] = (silu * upv).astype(act_ref.dtype)
            dgt_ref[:, sl] = dgt
            dup_ref[:, sl] = dup
            dhf = dhf + _nn(dgt, wg_ref[sl, :]) + _nn(dup, wu_ref[sl, :])
        x2v = x2_ref[...]
        g = g_ref[...]
        _, r = _rms(x2v, g)
        dxn, dg = _rms_bwd(dhf, x2v, r, g)
        dx2 = dx3v + dxn
        dx2_ref[...] = dx2
        dx2b_ref[...] = dx2.astype(dx2b_ref.dtype)
        _acc_row(dg_ref, dg)

    return _hbm_call(
        body, name="ffn_bwd", grid=(S // tm,),
        in_specs=[_rows(tm, D), _rows(tm, D), _rows(tm, F), _rows(tm, F), _full((1, D)),
                  _once((F, D)), _once((F, D)), _once((F, D))],
        out_specs=[_rows(tm, D), _rows(tm, D), _rows(tm, F), _rows(tm, F), _rows(tm, F), _full((8, D))],
        out_shape=[_sds((S, D), F32), _sds((S, D), _MXU), _sds((S, F), _MXU), _sds((S, F), _MXU),
                   _sds((S, F), _MXU), _sds((8, D), F32)],
        compiler_params=_params(56),
    )(dx3, x2, gt, up, g_ffn, wd, wgT, wuT)


_AFTER = pl.BlockSpec(memory_space=pl.ANY)


def _wgrad(pairs, name, after, rc=256):
    S, R = pairs[0][0].shape
    N = pairs[0][1].shape[1]
    rights = []
    for _, b in pairs:
        if not any(b is r for r in rights):
            rights.append(b)
    which = [next(i for i, r in enumerate(rights) if r is b) for _, b in pairs]
    n, nb = len(pairs), len(rights)

    def body(*refs):
        for k in range(n):
            refs[n + nb + 1 + k][...] = _tn(refs[k][...], refs[n + which[k]][...]).astype(jnp.bfloat16)

    res = _hbm_call(
        body, name=name, grid=(R // rc,),
        in_specs=[pl.BlockSpec((S, rc), lambda j: (0, j))] * n + [_once((S, N))] * nb + [_AFTER],
        out_specs=[_rows(rc, N)] * n,
        out_shape=[_sds((R, N), jnp.bfloat16)] * n,
        compiler_params=_params(56),
    )(*[a for a, _ in pairs], *rights, after)
    return list(res)


def _mem_attn_bwd(dx2, x1, qm, km, vm, g_q, wq, wo, wout, after, tm=512):
    S = x1.shape[0]
    M = km.shape[0]

    def body(dx2_ref, x1_ref, qm_ref, km_ref, vm_ref, g_ref, wq_ref, wo_ref, wout_ref, after_ref,
             dx1_ref, dx1b_ref, hq_ref, dqm_ref, dyc_ref, dya_ref, dkm_ref, dvm_ref, dg_ref, dbo_ref):
        i = pl.program_id(0)
        _acc_init(i, dkm_ref, dvm_ref, dg_ref, dbo_ref)
        dx2v = dx2_ref[...]
        domt = _nt(wo_ref[...], dx2v).astype(_MXU)
        dqs = []
        for h in range(MH):
            sl = slice(h * MHD, (h + 1) * MHD)
            qh = qm_ref[:, sl]
            kh = km_ref[:, sl]
            e, inv = _softmax_t(_nt(kh, qh))
            p = e * inv
            dp = _nn(vm_ref[:, sl], domt[sl, :])
            ds = (p * (dp - jnp.sum(dp * p, axis=0, keepdims=True))).astype(_MXU)
            dqs.append(_tn(ds, kh) * MSCALE)
            dkm_ref[:, sl] += _nn(ds, qh)
            dvm_ref[:, sl] += _nt(p, domt[sl, :])
        dqm = jnp.concatenate(dqs, axis=1).astype(dqm_ref.dtype)
        dqm_ref[...] = dqm
        dhq = _nt(dqm, wq_ref[...])
        x1v = x1_ref[...]
        g = g_ref[...]
        hq, r = _rms(x1v, g)
        hq_ref[...] = hq.astype(hq_ref.dtype)
        dxn, dg = _rms_bwd(dhq, x1v, r, g)
        dx1 = dx2v + dxn
        dx1_ref[...] = dx1
        dx1b = dx1.astype(dx1b_ref.dtype)
        dx1b_ref[...] = dx1b
        dym = _nt(dx1b, wout_ref[...])
        dyc_ref[...] = dym[:, :CC]
        dya_ref[...] = dym[:, CC:].astype(dya_ref.dtype)
        _acc_row(dg_ref, dg)
        _acc_row(dbo_ref, jnp.sum(dx1, axis=0, keepdims=True))

    return _hbm_call(
        body, name="mem_attn_bwd", grid=(S // tm,),
        in_specs=[_rows(tm, D), _rows(tm, D), _rows(tm, D), _full((M, D)), _full((M, D)), _full((1, D)),
                  _once((D, D)), _once((D, D)), _once((D, D)), _AFTER],
        out_specs=[_rows(tm, D), _rows(tm, D), _rows(tm, D), _rows(tm, D), _rows(tm, CC), _rows(tm, ATT),
                   _full((M, D)), _full((M, D)), _full((8, D)), _full((8, D))],
        out_shape=[_sds((S, D), F32), _sds((S, D), _MXU), _sds((S, D), _MXU), _sds((S, D), _MXU),
                   _sds((S, CC), F32), _sds((S, ATT), _MXU),
                   _sds((M, D), F32), _sds((M, D), F32), _sds((8, D), F32), _sds((8, D), F32)],
        compiler_params=_params(56),
    )(dx2, x1, qm, km, vm, g_q, wq, wo, wout, after)


def _mem_kv_bwd(dkm, dvm, mem, g_kv, memn, wkvT):
    M = mem.shape[0]

    def body(dkm_ref, dvm_ref, mem_ref, g_ref, mn_ref, w_ref, gw_ref, dg_ref):
        dkv = jnp.concatenate([dkm_ref[...], dvm_ref[...]], axis=1).astype(_MXU)
        gw_ref[...] = _tn(dkv, mn_ref[...]).astype(gw_ref.dtype)
        dmn = _nn(dkv, w_ref[...])
        memv = mem_ref[...]
        g = g_ref[...]
        _, r = _rms(memv, g)
        _, dg = _rms_bwd(dmn, memv, r, g)
        dg_ref[...] = jnp.broadcast_to(dg, dg_ref.shape)

    return _hbm_call(
        body, name="mem_kv_bwd", grid=(1,),
        in_specs=[_full((M, D)), _full((M, D)), _full((M, D)), _full((1, D)), _full((M, D)), _full((2 * D, D))],
        out_specs=[_full((2 * D, D)), _full((8, D))],
        out_shape=[_sds((2 * D, D), jnp.bfloat16), _sds((8, D), F32)],
        compiler_params=_params(48),
    )(dkm, dvm, mem, g_kv, memn, wkvT)


def _attn_bwd(qr, kr, vv, sink, dya, after):
    S = qr.shape[0]
    tq = ATTN_BLOCKS * BLK

    def body(q_ref, k_ref, v_ref, sk_ref, do_ref, after_ref, dq_ref, dk_ref, dv_ref, dsk_ref, sink_acc):
        _acc_init(pl.program_id(0), dk_ref, dv_ref, sink_acc)
        base = _attn_mask_base()
        for b in range(ATTN_BLOCKS):
            i = pl.program_id(0) * ATTN_BLOCKS + b
            rows = slice(b * BLK, (b + 1) * BLK)
            start = _attn_window(i, S)
            kb = k_ref[pl.ds(start, 3 * BLK), :]
            vb = v_ref[pl.ds(start, 3 * BLK), :]
            q = q_ref[rows, :]
            do = do_ref[rows, :]
            mask = _attn_mask(base, i, start)
            dqs, dks, dvs = [], [], []
            for hk in range(NKV):
                hs = slice(hk * HD, (hk + 1) * HD)
                qs = _stack_heads(q, hk)
                dos = _stack_heads(do, hk)
                e, inv, es = _attn_probs(qs, kb[:, hs], sk_ref, hk, mask)
                p = e * inv
                dp = _nt(vb[:, hs], dos)
                delta = jnp.sum(dp * p, axis=0, keepdims=True)
                ds = (p * (dp - delta)).astype(_MXU)
                dqs.append(_unstack_heads_t(_tn(kb[:, hs], ds) * SCALE))
                dks.append(_nn(ds, qs))
                dvs.append(_nn(p, dos))
                sink_acc[hk:hk + 1, :] -= es * inv * delta
            dq_ref[rows, :] = jnp.concatenate(dqs, axis=1)
            dk_ref[pl.ds(start, 3 * BLK), :] += jnp.concatenate(dks, axis=1)
            dv_ref[pl.ds(start, 3 * BLK), :] += jnp.concatenate(dvs, axis=1)

        @pl.when(pl.program_id(0) == pl.num_programs(0) - 1)
        def _():
            for h in range(NQ):
                tot = jnp.sum(sink_acc[h // GRP:h // GRP + 1, (h % GRP) * BLK:(h % GRP + 1) * BLK], axis=1,
                              keepdims=True)
                dsk_ref[h:h + 1, :] = jnp.broadcast_to(tot, (1, LANES))

    return _hbm_call(
        body, name="attn_bwd", grid=(S // tq,),
        in_specs=[_rows(tq, ATT), _full((S, KVC)), _full((S, KVC)), _full((1, NQ)), _rows(tq, ATT), _AFTER],
        out_specs=[_rows(tq, ATT), _full((S, KVC)), _full((S, KVC)), _full((8, LANES))],
        out_shape=[_sds((S, ATT), F32), _sds((S, KVC), F32), _sds((S, KVC), F32), _sds((8, LANES), F32)],
        scratch_shapes=[pltpu.VMEM((8, ATTN_COLS), F32)],
        compiler_params=_params(40),
    )(qr, kr, vv, sink, dya, after)


def _conv_norm_bwd(dyc, y, ln_g, ln_b, after, tm=512):
    S = y.shape[0]

    def body(dyc_ref, y_ref, g_ref, b_ref, after_ref, dy_ref, dg_ref, db_ref, dbdw_ref):
        i = pl.program_id(0)
        _acc_init(i, dg_ref, db_ref, dbdw_ref)
        g = g_ref[...]
        yh, rstd, z = _ln_silu(y_ref[...], g, b_ref[...])
        sig = _sigmoid(z)
        dz = dyc_ref[...] * (sig * (1.0 + z * (1.0 - sig)))
        dyh = dz * g
        dy = rstd * (dyh - jnp.mean(dyh, axis=-1, keepdims=True)
                     - yh * jnp.mean(dyh * yh, axis=-1, keepdims=True))
        dy_ref[...] = dy
        _acc_row(dg_ref, jnp.sum(dz * yh, axis=0, keepdims=True))
        _acc_row(db_ref, jnp.sum(dz, axis=0, keepdims=True))
        _acc_row(dbdw_ref, jnp.sum(dy, axis=0, keepdims=True))

    return _hbm_call(
        body, name="conv_norm_bwd", grid=(S // tm,),
        in_specs=[_rows(tm, CC), _rows(tm, CC), _full((1, CC)), _full((1, CC)), _AFTER],
        out_specs=[_rows(tm, CC), _full((8, CC)), _full((8, CC)), _full((8, CC))],
        out_shape=[_sds((S, CC), F32), _sds((8, CC), F32), _sds((8, CC), F32), _sds((8, CC), F32)],
        compiler_params=_params(40),
    )(dyc, y, ln_g, ln_b, after)


def _conv_bwd(dy, vg, uglu, w_dw):
    S = dy.shape[0]
    tm = min(CONV_TILE, S)

    def body(dy_hbm, vg_hbm, uglu_ref, w_ref, du_ref, gw_ref, dbin_ref, dyp_ref, vgp_ref, gacc_ref, sh_ref, dvg_ref,
             sems):
        i = pl.program_id(0)
        _acc_init(i, gacc_ref, dbin_ref)
        _fill_padded(i, dy_hbm, dyp_ref, sems.at[0], S)
        _fill_padded(i, vg_hbm, vgp_ref, sems.at[1], S)
        t0 = pl.multiple_of(i * tm, tm)
        _shifted_copies(dyp_ref, sh_ref.at[0], t0, tm)
        _shifted_copies(vgp_ref, sh_ref.at[1], t0, tm)
        _dwconv_tile(sh_ref.at[0], tm, w_ref, True, dvg_ref)
        dvg = dvg_ref[...]
        sub = CONV_ROWS
        for cl in range(CC // LANES):
            lanes = slice(cl * LANES, (cl + 1) * LANES)
            for k, b, off in _conv_taps():
                part = jnp.zeros((8, LANES), F32)
                for r in range(0, tm, sub):
                    prod = (sh_ref[0, cl, 0, CONV_HALO + r:CONV_HALO + r + sub, :]
                            * sh_ref[1, cl, b, off + r:off + r + sub, :])
                    part = part + jnp.sum(prod.reshape(sub // 8, 8, LANES), axis=0)
                gacc_ref[k, :, lanes] += part
        a_v = uglu_ref[:, :CC].astype(F32)
        sg = _sigmoid(uglu_ref[:, CC:].astype(F32))
        da = dvg * sg
        dgate = dvg * a_v * sg * (1.0 - sg)
        du = jnp.concatenate([da, dgate], axis=1)
        du_ref[...] = du.astype(du_ref.dtype)
        _acc_row(dbin_ref, jnp.sum(du, axis=0, keepdims=True))

        @pl.when(i == pl.num_programs(0) - 1)
        def _():
            gw_ref[...] = jnp.zeros(gw_ref.shape, F32)
            for k in range(TAPS):
                gw_ref[k:k + 1, :] = jnp.sum(gacc_ref[k], axis=0, keepdims=True)

    return _hbm_call(
        body, name="conv_bwd", grid=(S // tm,),
        in_specs=[pl.BlockSpec(memory_space=pl.ANY), pl.BlockSpec(memory_space=pl.ANY), _rows(tm, 2 * CC),
                  _full((TAPS, CC))],
        out_specs=[_rows(tm, 2 * CC), _full((TAPS + 1, CC)), _full((8, 2 * CC))],
        out_shape=[_sds((S, 2 * CC), _MXU), _sds((TAPS + 1, CC), F32), _sds((8, 2 * CC), F32)],
        scratch_shapes=[pltpu.VMEM((S + 2 * CONV_HALO, CC), F32), pltpu.VMEM((S + 2 * CONV_HALO, CC), F32),
                        pltpu.VMEM((TAPS, 8, CC), F32), _shift_scratch(tm, 2), pltpu.VMEM((tm, CC), F32),
                        pltpu.SemaphoreType.DMA((2,))],
        compiler_params=_params(56),
    )(dy, vg, uglu, w_dw)


def _in_proj_bwd(duglu, dqr, dkr, dv, cosq, sinq, x, dx1, g_mix, winT, tm=512):
    S = x.shape[0]
    NA = ATT + 2 * KVC

    def body(dug_ref, dq_ref, dk_ref, dv_ref, c_ref, s_ref, x_ref, dx1_ref, g_ref, w_ref,
             dx_ref, du_ref, dg_ref, dbin_ref):
        i = pl.program_id(0)
        _acc_init(i, dg_ref, dbin_ref)
        c = c_ref[...]
        s = s_ref[...]
        datt = jnp.concatenate([_rope_bwd(dq_ref[...], c, s), _rope_bwd(dk_ref[...], c, s), dv_ref[...]], axis=1)
        _acc_row(dbin_ref, jnp.sum(datt, axis=0, keepdims=True))
        du = jnp.concatenate([dug_ref[...], datt.astype(du_ref.dtype)], axis=1)
        du_ref[...] = du
        dh = _nn(du, w_ref[...])
        xv = x_ref[...]
        g = g_ref[...]
        _, r = _rms(xv, g)
        dxn, dg = _rms_bwd(dh, xv, r, g)
        dx_ref[...] = dx1_ref[...] + dxn
        _acc_row(dg_ref, dg)

    return _hbm_call(
        body, name="in_proj_bwd", grid=(S // tm,),
        in_specs=[_rows(tm, 2 * CC), _rows(tm, ATT), _rows(tm, KVC), _rows(tm, KVC), _rows(tm, LANES),
                  _rows(tm, LANES), _rows(tm, D), _rows(tm, D), _full((1, D)), _once((INC, D))],
        out_specs=[_rows(tm, D), _rows(tm, INC), _full((8, D)), _full((8, NA))],
        out_shape=[_sds((S, D), F32), _sds((S, INC), _MXU), _sds((8, D), F32), _sds((8, NA), F32)],
        compiler_params=_params(48),
    )(duglu, dqr, dkr, dv, cosq, sinq, x, dx1, g_mix, winT)


def _rope_tables(S):
    pos = jnp.arange(S, dtype=F32)
    inv_freq = ROPE_THETA ** (-jnp.arange(0, HD, 2, dtype=F32) / HD)
    ang = pos[:, None] * inv_freq[None, :]
    cos, sin = jnp.cos(ang), jnp.sin(ang)
    cosq = jnp.tile(jnp.concatenate([cos, cos], axis=1), (1, LANES // HD))
    sinq = jnp.tile(jnp.concatenate([-sin, sin], axis=1), (1, LANES // HD))
    return cosq, sinq


GATHER_GROUPS = (("w_inT", "w_dw"), ("w_out", "w_mem_kvT", "w_mem_q", "w_mem_o"), ("w_gateT", "w_upT"), ("w_down",))
ID_GATHER, ID_SIBLING, ID_CHIP = 1, 5, 8


def _local_step(x, mem, tgt, sp, gather, reduce):
    S = x.shape[0]
    cosq, sinq = _rope_tables(S)

    gather.forward(0, cosq)
    w0 = gather.finish(0, sinq)
    winT, w_dw_full = w0["w_inT"], w0["w_dw"]
    h, uglu, vg, qr, kr, vv = _in_proj_fwd(x, sp["g_mix"], winT, sp["b_in"], cosq, sinq)
    ya = _attn_fwd(qr, kr, vv, sp["attn_sink"])
    gather.forward(1, ya)
    y, yc = _conv_fwd(vg, w_dw_full, sp["b_dw"], sp["g_conv_ln"], sp["b_conv_ln"])
    w1 = gather.finish(1, yc)
    wout, wkvT, wq, wo = w1["w_out"], w1["w_mem_kvT"], w1["w_mem_q"], w1["w_mem_o"]
    km, vm, memn = _mem_kv_fwd(mem, sp["g_mem_kv"], wkvT)
    x1, ymix = _out_proj_fwd(x, yc, ya, wout, sp["b_out"])
    token = gather.forward(2, x1)
    x2, qm, om = _mem_attn_fwd(x1, sp["g_mem_q"], wq, km, vm, wo, token)
    w2 = gather.finish(2, om)
    gather.forward(3, om)
    wgT, wuT, wd = w2["w_gateT"], w2["w_upT"], gather.finish(3, om)["w_down"]
    gt, up, hf, dx3, dx3b, loss_acc, dg_final = _ffn_loss_fwd(x2, tgt, sp["g_ffn"], sp["g_final"], wgT, wuT, wd)

    dx2, dx2b, dgt, dup, act, dg_ffn = _ffn_bwd(dx3, x2, gt, up, sp["g_ffn"], wd, wgT, wuT)
    token = reduce.start(0, _wgrad([(dgt, hf), (dup, hf), (act, dx3b)], "wgrad_ffn", dx2))

    dx1, dx1b, hq, dqm, dyc, dya, dkm, dvm, dg_mem_q, db_out = _mem_attn_bwd(
        dx2, x1, qm, km, vm, sp["g_mem_q"], wq, wo, wout, token)
    token = reduce.middle(0, dx1)
    g_wout, g_wq, g_wo = _wgrad([(ymix, dx1b), (hq, dqm), (om, dx2b)], "wgrad_mix_mem", token)
    g_wkvT, dg_mem_kv = _mem_kv_bwd(dkm, dvm, mem, sp["g_mem_kv"], memn, wkvT)
    token = reduce.start(1, [g_wout, g_wq, g_wkvT, g_wo])

    dqr, dkr, dvv, dsink = _attn_bwd(qr, kr, vv, sp["attn_sink"], dya, token)
    token = reduce.middle(1, dqr)
    dy, dg_cln, db_cln, db_dw = _conv_norm_bwd(dyc, y, sp["g_conv_ln"], sp["b_conv_ln"], token)
    duglu, g_wdw, dbin_glu = _conv_bwd(dy, vg, uglu, w_dw_full)
    grad_x, du, dg_mix, dbin_att = _in_proj_bwd(duglu, dqr, dkr, dvv, cosq, sinq, x, dx1, sp["g_mix"], winT)
    token = reduce.start(2, _wgrad([(du, h)], "wgrad_in", grad_x))

    small = {
        "loss": loss_acc[0:1, :],
        "g_mix": dg_mix[0:1], "b_in": jnp.concatenate([dbin_glu[0:1], dbin_att[0:1]], axis=1),
        "b_dw": db_dw[0:1], "g_conv_ln": dg_cln[0:1], "b_conv_ln": db_cln[0:1],
        "attn_sink": dsink[:, 0].reshape(1, NQ), "b_out": db_out[0:1], "g_mem_q": dg_mem_q[0:1],
        "g_mem_kv": dg_mem_kv[0:1], "g_ffn": dg_ffn[0:1], "g_final": dg_final[0:1],
    }
    token = reduce.middle(2, token, reduce.small_zones(small, g_wdw))
    reduce.finish(0, token)
    reduce.finish(1, token)
    return grad_x


def _place():
    return lax.axis_index("x"), lax.axis_index("y"), lax.axis_index("c")


def _hbm_specs(n):
    return [pl.BlockSpec(memory_space=pl.ANY)] * n


_HBM = pl.BlockSpec(memory_space=pltpu.HBM)
_SEM = pl.BlockSpec(memory_space=pltpu.SEMAPHORE)
_ANY = pl.BlockSpec(memory_space=pl.ANY)
_EFFECT = pltpu.SideEffectType.DATAFLOW_SIDE_EFFECTING
_VMEM = pl.BlockSpec(memory_space=pltpu.VMEM)
_TOKEN = jax.ShapeDtypeStruct((8, LANES), F32)


def _dev_index(p):
    return 4 * p[0] + 2 * p[1] + p[2]


def _landing_zones(shards, me_idx):
    def body(m_ref, *refs):
        n = len(refs) // 2
        for k in range(n):
            refs[n + k][...] = refs[k][...].astype(jnp.bfloat16)

    return pl.pallas_call(
        body, name="landing_zones",
        grid_spec=pltpu.PrefetchScalarGridSpec(
            num_scalar_prefetch=1, grid=(1,),
            in_specs=[pl.BlockSpec(s.shape, lambda i, m: (0, 0)) for s in shards],
            out_specs=[pl.BlockSpec((None,) + s.shape, lambda i, m: (m[0], 0, 0)) for s in shards]),
        out_shape=[pltpu.HBM((NDEV,) + s.shape, jnp.bfloat16) for s in shards],
        compiler_params=_params(40),
    )(me_idx, *shards)


def _gather_start(groups):
    sizes = [len(g) for g in groups]
    flat = [z for g in groups for z in g]
    n = len(flat)

    def body(*refs):
        ins = refs[:n]
        sems = refs[2 * n:]
        x, y, c = _place()
        me = _dev_index((x, y, c))
        peers = [(x, y, 1 - c), (1 - x, y, c), (x, 1 - y, c), (1 - x, 1 - y, c)]
        base = 0
        for gi, size in enumerate(sizes):
            send_sems, recv_sems = sems[2 * gi], sems[2 * gi + 1]
            for k in range(size):
                zone = ins[base + k]
                for j, peer in enumerate(peers):
                    pltpu.make_async_remote_copy(
                        src_ref=zone.at[me], dst_ref=zone.at[me], send_sem=send_sems.at[4 * k + j],
                        recv_sem=recv_sems.at[4 * k + j], device_id=peer, device_id_type=MESH).start()
            base += size

    out_shape = [pltpu.HBM(z.shape, z.dtype) for z in flat]
    out_specs = [_HBM] * n
    for size in sizes:
        out_shape += [pltpu.SemaphoreType.DMA((4 * size,))] * 2
        out_specs += [_SEM, _SEM]
    res = pl.pallas_call(
        body, name="gather_start", in_specs=[_HBM] * n, out_specs=out_specs, out_shape=out_shape,
        input_output_aliases={k: k for k in range(n)},
        compiler_params=pltpu.CompilerParams(has_side_effects=_EFFECT),
    )(*[pltpu.with_memory_space_constraint(z, pltpu.HBM) for z in flat])
    out, base = [], 0
    for gi, size in enumerate(sizes):
        out.append((list(res[base:base + size]), res[n + 2 * gi], res[n + 2 * gi + 1]))
        base += size
    return out


def _gather_forward(zones, send_sems, recv_sems, after, collective_id, name):
    n = len(zones)

    def body(*refs):
        ins = refs[:n]
        send_ref, recv_ref = refs[n], refs[n + 1]
        fsend, frecv = refs[2 * n + 3], refs[2 * n + 4]
        x, y, c = _place()
        sibling = (x, y, 1 - c)
        barrier = pltpu.get_barrier_semaphore()
        pl.semaphore_signal(barrier, inc=1, device_id=sibling, device_id_type=MESH)
        pl.semaphore_wait(barrier, 1)
        chips = [(1 - x, y), (x, 1 - y), (1 - x, 1 - y)]
        for j, chip in enumerate(chips):
            for k in range(n):
                slot = ins[k].at[_dev_index((*chip, c))]
                pltpu.make_async_remote_copy(
                    src_ref=slot, dst_ref=slot, send_sem=send_ref.at[4 * k + 1 + j],
                    recv_sem=recv_ref.at[4 * k + 1 + j], device_id=sibling, device_id_type=MESH).wait_recv()
                pltpu.make_async_remote_copy(
                    src_ref=slot, dst_ref=slot, send_sem=fsend.at[3 * k + j], recv_sem=frecv.at[3 * k + j],
                    device_id=sibling, device_id_type=MESH).start()
        for k in range(n):
            own = ins[k].at[_dev_index((x, y, c))]
            theirs = ins[k].at[_dev_index(sibling)]
            pltpu.make_async_remote_copy(
                src_ref=theirs, dst_ref=theirs, send_sem=send_ref.at[4 * k], recv_sem=recv_ref.at[4 * k],
                device_id=sibling, device_id_type=MESH).wait_recv()
            for j in range(4):
                pltpu.make_async_remote_copy(
                    src_ref=own, dst_ref=own, send_sem=send_ref.at[4 * k + j], recv_sem=recv_ref.at[4 * k + j],
                    device_id=sibling, device_id_type=MESH).wait_send()
        refs[2 * n + 5][...] = jnp.zeros(_TOKEN.shape, _TOKEN.dtype)

    res = pl.pallas_call(
        body, name=name, in_specs=[_HBM] * n + [_SEM, _SEM, _ANY],
        out_specs=[_HBM] * n + [_SEM, _SEM, _VMEM],
        out_shape=[pltpu.HBM(z.shape, z.dtype) for z in zones] + [pltpu.SemaphoreType.DMA((3 * n,))] * 2 + [_TOKEN],
        input_output_aliases={k: k for k in range(n)},
        compiler_params=pltpu.CompilerParams(has_side_effects=_EFFECT, collective_id=collective_id),
    )(*zones, send_sems, recv_sems, after)
    return (list(res[:n]), res[n], res[n + 1]), res[n + 2]


def _gather_finish(zones, fsend, frecv, after, name):
    n = len(zones)

    def body(*refs):
        ins = refs[:n]
        fsend_ref, frecv_ref = refs[n], refs[n + 1]
        x, y, c = _place()
        sibling = (x, y, 1 - c)
        chips = [(1 - x, y), (x, 1 - y), (1 - x, 1 - y)]
        for j, chip in enumerate(chips):
            for k in range(n):
                mine = ins[k].at[_dev_index((*chip, c))]
                theirs = ins[k].at[_dev_index((*chip, 1 - c))]
                pltpu.make_async_remote_copy(
                    src_ref=theirs, dst_ref=theirs, send_sem=fsend_ref.at[3 * k + j],
                    recv_sem=frecv_ref.at[3 * k + j], device_id=sibling, device_id_type=MESH).wait_recv()
                pltpu.make_async_remote_copy(
                    src_ref=mine, dst_ref=mine, send_sem=fsend_ref.at[3 * k + j],
                    recv_sem=frecv_ref.at[3 * k + j], device_id=sibling, device_id_type=MESH).wait_send()

    res = pl.pallas_call(
        body, name=name, in_specs=[_HBM] * n + [_SEM, _SEM, _ANY], out_specs=[_HBM] * n,
        out_shape=[pltpu.HBM(z.shape, z.dtype) for z in zones],
        input_output_aliases={k: k for k in range(n)},
        compiler_params=pltpu.CompilerParams(has_side_effects=_EFFECT),
    )(*zones, fsend, frecv, after)
    return list(res)


class _Gather:
    def __init__(self, started):
        self._phase = list(started)

    def forward(self, gi, after):
        zones, send_sems, recv_sems = self._phase[gi]
        self._phase[gi], token = _gather_forward(zones, send_sems, recv_sems, after, ID_GATHER + gi,
                                                  "gather_forward_%d" % gi)
        return token

    def finish(self, gi, after):
        zones, fsend, frecv = self._phase[gi]
        out = {}
        for name, z in zip(GATHER_GROUPS[gi], _gather_finish(zones, fsend, frecv, after, "gather_finish_%d" % gi)):
            if name == "w_dw":
                out[name] = jnp.transpose(z, (1, 0, 2)).reshape(TAPS + 1, CC)[:TAPS]
            else:
                out[name] = z.reshape(NDEV * z.shape[1], D)
        return out


def _handshake(peers):
    barrier = pltpu.get_barrier_semaphore()
    for peer in peers:
        pl.semaphore_signal(barrier, inc=1, device_id=peer, device_id_type=MESH)
    pl.semaphore_wait(barrier, len(peers))


def _sibling_copies(srcs, lands, send_sems, recv_sems):
    x, y, c = _place()
    return [pltpu.make_async_remote_copy(
        src_ref=srcs[k].at[2 * s + (1 - c)], dst_ref=lands[k].at[s], send_sem=send_sems.at[4 * k + s],
        recv_sem=recv_sems.at[4 * k + s], device_id=(x, y, 1 - c), device_id_type=MESH)
        for k in range(len(srcs)) for s in range(4)]


def _sibling_start(gbs, collective_id, name):
    n = len(gbs)
    lands = [lax.empty((4,) + g.shape[1:], g.dtype) for g in gbs]

    def body(*refs):
        x, y, c = _place()
        _handshake([(x, y, 1 - c)])
        for cp in _sibling_copies(refs[:n], refs[n:2 * n], refs[4 * n], refs[4 * n + 1]):
            cp.start()
        refs[4 * n + 2][...] = jnp.zeros(_TOKEN.shape, _TOKEN.dtype)

    res = pl.pallas_call(
        body, name=name, in_specs=[_HBM] * (2 * n), out_specs=[_HBM] * (2 * n) + [_SEM, _SEM, _VMEM],
        out_shape=[pltpu.HBM(a.shape, a.dtype) for a in gbs + lands] + [pltpu.SemaphoreType.DMA((4 * n,))] * 2
        + [_TOKEN],
        input_output_aliases={k: k for k in range(2 * n)},
        compiler_params=pltpu.CompilerParams(has_side_effects=_EFFECT, collective_id=collective_id),
    )(*[pltpu.with_memory_space_constraint(a, pltpu.HBM) for a in gbs + lands])
    return list(res[:n]), list(res[n:2 * n]), res[2 * n], res[2 * n + 1], res[2 * n + 2]


def _sibling_wait(gbs, lands, send_sems, recv_sems, after, name):
    n = len(gbs)

    def body(*refs):
        for cp in _sibling_copies(refs[:n], refs[n:2 * n], refs[2 * n], refs[2 * n + 1]):
            cp.wait_recv()
            cp.wait_send()

    res = pl.pallas_call(
        body, name=name, in_specs=[_HBM] * (2 * n) + [_SEM, _SEM, _ANY], out_specs=[_HBM] * (2 * n),
        out_shape=[pltpu.HBM(a.shape, a.dtype) for a in gbs + lands],
        input_output_aliases={k: k for k in range(2 * n)},
        compiler_params=pltpu.CompilerParams(has_side_effects=_EFFECT),
    )(*gbs, *lands, send_sems, recv_sems, after)
    return list(res[:n]), list(res[n:])


def _pair_sum(gfs, ts, c_idx, name):
    n = len(gfs)
    halves = 2

    def body(c_ref, *refs):
        for k in range(n):
            refs[2 * n + k][...] = (refs[k][...].astype(F32) + refs[n + k][...].astype(F32)).astype(jnp.bfloat16)

    in_specs, out_specs, out_shape = [], [], []
    for g in gfs:
        rh = g.shape[1] // halves
        in_specs.append(pl.BlockSpec((None, rh, D), lambda s, h, c_ref: (2 * s + c_ref[0], h, 0)))
    for g in gfs:
        rh = g.shape[1] // halves
        in_specs.append(pl.BlockSpec((None, rh, D), lambda s, h, c_ref: (s, h, 0)))
        out_specs.append(pl.BlockSpec((None, rh, D), lambda s, h, c_ref: (s, h, 0)))
        out_shape.append(_sds((4, g.shape[1], D), jnp.bfloat16))
    return pl.pallas_call(
        body, name=name,
        grid_spec=pltpu.PrefetchScalarGridSpec(num_scalar_prefetch=1, grid=(4, halves), in_specs=in_specs,
                                               out_specs=out_specs),
        out_shape=out_shape,
        compiler_params=pltpu.CompilerParams(dimension_semantics=("arbitrary", "arbitrary"),
                                             vmem_limit_bytes=48 * 2 ** 20),
    )(c_idx, *gfs, *ts)


def _chip_copies(pbs, lands, smalls, send_sems, recv_sems):
    x, y, c = _place()
    own = 2 * x + y
    me = _dev_index((x, y, c))
    chips = [(1 - x, y), (x, 1 - y), (1 - x, 1 - y)]
    n = len(pbs)
    copies = [pltpu.make_async_remote_copy(
        src_ref=pbs[k].at[2 * chip[0] + chip[1]], dst_ref=lands[k].at[own], send_sem=send_sems.at[3 * k + j],
        recv_sem=recv_sems.at[3 * k + j], device_id=(*chip, c), device_id_type=MESH)
        for k in range(n) for j, chip in enumerate(chips)]
    peers = [(x, y, 1 - c)] + [(*chip, cc) for chip in chips for cc in (c, 1 - c)]
    copies += [pltpu.make_async_remote_copy(
        src_ref=smalls[t].at[me], dst_ref=smalls[t].at[me], send_sem=send_sems.at[3 * n + 7 * t + j],
        recv_sem=recv_sems.at[3 * n + 7 * t + j], device_id=peer, device_id_type=MESH)
        for t in range(len(smalls)) for j, peer in enumerate(peers)]
    return copies


def _chip_start(pbs, smalls, collective_id, name):
    n, m = len(pbs), len(smalls)
    lands = [lax.empty(p.shape, p.dtype) for p in pbs]
    arrays = list(pbs) + lands + list(smalls)
    na = len(arrays)
    nsem = 3 * n + 7 * m

    def body(*refs):
        x, y, c = _place()
        chips = [(1 - x, y), (x, 1 - y), (1 - x, 1 - y)]
        peers = [(*chip, c) for chip in chips]
        if m:
            peers += [(x, y, 1 - c)] + [(*chip, 1 - c) for chip in chips]
        _handshake(peers)
        for cp in _chip_copies(refs[:n], refs[n:2 * n], refs[2 * n:na], refs[2 * na], refs[2 * na + 1]):
            cp.start()
        refs[2 * na + 2][...] = jnp.zeros(_TOKEN.shape, _TOKEN.dtype)

    res = pl.pallas_call(
        body, name=name, in_specs=[_HBM] * na, out_specs=[_HBM] * na + [_SEM, _SEM, _VMEM],
        out_shape=[pltpu.HBM(a.shape, a.dtype) for a in arrays] + [pltpu.SemaphoreType.DMA((nsem,))] * 2 + [_TOKEN],
        input_output_aliases={k: k for k in range(na)},
        compiler_params=pltpu.CompilerParams(has_side_effects=_EFFECT, collective_id=collective_id),
    )(*[pltpu.with_memory_space_constraint(a, pltpu.HBM) for a in arrays])
    return (list(res[:n]), list(res[n:2 * n]), list(res[2 * n:na]), res[na], res[na + 1]), res[na + 2]


def _chip_wait(pbs, lands, smalls, send_sems, recv_sems, after, name):
    n = len(pbs)
    arrays = list(pbs) + list(lands) + list(smalls)
    na = len(arrays)

    def body(*refs):
        for cp in _chip_copies(refs[:n], refs[n:2 * n], refs[2 * n:na], refs[na], refs[na + 1]):
            cp.wait_recv()
            cp.wait_send()

    res = pl.pallas_call(
        body, name=name, in_specs=[_HBM] * na + [_SEM, _SEM, _ANY], out_specs=[_HBM] * na,
        out_shape=[pltpu.HBM(a.shape, a.dtype) for a in arrays],
        input_output_aliases={k: k for k in range(na)},
        compiler_params=pltpu.CompilerParams(has_side_effects=_EFFECT),
    )(*arrays, send_sems, recv_sems, after)
    return list(res[:n]), list(res[n:2 * n]), list(res[2 * n:])


def _chip_sum(pbs, qs, own_idx, states, name):
    n = len(pbs)
    halves = 2
    n_in = 4 * n + 3 * sum(s is not None for s in states)

    def body(o_ref, *refs):
        nxt_in, nxt_out = 4 * n, n_in
        for k in range(n):
            acc = refs[4 * k][...].astype(F32)
            for j in range(1, 4):
                acc = acc + refs[4 * k + j][...].astype(F32)
            refs[nxt_out][...] = acc
            nxt_out += 1
            if states[k] is not None:
                w_ref, m_ref, v_ref = refs[nxt_in:nxt_in + 3]
                nxt_in += 3
                d, nm, nv = _adamw_math(w_ref[...], acc, m_ref[...], v_ref[...])
                refs[nxt_out][...] = d
                refs[nxt_out + 1][...] = nm
                refs[nxt_out + 2][...] = nv
                nxt_out += 3

    def other(j):
        return lambda h, o: ((o[0] + j) % 4, h, 0)

    def rows(r):
        return pl.BlockSpec((r // halves, D), lambda h, o: (h, 0))

    in_specs, ins = [], []
    for p, q in zip(pbs, qs):
        rh = p.shape[1] // halves
        in_specs.append(pl.BlockSpec((None, rh, D), lambda h, o: (o[0], h, 0)))
        ins.append(p)
        for j in range(1, 4):
            in_specs.append(pl.BlockSpec((None, rh, D), other(j)))
            ins.append(q)
    out_specs, out_shape, counts = [], [], []
    for p, s in zip(pbs, states):
        r = p.shape[1]
        if s is not None:
            in_specs += [rows(r)] * 3
            ins += list(s)
        counts.append(1 if s is None else 4)
        out_specs += [rows(r)] * counts[-1]
        out_shape += [_sds((r, D), F32)] * counts[-1]
    res = pl.pallas_call(
        body, name=name,
        grid_spec=pltpu.PrefetchScalarGridSpec(num_scalar_prefetch=1, grid=(halves,), in_specs=in_specs,
                                               out_specs=out_specs),
        out_shape=out_shape,
        compiler_params=pltpu.CompilerParams(dimension_semantics=("arbitrary",), vmem_limit_bytes=56 * 2 ** 20),
    )(own_idx, *ins)
    out, base = [], 0
    for c in counts:
        out.append(tuple(res[base:base + c]))
        base += c
    return out


REDUCE_GROUPS = (("w_gateT", "w_upT", "w_down"), ("w_out", "w_mem_q", "w_mem_kvT", "w_mem_o"), ("w_inT",))


class _Reduce:
    def __init__(self, states):
        x, y, c = _place()
        self._c = c.astype(jnp.int32).reshape(1)
        self._own = (2 * x + y).astype(jnp.int32).reshape(1)
        self._me = _dev_index((x, y, c))
        self._state = {}
        self._adam = states
        self.results = {}
        self.smalls = None

    def start(self, gi, wgrads):
        gbs = [g.reshape(NDEV, g.shape[0] // NDEV, D) for g in wgrads]
        gbs, lands, send_sems, recv_sems, token = _sibling_start(gbs, ID_SIBLING + gi, "reduce_sibling_start_%d" % gi)
        self._state[gi] = (gbs, lands, send_sems, recv_sems)
        return token

    def middle(self, gi, after, smalls=()):
        gbs, lands, send_sems, recv_sems = self._state[gi]
        gbs, ts = _sibling_wait(gbs, lands, send_sems, recv_sems, after, "reduce_sibling_wait_%d" % gi)
        pbs = _pair_sum(gbs, ts, self._c, "reduce_pair_sum_%d" % gi)
        self._state[gi], token = _chip_start(pbs, list(smalls), ID_CHIP + gi, "reduce_chip_start_%d" % gi)
        return token

    def finish(self, gi, after):
        pbs, lands, smalls, send_sems, recv_sems = self._state[gi]
        pbs, qs, smalls = _chip_wait(pbs, lands, smalls, send_sems, recv_sems, after, "reduce_chip_wait_%d" % gi)
        names = REDUCE_GROUPS[gi]
        states = [self._adam.get(n) for n in names]
        self.results.update(zip(names, _chip_sum(pbs, qs, self._own, states, "reduce_update_%d" % gi)))
        if smalls:
            self.smalls = smalls

    def small_zones(self, small, g_wdw):
        pack = jnp.concatenate(
            [jnp.pad(small[name], ((0, 0), (0, lanes - small[name].shape[1]))) for name, lanes, _ in PACK], axis=1)
        zones = []
        for a in (pack, g_wdw):
            zone = lax.empty((NDEV,) + a.shape, a.dtype)
            zones.append(lax.dynamic_update_slice(zone, a[None], (self._me,) + (0,) * a.ndim))
        return zones


def _adamw_math(w, g, m, v):
    m = ADAM_B1 * m + (1.0 - ADAM_B1) * g
    v = ADAM_B2 * v + (1.0 - ADAM_B2) * (g * g)
    m_hat = m / (1.0 - ADAM_B1 ** ADAM_STEP)
    v_hat = v / (1.0 - ADAM_B2 ** ADAM_STEP)
    delta = -ADAM_LR * (m_hat / (jnp.sqrt(v_hat) + ADAM_EPS) + ADAM_WD * w)
    return delta, m, v


def _adamw(w, g, m, v, name):
    R, C = w.shape
    rc = 256 if R % 256 == 0 and R > 256 else R

    def body(w_ref, g_ref, m_ref, v_ref, d_ref, nm_ref, nv_ref):
        d, nm, nv = _adamw_math(w_ref[...], g_ref[...], m_ref[...], v_ref[...])
        d_ref[...] = d
        nm_ref[...] = nm
        nv_ref[...] = nv

    return _hbm_call(
        body, name=name, grid=(R // rc,),
        in_specs=[_rows(rc, C)] * 4, out_specs=[_rows(rc, C)] * 3,
        out_shape=[_sds((R, C), F32)] * 3,
        compiler_params=_params(40),
    )(w, g, m, v)


PACK = (("loss", LANES, 1), ("g_mix", D, D), ("b_in", INC, INC), ("b_dw", CC, CC), ("g_conv_ln", CC, CC),
        ("b_conv_ln", CC, CC), ("attn_sink", LANES, NQ), ("b_out", D, D), ("g_mem_q", D, D), ("g_mem_kv", D, D),
        ("g_ffn", D, D), ("g_final", D, D))
PACK_LANES = sum(p[1] for p in PACK)


def _small_update(packs, gdws, ws, ms, vs):
    names = [p[0] for p in PACK[1:]]
    n = len(names)

    def body(*refs):
        packs_ref, gdws_ref = refs[0], refs[1]
        w_refs, m_refs, v_refs = refs[2:2 + n], refs[2 + n:2 + 2 * n], refs[2 + 2 * n:2 + 3 * n]
        outs = refs[2 + 3 * n:]
        tot = packs_ref[0]
        gdw = gdws_ref[0]
        for s in range(1, NDEV):
            tot = tot + packs_ref[s]
            gdw = gdw + gdws_ref[s]
        outs[0][...] = tot[:, 0:LANES]
        outs[1][...] = gdw
        off = LANES
        for k, (_, lanes, used) in enumerate(PACK[1:]):
            g = tot[:, off:off + used]
            off += lanes
            d, nm, nv = _adamw_math(w_refs[k][...], g, m_refs[k][...], v_refs[k][...])
            o = outs[2 + 4 * k:6 + 4 * k]
            o[0][...] = g
            o[1][...] = d
            o[2][...] = nm
            o[3][...] = nv

    def full(a):
        nd = len(a.shape)
        return pl.BlockSpec(a.shape, lambda i: (0,) * nd)

    ins = [packs, gdws, *ws, *ms, *vs]
    out_shape = [_sds((1, LANES), F32), _sds(gdws.shape[1:], F32)]
    for w in ws:
        out_shape += [_sds(w.shape, F32)] * 4
    return pl.pallas_call(
        body, name="small_update", grid=(1,),
        in_specs=[full(a) for a in ins], out_specs=[full(o) for o in out_shape], out_shape=out_shape,
        compiler_params=_params(40),
    )(*ins)


def kernel(x, mem, g_mix, w_in, b_in, w_dw, b_dw, g_conv_ln, b_conv_ln, attn_sink, w_out, b_out, g_mem_q, g_mem_kv, w_mem_q, w_mem_kv, w_mem_o, g_ffn, w_gate, w_up, w_down, g_final, loss_target, m_g_mix, m_w_in, m_b_in, m_w_dw, m_b_dw, m_g_conv_ln, m_b_conv_ln, m_attn_sink, m_w_out, m_b_out, m_g_mem_q, m_g_mem_kv, m_w_mem_q, m_w_mem_kv, m_w_mem_o, m_g_ffn, m_w_gate, m_w_up, m_w_down, m_g_final, v_g_mix, v_w_in, v_b_in, v_w_dw, v_b_dw, v_g_conv_ln, v_b_conv_ln, v_attn_sink, v_w_out, v_b_out, v_g_mem_q, v_g_mem_kv, v_w_mem_q, v_w_mem_kv, v_w_mem_o, v_g_ffn, v_w_gate, v_w_up, v_w_down, v_g_final):
    given = dict(locals())
    bf = jnp.bfloat16
    xi, yi, ci = _place()
    me = 4 * xi + 2 * yi + ci

    transposed = {"w_inT": w_in, "w_mem_kvT": w_mem_kv, "w_gateT": w_gate, "w_upT": w_up}
    plain = {"w_out": w_out, "w_mem_q": w_mem_q, "w_mem_o": w_mem_o, "w_down": w_down}
    shards = [transposed[n][0].T if n in transposed else plain[n][0] for n in W_NAMES]
    zones = dict(zip(W_NAMES, _landing_zones(shards, me.astype(jnp.int32).reshape(1))))
    dw = jnp.pad(w_dw[0], ((0, 1), (0, 0)))
    zones["w_dw"] = lax.dynamic_update_slice(lax.empty((NDEV,) + dw.shape, dw.dtype), dw[None], (me, 0, 0))
    gather = _Gather(_gather_start([[zones[n] for n in names] for names in GATHER_GROUPS]))

    sp = {n: given[n] for n in ("g_mix", "b_in", "b_dw", "g_conv_ln", "b_conv_ln", "attn_sink", "b_out", "g_mem_q",
                                "g_mem_kv", "g_ffn")}
    sp["g_final"] = g_final.reshape(1, D)
    frames = {"w_inT": ("w_in", True), "w_gateT": ("w_gate", True), "w_upT": ("w_up", True),
              "w_out": ("w_out", False), "w_mem_q": ("w_mem_q", False), "w_mem_o": ("w_mem_o", False),
              "w_down": ("w_down", False)}
    states = {}
    for frame, (n, transposed) in frames.items():
        wmv = (given[n][0], given["m_" + n][0], given["v_" + n][0])
        states[frame] = tuple(a.T for a in wmv) if transposed else wmv
    reduce = _Reduce(states)
    grad_x = _local_step(x[0], mem[0], loss_target[0], sp, gather, reduce)
    reduce.finish(2, reduce.results["w_out"][1])

    grads, deltas, new_m, new_v = {}, {}, {}, {}
    for frame, (n, transposed) in frames.items():
        outs = [a.T if transposed else a for a in reduce.results[frame]]
        grads[n], deltas[n], new_m[n], new_v[n] = [a[None] for a in outs]

    def update(n, g):
        d, nm, nv = _adamw(given[n][0], g, given["m_" + n][0], given["v_" + n][0], "adamw_" + n)
        grads[n], deltas[n], new_m[n], new_v[n] = g[None], d[None], nm[None], nv[None]

    update("w_mem_kv", reduce.results["w_mem_kvT"][0].T)

    packs, gdws = reduce.smalls
    names_small = [p[0] for p in PACK[1:]]

    def as_row(a):
        return a.reshape(1, -1)

    res = _small_update(packs, gdws, [as_row(given[n]) for n in names_small],
                        [as_row(given["m_" + n]) for n in names_small],
                        [as_row(given["v_" + n]) for n in names_small])
    loss = res[0][0, 0]
    for k, n in enumerate(names_small):
        shape = given[n].shape
        g, d, nm, nv = res[2 + 4 * k:6 + 4 * k]
        grads[n], deltas[n], new_m[n], new_v[n] = (g.reshape(shape), d.reshape(shape), nm.reshape(shape),
                                                   nv.reshape(shape))
    update("w_dw", lax.dynamic_slice(res[1], (0, me * (CC // NDEV)), (TAPS, CC // NDEV)))

    order = ("g_mix", "w_in", "b_in", "w_dw", "b_dw", "g_conv_ln", "b_conv_ln", "attn_sink", "w_out", "b_out",
             "g_mem_q", "g_mem_kv", "w_mem_q", "w_mem_kv", "w_mem_o", "g_ffn", "w_gate", "w_up", "w_down", "g_final")
    return (loss, grad_x[None], *[grads[n] for n in order], *[deltas[n] for n in order],
            *[new_m[n] for n in order], *[new_v[n] for n in order])
```

```python
import functools

import jax
import jax.numpy as jnp
from jax import lax
from jax.experimental import pallas as pl
from jax.experimental.pallas import tpu as pltpu

D = 1024
CC = 512
HD = 64
NQ = 8
NKV = 2
GRP = NQ // NKV
ATT = NQ * HD
KVC = NKV * HD
BLK = 128
MH = 4
MHD = D // MH
F = 2816
FCH = 256
FCH_BWD = F
INC = 2 * CC + ATT + 2 * KVC
TAPS = 31
EPS = 1e-6
ROPE_THETA = 10000.0
SCALE = HD ** -0.5
MSCALE = MHD ** -0.5
LANES = 128
NDEV = 8

ADAM_LR = 0.001
ADAM_B1 = 0.9
ADAM_B2 = 0.999
ADAM_EPS = 1e-08
ADAM_WD = 0.01
ADAM_STEP = 10

F32 = jnp.float32
_MXU = jnp.bfloat16
MESH = pl.DeviceIdType.MESH

W_NAMES = ("w_inT", "w_out", "w_mem_q", "w_mem_kvT", "w_mem_o", "w_gateT", "w_upT", "w_down")
W_ROWS = (INC // NDEV, D // NDEV, D // NDEV, 2 * D // NDEV, D // NDEV, F // NDEV, F // NDEV, F // NDEV)


def _mm(a, b, ca, cb):
    return lax.dot_general(a.astype(_MXU), b.astype(_MXU), (((ca,), (cb,)), ((), ())),
                           preferred_element_type=F32)


def _nn(a, b):
    return _mm(a, b, 1, 0)


def _nt(a, b):
    return _mm(a, b, 1, 1)


def _tn(a, b):
    return _mm(a, b, 0, 0)


def _sigmoid(x):
    return 0.5 * jnp.tanh(0.5 * x) + 0.5


def _rms(x, g):
    r = lax.rsqrt(jnp.mean(x * x, axis=-1, keepdims=True) + EPS)
    return x * r * g, r


def _rms_bwd(dy, x, r, g):
    xh = x * r
    dg = jnp.sum(dy * xh, axis=0, keepdims=True)
    dxh = dy * g
    dx = r * (dxh - xh * jnp.mean(dxh * xh, axis=-1, keepdims=True))
    return dx, dg


def _rope(t, c, s):
    n = t.shape[1]
    reps = n // LANES
    if reps > 1:
        c = jnp.tile(c, (1, reps))
        s = jnp.tile(s, (1, reps))
    lane = lax.broadcasted_iota(jnp.int32, t.shape, 1)
    first = (lane & (HD - 1)) < (HD // 2)
    partner = jnp.where(first, pltpu.roll(t, n - HD // 2, 1), pltpu.roll(t, HD // 2, 1))
    return t * c + partner * s


def _rope_bwd(dt, c, s):
    n = dt.shape[1]
    reps = n // LANES
    if reps > 1:
        c = jnp.tile(c, (1, reps))
        s = jnp.tile(s, (1, reps))
    lane = lax.broadcasted_iota(jnp.int32, dt.shape, 1)
    first = (lane & (HD - 1)) < (HD // 2)
    ds = dt * s
    partner = jnp.where(first, pltpu.roll(ds, n - HD // 2, 1), pltpu.roll(ds, HD // 2, 1))
    return dt * c + partner


def _acc_init(i, *refs):
    @pl.when(i == 0)
    def _():
        for r in refs:
            r[...] = jnp.zeros(r.shape, r.dtype)


def _acc_row(ref, v):
    ref[...] += jnp.broadcast_to(v, ref.shape)


def _rows(tm, n):
    return pl.BlockSpec((tm, n), lambda i: (i, 0))


def _full(shape):
    nd = len(shape)
    return pl.BlockSpec(shape, lambda i: (0,) * nd)


def _once(shape):
    nd = len(shape)
    return pl.BlockSpec(shape, lambda i: (0,) * nd, pipeline_mode=pl.Buffered(1))


def _params(vmem_mb):
    return pltpu.CompilerParams(dimension_semantics=("arbitrary",), vmem_limit_bytes=vmem_mb * 2 ** 20)


def _sds(shape, dtype):
    return pltpu.HBM(shape, dtype)


def _hbm_call(*args, **kwargs):
    call = pl.pallas_call(*args, **kwargs)
    return lambda *ops: call(*[pltpu.with_memory_space_constraint(o, pltpu.HBM) for o in ops])


def _mem_kv_fwd(mem, g_kv, wkvT):
    M = mem.shape[0]

    def body(mem_ref, g_ref, w_ref, km_ref, vm_ref, mn_ref):
        y, _ = _rms(mem_ref[...], g_ref[...])
        kv = _nt(y, w_ref[...])
        km_ref[...] = kv[:, :D].astype(km_ref.dtype)
        vm_ref[...] = kv[:, D:].astype(vm_ref.dtype)
        mn_ref[...] = y.astype(mn_ref.dtype)

    return _hbm_call(
        body, name="mem_kv_fwd", grid=(1,),
        in_specs=[_full((M, D)), _full((1, D)), _full((2 * D, D))],
        out_specs=[_full((M, D))] * 3,
        out_shape=[_sds((M, D), _MXU)] * 3,
        compiler_params=_params(40),
    )(mem, g_kv, wkvT)


def _in_proj_fwd(x, g_mix, winT, b_in, cosq, sinq, tm=512):
    S = x.shape[0]

    def body(x_ref, g_ref, w_ref, b_ref, c_ref, s_ref, h_ref, uglu_ref, vg_ref, q_ref, k_ref, v_ref):
        h, _ = _rms(x_ref[...], g_ref[...])
        hb = h.astype(h_ref.dtype)
        h_ref[...] = hb
        u = _nt(hb, w_ref[...]) + b_ref[...]
        uglu_ref[...] = u[:, :2 * CC].astype(uglu_ref.dtype)
        vg_ref[...] = u[:, :CC] * _sigmoid(u[:, CC:2 * CC])
        c = c_ref[...]
        s = s_ref[...]
        q_ref[...] = (_rope(u[:, 2 * CC:2 * CC + ATT], c, s) * SCALE).astype(q_ref.dtype)
        k_ref[...] = _rope(u[:, 2 * CC + ATT:2 * CC + ATT + KVC], c, s).astype(k_ref.dtype)
        v_ref[...] = u[:, 2 * CC + ATT + KVC:].astype(v_ref.dtype)

    return _hbm_call(
        body, name="in_proj_fwd", grid=(S // tm,),
        in_specs=[_rows(tm, D), _full((1, D)), _full((INC, D)), _full((1, INC)), _rows(tm, LANES), _rows(tm, LANES)],
        out_specs=[_rows(tm, D), _rows(tm, 2 * CC), _rows(tm, CC), _rows(tm, ATT), _rows(tm, KVC), _rows(tm, KVC)],
        out_shape=[_sds((S, D), _MXU), _sds((S, 2 * CC), _MXU), _sds((S, CC), F32), _sds((S, ATT), _MXU),
                   _sds((S, KVC), _MXU), _sds((S, KVC), _MXU)],
        compiler_params=_params(48),
    )(x, g_mix, winT, b_in, cosq, sinq)


CONV_HALO = 16


CONV_TILE = 512


def _shift_scratch(tm, sets):
    return pltpu.VMEM((sets, CC // LANES, 8, tm + 24, LANES), F32)


def _shifted_copies(pad_ref, sh_ref, t0, tm):
    for cl in range(CC // LANES):
        blk = pad_ref[pl.ds(t0, tm + 2 * CONV_HALO), cl * LANES:(cl + 1) * LANES]
        for b in range(8):
            sh_ref[cl, b, :, :] = blk[b:b + tm + 24, :]


def _conv_taps():
    return [(8 * a + b - 1, b, 8 * a) for b in range(8) for a in range(4) if 0 <= 8 * a + b - 1 < TAPS]


CONV_ROWS = 64
SUBLANES = 8


def _dwconv_tile(sh_ref, tm, w_ref, flip, out_ref):
    for cl in range(CC // LANES):
        lanes = slice(cl * LANES, (cl + 1) * LANES)

        def chunk(c, carry):
            r = pl.multiple_of(c * CONV_ROWS, CONV_ROWS)
            accs = [None] * 4
            for n, (k, b, off) in enumerate(_conv_taps()):
                kk = TAPS - 1 - k if flip else k
                term = sh_ref[cl, b, pl.ds(off + r, CONV_ROWS), :] * w_ref[kk:kk + 1, lanes]
                accs[n % 4] = term if accs[n % 4] is None else accs[n % 4] + term
            out_ref[pl.ds(r, CONV_ROWS), lanes] = (accs[0] + accs[1]) + (accs[2] + accs[3])
            return carry

        lax.fori_loop(0, tm // CONV_ROWS, chunk, 0)


def _fill_padded(i, src_hbm, pad_ref, sem, S):
    @pl.when(i == 0)
    def _():
        zeros = jnp.zeros((CONV_HALO, pad_ref.shape[1]), pad_ref.dtype)
        pad_ref[0:CONV_HALO, :] = zeros
        pad_ref[CONV_HALO + S:2 * CONV_HALO + S, :] = zeros
        cp = pltpu.make_async_copy(src_hbm, pad_ref.at[pl.ds(CONV_HALO, S), :], sem)
        cp.start()
        cp.wait()


def _ln_silu(y, g, b):
    mu = jnp.mean(y, axis=-1, keepdims=True)
    yc = y - mu
    var = jnp.mean(yc * yc, axis=-1, keepdims=True)
    rstd = lax.rsqrt(var + EPS)
    yh = yc * rstd
    z = yh * g + b
    return yh, rstd, z


def _conv_fwd(vg, w_dw, b_dw, ln_g, ln_b):
    S = vg.shape[0]
    tm = min(CONV_TILE, S)

    def body(vg_hbm, w_ref, b_ref, g_ref, bb_ref, y_ref, yc_ref, pad_ref, sh_ref, sem):
        i = pl.program_id(0)
        _fill_padded(i, vg_hbm, pad_ref, sem, S)
        _shifted_copies(pad_ref, sh_ref.at[0], pl.multiple_of(i * tm, tm), tm)
        _dwconv_tile(sh_ref.at[0], tm, w_ref, False, y_ref)
        y = y_ref[...] + b_ref[...]
        y_ref[...] = y
        _, _, z = _ln_silu(y, g_ref[...], bb_ref[...])
        yc_ref[...] = (z * _sigmoid(z)).astype(yc_ref.dtype)

    return _hbm_call(
        body, name="conv_fwd", grid=(S // tm,),
        in_specs=[pl.BlockSpec(memory_space=pl.ANY), _full((TAPS, CC)), _full((1, CC)), _full((1, CC)), _full((1, CC))],
        out_specs=[_rows(tm, CC), _rows(tm, CC)],
        out_shape=[_sds((S, CC), F32), _sds((S, CC), _MXU)],
        scratch_shapes=[pltpu.VMEM((S + 2 * CONV_HALO, CC), F32), _shift_scratch(tm, 1), pltpu.SemaphoreType.DMA],
        compiler_params=_params(40),
    )(vg, w_dw, b_dw, ln_g, ln_b)


def _attn_window(i, S):
    start = jnp.clip((i - 1) * BLK, 0, S - 3 * BLK)
    return pl.multiple_of(start, BLK)


ATTN_COLS = GRP * BLK


def _attn_mask_base():
    keys = lax.broadcasted_iota(jnp.int32, (3 * BLK, ATTN_COLS), 0)
    cols = lax.broadcasted_iota(jnp.int32, (3 * BLK, ATTN_COLS), 1)
    return keys - (cols & (BLK - 1))


def _attn_mask(base, i, start):
    return jnp.abs(base + (start - i * BLK)) <= BLK


def _stack_heads(t, hk):
    return jnp.concatenate([t[:, (hk * GRP + g) * HD:(hk * GRP + g + 1) * HD] for g in range(GRP)], axis=0)


def _unstack_heads_t(t):
    cols = []
    for p in range(GRP // 2):
        pair = jnp.concatenate([t[:, (2 * p) * BLK:(2 * p + 1) * BLK], t[:, (2 * p + 1) * BLK:(2 * p + 2) * BLK]],
                               axis=0)
        cols.append(pair.T)
    return jnp.concatenate(cols, axis=1)


def _attn_probs(qs, kh, sk_ref, hk, mask):
    s = jnp.where(mask, _nt(kh, qs), -1e30)
    sk = jnp.concatenate(
        [jnp.broadcast_to(sk_ref[0:1, hk * GRP + g:hk * GRP + g + 1], (1, BLK)) for g in range(GRP)], axis=1)
    m = jnp.maximum(jnp.max(s, axis=0, keepdims=True), sk)
    e = jnp.exp(s - m)
    es = jnp.exp(sk - m)
    inv = 1.0 / (jnp.sum(e, axis=0, keepdims=True) + es)
    return e, inv, es


ATTN_BLOCKS = 8


def _attn_fwd(qr, kr, vv, sink):
    S = qr.shape[0]
    tq = ATTN_BLOCKS * BLK

    def body(q_ref, k_ref, v_ref, sk_ref, o_ref):
        base = _attn_mask_base()
        for b in range(ATTN_BLOCKS):
            i = pl.program_id(0) * ATTN_BLOCKS + b
            rows = slice(b * BLK, (b + 1) * BLK)
            start = _attn_window(i, S)
            kb = k_ref[pl.ds(start, 3 * BLK), :]
            vb = v_ref[pl.ds(start, 3 * BLK), :]
            q = q_ref[rows, :]
            mask = _attn_mask(base, i, start)
            outs = []
            for hk in range(NKV):
                hs = slice(hk * HD, (hk + 1) * HD)
                e, inv, _ = _attn_probs(_stack_heads(q, hk), kb[:, hs], sk_ref, hk, mask)
                outs.append(_unstack_heads_t(_tn(vb[:, hs], e) * inv))
            o_ref[rows, :] = jnp.concatenate(outs, axis=1).astype(o_ref.dtype)

    return _hbm_call(
        body, name="attn_fwd", grid=(S // tq,),
        in_specs=[_rows(tq, ATT), _full((S, KVC)), _full((S, KVC)), _full((1, NQ))],
        out_specs=_rows(tq, ATT),
        out_shape=_sds((S, ATT), _MXU),
        compiler_params=_params(40),
    )(qr, kr, vv, sink)


def _out_proj_fwd(x, yc, ya, wout, b_out, tm=512):
    S = x.shape[0]

    def body(x_ref, yc_ref, ya_ref, w_ref, b_ref, x1_ref, ymix_ref):
        ymix = jnp.concatenate([yc_ref[...], ya_ref[...]], axis=1)
        ymix_ref[...] = ymix
        x1_ref[...] = x_ref[...] + _nn(ymix, w_ref[...]) + b_ref[...]

    return _hbm_call(
        body, name="out_proj_fwd", grid=(S // tm,),
        in_specs=[_rows(tm, D), _rows(tm, CC), _rows(tm, ATT), _full((D, D)), _full((1, D))],
        out_specs=[_rows(tm, D), _rows(tm, D)],
        out_shape=[_sds((S, D), F32), _sds((S, D), _MXU)],
        compiler_params=_params(40),
    )(x, yc, ya, wout, b_out)


def _softmax_t(s):
    m = jnp.max(s, axis=0, keepdims=True)
    e = jnp.exp(s - m)
    return e, 1.0 / jnp.sum(e, axis=0, keepdims=True)


def _softmax(s):
    m = jnp.max(s, axis=-1, keepdims=True)
    e = jnp.exp(s - m)
    return e * (1.0 / jnp.sum(e, axis=-1, keepdims=True))


def _mem_attn_fwd(x1, g_q, wq, km, vm, wo, after, tm=512):
    S = x1.shape[0]
    M = km.shape[0]

    def body(x1_ref, g_ref, wq_ref, km_ref, vm_ref, wo_ref, after_ref, x2_ref, qm_ref, om_ref):
        x1v = x1_ref[...]
        hq, _ = _rms(x1v, g_ref[...])
        qm = (_nn(hq, wq_ref[...]) * MSCALE).astype(qm_ref.dtype)
        qm_ref[...] = qm
        oms = []
        for h in range(MH):
            sl = slice(h * MHD, (h + 1) * MHD)
            p = _softmax(_nt(qm[:, sl], km_ref[:, sl]))
            oms.append(_nn(p, vm_ref[:, sl]))
        om = jnp.concatenate(oms, axis=1).astype(om_ref.dtype)
        om_ref[...] = om
        x2_ref[...] = x1v + _nn(om, wo_ref[...])

    return _hbm_call(
        body, name="mem_attn_fwd", grid=(S // tm,),
        in_specs=[_rows(tm, D), _full((1, D)), _full((D, D)), _full((M, D)), _full((M, D)), _full((D, D)), _AFTER],
        out_specs=[_rows(tm, D)] * 3,
        out_shape=[_sds((S, D), F32), _sds((S, D), _MXU), _sds((S, D), _MXU)],
        compiler_params=_params(48),
    )(x1, g_q, wq, km, vm, wo, after)


def _ffn_loss_fwd(x2, tgt, g_ffn, g_final, wgT, wuT, wd, tm=512):
    S = x2.shape[0]

    def body(x2_ref, t_ref, gf_ref, gl_ref, wg_ref, wu_ref, wd_ref,
             gt_ref, up_ref, hf_ref, dx3_ref, dx3b_ref, loss_ref, dgl_ref):
        i = pl.program_id(0)
        _acc_init(i, loss_ref, dgl_ref)
        x2v = x2_ref[...]
        hf, _ = _rms(x2v, gf_ref[...])
        hb = hf.astype(hf_ref.dtype)
        hf_ref[...] = hb
        x3 = x2v
        for c in range(F // FCH):
            sl = slice(c * FCH, (c + 1) * FCH)
            gt = _nt(hb, wg_ref[sl, :])
            up = _nt(hb, wu_ref[sl, :])
            gt_ref[:, sl] = gt.astype(gt_ref.dtype)
            up_ref[:, sl] = up.astype(up_ref.dtype)
            x3 = x3 + _nn(gt * _sigmoid(gt) * up, wd_ref[sl, :])
        gl = gl_ref[...]
        y, r = _rms(x3, gl)
        diff = y - t_ref[...]
        _acc_row(loss_ref, jnp.sum(jnp.sum(diff * diff, axis=-1, keepdims=True), axis=0, keepdims=True) * (0.5 / D))
        dx3, dgl = _rms_bwd(diff * (1.0 / D), x3, r, gl)
        dx3_ref[...] = dx3
        dx3b_ref[...] = dx3.astype(dx3b_ref.dtype)
        _acc_row(dgl_ref, dgl)

    return _hbm_call(
        body, name="ffn_loss_fwd", grid=(S // tm,),
        in_specs=[_rows(tm, D), _rows(tm, D), _full((1, D)), _full((1, D)),
                  _once((F, D)), _once((F, D)), _once((F, D))],
        out_specs=[_rows(tm, F), _rows(tm, F), _rows(tm, D), _rows(tm, D), _rows(tm, D),
                   _full((8, LANES)), _full((8, D))],
        out_shape=[_sds((S, F), _MXU), _sds((S, F), _MXU), _sds((S, D), _MXU), _sds((S, D), F32),
                   _sds((S, D), _MXU), _sds((8, LANES), F32), _sds((8, D), F32)],
        compiler_params=_params(56),
    )(x2, tgt, g_ffn, g_final, wgT, wuT, wd)


def _ffn_bwd(dx3, x2, gt, up, g_ffn, wd, wgT, wuT, tm=256):
    S = x2.shape[0]

    def body(dx3_ref, x2_ref, gt_ref, up_ref, g_ref, wd_ref, wg_ref, wu_ref,
             dx2_ref, dx2b_ref, dgt_ref, dup_ref, act_ref, dg_ref):
        i = pl.program_id(0)
        _acc_init(i, dg_ref)
        dx3v = dx3_ref[...]
        db = dx3v.astype(_MXU)
        dhf = jnp.zeros((tm, D), F32)
        for c in range(F // FCH_BWD):
            sl = slice(c * FCH_BWD, (c + 1) * FCH_BWD)
            gtv = gt_ref[:, sl].astype(F32)
            upv = up_ref[:, sl].astype(F32)
            dact = _nt(db, wd_ref[sl, :])
            sig = _sigmoid(gtv)
            silu = gtv * sig
            dup = (dact * silu).astype(dup_ref.dtype)
            dgt = (dact * upv * (sig * (1.0 + gtv * (1.0 - sig)))).astype(dgt_ref.dtype)
            act_ref[:, sl] = (silu * upv).astype(act_ref.dtype)
            dgt_ref[:, sl] = dgt
            dup_ref[:, sl] = dup
            dhf = dhf + _nn(dgt, wg_ref[sl, :]) + _nn(dup, wu_ref[sl, :])
        x2v = x2_ref[...]
        g = g_ref[...]
        _, r = _rms(x2v, g)
        dxn, dg = _rms_bwd(dhf, x2v, r, g)
        dx2 = dx3v + dxn
        dx2_ref[...] = dx2
        dx2b_ref[...] = dx2.astype(dx2b_ref.dtype)
        _acc_row(dg_ref, dg)

    return _hbm_call(
        body, name="ffn_bwd", grid=(S // tm,),
        in_specs=[_rows(tm, D), _rows(tm, D), _rows(tm, F), _rows(tm, F), _full((1, D)),
                  _once((F, D)), _once((F, D)), _once((F, D))],
        out_specs=[_rows(tm, D), _rows(tm, D), _rows(tm, F), _rows(tm, F), _rows(tm, F), _full((8, D))],
        out_shape=[_sds((S, D), F32), _sds((S, D), _MXU), _sds((S, F), _MXU), _sds((S, F), _MXU),
                   _sds((S, F), _MXU), _sds((8, D), F32)],
        compiler_params=_params(56),
    )(dx3, x2, gt, up, g_ffn, wd, wgT, wuT)


_AFTER = pl.BlockSpec(memory_space=pl.ANY)


def _wgrad(pairs, name, after, rc=256):
    S, R = pairs[0][0].shape
    N = pairs[0][1].shape[1]
    rights = []
    for _, b in pairs:
        if not any(b is r for r in rights):
            rights.append(b)
    which = [next(i for i, r in enumerate(rights) if r is b) for _, b in pairs]
    n, nb = len(pairs), len(rights)

    def body(*refs):
        for k in range(n):
            refs[n + nb + 1 + k][...] = _tn(refs[k][...], refs[n + which[k]][...]).astype(jnp.bfloat16)

    res = _hbm_call(
        body, name=name, grid=(R // rc,),
        in_specs=[pl.BlockSpec((S, rc), lambda j: (0, j))] * n + [_once((S, N))] * nb + [_AFTER],
        out_specs=[_rows(rc, N)] * n,
        out_shape=[_sds((R, N), jnp.bfloat16)] * n,
        compiler_params=_params(56),
    )(*[a for a, _ in pairs], *rights, after)
    return list(res)


def _mem_attn_bwd(dx2, x1, qm, km, vm, g_q, wq, wo, wout, y, ln_g, ln_b, after, tm=512):
    S = x1.shape[0]
    M = km.shape[0]

    def body(dx2_ref, x1_ref, qm_ref, km_ref, vm_ref, g_ref, wq_ref, wo_ref, wout_ref, y_ref, lg_ref, lb_ref,
             after_ref, dx1_ref, dx1b_ref, hq_ref, dqm_ref, dy_ref, dya_ref, dkm_ref, dvm_ref, dg_ref, dbo_ref,
             dlg_ref, dlb_ref, dbdw_ref):
        i = pl.program_id(0)
        _acc_init(i, dkm_ref, dvm_ref, dg_ref, dbo_ref, dlg_ref, dlb_ref, dbdw_ref)
        dx2v = dx2_ref[...]
        domt = _nt(wo_ref[...], dx2v).astype(_MXU)
        dqs = []
        for h in range(MH):
            sl = slice(h * MHD, (h + 1) * MHD)
            qh = qm_ref[:, sl]
            kh = km_ref[:, sl]
            e, inv = _softmax_t(_nt(kh, qh))
            p = e * inv
            dp = _nn(vm_ref[:, sl], domt[sl, :])
            ds = (p * (dp - jnp.sum(dp * p, axis=0, keepdims=True))).astype(_MXU)
            dqs.append(_tn(ds, kh) * MSCALE)
            dkm_ref[:, sl] += _nn(ds, qh)
            dvm_ref[:, sl] += _nt(p, domt[sl, :])
        dqm = jnp.concatenate(dqs, axis=1).astype(dqm_ref.dtype)
        dqm_ref[...] = dqm
        dhq = _nt(dqm, wq_ref[...])
        x1v = x1_ref[...]
        g = g_ref[...]
        hq, r = _rms(x1v, g)
        hq_ref[...] = hq.astype(hq_ref.dtype)
        dxn, dg = _rms_bwd(dhq, x1v, r, g)
        dx1 = dx2v + dxn
        dx1_ref[...] = dx1
        dx1b = dx1.astype(dx1b_ref.dtype)
        dx1b_ref[...] = dx1b
        dym = _nt(dx1b, wout_ref[...])
        dya_ref[...] = dym[:, CC:].astype(dya_ref.dtype)
        _acc_row(dg_ref, dg)
        _acc_row(dbo_ref, jnp.sum(dx1, axis=0, keepdims=True))
        lg = lg_ref[...]
        yh, rstd, z = _ln_silu(y_ref[...], lg, lb_ref[...])
        sig = _sigmoid(z)
        dz = dym[:, :CC] * (sig * (1.0 + z * (1.0 - sig)))
        dyh = dz * lg
        dy = rstd * (dyh - jnp.mean(dyh, axis=-1, keepdims=True) - yh * jnp.mean(dyh * yh, axis=-1, keepdims=True))
        dy_ref[...] = dy
        _acc_row(dlg_ref, jnp.sum(dz * yh, axis=0, keepdims=True))
        _acc_row(dlb_ref, jnp.sum(dz, axis=0, keepdims=True))
        _acc_row(dbdw_ref, jnp.sum(dy, axis=0, keepdims=True))

    return _hbm_call(
        body, name="mem_attn_bwd", grid=(S // tm,),
        in_specs=[_rows(tm, D), _rows(tm, D), _rows(tm, D), _full((M, D)), _full((M, D)), _full((1, D)),
                  _once((D, D)), _once((D, D)), _once((D, D)), _rows(tm, CC), _full((1, CC)), _full((1, CC)),
                  _AFTER],
        out_specs=[_rows(tm, D), _rows(tm, D), _rows(tm, D), _rows(tm, D), _rows(tm, CC), _rows(tm, ATT),
                   _full((M, D)), _full((M, D)), _full((8, D)), _full((8, D)),
                   _full((8, CC)), _full((8, CC)), _full((8, CC))],
        out_shape=[_sds((S, D), F32), _sds((S, D), _MXU), _sds((S, D), _MXU), _sds((S, D), _MXU),
                   _sds((S, CC), F32), _sds((S, ATT), _MXU),
                   _sds((M, D), F32), _sds((M, D), F32), _sds((8, D), F32), _sds((8, D), F32),
                   _sds((8, CC), F32), _sds((8, CC), F32), _sds((8, CC), F32)],
        compiler_params=_params(56),
    )(dx2, x1, qm, km, vm, g_q, wq, wo, wout, y, ln_g, ln_b, after)


def _mem_kv_bwd(dkm, dvm, mem, g_kv, memn, wkvT):
    M = mem.shape[0]

    def body(dkm_ref, dvm_ref, mem_ref, g_ref, mn_ref, w_ref, gw_ref, dg_ref):
        dkv = jnp.concatenate([dkm_ref[...], dvm_ref[...]], axis=1).astype(_MXU)
        gw_ref[...] = _tn(dkv, mn_ref[...]).astype(gw_ref.dtype)
        dmn = _nn(dkv, w_ref[...])
        memv = mem_ref[...]
        g = g_ref[...]
        _, r = _rms(memv, g)
        _, dg = _rms_bwd(dmn, memv, r, g)
        dg_ref[...] = jnp.broadcast_to(dg, dg_ref.shape)

    return _hbm_call(
        body, name="mem_kv_bwd", grid=(1,),
        in_specs=[_full((M, D)), _full((M, D)), _full((M, D)), _full((1, D)), _full((M, D)), _full((2 * D, D))],
        out_specs=[_full((2 * D, D)), _full((8, D))],
        out_shape=[_sds((2 * D, D), jnp.bfloat16), _sds((8, D), F32)],
        compiler_params=_params(48),
    )(dkm, dvm, mem, g_kv, memn, wkvT)


def _attn_bwd(qr, kr, vv, sink, dya, after):
    S = qr.shape[0]
    tq = ATTN_BLOCKS * BLK

    def body(q_ref, k_ref, v_ref, sk_ref, do_ref, after_ref, dq_ref, dk_ref, dv_ref, dsk_ref, sink_acc):
        _acc_init(pl.program_id(0), dk_ref, dv_ref, sink_acc)
        base = _attn_mask_base()
        for b in range(ATTN_BLOCKS):
            i = pl.program_id(0) * ATTN_BLOCKS + b
            rows = slice(b * BLK, (b + 1) * BLK)
            start = _attn_window(i, S)
            kb = k_ref[pl.ds(start, 3 * BLK), :]
            vb = v_ref[pl.ds(start, 3 * BLK), :]
            q = q_ref[rows, :]
            do = do_ref[rows, :]
            mask = _attn_mask(base, i, start)
            dqs, dks, dvs = [], [], []
            for hk in range(NKV):
                hs = slice(hk * HD, (hk + 1) * HD)
                qs = _stack_heads(q, hk)
                dos = _stack_heads(do, hk)
                e, inv, es = _attn_probs(qs, kb[:, hs], sk_ref, hk, mask)
                p = e * inv
                dp = _nt(vb[:, hs], dos)
                delta = jnp.sum(dp * p, axis=0, keepdims=True)
                ds = (p * (dp - delta)).astype(_MXU)
                dqs.append(_unstack_heads_t(_tn(kb[:, hs], ds) * SCALE))
                dks.append(_nn(ds, qs))
                dvs.append(_nn(p, dos))
                sink_acc[hk:hk + 1, :] -= es * inv * delta
            dq_ref[rows, :] = jnp.concatenate(dqs, axis=1)
            dk_ref[pl.ds(start, 3 * BLK), :] += jnp.concatenate(dks, axis=1)
            dv_ref[pl.ds(start, 3 * BLK), :] += jnp.concatenate(dvs, axis=1)

        @pl.when(pl.program_id(0) == pl.num_programs(0) - 1)
        def _():
            for h in range(NQ):
                tot = jnp.sum(sink_acc[h // GRP:h // GRP + 1, (h % GRP) * BLK:(h % GRP + 1) * BLK], axis=1,
                              keepdims=True)
                dsk_ref[h:h + 1, :] = jnp.broadcast_to(tot, (1, LANES))

    return _hbm_call(
        body, name="attn_bwd", grid=(S // tq,),
        in_specs=[_rows(tq, ATT), _full((S, KVC)), _full((S, KVC)), _full((1, NQ)), _rows(tq, ATT), _AFTER],
        out_specs=[_rows(tq, ATT), _full((S, KVC)), _full((S, KVC)), _full((8, LANES))],
        out_shape=[_sds((S, ATT), F32), _sds((S, KVC), F32), _sds((S, KVC), F32), _sds((8, LANES), F32)],
        scratch_shapes=[pltpu.VMEM((8, ATTN_COLS), F32)],
        compiler_params=_params(40),
    )(qr, kr, vv, sink, dya, after)


def _conv_bwd(dy, vg, uglu, w_dw, after):
    S = dy.shape[0]
    tm = min(CONV_TILE, S)

    def body(dy_hbm, vg_hbm, uglu_ref, w_ref, after_ref, du_ref, gw_ref, dbin_ref, dyp_ref, vgp_ref, gacc_ref, sh_ref,
             dvg_ref, sems):
        i = pl.program_id(0)
        _acc_init(i, gacc_ref, dbin_ref)
        _fill_padded(i, dy_hbm, dyp_ref, sems.at[0], S)
        _fill_padded(i, vg_hbm, vgp_ref, sems.at[1], S)
        t0 = pl.multiple_of(i * tm, tm)
        _shifted_copies(dyp_ref, sh_ref.at[0], t0, tm)
        _shifted_copies(vgp_ref, sh_ref.at[1], t0, tm)
        _dwconv_tile(sh_ref.at[0], tm, w_ref, True, dvg_ref)
        dvg = dvg_ref[...]
        sub = CONV_ROWS
        for cl in range(CC // LANES):
            lanes = slice(cl * LANES, (cl + 1) * LANES)
            for k, b, off in _conv_taps():
                part = jnp.zeros((8, LANES), F32)
                for r in range(0, tm, sub):
                    prod = (sh_ref[0, cl, 0, CONV_HALO + r:CONV_HALO + r + sub, :]
                            * sh_ref[1, cl, b, off + r:off + r + sub, :])
                    part = part + jnp.sum(prod.reshape(sub // 8, 8, LANES), axis=0)
                gacc_ref[k, :, lanes] += part
        a_v = uglu_ref[:, :CC].astype(F32)
        sg = _sigmoid(uglu_ref[:, CC:].astype(F32))
        da = dvg * sg
        dgate = dvg * a_v * sg * (1.0 - sg)
        du = jnp.concatenate([da, dgate], axis=1)
        du_ref[...] = du.astype(du_ref.dtype)
        _acc_row(dbin_ref, jnp.sum(du, axis=0, keepdims=True))

        @pl.when(i == pl.num_programs(0) - 1)
        def _():
            gw_ref[...] = jnp.zeros(gw_ref.shape, F32)
            for k in range(TAPS):
                gw_ref[k:k + 1, :] = jnp.sum(gacc_ref[k], axis=0, keepdims=True)

    return _hbm_call(
        body, name="conv_bwd", grid=(S // tm,),
        in_specs=[pl.BlockSpec(memory_space=pl.ANY), pl.BlockSpec(memory_space=pl.ANY), _rows(tm, 2 * CC),
                  _full((TAPS, CC)), _AFTER],
        out_specs=[_rows(tm, 2 * CC), _full((TAPS + 1, CC)), _full((8, 2 * CC))],
        out_shape=[_sds((S, 2 * CC), _MXU), _sds((TAPS + 1, CC), F32), _sds((8, 2 * CC), F32)],
        scratch_shapes=[pltpu.VMEM((S + 2 * CONV_HALO, CC), F32), pltpu.VMEM((S + 2 * CONV_HALO, CC), F32),
                        pltpu.VMEM((TAPS, 8, CC), F32), _shift_scratch(tm, 2), pltpu.VMEM((tm, CC), F32),
                        pltpu.SemaphoreType.DMA((2,))],
        compiler_params=_params(56),
    )(dy, vg, uglu, w_dw, after)


def _in_proj_bwd(duglu, dqr, dkr, dv, cosq, sinq, x, dx1, g_mix, winT, tm=512):
    S = x.shape[0]
    NA = ATT + 2 * KVC

    def body(dug_ref, dq_ref, dk_ref, dv_ref, c_ref, s_ref, x_ref, dx1_ref, g_ref, w_ref,
             dx_ref, du_ref, dg_ref, dbin_ref):
        i = pl.program_id(0)
        _acc_init(i, dg_ref, dbin_ref)
        c = c_ref[...]
        s = s_ref[...]
        datt = jnp.concatenate([_rope_bwd(dq_ref[...], c, s), _rope_bwd(dk_ref[...], c, s), dv_ref[...]], axis=1)
        _acc_row(dbin_ref, jnp.sum(datt, axis=0, keepdims=True))
        du = jnp.concatenate([dug_ref[...], datt.astype(du_ref.dtype)], axis=1)
        du_ref[...] = du
        dh = _nn(du, w_ref[...])
        xv = x_ref[...]
        g = g_ref[...]
        _, r = _rms(xv, g)
        dxn, dg = _rms_bwd(dh, xv, r, g)
        dx_ref[...] = dx1_ref[...] + dxn
        _acc_row(dg_ref, dg)

    return _hbm_call(
        body, name="in_proj_bwd", grid=(S // tm,),
        in_specs=[_rows(tm, 2 * CC), _rows(tm, ATT), _rows(tm, KVC), _rows(tm, KVC), _rows(tm, LANES),
                  _rows(tm, LANES), _rows(tm, D), _rows(tm, D), _full((1, D)), _once((INC, D))],
        out_specs=[_rows(tm, D), _rows(tm, INC), _full((8, D)), _full((8, NA))],
        out_shape=[_sds((S, D), F32), _sds((S, INC), _MXU), _sds((8, D), F32), _sds((8, NA), F32)],
        compiler_params=_params(48),
    )(duglu, dqr, dkr, dv, cosq, sinq, x, dx1, g_mix, winT)


def _rope_tables(S):
    pos = jnp.arange(S, dtype=F32)
    inv_freq = ROPE_THETA ** (-jnp.arange(0, HD, 2, dtype=F32) / HD)
    ang = pos[:, None] * inv_freq[None, :]
    cos, sin = jnp.cos(ang), jnp.sin(ang)
    cosq = jnp.tile(jnp.concatenate([cos, cos], axis=1), (1, LANES // HD))
    sinq = jnp.tile(jnp.concatenate([-sin, sin], axis=1), (1, LANES // HD))
    return cosq, sinq


GATHER_GROUPS = (("w_inT", "w_dw"), ("w_out", "w_mem_kvT", "w_mem_q", "w_mem_o"), ("w_gateT", "w_upT"), ("w_down",))
ID_GATHER, ID_SIBLING, ID_CHIP = 1, 5, 8


def _local_step(x, mem, tgt, sp, gather, reduce):
    S = x.shape[0]
    cosq, sinq = _rope_tables(S)

    gather.forward(0, cosq)
    w0 = gather.finish(0, sinq)
    winT, w_dw_full = w0["w_inT"], w0["w_dw"]
    h, uglu, vg, qr, kr, vv = _in_proj_fwd(x, sp["g_mix"], winT, sp["b_in"], cosq, sinq)
    ya = _attn_fwd(qr, kr, vv, sp["attn_sink"])
    gather.forward(1, ya)
    y, yc = _conv_fwd(vg, w_dw_full, sp["b_dw"], sp["g_conv_ln"], sp["b_conv_ln"])
    w1 = gather.finish(1, yc)
    wout, wkvT, wq, wo = w1["w_out"], w1["w_mem_kvT"], w1["w_mem_q"], w1["w_mem_o"]
    km, vm, memn = _mem_kv_fwd(mem, sp["g_mem_kv"], wkvT)
    x1, ymix = _out_proj_fwd(x, yc, ya, wout, sp["b_out"])
    token = gather.forward(2, x1)
    x2, qm, om = _mem_attn_fwd(x1, sp["g_mem_q"], wq, km, vm, wo, token)
    w2 = gather.finish(2, om)
    gather.forward(3, om)
    wgT, wuT, wd = w2["w_gateT"], w2["w_upT"], gather.finish(3, om)["w_down"]
    gt, up, hf, dx3, dx3b, loss_acc, dg_final = _ffn_loss_fwd(x2, tgt, sp["g_ffn"], sp["g_final"], wgT, wuT, wd)

    dx2, dx2b, dgt, dup, act, dg_ffn = _ffn_bwd(dx3, x2, gt, up, sp["g_ffn"], wd, wgT, wuT)
    token = reduce.start(0, _wgrad([(dgt, hf), (dup, hf), (act, dx3b)], "wgrad_ffn", dx2))

    dx1, dx1b, hq, dqm, dy, dya, dkm, dvm, dg_mem_q, db_out, dg_cln, db_cln, db_dw = _mem_attn_bwd(
        dx2, x1, qm, km, vm, sp["g_mem_q"], wq, wo, wout, y, sp["g_conv_ln"], sp["b_conv_ln"], token)
    token = reduce.middle(0, dx1)
    g_wout, g_wq, g_wo = _wgrad([(ymix, dx1b), (hq, dqm), (om, dx2b)], "wgrad_mix_mem", token)
    g_wkvT, dg_mem_kv = _mem_kv_bwd(dkm, dvm, mem, sp["g_mem_kv"], memn, wkvT)
    token = reduce.start(1, [g_wout, g_wq, g_wkvT, g_wo])

    dqr, dkr, dvv, dsink = _attn_bwd(qr, kr, vv, sp["attn_sink"], dya, token)
    token = reduce.middle(1, dqr)
    duglu, g_wdw, dbin_glu = _conv_bwd(dy, vg, uglu, w_dw_full, token)
    grad_x, du, dg_mix, dbin_att = _in_proj_bwd(duglu, dqr, dkr, dvv, cosq, sinq, x, dx1, sp["g_mix"], winT)
    token = reduce.start(2, _wgrad([(du, h)], "wgrad_in", grad_x))

    small = {
        "loss": loss_acc[0:1, :],
        "g_mix": dg_mix[0:1], "b_in": jnp.concatenate([dbin_glu[0:1], dbin_att[0:1]], axis=1),
        "b_dw": db_dw[0:1], "g_conv_ln": dg_cln[0:1], "b_conv_ln": db_cln[0:1],
        "attn_sink": dsink[:, 0].reshape(1, NQ), "b_out": db_out[0:1], "g_mem_q": dg_mem_q[0:1],
        "g_mem_kv": dg_mem_kv[0:1], "g_ffn": dg_ffn[0:1], "g_final": dg_final[0:1],
    }
    token = reduce.middle(2, token, reduce.small_zones(small, g_wdw))
    reduce.finish(0, token)
    reduce.finish(1, token)
    return grad_x


def _place():
    return lax.axis_index("x"), lax.axis_index("y"), lax.axis_index("c")


def _hbm_specs(n):
    return [pl.BlockSpec(memory_space=pl.ANY)] * n


_HBM = pl.BlockSpec(memory_space=pltpu.HBM)
_SEM = pl.BlockSpec(memory_space=pltpu.SEMAPHORE)
_ANY = pl.BlockSpec(memory_space=pl.ANY)
_EFFECT = pltpu.SideEffectType.DATAFLOW_SIDE_EFFECTING
_VMEM = pl.BlockSpec(memory_space=pltpu.VMEM)
_TOKEN = jax.ShapeDtypeStruct((8, LANES), F32)


def _dev_index(p):
    return 4 * p[0] + 2 * p[1] + p[2]


def _landing_zones(shards, me_idx):
    def body(m_ref, *refs):
        n = len(refs) // 2
        for k in range(n):
            refs[n + k][...] = refs[k][...].astype(jnp.bfloat16)

    return pl.pallas_call(
        body, name="landing_zones",
        grid_spec=pltpu.PrefetchScalarGridSpec(
            num_scalar_prefetch=1, grid=(1,),
            in_specs=[pl.BlockSpec(s.shape, lambda i, m: (0, 0)) for s in shards],
            out_specs=[pl.BlockSpec((None,) + s.shape, lambda i, m: (m[0], 0, 0)) for s in shards]),
        out_shape=[pltpu.HBM((NDEV,) + s.shape, jnp.bfloat16) for s in shards],
        compiler_params=_params(40),
    )(me_idx, *shards)


def _gather_start(groups):
    sizes = [len(g) for g in groups]
    flat = [z for g in groups for z in g]
    n = len(flat)

    def body(*refs):
        ins = refs[:n]
        sems = refs[2 * n:]
        x, y, c = _place()
        me = _dev_index((x, y, c))
        peers = [(x, y, 1 - c), (1 - x, y, c), (x, 1 - y, c), (1 - x, 1 - y, c)]
        base = 0
        for gi, size in enumerate(sizes):
            send_sems, recv_sems = sems[2 * gi], sems[2 * gi + 1]
            for k in range(size):
                zone = ins[base + k]
                for j, peer in enumerate(peers):
                    pltpu.make_async_remote_copy(
                        src_ref=zone.at[me], dst_ref=zone.at[me], send_sem=send_sems.at[4 * k + j],
                        recv_sem=recv_sems.at[4 * k + j], device_id=peer, device_id_type=MESH).start()
            base += size

    out_shape = [pltpu.HBM(z.shape, z.dtype) for z in flat]
    out_specs = [_HBM] * n
    for size in sizes:
        out_shape += [pltpu.SemaphoreType.DMA((4 * size,))] * 2
        out_specs += [_SEM, _SEM]
    res = pl.pallas_call(
        body, name="gather_start", in_specs=[_HBM] * n, out_specs=out_specs, out_shape=out_shape,
        input_output_aliases={k: k for k in range(n)},
        compiler_params=pltpu.CompilerParams(has_side_effects=_EFFECT),
    )(*[pltpu.with_memory_space_constraint(z, pltpu.HBM) for z in flat])
    out, base = [], 0
    for gi, size in enumerate(sizes):
        out.append((list(res[base:base + size]), res[n + 2 * gi], res[n + 2 * gi + 1]))
        base += size
    return out


def _gather_forward(zones, send_sems, recv_sems, after, collective_id, name):
    n = len(zones)

    def body(*refs):
        ins = refs[:n]
        send_ref, recv_ref = refs[n], refs[n + 1]
        fsend, frecv = refs[2 * n + 3], refs[2 * n + 4]
        x, y, c = _place()
        sibling = (x, y, 1 - c)
        barrier = pltpu.get_barrier_semaphore()
        pl.semaphore_signal(barrier, inc=1, device_id=sibling, device_id_type=MESH)
        pl.semaphore_wait(barrier, 1)
        chips = [(1 - x, y), (x, 1 - y), (1 - x, 1 - y)]
        for j, chip in enumerate(chips):
            for k in range(n):
                slot = ins[k].at[_dev_index((*chip, c))]
                pltpu.make_async_remote_copy(
                    src_ref=slot, dst_ref=slot, send_sem=send_ref.at[4 * k + 1 + j],
                    recv_sem=recv_ref.at[4 * k + 1 + j], device_id=sibling, device_id_type=MESH).wait_recv()
                pltpu.make_async_remote_copy(
                    src_ref=slot, dst_ref=slot, send_sem=fsend.at[3 * k + j], recv_sem=frecv.at[3 * k + j],
                    device_id=sibling, device_id_type=MESH).start()
        for k in range(n):
            own = ins[k].at[_dev_index((x, y, c))]
            theirs = ins[k].at[_dev_index(sibling)]
            pltpu.make_async_remote_copy(
                src_ref=theirs, dst_ref=theirs, send_sem=send_ref.at[4 * k], recv_sem=recv_ref.at[4 * k],
                device_id=sibling, device_id_type=MESH).wait_recv()
            for j in range(4):
                pltpu.make_async_remote_copy(
                    src_ref=own, dst_ref=own, send_sem=send_ref.at[4 * k + j], recv_sem=recv_ref.at[4 * k + j],
                    device_id=sibling, device_id_type=MESH).wait_send()
        refs[2 * n + 5][...] = jnp.zeros(_TOKEN.shape, _TOKEN.dtype)

    res = pl.pallas_call(
        body, name=name, in_specs=[_HBM] * n + [_SEM, _SEM, _ANY],
        out_specs=[_HBM] * n + [_SEM, _SEM, _VMEM],
        out_shape=[pltpu.HBM(z.shape, z.dtype) for z in zones] + [pltpu.SemaphoreType.DMA((3 * n,))] * 2 + [_TOKEN],
        input_output_aliases={k: k for k in range(n)},
        compiler_params=pltpu.CompilerParams(has_side_effects=_EFFECT, collective_id=collective_id),
    )(*zones, send_sems, recv_sems, after)
    return (list(res[:n]), res[n], res[n + 1]), res[n + 2]


def _gather_finish(zones, fsend, frecv, after, name):
    n = len(zones)

    def body(*refs):
        ins = refs[:n]
        fsend_ref, frecv_ref = refs[n], refs[n + 1]
        x, y, c = _place()
        sibling = (x, y, 1 - c)
        chips = [(1 - x, y), (x, 1 - y), (1 - x, 1 - y)]
        for j, chip in enumerate(chips):
            for k in range(n):
                mine = ins[k].at[_dev_index((*chip, c))]
                theirs = ins[k].at[_dev_index((*chip, 1 - c))]
                pltpu.make_async_remote_copy(
                    src_ref=theirs, dst_ref=theirs, send_sem=fsend_ref.at[3 * k + j],
                    recv_sem=frecv_ref.at[3 * k + j], device_id=sibling, device_id_type=MESH).wait_recv()
                pltpu.make_async_remote_copy(
                    src_ref=mine, dst_ref=mine, send_sem=fsend_ref.at[3 * k + j],
                    recv_sem=frecv_ref.at[3 * k + j], device_id=sibling, device_id_type=MESH).wait_send()

    res = pl.pallas_call(
        body, name=name, in_specs=[_HBM] * n + [_SEM, _SEM, _ANY], out_specs=[_HBM] * n,
        out_shape=[pltpu.HBM(z.shape, z.dtype) for z in zones],
        input_output_aliases={k: k for k in range(n)},
        compiler_params=pltpu.CompilerParams(has_side_effects=_EFFECT),
    )(*zones, fsend, frecv, after)
    return list(res)


class _Gather:
    def __init__(self, started):
        self._phase = list(started)

    def forward(self, gi, after):
        zones, send_sems, recv_sems = self._phase[gi]
        self._phase[gi], token = _gather_forward(zones, send_sems, recv_sems, after, ID_GATHER + gi,
                                                  "gather_forward_%d" % gi)
        return token

    def finish(self, gi, after):
        zones, fsend, frecv = self._phase[gi]
        out = {}
        for name, z in zip(GATHER_GROUPS[gi], _gather_finish(zones, fsend, frecv, after, "gather_finish_%d" % gi)):
            if name == "w_dw":
                out[name] = jnp.transpose(z, (1, 0, 2)).reshape(TAPS + 1, CC)[:TAPS]
            else:
                out[name] = z.reshape(NDEV * z.shape[1], D)
        return out


def _handshake(peers):
    barrier = pltpu.get_barrier_semaphore()
    for peer in peers:
        pl.semaphore_signal(barrier, inc=1, device_id=peer, device_id_type=MESH)
    pl.semaphore_wait(barrier, len(peers))


def _sibling_copies(srcs, lands, send_sems, recv_sems):
    x, y, c = _place()
    return [pltpu.make_async_remote_copy(
        src_ref=srcs[k].at[2 * s + (1 - c)], dst_ref=lands[k].at[s], send_sem=send_sems.at[4 * k + s],
        recv_sem=recv_sems.at[4 * k + s], device_id=(x, y, 1 - c), device_id_type=MESH)
        for k in range(len(srcs)) for s in range(4)]


def _sibling_start(gbs, collective_id, name):
    n = len(gbs)
    lands = [lax.empty((4,) + g.shape[1:], g.dtype) for g in gbs]

    def body(*refs):
        x, y, c = _place()
        _handshake([(x, y, 1 - c)])
        for cp in _sibling_copies(refs[:n], refs[n:2 * n], refs[4 * n], refs[4 * n + 1]):
            cp.start()
        refs[4 * n + 2][...] = jnp.zeros(_TOKEN.shape, _TOKEN.dtype)

    res = pl.pallas_call(
        body, name=name, in_specs=[_HBM] * (2 * n), out_specs=[_HBM] * (2 * n) + [_SEM, _SEM, _VMEM],
        out_shape=[pltpu.HBM(a.shape, a.dtype) for a in gbs + lands] + [pltpu.SemaphoreType.DMA((4 * n,))] * 2
        + [_TOKEN],
        input_output_aliases={k: k for k in range(2 * n)},
        compiler_params=pltpu.CompilerParams(has_side_effects=_EFFECT, collective_id=collective_id),
    )(*[pltpu.with_memory_space_constraint(a, pltpu.HBM) for a in gbs + lands])
    return list(res[:n]), list(res[n:2 * n]), res[2 * n], res[2 * n + 1], res[2 * n + 2]


def _sibling_wait(gbs, lands, send_sems, recv_sems, after, name):
    n = len(gbs)

    def body(*refs):
        for cp in _sibling_copies(refs[:n], refs[n:2 * n], refs[2 * n], refs[2 * n + 1]):
            cp.wait_recv()
            cp.wait_send()

    res = pl.pallas_call(
        body, name=name, in_specs=[_HBM] * (2 * n) + [_SEM, _SEM, _ANY], out_specs=[_HBM] * (2 * n),
        out_shape=[pltpu.HBM(a.shape, a.dtype) for a in gbs + lands],
        input_output_aliases={k: k for k in range(2 * n)},
        compiler_params=pltpu.CompilerParams(has_side_effects=_EFFECT),
    )(*gbs, *lands, send_sems, recv_sems, after)
    return list(res[:n]), list(res[n:])


def _pair_sum(gfs, ts, c_idx, name):
    n = len(gfs)
    halves = 2

    def body(c_ref, *refs):
        for k in range(n):
            refs[2 * n + k][...] = (refs[k][...].astype(F32) + refs[n + k][...].astype(F32)).astype(jnp.bfloat16)

    in_specs, out_specs, out_shape = [], [], []
    for g in gfs:
        rh = g.shape[1] // halves
        in_specs.append(pl.BlockSpec((None, rh, D), lambda s, h, c_ref: (2 * s + c_ref[0], h, 0)))
    for g in gfs:
        rh = g.shape[1] // halves
        in_specs.append(pl.BlockSpec((None, rh, D), lambda s, h, c_ref: (s, h, 0)))
        out_specs.append(pl.BlockSpec((None, rh, D), lambda s, h, c_ref: (s, h, 0)))
        out_shape.append(_sds((4, g.shape[1], D), jnp.bfloat16))
    return pl.pallas_call(
        body, name=name,
        grid_spec=pltpu.PrefetchScalarGridSpec(num_scalar_prefetch=1, grid=(4, halves), in_specs=in_specs,
                                               out_specs=out_specs),
        out_shape=out_shape,
        compiler_params=pltpu.CompilerParams(dimension_semantics=("arbitrary", "arbitrary"),
                                             vmem_limit_bytes=48 * 2 ** 20),
    )(c_idx, *gfs, *ts)


def _chip_copies(pbs, lands, smalls, send_sems, recv_sems):
    x, y, c = _place()
    own = 2 * x + y
    me = _dev_index((x, y, c))
    chips = [(1 - x, y), (x, 1 - y), (1 - x, 1 - y)]
    n = len(pbs)
    copies = [pltpu.make_async_remote_copy(
        src_ref=pbs[k].at[2 * chip[0] + chip[1]], dst_ref=lands[k].at[own], send_sem=send_sems.at[3 * k + j],
        recv_sem=recv_sems.at[3 * k + j], device_id=(*chip, c), device_id_type=MESH)
        for k in range(n) for j, chip in enumerate(chips)]
    peers = [(x, y, 1 - c)] + [(*chip, cc) for chip in chips for cc in (c, 1 - c)]
    copies += [pltpu.make_async_remote_copy(
        src_ref=smalls[t].at[me], dst_ref=smalls[t].at[me], send_sem=send_sems.at[3 * n + 7 * t + j],
        recv_sem=recv_sems.at[3 * n + 7 * t + j], device_id=peer, device_id_type=MESH)
        for t in range(len(smalls)) for j, peer in enumerate(peers)]
    return copies


def _chip_start(pbs, smalls, collective_id, name):
    n, m = len(pbs), len(smalls)
    lands = [lax.empty(p.shape, p.dtype) for p in pbs]
    arrays = list(pbs) + lands + list(smalls)
    na = len(arrays)
    nsem = 3 * n + 7 * m

    def body(*refs):
        x, y, c = _place()
        chips = [(1 - x, y), (x, 1 - y), (1 - x, 1 - y)]
        peers = [(*chip, c) for chip in chips]
        if m:
            peers += [(x, y, 1 - c)] + [(*chip, 1 - c) for chip in chips]
        _handshake(peers)
        for cp in _chip_copies(refs[:n], refs[n:2 * n], refs[2 * n:na], refs[2 * na], refs[2 * na + 1]):
            cp.start()
        refs[2 * na + 2][...] = jnp.zeros(_TOKEN.shape, _TOKEN.dtype)

    res = pl.pallas_call(
        body, name=name, in_specs=[_HBM] * na, out_specs=[_HBM] * na + [_SEM, _SEM, _VMEM],
        out_shape=[pltpu.HBM(a.shape, a.dtype) for a in arrays] + [pltpu.SemaphoreType.DMA((nsem,))] * 2 + [_TOKEN],
        input_output_aliases={k: k for k in range(na)},
        compiler_params=pltpu.CompilerParams(has_side_effects=_EFFECT, collective_id=collective_id),
    )(*[pltpu.with_memory_space_constraint(a, pltpu.HBM) for a in arrays])
    return (list(res[:n]), list(res[n:2 * n]), list(res[2 * n:na]), res[na], res[na + 1]), res[na + 2]


def _chip_wait(pbs, lands, smalls, send_sems, recv_sems, after, name):
    n = len(pbs)
    arrays = list(pbs) + list(lands) + list(smalls)
    na = len(arrays)

    def body(*refs):
        for cp in _chip_copies(refs[:n], refs[n:2 * n], refs[2 * n:na], refs[na], refs[na + 1]):
            cp.wait_recv()
            cp.wait_send()

    res = pl.pallas_call(
        body, name=name, in_specs=[_HBM] * na + [_SEM, _SEM, _ANY], out_specs=[_HBM] * na,
        out_shape=[pltpu.HBM(a.shape, a.dtype) for a in arrays],
        input_output_aliases={k: k for k in range(na)},
        compiler_params=pltpu.CompilerParams(has_side_effects=_EFFECT),
    )(*arrays, send_sems, recv_sems, after)
    return list(res[:n]), list(res[n:2 * n]), list(res[2 * n:])


def _chip_sum(pbs, qs, own_idx, states, name):
    n = len(pbs)
    halves = 2
    n_in = 4 * n + 3 * sum(s is not None for s in states)

    def body(o_ref, *refs):
        nxt_in, nxt_out = 4 * n, n_in
        for k in range(n):
            acc = refs[4 * k][...].astype(F32)
            for j in range(1, 4):
                acc = acc + refs[4 * k + j][...].astype(F32)
            refs[nxt_out][...] = acc
            nxt_out += 1
            if states[k] is not None:
                w_ref, m_ref, v_ref = refs[nxt_in:nxt_in + 3]
                nxt_in += 3
                d, nm, nv = _adamw_math(w_ref[...], acc, m_ref[...], v_ref[...])
                refs[nxt_out][...] = d
                refs[nxt_out + 1][...] = nm
                refs[nxt_out + 2][...] = nv
                nxt_out += 3

    def other(j):
        return lambda h, o: ((o[0] + j) % 4, h, 0)

    def rows(r):
        return pl.BlockSpec((r // halves, D), lambda h, o: (h, 0))

    in_specs, ins = [], []
    for p, q in zip(pbs, qs):
        rh = p.shape[1] // halves
        in_specs.append(pl.BlockSpec((None, rh, D), lambda h, o: (o[0], h, 0)))
        ins.append(p)
        for j in range(1, 4):
            in_specs.append(pl.BlockSpec((None, rh, D), other(j)))
            ins.append(q)
    out_specs, out_shape, counts = [], [], []
    for p, s in zip(pbs, states):
        r = p.shape[1]
        if s is not None:
            in_specs += [rows(r)] * 3
            ins += list(s)
        counts.append(1 if s is None else 4)
        out_specs += [rows(r)] * counts[-1]
        out_shape += [_sds((r, D), F32)] * counts[-1]
    res = pl.pallas_call(
        body, name=name,
        grid_spec=pltpu.PrefetchScalarGridSpec(num_scalar_prefetch=1, grid=(halves,), in_specs=in_specs,
                                               out_specs=out_specs),
        out_shape=out_shape,
        compiler_params=pltpu.CompilerParams(dimension_semantics=("arbitrary",), vmem_limit_bytes=56 * 2 ** 20),
    )(own_idx, *ins)
    out, base = [], 0
    for c in counts:
        out.append(tuple(res[base:base + c]))
        base += c
    return out


REDUCE_GROUPS = (("w_gateT", "w_upT", "w_down"), ("w_out", "w_mem_q", "w_mem_kvT", "w_mem_o"), ("w_inT",))


class _Reduce:
    def __init__(self, states):
        x, y, c = _place()
        self._c = c.astype(jnp.int32).reshape(1)
        self._own = (2 * x + y).astype(jnp.int32).reshape(1)
        self._me = _dev_index((x, y, c))
        self._state = {}
        self._adam = states
        self.results = {}
        self.smalls = None

    def start(self, gi, wgrads):
        gbs = [g.reshape(NDEV, g.shape[0] // NDEV, D) for g in wgrads]
        gbs, lands, send_sems, recv_sems, token = _sibling_start(gbs, ID_SIBLING + gi, "reduce_sibling_start_%d" % gi)
        self._state[gi] = (gbs, lands, send_sems, recv_sems)
        return token

    def middle(self, gi, after, smalls=()):
        gbs, lands, send_sems, recv_sems = self._state[gi]
        gbs, ts = _sibling_wait(gbs, lands, send_sems, recv_sems, after, "reduce_sibling_wait_%d" % gi)
        pbs = _pair_sum(gbs, ts, self._c, "reduce_pair_sum_%d" % gi)
        self._state[gi], token = _chip_start(pbs, list(smalls), ID_CHIP + gi, "reduce_chip_start_%d" % gi)
        return token

    def finish(self, gi, after):
        pbs, lands, smalls, send_sems, recv_sems = self._state[gi]
        pbs, qs, smalls = _chip_wait(pbs, lands, smalls, send_sems, recv_sems, after, "reduce_chip_wait_%d" % gi)
        names = REDUCE_GROUPS[gi]
        states = [self._adam.get(n) for n in names]
        self.results.update(zip(names, _chip_sum(pbs, qs, self._own, states, "reduce_update_%d" % gi)))
        if smalls:
            self.smalls = smalls

    def small_zones(self, small, g_wdw):
        pack = jnp.concatenate(
            [jnp.pad(small[name], ((0, 0), (0, lanes - small[name].shape[1]))) for name, lanes, _ in PACK], axis=1)
        zones = []
        for a in (pack, g_wdw):
            zone = lax.empty((NDEV,) + a.shape, a.dtype)
            zones.append(lax.dynamic_update_slice(zone, a[None], (self._me,) + (0,) * a.ndim))
        return zones


def _adamw_math(w, g, m, v):
    m = ADAM_B1 * m + (1.0 - ADAM_B1) * g
    v = ADAM_B2 * v + (1.0 - ADAM_B2) * (g * g)
    m_hat = m / (1.0 - ADAM_B1 ** ADAM_STEP)
    v_hat = v / (1.0 - ADAM_B2 ** ADAM_STEP)
    delta = -ADAM_LR * (m_hat / (jnp.sqrt(v_hat) + ADAM_EPS) + ADAM_WD * w)
    return delta, m, v


def _adamw(w, g, m, v, name):
    R, C = w.shape
    rc = 256 if R % 256 == 0 and R > 256 else R

    def body(w_ref, g_ref, m_ref, v_ref, d_ref, nm_ref, nv_ref):
        d, nm, nv = _adamw_math(w_ref[...], g_ref[...], m_ref[...], v_ref[...])
        d_ref[...] = d
        nm_ref[...] = nm
        nv_ref[...] = nv

    return _hbm_call(
        body, name=name, grid=(R // rc,),
        in_specs=[_rows(rc, C)] * 4, out_specs=[_rows(rc, C)] * 3,
        out_shape=[_sds((R, C), F32)] * 3,
        compiler_params=_params(40),
    )(w, g, m, v)


PACK = (("loss", LANES, 1), ("g_mix", D, D), ("b_in", INC, INC), ("b_dw", CC, CC), ("g_conv_ln", CC, CC),
        ("b_conv_ln", CC, CC), ("attn_sink", LANES, NQ), ("b_out", D, D), ("g_mem_q", D, D), ("g_mem_kv", D, D),
        ("g_ffn", D, D), ("g_final", D, D))
PACK_LANES = sum(p[1] for p in PACK)


def _small_update(packs, gdws, ws, ms, vs):
    names = [p[0] for p in PACK[1:]]
    n = len(names)

    def body(*refs):
        packs_ref, gdws_ref = refs[0], refs[1]
        w_refs, m_refs, v_refs = refs[2:2 + n], refs[2 + n:2 + 2 * n], refs[2 + 2 * n:2 + 3 * n]
        outs = refs[2 + 3 * n:]
        tot = packs_ref[0]
        gdw = gdws_ref[0]
        for s in range(1, NDEV):
            tot = tot + packs_ref[s]
            gdw = gdw + gdws_ref[s]
        outs[0][...] = tot[:, 0:LANES]
        outs[1][...] = gdw
        off = LANES
        for k, (_, lanes, used) in enumerate(PACK[1:]):
            g = tot[:, off:off + used]
            off += lanes
            d, nm, nv = _adamw_math(w_refs[k][...], g, m_refs[k][...], v_refs[k][...])
            o = outs[2 + 4 * k:6 + 4 * k]
            o[0][...] = g
            o[1][...] = d
            o[2][...] = nm
            o[3][...] = nv

    def full(a):
        nd = len(a.shape)
        return pl.BlockSpec(a.shape, lambda i: (0,) * nd)

    ins = [packs, gdws, *ws, *ms, *vs]
    out_shape = [_sds((1, LANES), F32), _sds(gdws.shape[1:], F32)]
    for w in ws:
        out_shape += [_sds(w.shape, F32)] * 4
    return pl.pallas_call(
        body, name="small_update", grid=(1,),
        in_specs=[full(a) for a in ins], out_specs=[full(o) for o in out_shape], out_shape=out_shape,
        compiler_params=_params(40),
    )(*ins)


def kernel(x, mem, g_mix, w_in, b_in, w_dw, b_dw, g_conv_ln, b_conv_ln, attn_sink, w_out, b_out, g_mem_q, g_mem_kv, w_mem_q, w_mem_kv, w_mem_o, g_ffn, w_gate, w_up, w_down, g_final, loss_target, m_g_mix, m_w_in, m_b_in, m_w_dw, m_b_dw, m_g_conv_ln, m_b_conv_ln, m_attn_sink, m_w_out, m_b_out, m_g_mem_q, m_g_mem_kv, m_w_mem_q, m_w_mem_kv, m_w_mem_o, m_g_ffn, m_w_gate, m_w_up, m_w_down, m_g_final, v_g_mix, v_w_in, v_b_in, v_w_dw, v_b_dw, v_g_conv_ln, v_b_conv_ln, v_attn_sink, v_w_out, v_b_out, v_g_mem_q, v_g_mem_kv, v_w_mem_q, v_w_mem_kv, v_w_mem_o, v_g_ffn, v_w_gate, v_w_up, v_w_down, v_g_final):
    given = dict(locals())
    bf = jnp.bfloat16
    xi, yi, ci = _place()
    me = 4 * xi + 2 * yi + ci

    transposed = {"w_inT": w_in, "w_mem_kvT": w_mem_kv, "w_gateT": w_gate, "w_upT": w_up}
    plain = {"w_out": w_out, "w_mem_q": w_mem_q, "w_mem_o": w_mem_o, "w_down": w_down}
    shards = [transposed[n][0].T if n in transposed else plain[n][0] for n in W_NAMES]
    zones = dict(zip(W_NAMES, _landing_zones(shards, me.astype(jnp.int32).reshape(1))))
    dw = jnp.pad(w_dw[0], ((0, 1), (0, 0)))
    zones["w_dw"] = lax.dynamic_update_slice(lax.empty((NDEV,) + dw.shape, dw.dtype), dw[None], (me, 0, 0))
    gather = _Gather(_gather_start([[zones[n] for n in names] for names in GATHER_GROUPS]))

    sp = {n: given[n] for n in ("g_mix", "b_in", "b_dw", "g_conv_ln", "b_conv_ln", "attn_sink", "b_out", "g_mem_q",
                                "g_mem_kv", "g_ffn")}
    sp["g_final"] = g_final.reshape(1, D)
    frames = {"w_inT": ("w_in", True), "w_gateT": ("w_gate", True), "w_upT": ("w_up", True),
              "w_out": ("w_out", False), "w_mem_q": ("w_mem_q", False), "w_mem_o": ("w_mem_o", False),
              "w_down": ("w_down", False)}
    states = {}
    for frame, (n, transposed) in frames.items():
        wmv = (given[n][0], given["m_" + n][0], given["v_" + n][0])
        states[frame] = tuple(a.T for a in wmv) if transposed else wmv
    reduce = _Reduce(states)
    grad_x = _local_step(x[0], mem[0], loss_target[0], sp, gather, reduce)
    reduce.finish(2, reduce.results["w_out"][1])

    grads, deltas, new_m, new_v = {}, {}, {}, {}
    for frame, (n, transposed) in frames.items():
        outs = [a.T if transposed else a for a in reduce.results[frame]]
        grads[n], deltas[n], new_m[n], new_v[n] = [a[None] for a in outs]

    def update(n, g):
        d, nm, nv = _adamw(given[n][0], g, given["m_" + n][0], given["v_" + n][0], "adamw_" + n)
        grads[n], deltas[n], new_m[n], new_v[n] = g[None], d[None], nm[None], nv[None]

    update("w_mem_kv", reduce.results["w_mem_kvT"][0].T)

    packs, gdws = reduce.smalls
    names_small = [p[0] for p in PACK[1:]]

    def as_row(a):
        return a.reshape(1, -1)

    res = _small_update(packs, gdws, [as_row(given[n]) for n in names_small],
                        [as_row(given["m_" + n]) for n in names_small],
                        [as_row(given["v_" + n]) for n in names_small])
    loss = res[0][0, 0]
    for k, n in enumerate(names_small):
        shape = given[n].shape
        g, d, nm, nv = res[2 + 4 * k:6 + 4 * k]
        grads[n], deltas[n], new_m[n], new_v[n] = (g.reshape(shape), d.reshape(shape), nm.reshape(shape),
                                                   nv.reshape(shape))
    update("w_dw", lax.dynamic_slice(res[1], (0, me * (CC // NDEV)), (TAPS, CC // NDEV)))

    order = ("g_mix", "w_in", "b_in", "w_dw", "b_dw", "g_conv_ln", "b_conv_ln", "attn_sink", "w_out", "b_out",
             "g_mem_q", "g_mem_kv", "w_mem_q", "w_mem_kv", "w_mem_o", "g_ffn", "w_gate", "w_up", "w_down", "g_final")
    return (loss, grad_x[None], *[grads[n] for n in order], *[deltas[n] for n in order],
            *[new_m[n] for n in order], *[new_v[n] for n in order])
```

```python
import jax
import jax.numpy as jnp
from jax import lax
from jax.experimental import pallas as pl
from jax.experimental.pallas import tpu as pltpu

D = 1024
CC = 512
HD = 64
NQ = 8
NKV = 2
GRP = NQ // NKV
ATT = NQ * HD
KVC = NKV * HD
BLK = 128
MH = 4
MHD = D // MH
F = 2816
FCH = 256
FCH_BWD = F
INC = 2 * CC + ATT + 2 * KVC
TAPS = 31
EPS = 1e-6
ROPE_THETA = 10000.0
SCALE = HD ** -0.5
MSCALE = MHD ** -0.5
LANES = 128
NDEV = 8

ADAM_LR = 0.001
ADAM_B1 = 0.9
ADAM_B2 = 0.999
ADAM_EPS = 1e-08
ADAM_WD = 0.01
ADAM_STEP = 10

F32 = jnp.float32
_MXU = jnp.bfloat16
MESH = pl.DeviceIdType.MESH

W_NAMES = ("w_inT", "w_out", "w_mem_q", "w_mem_kvT", "w_mem_o", "w_gateT", "w_upT", "w_down")


def _mm(a, b, ca, cb):
    return lax.dot_general(a.astype(_MXU), b.astype(_MXU), (((ca,), (cb,)), ((), ())),
                           preferred_element_type=F32)


def _nn(a, b):
    return _mm(a, b, 1, 0)


def _nt(a, b):
    return _mm(a, b, 1, 1)


def _tn(a, b):
    return _mm(a, b, 0, 0)


def _sigmoid(x):
    return 0.5 * jnp.tanh(0.5 * x) + 0.5


def _rms(x, g):
    r = lax.rsqrt(jnp.mean(x * x, axis=-1, keepdims=True) + EPS)
    return x * r * g, r


def _rms_bwd(dy, x, r, g):
    xh = x * r
    dg = jnp.sum(dy * xh, axis=0, keepdims=True)
    dxh = dy * g
    dx = r * (dxh - xh * jnp.mean(dxh * xh, axis=-1, keepdims=True))
    return dx, dg


def _rope(t, c, s):
    n = t.shape[1]
    reps = n // LANES
    if reps > 1:
        c = jnp.tile(c, (1, reps))
        s = jnp.tile(s, (1, reps))
    lane = lax.broadcasted_iota(jnp.int32, t.shape, 1)
    first = (lane & (HD - 1)) < (HD // 2)
    partner = jnp.where(first, pltpu.roll(t, n - HD // 2, 1), pltpu.roll(t, HD // 2, 1))
    return t * c + partner * s


def _rope_bwd(dt, c, s):
    n = dt.shape[1]
    reps = n // LANES
    if reps > 1:
        c = jnp.tile(c, (1, reps))
        s = jnp.tile(s, (1, reps))
    lane = lax.broadcasted_iota(jnp.int32, dt.shape, 1)
    first = (lane & (HD - 1)) < (HD // 2)
    ds = dt * s
    partner = jnp.where(first, pltpu.roll(ds, n - HD // 2, 1), pltpu.roll(ds, HD // 2, 1))
    return dt * c + partner


def _acc_init(i, *refs):
    @pl.when(i == 0)
    def _():
        for r in refs:
            r[...] = jnp.zeros(r.shape, r.dtype)


def _acc_row(ref, v):
    ref[...] += jnp.broadcast_to(v, ref.shape)


def _rows(tm, n):
    return pl.BlockSpec((tm, n), lambda i: (i, 0))


def _full(shape):
    nd = len(shape)
    return pl.BlockSpec(shape, lambda i: (0,) * nd)


def _once(shape):
    nd = len(shape)
    return pl.BlockSpec(shape, lambda i: (0,) * nd, pipeline_mode=pl.Buffered(1))


def _params(vmem_mb):
    return pltpu.CompilerParams(dimension_semantics=("arbitrary",), vmem_limit_bytes=vmem_mb * 2 ** 20)


def _sds(shape, dtype):
    return pltpu.HBM(shape, dtype)


def _hbm_call(*args, **kwargs):
    call = pl.pallas_call(*args, **kwargs)
    return lambda *ops: call(*[pltpu.with_memory_space_constraint(o, pltpu.HBM) for o in ops])


def _mem_kv_fwd(mem, g_kv, wkvT):
    M = mem.shape[0]

    def body(mem_ref, g_ref, w_ref, km_ref, vm_ref, mn_ref):
        y, _ = _rms(mem_ref[...], g_ref[...])
        kv = _nt(y, w_ref[...])
        km_ref[...] = kv[:, :D].astype(km_ref.dtype)
        vm_ref[...] = kv[:, D:].astype(vm_ref.dtype)
        mn_ref[...] = y.astype(mn_ref.dtype)

    return _hbm_call(
        body, name="mem_kv_fwd", grid=(1,),
        in_specs=[_full((M, D)), _full((1, D)), _full((2 * D, D))],
        out_specs=[_full((M, D))] * 3,
        out_shape=[_sds((M, D), _MXU)] * 3,
        compiler_params=_params(40),
    )(mem, g_kv, wkvT)


def _in_proj_fwd(x, g_mix, winT, b_in, cosq, sinq, tm=512):
    S = x.shape[0]

    def body(x_ref, g_ref, w_ref, b_ref, c_ref, s_ref, h_ref, uglu_ref, vg_ref, q_ref, k_ref, v_ref):
        h, _ = _rms(x_ref[...], g_ref[...])
        hb = h.astype(h_ref.dtype)
        h_ref[...] = hb
        u = _nt(hb, w_ref[...]) + b_ref[...]
        uglu_ref[...] = u[:, :2 * CC].astype(uglu_ref.dtype)
        vg_ref[...] = u[:, :CC] * _sigmoid(u[:, CC:2 * CC])
        c = c_ref[...]
        s = s_ref[...]
        q_ref[...] = (_rope(u[:, 2 * CC:2 * CC + ATT], c, s) * SCALE).astype(q_ref.dtype)
        k_ref[...] = _rope(u[:, 2 * CC + ATT:2 * CC + ATT + KVC], c, s).astype(k_ref.dtype)
        v_ref[...] = u[:, 2 * CC + ATT + KVC:].astype(v_ref.dtype)

    return _hbm_call(
        body, name="in_proj_fwd", grid=(S // tm,),
        in_specs=[_rows(tm, D), _full((1, D)), _full((INC, D)), _full((1, INC)), _rows(tm, LANES), _rows(tm, LANES)],
        out_specs=[_rows(tm, D), _rows(tm, 2 * CC), _rows(tm, CC), _rows(tm, ATT), _rows(tm, KVC), _rows(tm, KVC)],
        out_shape=[_sds((S, D), _MXU), _sds((S, 2 * CC), _MXU), _sds((S, CC), F32), _sds((S, ATT), _MXU),
                   _sds((S, KVC), _MXU), _sds((S, KVC), _MXU)],
        compiler_params=_params(48),
    )(x, g_mix, winT, b_in, cosq, sinq)


CONV_HALO = 16


CONV_TILE = 512


def _shift_scratch(tm, sets):
    return pltpu.VMEM((sets, CC // LANES, 8, tm + 24, LANES), F32)


def _shifted_copies(pad_ref, sh_ref, t0, tm):
    for cl in range(CC // LANES):
        blk = pad_ref[pl.ds(t0, tm + 2 * CONV_HALO), cl * LANES:(cl + 1) * LANES]
        for b in range(8):
            sh_ref[cl, b, :, :] = blk[b:b + tm + 24, :]


def _conv_taps():
    return [(8 * a + b - 1, b, 8 * a) for b in range(8) for a in range(4) if 0 <= 8 * a + b - 1 < TAPS]


CONV_ROWS = 64
SUBLANES = 8


def _dwconv_tile(sh_ref, tm, w_ref, flip, out_ref):
    for cl in range(CC // LANES):
        lanes = slice(cl * LANES, (cl + 1) * LANES)

        def chunk(c, carry):
            r = pl.multiple_of(c * CONV_ROWS, CONV_ROWS)
            accs = [None] * 4
            for n, (k, b, off) in enumerate(_conv_taps()):
                kk = TAPS - 1 - k if flip else k
                term = sh_ref[cl, b, pl.ds(off + r, CONV_ROWS), :] * w_ref[kk:kk + 1, lanes]
                accs[n % 4] = term if accs[n % 4] is None else accs[n % 4] + term
            out_ref[pl.ds(r, CONV_ROWS), lanes] = (accs[0] + accs[1]) + (accs[2] + accs[3])
            return carry

        lax.fori_loop(0, tm // CONV_ROWS, chunk, 0)


def _fill_padded(i, src_hbm, pad_ref, sem, S):
    @pl.when(i == 0)
    def _():
        zeros = jnp.zeros((CONV_HALO, pad_ref.shape[1]), pad_ref.dtype)
        pad_ref[0:CONV_HALO, :] = zeros
        pad_ref[CONV_HALO + S:2 * CONV_HALO + S, :] = zeros
        cp = pltpu.make_async_copy(src_hbm, pad_ref.at[pl.ds(CONV_HALO, S), :], sem)
        cp.start()
        cp.wait()


def _ln_silu(y, g, b):
    mu = jnp.mean(y, axis=-1, keepdims=True)
    yc = y - mu
    var = jnp.mean(yc * yc, axis=-1, keepdims=True)
    rstd = lax.rsqrt(var + EPS)
    yh = yc * rstd
    z = yh * g + b
    return yh, rstd, z


def _conv_fwd(vg, w_dw, b_dw, ln_g, ln_b):
    S = vg.shape[0]
    tm = min(CONV_TILE, S)

    def body(vg_hbm, w_ref, b_ref, g_ref, bb_ref, y_ref, yc_ref, pad_ref, sh_ref, sem):
        i = pl.program_id(0)
        _fill_padded(i, vg_hbm, pad_ref, sem, S)
        _shifted_copies(pad_ref, sh_ref.at[0], pl.multiple_of(i * tm, tm), tm)
        _dwconv_tile(sh_ref.at[0], tm, w_ref, False, y_ref)
        y = y_ref[...] + b_ref[...]
        y_ref[...] = y
        _, _, z = _ln_silu(y, g_ref[...], bb_ref[...])
        yc_ref[...] = (z * _sigmoid(z)).astype(yc_ref.dtype)

    return _hbm_call(
        body, name="conv_fwd", grid=(S // tm,),
        in_specs=[pl.BlockSpec(memory_space=pl.ANY), _full((TAPS, CC)), _full((1, CC)), _full((1, CC)), _full((1, CC))],
        out_specs=[_rows(tm, CC), _rows(tm, CC)],
        out_shape=[_sds((S, CC), F32), _sds((S, CC), _MXU)],
        scratch_shapes=[pltpu.VMEM((S + 2 * CONV_HALO, CC), F32), _shift_scratch(tm, 1), pltpu.SemaphoreType.DMA],
        compiler_params=_params(40),
    )(vg, w_dw, b_dw, ln_g, ln_b)


def _attn_window(i, S):
    start = jnp.clip((i - 1) * BLK, 0, S - 3 * BLK)
    return pl.multiple_of(start, BLK)


ATTN_COLS = GRP * BLK


def _attn_mask_base():
    keys = lax.broadcasted_iota(jnp.int32, (3 * BLK, ATTN_COLS), 0)
    cols = lax.broadcasted_iota(jnp.int32, (3 * BLK, ATTN_COLS), 1)
    return keys - (cols & (BLK - 1))


def _attn_mask(base, i, start):
    return jnp.abs(base + (start - i * BLK)) <= BLK


def _stack_heads(t, hk):
    return jnp.concatenate([t[:, (hk * GRP + g) * HD:(hk * GRP + g + 1) * HD] for g in range(GRP)], axis=0)


def _unstack_heads_t(t):
    cols = []
    for p in range(GRP // 2):
        pair = jnp.concatenate([t[:, (2 * p) * BLK:(2 * p + 1) * BLK], t[:, (2 * p + 1) * BLK:(2 * p + 2) * BLK]],
                               axis=0)
        cols.append(pair.T)
    return jnp.concatenate(cols, axis=1)


def _attn_probs(qs, kh, sk_ref, hk, mask):
    s = jnp.where(mask, _nt(kh, qs), -1e30)
    sk = jnp.concatenate(
        [jnp.broadcast_to(sk_ref[0:1, hk * GRP + g:hk * GRP + g + 1], (1, BLK)) for g in range(GRP)], axis=1)
    m = jnp.maximum(jnp.max(s, axis=0, keepdims=True), sk)
    e = jnp.exp(s - m)
    es = jnp.exp(sk - m)
    inv = 1.0 / (jnp.sum(e, axis=0, keepdims=True) + es)
    return e, inv, es


ATTN_BLOCKS = 8


def _attn_fwd(qr, kr, vv, sink):
    S = qr.shape[0]
    tq = ATTN_BLOCKS * BLK

    def body(q_ref, k_ref, v_ref, sk_ref, o_ref):
        base = _attn_mask_base()
        for b in range(ATTN_BLOCKS):
            i = pl.program_id(0) * ATTN_BLOCKS + b
            rows = slice(b * BLK, (b + 1) * BLK)
            start = _attn_window(i, S)
            kb = k_ref[pl.ds(start, 3 * BLK), :]
            vb = v_ref[pl.ds(start, 3 * BLK), :]
            q = q_ref[rows, :]
            mask = _attn_mask(base, i, start)
            outs = []
            for hk in range(NKV):
                hs = slice(hk * HD, (hk + 1) * HD)
                e, inv, _ = _attn_probs(_stack_heads(q, hk), kb[:, hs], sk_ref, hk, mask)
                outs.append(_unstack_heads_t(_tn(vb[:, hs], e) * inv))
            o_ref[rows, :] = jnp.concatenate(outs, axis=1).astype(o_ref.dtype)

    return _hbm_call(
        body, name="attn_fwd", grid=(S // tq,),
        in_specs=[_rows(tq, ATT), _full((S, KVC)), _full((S, KVC)), _full((1, NQ))],
        out_specs=_rows(tq, ATT),
        out_shape=_sds((S, ATT), _MXU),
        compiler_params=_params(40),
    )(qr, kr, vv, sink)


def _out_proj_fwd(x, yc, ya, wout, b_out, tm=512):
    S = x.shape[0]

    def body(x_ref, yc_ref, ya_ref, w_ref, b_ref, x1_ref, ymix_ref):
        ymix = jnp.concatenate([yc_ref[...], ya_ref[...]], axis=1)
        ymix_ref[...] = ymix
        x1_ref[...] = x_ref[...] + _nn(ymix, w_ref[...]) + b_ref[...]

    return _hbm_call(
        body, name="out_proj_fwd", grid=(S // tm,),
        in_specs=[_rows(tm, D), _rows(tm, CC), _rows(tm, ATT), _full((D, D)), _full((1, D))],
        out_specs=[_rows(tm, D), _rows(tm, D)],
        out_shape=[_sds((S, D), F32), _sds((S, D), _MXU)],
        compiler_params=_params(40),
    )(x, yc, ya, wout, b_out)


def _softmax_t(s):
    m = jnp.max(s, axis=0, keepdims=True)
    e = jnp.exp(s - m)
    return e, 1.0 / jnp.sum(e, axis=0, keepdims=True)


def _softmax(s):
    m = jnp.max(s, axis=-1, keepdims=True)
    e = jnp.exp(s - m)
    return e * (1.0 / jnp.sum(e, axis=-1, keepdims=True))


def _mem_attn_fwd(x1, g_q, wq, km, vm, wo, after, tm=512):
    S = x1.shape[0]
    M = km.shape[0]

    def body(x1_ref, g_ref, wq_ref, km_ref, vm_ref, wo_ref, after_ref, x2_ref, qm_ref, om_ref):
        x1v = x1_ref[...]
        hq, _ = _rms(x1v, g_ref[...])
        qm = (_nn(hq, wq_ref[...]) * MSCALE).astype(qm_ref.dtype)
        qm_ref[...] = qm
        oms = []
        for h in range(MH):
            sl = slice(h * MHD, (h + 1) * MHD)
            p = _softmax(_nt(qm[:, sl], km_ref[:, sl]))
            oms.append(_nn(p, vm_ref[:, sl]))
        om = jnp.concatenate(oms, axis=1).astype(om_ref.dtype)
        om_ref[...] = om
        x2_ref[...] = x1v + _nn(om, wo_ref[...])

    return _hbm_call(
        body, name="mem_attn_fwd", grid=(S // tm,),
        in_specs=[_rows(tm, D), _full((1, D)), _full((D, D)), _full((M, D)), _full((M, D)), _full((D, D)), _AFTER],
        out_specs=[_rows(tm, D)] * 3,
        out_shape=[_sds((S, D), F32), _sds((S, D), _MXU), _sds((S, D), _MXU)],
        compiler_params=_params(48),
    )(x1, g_q, wq, km, vm, wo, after)


def _ffn_loss_fwd(x2, tgt, g_ffn, g_final, wgT, wuT, wd, tm=512):
    S = x2.shape[0]

    def body(x2_ref, t_ref, gf_ref, gl_ref, wg_ref, wu_ref, wd_ref,
             gt_ref, up_ref, hf_ref, dx3_ref, dx3b_ref, loss_ref, dgl_ref):
        i = pl.program_id(0)
        _acc_init(i, loss_ref, dgl_ref)
        x2v = x2_ref[...]
        hf, _ = _rms(x2v, gf_ref[...])
        hb = hf.astype(hf_ref.dtype)
        hf_ref[...] = hb
        x3 = x2v
        for c in range(F // FCH):
            sl = slice(c * FCH, (c + 1) * FCH)
            gt = _nt(hb, wg_ref[sl, :])
            up = _nt(hb, wu_ref[sl, :])
            gt_ref[:, sl] = gt.astype(gt_ref.dtype)
            up_ref[:, sl] = up.astype(up_ref.dtype)
            x3 = x3 + _nn(gt * _sigmoid(gt) * up, wd_ref[sl, :])
        gl = gl_ref[...]
        y, r = _rms(x3, gl)
        diff = y - t_ref[...]
        _acc_row(loss_ref, jnp.sum(jnp.sum(diff * diff, axis=-1, keepdims=True), axis=0, keepdims=True) * (0.5 / D))
        dx3, dgl = _rms_bwd(diff * (1.0 / D), x3, r, gl)
        dx3_ref[...] = dx3
        dx3b_ref[...] = dx3.astype(dx3b_ref.dtype)
        _acc_row(dgl_ref, dgl)

    return _hbm_call(
        body, name="ffn_loss_fwd", grid=(S // tm,),
        in_specs=[_rows(tm, D), _rows(tm, D), _full((1, D)), _full((1, D)),
                  _once((F, D)), _once((F, D)), _once((F, D))],
        out_specs=[_rows(tm, F), _rows(tm, F), _rows(tm, D), _rows(tm, D), _rows(tm, D),
                   _full((8, LANES)), _full((8, D))],
        out_shape=[_sds((S, F), _MXU), _sds((S, F), _MXU), _sds((S, D), _MXU), _sds((S, D), F32),
                   _sds((S, D), _MXU), _sds((8, LANES), F32), _sds((8, D), F32)],
        compiler_params=_params(56),
    )(x2, tgt, g_ffn, g_final, wgT, wuT, wd)


def _ffn_bwd(dx3, x2, gt, up, g_ffn, wd, wgT, wuT, tm=256):
    S = x2.shape[0]

    def body(dx3_ref, x2_ref, gt_ref, up_ref, g_ref, wd_ref, wg_ref, wu_ref,
             dx2_ref, dx2b_ref, dgt_ref, dup_ref, act_ref, dg_ref):
        i = pl.program_id(0)
        _acc_init(i, dg_ref)
        dx3v = dx3_ref[...]
        db = dx3v.astype(_MXU)
        dhf = jnp.zeros((tm, D), F32)
        for c in range(F // FCH_BWD):
            sl = slice(c * FCH_BWD, (c + 1) * FCH_BWD)
            gtv = gt_ref[:, sl].astype(F32)
            upv = up_ref[:, sl].astype(F32)
            dact = _nt(db, wd_ref[sl, :])
            sig = _sigmoid(gtv)
            silu = gtv * sig
            dup = (dact * silu).astype(dup_ref.dtype)
            dgt = (dact * upv * (sig * (1.0 + gtv * (1.0 - sig)))).astype(dgt_ref.dtype)
            act_ref[:, sl] = (silu * upv).astype(act_ref.dtype)
            dgt_ref[:, sl] = dgt
            dup_ref[:, sl] = dup
            dhf = dhf + _nn(dgt, wg_ref[sl, :]) + _nn(dup, wu_ref[sl, :])
        x2v = x2_ref[...]
        g = g_ref[...]
        _, r = _rms(x2v, g)
        dxn, dg = _rms_bwd(dhf, x2v, r, g)
        dx2 = dx3v + dxn
        dx2_ref[...] = dx2
        dx2b_ref[...] = dx2.astype(dx2b_ref.dtype)
        _acc_row(dg_ref, dg)

    return _hbm_call(
        body, name="ffn_bwd", grid=(S // tm,),
        in_specs=[_rows(tm, D), _rows(tm, D), _rows(tm, F), _rows(tm, F), _full((1, D)),
                  _once((F, D)), _once((F, D)), _once((F, D))],
        out_specs=[_rows(tm, D), _rows(tm, D), _rows(tm, F), _rows(tm, F), _rows(tm, F), _full((8, D))],
        out_shape=[_sds((S, D), F32), _sds((S, D), _MXU), _sds((S, F), _MXU), _sds((S, F), _MXU),
                   _sds((S, F), _MXU), _sds((8, D), F32)],
        compiler_params=_params(56),
    )(dx3, x2, gt, up, g_ffn, wd, wgT, wuT)


_AFTER = pl.BlockSpec(memory_space=pl.ANY)


def _wgrad(pairs, name, after, rc=256):
    S, R = pairs[0][0].shape
    N = pairs[0][1].shape[1]
    rights = []
    for _, b in pairs:
        if not any(b is r for r in rights):
            rights.append(b)
    which = [next(i for i, r in enumerate(rights) if r is b) for _, b in pairs]
    n, nb = len(pairs), len(rights)

    def body(*refs):
        for k in range(n):
            refs[n + nb + 1 + k][...] = _tn(refs[k][...], refs[n + which[k]][...]).astype(jnp.bfloat16)

    res = _hbm_call(
        body, name=name, grid=(R // rc,),
        in_specs=[pl.BlockSpec((S, rc), lambda j: (0, j))] * n + [_once((S, N))] * nb + [_AFTER],
        out_specs=[_rows(rc, N)] * n,
        out_shape=[_sds((R, N), jnp.bfloat16)] * n,
        compiler_params=_params(56),
    )(*[a for a, _ in pairs], *rights, after)
    return list(res)


def _mem_attn_bwd(dx2, x1, qm, km, vm, g_q, wq, wo, wout, y, ln_g, ln_b, after, tm=512):
    S = x1.shape[0]
    M = km.shape[0]

    def body(dx2_ref, x1_ref, qm_ref, km_ref, vm_ref, g_ref, wq_ref, wo_ref, wout_ref, y_ref, lg_ref, lb_ref,
             after_ref, dx1_ref, dx1b_ref, hq_ref, dqm_ref, dy_ref, dya_ref, dkm_ref, dvm_ref, dg_ref, dbo_ref,
             dlg_ref, dlb_ref, dbdw_ref):
        i = pl.program_id(0)
        _acc_init(i, dkm_ref, dvm_ref, dg_ref, dbo_ref, dlg_ref, dlb_ref, dbdw_ref)
        dx2v = dx2_ref[...]
        domt = _nt(wo_ref[...], dx2v).astype(_MXU)
        dqs = []
        for h in range(MH):
            sl = slice(h * MHD, (h + 1) * MHD)
            qh = qm_ref[:, sl]
            kh = km_ref[:, sl]
            e, inv = _softmax_t(_nt(kh, qh))
            p = e * inv
            dp = _nn(vm_ref[:, sl], domt[sl, :])
            ds = (p * (dp - jnp.sum(dp * p, axis=0, keepdims=True))).astype(_MXU)
            dqs.append(_tn(ds, kh) * MSCALE)
            dkm_ref[:, sl] += _nn(ds, qh)
            dvm_ref[:, sl] += _nt(p, domt[sl, :])
        dqm = jnp.concatenate(dqs, axis=1).astype(dqm_ref.dtype)
        dqm_ref[...] = dqm
        dhq = _nt(dqm, wq_ref[...])
        x1v = x1_ref[...]
        g = g_ref[...]
        hq, r = _rms(x1v, g)
        hq_ref[...] = hq.astype(hq_ref.dtype)
        dxn, dg = _rms_bwd(dhq, x1v, r, g)
        dx1 = dx2v + dxn
        dx1_ref[...] = dx1
        dx1b = dx1.astype(dx1b_ref.dtype)
        dx1b_ref[...] = dx1b
        dym = _nt(dx1b, wout_ref[...])
        dya_ref[...] = dym[:, CC:].astype(dya_ref.dtype)
        _acc_row(dg_ref, dg)
        _acc_row(dbo_ref, jnp.sum(dx1, axis=0, keepdims=True))
        lg = lg_ref[...]
        yh, rstd, z = _ln_silu(y_ref[...], lg, lb_ref[...])
        sig = _sigmoid(z)
        dz = dym[:, :CC] * (sig * (1.0 + z * (1.0 - sig)))
        dyh = dz * lg
        dy = rstd * (dyh - jnp.mean(dyh, axis=-1, keepdims=True) - yh * jnp.mean(dyh * yh, axis=-1, keepdims=True))
        dy_ref[...] = dy
        _acc_row(dlg_ref, jnp.sum(dz * yh, axis=0, keepdims=True))
        _acc_row(dlb_ref, jnp.sum(dz, axis=0, keepdims=True))
        _acc_row(dbdw_ref, jnp.sum(dy, axis=0, keepdims=True))

    return _hbm_call(
        body, name="mem_attn_bwd", grid=(S // tm,),
        in_specs=[_rows(tm, D), _rows(tm, D), _rows(tm, D), _full((M, D)), _full((M, D)), _full((1, D)),
                  _once((D, D)), _once((D, D)), _once((D, D)), _rows(tm, CC), _full((1, CC)), _full((1, CC)),
                  _AFTER],
        out_specs=[_rows(tm, D), _rows(tm, D), _rows(tm, D), _rows(tm, D), _rows(tm, CC), _rows(tm, ATT),
                   _full((M, D)), _full((M, D)), _full((8, D)), _full((8, D)),
                   _full((8, CC)), _full((8, CC)), _full((8, CC))],
        out_shape=[_sds((S, D), F32), _sds((S, D), _MXU), _sds((S, D), _MXU), _sds((S, D), _MXU),
                   _sds((S, CC), F32), _sds((S, ATT), _MXU),
                   _sds((M, D), F32), _sds((M, D), F32), _sds((8, D), F32), _sds((8, D), F32),
                   _sds((8, CC), F32), _sds((8, CC), F32), _sds((8, CC), F32)],
        compiler_params=_params(56),
    )(dx2, x1, qm, km, vm, g_q, wq, wo, wout, y, ln_g, ln_b, after)


def _mem_kv_bwd(dkm, dvm, mem, g_kv, memn, wkvT):
    M = mem.shape[0]

    def body(dkm_ref, dvm_ref, mem_ref, g_ref, mn_ref, w_ref, gw_ref, dg_ref):
        dkv = jnp.concatenate([dkm_ref[...], dvm_ref[...]], axis=1).astype(_MXU)
        gw_ref[...] = _tn(dkv, mn_ref[...]).astype(gw_ref.dtype)
        dmn = _nn(dkv, w_ref[...])
        memv = mem_ref[...]
        g = g_ref[...]
        _, r = _rms(memv, g)
        _, dg = _rms_bwd(dmn, memv, r, g)
        dg_ref[...] = jnp.broadcast_to(dg, dg_ref.shape)

    return _hbm_call(
        body, name="mem_kv_bwd", grid=(1,),
        in_specs=[_full((M, D)), _full((M, D)), _full((M, D)), _full((1, D)), _full((M, D)), _full((2 * D, D))],
        out_specs=[_full((2 * D, D)), _full((8, D))],
        out_shape=[_sds((2 * D, D), jnp.bfloat16), _sds((8, D), F32)],
        compiler_params=_params(48),
    )(dkm, dvm, mem, g_kv, memn, wkvT)


def _attn_bwd(qr, kr, vv, sink, dya, after):
    S = qr.shape[0]
    tq = ATTN_BLOCKS * BLK

    def body(q_ref, k_ref, v_ref, sk_ref, do_ref, after_ref, dq_ref, dk_ref, dv_ref, dsk_ref, sink_acc):
        _acc_init(pl.program_id(0), dk_ref, dv_ref, sink_acc)
        base = _attn_mask_base()
        for b in range(ATTN_BLOCKS):
            i = pl.program_id(0) * ATTN_BLOCKS + b
            rows = slice(b * BLK, (b + 1) * BLK)
            start = _attn_window(i, S)
            kb = k_ref[pl.ds(start, 3 * BLK), :]
            vb = v_ref[pl.ds(start, 3 * BLK), :]
            q = q_ref[rows, :]
            do = do_ref[rows, :]
            mask = _attn_mask(base, i, start)
            dqs, dks, dvs = [], [], []
            for hk in range(NKV):
                hs = slice(hk * HD, (hk + 1) * HD)
                qs = _stack_heads(q, hk)
                dos = _stack_heads(do, hk)
                e, inv, es = _attn_probs(qs, kb[:, hs], sk_ref, hk, mask)
                p = e * inv
                dp = _nt(vb[:, hs], dos)
                delta = jnp.sum(dp * p, axis=0, keepdims=True)
                ds = (p * (dp - delta)).astype(_MXU)
                dqs.append(_unstack_heads_t(_tn(kb[:, hs], ds) * SCALE))
                dks.append(_nn(ds, qs))
                dvs.append(_nn(p, dos))
                sink_acc[hk:hk + 1, :] -= es * inv * delta
            dq_ref[rows, :] = jnp.concatenate(dqs, axis=1)
            dk_ref[pl.ds(start, 3 * BLK), :] += jnp.concatenate(dks, axis=1)
            dv_ref[pl.ds(start, 3 * BLK), :] += jnp.concatenate(dvs, axis=1)

        @pl.when(pl.program_id(0) == pl.num_programs(0) - 1)
        def _():
            for h in range(NQ):
                tot = jnp.sum(sink_acc[h // GRP:h // GRP + 1, (h % GRP) * BLK:(h % GRP + 1) * BLK], axis=1,
                              keepdims=True)
                dsk_ref[h:h + 1, :] = jnp.broadcast_to(tot, (1, LANES))

    return _hbm_call(
        body, name="attn_bwd", grid=(S // tq,),
        in_specs=[_rows(tq, ATT), _full((S, KVC)), _full((S, KVC)), _full((1, NQ)), _rows(tq, ATT), _AFTER],
        out_specs=[_rows(tq, ATT), _full((S, KVC)), _full((S, KVC)), _full((8, LANES))],
        out_shape=[_sds((S, ATT), F32), _sds((S, KVC), F32), _sds((S, KVC), F32), _sds((8, LANES), F32)],
        scratch_shapes=[pltpu.VMEM((8, ATTN_COLS), F32)],
        compiler_params=_params(40),
    )(qr, kr, vv, sink, dya, after)


def _conv_bwd(dy, vg, uglu, w_dw, after):
    S = dy.shape[0]
    tm = min(CONV_TILE, S)

    def body(dy_hbm, vg_hbm, uglu_ref, w_ref, after_ref, du_ref, gw_ref, dbin_ref, dyp_ref, vgp_ref, gacc_ref, sh_ref,
             dvg_ref, sems):
        i = pl.program_id(0)
        _acc_init(i, gacc_ref, dbin_ref)
        _fill_padded(i, dy_hbm, dyp_ref, sems.at[0], S)
        _fill_padded(i, vg_hbm, vgp_ref, sems.at[1], S)
        t0 = pl.multiple_of(i * tm, tm)
        _shifted_copies(dyp_ref, sh_ref.at[0], t0, tm)
        _shifted_copies(vgp_ref, sh_ref.at[1], t0, tm)
        _dwconv_tile(sh_ref.at[0], tm, w_ref, True, dvg_ref)
        dvg = dvg_ref[...]
        sub = CONV_ROWS
        for cl in range(CC // LANES):
            lanes = slice(cl * LANES, (cl + 1) * LANES)
            for k, b, off in _conv_taps():
                part = jnp.zeros((8, LANES), F32)
                for r in range(0, tm, sub):
                    prod = (sh_ref[0, cl, 0, CONV_HALO + r:CONV_HALO + r + sub, :]
                            * sh_ref[1, cl, b, off + r:off + r + sub, :])
                    part = part + jnp.sum(prod.reshape(sub // 8, 8, LANES), axis=0)
                gacc_ref[k, :, lanes] += part
        a_v = uglu_ref[:, :CC].astype(F32)
        sg = _sigmoid(uglu_ref[:, CC:].astype(F32))
        da = dvg * sg
        dgate = dvg * a_v * sg * (1.0 - sg)
        du = jnp.concatenate([da, dgate], axis=1)
        du_ref[...] = du.astype(du_ref.dtype)
        _acc_row(dbin_ref, jnp.sum(du, axis=0, keepdims=True))

        @pl.when(i == pl.num_programs(0) - 1)
        def _():
            gw_ref[...] = jnp.zeros(gw_ref.shape, F32)
            for k in range(TAPS):
                gw_ref[k:k + 1, :] = jnp.sum(gacc_ref[k], axis=0, keepdims=True)

    return _hbm_call(
        body, name="conv_bwd", grid=(S // tm,),
        in_specs=[pl.BlockSpec(memory_space=pl.ANY), pl.BlockSpec(memory_space=pl.ANY), _rows(tm, 2 * CC),
                  _full((TAPS, CC)), _AFTER],
        out_specs=[_rows(tm, 2 * CC), _full((TAPS + 1, CC)), _full((8, 2 * CC))],
        out_shape=[_sds((S, 2 * CC), _MXU), _sds((TAPS + 1, CC), F32), _sds((8, 2 * CC), F32)],
        scratch_shapes=[pltpu.VMEM((S + 2 * CONV_HALO, CC), F32), pltpu.VMEM((S + 2 * CONV_HALO, CC), F32),
                        pltpu.VMEM((TAPS, 8, CC), F32), _shift_scratch(tm, 2), pltpu.VMEM((tm, CC), F32),
                        pltpu.SemaphoreType.DMA((2,))],
        compiler_params=_params(56),
    )(dy, vg, uglu, w_dw, after)


def _in_proj_bwd(duglu, dqr, dkr, dv, cosq, sinq, x, dx1, g_mix, winT, tm=512):
    S = x.shape[0]
    NA = ATT + 2 * KVC

    def body(dug_ref, dq_ref, dk_ref, dv_ref, c_ref, s_ref, x_ref, dx1_ref, g_ref, w_ref,
             dx_ref, du_ref, dg_ref, dbin_ref):
        i = pl.program_id(0)
        _acc_init(i, dg_ref, dbin_ref)
        c = c_ref[...]
        s = s_ref[...]
        datt = jnp.concatenate([_rope_bwd(dq_ref[...], c, s), _rope_bwd(dk_ref[...], c, s), dv_ref[...]], axis=1)
        _acc_row(dbin_ref, jnp.sum(datt, axis=0, keepdims=True))
        du = jnp.concatenate([dug_ref[...], datt.astype(du_ref.dtype)], axis=1)
        du_ref[...] = du
        dh = _nn(du, w_ref[...])
        xv = x_ref[...]
        g = g_ref[...]
        _, r = _rms(xv, g)
        dxn, dg = _rms_bwd(dh, xv, r, g)
        dx_ref[...] = dx1_ref[...] + dxn
        _acc_row(dg_ref, dg)

    return _hbm_call(
        body, name="in_proj_bwd", grid=(S // tm,),
        in_specs=[_rows(tm, 2 * CC), _rows(tm, ATT), _rows(tm, KVC), _rows(tm, KVC), _rows(tm, LANES),
                  _rows(tm, LANES), _rows(tm, D), _rows(tm, D), _full((1, D)), _once((INC, D))],
        out_specs=[_rows(tm, D), _rows(tm, INC), _full((8, D)), _full((8, NA))],
        out_shape=[_sds((S, D), F32), _sds((S, INC), _MXU), _sds((8, D), F32), _sds((8, NA), F32)],
        compiler_params=_params(48),
    )(duglu, dqr, dkr, dv, cosq, sinq, x, dx1, g_mix, winT)


def _rope_tables(S):
    pos = jnp.arange(S, dtype=F32)
    inv_freq = ROPE_THETA ** (-jnp.arange(0, HD, 2, dtype=F32) / HD)
    ang = pos[:, None] * inv_freq[None, :]
    cos, sin = jnp.cos(ang), jnp.sin(ang)
    cosq = jnp.tile(jnp.concatenate([cos, cos], axis=1), (1, LANES // HD))
    sinq = jnp.tile(jnp.concatenate([-sin, sin], axis=1), (1, LANES // HD))
    return cosq, sinq


GATHER_GROUPS = (("w_inT", "w_dw"), ("w_out", "w_mem_kvT", "w_mem_q", "w_mem_o"), ("w_gateT", "w_upT"), ("w_down",))
ID_GATHER, ID_SIBLING, ID_CHIP = 1, 5, 8


def _local_step(x, mem, tgt, sp, gather, reduce):
    S = x.shape[0]
    assert S % (ATTN_BLOCKS * BLK) == 0 and S % 512 == 0, S
    cosq, sinq = _rope_tables(S)

    gather.forward(0, cosq)
    w0 = gather.finish(0, sinq)
    winT, w_dw_full = w0["w_inT"], w0["w_dw"]
    h, uglu, vg, qr, kr, vv = _in_proj_fwd(x, sp["g_mix"], winT, sp["b_in"], cosq, sinq)
    ya = _attn_fwd(qr, kr, vv, sp["attn_sink"])
    gather.forward(1, ya)
    y, yc = _conv_fwd(vg, w_dw_full, sp["b_dw"], sp["g_conv_ln"], sp["b_conv_ln"])
    w1 = gather.finish(1, yc)
    wout, wkvT, wq, wo = w1["w_out"], w1["w_mem_kvT"], w1["w_mem_q"], w1["w_mem_o"]
    km, vm, memn = _mem_kv_fwd(mem, sp["g_mem_kv"], wkvT)
    x1, ymix = _out_proj_fwd(x, yc, ya, wout, sp["b_out"])
    token = gather.forward(2, x1)
    x2, qm, om = _mem_attn_fwd(x1, sp["g_mem_q"], wq, km, vm, wo, token)
    w2 = gather.finish(2, om)
    gather.forward(3, om)
    wgT, wuT, wd = w2["w_gateT"], w2["w_upT"], gather.finish(3, om)["w_down"]
    gt, up, hf, dx3, dx3b, loss_acc, dg_final = _ffn_loss_fwd(x2, tgt, sp["g_ffn"], sp["g_final"], wgT, wuT, wd)

    dx2, dx2b, dgt, dup, act, dg_ffn = _ffn_bwd(dx3, x2, gt, up, sp["g_ffn"], wd, wgT, wuT)
    token = reduce.start(0, _wgrad([(dgt, hf), (dup, hf), (act, dx3b)], "wgrad_ffn", dx2))

    dx1, dx1b, hq, dqm, dy, dya, dkm, dvm, dg_mem_q, db_out, dg_cln, db_cln, db_dw = _mem_attn_bwd(
        dx2, x1, qm, km, vm, sp["g_mem_q"], wq, wo, wout, y, sp["g_conv_ln"], sp["b_conv_ln"], token)
    token = reduce.middle(0, dx1)
    g_wout, g_wq, g_wo = _wgrad([(ymix, dx1b), (hq, dqm), (om, dx2b)], "wgrad_mix_mem", token)
    g_wkvT, dg_mem_kv = _mem_kv_bwd(dkm, dvm, mem, sp["g_mem_kv"], memn, wkvT)
    token = reduce.start(1, [g_wout, g_wq, g_wkvT, g_wo])

    dqr, dkr, dvv, dsink = _attn_bwd(qr, kr, vv, sp["attn_sink"], dya, token)
    token = reduce.middle(1, dqr)
    duglu, g_wdw, dbin_glu = _conv_bwd(dy, vg, uglu, w_dw_full, token)
    grad_x, du, dg_mix, dbin_att = _in_proj_bwd(duglu, dqr, dkr, dvv, cosq, sinq, x, dx1, sp["g_mix"], winT)
    token = reduce.start(2, _wgrad([(du, h)], "wgrad_in", grad_x))

    small = {
        "loss": loss_acc[0:1, :],
        "g_mix": dg_mix[0:1], "b_in": jnp.concatenate([dbin_glu[0:1], dbin_att[0:1]], axis=1),
        "b_dw": db_dw[0:1], "g_conv_ln": dg_cln[0:1], "b_conv_ln": db_cln[0:1],
        "attn_sink": dsink[:, 0].reshape(1, NQ), "b_out": db_out[0:1], "g_mem_q": dg_mem_q[0:1],
        "g_mem_kv": dg_mem_kv[0:1], "g_ffn": dg_ffn[0:1], "g_final": dg_final[0:1],
    }
    token = reduce.middle(2, token, reduce.small_zones(small, g_wdw))
    reduce.finish(0, token)
    reduce.finish(1, token)
    return grad_x


def _place():
    return lax.axis_index("x"), lax.axis_index("y"), lax.axis_index("c")


_HBM = pl.BlockSpec(memory_space=pltpu.HBM)
_SEM = pl.BlockSpec(memory_space=pltpu.SEMAPHORE)
_ANY = pl.BlockSpec(memory_space=pl.ANY)
_EFFECT = pltpu.SideEffectType.DATAFLOW_SIDE_EFFECTING
_VMEM = pl.BlockSpec(memory_space=pltpu.VMEM)
_TOKEN = jax.ShapeDtypeStruct((8, LANES), F32)


def _dev_index(p):
    return 4 * p[0] + 2 * p[1] + p[2]


def _landing_zones(shards, me_idx):
    def body(m_ref, *refs):
        n = len(refs) // 2
        for k in range(n):
            refs[n + k][...] = refs[k][...].astype(jnp.bfloat16)

    return pl.pallas_call(
        body, name="landing_zones",
        grid_spec=pltpu.PrefetchScalarGridSpec(
            num_scalar_prefetch=1, grid=(1,),
            in_specs=[pl.BlockSpec(s.shape, lambda i, m: (0, 0)) for s in shards],
            out_specs=[pl.BlockSpec((None,) + s.shape, lambda i, m: (m[0], 0, 0)) for s in shards]),
        out_shape=[pltpu.HBM((NDEV,) + s.shape, jnp.bfloat16) for s in shards],
        compiler_params=_params(40),
    )(me_idx, *shards)


def _gather_start(groups):
    sizes = [len(g) for g in groups]
    flat = [z for g in groups for z in g]
    n = len(flat)

    def body(*refs):
        ins = refs[:n]
        sems = refs[2 * n:]
        x, y, c = _place()
        me = _dev_index((x, y, c))
        peers = [(x, y, 1 - c), (1 - x, y, c), (x, 1 - y, c), (1 - x, 1 - y, c)]
        base = 0
        for gi, size in enumerate(sizes):
            send_sems, recv_sems = sems[2 * gi], sems[2 * gi + 1]
            for k in range(size):
                zone = ins[base + k]
                for j, peer in enumerate(peers):
                    pltpu.make_async_remote_copy(
                        src_ref=zone.at[me], dst_ref=zone.at[me], send_sem=send_sems.at[4 * k + j],
                        recv_sem=recv_sems.at[4 * k + j], device_id=peer, device_id_type=MESH).start()
            base += size

    out_shape = [pltpu.HBM(z.shape, z.dtype) for z in flat]
    out_specs = [_HBM] * n
    for size in sizes:
        out_shape += [pltpu.SemaphoreType.DMA((4 * size,))] * 2
        out_specs += [_SEM, _SEM]
    res = pl.pallas_call(
        body, name="gather_start", in_specs=[_HBM] * n, out_specs=out_specs, out_shape=out_shape,
        input_output_aliases={k: k for k in range(n)},
        compiler_params=pltpu.CompilerParams(has_side_effects=_EFFECT),
    )(*[pltpu.with_memory_space_constraint(z, pltpu.HBM) for z in flat])
    out, base = [], 0
    for gi, size in enumerate(sizes):
        out.append((list(res[base:base + size]), res[n + 2 * gi], res[n + 2 * gi + 1]))
        base += size
    return out


def _gather_forward(zones, send_sems, recv_sems, after, collective_id, name):
    n = len(zones)

    def body(*refs):
        ins = refs[:n]
        send_ref, recv_ref = refs[n], refs[n + 1]
        fsend, frecv = refs[2 * n + 3], refs[2 * n + 4]
        x, y, c = _place()
        sibling = (x, y, 1 - c)
        barrier = pltpu.get_barrier_semaphore()
        pl.semaphore_signal(barrier, inc=1, device_id=sibling, device_id_type=MESH)
        pl.semaphore_wait(barrier, 1)
        chips = [(1 - x, y), (x, 1 - y), (1 - x, 1 - y)]
        for j, chip in enumerate(chips):
            for k in range(n):
                slot = ins[k].at[_dev_index((*chip, c))]
                pltpu.make_async_remote_copy(
                    src_ref=slot, dst_ref=slot, send_sem=send_ref.at[4 * k + 1 + j],
                    recv_sem=recv_ref.at[4 * k + 1 + j], device_id=sibling, device_id_type=MESH).wait_recv()
                pltpu.make_async_remote_copy(
                    src_ref=slot, dst_ref=slot, send_sem=fsend.at[3 * k + j], recv_sem=frecv.at[3 * k + j],
                    device_id=sibling, device_id_type=MESH).start()
        for k in range(n):
            own = ins[k].at[_dev_index((x, y, c))]
            theirs = ins[k].at[_dev_index(sibling)]
            pltpu.make_async_remote_copy(
                src_ref=theirs, dst_ref=theirs, send_sem=send_ref.at[4 * k], recv_sem=recv_ref.at[4 * k],
                device_id=sibling, device_id_type=MESH).wait_recv()
            for j in range(4):
                pltpu.make_async_remote_copy(
                    src_ref=own, dst_ref=own, send_sem=send_ref.at[4 * k + j], recv_sem=recv_ref.at[4 * k + j],
                    device_id=sibling, device_id_type=MESH).wait_send()
        refs[2 * n + 5][...] = jnp.zeros(_TOKEN.shape, _TOKEN.dtype)

    res = pl.pallas_call(
        body, name=name, in_specs=[_HBM] * n + [_SEM, _SEM, _ANY],
        out_specs=[_HBM] * n + [_SEM, _SEM, _VMEM],
        out_shape=[pltpu.HBM(z.shape, z.dtype) for z in zones] + [pltpu.SemaphoreType.DMA((3 * n,))] * 2 + [_TOKEN],
        input_output_aliases={k: k for k in range(n)},
        compiler_params=pltpu.CompilerParams(has_side_effects=_EFFECT, collective_id=collective_id),
    )(*zones, send_sems, recv_sems, after)
    return (list(res[:n]), res[n], res[n + 1]), res[n + 2]


def _gather_finish(zones, fsend, frecv, after, name):
    n = len(zones)

    def body(*refs):
        ins = refs[:n]
        fsend_ref, frecv_ref = refs[n], refs[n + 1]
        x, y, c = _place()
        sibling = (x, y, 1 - c)
        chips = [(1 - x, y), (x, 1 - y), (1 - x, 1 - y)]
        for j, chip in enumerate(chips):
            for k in range(n):
                mine = ins[k].at[_dev_index((*chip, c))]
                theirs = ins[k].at[_dev_index((*chip, 1 - c))]
                pltpu.make_async_remote_copy(
                    src_ref=theirs, dst_ref=theirs, send_sem=fsend_ref.at[3 * k + j],
                    recv_sem=frecv_ref.at[3 * k + j], device_id=sibling, device_id_type=MESH).wait_recv()
                pltpu.make_async_remote_copy(
                    src_ref=mine, dst_ref=mine, send_sem=fsend_ref.at[3 * k + j],
                    recv_sem=frecv_ref.at[3 * k + j], device_id=sibling, device_id_type=MESH).wait_send()

    res = pl.pallas_call(
        body, name=name, in_specs=[_HBM] * n + [_SEM, _SEM, _ANY], out_specs=[_HBM] * n,
        out_shape=[pltpu.HBM(z.shape, z.dtype) for z in zones],
        input_output_aliases={k: k for k in range(n)},
        compiler_params=pltpu.CompilerParams(has_side_effects=_EFFECT),
    )(*zones, fsend, frecv, after)
    return list(res)


class _Gather:
    def __init__(self, started):
        self._phase = list(started)

    def forward(self, gi, after):
        zones, send_sems, recv_sems = self._phase[gi]
        self._phase[gi], token = _gather_forward(zones, send_sems, recv_sems, after, ID_GATHER + gi,
                                                  "gather_forward_%d" % gi)
        return token

    def finish(self, gi, after):
        zones, fsend, frecv = self._phase[gi]
        out = {}
        for name, z in zip(GATHER_GROUPS[gi], _gather_finish(zones, fsend, frecv, after, "gather_finish_%d" % gi)):
            if name == "w_dw":
                out[name] = jnp.transpose(z, (1, 0, 2)).reshape(TAPS + 1, CC)[:TAPS]
            else:
                out[name] = z.reshape(NDEV * z.shape[1], D)
        return out


def _handshake(peers):
    barrier = pltpu.get_barrier_semaphore()
    for peer in peers:
        pl.semaphore_signal(barrier, inc=1, device_id=peer, device_id_type=MESH)
    pl.semaphore_wait(barrier, len(peers))


def _sibling_copies(srcs, lands, send_sems, recv_sems):
    x, y, c = _place()
    return [pltpu.make_async_remote_copy(
        src_ref=srcs[k].at[2 * s + (1 - c)], dst_ref=lands[k].at[s], send_sem=send_sems.at[4 * k + s],
        recv_sem=recv_sems.at[4 * k + s], device_id=(x, y, 1 - c), device_id_type=MESH)
        for k in range(len(srcs)) for s in range(4)]


def _sibling_start(gbs, collective_id, name):
    n = len(gbs)
    lands = [lax.empty((4,) + g.shape[1:], g.dtype) for g in gbs]

    def body(*refs):
        x, y, c = _place()
        _handshake([(x, y, 1 - c)])
        for cp in _sibling_copies(refs[:n], refs[n:2 * n], refs[4 * n], refs[4 * n + 1]):
            cp.start()
        refs[4 * n + 2][...] = jnp.zeros(_TOKEN.shape, _TOKEN.dtype)

    res = pl.pallas_call(
        body, name=name, in_specs=[_HBM] * (2 * n), out_specs=[_HBM] * (2 * n) + [_SEM, _SEM, _VMEM],
        out_shape=[pltpu.HBM(a.shape, a.dtype) for a in gbs + lands] + [pltpu.SemaphoreType.DMA((4 * n,))] * 2
        + [_TOKEN],
        input_output_aliases={k: k for k in range(2 * n)},
        compiler_params=pltpu.CompilerParams(has_side_effects=_EFFECT, collective_id=collective_id),
    )(*[pltpu.with_memory_space_constraint(a, pltpu.HBM) for a in gbs + lands])
    return list(res[:n]), list(res[n:2 * n]), res[2 * n], res[2 * n + 1], res[2 * n + 2]


def _sibling_wait(gbs, lands, send_sems, recv_sems, after, name):
    n = len(gbs)

    def body(*refs):
        for cp in _sibling_copies(refs[:n], refs[n:2 * n], refs[2 * n], refs[2 * n + 1]):
            cp.wait_recv()
            cp.wait_send()

    res = pl.pallas_call(
        body, name=name, in_specs=[_HBM] * (2 * n) + [_SEM, _SEM, _ANY], out_specs=[_HBM] * (2 * n),
        out_shape=[pltpu.HBM(a.shape, a.dtype) for a in gbs + lands],
        input_output_aliases={k: k for k in range(2 * n)},
        compiler_params=pltpu.CompilerParams(has_side_effects=_EFFECT),
    )(*gbs, *lands, send_sems, recv_sems, after)
    return list(res[:n]), list(res[n:])


def _pair_sum(gfs, ts, c_idx, name):
    n = len(gfs)
    halves = 1

    def body(c_ref, *refs):
        for k in range(n):
            refs[2 * n + k][...] = (refs[k][...].astype(F32) + refs[n + k][...].astype(F32)).astype(jnp.bfloat16)

    in_specs, out_specs, out_shape = [], [], []
    for g in gfs:
        rh = g.shape[1] // halves
        in_specs.append(pl.BlockSpec((None, rh, D), lambda s, h, c_ref: (2 * s + c_ref[0], h, 0)))
    for g in gfs:
        rh = g.shape[1] // halves
        in_specs.append(pl.BlockSpec((None, rh, D), lambda s, h, c_ref: (s, h, 0)))
        out_specs.append(pl.BlockSpec((None, rh, D), lambda s, h, c_ref: (s, h, 0)))
        out_shape.append(_sds((4, g.shape[1], D), jnp.bfloat16))
    return pl.pallas_call(
        body, name=name,
        grid_spec=pltpu.PrefetchScalarGridSpec(num_scalar_prefetch=1, grid=(4, halves), in_specs=in_specs,
                                               out_specs=out_specs),
        out_shape=out_shape,
        compiler_params=pltpu.CompilerParams(dimension_semantics=("arbitrary", "arbitrary"),
                                             vmem_limit_bytes=48 * 2 ** 20),
    )(c_idx, *gfs, *ts)


def _chip_copies(pbs, lands, smalls, send_sems, recv_sems):
    x, y, c = _place()
    own = 2 * x + y
    me = _dev_index((x, y, c))
    chips = [(1 - x, y), (x, 1 - y), (1 - x, 1 - y)]
    n = len(pbs)
    copies = [pltpu.make_async_remote_copy(
        src_ref=pbs[k].at[2 * chip[0] + chip[1]], dst_ref=lands[k].at[own], send_sem=send_sems.at[3 * k + j],
        recv_sem=recv_sems.at[3 * k + j], device_id=(*chip, c), device_id_type=MESH)
        for k in range(n) for j, chip in enumerate(chips)]
    peers = [(x, y, 1 - c)] + [(*chip, cc) for chip in chips for cc in (c, 1 - c)]
    copies += [pltpu.make_async_remote_copy(
        src_ref=smalls[t].at[me], dst_ref=smalls[t].at[me], send_sem=send_sems.at[3 * n + 7 * t + j],
        recv_sem=recv_sems.at[3 * n + 7 * t + j], device_id=peer, device_id_type=MESH)
        for t in range(len(smalls)) for j, peer in enumerate(peers)]
    return copies


def _chip_start(pbs, smalls, collective_id, name):
    n, m = len(pbs), len(smalls)
    lands = [lax.empty(p.shape, p.dtype) for p in pbs]
    arrays = list(pbs) + lands + list(smalls)
    na = len(arrays)
    nsem = 3 * n + 7 * m

    def body(*refs):
        x, y, c = _place()
        chips = [(1 - x, y), (x, 1 - y), (1 - x, 1 - y)]
        peers = [(*chip, c) for chip in chips]
        if m:
            peers += [(x, y, 1 - c)] + [(*chip, 1 - c) for chip in chips]
        _handshake(peers)
        for cp in _chip_copies(refs[:n], refs[n:2 * n], refs[2 * n:na], refs[2 * na], refs[2 * na + 1]):
            cp.start()
        refs[2 * na + 2][...] = jnp.zeros(_TOKEN.shape, _TOKEN.dtype)

    res = pl.pallas_call(
        body, name=name, in_specs=[_HBM] * na, out_specs=[_HBM] * na + [_SEM, _SEM, _VMEM],
        out_shape=[pltpu.HBM(a.shape, a.dtype) for a in arrays] + [pltpu.SemaphoreType.DMA((nsem,))] * 2 + [_TOKEN],
        input_output_aliases={k: k for k in range(na)},
        compiler_params=pltpu.CompilerParams(has_side_effects=_EFFECT, collective_id=collective_id),
    )(*[pltpu.with_memory_space_constraint(a, pltpu.HBM) for a in arrays])
    return (list(res[:n]), list(res[n:2 * n]), list(res[2 * n:na]), res[na], res[na + 1]), res[na + 2]


def _chip_wait(pbs, lands, smalls, send_sems, recv_sems, after, name):
    n = len(pbs)
    arrays = list(pbs) + list(lands) + list(smalls)
    na = len(arrays)

    def body(*refs):
        for cp in _chip_copies(refs[:n], refs[n:2 * n], refs[2 * n:na], refs[na], refs[na + 1]):
            cp.wait_recv()
            cp.wait_send()

    res = pl.pallas_call(
        body, name=name, in_specs=[_HBM] * na + [_SEM, _SEM, _ANY], out_specs=[_HBM] * na,
        out_shape=[pltpu.HBM(a.shape, a.dtype) for a in arrays],
        input_output_aliases={k: k for k in range(na)},
        compiler_params=pltpu.CompilerParams(has_side_effects=_EFFECT),
    )(*arrays, send_sems, recv_sems, after)
    return list(res[:n]), list(res[n:2 * n]), list(res[2 * n:])


def _chip_sum(pbs, qs, own_idx, states, name):
    n = len(pbs)
    halves = 2
    n_in = 4 * n + 3 * sum(s is not None for s in states)

    def body(o_ref, *refs):
        nxt_in, nxt_out = 4 * n, n_in
        for k in range(n):
            acc = refs[4 * k][...].astype(F32)
            for j in range(1, 4):
                acc = acc + refs[4 * k + j][...].astype(F32)
            refs[nxt_out][...] = acc
            nxt_out += 1
            if states[k] is not None:
                w_ref, m_ref, v_ref = refs[nxt_in:nxt_in + 3]
                nxt_in += 3
                d, nm, nv = _adamw_math(w_ref[...], acc, m_ref[...], v_ref[...])
                refs[nxt_out][...] = d
                refs[nxt_out + 1][...] = nm
                refs[nxt_out + 2][...] = nv
                nxt_out += 3

    def other(j):
        return lambda h, o: ((o[0] + j) % 4, h, 0)

    def rows(r):
        return pl.BlockSpec((r // halves, D), lambda h, o: (h, 0))

    in_specs, ins = [], []
    for p, q in zip(pbs, qs):
        rh = p.shape[1] // halves
        in_specs.append(pl.BlockSpec((None, rh, D), lambda h, o: (o[0], h, 0)))
        ins.append(p)
        for j in range(1, 4):
            in_specs.append(pl.BlockSpec((None, rh, D), other(j)))
            ins.append(q)
    out_specs, out_shape, counts = [], [], []
    for p, s in zip(pbs, states):
        r = p.shape[1]
        if s is not None:
            in_specs += [rows(r)] * 3
            ins += list(s)
        counts.append(1 if s is None else 4)
        out_specs += [rows(r)] * counts[-1]
        out_shape += [_sds((r, D), F32)] * counts[-1]
    res = pl.pallas_call(
        body, name=name,
        grid_spec=pltpu.PrefetchScalarGridSpec(num_scalar_prefetch=1, grid=(halves,), in_specs=in_specs,
                                               out_specs=out_specs),
        out_shape=out_shape,
        compiler_params=pltpu.CompilerParams(dimension_semantics=("arbitrary",), vmem_limit_bytes=56 * 2 ** 20),
    )(own_idx, *ins)
    out, base = [], 0
    for c in counts:
        out.append(tuple(res[base:base + c]))
        base += c
    return out


REDUCE_GROUPS = (("w_gateT", "w_upT", "w_down"), ("w_out", "w_mem_q", "w_mem_kvT", "w_mem_o"), ("w_inT",))


class _Reduce:
    def __init__(self, states):
        x, y, c = _place()
        self._c = c.astype(jnp.int32).reshape(1)
        self._own = (2 * x + y).astype(jnp.int32).reshape(1)
        self._me = _dev_index((x, y, c))
        self._state = {}
        self._adam = states
        self.results = {}
        self.smalls = None

    def start(self, gi, wgrads):
        gbs = [g.reshape(NDEV, g.shape[0] // NDEV, D) for g in wgrads]
        gbs, lands, send_sems, recv_sems, token = _sibling_start(gbs, ID_SIBLING + gi, "reduce_sibling_start_%d" % gi)
        self._state[gi] = (gbs, lands, send_sems, recv_sems)
        return token

    def middle(self, gi, after, smalls=()):
        gbs, lands, send_sems, recv_sems = self._state[gi]
        gbs, ts = _sibling_wait(gbs, lands, send_sems, recv_sems, after, "reduce_sibling_wait_%d" % gi)
        pbs = _pair_sum(gbs, ts, self._c, "reduce_pair_sum_%d" % gi)
        self._state[gi], token = _chip_start(pbs, list(smalls), ID_CHIP + gi, "reduce_chip_start_%d" % gi)
        return token

    def finish(self, gi, after):
        pbs, lands, smalls, send_sems, recv_sems = self._state[gi]
        pbs, qs, smalls = _chip_wait(pbs, lands, smalls, send_sems, recv_sems, after, "reduce_chip_wait_%d" % gi)
        names = REDUCE_GROUPS[gi]
        states = [self._adam.get(n) for n in names]
        self.results.update(zip(names, _chip_sum(pbs, qs, self._own, states, "reduce_update_%d" % gi)))
        if smalls:
            self.smalls = smalls

    def small_zones(self, small, g_wdw):
        pack = jnp.concatenate(
            [jnp.pad(small[name], ((0, 0), (0, lanes - small[name].shape[1]))) for name, lanes, _ in PACK], axis=1)
        zones = []
        for a in (pack, g_wdw):
            zone = lax.empty((NDEV,) + a.shape, a.dtype)
            zones.append(lax.dynamic_update_slice(zone, a[None], (self._me,) + (0,) * a.ndim))
        return zones


def _adamw_math(w, g, m, v):
    m = ADAM_B1 * m + (1.0 - ADAM_B1) * g
    v = ADAM_B2 * v + (1.0 - ADAM_B2) * (g * g)
    m_hat = m / (1.0 - ADAM_B1 ** ADAM_STEP)
    v_hat = v / (1.0 - ADAM_B2 ** ADAM_STEP)
    delta = -ADAM_LR * (m_hat / (jnp.sqrt(v_hat) + ADAM_EPS) + ADAM_WD * w)
    return delta, m, v


def _adamw(w, g, m, v, name):
    R, C = w.shape
    rc = 256 if R % 256 == 0 and R > 256 else R

    def body(w_ref, g_ref, m_ref, v_ref, d_ref, nm_ref, nv_ref):
        d, nm, nv = _adamw_math(w_ref[...], g_ref[...], m_ref[...], v_ref[...])
        d_ref[...] = d
        nm_ref[...] = nm
        nv_ref[...] = nv

    return _hbm_call(
        body, name=name, grid=(R // rc,),
        in_specs=[_rows(rc, C)] * 4, out_specs=[_rows(rc, C)] * 3,
        out_shape=[_sds((R, C), F32)] * 3,
        compiler_params=_params(40),
    )(w, g, m, v)


PACK = (("loss", LANES, 1), ("g_mix", D, D), ("b_in", INC, INC), ("b_dw", CC, CC), ("g_conv_ln", CC, CC),
        ("b_conv_ln", CC, CC), ("attn_sink", LANES, NQ), ("b_out", D, D), ("g_mem_q", D, D), ("g_mem_kv", D, D),
        ("g_ffn", D, D), ("g_final", D, D))
PACK_LANES = sum(p[1] for p in PACK)


def _small_update(packs, gdws, ws, ms, vs):
    names = [p[0] for p in PACK[1:]]
    n = len(names)

    def body(*refs):
        packs_ref, gdws_ref = refs[0], refs[1]
        w_refs, m_refs, v_refs = refs[2:2 + n], refs[2 + n:2 + 2 * n], refs[2 + 2 * n:2 + 3 * n]
        outs = refs[2 + 3 * n:]
        tot = packs_ref[0]
        gdw = gdws_ref[0]
        for s in range(1, NDEV):
            tot = tot + packs_ref[s]
            gdw = gdw + gdws_ref[s]
        outs[0][...] = tot[:, 0:LANES]
        outs[1][...] = gdw
        off = LANES
        for k, (_, lanes, used) in enumerate(PACK[1:]):
            g = tot[:, off:off + used]
            off += lanes
            d, nm, nv = _adamw_math(w_refs[k][...], g, m_refs[k][...], v_refs[k][...])
            o = outs[2 + 4 * k:6 + 4 * k]
            o[0][...] = g
            o[1][...] = d
            o[2][...] = nm
            o[3][...] = nv

    def full(a):
        nd = len(a.shape)
        return pl.BlockSpec(a.shape, lambda i: (0,) * nd)

    ins = [packs, gdws, *ws, *ms, *vs]
    out_shape = [_sds((1, LANES), F32), _sds(gdws.shape[1:], F32)]
    for w in ws:
        out_shape += [_sds(w.shape, F32)] * 4
    return pl.pallas_call(
        body, name="small_update", grid=(1,),
        in_specs=[full(a) for a in ins], out_specs=[full(o) for o in out_shape], out_shape=out_shape,
        compiler_params=_params(40),
    )(*ins)


def kernel(x, mem, g_mix, w_in, b_in, w_dw, b_dw, g_conv_ln, b_conv_ln, attn_sink, w_out, b_out, g_mem_q, g_mem_kv, w_mem_q, w_mem_kv, w_mem_o, g_ffn, w_gate, w_up, w_down, g_final, loss_target, m_g_mix, m_w_in, m_b_in, m_w_dw, m_b_dw, m_g_conv_ln, m_b_conv_ln, m_attn_sink, m_w_out, m_b_out, m_g_mem_q, m_g_mem_kv, m_w_mem_q, m_w_mem_kv, m_w_mem_o, m_g_ffn, m_w_gate, m_w_up, m_w_down, m_g_final, v_g_mix, v_w_in, v_b_in, v_w_dw, v_b_dw, v_g_conv_ln, v_b_conv_ln, v_attn_sink, v_w_out, v_b_out, v_g_mem_q, v_g_mem_kv, v_w_mem_q, v_w_mem_kv, v_w_mem_o, v_g_ffn, v_w_gate, v_w_up, v_w_down, v_g_final):
    given = dict(locals())
    bf = jnp.bfloat16
    xi, yi, ci = _place()
    me = 4 * xi + 2 * yi + ci

    transposed = {"w_inT": w_in, "w_mem_kvT": w_mem_kv, "w_gateT": w_gate, "w_upT": w_up}
    plain = {"w_out": w_out, "w_mem_q": w_mem_q, "w_mem_o": w_mem_o, "w_down": w_down}
    shards = [transposed[n][0].T if n in transposed else plain[n][0] for n in W_NAMES]
    zones = dict(zip(W_NAMES, _landing_zones(shards, me.astype(jnp.int32).reshape(1))))
    dw = jnp.pad(w_dw[0], ((0, 1), (0, 0)))
    zones["w_dw"] = lax.dynamic_update_slice(lax.empty((NDEV,) + dw.shape, dw.dtype), dw[None], (me, 0, 0))
    gather = _Gather(_gather_start([[zones[n] for n in names] for names in GATHER_GROUPS]))

    sp = {n: given[n] for n in ("g_mix", "b_in", "b_dw", "g_conv_ln", "b_conv_ln", "attn_sink", "b_out", "g_mem_q",
                                "g_mem_kv", "g_ffn")}
    sp["g_final"] = g_final.reshape(1, D)
    frames = {"w_inT": ("w_in", True), "w_gateT": ("w_gate", True), "w_upT": ("w_up", True),
              "w_out": ("w_out", False), "w_mem_q": ("w_mem_q", False), "w_mem_o": ("w_mem_o", False),
              "w_down": ("w_down", False)}
    states = {}
    for frame, (n, transposed) in frames.items():
        wmv = (given[n][0], given["m_" + n][0], given["v_" + n][0])
        states[frame] = tuple(a.T for a in wmv) if transposed else wmv
    reduce = _Reduce(states)
    grad_x = _local_step(x[0], mem[0], loss_target[0], sp, gather, reduce)
    reduce.finish(2, reduce.results["w_out"][1])

    grads, deltas, new_m, new_v = {}, {}, {}, {}
    for frame, (n, transposed) in frames.items():
        outs = [a.T if transposed else a for a in reduce.results[frame]]
        grads[n], deltas[n], new_m[n], new_v[n] = [a[None] for a in outs]

    def update(n, g):
        d, nm, nv = _adamw(given[n][0], g, given["m_" + n][0], given["v_" + n][0], "adamw_" + n)
        grads[n], deltas[n], new_m[n], new_v[n] = g[None], d[None], nm[None], nv[None]

    update("w_mem_kv", reduce.results["w_mem_kvT"][0].T)

    packs, gdws = reduce.smalls
    names_small = [p[0] for p in PACK[1:]]

    def as_row(a):
        return a.reshape(1, -1)

    res = _small_update(packs, gdws, [as_row(given[n]) for n in names_small],
                        [as_row(given["m_" + n]) for n in names_small],
                        [as_row(given["v_" + n]) for n in names_small])
    loss = res[0][0, 0]
    for k, n in enumerate(names_small):
        shape = given[n].shape
        g, d, nm, nv = res[2 + 4 * k:6 + 4 * k]
        grads[n], deltas[n], new_m[n], new_v[n] = (g.reshape(shape), d.reshape(shape), nm.reshape(shape),
                                                   nv.reshape(shape))
    update("w_dw", lax.dynamic_slice(res[1], (0, me * (CC // NDEV)), (TAPS, CC // NDEV)))

    order = ("g_mix", "w_in", "b_in", "w_dw", "b_dw", "g_conv_ln", "b_conv_ln", "attn_sink", "w_out", "b_out",
             "g_mem_q", "g_mem_kv", "w_mem_q", "w_mem_kv", "w_mem_o", "g_ffn", "w_gate", "w_up", "w_down", "g_final")
    return (loss, grad_x[None], *[grads[n] for n in order], *[deltas[n] for n in order],
            *[new_m[n] for n in order], *[new_v[n] for n in order])
```

```python
import jax
import jax.numpy as jnp
from jax import lax
from jax.experimental import pallas as pl
from jax.experimental.pallas import tpu as pltpu

D = 1024
CC = 512
HD = 64
NQ = 8
NKV = 2
GRP = NQ // NKV
ATT = NQ * HD
KVC = NKV * HD
BLK = 128
MH = 4
MHD = D // MH
F = 2816
FCH = 256
FCH_BWD = F
INC = 2 * CC + ATT + 2 * KVC
TAPS = 31
EPS = 1e-6
ROPE_THETA = 10000.0
SCALE = HD ** -0.5
MSCALE = MHD ** -0.5
LANES = 128
NDEV = 8

ADAM_LR = 0.001
ADAM_B1 = 0.9
ADAM_B2 = 0.999
ADAM_EPS = 1e-08
ADAM_WD = 0.01
ADAM_STEP = 10

F32 = jnp.float32
_MXU = jnp.bfloat16
MESH = pl.DeviceIdType.MESH

W_NAMES = ("w_inT", "w_out", "w_mem_q", "w_mem_kvT", "w_mem_o", "w_gateT", "w_upT", "w_down")


def _mm(a, b, ca, cb):
    return lax.dot_general(a.astype(_MXU), b.astype(_MXU), (((ca,), (cb,)), ((), ())),
                           preferred_element_type=F32)


def _nn(a, b):
    return _mm(a, b, 1, 0)


def _nt(a, b):
    return _mm(a, b, 1, 1)


def _tn(a, b):
    return _mm(a, b, 0, 0)


def _sigmoid(x):
    return 0.5 * jnp.tanh(0.5 * x) + 0.5


def _rms(x, g):
    r = lax.rsqrt(jnp.mean(x * x, axis=-1, keepdims=True) + EPS)
    return x * r * g, r


def _rms_bwd(dy, x, r, g):
    xh = x * r
    dg = jnp.sum(dy * xh, axis=0, keepdims=True)
    dxh = dy * g
    dx = r * (dxh - xh * jnp.mean(dxh * xh, axis=-1, keepdims=True))
    return dx, dg


def _rope(t, c, s):
    n = t.shape[1]
    reps = n // LANES
    if reps > 1:
        c = jnp.tile(c, (1, reps))
        s = jnp.tile(s, (1, reps))
    lane = lax.broadcasted_iota(jnp.int32, t.shape, 1)
    first = (lane & (HD - 1)) < (HD // 2)
    partner = jnp.where(first, pltpu.roll(t, n - HD // 2, 1), pltpu.roll(t, HD // 2, 1))
    return t * c + partner * s


def _rope_bwd(dt, c, s):
    n = dt.shape[1]
    reps = n // LANES
    if reps > 1:
        c = jnp.tile(c, (1, reps))
        s = jnp.tile(s, (1, reps))
    lane = lax.broadcasted_iota(jnp.int32, dt.shape, 1)
    first = (lane & (HD - 1)) < (HD // 2)
    ds = dt * s
    partner = jnp.where(first, pltpu.roll(ds, n - HD // 2, 1), pltpu.roll(ds, HD // 2, 1))
    return dt * c + partner


def _acc_init(i, *refs):
    @pl.when(i == 0)
    def _():
        for r in refs:
            r[...] = jnp.zeros(r.shape, r.dtype)


def _acc_row(ref, v):
    ref[...] += jnp.broadcast_to(v, ref.shape)


def _rows(tm, n):
    return pl.BlockSpec((tm, n), lambda i: (i, 0))


def _full(shape):
    nd = len(shape)
    return pl.BlockSpec(shape, lambda i: (0,) * nd)


def _once(shape):
    nd = len(shape)
    return pl.BlockSpec(shape, lambda i: (0,) * nd, pipeline_mode=pl.Buffered(1))


def _params(vmem_mb):
    return pltpu.CompilerParams(dimension_semantics=("arbitrary",), vmem_limit_bytes=vmem_mb * 2 ** 20)


def _sds(shape, dtype):
    return pltpu.HBM(shape, dtype)


def _hbm_call(*args, **kwargs):
    call = pl.pallas_call(*args, **kwargs)
    return lambda *ops: call(*[pltpu.with_memory_space_constraint(o, pltpu.HBM) for o in ops])


def _mem_kv_fwd(mem, g_kv, wkvT):
    M = mem.shape[0]

    def body(mem_ref, g_ref, w_ref, km_ref, vm_ref, mn_ref):
        y, _ = _rms(mem_ref[...], g_ref[...])
        kv = _nt(y, w_ref[...])
        km_ref[...] = kv[:, :D].astype(km_ref.dtype)
        vm_ref[...] = kv[:, D:].astype(vm_ref.dtype)
        mn_ref[...] = y.astype(mn_ref.dtype)

    return _hbm_call(
        body, name="mem_kv_fwd", grid=(1,),
        in_specs=[_full((M, D)), _full((1, D)), _full((2 * D, D))],
        out_specs=[_full((M, D))] * 3,
        out_shape=[_sds((M, D), _MXU)] * 3,
        compiler_params=_params(40),
    )(mem, g_kv, wkvT)


def _in_proj_fwd(x, g_mix, winT, b_in, cosq, sinq, tm=512):
    S = x.shape[0]

    def body(x_ref, g_ref, w_ref, b_ref, c_ref, s_ref, h_ref, uglu_ref, vg_ref, q_ref, k_ref, v_ref):
        h, _ = _rms(x_ref[...], g_ref[...])
        hb = h.astype(h_ref.dtype)
        h_ref[...] = hb
        u = _nt(hb, w_ref[...]) + b_ref[...]
        uglu_ref[...] = u[:, :2 * CC].astype(uglu_ref.dtype)
        vg_ref[...] = u[:, :CC] * _sigmoid(u[:, CC:2 * CC])
        c = c_ref[...]
        s = s_ref[...]
        q_ref[...] = (_rope(u[:, 2 * CC:2 * CC + ATT], c, s) * SCALE).astype(q_ref.dtype)
        k_ref[...] = _rope(u[:, 2 * CC + ATT:2 * CC + ATT + KVC], c, s).astype(k_ref.dtype)
        v_ref[...] = u[:, 2 * CC + ATT + KVC:].astype(v_ref.dtype)

    return _hbm_call(
        body, name="in_proj_fwd", grid=(S // tm,),
        in_specs=[_rows(tm, D), _full((1, D)), _full((INC, D)), _full((1, INC)), _rows(tm, LANES), _rows(tm, LANES)],
        out_specs=[_rows(tm, D), _rows(tm, 2 * CC), _rows(tm, CC), _rows(tm, ATT), _rows(tm, KVC), _rows(tm, KVC)],
        out_shape=[_sds((S, D), _MXU), _sds((S, 2 * CC), _MXU), _sds((S, CC), F32), _sds((S, ATT), _MXU),
                   _sds((S, KVC), _MXU), _sds((S, KVC), _MXU)],
        compiler_params=_params(48),
    )(x, g_mix, winT, b_in, cosq, sinq)


CONV_HALO = 16


CONV_TILE = 512


def _shift_scratch(tm, sets):
    return pltpu.VMEM((sets, CC // LANES, 8, tm + 24, LANES), F32)


def _shifted_copies(pad_ref, sh_ref, t0, tm):
    for cl in range(CC // LANES):
        blk = pad_ref[pl.ds(t0, tm + 2 * CONV_HALO), cl * LANES:(cl + 1) * LANES]
        for b in range(8):
            sh_ref[cl, b, :, :] = blk[b:b + tm + 24, :]


def _conv_taps():
    return [(8 * a + b - 1, b, 8 * a) for b in range(8) for a in range(4) if 0 <= 8 * a + b - 1 < TAPS]


CONV_ROWS = 64
SUBLANES = 8


def _dwconv_tile(sh_ref, tm, w_ref, flip, out_ref):
    for cl in range(CC // LANES):
        lanes = slice(cl * LANES, (cl + 1) * LANES)

        def chunk(c, carry):
            r = pl.multiple_of(c * CONV_ROWS, CONV_ROWS)
            accs = [None] * 4
            for n, (k, b, off) in enumerate(_conv_taps()):
                kk = TAPS - 1 - k if flip else k
                term = sh_ref[cl, b, pl.ds(off + r, CONV_ROWS), :] * w_ref[kk:kk + 1, lanes]
                accs[n % 4] = term if accs[n % 4] is None else accs[n % 4] + term
            out_ref[pl.ds(r, CONV_ROWS), lanes] = (accs[0] + accs[1]) + (accs[2] + accs[3])
            return carry

        lax.fori_loop(0, tm // CONV_ROWS, chunk, 0)


def _fill_padded(i, src_hbm, pad_ref, sem, S):
    @pl.when(i == 0)
    def _():
        zeros = jnp.zeros((CONV_HALO, pad_ref.shape[1]), pad_ref.dtype)
        pad_ref[0:CONV_HALO, :] = zeros
        pad_ref[CONV_HALO + S:2 * CONV_HALO + S, :] = zeros
        cp = pltpu.make_async_copy(src_hbm, pad_ref.at[pl.ds(CONV_HALO, S), :], sem)
        cp.start()
        cp.wait()


def _ln_silu(y, g, b):
    mu = jnp.mean(y, axis=-1, keepdims=True)
    yc = y - mu
    var = jnp.mean(yc * yc, axis=-1, keepdims=True)
    rstd = lax.rsqrt(var + EPS)
    yh = yc * rstd
    z = yh * g + b
    return yh, rstd, z


def _conv_fwd(vg, w_dw, b_dw, ln_g, ln_b):
    S = vg.shape[0]
    tm = min(CONV_TILE, S)

    def body(vg_hbm, w_ref, b_ref, g_ref, bb_ref, y_ref, yc_ref, pad_ref, sh_ref, sem):
        i = pl.program_id(0)
        _fill_padded(i, vg_hbm, pad_ref, sem, S)
        _shifted_copies(pad_ref, sh_ref.at[0], pl.multiple_of(i * tm, tm), tm)
        _dwconv_tile(sh_ref.at[0], tm, w_ref, False, y_ref)
        y = y_ref[...] + b_ref[...]
        y_ref[...] = y
        _, _, z = _ln_silu(y, g_ref[...], bb_ref[...])
        yc_ref[...] = (z * _sigmoid(z)).astype(yc_ref.dtype)

    return _hbm_call(
        body, name="conv_fwd", grid=(S // tm,),
        in_specs=[pl.BlockSpec(memory_space=pl.ANY), _full((TAPS, CC)), _full((1, CC)), _full((1, CC)), _full((1, CC))],
        out_specs=[_rows(tm, CC), _rows(tm, CC)],
        out_shape=[_sds((S, CC), F32), _sds((S, CC), _MXU)],
        scratch_shapes=[pltpu.VMEM((S + 2 * CONV_HALO, CC), F32), _shift_scratch(tm, 1), pltpu.SemaphoreType.DMA],
        compiler_params=_params(40),
    )(vg, w_dw, b_dw, ln_g, ln_b)


def _attn_window(i, S):
    start = jnp.clip((i - 1) * BLK, 0, S - 3 * BLK)
    return pl.multiple_of(start, BLK)


ATTN_COLS = GRP * BLK


def _attn_mask_base():
    keys = lax.broadcasted_iota(jnp.int32, (3 * BLK, ATTN_COLS), 0)
    cols = lax.broadcasted_iota(jnp.int32, (3 * BLK, ATTN_COLS), 1)
    return keys - (cols & (BLK - 1))


def _attn_mask(base, i, start):
    return jnp.abs(base + (start - i * BLK)) <= BLK


def _stack_heads(t, hk):
    return jnp.concatenate([t[:, (hk * GRP + g) * HD:(hk * GRP + g + 1) * HD] for g in range(GRP)], axis=0)


def _unstack_heads_t(t):
    cols = []
    for p in range(GRP // 2):
        pair = jnp.concatenate([t[:, (2 * p) * BLK:(2 * p + 1) * BLK], t[:, (2 * p + 1) * BLK:(2 * p + 2) * BLK]],
                               axis=0)
        cols.append(pair.T)
    return jnp.concatenate(cols, axis=1)


def _attn_probs(qs, kh, sk_ref, hk, mask):
    s = jnp.where(mask, _nt(kh, qs), -1e30)
    sk = jnp.concatenate(
        [jnp.broadcast_to(sk_ref[0:1, hk * GRP + g:hk * GRP + g + 1], (1, BLK)) for g in range(GRP)], axis=1)
    m = jnp.maximum(jnp.max(s, axis=0, keepdims=True), sk)
    e = jnp.exp(s - m)
    es = jnp.exp(sk - m)
    inv = 1.0 / (jnp.sum(e, axis=0, keepdims=True) + es)
    return e, inv, es


ATTN_BLOCKS = 8


def _attn_fwd(qr, kr, vv, sink):
    S = qr.shape[0]
    tq = ATTN_BLOCKS * BLK

    def body(q_ref, k_ref, v_ref, sk_ref, o_ref):
        base = _attn_mask_base()
        for b in range(ATTN_BLOCKS):
            i = pl.program_id(0) * ATTN_BLOCKS + b
            rows = slice(b * BLK, (b + 1) * BLK)
            start = _attn_window(i, S)
            kb = k_ref[pl.ds(start, 3 * BLK), :]
            vb = v_ref[pl.ds(start, 3 * BLK), :]
            q = q_ref[rows, :]
            mask = _attn_mask(base, i, start)
            outs = []
            for hk in range(NKV):
                hs = slice(hk * HD, (hk + 1) * HD)
                e, inv, _ = _attn_probs(_stack_heads(q, hk), kb[:, hs], sk_ref, hk, mask)
                outs.append(_unstack_heads_t(_tn(vb[:, hs], e) * inv))
            o_ref[rows, :] = jnp.concatenate(outs, axis=1).astype(o_ref.dtype)

    return _hbm_call(
        body, name="attn_fwd", grid=(S // tq,),
        in_specs=[_rows(tq, ATT), _full((S, KVC)), _full((S, KVC)), _full((1, NQ))],
        out_specs=_rows(tq, ATT),
        out_shape=_sds((S, ATT), _MXU),
        compiler_params=_params(40),
    )(qr, kr, vv, sink)


def _out_proj_fwd(x, yc, ya, wout, b_out, tm=512):
    S = x.shape[0]

    def body(x_ref, yc_ref, ya_ref, w_ref, b_ref, x1_ref, ymix_ref):
        ymix = jnp.concatenate([yc_ref[...], ya_ref[...]], axis=1)
        ymix_ref[...] = ymix
        x1_ref[...] = x_ref[...] + _nn(ymix, w_ref[...]) + b_ref[...]

    return _hbm_call(
        body, name="out_proj_fwd", grid=(S // tm,),
        in_specs=[_rows(tm, D), _rows(tm, CC), _rows(tm, ATT), _full((D, D)), _full((1, D))],
        out_specs=[_rows(tm, D), _rows(tm, D)],
        out_shape=[_sds((S, D), F32), _sds((S, D), _MXU)],
        compiler_params=_params(40),
    )(x, yc, ya, wout, b_out)


def _softmax_t(s):
    m = jnp.max(s, axis=0, keepdims=True)
    e = jnp.exp(s - m)
    return e, 1.0 / jnp.sum(e, axis=0, keepdims=True)


def _softmax(s):
    m = jnp.max(s, axis=-1, keepdims=True)
    e = jnp.exp(s - m)
    return e * (1.0 / jnp.sum(e, axis=-1, keepdims=True))


def _mem_attn_fwd(x1, g_q, wq, km, vm, wo, after, tm=512):
    S = x1.shape[0]
    M = km.shape[0]

    def body(x1_ref, g_ref, wq_ref, km_ref, vm_ref, wo_ref, after_ref, x2_ref, qm_ref, om_ref):
        x1v = x1_ref[...]
        hq, _ = _rms(x1v, g_ref[...])
        qm = (_nn(hq, wq_ref[...]) * MSCALE).astype(qm_ref.dtype)
        qm_ref[...] = qm
        oms = []
        for h in range(MH):
            sl = slice(h * MHD, (h + 1) * MHD)
            p = _softmax(_nt(qm[:, sl], km_ref[:, sl]))
            oms.append(_nn(p, vm_ref[:, sl]))
        om = jnp.concatenate(oms, axis=1).astype(om_ref.dtype)
        om_ref[...] = om
        x2_ref[...] = x1v + _nn(om, wo_ref[...])

    return _hbm_call(
        body, name="mem_attn_fwd", grid=(S // tm,),
        in_specs=[_rows(tm, D), _full((1, D)), _full((D, D)), _full((M, D)), _full((M, D)), _full((D, D)), _AFTER],
        out_specs=[_rows(tm, D)] * 3,
        out_shape=[_sds((S, D), F32), _sds((S, D), _MXU), _sds((S, D), _MXU)],
        compiler_params=_params(48),
    )(x1, g_q, wq, km, vm, wo, after)


def _ffn_loss_fwd(x2, tgt, g_ffn, g_final, wgT, wuT, wd, tm=512):
    S = x2.shape[0]

    def body(x2_ref, t_ref, gf_ref, gl_ref, wg_ref, wu_ref, wd_ref,
             gt_ref, up_ref, hf_ref, dx3_ref, dx3b_ref, loss_ref, dgl_ref):
        i = pl.program_id(0)
        _acc_init(i, loss_ref, dgl_ref)
        x2v = x2_ref[...]
        hf, _ = _rms(x2v, gf_ref[...])
        hb = hf.astype(hf_ref.dtype)
        hf_ref[...] = hb
        x3 = x2v
        for c in range(F // FCH):
            sl = slice(c * FCH, (c + 1) * FCH)
            gt = _nt(hb, wg_ref[sl, :])
            up = _nt(hb, wu_ref[sl, :])
            gt_ref[:, sl] = gt.astype(gt_ref.dtype)
            up_ref[:, sl] = up.astype(up_ref.dtype)
            x3 = x3 + _nn(gt * _sigmoid(gt) * up, wd_ref[sl, :])
        gl = gl_ref[...]
        y, r = _rms(x3, gl)
        diff = y - t_ref[...]
        _acc_row(loss_ref, jnp.sum(jnp.sum(diff * diff, axis=-1, keepdims=True), axis=0, keepdims=True) * (0.5 / D))
        dx3, dgl = _rms_bwd(diff * (1.0 / D), x3, r, gl)
        dx3_ref[...] = dx3
        dx3b_ref[...] = dx3.astype(dx3b_ref.dtype)
        _acc_row(dgl_ref, dgl)

    return _hbm_call(
        body, name="ffn_loss_fwd", grid=(S // tm,),
        in_specs=[_rows(tm, D), _rows(tm, D), _full((1, D)), _full((1, D)),
                  _once((F, D)), _once((F, D)), _once((F, D))],
        out_specs=[_rows(tm, F), _rows(tm, F), _rows(tm, D), _rows(tm, D), _rows(tm, D),
                   _full((8, LANES)), _full((8, D))],
        out_shape=[_sds((S, F), _MXU), _sds((S, F), _MXU), _sds((S, D), _MXU), _sds((S, D), F32),
                   _sds((S, D), _MXU), _sds((8, LANES), F32), _sds((8, D), F32)],
        compiler_params=_params(56),
    )(x2, tgt, g_ffn, g_final, wgT, wuT, wd)


def _ffn_bwd(dx3, x2, gt, up, g_ffn, wd, wgT, wuT, tm=256):
    S = x2.shape[0]

    def body(dx3_ref, x2_ref, gt_ref, up_ref, g_ref, wd_ref, wg_ref, wu_ref,
             dx2_ref, dx2b_ref, dgt_ref, dup_ref, act_ref, dg_ref):
        i = pl.program_id(0)
        _acc_init(i, dg_ref)
        dx3v = dx3_ref[...]
        db = dx3v.astype(_MXU)
        dhf = jnp.zeros((tm, D), F32)
        for c in range(F // FCH_BWD):
            sl = slice(c * FCH_BWD, (c + 1) * FCH_BWD)
            gtv = gt_ref[:, sl].astype(F32)
            upv = up_ref[:, sl].astype(F32)
            dact = _nt(db, wd_ref[sl, :])
            sig = _sigmoid(gtv)
            silu = gtv * sig
            dup = (dact * silu).astype(dup_ref.dtype)
            dgt = (dact * upv * (sig * (1.0 + gtv * (1.0 - sig)))).astype(dgt_ref.dtype)
            act_ref[:, sl] = (silu * upv).astype(act_ref.dtype)
            dgt_ref[:, sl] = dgt
            dup_ref[:, sl] = dup
            dhf = dhf + _nn(dgt, wg_ref[sl, :]) + _nn(dup, wu_ref[sl, :])
        x2v = x2_ref[...]
        g = g_ref[...]
        _, r = _rms(x2v, g)
        dxn, dg = _rms_bwd(dhf, x2v, r, g)
        dx2 = dx3v + dxn
        dx2_ref[...] = dx2
        dx2b_ref[...] = dx2.astype(dx2b_ref.dtype)
        _acc_row(dg_ref, dg)

    return _hbm_call(
        body, name="ffn_bwd", grid=(S // tm,),
        in_specs=[_rows(tm, D), _rows(tm, D), _rows(tm, F), _rows(tm, F), _full((1, D)),
                  _once((F, D)), _once((F, D)), _once((F, D))],
        out_specs=[_rows(tm, D), _rows(tm, D), _rows(tm, F), _rows(tm, F), _rows(tm, F), _full((8, D))],
        out_shape=[_sds((S, D), F32), _sds((S, D), _MXU), _sds((S, F), _MXU), _sds((S, F), _MXU),
                   _sds((S, F), _MXU), _sds((8, D), F32)],
        compiler_params=_params(56),
    )(dx3, x2, gt, up, g_ffn, wd, wgT, wuT)


_AFTER = pl.BlockSpec(memory_space=pl.ANY)


def _wgrad(pairs, name, after, rc=256):
    S, R = pairs[0][0].shape
    N = pairs[0][1].shape[1]
    rights = []
    for _, b in pairs:
        if not any(b is r for r in rights):
            rights.append(b)
    which = [next(i for i, r in enumerate(rights) if r is b) for _, b in pairs]
    n, nb = len(pairs), len(rights)

    def body(*refs):
        for k in range(n):
            refs[n + nb + 1 + k][...] = _tn(refs[k][...], refs[n + which[k]][...]).astype(jnp.bfloat16)

    res = _hbm_call(
        body, name=name, grid=(R // rc,),
        in_specs=[pl.BlockSpec((S, rc), lambda j: (0, j))] * n + [_once((S, N))] * nb + [_AFTER],
        out_specs=[_rows(rc, N)] * n,
        out_shape=[_sds((R, N), jnp.bfloat16)] * n,
        compiler_params=_params(56),
    )(*[a for a, _ in pairs], *rights, after)
    return list(res)


def _mem_attn_bwd(dx2, x1, qm, km, vm, g_q, wq, wo, wout, y, ln_g, ln_b, after, tm=512):
    S = x1.shape[0]
    M = km.shape[0]

    def body(dx2_ref, x1_ref, qm_ref, km_ref, vm_ref, g_ref, wq_ref, wo_ref, wout_ref, y_ref, lg_ref, lb_ref,
             after_ref, dx1_ref, dx1b_ref, hq_ref, dqm_ref, dy_ref, dya_ref, dkm_ref, dvm_ref, dg_ref, dbo_ref,
             dlg_ref, dlb_ref, dbdw_ref):
        i = pl.program_id(0)
        _acc_init(i, dkm_ref, dvm_ref, dg_ref, dbo_ref, dlg_ref, dlb_ref, dbdw_ref)
        dx2v = dx2_ref[...]
        domt = _nt(wo_ref[...], dx2v).astype(_MXU)
        dqs = []
        for h in range(MH):
            sl = slice(h * MHD, (h + 1) * MHD)
            qh = qm_ref[:, sl]
            kh = km_ref[:, sl]
            e, inv = _softmax_t(_nt(kh, qh))
            p = e * inv
            dp = _nn(vm_ref[:, sl], domt[sl, :])
            ds = (p * (dp - jnp.sum(dp * p, axis=0, keepdims=True))).astype(_MXU)
            dqs.append(_tn(ds, kh) * MSCALE)
            dkm_ref[:, sl] += _nn(ds, qh)
            dvm_ref[:, sl] += _nt(p, domt[sl, :])
        dqm = jnp.concatenate(dqs, axis=1).astype(dqm_ref.dtype)
        dqm_ref[...] = dqm
        dhq = _nt(dqm, wq_ref[...])
        x1v = x1_ref[...]
        g = g_ref[...]
        hq, r = _rms(x1v, g)
        hq_ref[...] = hq.astype(hq_ref.dtype)
        dxn, dg = _rms_bwd(dhq, x1v, r, g)
        dx1 = dx2v + dxn
        dx1_ref[...] = dx1
        dx1b = dx1.astype(dx1b_ref.dtype)
        dx1b_ref[...] = dx1b
        dym = _nt(dx1b, wout_ref[...])
        dya_ref[...] = dym[:, CC:].astype(dya_ref.dtype)
        _acc_row(dg_ref, dg)
        _acc_row(dbo_ref, jnp.sum(dx1, axis=0, keepdims=True))
        lg = lg_ref[...]
        yh, rstd, z = _ln_silu(y_ref[...], lg, lb_ref[...])
        sig = _sigmoid(z)
        dz = dym[:, :CC] * (sig * (1.0 + z * (1.0 - sig)))
        dyh = dz * lg
        dy = rstd * (dyh - jnp.mean(dyh, axis=-1, keepdims=True) - yh * jnp.mean(dyh * yh, axis=-1, keepdims=True))
        dy_ref[...] = dy
        _acc_row(dlg_ref, jnp.sum(dz * yh, axis=0, keepdims=True))
        _acc_row(dlb_ref, jnp.sum(dz, axis=0, keepdims=True))
        _acc_row(dbdw_ref, jnp.sum(dy, axis=0, keepdims=True))

    return _hbm_call(
        body, name="mem_attn_bwd", grid=(S // tm,),
        in_specs=[_rows(tm, D), _rows(tm, D), _rows(tm, D), _full((M, D)), _full((M, D)), _full((1, D)),
                  _once((D, D)), _once((D, D)), _once((D, D)), _rows(tm, CC), _full((1, CC)), _full((1, CC)),
                  _AFTER],
        out_specs=[_rows(tm, D), _rows(tm, D), _rows(tm, D), _rows(tm, D), _rows(tm, CC), _rows(tm, ATT),
                   _full((M, D)), _full((M, D)), _full((8, D)), _full((8, D)),
                   _full((8, CC)), _full((8, CC)), _full((8, CC))],
        out_shape=[_sds((S, D), F32), _sds((S, D), _MXU), _sds((S, D), _MXU), _sds((S, D), _MXU),
                   _sds((S, CC), F32), _sds((S, ATT), _MXU),
                   _sds((M, D), F32), _sds((M, D), F32), _sds((8, D), F32), _sds((8, D), F32),
                   _sds((8, CC), F32), _sds((8, CC), F32), _sds((8, CC), F32)],
        compiler_params=_params(56),
    )(dx2, x1, qm, km, vm, g_q, wq, wo, wout, y, ln_g, ln_b, after)


def _mem_kv_bwd(dkm, dvm, mem, g_kv, memn, wkvT):
    M = mem.shape[0]

    def body(dkm_ref, dvm_ref, mem_ref, g_ref, mn_ref, w_ref, gw_ref, dg_ref):
        dkv = jnp.concatenate([dkm_ref[...], dvm_ref[...]], axis=1).astype(_MXU)
        gw_ref[...] = _tn(dkv, mn_ref[...]).astype(gw_ref.dtype)
        dmn = _nn(dkv, w_ref[...])
        memv = mem_ref[...]
        g = g_ref[...]
        _, r = _rms(memv, g)
        _, dg = _rms_bwd(dmn, memv, r, g)
        dg_ref[...] = jnp.broadcast_to(dg, dg_ref.shape)

    return _hbm_call(
        body, name="mem_kv_bwd", grid=(1,),
        in_specs=[_full((M, D)), _full((M, D)), _full((M, D)), _full((1, D)), _full((M, D)), _full((2 * D, D))],
        out_specs=[_full((2 * D, D)), _full((8, D))],
        out_shape=[_sds((2 * D, D), jnp.bfloat16), _sds((8, D), F32)],
        compiler_params=_params(48),
    )(dkm, dvm, mem, g_kv, memn, wkvT)


def _attn_bwd(qr, kr, vv, sink, dya, after):
    S = qr.shape[0]
    tq = ATTN_BLOCKS * BLK

    def body(q_ref, k_ref, v_ref, sk_ref, do_ref, after_ref, dq_ref, dk_ref, dv_ref, dsk_ref, sink_acc):
        _acc_init(pl.program_id(0), dk_ref, dv_ref, sink_acc)
        base = _attn_mask_base()
        for b in range(ATTN_BLOCKS):
            i = pl.program_id(0) * ATTN_BLOCKS + b
            rows = slice(b * BLK, (b + 1) * BLK)
            start = _attn_window(i, S)
            kb = k_ref[pl.ds(start, 3 * BLK), :]
            vb = v_ref[pl.ds(start, 3 * BLK), :]
            q = q_ref[rows, :]
            do = do_ref[rows, :]
            mask = _attn_mask(base, i, start)
            dqs, dks, dvs = [], [], []
            for hk in range(NKV):
                hs = slice(hk * HD, (hk + 1) * HD)
                qs = _stack_heads(q, hk)
                dos = _stack_heads(do, hk)
                e, inv, es = _attn_probs(qs, kb[:, hs], sk_ref, hk, mask)
                p = e * inv
                dp = _nt(vb[:, hs], dos)
                delta = jnp.sum(dp * p, axis=0, keepdims=True)
                ds = (p * (dp - delta)).astype(_MXU)
                dqs.append(_unstack_heads_t(_tn(kb[:, hs], ds) * SCALE))
                dks.append(_nn(ds, qs))
                dvs.append(_nn(p, dos))
                sink_acc[hk:hk + 1, :] -= es * inv * delta
            dq_ref[rows, :] = jnp.concatenate(dqs, axis=1)
            dk_ref[pl.ds(start, 3 * BLK), :] += jnp.concatenate(dks, axis=1)
            dv_ref[pl.ds(start, 3 * BLK), :] += jnp.concatenate(dvs, axis=1)

        @pl.when(pl.program_id(0) == pl.num_programs(0) - 1)
        def _():
            for h in range(NQ):
                tot = jnp.sum(sink_acc[h // GRP:h // GRP + 1, (h % GRP) * BLK:(h % GRP + 1) * BLK], axis=1,
                              keepdims=True)
                dsk_ref[h:h + 1, :] = jnp.broadcast_to(tot, (1, LANES))

    return _hbm_call(
        body, name="attn_bwd", grid=(S // tq,),
        in_specs=[_rows(tq, ATT), _full((S, KVC)), _full((S, KVC)), _full((1, NQ)), _rows(tq, ATT), _AFTER],
        out_specs=[_rows(tq, ATT), _full((S, KVC)), _full((S, KVC)), _full((8, LANES))],
        out_shape=[_sds((S, ATT), F32), _sds((S, KVC), F32), _sds((S, KVC), F32), _sds((8, LANES), F32)],
        scratch_shapes=[pltpu.VMEM((8, ATTN_COLS), F32)],
        compiler_params=_params(40),
    )(qr, kr, vv, sink, dya, after)


def _conv_bwd(dy, vg, uglu, w_dw, after):
    S = dy.shape[0]
    tm = min(CONV_TILE, S)

    def body(dy_hbm, vg_hbm, uglu_ref, w_ref, after_ref, du_ref, gw_ref, dbin_ref, dyp_ref, vgp_ref, gacc_ref, sh_ref,
             dvg_ref, sems):
        i = pl.program_id(0)
        _acc_init(i, gacc_ref, dbin_ref)
        _fill_padded(i, dy_hbm, dyp_ref, sems.at[0], S)
        _fill_padded(i, vg_hbm, vgp_ref, sems.at[1], S)
        t0 = pl.multiple_of(i * tm, tm)
        _shifted_copies(dyp_ref, sh_ref.at[0], t0, tm)
        _shifted_copies(vgp_ref, sh_ref.at[1], t0, tm)
        _dwconv_tile(sh_ref.at[0], tm, w_ref, True, dvg_ref)
        dvg = dvg_ref[...]
        sub = CONV_ROWS
        for cl in range(CC // LANES):
            lanes = slice(cl * LANES, (cl + 1) * LANES)
            for k, b, off in _conv_taps():
                part = jnp.zeros((8, LANES), F32)
                for r in range(0, tm, sub):
                    prod = (sh_ref[0, cl, 0, CONV_HALO + r:CONV_HALO + r + sub, :]
                            * sh_ref[1, cl, b, off + r:off + r + sub, :])
                    part = part + jnp.sum(prod.reshape(sub // 8, 8, LANES), axis=0)
                gacc_ref[k, :, lanes] += part
        a_v = uglu_ref[:, :CC].astype(F32)
        sg = _sigmoid(uglu_ref[:, CC:].astype(F32))
        da = dvg * sg
        dgate = dvg * a_v * sg * (1.0 - sg)
        du = jnp.concatenate([da, dgate], axis=1)
        du_ref[...] = du.astype(du_ref.dtype)
        _acc_row(dbin_ref, jnp.sum(du, axis=0, keepdims=True))

        @pl.when(i == pl.num_programs(0) - 1)
        def _():
            gw_ref[...] = jnp.zeros(gw_ref.shape, F32)
            for k in range(TAPS):
                gw_ref[k:k + 1, :] = jnp.sum(gacc_ref[k], axis=0, keepdims=True)

    return _hbm_call(
        body, name="conv_bwd", grid=(S // tm,),
        in_specs=[pl.BlockSpec(memory_space=pl.ANY), pl.BlockSpec(memory_space=pl.ANY), _rows(tm, 2 * CC),
                  _full((TAPS, CC)), _AFTER],
        out_specs=[_rows(tm, 2 * CC), _full((TAPS + 1, CC)), _full((8, 2 * CC))],
        out_shape=[_sds((S, 2 * CC), _MXU), _sds((TAPS + 1, CC), F32), _sds((8, 2 * CC), F32)],
        scratch_shapes=[pltpu.VMEM((S + 2 * CONV_HALO, CC), F32), pltpu.VMEM((S + 2 * CONV_HALO, CC), F32),
                        pltpu.VMEM((TAPS, 8, CC), F32), _shift_scratch(tm, 2), pltpu.VMEM((tm, CC), F32),
                        pltpu.SemaphoreType.DMA((2,))],
        compiler_params=_params(56),
    )(dy, vg, uglu, w_dw, after)


def _in_proj_bwd(duglu, dqr, dkr, dv, cosq, sinq, x, dx1, g_mix, winT, tm=512):
    S = x.shape[0]
    NA = ATT + 2 * KVC

    def body(dug_ref, dq_ref, dk_ref, dv_ref, c_ref, s_ref, x_ref, dx1_ref, g_ref, w_ref,
             dx_ref, du_ref, dg_ref, dbin_ref):
        i = pl.program_id(0)
        _acc_init(i, dg_ref, dbin_ref)
        c = c_ref[...]
        s = s_ref[...]
        datt = jnp.concatenate([_rope_bwd(dq_ref[...], c, s), _rope_bwd(dk_ref[...], c, s), dv_ref[...]], axis=1)
        _acc_row(dbin_ref, jnp.sum(datt, axis=0, keepdims=True))
        du = jnp.concatenate([dug_ref[...], datt.astype(du_ref.dtype)], axis=1)
        du_ref[...] = du
        dh = _nn(du, w_ref[...])
        xv = x_ref[...]
        g = g_ref[...]
        _, r = _rms(xv, g)
        dxn, dg = _rms_bwd(dh, xv, r, g)
        dx_ref[...] = dx1_ref[...] + dxn
        _acc_row(dg_ref, dg)

    return _hbm_call(
        body, name="in_proj_bwd", grid=(S // tm,),
        in_specs=[_rows(tm, 2 * CC), _rows(tm, ATT), _rows(tm, KVC), _rows(tm, KVC), _rows(tm, LANES),
                  _rows(tm, LANES), _rows(tm, D), _rows(tm, D), _full((1, D)), _once((INC, D))],
        out_specs=[_rows(tm, D), _rows(tm, INC), _full((8, D)), _full((8, NA))],
        out_shape=[_sds((S, D), F32), _sds((S, INC), _MXU), _sds((8, D), F32), _sds((8, NA), F32)],
        compiler_params=_params(48),
    )(duglu, dqr, dkr, dv, cosq, sinq, x, dx1, g_mix, winT)


def _rope_tables(S):
    pos = jnp.arange(S, dtype=F32)
    inv_freq = ROPE_THETA ** (-jnp.arange(0, HD, 2, dtype=F32) / HD)
    ang = pos[:, None] * inv_freq[None, :]
    cos, sin = jnp.cos(ang), jnp.sin(ang)
    cosq = jnp.tile(jnp.concatenate([cos, cos], axis=1), (1, LANES // HD))
    sinq = jnp.tile(jnp.concatenate([-sin, sin], axis=1), (1, LANES // HD))
    return cosq, sinq


GATHER_GROUPS = (("w_inT", "w_dw"), ("w_out", "w_mem_kvT", "w_mem_q", "w_mem_o"), ("w_gateT", "w_upT"), ("w_down",))
ID_GATHER, ID_SIBLING, ID_CHIP = 1, 5, 8


def _local_step(x, mem, tgt, sp, gather, reduce):
    S = x.shape[0]
    assert S % (ATTN_BLOCKS * BLK) == 0 and S % 512 == 0, S
    cosq, sinq = _rope_tables(S)

    gather.forward(0, cosq)
    w0 = gather.finish(0, sinq)
    winT, w_dw_full = w0["w_inT"], w0["w_dw"]
    h, uglu, vg, qr, kr, vv = _in_proj_fwd(x, sp["g_mix"], winT, sp["b_in"], cosq, sinq)
    ya = _attn_fwd(qr, kr, vv, sp["attn_sink"])
    gather.forward(1, ya)
    y, yc = _conv_fwd(vg, w_dw_full, sp["b_dw"], sp["g_conv_ln"], sp["b_conv_ln"])
    w1 = gather.finish(1, yc)
    wout, wkvT, wq, wo = w1["w_out"], w1["w_mem_kvT"], w1["w_mem_q"], w1["w_mem_o"]
    km, vm, memn = _mem_kv_fwd(mem, sp["g_mem_kv"], wkvT)
    x1, ymix = _out_proj_fwd(x, yc, ya, wout, sp["b_out"])
    token = gather.forward(2, x1)
    x2, qm, om = _mem_attn_fwd(x1, sp["g_mem_q"], wq, km, vm, wo, token)
    w2 = gather.finish(2, om)
    gather.forward(3, om)
    wgT, wuT, wd = w2["w_gateT"], w2["w_upT"], gather.finish(3, om)["w_down"]
    gt, up, hf, dx3, dx3b, loss_acc, dg_final = _ffn_loss_fwd(x2, tgt, sp["g_ffn"], sp["g_final"], wgT, wuT, wd)

    dx2, dx2b, dgt, dup, act, dg_ffn = _ffn_bwd(dx3, x2, gt, up, sp["g_ffn"], wd, wgT, wuT)
    token = reduce.start(0, _wgrad([(dgt, hf), (dup, hf), (act, dx3b)], "wgrad_ffn", dx2))

    dx1, dx1b, hq, dqm, dy, dya, dkm, dvm, dg_mem_q, db_out, dg_cln, db_cln, db_dw = _mem_attn_bwd(
        dx2, x1, qm, km, vm, sp["g_mem_q"], wq, wo, wout, y, sp["g_conv_ln"], sp["b_conv_ln"], token)
    token = reduce.middle(0, dx1)
    g_wout, g_wq, g_wo = _wgrad([(ymix, dx1b), (hq, dqm), (om, dx2b)], "wgrad_mix_mem", token)
    g_wkvT, dg_mem_kv = _mem_kv_bwd(dkm, dvm, mem, sp["g_mem_kv"], memn, wkvT)
    token = reduce.start(1, [g_wout, g_wq, g_wkvT, g_wo])

    dqr, dkr, dvv, dsink = _attn_bwd(qr, kr, vv, sp["attn_sink"], dya, token)
    token = reduce.middle(1, dqr)
    duglu, g_wdw, dbin_glu = _conv_bwd(dy, vg, uglu, w_dw_full, token)
    grad_x, du, dg_mix, dbin_att = _in_proj_bwd(duglu, dqr, dkr, dvv, cosq, sinq, x, dx1, sp["g_mix"], winT)
    token = reduce.start(2, _wgrad([(du, h)], "wgrad_in", grad_x))

    small = {
        "loss": loss_acc[0:1, :],
        "g_mix": dg_mix[0:1], "b_in": jnp.concatenate([dbin_glu[0:1], dbin_att[0:1]], axis=1),
        "b_dw": db_dw[0:1], "g_conv_ln": dg_cln[0:1], "b_conv_ln": db_cln[0:1],
        "attn_sink": dsink[:, 0].reshape(1, NQ), "b_out": db_out[0:1], "g_mem_q": dg_mem_q[0:1],
        "g_mem_kv": dg_mem_kv[0:1], "g_ffn": dg_ffn[0:1], "g_final": dg_final[0:1],
    }
    token = reduce.middle(2, token, reduce.small_zones(small, g_wdw))
    reduce.finish(0, token)
    reduce.finish(1, token)
    return grad_x


def _place():
    return lax.axis_index("x"), lax.axis_index("y"), lax.axis_index("c")


_HBM = pl.BlockSpec(memory_space=pltpu.HBM)
_SEM = pl.BlockSpec(memory_space=pltpu.SEMAPHORE)
_ANY = pl.BlockSpec(memory_space=pl.ANY)
_EFFECT = pltpu.SideEffectType.DATAFLOW_SIDE_EFFECTING
_VMEM = pl.BlockSpec(memory_space=pltpu.VMEM)
_TOKEN = jax.ShapeDtypeStruct((8, LANES), F32)


def _dev_index(p):
    return 4 * p[0] + 2 * p[1] + p[2]


def _landing_zones(shards, me_idx):
    def body(m_ref, *refs):
        n = len(refs) // 2
        for k in range(n):
            refs[n + k][...] = refs[k][...].astype(jnp.bfloat16)

    return pl.pallas_call(
        body, name="landing_zones",
        grid_spec=pltpu.PrefetchScalarGridSpec(
            num_scalar_prefetch=1, grid=(1,),
            in_specs=[pl.BlockSpec(s.shape, lambda i, m: (0, 0)) for s in shards],
            out_specs=[pl.BlockSpec((None,) + s.shape, lambda i, m: (m[0], 0, 0)) for s in shards]),
        out_shape=[pltpu.HBM((NDEV,) + s.shape, jnp.bfloat16) for s in shards],
        compiler_params=_params(40),
    )(me_idx, *shards)


def _gather_start(groups):
    sizes = [len(g) for g in groups]
    flat = [z for g in groups for z in g]
    n = len(flat)

    def body(*refs):
        ins = refs[:n]
        sems = refs[2 * n:]
        x, y, c = _place()
        me = _dev_index((x, y, c))
        peers = [(x, y, 1 - c), (1 - x, y, c), (x, 1 - y, c), (1 - x, 1 - y, c)]
        base = 0
        for gi, size in enumerate(sizes):
            send_sems, recv_sems = sems[2 * gi], sems[2 * gi + 1]
            for k in range(size):
                zone = ins[base + k]
                for j, peer in enumerate(peers):
                    pltpu.make_async_remote_copy(
                        src_ref=zone.at[me], dst_ref=zone.at[me], send_sem=send_sems.at[4 * k + j],
                        recv_sem=recv_sems.at[4 * k + j], device_id=peer, device_id_type=MESH).start()
            base += size

    out_shape = [pltpu.HBM(z.shape, z.dtype) for z in flat]
    out_specs = [_HBM] * n
    for size in sizes:
        out_shape += [pltpu.SemaphoreType.DMA((4 * size,))] * 2
        out_specs += [_SEM, _SEM]
    res = pl.pallas_call(
        body, name="gather_start", in_specs=[_HBM] * n, out_specs=out_specs, out_shape=out_shape,
        input_output_aliases={k: k for k in range(n)},
        compiler_params=pltpu.CompilerParams(has_side_effects=_EFFECT),
    )(*[pltpu.with_memory_space_constraint(z, pltpu.HBM) for z in flat])
    out, base = [], 0
    for gi, size in enumerate(sizes):
        out.append((list(res[base:base + size]), res[n + 2 * gi], res[n + 2 * gi + 1]))
        base += size
    return out


def _gather_forward(zones, send_sems, recv_sems, after, collective_id, name):
    n = len(zones)

    def body(*refs):
        ins = refs[:n]
        send_ref, recv_ref = refs[n], refs[n + 1]
        fsend, frecv = refs[2 * n + 3], refs[2 * n + 4]
        x, y, c = _place()
        sibling = (x, y, 1 - c)
        barrier = pltpu.get_barrier_semaphore()
        pl.semaphore_signal(barrier, inc=1, device_id=sibling, device_id_type=MESH)
        pl.semaphore_wait(barrier, 1)
        chips = [(1 - x, y), (x, 1 - y), (1 - x, 1 - y)]
        for j, chip in enumerate(chips):
            for k in range(n):
                slot = ins[k].at[_dev_index((*chip, c))]
                pltpu.make_async_remote_copy(
                    src_ref=slot, dst_ref=slot, send_sem=send_ref.at[4 * k + 1 + j],
                    recv_sem=recv_ref.at[4 * k + 1 + j], device_id=sibling, device_id_type=MESH).wait_recv()
                pltpu.make_async_remote_copy(
                    src_ref=slot, dst_ref=slot, send_sem=fsend.at[3 * k + j], recv_sem=frecv.at[3 * k + j],
                    device_id=sibling, device_id_type=MESH).start()
        for k in range(n):
            own = ins[k].at[_dev_index((x, y, c))]
            theirs = ins[k].at[_dev_index(sibling)]
            pltpu.make_async_remote_copy(
                src_ref=theirs, dst_ref=theirs, send_sem=send_ref.at[4 * k], recv_sem=recv_ref.at[4 * k],
                device_id=sibling, device_id_type=MESH).wait_recv()
            for j in range(4):
                pltpu.make_async_remote_copy(
                    src_ref=own, dst_ref=own, send_sem=send_ref.at[4 * k + j], recv_sem=recv_ref.at[4 * k + j],
                    device_id=sibling, device_id_type=MESH).wait_send()
        refs[2 * n + 5][...] = jnp.zeros(_TOKEN.shape, _TOKEN.dtype)

    res = pl.pallas_call(
        body, name=name, in_specs=[_HBM] * n + [_SEM, _SEM, _ANY],
        out_specs=[_HBM] * n + [_SEM, _SEM, _VMEM],
        out_shape=[pltpu.HBM(z.shape, z.dtype) for z in zones] + [pltpu.SemaphoreType.DMA((3 * n,))] * 2 + [_TOKEN],
        input_output_aliases={k: k for k in range(n)},
        compiler_params=pltpu.CompilerParams(has_side_effects=_EFFECT, collective_id=collective_id),
    )(*zones, send_sems, recv_sems, after)
    return (list(res[:n]), res[n], res[n + 1]), res[n + 2]


def _gather_finish(zones, fsend, frecv, after, name):
    n = len(zones)

    def body(*refs):
        ins = refs[:n]
        fsend_ref, frecv_ref = refs[n], refs[n + 1]
        x, y, c = _place()
        sibling = (x, y, 1 - c)
        chips = [(1 - x, y), (x, 1 - y), (1 - x, 1 - y)]
        for j, chip in enumerate(chips):
            for k in range(n):
                mine = ins[k].at[_dev_index((*chip, c))]
                theirs = ins[k].at[_dev_index((*chip, 1 - c))]
                pltpu.make_async_remote_copy(
                    src_ref=theirs, dst_ref=theirs, send_sem=fsend_ref.at[3 * k + j],
                    recv_sem=frecv_ref.at[3 * k + j], device_id=sibling, device_id_type=MESH).wait_recv()
                pltpu.make_async_remote_copy(
                    src_ref=mine, dst_ref=mine, send_sem=fsend_ref.at[3 * k + j],
                    recv_sem=frecv_ref.at[3 * k + j], device_id=sibling, device_id_type=MESH).wait_send()

    res = pl.pallas_call(
        body, name=name, in_specs=[_HBM] * n + [_SEM, _SEM, _ANY], out_specs=[_HBM] * n,
        out_shape=[pltpu.HBM(z.shape, z.dtype) for z in zones],
        input_output_aliases={k: k for k in range(n)},
        compiler_params=pltpu.CompilerParams(has_side_effects=_EFFECT),
    )(*zones, fsend, frecv, after)
    return list(res)


class _Gather:
    def __init__(self, started):
        self._phase = list(started)

    def forward(self, gi, after):
        zones, send_sems, recv_sems = self._phase[gi]
        self._phase[gi], token = _gather_forward(zones, send_sems, recv_sems, after, ID_GATHER + gi,
                                                  "gather_forward_%d" % gi)
        return token

    def finish(self, gi, after):
        zones, fsend, frecv = self._phase[gi]
        out = {}
        for name, z in zip(GATHER_GROUPS[gi], _gather_finish(zones, fsend, frecv, after, "gather_finish_%d" % gi)):
            if name == "w_dw":
                out[name] = jnp.transpose(z, (1, 0, 2)).reshape(TAPS + 1, CC)[:TAPS]
            else:
                out[name] = z.reshape(NDEV * z.shape[1], D)
        return out


def _handshake(peers):
    barrier = pltpu.get_barrier_semaphore()
    for peer in peers:
        pl.semaphore_signal(barrier, inc=1, device_id=peer, device_id_type=MESH)
    pl.semaphore_wait(barrier, len(peers))


def _sibling_copies(srcs, lands, send_sems, recv_sems):
    x, y, c = _place()
    return [pltpu.make_async_remote_copy(
        src_ref=srcs[k].at[2 * s + (1 - c)], dst_ref=lands[k].at[s], send_sem=send_sems.at[4 * k + s],
        recv_sem=recv_sems.at[4 * k + s], device_id=(x, y, 1 - c), device_id_type=MESH)
        for k in range(len(srcs)) for s in range(4)]


def _sibling_start(gbs, collective_id, name):
    n = len(gbs)
    lands = [lax.empty((4,) + g.shape[1:], g.dtype) for g in gbs]

    def body(*refs):
        x, y, c = _place()
        _handshake([(x, y, 1 - c)])
        for cp in _sibling_copies(refs[:n], refs[n:2 * n], refs[4 * n], refs[4 * n + 1]):
            cp.start()
        refs[4 * n + 2][...] = jnp.zeros(_TOKEN.shape, _TOKEN.dtype)

    res = pl.pallas_call(
        body, name=name, in_specs=[_HBM] * (2 * n), out_specs=[_HBM] * (2 * n) + [_SEM, _SEM, _VMEM],
        out_shape=[pltpu.HBM(a.shape, a.dtype) for a in gbs + lands] + [pltpu.SemaphoreType.DMA((4 * n,))] * 2
        + [_TOKEN],
        input_output_aliases={k: k for k in range(2 * n)},
        compiler_params=pltpu.CompilerParams(has_side_effects=_EFFECT, collective_id=collective_id),
    )(*[pltpu.with_memory_space_constraint(a, pltpu.HBM) for a in gbs + lands])
    return list(res[:n]), list(res[n:2 * n]), res[2 * n], res[2 * n + 1], res[2 * n + 2]


def _sibling_wait(gbs, lands, send_sems, recv_sems, after, name):
    n = len(gbs)

    def body(*refs):
        for cp in _sibling_copies(refs[:n], refs[n:2 * n], refs[2 * n], refs[2 * n + 1]):
            cp.wait_recv()
            cp.wait_send()

    res = pl.pallas_call(
        body, name=name, in_specs=[_HBM] * (2 * n) + [_SEM, _SEM, _ANY], out_specs=[_HBM] * (2 * n),
        out_shape=[pltpu.HBM(a.shape, a.dtype) for a in gbs + lands],
        input_output_aliases={k: k for k in range(2 * n)},
        compiler_params=pltpu.CompilerParams(has_side_effects=_EFFECT),
    )(*gbs, *lands, send_sems, recv_sems, after)
    return list(res[:n]), list(res[n:])


def _pair_sum(gfs, ts, c_idx, name):
    n = len(gfs)
    halves = 1

    def body(c_ref, *refs):
        for k in range(n):
            refs[2 * n + k][...] = (refs[k][...].astype(F32) + refs[n + k][...].astype(F32)).astype(jnp.bfloat16)

    in_specs, out_specs, out_shape = [], [], []
    for g in gfs:
        rh = g.shape[1] // halves
        in_specs.append(pl.BlockSpec((None, rh, D), lambda s, h, c_ref: (2 * s + c_ref[0], h, 0)))
    for g in gfs:
        rh = g.shape[1] // halves
        in_specs.append(pl.BlockSpec((None, rh, D), lambda s, h, c_ref: (s, h, 0)))
        out_specs.append(pl.BlockSpec((None, rh, D), lambda s, h, c_ref: (s, h, 0)))
        out_shape.append(_sds((4, g.shape[1], D), jnp.bfloat16))
    return pl.pallas_call(
        body, name=name,
        grid_spec=pltpu.PrefetchScalarGridSpec(num_scalar_prefetch=1, grid=(4, halves), in_specs=in_specs,
                                               out_specs=out_specs),
        out_shape=out_shape,
        compiler_params=pltpu.CompilerParams(dimension_semantics=("arbitrary", "arbitrary"),
                                             vmem_limit_bytes=48 * 2 ** 20),
    )(c_idx, *gfs, *ts)


def _chip_copies(pbs, lands, smalls, send_sems, recv_sems):
    x, y, c = _place()
    own = 2 * x + y
    me = _dev_index((x, y, c))
    chips = [(1 - x, y), (x, 1 - y), (1 - x, 1 - y)]
    n = len(pbs)
    copies = [pltpu.make_async_remote_copy(
        src_ref=pbs[k].at[2 * chip[0] + chip[1]], dst_ref=lands[k].at[own], send_sem=send_sems.at[3 * k + j],
        recv_sem=recv_sems.at[3 * k + j], device_id=(*chip, c), device_id_type=MESH)
        for k in range(n) for j, chip in enumerate(chips)]
    peers = [(x, y, 1 - c)] + [(*chip, cc) for chip in chips for cc in (c, 1 - c)]
    copies += [pltpu.make_async_remote_copy(
        src_ref=smalls[t].at[me], dst_ref=smalls[t].at[me], send_sem=send_sems.at[3 * n + 7 * t + j],
        recv_sem=recv_sems.at[3 * n + 7 * t + j], device_id=peer, device_id_type=MESH)
        for t in range(len(smalls)) for j, peer in enumerate(peers)]
    return copies


def _chip_start(pbs, smalls, collective_id, name):
    n, m = len(pbs), len(smalls)
    lands = [lax.empty(p.shape, p.dtype) for p in pbs]
    arrays = list(pbs) + lands + list(smalls)
    na = len(arrays)
    nsem = 3 * n + 7 * m

    def body(*refs):
        x, y, c = _place()
        chips = [(1 - x, y), (x, 1 - y), (1 - x, 1 - y)]
        peers = [(*chip, c) for chip in chips]
        if m:
            peers += [(x, y, 1 - c)] + [(*chip, 1 - c) for chip in chips]
        _handshake(peers)
        for cp in _chip_copies(refs[:n], refs[n:2 * n], refs[2 * n:na], refs[2 * na], refs[2 * na + 1]):
            cp.start()
        refs[2 * na + 2][...] = jnp.zeros(_TOKEN.shape, _TOKEN.dtype)

    res = pl.pallas_call(
        body, name=name, in_specs=[_HBM] * na, out_specs=[_HBM] * na + [_SEM, _SEM, _VMEM],
        out_shape=[pltpu.HBM(a.shape, a.dtype) for a in arrays] + [pltpu.SemaphoreType.DMA((nsem,))] * 2 + [_TOKEN],
        input_output_aliases={k: k for k in range(na)},
        compiler_params=pltpu.CompilerParams(has_side_effects=_EFFECT, collective_id=collective_id),
    )(*[pltpu.with_memory_space_constraint(a, pltpu.HBM) for a in arrays])
    return (list(res[:n]), list(res[n:2 * n]), list(res[2 * n:na]), res[na], res[na + 1]), res[na + 2]


def _chip_wait(pbs, lands, smalls, send_sems, recv_sems, after, name):
    n = len(pbs)
    arrays = list(pbs) + list(lands) + list(smalls)
    na = len(arrays)

    def body(*refs):
        for cp in _chip_copies(refs[:n], refs[n:2 * n], refs[2 * n:na], refs[na], refs[na + 1]):
            cp.wait_recv()
            cp.wait_send()

    res = pl.pallas_call(
        body, name=name, in_specs=[_HBM] * na + [_SEM, _SEM, _ANY], out_specs=[_HBM] * na,
        out_shape=[pltpu.HBM(a.shape, a.dtype) for a in arrays],
        input_output_aliases={k: k for k in range(na)},
        compiler_params=pltpu.CompilerParams(has_side_effects=_EFFECT),
    )(*arrays, send_sems, recv_sems, after)
    return list(res[:n]), list(res[n:2 * n]), list(res[2 * n:])


def _chip_sum(pbs, qs, own_idx, states, name):
    n = len(pbs)
    halves = 2
    n_in = 4 * n + 3 * sum(s is not None for s in states)

    def body(o_ref, *refs):
        nxt_in, nxt_out = 4 * n, n_in
        for k in range(n):
            acc = refs[4 * k][...].astype(F32)
            for j in range(1, 4):
                acc = acc + refs[4 * k + j][...].astype(F32)
            refs[nxt_out][...] = acc
            nxt_out += 1
            if states[k] is not None:
                w_ref, m_ref, v_ref = refs[nxt_in:nxt_in + 3]
                nxt_in += 3
                d, nm, nv = _adamw_math(w_ref[...], acc, m_ref[...], v_ref[...])
                refs[nxt_out][...] = d
                refs[nxt_out + 1][...] = nm
                refs[nxt_out + 2][...] = nv
                nxt_out += 3

    def other(j):
        return lambda h, o: ((o[0] + j) % 4, h, 0)

    def rows(r):
        return pl.BlockSpec((r // halves, D), lambda h, o: (h, 0))

    in_specs, ins = [], []
    for p, q in zip(pbs, qs):
        rh = p.shape[1] // halves
        in_specs.append(pl.BlockSpec((None, rh, D), lambda h, o: (o[0], h, 0)))
        ins.append(p)
        for j in range(1, 4):
            in_specs.append(pl.BlockSpec((None, rh, D), other(j)))
            ins.append(q)
    out_specs, out_shape, counts = [], [], []
    for p, s in zip(pbs, states):
        r = p.shape[1]
        if s is not None:
            in_specs += [rows(r)] * 3
            ins += list(s)
        counts.append(1 if s is None else 4)
        out_specs += [rows(r)] * counts[-1]
        out_shape += [_sds((r, D), F32)] * counts[-1]
    res = pl.pallas_call(
        body, name=name,
        grid_spec=pltpu.PrefetchScalarGridSpec(num_scalar_prefetch=1, grid=(halves,), in_specs=in_specs,
                                               out_specs=out_specs),
        out_shape=out_shape,
        compiler_params=pltpu.CompilerParams(dimension_semantics=("arbitrary",), vmem_limit_bytes=56 * 2 ** 20),
    )(own_idx, *ins)
    out, base = [], 0
    for c in counts:
        out.append(tuple(res[base:base + c]))
        base += c
    return out


REDUCE_GROUPS = (("w_gateT", "w_upT", "w_down"), ("w_out", "w_mem_q", "w_mem_kvT", "w_mem_o"), ("w_inT",))


class _Reduce:
    def __init__(self, states):
        x, y, c = _place()
        self._c = c.astype(jnp.int32).reshape(1)
        self._own = (2 * x + y).astype(jnp.int32).reshape(1)
        self._me = _dev_index((x, y, c))
        self._state = {}
        self._adam = states
        self.results = {}
        self.smalls = None

    def start(self, gi, wgrads):
        gbs = [g.reshape(NDEV, g.shape[0] // NDEV, D) for g in wgrads]
        gbs, lands, send_sems, recv_sems, token = _sibling_start(gbs, ID_SIBLING + gi, "reduce_sibling_start_%d" % gi)
        self._state[gi] = (gbs, lands, send_sems, recv_sems)
        return token

    def middle(self, gi, after, smalls=()):
        gbs, lands, send_sems, recv_sems = self._state[gi]
        gbs, ts = _sibling_wait(gbs, lands, send_sems, recv_sems, after, "reduce_sibling_wait_%d" % gi)
        pbs = _pair_sum(gbs, ts, self._c, "reduce_pair_sum_%d" % gi)
        self._state[gi], token = _chip_start(pbs, list(smalls), ID_CHIP + gi, "reduce_chip_start_%d" % gi)
        return token

    def finish(self, gi, after):
        pbs, lands, smalls, send_sems, recv_sems = self._state[gi]
        pbs, qs, smalls = _chip_wait(pbs, lands, smalls, send_sems, recv_sems, after, "reduce_chip_wait_%d" % gi)
        names = REDUCE_GROUPS[gi]
        states = [self._adam.get(n) for n in names]
        self.results.update(zip(names, _chip_sum(pbs, qs, self._own, states, "reduce_update_%d" % gi)))
        if smalls:
            self.smalls = smalls

    def small_zones(self, small, g_wdw):
        pack = jnp.concatenate(
            [jnp.pad(small[name], ((0, 0), (0, lanes - small[name].shape[1]))) for name, lanes, _ in PACK], axis=1)
        zones = []
        for a in (pack, g_wdw):
            zone = lax.empty((NDEV,) + a.shape, a.dtype)
            zones.append(lax.dynamic_update_slice(zone, a[None], (self._me,) + (0,) * a.ndim))
        return zones


def _adamw_math(w, g, m, v):
    m = ADAM_B1 * m + (1.0 - ADAM_B1) * g
    v = ADAM_B2 * v + (1.0 - ADAM_B2) * (g * g)
    m_hat = m / (1.0 - ADAM_B1 ** ADAM_STEP)
    v_hat = v / (1.0 - ADAM_B2 ** ADAM_STEP)
    delta = -ADAM_LR * (m_hat / (jnp.sqrt(v_hat) + ADAM_EPS) + ADAM_WD * w)
    return delta, m, v


def _adamw(w, g, m, v, name):
    R, C = w.shape
    rc = 256 if R % 256 == 0 and R > 256 else R

    def body(w_ref, g_ref, m_ref, v_ref, d_ref, nm_ref, nv_ref):
        d, nm, nv = _adamw_math(w_ref[...], g_ref[...], m_ref[...], v_ref[...])
        d_ref[...] = d
        nm_ref[...] = nm
        nv_ref[...] = nv

    return _hbm_call(
        body, name=name, grid=(R // rc,),
        in_specs=[_rows(rc, C)] * 4, out_specs=[_rows(rc, C)] * 3,
        out_shape=[_sds((R, C), F32)] * 3,
        compiler_params=_params(40),
    )(w, g, m, v)


PACK = (("loss", LANES, 1), ("g_mix", D, D), ("b_in", INC, INC), ("b_dw", CC, CC), ("g_conv_ln", CC, CC),
        ("b_conv_ln", CC, CC), ("attn_sink", LANES, NQ), ("b_out", D, D), ("g_mem_q", D, D), ("g_mem_kv", D, D),
        ("g_ffn", D, D), ("g_final", D, D))
PACK_LANES = sum(p[1] for p in PACK)


def _small_update(packs, gdws, ws, ms, vs):
    names = [p[0] for p in PACK[1:]]
    n = len(names)

    def body(*refs):
        packs_ref, gdws_ref = refs[0], refs[1]
        w_refs, m_refs, v_refs = refs[2:2 + n], refs[2 + n:2 + 2 * n], refs[2 + 2 * n:2 + 3 * n]
        outs = refs[2 + 3 * n:]
        tot = packs_ref[0]
        gdw = gdws_ref[0]
        for s in range(1, NDEV):
            tot = tot + packs_ref[s]
            gdw = gdw + gdws_ref[s]
        outs[0][...] = tot[:, 0:LANES]
        outs[1][...] = gdw
        off = LANES
        for k, (_, lanes, used) in enumerate(PACK[1:]):
            g = tot[:, off:off + used]
            off += lanes
            d, nm, nv = _adamw_math(w_refs[k][...], g, m_refs[k][...], v_refs[k][...])
            o = outs[2 + 4 * k:6 + 4 * k]
            o[0][...] = g
            o[1][...] = d
            o[2][...] = nm
            o[3][...] = nv

    def full(a):
        nd = len(a.shape)
        return pl.BlockSpec(a.shape, lambda i: (0,) * nd)

    ins = [packs, gdws, *ws, *ms, *vs]
    out_shape = [_sds((1, LANES), F32), _sds(gdws.shape[1:], F32)]
    for w in ws:
        out_shape += [_sds(w.shape, F32)] * 4
    return pl.pallas_call(
        body, name="small_update", grid=(1,),
        in_specs=[full(a) for a in ins], out_specs=[full(o) for o in out_shape], out_shape=out_shape,
        compiler_params=_params(40),
    )(*ins)


def kernel(x, mem, g_mix, w_in, b_in, w_dw, b_dw, g_conv_ln, b_conv_ln, attn_sink, w_out, b_out, g_mem_q, g_mem_kv, w_mem_q, w_mem_kv, w_mem_o, g_ffn, w_gate, w_up, w_down, g_final, loss_target, m_g_mix, m_w_in, m_b_in, m_w_dw, m_b_dw, m_g_conv_ln, m_b_conv_ln, m_attn_sink, m_w_out, m_b_out, m_g_mem_q, m_g_mem_kv, m_w_mem_q, m_w_mem_kv, m_w_mem_o, m_g_ffn, m_w_gate, m_w_up, m_w_down, m_g_final, v_g_mix, v_w_in, v_b_in, v_w_dw, v_b_dw, v_g_conv_ln, v_b_conv_ln, v_attn_sink, v_w_out, v_b_out, v_g_mem_q, v_g_mem_kv, v_w_mem_q, v_w_mem_kv, v_w_mem_o, v_g_ffn, v_w_gate, v_w_up, v_w_down, v_g_final):
    given = dict(locals())
    bf = jnp.bfloat16
    xi, yi, ci = _place()
    me = 4 * xi + 2 * yi + ci

    transposed = {"w_inT": w_in, "w_mem_kvT": w_mem_kv, "w_gateT": w_gate, "w_upT": w_up}
    plain = {"w_out": w_out, "w_mem_q": w_mem_q, "w_mem_o": w_mem_o, "w_down": w_down}
    shards = [transposed[n][0].T if n in transposed else plain[n][0] for n in W_NAMES]
    zones = dict(zip(W_NAMES, _landing_zones(shards, me.astype(jnp.int32).reshape(1))))
    dw = jnp.pad(w_dw[0], ((0, 1), (0, 0)))
    zones["w_dw"] = lax.dynamic_update_slice(lax.empty((NDEV,) + dw.shape, dw.dtype), dw[None], (me, 0, 0))
    gather = _Gather(_gather_start([[zones[n] for n in names] for names in GATHER_GROUPS]))

    sp = {n: given[n] for n in ("g_mix", "b_in", "b_dw", "g_conv_ln", "b_conv_ln", "attn_sink", "b_out", "g_mem_q",
                                "g_mem_kv", "g_ffn")}
    sp["g_final"] = g_final.reshape(1, D)
    frames = {"w_inT": ("w_in", True), "w_gateT": ("w_gate", True), "w_upT": ("w_up", True),
              "w_out": ("w_out", False), "w_mem_q": ("w_mem_q", False), "w_mem_o": ("w_mem_o", False),
              "w_down": ("w_down", False)}
    states = {}
    for frame, (n, transposed) in frames.items():
        wmv = (given[n][0], given["m_" + n][0], given["v_" + n][0])
        states[frame] = tuple(a.T for a in wmv) if transposed else wmv
    reduce = _Reduce(states)
    grad_x = _local_step(x[0], mem[0], loss_target[0], sp, gather, reduce)

    grads, deltas, new_m, new_v = {}, {}, {}, {}

    def update(n, g):
        d, nm, nv = _adamw(given[n][0], g, given["m_" + n][0], given["v_" + n][0], "adamw_" + n)
        grads[n], deltas[n], new_m[n], new_v[n] = g[None], d[None], nm[None], nv[None]
        return d

    reduce.finish(2, update("w_mem_kv", reduce.results["w_mem_kvT"][0].T))
    for frame, (n, transposed) in frames.items():
        outs = [a.T if transposed else a for a in reduce.results[frame]]
        grads[n], deltas[n], new_m[n], new_v[n] = [a[None] for a in outs]

    packs, gdws = reduce.smalls
    names_small = [p[0] for p in PACK[1:]]

    def as_row(a):
        return a.reshape(1, -1)

    res = _small_update(packs, gdws, [as_row(given[n]) for n in names_small],
                        [as_row(given["m_" + n]) for n in names_small],
                        [as_row(given["v_" + n]) for n in names_small])
    loss = res[0][0, 0]
    for k, n in enumerate(names_small):
        shape = given[n].shape
        g, d, nm, nv = res[2 + 4 * k:6 + 4 * k]
        grads[n], deltas[n], new_m[n], new_v[n] = (g.reshape(shape), d.reshape(shape), nm.reshape(shape),
                                                   nv.reshape(shape))
    update("w_dw", lax.dynamic_slice(res[1], (0, me * (CC // NDEV)), (TAPS, CC // NDEV)))

    order = ("g_mix", "w_in", "b_in", "w_dw", "b_dw", "g_conv_ln", "b_conv_ln", "attn_sink", "w_out", "b_out",
             "g_mem_q", "g_mem_kv", "w_mem_q", "w_mem_kv", "w_mem_o", "g_ffn", "w_gate", "w_up", "w_down", "g_final")
    return (loss, grad_x[None], *[grads[n] for n in order], *[deltas[n] for n in order],
            *[new_m[n] for n in order], *[new_v[n] for n in order])
```

```python
import jax
import jax.numpy as jnp
from jax import lax
from jax.experimental import pallas as pl
from jax.experimental.pallas import tpu as pltpu

D = 1024
CC = 512
HD = 64
NQ = 8
NKV = 2
GRP = NQ // NKV
ATT = NQ * HD
KVC = NKV * HD
BLK = 128
MH = 4
MHD = D // MH
F = 2816
FCH = 256
FCH_BWD = F
INC = 2 * CC + ATT + 2 * KVC
TAPS = 31
EPS = 1e-6
ROPE_THETA = 10000.0
SCALE = HD ** -0.5
MSCALE = MHD ** -0.5
LANES = 128
NDEV = 8

ADAM_LR = 0.001
ADAM_B1 = 0.9
ADAM_B2 = 0.999
ADAM_EPS = 1e-08
ADAM_WD = 0.01
ADAM_STEP = 10

F32 = jnp.float32
_MXU = jnp.bfloat16
MESH = pl.DeviceIdType.MESH

W_NAMES = ("w_inT", "w_out", "w_mem_q", "w_mem_kvT", "w_mem_o", "w_gateT", "w_upT", "w_down")


def _mm(a, b, ca, cb):
    return lax.dot_general(a.astype(_MXU), b.astype(_MXU), (((ca,), (cb,)), ((), ())),
                           preferred_element_type=F32)


def _nn(a, b):
    return _mm(a, b, 1, 0)


def _nt(a, b):
    return _mm(a, b, 1, 1)


def _tn(a, b):
    return _mm(a, b, 0, 0)


def _sigmoid(x):
    return 0.5 * jnp.tanh(0.5 * x) + 0.5


def _rms(x, g):
    r = lax.rsqrt(jnp.mean(x * x, axis=-1, keepdims=True) + EPS)
    return x * r * g, r


def _rms_bwd(dy, x, r, g):
    xh = x * r
    dg = jnp.sum(dy * xh, axis=0, keepdims=True)
    dxh = dy * g
    dx = r * (dxh - xh * jnp.mean(dxh * xh, axis=-1, keepdims=True))
    return dx, dg


def _rope(t, c, s):
    n = t.shape[1]
    reps = n // LANES
    if reps > 1:
        c = jnp.tile(c, (1, reps))
        s = jnp.tile(s, (1, reps))
    lane = lax.broadcasted_iota(jnp.int32, t.shape, 1)
    first = (lane & (HD - 1)) < (HD // 2)
    partner = jnp.where(first, pltpu.roll(t, n - HD // 2, 1), pltpu.roll(t, HD // 2, 1))
    return t * c + partner * s


def _rope_bwd(dt, c, s):
    n = dt.shape[1]
    reps = n // LANES
    if reps > 1:
        c = jnp.tile(c, (1, reps))
        s = jnp.tile(s, (1, reps))
    lane = lax.broadcasted_iota(jnp.int32, dt.shape, 1)
    first = (lane & (HD - 1)) < (HD // 2)
    ds = dt * s
    partner = jnp.where(first, pltpu.roll(ds, n - HD // 2, 1), pltpu.roll(ds, HD // 2, 1))
    return dt * c + partner


def _acc_init(i, *refs):
    @pl.when(i == 0)
    def _():
        for r in refs:
            r[...] = jnp.zeros(r.shape, r.dtype)


def _acc_row(ref, v):
    ref[...] += jnp.broadcast_to(v, ref.shape)


def _rows(tm, n):
    return pl.BlockSpec((tm, n), lambda i: (i, 0))


def _full(shape):
    nd = len(shape)
    return pl.BlockSpec(shape, lambda i: (0,) * nd)


def _once(shape):
    nd = len(shape)
    return pl.BlockSpec(shape, lambda i: (0,) * nd, pipeline_mode=pl.Buffered(1))


def _params(vmem_mb):
    return pltpu.CompilerParams(dimension_semantics=("arbitrary",), vmem_limit_bytes=vmem_mb * 2 ** 20)


def _sds(shape, dtype):
    return pltpu.HBM(shape, dtype)


def _hbm_call(*args, **kwargs):
    call = pl.pallas_call(*args, **kwargs)
    return lambda *ops: call(*[pltpu.with_memory_space_constraint(o, pltpu.HBM) for o in ops])


def _in_proj_fwd(x, g_mix, winT, b_in, cosq, sinq, tm=512):
    S = x.shape[0]

    def body(x_ref, g_ref, w_ref, b_ref, c_ref, s_ref, h_ref, uglu_ref, vg_ref, q_ref, k_ref, v_ref):
        h, _ = _rms(x_ref[...], g_ref[...])
        hb = h.astype(h_ref.dtype)
        h_ref[...] = hb
        u = _nt(hb, w_ref[...]) + b_ref[...]
        uglu_ref[...] = u[:, :2 * CC].astype(uglu_ref.dtype)
        vg_ref[...] = u[:, :CC] * _sigmoid(u[:, CC:2 * CC])
        c = c_ref[...]
        s = s_ref[...]
        q_ref[...] = (_rope(u[:, 2 * CC:2 * CC + ATT], c, s) * SCALE).astype(q_ref.dtype)
        k_ref[...] = _rope(u[:, 2 * CC + ATT:2 * CC + ATT + KVC], c, s).astype(k_ref.dtype)
        v_ref[...] = u[:, 2 * CC + ATT + KVC:].astype(v_ref.dtype)

    return _hbm_call(
        body, name="in_proj_fwd", grid=(S // tm,),
        in_specs=[_rows(tm, D), _full((1, D)), _full((INC, D)), _full((1, INC)), _rows(tm, LANES), _rows(tm, LANES)],
        out_specs=[_rows(tm, D), _rows(tm, 2 * CC), _rows(tm, CC), _rows(tm, ATT), _rows(tm, KVC), _rows(tm, KVC)],
        out_shape=[_sds((S, D), _MXU), _sds((S, 2 * CC), _MXU), _sds((S, CC), F32), _sds((S, ATT), _MXU),
                   _sds((S, KVC), _MXU), _sds((S, KVC), _MXU)],
        compiler_params=_params(48),
    )(x, g_mix, winT, b_in, cosq, sinq)


CONV_HALO = 16


CONV_TILE = 512


def _shift_scratch(tm, sets):
    return pltpu.VMEM((sets, CC // LANES, 8, tm + 24, LANES), F32)


def _shifted_copies(pad_ref, sh_ref, t0, tm):
    for cl in range(CC // LANES):
        blk = pad_ref[pl.ds(t0, tm + 2 * CONV_HALO), cl * LANES:(cl + 1) * LANES]
        for b in range(8):
            sh_ref[cl, b, :, :] = blk[b:b + tm + 24, :]


def _conv_taps():
    return [(8 * a + b - 1, b, 8 * a) for b in range(8) for a in range(4) if 0 <= 8 * a + b - 1 < TAPS]


CONV_ROWS = 64
SUBLANES = 8


def _dwconv_tile(sh_ref, tm, w_ref, flip, out_ref):
    for cl in range(CC // LANES):
        lanes = slice(cl * LANES, (cl + 1) * LANES)

        def chunk(c, carry):
            r = pl.multiple_of(c * CONV_ROWS, CONV_ROWS)
            accs = [None] * 4
            for n, (k, b, off) in enumerate(_conv_taps()):
                kk = TAPS - 1 - k if flip else k
                term = sh_ref[cl, b, pl.ds(off + r, CONV_ROWS), :] * w_ref[kk:kk + 1, lanes]
                accs[n % 4] = term if accs[n % 4] is None else accs[n % 4] + term
            out_ref[pl.ds(r, CONV_ROWS), lanes] = (accs[0] + accs[1]) + (accs[2] + accs[3])
            return carry

        lax.fori_loop(0, tm // CONV_ROWS, chunk, 0)


def _fill_padded(i, src_hbm, pad_ref, sem, S):
    @pl.when(i == 0)
    def _():
        zeros = jnp.zeros((CONV_HALO, pad_ref.shape[1]), pad_ref.dtype)
        pad_ref[0:CONV_HALO, :] = zeros
        pad_ref[CONV_HALO + S:2 * CONV_HALO + S, :] = zeros
        cp = pltpu.make_async_copy(src_hbm, pad_ref.at[pl.ds(CONV_HALO, S), :], sem)
        cp.start()
        cp.wait()


def _ln_silu(y, g, b):
    mu = jnp.mean(y, axis=-1, keepdims=True)
    yc = y - mu
    var = jnp.mean(yc * yc, axis=-1, keepdims=True)
    rstd = lax.rsqrt(var + EPS)
    yh = yc * rstd
    z = yh * g + b
    return yh, rstd, z


def _conv_fwd(vg, w_dw, b_dw, ln_g, ln_b):
    S = vg.shape[0]
    tm = min(CONV_TILE, S)

    def body(vg_hbm, w_ref, b_ref, g_ref, bb_ref, y_ref, yc_ref, pad_ref, sh_ref, sem):
        i = pl.program_id(0)
        _fill_padded(i, vg_hbm, pad_ref, sem, S)
        _shifted_copies(pad_ref, sh_ref.at[0], pl.multiple_of(i * tm, tm), tm)
        _dwconv_tile(sh_ref.at[0], tm, w_ref, False, y_ref)
        y = y_ref[...] + b_ref[...]
        y_ref[...] = y
        _, _, z = _ln_silu(y, g_ref[...], bb_ref[...])
        yc_ref[...] = (z * _sigmoid(z)).astype(yc_ref.dtype)

    return _hbm_call(
        body, name="conv_fwd", grid=(S // tm,),
        in_specs=[pl.BlockSpec(memory_space=pl.ANY), _full((TAPS, CC)), _full((1, CC)), _full((1, CC)), _full((1, CC))],
        out_specs=[_rows(tm, CC), _rows(tm, CC)],
        out_shape=[_sds((S, CC), F32), _sds((S, CC), _MXU)],
        scratch_shapes=[pltpu.VMEM((S + 2 * CONV_HALO, CC), F32), _shift_scratch(tm, 1), pltpu.SemaphoreType.DMA],
        compiler_params=_params(40),
    )(vg, w_dw, b_dw, ln_g, ln_b)


def _attn_window(i, S):
    start = jnp.clip((i - 1) * BLK, 0, S - 3 * BLK)
    return pl.multiple_of(start, BLK)


ATTN_COLS = GRP * BLK


def _attn_mask_base():
    keys = lax.broadcasted_iota(jnp.int32, (3 * BLK, ATTN_COLS), 0)
    cols = lax.broadcasted_iota(jnp.int32, (3 * BLK, ATTN_COLS), 1)
    return keys - (cols & (BLK - 1))


def _attn_mask(base, i, start):
    return jnp.abs(base + (start - i * BLK)) <= BLK


def _stack_heads(t, hk):
    return jnp.concatenate([t[:, (hk * GRP + g) * HD:(hk * GRP + g + 1) * HD] for g in range(GRP)], axis=0)


def _unstack_heads_t(t):
    cols = []
    for p in range(GRP // 2):
        pair = jnp.concatenate([t[:, (2 * p) * BLK:(2 * p + 1) * BLK], t[:, (2 * p + 1) * BLK:(2 * p + 2) * BLK]],
                               axis=0)
        cols.append(pair.T)
    return jnp.concatenate(cols, axis=1)


def _attn_probs(qs, kh, sk_ref, hk, mask):
    s = jnp.where(mask, _nt(kh, qs), -1e30)
    sk = jnp.concatenate(
        [jnp.broadcast_to(sk_ref[0:1, hk * GRP + g:hk * GRP + g + 1], (1, BLK)) for g in range(GRP)], axis=1)
    m = jnp.maximum(jnp.max(s, axis=0, keepdims=True), sk)
    e = jnp.exp(s - m)
    es = jnp.exp(sk - m)
    inv = 1.0 / (jnp.sum(e, axis=0, keepdims=True) + es)
    return e, inv, es


ATTN_BLOCKS = 8


def _attn_fwd(qr, kr, vv, sink):
    S = qr.shape[0]
    tq = ATTN_BLOCKS * BLK

    def body(q_ref, k_ref, v_ref, sk_ref, o_ref):
        base = _attn_mask_base()
        for b in range(ATTN_BLOCKS):
            i = pl.program_id(0) * ATTN_BLOCKS + b
            rows = slice(b * BLK, (b + 1) * BLK)
            start = _attn_window(i, S)
            kb = k_ref[pl.ds(start, 3 * BLK), :]
            vb = v_ref[pl.ds(start, 3 * BLK), :]
            q = q_ref[rows, :]
            mask = _attn_mask(base, i, start)
            outs = []
            for hk in range(NKV):
                hs = slice(hk * HD, (hk + 1) * HD)
                e, inv, _ = _attn_probs(_stack_heads(q, hk), kb[:, hs], sk_ref, hk, mask)
                outs.append(_unstack_heads_t(_tn(vb[:, hs], e) * inv))
            o_ref[rows, :] = jnp.concatenate(outs, axis=1).astype(o_ref.dtype)

    return _hbm_call(
        body, name="attn_fwd", grid=(S // tq,),
        in_specs=[_rows(tq, ATT), _full((S, KVC)), _full((S, KVC)), _full((1, NQ))],
        out_specs=_rows(tq, ATT),
        out_shape=_sds((S, ATT), _MXU),
        compiler_params=_params(40),
    )(qr, kr, vv, sink)


def _out_proj_fwd(x, yc, ya, wout, b_out, tm=512):
    S = x.shape[0]

    def body(x_ref, yc_ref, ya_ref, w_ref, b_ref, x1_ref, ymix_ref):
        ymix = jnp.concatenate([yc_ref[...], ya_ref[...]], axis=1)
        ymix_ref[...] = ymix
        x1_ref[...] = x_ref[...] + _nn(ymix, w_ref[...]) + b_ref[...]

    return _hbm_call(
        body, name="out_proj_fwd", grid=(S // tm,),
        in_specs=[_rows(tm, D), _rows(tm, CC), _rows(tm, ATT), _full((D, D)), _full((1, D))],
        out_specs=[_rows(tm, D), _rows(tm, D)],
        out_shape=[_sds((S, D), F32), _sds((S, D), _MXU)],
        compiler_params=_params(40),
    )(x, yc, ya, wout, b_out)


def _softmax_t(s):
    m = jnp.max(s, axis=0, keepdims=True)
    e = jnp.exp(s - m)
    return e, 1.0 / jnp.sum(e, axis=0, keepdims=True)


def _softmax(s):
    m = jnp.max(s, axis=-1, keepdims=True)
    e = jnp.exp(s - m)
    return e * (1.0 / jnp.sum(e, axis=-1, keepdims=True))


def _mem_attn_fwd(x1, g_q, wq, mem, g_kv, wkvT, wo, after, tm=512):
    S = x1.shape[0]
    M = mem.shape[0]

    def body(x1_ref, g_ref, wq_ref, mem_ref, gkv_ref, wkv_ref, wo_ref, after_ref,
             x2_ref, qm_ref, om_ref, km_ref, vm_ref, mn_ref):
        @pl.when(pl.program_id(0) == 0)
        def _():
            memn, _ = _rms(mem_ref[...], gkv_ref[...])
            kv = _nt(memn, wkv_ref[...])
            km_ref[...] = kv[:, :D].astype(km_ref.dtype)
            vm_ref[...] = kv[:, D:].astype(vm_ref.dtype)
            mn_ref[...] = memn.astype(mn_ref.dtype)

        x1v = x1_ref[...]
        hq, _ = _rms(x1v, g_ref[...])
        qm = (_nn(hq, wq_ref[...]) * MSCALE).astype(qm_ref.dtype)
        qm_ref[...] = qm
        oms = []
        for h in range(MH):
            sl = slice(h * MHD, (h + 1) * MHD)
            p = _softmax(_nt(qm[:, sl], km_ref[:, sl]))
            oms.append(_nn(p, vm_ref[:, sl]))
        om = jnp.concatenate(oms, axis=1).astype(om_ref.dtype)
        om_ref[...] = om
        x2_ref[...] = x1v + _nn(om, wo_ref[...])

    return _hbm_call(
        body, name="mem_attn_fwd", grid=(S // tm,),
        in_specs=[_rows(tm, D), _full((1, D)), _full((D, D)), _full((M, D)), _full((1, D)), _once((2 * D, D)),
                  _full((D, D)), _AFTER],
        out_specs=[_rows(tm, D)] * 3 + [_full((M, D))] * 3,
        out_shape=[_sds((S, D), F32), _sds((S, D), _MXU), _sds((S, D), _MXU)] + [_sds((M, D), _MXU)] * 3,
        compiler_params=_params(48),
    )(x1, g_q, wq, mem, g_kv, wkvT, wo, after)


def _ffn_loss_fwd(x2, tgt, g_ffn, g_final, wgT, wuT, wd, tm=512):
    S = x2.shape[0]

    def body(x2_ref, t_ref, gf_ref, gl_ref, wg_ref, wu_ref, wd_ref,
             gt_ref, up_ref, hf_ref, dx3_ref, dx3b_ref, loss_ref, dgl_ref):
        i = pl.program_id(0)
        _acc_init(i, loss_ref, dgl_ref)
        x2v = x2_ref[...]
        hf, _ = _rms(x2v, gf_ref[...])
        hb = hf.astype(hf_ref.dtype)
        hf_ref[...] = hb
        x3 = x2v
        for c in range(F // FCH):
            sl = slice(c * FCH, (c + 1) * FCH)
            gt = _nt(hb, wg_ref[sl, :])
            up = _nt(hb, wu_ref[sl, :])
            gt_ref[:, sl] = gt.astype(gt_ref.dtype)
            up_ref[:, sl] = up.astype(up_ref.dtype)
            x3 = x3 + _nn(gt * _sigmoid(gt) * up, wd_ref[sl, :])
        gl = gl_ref[...]
        y, r = _rms(x3, gl)
        diff = y - t_ref[...]
        _acc_row(loss_ref, jnp.sum(jnp.sum(diff * diff, axis=-1, keepdims=True), axis=0, keepdims=True) * (0.5 / D))
        dx3, dgl = _rms_bwd(diff * (1.0 / D), x3, r, gl)
        dx3_ref[...] = dx3
        dx3b_ref[...] = dx3.astype(dx3b_ref.dtype)
        _acc_row(dgl_ref, dgl)

    return _hbm_call(
        body, name="ffn_loss_fwd", grid=(S // tm,),
        in_specs=[_rows(tm, D), _rows(tm, D), _full((1, D)), _full((1, D)),
                  _once((F, D)), _once((F, D)), _once((F, D))],
        out_specs=[_rows(tm, F), _rows(tm, F), _rows(tm, D), _rows(tm, D), _rows(tm, D),
                   _full((8, LANES)), _full((8, D))],
        out_shape=[_sds((S, F), _MXU), _sds((S, F), _MXU), _sds((S, D), _MXU), _sds((S, D), F32),
                   _sds((S, D), _MXU), _sds((8, LANES), F32), _sds((8, D), F32)],
        compiler_params=_params(56),
    )(x2, tgt, g_ffn, g_final, wgT, wuT, wd)


def _ffn_bwd(dx3, x2, gt, up, g_ffn, wd, wgT, wuT, tm=256):
    S = x2.shape[0]

    def body(dx3_ref, x2_ref, gt_ref, up_ref, g_ref, wd_ref, wg_ref, wu_ref,
             dx2_ref, dx2b_ref, dgt_ref, dup_ref, act_ref, dg_ref):
        i = pl.program_id(0)
        _acc_init(i, dg_ref)
        dx3v = dx3_ref[...]
        db = dx3v.astype(_MXU)
        dhf = jnp.zeros((tm, D), F32)
        for c in range(F // FCH_BWD):
            sl = slice(c * FCH_BWD, (c + 1) * FCH_BWD)
            gtv = gt_ref[:, sl].astype(F32)
            upv = up_ref[:, sl].astype(F32)
            dact = _nt(db, wd_ref[sl, :])
            sig = _sigmoid(gtv)
            silu = gtv * sig
            dup = (dact * silu).astype(dup_ref.dtype)
            dgt = (dact * upv * (sig * (1.0 + gtv * (1.0 - sig)))).astype(dgt_ref.dtype)
            act_ref[:, sl] = (silu * upv).astype(act_ref.dtype)
            dgt_ref[:, sl] = dgt
            dup_ref[:, sl] = dup
            dhf = dhf + _nn(dgt, wg_ref[sl, :]) + _nn(dup, wu_ref[sl, :])
        x2v = x2_ref[...]
        g = g_ref[...]
        _, r = _rms(x2v, g)
        dxn, dg = _rms_bwd(dhf, x2v, r, g)
        dx2 = dx3v + dxn
        dx2_ref[...] = dx2
        dx2b_ref[...] = dx2.astype(dx2b_ref.dtype)
        _acc_row(dg_ref, dg)

    return _hbm_call(
        body, name="ffn_bwd", grid=(S // tm,),
        in_specs=[_rows(tm, D), _rows(tm, D), _rows(tm, F), _rows(tm, F), _full((1, D)),
                  _once((F, D)), _once((F, D)), _once((F, D))],
        out_specs=[_rows(tm, D), _rows(tm, D), _rows(tm, F), _rows(tm, F), _rows(tm, F), _full((8, D))],
        out_shape=[_sds((S, D), F32), _sds((S, D), _MXU), _sds((S, F), _MXU), _sds((S, F), _MXU),
                   _sds((S, F), _MXU), _sds((8, D), F32)],
        compiler_params=_params(56),
    )(dx3, x2, gt, up, g_ffn, wd, wgT, wuT)


_AFTER = pl.BlockSpec(memory_space=pl.ANY)


def _wgrad(pairs, name, after, rc=256):
    S, R = pairs[0][0].shape
    N = pairs[0][1].shape[1]
    rights = []
    for _, b in pairs:
        if not any(b is r for r in rights):
            rights.append(b)
    which = [next(i for i, r in enumerate(rights) if r is b) for _, b in pairs]
    n, nb = len(pairs), len(rights)

    def body(*refs):
        for k in range(n):
            refs[n + nb + 1 + k][...] = _tn(refs[k][...], refs[n + which[k]][...]).astype(jnp.bfloat16)

    res = _hbm_call(
        body, name=name, grid=(R // rc,),
        in_specs=[pl.BlockSpec((S, rc), lambda j: (0, j))] * n + [_once((S, N))] * nb + [_AFTER],
        out_specs=[_rows(rc, N)] * n,
        out_shape=[_sds((R, N), jnp.bfloat16)] * n,
        compiler_params=_params(56),
    )(*[a for a, _ in pairs], *rights, after)
    return list(res)


def _mem_attn_bwd(dx2, x1, qm, km, vm, g_q, wq, wo, wout, y, ln_g, ln_b, after, tm=512):
    S = x1.shape[0]
    M = km.shape[0]

    def body(dx2_ref, x1_ref, qm_ref, km_ref, vm_ref, g_ref, wq_ref, wo_ref, wout_ref, y_ref, lg_ref, lb_ref,
             after_ref, dx1_ref, dx1b_ref, hq_ref, dqm_ref, dy_ref, dya_ref, dkm_ref, dvm_ref, dg_ref, dbo_ref,
             dlg_ref, dlb_ref, dbdw_ref):
        i = pl.program_id(0)
        _acc_init(i, dkm_ref, dvm_ref, dg_ref, dbo_ref, dlg_ref, dlb_ref, dbdw_ref)
        dx2v = dx2_ref[...]
        domt = _nt(wo_ref[...], dx2v).astype(_MXU)
        dqs = []
        for h in range(MH):
            sl = slice(h * MHD, (h + 1) * MHD)
            qh = qm_ref[:, sl]
            kh = km_ref[:, sl]
            e, inv = _softmax_t(_nt(kh, qh))
            p = e * inv
            dp = _nn(vm_ref[:, sl], domt[sl, :])
            ds = (p * (dp - jnp.sum(dp * p, axis=0, keepdims=True))).astype(_MXU)
            dqs.append(_tn(ds, kh) * MSCALE)
            dkm_ref[:, sl] += _nn(ds, qh)
            dvm_ref[:, sl] += _nt(p, domt[sl, :])
        dqm = jnp.concatenate(dqs, axis=1).astype(dqm_ref.dtype)
        dqm_ref[...] = dqm
        dhq = _nt(dqm, wq_ref[...])
        x1v = x1_ref[...]
        g = g_ref[...]
        hq, r = _rms(x1v, g)
        hq_ref[...] = hq.astype(hq_ref.dtype)
        dxn, dg = _rms_bwd(dhq, x1v, r, g)
        dx1 = dx2v + dxn
        dx1_ref[...] = dx1
        dx1b = dx1.astype(dx1b_ref.dtype)
        dx1b_ref[...] = dx1b
        dym = _nt(dx1b, wout_ref[...])
        dya_ref[...] = dym[:, CC:].astype(dya_ref.dtype)
        _acc_row(dg_ref, dg)
        _acc_row(dbo_ref, jnp.sum(dx1, axis=0, keepdims=True))
        lg = lg_ref[...]
        yh, rstd, z = _ln_silu(y_ref[...], lg, lb_ref[...])
        sig = _sigmoid(z)
        dz = dym[:, :CC] * (sig * (1.0 + z * (1.0 - sig)))
        dyh = dz * lg
        dy = rstd * (dyh - jnp.mean(dyh, axis=-1, keepdims=True) - yh * jnp.mean(dyh * yh, axis=-1, keepdims=True))
        dy_ref[...] = dy
        _acc_row(dlg_ref, jnp.sum(dz * yh, axis=0, keepdims=True))
        _acc_row(dlb_ref, jnp.sum(dz, axis=0, keepdims=True))
        _acc_row(dbdw_ref, jnp.sum(dy, axis=0, keepdims=True))

    return _hbm_call(
        body, name="mem_attn_bwd", grid=(S // tm,),
        in_specs=[_rows(tm, D), _rows(tm, D), _rows(tm, D), _full((M, D)), _full((M, D)), _full((1, D)),
                  _once((D, D)), _once((D, D)), _once((D, D)), _rows(tm, CC), _full((1, CC)), _full((1, CC)),
                  _AFTER],
        out_specs=[_rows(tm, D), _rows(tm, D), _rows(tm, D), _rows(tm, D), _rows(tm, CC), _rows(tm, ATT),
                   _full((M, D)), _full((M, D)), _full((8, D)), _full((8, D)),
                   _full((8, CC)), _full((8, CC)), _full((8, CC))],
        out_shape=[_sds((S, D), F32), _sds((S, D), _MXU), _sds((S, D), _MXU), _sds((S, D), _MXU),
                   _sds((S, CC), F32), _sds((S, ATT), _MXU),
                   _sds((M, D), F32), _sds((M, D), F32), _sds((8, D), F32), _sds((8, D), F32),
                   _sds((8, CC), F32), _sds((8, CC), F32), _sds((8, CC), F32)],
        compiler_params=_params(56),
    )(dx2, x1, qm, km, vm, g_q, wq, wo, wout, y, ln_g, ln_b, after)


def _mem_kv_bwd(dkm, dvm, mem, g_kv, memn, wkvT):
    M = mem.shape[0]

    def body(dkm_ref, dvm_ref, mem_ref, g_ref, mn_ref, w_ref, gw_ref, dg_ref):
        dkv = jnp.concatenate([dkm_ref[...], dvm_ref[...]], axis=1).astype(_MXU)
        gw_ref[...] = _tn(dkv, mn_ref[...]).astype(gw_ref.dtype)
        dmn = _nn(dkv, w_ref[...])
        memv = mem_ref[...]
        g = g_ref[...]
        _, r = _rms(memv, g)
        _, dg = _rms_bwd(dmn, memv, r, g)
        dg_ref[...] = jnp.broadcast_to(dg, dg_ref.shape)

    return _hbm_call(
        body, name="mem_kv_bwd", grid=(1,),
        in_specs=[_full((M, D)), _full((M, D)), _full((M, D)), _full((1, D)), _full((M, D)), _full((2 * D, D))],
        out_specs=[_full((2 * D, D)), _full((8, D))],
        out_shape=[_sds((2 * D, D), jnp.bfloat16), _sds((8, D), F32)],
        compiler_params=_params(48),
    )(dkm, dvm, mem, g_kv, memn, wkvT)


def _attn_bwd(qr, kr, vv, sink, dya, after):
    S = qr.shape[0]
    tq = ATTN_BLOCKS * BLK

    def body(q_ref, k_ref, v_ref, sk_ref, do_ref, after_ref, dq_ref, dk_ref, dv_ref, dsk_ref, sink_acc):
        _acc_init(pl.program_id(0), dk_ref, dv_ref, sink_acc)
        base = _attn_mask_base()
        for b in range(ATTN_BLOCKS):
            i = pl.program_id(0) * ATTN_BLOCKS + b
            rows = slice(b * BLK, (b + 1) * BLK)
            start = _attn_window(i, S)
            kb = k_ref[pl.ds(start, 3 * BLK), :]
            vb = v_ref[pl.ds(start, 3 * BLK), :]
            q = q_ref[rows, :]
            do = do_ref[rows, :]
            mask = _attn_mask(base, i, start)
            dqs, dks, dvs = [], [], []
            for hk in range(NKV):
                hs = slice(hk * HD, (hk + 1) * HD)
                qs = _stack_heads(q, hk)
                dos = _stack_heads(do, hk)
                e, inv, es = _attn_probs(qs, kb[:, hs], sk_ref, hk, mask)
                p = e * inv
                dp = _nt(vb[:, hs], dos)
                delta = jnp.sum(dp * p, axis=0, keepdims=True)
                ds = (p * (dp - delta)).astype(_MXU)
                dqs.append(_unstack_heads_t(_tn(kb[:, hs], ds) * SCALE))
                dks.append(_nn(ds, qs))
                dvs.append(_nn(p, dos))
                sink_acc[hk:hk + 1, :] -= es * inv * delta
            dq_ref[rows, :] = jnp.concatenate(dqs, axis=1)
            dk_ref[pl.ds(start, 3 * BLK), :] += jnp.concatenate(dks, axis=1)
            dv_ref[pl.ds(start, 3 * BLK), :] += jnp.concatenate(dvs, axis=1)

        @pl.when(pl.program_id(0) == pl.num_programs(0) - 1)
        def _():
            for h in range(NQ):
                tot = jnp.sum(sink_acc[h // GRP:h // GRP + 1, (h % GRP) * BLK:(h % GRP + 1) * BLK], axis=1,
                              keepdims=True)
                dsk_ref[h:h + 1, :] = jnp.broadcast_to(tot, (1, LANES))

    return _hbm_call(
        body, name="attn_bwd", grid=(S // tq,),
        in_specs=[_rows(tq, ATT), _full((S, KVC)), _full((S, KVC)), _full((1, NQ)), _rows(tq, ATT), _AFTER],
        out_specs=[_rows(tq, ATT), _full((S, KVC)), _full((S, KVC)), _full((8, LANES))],
        out_shape=[_sds((S, ATT), F32), _sds((S, KVC), F32), _sds((S, KVC), F32), _sds((8, LANES), F32)],
        scratch_shapes=[pltpu.VMEM((8, ATTN_COLS), F32)],
        compiler_params=_params(40),
    )(qr, kr, vv, sink, dya, after)


def _conv_bwd(dy, vg, uglu, w_dw, after):
    S = dy.shape[0]
    tm = min(CONV_TILE, S)

    def body(dy_hbm, vg_hbm, uglu_ref, w_ref, after_ref, du_ref, gw_ref, dbin_ref, dyp_ref, vgp_ref, gacc_ref, sh_ref,
             dvg_ref, sems):
        i = pl.program_id(0)
        _acc_init(i, gacc_ref, dbin_ref)
        _fill_padded(i, dy_hbm, dyp_ref, sems.at[0], S)
        _fill_padded(i, vg_hbm, vgp_ref, sems.at[1], S)
        t0 = pl.multiple_of(i * tm, tm)
        _shifted_copies(dyp_ref, sh_ref.at[0], t0, tm)
        _shifted_copies(vgp_ref, sh_ref.at[1], t0, tm)
        _dwconv_tile(sh_ref.at[0], tm, w_ref, True, dvg_ref)
        dvg = dvg_ref[...]
        sub = CONV_ROWS
        for cl in range(CC // LANES):
            lanes = slice(cl * LANES, (cl + 1) * LANES)
            for k, b, off in _conv_taps():
                part = jnp.zeros((8, LANES), F32)
                for r in range(0, tm, sub):
                    prod = (sh_ref[0, cl, 0, CONV_HALO + r:CONV_HALO + r + sub, :]
                            * sh_ref[1, cl, b, off + r:off + r + sub, :])
                    part = part + jnp.sum(prod.reshape(sub // 8, 8, LANES), axis=0)
                gacc_ref[k, :, lanes] += part
        a_v = uglu_ref[:, :CC].astype(F32)
        sg = _sigmoid(uglu_ref[:, CC:].astype(F32))
        da = dvg * sg
        dgate = dvg * a_v * sg * (1.0 - sg)
        du = jnp.concatenate([da, dgate], axis=1)
        du_ref[...] = du.astype(du_ref.dtype)
        _acc_row(dbin_ref, jnp.sum(du, axis=0, keepdims=True))

        @pl.when(i == pl.num_programs(0) - 1)
        def _():
            gw_ref[...] = jnp.zeros(gw_ref.shape, F32)
            for k in range(TAPS):
                gw_ref[k:k + 1, :] = jnp.sum(gacc_ref[k], axis=0, keepdims=True)

    return _hbm_call(
        body, name="conv_bwd", grid=(S // tm,),
        in_specs=[pl.BlockSpec(memory_space=pl.ANY), pl.BlockSpec(memory_space=pl.ANY), _rows(tm, 2 * CC),
                  _full((TAPS, CC)), _AFTER],
        out_specs=[_rows(tm, 2 * CC), _full((TAPS + 1, CC)), _full((8, 2 * CC))],
        out_shape=[_sds((S, 2 * CC), _MXU), _sds((TAPS + 1, CC), F32), _sds((8, 2 * CC), F32)],
        scratch_shapes=[pltpu.VMEM((S + 2 * CONV_HALO, CC), F32), pltpu.VMEM((S + 2 * CONV_HALO, CC), F32),
                        pltpu.VMEM((TAPS, 8, CC), F32), _shift_scratch(tm, 2), pltpu.VMEM((tm, CC), F32),
                        pltpu.SemaphoreType.DMA((2,))],
        compiler_params=_params(56),
    )(dy, vg, uglu, w_dw, after)


def _in_proj_bwd(duglu, dqr, dkr, dv, cosq, sinq, x, dx1, g_mix, winT, tm=512):
    S = x.shape[0]
    NA = ATT + 2 * KVC

    def body(dug_ref, dq_ref, dk_ref, dv_ref, c_ref, s_ref, x_ref, dx1_ref, g_ref, w_ref,
             dx_ref, du_ref, dg_ref, dbin_ref):
        i = pl.program_id(0)
        _acc_init(i, dg_ref, dbin_ref)
        c = c_ref[...]
        s = s_ref[...]
        datt = jnp.concatenate([_rope_bwd(dq_ref[...], c, s), _rope_bwd(dk_ref[...], c, s), dv_ref[...]], axis=1)
        _acc_row(dbin_ref, jnp.sum(datt, axis=0, keepdims=True))
        du = jnp.concatenate([dug_ref[...], datt.astype(du_ref.dtype)], axis=1)
        du_ref[...] = du
        dh = _nn(du, w_ref[...])
        xv = x_ref[...]
        g = g_ref[...]
        _, r = _rms(xv, g)
        dxn, dg = _rms_bwd(dh, xv, r, g)
        dx_ref[...] = dx1_ref[...] + dxn
        _acc_row(dg_ref, dg)

    return _hbm_call(
        body, name="in_proj_bwd", grid=(S // tm,),
        in_specs=[_rows(tm, 2 * CC), _rows(tm, ATT), _rows(tm, KVC), _rows(tm, KVC), _rows(tm, LANES),
                  _rows(tm, LANES), _rows(tm, D), _rows(tm, D), _full((1, D)), _once((INC, D))],
        out_specs=[_rows(tm, D), _rows(tm, INC), _full((8, D)), _full((8, NA))],
        out_shape=[_sds((S, D), F32), _sds((S, INC), _MXU), _sds((8, D), F32), _sds((8, NA), F32)],
        compiler_params=_params(48),
    )(duglu, dqr, dkr, dv, cosq, sinq, x, dx1, g_mix, winT)


def _rope_tables(S):
    pos = jnp.arange(S, dtype=F32)
    inv_freq = ROPE_THETA ** (-jnp.arange(0, HD, 2, dtype=F32) / HD)
    ang = pos[:, None] * inv_freq[None, :]
    cos, sin = jnp.cos(ang), jnp.sin(ang)
    cosq = jnp.tile(jnp.concatenate([cos, cos], axis=1), (1, LANES // HD))
    sinq = jnp.tile(jnp.concatenate([-sin, sin], axis=1), (1, LANES // HD))
    return cosq, sinq


GATHER_GROUPS = (("w_inT", "w_dw"), ("w_out", "w_mem_kvT", "w_mem_q", "w_mem_o"), ("w_gateT", "w_upT"), ("w_down",))
ID_GATHER, ID_SIBLING, ID_CHIP = 1, 5, 8


def _local_step(x, mem, tgt, sp, gather, reduce):
    S = x.shape[0]
    assert S % (ATTN_BLOCKS * BLK) == 0 and S % 512 == 0, S
    cosq, sinq = _rope_tables(S)

    gather.forward(0, cosq)
    w0 = gather.finish(0, sinq)
    winT, w_dw_full = w0["w_inT"], w0["w_dw"]
    h, uglu, vg, qr, kr, vv = _in_proj_fwd(x, sp["g_mix"], winT, sp["b_in"], cosq, sinq)
    ya = _attn_fwd(qr, kr, vv, sp["attn_sink"])
    gather.forward(1, ya)
    y, yc = _conv_fwd(vg, w_dw_full, sp["b_dw"], sp["g_conv_ln"], sp["b_conv_ln"])
    w1 = gather.finish(1, yc)
    wout, wkvT, wq, wo = w1["w_out"], w1["w_mem_kvT"], w1["w_mem_q"], w1["w_mem_o"]
    x1, ymix = _out_proj_fwd(x, yc, ya, wout, sp["b_out"])
    token = gather.forward(2, x1)
    x2, qm, om, km, vm, memn = _mem_attn_fwd(x1, sp["g_mem_q"], wq, mem, sp["g_mem_kv"], wkvT, wo, token)
    w2 = gather.finish(2, om)
    gather.forward(3, om)
    wgT, wuT, wd = w2["w_gateT"], w2["w_upT"], gather.finish(3, om)["w_down"]
    gt, up, hf, dx3, dx3b, loss_acc, dg_final = _ffn_loss_fwd(x2, tgt, sp["g_ffn"], sp["g_final"], wgT, wuT, wd)

    dx2, dx2b, dgt, dup, act, dg_ffn = _ffn_bwd(dx3, x2, gt, up, sp["g_ffn"], wd, wgT, wuT)
    token = reduce.start(0, _wgrad([(dgt, hf), (dup, hf), (act, dx3b)], "wgrad_ffn", dx2))

    dx1, dx1b, hq, dqm, dy, dya, dkm, dvm, dg_mem_q, db_out, dg_cln, db_cln, db_dw = _mem_attn_bwd(
        dx2, x1, qm, km, vm, sp["g_mem_q"], wq, wo, wout, y, sp["g_conv_ln"], sp["b_conv_ln"], token)
    token = reduce.middle(0, dx1)
    g_wout, g_wq, g_wo = _wgrad([(ymix, dx1b), (hq, dqm), (om, dx2b)], "wgrad_mix_mem", token)
    g_wkvT, dg_mem_kv = _mem_kv_bwd(dkm, dvm, mem, sp["g_mem_kv"], memn, wkvT)
    token = reduce.start(1, [g_wout, g_wq, g_wkvT, g_wo])

    dqr, dkr, dvv, dsink = _attn_bwd(qr, kr, vv, sp["attn_sink"], dya, token)
    token = reduce.middle(1, dqr)
    duglu, g_wdw, dbin_glu = _conv_bwd(dy, vg, uglu, w_dw_full, token)
    grad_x, du, dg_mix, dbin_att = _in_proj_bwd(duglu, dqr, dkr, dvv, cosq, sinq, x, dx1, sp["g_mix"], winT)
    token = reduce.start(2, _wgrad([(du, h)], "wgrad_in", grad_x))

    small = {
        "loss": loss_acc[0:1, :],
        "g_mix": dg_mix[0:1], "b_in": jnp.concatenate([dbin_glu[0:1], dbin_att[0:1]], axis=1),
        "b_dw": db_dw[0:1], "g_conv_ln": dg_cln[0:1], "b_conv_ln": db_cln[0:1],
        "attn_sink": dsink[:, 0].reshape(1, NQ), "b_out": db_out[0:1], "g_mem_q": dg_mem_q[0:1],
        "g_mem_kv": dg_mem_kv[0:1], "g_ffn": dg_ffn[0:1], "g_final": dg_final[0:1],
    }
    token = reduce.middle(2, token, reduce.small_zones(small, g_wdw))
    reduce.finish(0, token)
    reduce.finish(1, token)
    return grad_x


def _place():
    return lax.axis_index("x"), lax.axis_index("y"), lax.axis_index("c")


_HBM = pl.BlockSpec(memory_space=pltpu.HBM)
_SEM = pl.BlockSpec(memory_space=pltpu.SEMAPHORE)
_ANY = pl.BlockSpec(memory_space=pl.ANY)
_EFFECT = pltpu.SideEffectType.DATAFLOW_SIDE_EFFECTING
_VMEM = pl.BlockSpec(memory_space=pltpu.VMEM)
_TOKEN = jax.ShapeDtypeStruct((8, LANES), F32)


def _dev_index(p):
    return 4 * p[0] + 2 * p[1] + p[2]


def _landing_zones(shards, me_idx):
    def body(m_ref, *refs):
        n = len(refs) // 2
        for k in range(n):
            refs[n + k][...] = refs[k][...].astype(jnp.bfloat16)

    return pl.pallas_call(
        body, name="landing_zones",
        grid_spec=pltpu.PrefetchScalarGridSpec(
            num_scalar_prefetch=1, grid=(1,),
            in_specs=[pl.BlockSpec(s.shape, lambda i, m: (0, 0)) for s in shards],
            out_specs=[pl.BlockSpec((None,) + s.shape, lambda i, m: (m[0], 0, 0)) for s in shards]),
        out_shape=[pltpu.HBM((NDEV,) + s.shape, jnp.bfloat16) for s in shards],
        compiler_params=_params(40),
    )(me_idx, *shards)


def _gather_start(groups):
    sizes = [len(g) for g in groups]
    flat = [z for g in groups for z in g]
    n = len(flat)

    def body(*refs):
        ins = refs[:n]
        sems = refs[2 * n:]
        x, y, c = _place()
        me = _dev_index((x, y, c))
        peers = [(x, y, 1 - c), (1 - x, y, c), (x, 1 - y, c), (1 - x, 1 - y, c)]
        base = 0
        for gi, size in enumerate(sizes):
            send_sems, recv_sems = sems[2 * gi], sems[2 * gi + 1]
            for k in range(size):
                zone = ins[base + k]
                for j, peer in enumerate(peers):
                    pltpu.make_async_remote_copy(
                        src_ref=zone.at[me], dst_ref=zone.at[me], send_sem=send_sems.at[4 * k + j],
                        recv_sem=recv_sems.at[4 * k + j], device_id=peer, device_id_type=MESH).start()
            base += size

    out_shape = [pltpu.HBM(z.shape, z.dtype) for z in flat]
    out_specs = [_HBM] * n
    for size in sizes:
        out_shape += [pltpu.SemaphoreType.DMA((4 * size,))] * 2
        out_specs += [_SEM, _SEM]
    res = pl.pallas_call(
        body, name="gather_start", in_specs=[_HBM] * n, out_specs=out_specs, out_shape=out_shape,
        input_output_aliases={k: k for k in range(n)},
        compiler_params=pltpu.CompilerParams(has_side_effects=_EFFECT),
    )(*[pltpu.with_memory_space_constraint(z, pltpu.HBM) for z in flat])
    out, base = [], 0
    for gi, size in enumerate(sizes):
        out.append((list(res[base:base + size]), res[n + 2 * gi], res[n + 2 * gi + 1]))
        base += size
    return out


def _gather_forward(zones, send_sems, recv_sems, after, collective_id, name):
    n = len(zones)

    def body(*refs):
        ins = refs[:n]
        send_ref, recv_ref = refs[n], refs[n + 1]
        fsend, frecv = refs[2 * n + 3], refs[2 * n + 4]
        x, y, c = _place()
        sibling = (x, y, 1 - c)
        barrier = pltpu.get_barrier_semaphore()
        pl.semaphore_signal(barrier, inc=1, device_id=sibling, device_id_type=MESH)
        pl.semaphore_wait(barrier, 1)
        chips = [(1 - x, y), (x, 1 - y), (1 - x, 1 - y)]
        for j, chip in enumerate(chips):
            for k in range(n):
                slot = ins[k].at[_dev_index((*chip, c))]
                pltpu.make_async_remote_copy(
                    src_ref=slot, dst_ref=slot, send_sem=send_ref.at[4 * k + 1 + j],
                    recv_sem=recv_ref.at[4 * k + 1 + j], device_id=sibling, device_id_type=MESH).wait_recv()
                pltpu.make_async_remote_copy(
                    src_ref=slot, dst_ref=slot, send_sem=fsend.at[3 * k + j], recv_sem=frecv.at[3 * k + j],
                    device_id=sibling, device_id_type=MESH).start()
        for k in range(n):
            own = ins[k].at[_dev_index((x, y, c))]
            theirs = ins[k].at[_dev_index(sibling)]
            pltpu.make_async_remote_copy(
                src_ref=theirs, dst_ref=theirs, send_sem=send_ref.at[4 * k], recv_sem=recv_ref.at[4 * k],
                device_id=sibling, device_id_type=MESH).wait_recv()
            for j in range(4):
                pltpu.make_async_remote_copy(
                    src_ref=own, dst_ref=own, send_sem=send_ref.at[4 * k + j], recv_sem=recv_ref.at[4 * k + j],
                    device_id=sibling, device_id_type=MESH).wait_send()
        refs[2 * n + 5][...] = jnp.zeros(_TOKEN.shape, _TOKEN.dtype)

    res = pl.pallas_call(
        body, name=name, in_specs=[_HBM] * n + [_SEM, _SEM, _ANY],
        out_specs=[_HBM] * n + [_SEM, _SEM, _VMEM],
        out_shape=[pltpu.HBM(z.shape, z.dtype) for z in zones] + [pltpu.SemaphoreType.DMA((3 * n,))] * 2 + [_TOKEN],
        input_output_aliases={k: k for k in range(n)},
        compiler_params=pltpu.CompilerParams(has_side_effects=_EFFECT, collective_id=collective_id),
    )(*zones, send_sems, recv_sems, after)
    return (list(res[:n]), res[n], res[n + 1]), res[n + 2]


def _gather_finish(zones, fsend, frecv, after, name):
    n = len(zones)

    def body(*refs):
        ins = refs[:n]
        fsend_ref, frecv_ref = refs[n], refs[n + 1]
        x, y, c = _place()
        sibling = (x, y, 1 - c)
        chips = [(1 - x, y), (x, 1 - y), (1 - x, 1 - y)]
        for j, chip in enumerate(chips):
            for k in range(n):
                mine = ins[k].at[_dev_index((*chip, c))]
                theirs = ins[k].at[_dev_index((*chip, 1 - c))]
                pltpu.make_async_remote_copy(
                    src_ref=theirs, dst_ref=theirs, send_sem=fsend_ref.at[3 * k + j],
                    recv_sem=frecv_ref.at[3 * k + j], device_id=sibling, device_id_type=MESH).wait_recv()
                pltpu.make_async_remote_copy(
                    src_ref=mine, dst_ref=mine, send_sem=fsend_ref.at[3 * k + j],
                    recv_sem=frecv_ref.at[3 * k + j], device_id=sibling, device_id_type=MESH).wait_send()

    res = pl.pallas_call(
        body, name=name, in_specs=[_HBM] * n + [_SEM, _SEM, _ANY], out_specs=[_HBM] * n,
        out_shape=[pltpu.HBM(z.shape, z.dtype) for z in zones],
        input_output_aliases={k: k for k in range(n)},
        compiler_params=pltpu.CompilerParams(has_side_effects=_EFFECT),
    )(*zones, fsend, frecv, after)
    return list(res)


class _Gather:
    def __init__(self, started):
        self._phase = list(started)

    def forward(self, gi, after):
        zones, send_sems, recv_sems = self._phase[gi]
        self._phase[gi], token = _gather_forward(zones, send_sems, recv_sems, after, ID_GATHER + gi,
                                                  "gather_forward_%d" % gi)
        return token

    def finish(self, gi, after):
        zones, fsend, frecv = self._phase[gi]
        out = {}
        for name, z in zip(GATHER_GROUPS[gi], _gather_finish(zones, fsend, frecv, after, "gather_finish_%d" % gi)):
            if name == "w_dw":
                out[name] = jnp.transpose(z, (1, 0, 2)).reshape(TAPS + 1, CC)[:TAPS]
            else:
                out[name] = z.reshape(NDEV * z.shape[1], D)
        return out


def _handshake(peers):
    barrier = pltpu.get_barrier_semaphore()
    for peer in peers:
        pl.semaphore_signal(barrier, inc=1, device_id=peer, device_id_type=MESH)
    pl.semaphore_wait(barrier, len(peers))


def _sibling_copies(srcs, lands, send_sems, recv_sems):
    x, y, c = _place()
    return [pltpu.make_async_remote_copy(
        src_ref=srcs[k].at[2 * s + (1 - c)], dst_ref=lands[k].at[s], send_sem=send_sems.at[4 * k + s],
        recv_sem=recv_sems.at[4 * k + s], device_id=(x, y, 1 - c), device_id_type=MESH)
        for k in range(len(srcs)) for s in range(4)]


def _sibling_start(gbs, collective_id, name):
    n = len(gbs)
    lands = [lax.empty((4,) + g.shape[1:], g.dtype) for g in gbs]

    def body(*refs):
        x, y, c = _place()
        _handshake([(x, y, 1 - c)])
        for cp in _sibling_copies(refs[:n], refs[n:2 * n], refs[4 * n], refs[4 * n + 1]):
            cp.start()
        refs[4 * n + 2][...] = jnp.zeros(_TOKEN.shape, _TOKEN.dtype)

    res = pl.pallas_call(
        body, name=name, in_specs=[_HBM] * (2 * n), out_specs=[_HBM] * (2 * n) + [_SEM, _SEM, _VMEM],
        out_shape=[pltpu.HBM(a.shape, a.dtype) for a in gbs + lands] + [pltpu.SemaphoreType.DMA((4 * n,))] * 2
        + [_TOKEN],
        input_output_aliases={k: k for k in range(2 * n)},
        compiler_params=pltpu.CompilerParams(has_side_effects=_EFFECT, collective_id=collective_id),
    )(*[pltpu.with_memory_space_constraint(a, pltpu.HBM) for a in gbs + lands])
    return list(res[:n]), list(res[n:2 * n]), res[2 * n], res[2 * n + 1], res[2 * n + 2]


def _sibling_wait(gbs, lands, send_sems, recv_sems, after, name):
    n = len(gbs)

    def body(*refs):
        for cp in _sibling_copies(refs[:n], refs[n:2 * n], refs[2 * n], refs[2 * n + 1]):
            cp.wait_recv()
            cp.wait_send()

    res = pl.pallas_call(
        body, name=name, in_specs=[_HBM] * (2 * n) + [_SEM, _SEM, _ANY], out_specs=[_HBM] * (2 * n),
        out_shape=[pltpu.HBM(a.shape, a.dtype) for a in gbs + lands],
        input_output_aliases={k: k for k in range(2 * n)},
        compiler_params=pltpu.CompilerParams(has_side_effects=_EFFECT),
    )(*gbs, *lands, send_sems, recv_sems, after)
    return list(res[:n]), list(res[n:])


def _pair_sum(gfs, ts, c_idx, name):
    n = len(gfs)
    halves = 1

    def body(c_ref, *refs):
        for k in range(n):
            refs[2 * n + k][...] = (refs[k][...].astype(F32) + refs[n + k][...].astype(F32)).astype(jnp.bfloat16)

    in_specs, out_specs, out_shape = [], [], []
    for g in gfs:
        rh = g.shape[1] // halves
        in_specs.append(pl.BlockSpec((None, rh, D), lambda s, h, c_ref: (2 * s + c_ref[0], h, 0)))
    for g in gfs:
        rh = g.shape[1] // halves
        in_specs.append(pl.BlockSpec((None, rh, D), lambda s, h, c_ref: (s, h, 0)))
        out_specs.append(pl.BlockSpec((None, rh, D), lambda s, h, c_ref: (s, h, 0)))
        out_shape.append(_sds((4, g.shape[1], D), jnp.bfloat16))
    return pl.pallas_call(
        body, name=name,
        grid_spec=pltpu.PrefetchScalarGridSpec(num_scalar_prefetch=1, grid=(4, halves), in_specs=in_specs,
                                               out_specs=out_specs),
        out_shape=out_shape,
        compiler_params=pltpu.CompilerParams(dimension_semantics=("arbitrary", "arbitrary"),
                                             vmem_limit_bytes=48 * 2 ** 20),
    )(c_idx, *gfs, *ts)


def _chip_copies(pbs, lands, smalls, send_sems, recv_sems):
    x, y, c = _place()
    own = 2 * x + y
    me = _dev_index((x, y, c))
    chips = [(1 - x, y), (x, 1 - y), (1 - x, 1 - y)]
    n = len(pbs)
    copies = [pltpu.make_async_remote_copy(
        src_ref=pbs[k].at[2 * chip[0] + chip[1]], dst_ref=lands[k].at[own], send_sem=send_sems.at[3 * k + j],
        recv_sem=recv_sems.at[3 * k + j], device_id=(*chip, c), device_id_type=MESH)
        for k in range(n) for j, chip in enumerate(chips)]
    peers = [(x, y, 1 - c)] + [(*chip, cc) for chip in chips for cc in (c, 1 - c)]
    copies += [pltpu.make_async_remote_copy(
        src_ref=smalls[t].at[me], dst_ref=smalls[t].at[me], send_sem=send_sems.at[3 * n + 7 * t + j],
        recv_sem=recv_sems.at[3 * n + 7 * t + j], device_id=peer, device_id_type=MESH)
        for t in range(len(smalls)) for j, peer in enumerate(peers)]
    return copies


def _chip_start(pbs, smalls, collective_id, name):
    n, m = len(pbs), len(smalls)
    lands = [lax.empty(p.shape, p.dtype) for p in pbs]
    arrays = list(pbs) + lands + list(smalls)
    na = len(arrays)
    nsem = 3 * n + 7 * m

    def body(*refs):
        x, y, c = _place()
        chips = [(1 - x, y), (x, 1 - y), (1 - x, 1 - y)]
        peers = [(*chip, c) for chip in chips]
        if m:
            peers += [(x, y, 1 - c)] + [(*chip, 1 - c) for chip in chips]
        _handshake(peers)
        for cp in _chip_copies(refs[:n], refs[n:2 * n], refs[2 * n:na], refs[2 * na], refs[2 * na + 1]):
            cp.start()
        refs[2 * na + 2][...] = jnp.zeros(_TOKEN.shape, _TOKEN.dtype)

    res = pl.pallas_call(
        body, name=name, in_specs=[_HBM] * na, out_specs=[_HBM] * na + [_SEM, _SEM, _VMEM],
        out_shape=[pltpu.HBM(a.shape, a.dtype) for a in arrays] + [pltpu.SemaphoreType.DMA((nsem,))] * 2 + [_TOKEN],
        input_output_aliases={k: k for k in range(na)},
        compiler_params=pltpu.CompilerParams(has_side_effects=_EFFECT, collective_id=collective_id),
    )(*[pltpu.with_memory_space_constraint(a, pltpu.HBM) for a in arrays])
    return (list(res[:n]), list(res[n:2 * n]), list(res[2 * n:na]), res[na], res[na + 1]), res[na + 2]


def _chip_wait(pbs, lands, smalls, send_sems, recv_sems, after, name):
    n = len(pbs)
    arrays = list(pbs) + list(lands) + list(smalls)
    na = len(arrays)

    def body(*refs):
        for cp in _chip_copies(refs[:n], refs[n:2 * n], refs[2 * n:na], refs[na], refs[na + 1]):
            cp.wait_recv()
            cp.wait_send()

    res = pl.pallas_call(
        body, name=name, in_specs=[_HBM] * na + [_SEM, _SEM, _ANY], out_specs=[_HBM] * na,
        out_shape=[pltpu.HBM(a.shape, a.dtype) for a in arrays],
        input_output_aliases={k: k for k in range(na)},
        compiler_params=pltpu.CompilerParams(has_side_effects=_EFFECT),
    )(*arrays, send_sems, recv_sems, after)
    return list(res[:n]), list(res[n:2 * n]), list(res[2 * n:])


def _chip_sum(pbs, qs, own_idx, states, name):
    n = len(pbs)
    halves = 2
    n_in = 4 * n + 3 * sum(s is not None for s in states)

    def body(o_ref, *refs):
        nxt_in, nxt_out = 4 * n, n_in
        for k in range(n):
            acc = refs[4 * k][...].astype(F32)
            for j in range(1, 4):
                acc = acc + refs[4 * k + j][...].astype(F32)
            refs[nxt_out][...] = acc
            nxt_out += 1
            if states[k] is not None:
                w_ref, m_ref, v_ref = refs[nxt_in:nxt_in + 3]
                nxt_in += 3
                d, nm, nv = _adamw_math(w_ref[...], acc, m_ref[...], v_ref[...])
                refs[nxt_out][...] = d
                refs[nxt_out + 1][...] = nm
                refs[nxt_out + 2][...] = nv
                nxt_out += 3

    def other(j):
        return lambda h, o: ((o[0] + j) % 4, h, 0)

    def rows(r):
        return pl.BlockSpec((r // halves, D), lambda h, o: (h, 0))

    in_specs, ins = [], []
    for p, q in zip(pbs, qs):
        rh = p.shape[1] // halves
        in_specs.append(pl.BlockSpec((None, rh, D), lambda h, o: (o[0], h, 0)))
        ins.append(p)
        for j in range(1, 4):
            in_specs.append(pl.BlockSpec((None, rh, D), other(j)))
            ins.append(q)
    out_specs, out_shape, counts = [], [], []
    for p, s in zip(pbs, states):
        r = p.shape[1]
        if s is not None:
            in_specs += [rows(r)] * 3
            ins += list(s)
        counts.append(1 if s is None else 4)
        out_specs += [rows(r)] * counts[-1]
        out_shape += [_sds((r, D), F32)] * counts[-1]
    res = pl.pallas_call(
        body, name=name,
        grid_spec=pltpu.PrefetchScalarGridSpec(num_scalar_prefetch=1, grid=(halves,), in_specs=in_specs,
                                               out_specs=out_specs),
        out_shape=out_shape,
        compiler_params=pltpu.CompilerParams(dimension_semantics=("arbitrary",), vmem_limit_bytes=56 * 2 ** 20),
    )(own_idx, *ins)
    out, base = [], 0
    for c in counts:
        out.append(tuple(res[base:base + c]))
        base += c
    return out


REDUCE_GROUPS = (("w_gateT", "w_upT", "w_down"), ("w_out", "w_mem_q", "w_mem_kvT", "w_mem_o"), ("w_inT",))


class _Reduce:
    def __init__(self, states):
        x, y, c = _place()
        self._c = c.astype(jnp.int32).reshape(1)
        self._own = (2 * x + y).astype(jnp.int32).reshape(1)
        self._me = _dev_index((x, y, c))
        self._state = {}
        self._adam = states
        self.results = {}
        self.smalls = None

    def start(self, gi, wgrads):
        gbs = [g.reshape(NDEV, g.shape[0] // NDEV, D) for g in wgrads]
        gbs, lands, send_sems, recv_sems, token = _sibling_start(gbs, ID_SIBLING + gi, "reduce_sibling_start_%d" % gi)
        self._state[gi] = (gbs, lands, send_sems, recv_sems)
        return token

    def middle(self, gi, after, smalls=()):
        gbs, lands, send_sems, recv_sems = self._state[gi]
        gbs, ts = _sibling_wait(gbs, lands, send_sems, recv_sems, after, "reduce_sibling_wait_%d" % gi)
        pbs = _pair_sum(gbs, ts, self._c, "reduce_pair_sum_%d" % gi)
        self._state[gi], token = _chip_start(pbs, list(smalls), ID_CHIP + gi, "reduce_chip_start_%d" % gi)
        return token

    def finish(self, gi, after):
        pbs, lands, smalls, send_sems, recv_sems = self._state[gi]
        pbs, qs, smalls = _chip_wait(pbs, lands, smalls, send_sems, recv_sems, after, "reduce_chip_wait_%d" % gi)
        names = REDUCE_GROUPS[gi]
        states = [self._adam.get(n) for n in names]
        self.results.update(zip(names, _chip_sum(pbs, qs, self._own, states, "reduce_update_%d" % gi)))
        if smalls:
            self.smalls = smalls

    def small_zones(self, small, g_wdw):
        pack = jnp.concatenate(
            [jnp.pad(small[name], ((0, 0), (0, lanes - small[name].shape[1]))) for name, lanes, _ in PACK], axis=1)
        zones = []
        for a in (pack, g_wdw):
            zone = lax.empty((NDEV,) + a.shape, a.dtype)
            zones.append(lax.dynamic_update_slice(zone, a[None], (self._me,) + (0,) * a.ndim))
        return zones


def _adamw_math(w, g, m, v):
    m = ADAM_B1 * m + (1.0 - ADAM_B1) * g
    v = ADAM_B2 * v + (1.0 - ADAM_B2) * (g * g)
    m_hat = m / (1.0 - ADAM_B1 ** ADAM_STEP)
    v_hat = v / (1.0 - ADAM_B2 ** ADAM_STEP)
    delta = -ADAM_LR * (m_hat / (jnp.sqrt(v_hat) + ADAM_EPS) + ADAM_WD * w)
    return delta, m, v


def _adamw(w, g, m, v, name):
    R, C = w.shape
    rc = 256 if R % 256 == 0 and R > 256 else R

    def body(w_ref, g_ref, m_ref, v_ref, d_ref, nm_ref, nv_ref):
        d, nm, nv = _adamw_math(w_ref[...], g_ref[...], m_ref[...], v_ref[...])
        d_ref[...] = d
        nm_ref[...] = nm
        nv_ref[...] = nv

    return _hbm_call(
        body, name=name, grid=(R // rc,),
        in_specs=[_rows(rc, C)] * 4, out_specs=[_rows(rc, C)] * 3,
        out_shape=[_sds((R, C), F32)] * 3,
        compiler_params=_params(40),
    )(w, g, m, v)


PACK = (("loss", LANES, 1), ("g_mix", D, D), ("b_in", INC, INC), ("b_dw", CC, CC), ("g_conv_ln", CC, CC),
        ("b_conv_ln", CC, CC), ("attn_sink", LANES, NQ), ("b_out", D, D), ("g_mem_q", D, D), ("g_mem_kv", D, D),
        ("g_ffn", D, D), ("g_final", D, D))
PACK_LANES = sum(p[1] for p in PACK)


def _small_update(packs, gdws, ws, ms, vs):
    names = [p[0] for p in PACK[1:]]
    n = len(names)

    def body(*refs):
        packs_ref, gdws_ref = refs[0], refs[1]
        w_refs, m_refs, v_refs = refs[2:2 + n], refs[2 + n:2 + 2 * n], refs[2 + 2 * n:2 + 3 * n]
        outs = refs[2 + 3 * n:]
        tot = packs_ref[0]
        gdw = gdws_ref[0]
        for s in range(1, NDEV):
            tot = tot + packs_ref[s]
            gdw = gdw + gdws_ref[s]
        outs[0][...] = tot[:, 0:LANES]
        outs[1][...] = gdw
        off = LANES
        for k, (_, lanes, used) in enumerate(PACK[1:]):
            g = tot[:, off:off + used]
            off += lanes
            d, nm, nv = _adamw_math(w_refs[k][...], g, m_refs[k][...], v_refs[k][...])
            o = outs[2 + 4 * k:6 + 4 * k]
            o[0][...] = g
            o[1][...] = d
            o[2][...] = nm
            o[3][...] = nv

    def full(a):
        nd = len(a.shape)
        return pl.BlockSpec(a.shape, lambda i: (0,) * nd)

    ins = [packs, gdws, *ws, *ms, *vs]
    out_shape = [_sds((1, LANES), F32), _sds(gdws.shape[1:], F32)]
    for w in ws:
        out_shape += [_sds(w.shape, F32)] * 4
    return pl.pallas_call(
        body, name="small_update", grid=(1,),
        in_specs=[full(a) for a in ins], out_specs=[full(o) for o in out_shape], out_shape=out_shape,
        compiler_params=_params(40),
    )(*ins)


def kernel(x, mem, g_mix, w_in, b_in, w_dw, b_dw, g_conv_ln, b_conv_ln, attn_sink, w_out, b_out, g_mem_q, g_mem_kv, w_mem_q, w_mem_kv, w_mem_o, g_ffn, w_gate, w_up, w_down, g_final, loss_target, m_g_mix, m_w_in, m_b_in, m_w_dw, m_b_dw, m_g_conv_ln, m_b_conv_ln, m_attn_sink, m_w_out, m_b_out, m_g_mem_q, m_g_mem_kv, m_w_mem_q, m_w_mem_kv, m_w_mem_o, m_g_ffn, m_w_gate, m_w_up, m_w_down, m_g_final, v_g_mix, v_w_in, v_b_in, v_w_dw, v_b_dw, v_g_conv_ln, v_b_conv_ln, v_attn_sink, v_w_out, v_b_out, v_g_mem_q, v_g_mem_kv, v_w_mem_q, v_w_mem_kv, v_w_mem_o, v_g_ffn, v_w_gate, v_w_up, v_w_down, v_g_final):
    given = dict(locals())
    bf = jnp.bfloat16
    xi, yi, ci = _place()
    me = 4 * xi + 2 * yi + ci

    transposed = {"w_inT": w_in, "w_mem_kvT": w_mem_kv, "w_gateT": w_gate, "w_upT": w_up}
    plain = {"w_out": w_out, "w_mem_q": w_mem_q, "w_mem_o": w_mem_o, "w_down": w_down}
    shards = [transposed[n][0].T if n in transposed else plain[n][0] for n in W_NAMES]
    zones = dict(zip(W_NAMES, _landing_zones(shards, me.astype(jnp.int32).reshape(1))))
    dw = jnp.pad(w_dw[0], ((0, 1), (0, 0)))
    zones["w_dw"] = lax.dynamic_update_slice(lax.empty((NDEV,) + dw.shape, dw.dtype), dw[None], (me, 0, 0))
    gather = _Gather(_gather_start([[zones[n] for n in names] for names in GATHER_GROUPS]))

    sp = {n: given[n] for n in ("g_mix", "b_in", "b_dw", "g_conv_ln", "b_conv_ln", "attn_sink", "b_out", "g_mem_q",
                                "g_mem_kv", "g_ffn")}
    sp["g_final"] = g_final.reshape(1, D)
    frames = {"w_inT": ("w_in", True), "w_gateT": ("w_gate", True), "w_upT": ("w_up", True),
              "w_out": ("w_out", False), "w_mem_q": ("w_mem_q", False), "w_mem_o": ("w_mem_o", False),
              "w_down": ("w_down", False)}
    states = {}
    for frame, (n, transposed) in frames.items():
        wmv = (given[n][0], given["m_" + n][0], given["v_" + n][0])
        states[frame] = tuple(a.T for a in wmv) if transposed else wmv
    reduce = _Reduce(states)
    grad_x = _local_step(x[0], mem[0], loss_target[0], sp, gather, reduce)

    grads, deltas, new_m, new_v = {}, {}, {}, {}

    def update(n, g):
        d, nm, nv = _adamw(given[n][0], g, given["m_" + n][0], given["v_" + n][0], "adamw_" + n)
        grads[n], deltas[n], new_m[n], new_v[n] = g[None], d[None], nm[None], nv[None]
        return d

    reduce.finish(2, update("w_mem_kv", reduce.results["w_mem_kvT"][0].T))
    for frame, (n, transposed) in frames.items():
        outs = [a.T if transposed else a for a in reduce.results[frame]]
        grads[n], deltas[n], new_m[n], new_v[n] = [a[None] for a in outs]

    packs, gdws = reduce.smalls
    names_small = [p[0] for p in PACK[1:]]

    def as_row(a):
        return a.reshape(1, -1)

    res = _small_update(packs, gdws, [as_row(given[n]) for n in names_small],
                        [as_row(given["m_" + n]) for n in names_small],
                        [as_row(given["v_" + n]) for n in names_small])
    loss = res[0][0, 0]
    for k, n in enumerate(names_small):
        shape = given[n].shape
        g, d, nm, nv = res[2 + 4 * k:6 + 4 * k]
        grads[n], deltas[n], new_m[n], new_v[n] = (g.reshape(shape), d.reshape(shape), nm.reshape(shape),
                                                   nv.reshape(shape))
    update("w_dw", lax.dynamic_slice(res[1], (0, me * (CC // NDEV)), (TAPS, CC // NDEV)))

    order = ("g_mix", "w_in", "b_in", "w_dw", "b_dw", "g_conv_ln", "b_conv_ln", "attn_sink", "w_out", "b_out",
             "g_mem_q", "g_mem_kv", "w_mem_q", "w_mem_kv", "w_mem_o", "g_ffn", "w_gate", "w_up", "w_down", "g_final")
    return (loss, grad_x[None], *[grads[n] for n in order], *[deltas[n] for n in order],
            *[new_m[n] for n in order], *[new_v[n] for n in order])
```

```python
import jax
import jax.numpy as jnp
from jax import lax
from jax.experimental import pallas as pl
from jax.experimental.pallas import tpu as pltpu

D = 1024
CC = 512
HD = 64
NQ = 8
NKV = 2
GRP = NQ // NKV
ATT = NQ * HD
KVC = NKV * HD
BLK = 128
MH = 4
MHD = D // MH
F = 2816
FCH = 256
FCH_BWD = F
INC = 2 * CC + ATT + 2 * KVC
TAPS = 31
EPS = 1e-6
ROPE_THETA = 10000.0
SCALE = HD ** -0.5
MSCALE = MHD ** -0.5
LANES = 128
NDEV = 8

ADAM_LR = 0.001
ADAM_B1 = 0.9
ADAM_B2 = 0.999
ADAM_EPS = 1e-08
ADAM_WD = 0.01
ADAM_STEP = 10

F32 = jnp.float32
_MXU = jnp.bfloat16
MESH = pl.DeviceIdType.MESH

W_NAMES = ("w_inT", "w_out", "w_mem_q", "w_mem_kvT", "w_mem_o", "w_gateT", "w_upT", "w_down")


def _mm(a, b, ca, cb):
    return lax.dot_general(a.astype(_MXU), b.astype(_MXU), (((ca,), (cb,)), ((), ())),
                           preferred_element_type=F32)


def _nn(a, b):
    return _mm(a, b, 1, 0)


def _nt(a, b):
    return _mm(a, b, 1, 1)


def _tn(a, b):
    return _mm(a, b, 0, 0)


def _sigmoid(x):
    return 0.5 * jnp.tanh(0.5 * x) + 0.5


def _rms(x, g):
    r = lax.rsqrt(jnp.mean(x * x, axis=-1, keepdims=True) + EPS)
    return x * r * g, r


def _rms_bwd(dy, x, r, g):
    xh = x * r
    dg = jnp.sum(dy * xh, axis=0, keepdims=True)
    dxh = dy * g
    dx = r * (dxh - xh * jnp.mean(dxh * xh, axis=-1, keepdims=True))
    return dx, dg


def _rope(t, c, s):
    n = t.shape[1]
    reps = n // LANES
    if reps > 1:
        c = jnp.tile(c, (1, reps))
        s = jnp.tile(s, (1, reps))
    lane = lax.broadcasted_iota(jnp.int32, t.shape, 1)
    first = (lane & (HD - 1)) < (HD // 2)
    partner = jnp.where(first, pltpu.roll(t, n - HD // 2, 1), pltpu.roll(t, HD // 2, 1))
    return t * c + partner * s


def _rope_bwd(dt, c, s):
    n = dt.shape[1]
    reps = n // LANES
    if reps > 1:
        c = jnp.tile(c, (1, reps))
        s = jnp.tile(s, (1, reps))
    lane = lax.broadcasted_iota(jnp.int32, dt.shape, 1)
    first = (lane & (HD - 1)) < (HD // 2)
    ds = dt * s
    partner = jnp.where(first, pltpu.roll(ds, n - HD // 2, 1), pltpu.roll(ds, HD // 2, 1))
    return dt * c + partner


def _acc_init(i, *refs):
    @pl.when(i == 0)
    def _():
        for r in refs:
            r[...] = jnp.zeros(r.shape, r.dtype)


def _acc_row(ref, v):
    ref[...] += jnp.broadcast_to(v, ref.shape)


def _rows(tm, n):
    return pl.BlockSpec((tm, n), lambda i: (i, 0))


def _full(shape):
    nd = len(shape)
    return pl.BlockSpec(shape, lambda i: (0,) * nd)


def _once(shape):
    nd = len(shape)
    return pl.BlockSpec(shape, lambda i: (0,) * nd, pipeline_mode=pl.Buffered(1))


def _params(vmem_mb):
    return pltpu.CompilerParams(dimension_semantics=("arbitrary",), vmem_limit_bytes=vmem_mb * 2 ** 20)


def _sds(shape, dtype):
    return pltpu.HBM(shape, dtype)


def _hbm_call(*args, **kwargs):
    call = pl.pallas_call(*args, **kwargs)
    return lambda *ops: call(*[pltpu.with_memory_space_constraint(o, pltpu.HBM) for o in ops])


def _in_proj_fwd(x, g_mix, winT, b_in, cosq, sinq, tm=512):
    S = x.shape[0]

    def body(x_ref, g_ref, w_ref, b_ref, c_ref, s_ref, h_ref, uglu_ref, vg_ref, q_ref, k_ref, v_ref):
        h, _ = _rms(x_ref[...], g_ref[...])
        hb = h.astype(h_ref.dtype)
        h_ref[...] = hb
        u = _nt(hb, w_ref[...]) + b_ref[...]
        uglu_ref[...] = u[:, :2 * CC].astype(uglu_ref.dtype)
        vg_ref[...] = u[:, :CC] * _sigmoid(u[:, CC:2 * CC])
        c = c_ref[...]
        s = s_ref[...]
        q_ref[...] = (_rope(u[:, 2 * CC:2 * CC + ATT], c, s) * SCALE).astype(q_ref.dtype)
        k_ref[...] = _rope(u[:, 2 * CC + ATT:2 * CC + ATT + KVC], c, s).astype(k_ref.dtype)
        v_ref[...] = u[:, 2 * CC + ATT + KVC:].astype(v_ref.dtype)

    return _hbm_call(
        body, name="in_proj_fwd", grid=(S // tm,),
        in_specs=[_rows(tm, D), _full((1, D)), _full((INC, D)), _full((1, INC)), _rows(tm, LANES), _rows(tm, LANES)],
        out_specs=[_rows(tm, D), _rows(tm, 2 * CC), _rows(tm, CC), _rows(tm, ATT), _rows(tm, KVC), _rows(tm, KVC)],
        out_shape=[_sds((S, D), _MXU), _sds((S, 2 * CC), _MXU), _sds((S, CC), F32), _sds((S, ATT), _MXU),
                   _sds((S, KVC), _MXU), _sds((S, KVC), _MXU)],
        compiler_params=_params(48),
    )(x, g_mix, winT, b_in, cosq, sinq)


CONV_HALO = 16


CONV_TILE = 512


def _shift_scratch(tm, sets):
    return pltpu.VMEM((sets, CC // LANES, 8, tm + 24, LANES), F32)


def _shifted_copies(pad_ref, sh_ref, t0, tm):
    for cl in range(CC // LANES):
        blk = pad_ref[pl.ds(t0, tm + 2 * CONV_HALO), cl * LANES:(cl + 1) * LANES]
        for b in range(8):
            sh_ref[cl, b, :, :] = blk[b:b + tm + 24, :]


def _conv_taps():
    return [(8 * a + b - 1, b, 8 * a) for b in range(8) for a in range(4) if 0 <= 8 * a + b - 1 < TAPS]


CONV_ROWS = 64
SUBLANES = 8


def _dwconv_tile(sh_ref, tm, w_ref, flip, out_ref):
    for cl in range(CC // LANES):
        lanes = slice(cl * LANES, (cl + 1) * LANES)

        def chunk(c, carry):
            r = pl.multiple_of(c * CONV_ROWS, CONV_ROWS)
            accs = [None] * 4
            for n, (k, b, off) in enumerate(_conv_taps()):
                kk = TAPS - 1 - k if flip else k
                term = sh_ref[cl, b, pl.ds(off + r, CONV_ROWS), :] * w_ref[kk:kk + 1, lanes]
                accs[n % 4] = term if accs[n % 4] is None else accs[n % 4] + term
            out_ref[pl.ds(r, CONV_ROWS), lanes] = (accs[0] + accs[1]) + (accs[2] + accs[3])
            return carry

        lax.fori_loop(0, tm // CONV_ROWS, chunk, 0)


def _fill_padded(i, src_hbm, pad_ref, sem, S):
    @pl.when(i == 0)
    def _():
        zeros = jnp.zeros((CONV_HALO, pad_ref.shape[1]), pad_ref.dtype)
        pad_ref[0:CONV_HALO, :] = zeros
        pad_ref[CONV_HALO + S:2 * CONV_HALO + S, :] = zeros
        cp = pltpu.make_async_copy(src_hbm, pad_ref.at[pl.ds(CONV_HALO, S), :], sem)
        cp.start()
        cp.wait()


def _ln_silu(y, g, b):
    mu = jnp.mean(y, axis=-1, keepdims=True)
    yc = y - mu
    var = jnp.mean(yc * yc, axis=-1, keepdims=True)
    rstd = lax.rsqrt(var + EPS)
    yh = yc * rstd
    z = yh * g + b
    return yh, rstd, z


def _conv_fwd(vg, w_dw, b_dw, ln_g, ln_b):
    S = vg.shape[0]
    tm = min(CONV_TILE, S)

    def body(vg_hbm, w_ref, b_ref, g_ref, bb_ref, y_ref, yc_ref, pad_ref, sh_ref, sem):
        i = pl.program_id(0)
        _fill_padded(i, vg_hbm, pad_ref, sem, S)
        _shifted_copies(pad_ref, sh_ref.at[0], pl.multiple_of(i * tm, tm), tm)
        _dwconv_tile(sh_ref.at[0], tm, w_ref, False, y_ref)
        y = y_ref[...] + b_ref[...]
        y_ref[...] = y
        _, _, z = _ln_silu(y, g_ref[...], bb_ref[...])
        yc_ref[...] = (z * _sigmoid(z)).astype(yc_ref.dtype)

    return _hbm_call(
        body, name="conv_fwd", grid=(S // tm,),
        in_specs=[pl.BlockSpec(memory_space=pl.ANY), _full((TAPS, CC)), _full((1, CC)), _full((1, CC)), _full((1, CC))],
        out_specs=[_rows(tm, CC), _rows(tm, CC)],
        out_shape=[_sds((S, CC), F32), _sds((S, CC), _MXU)],
        scratch_shapes=[pltpu.VMEM((S + 2 * CONV_HALO, CC), F32), _shift_scratch(tm, 1), pltpu.SemaphoreType.DMA],
        compiler_params=_params(40),
    )(vg, w_dw, b_dw, ln_g, ln_b)


def _attn_window(i, S):
    start = jnp.clip((i - 1) * BLK, 0, S - 3 * BLK)
    return pl.multiple_of(start, BLK)


ATTN_COLS = GRP * BLK


def _attn_mask_base():
    keys = lax.broadcasted_iota(jnp.int32, (3 * BLK, ATTN_COLS), 0)
    cols = lax.broadcasted_iota(jnp.int32, (3 * BLK, ATTN_COLS), 1)
    return keys - (cols & (BLK - 1))


def _attn_mask(base, i, start):
    return jnp.abs(base + (start - i * BLK)) <= BLK


def _stack_heads(t, hk):
    return jnp.concatenate([t[:, (hk * GRP + g) * HD:(hk * GRP + g + 1) * HD] for g in range(GRP)], axis=0)


def _unstack_heads_t(t):
    cols = []
    for p in range(GRP // 2):
        pair = jnp.concatenate([t[:, (2 * p) * BLK:(2 * p + 1) * BLK], t[:, (2 * p + 1) * BLK:(2 * p + 2) * BLK]],
                               axis=0)
        cols.append(pair.T)
    return jnp.concatenate(cols, axis=1)


def _attn_probs(qs, kh, sk_ref, hk, mask):
    s = jnp.where(mask, _nt(kh, qs), -1e30)
    sk = jnp.concatenate(
        [jnp.broadcast_to(sk_ref[0:1, hk * GRP + g:hk * GRP + g + 1], (1, BLK)) for g in range(GRP)], axis=1)
    m = jnp.maximum(jnp.max(s, axis=0, keepdims=True), sk)
    e = jnp.exp(s - m)
    es = jnp.exp(sk - m)
    inv = 1.0 / (jnp.sum(e, axis=0, keepdims=True) + es)
    return e, inv, es


ATTN_BLOCKS = 8


def _attn_fwd(qr, kr, vv, sink):
    S = qr.shape[0]
    tq = ATTN_BLOCKS * BLK

    def body(q_ref, k_ref, v_ref, sk_ref, o_ref):
        base = _attn_mask_base()
        for b in range(ATTN_BLOCKS):
            i = pl.program_id(0) * ATTN_BLOCKS + b
            rows = slice(b * BLK, (b + 1) * BLK)
            start = _attn_window(i, S)
            kb = k_ref[pl.ds(start, 3 * BLK), :]
            vb = v_ref[pl.ds(start, 3 * BLK), :]
            q = q_ref[rows, :]
            mask = _attn_mask(base, i, start)
            outs = []
            for hk in range(NKV):
                hs = slice(hk * HD, (hk + 1) * HD)
                e, inv, _ = _attn_probs(_stack_heads(q, hk), kb[:, hs], sk_ref, hk, mask)
                outs.append(_unstack_heads_t(_tn(vb[:, hs], e) * inv))
            o_ref[rows, :] = jnp.concatenate(outs, axis=1).astype(o_ref.dtype)

    return _hbm_call(
        body, name="attn_fwd", grid=(S // tq,),
        in_specs=[_rows(tq, ATT), _full((S, KVC)), _full((S, KVC)), _full((1, NQ))],
        out_specs=_rows(tq, ATT),
        out_shape=_sds((S, ATT), _MXU),
        compiler_params=_params(40),
    )(qr, kr, vv, sink)


def _out_proj_fwd(x, yc, ya, wout, b_out, tm=512):
    S = x.shape[0]

    def body(x_ref, yc_ref, ya_ref, w_ref, b_ref, x1_ref, ymix_ref):
        ymix = jnp.concatenate([yc_ref[...], ya_ref[...]], axis=1)
        ymix_ref[...] = ymix
        x1_ref[...] = x_ref[...] + _nn(ymix, w_ref[...]) + b_ref[...]

    return _hbm_call(
        body, name="out_proj_fwd", grid=(S // tm,),
        in_specs=[_rows(tm, D), _rows(tm, CC), _rows(tm, ATT), _full((D, D)), _full((1, D))],
        out_specs=[_rows(tm, D), _rows(tm, D)],
        out_shape=[_sds((S, D), F32), _sds((S, D), _MXU)],
        compiler_params=_params(40),
    )(x, yc, ya, wout, b_out)


def _softmax_t(s):
    m = jnp.max(s, axis=0, keepdims=True)
    e = jnp.exp(s - m)
    return e, 1.0 / jnp.sum(e, axis=0, keepdims=True)


def _softmax(s):
    m = jnp.max(s, axis=-1, keepdims=True)
    e = jnp.exp(s - m)
    return e * (1.0 / jnp.sum(e, axis=-1, keepdims=True))


def _mem_attn_fwd(x1, g_q, wq, mem, g_kv, wkvT, wo, after, tm=512):
    S = x1.shape[0]
    M = mem.shape[0]

    def body(x1_ref, g_ref, wq_ref, mem_ref, gkv_ref, wkv_ref, wo_ref, after_ref,
             x2_ref, qm_ref, om_ref, km_ref, vm_ref, mn_ref):
        @pl.when(pl.program_id(0) == 0)
        def _():
            memn, _ = _rms(mem_ref[...], gkv_ref[...])
            kv = _nt(memn, wkv_ref[...])
            km_ref[...] = kv[:, :D].astype(km_ref.dtype)
            vm_ref[...] = kv[:, D:].astype(vm_ref.dtype)
            mn_ref[...] = memn.astype(mn_ref.dtype)

        x1v = x1_ref[...]
        hq, _ = _rms(x1v, g_ref[...])
        qm = (_nn(hq, wq_ref[...]) * MSCALE).astype(qm_ref.dtype)
        qm_ref[...] = qm
        oms = []
        for h in range(MH):
            sl = slice(h * MHD, (h + 1) * MHD)
            p = _softmax(_nt(qm[:, sl], km_ref[:, sl]))
            oms.append(_nn(p, vm_ref[:, sl]))
        om = jnp.concatenate(oms, axis=1).astype(om_ref.dtype)
        om_ref[...] = om
        x2_ref[...] = x1v + _nn(om, wo_ref[...])

    return _hbm_call(
        body, name="mem_attn_fwd", grid=(S // tm,),
        in_specs=[_rows(tm, D), _full((1, D)), _full((D, D)), _full((M, D)), _full((1, D)), _once((2 * D, D)),
                  _full((D, D)), _AFTER],
        out_specs=[_rows(tm, D)] * 3 + [_full((M, D))] * 3,
        out_shape=[_sds((S, D), F32), _sds((S, D), _MXU), _sds((S, D), _MXU)] + [_sds((M, D), _MXU)] * 3,
        compiler_params=_params(48),
    )(x1, g_q, wq, mem, g_kv, wkvT, wo, after)


def _ffn_loss_fwd(x2, tgt, g_ffn, g_final, wgT, wuT, wd, tm=512):
    S = x2.shape[0]

    def body(x2_ref, t_ref, gf_ref, gl_ref, wg_ref, wu_ref, wd_ref,
             gt_ref, up_ref, hf_ref, dx3_ref, dx3b_ref, loss_ref, dgl_ref):
        i = pl.program_id(0)
        _acc_init(i, loss_ref, dgl_ref)
        x2v = x2_ref[...]
        hf, _ = _rms(x2v, gf_ref[...])
        hb = hf.astype(hf_ref.dtype)
        hf_ref[...] = hb
        x3 = x2v
        for c in range(F // FCH):
            sl = slice(c * FCH, (c + 1) * FCH)
            gt = _nt(hb, wg_ref[sl, :])
            up = _nt(hb, wu_ref[sl, :])
            gt_ref[:, sl] = gt.astype(gt_ref.dtype)
            up_ref[:, sl] = up.astype(up_ref.dtype)
            x3 = x3 + _nn(gt * _sigmoid(gt) * up, wd_ref[sl, :])
        gl = gl_ref[...]
        y, r = _rms(x3, gl)
        diff = y - t_ref[...]
        _acc_row(loss_ref, jnp.sum(jnp.sum(diff * diff, axis=-1, keepdims=True), axis=0, keepdims=True) * (0.5 / D))
        dx3, dgl = _rms_bwd(diff * (1.0 / D), x3, r, gl)
        dx3_ref[...] = dx3
        dx3b_ref[...] = dx3.astype(dx3b_ref.dtype)
        _acc_row(dgl_ref, dgl)

    return _hbm_call(
        body, name="ffn_loss_fwd", grid=(S // tm,),
        in_specs=[_rows(tm, D), _rows(tm, D), _full((1, D)), _full((1, D)),
                  _once((F, D)), _once((F, D)), _once((F, D))],
        out_specs=[_rows(tm, F), _rows(tm, F), _rows(tm, D), _rows(tm, D), _rows(tm, D),
                   _full((8, LANES)), _full((8, D))],
        out_shape=[_sds((S, F), _MXU), _sds((S, F), _MXU), _sds((S, D), _MXU), _sds((S, D), F32),
                   _sds((S, D), _MXU), _sds((8, LANES), F32), _sds((8, D), F32)],
        compiler_params=_params(56),
    )(x2, tgt, g_ffn, g_final, wgT, wuT, wd)


def _ffn_bwd(dx3, x2, gt, up, g_ffn, wd, wgT, wuT, tm=256):
    S = x2.shape[0]

    def body(dx3_ref, x2_ref, gt_ref, up_ref, g_ref, wd_ref, wg_ref, wu_ref,
             dx2_ref, dx2b_ref, dgt_ref, dup_ref, act_ref, dg_ref):
        i = pl.program_id(0)
        _acc_init(i, dg_ref)
        dx3v = dx3_ref[...]
        db = dx3v.astype(_MXU)
        dhf = jnp.zeros((tm, D), F32)
        for c in range(F // FCH_BWD):
            sl = slice(c * FCH_BWD, (c + 1) * FCH_BWD)
            gtv = gt_ref[:, sl].astype(F32)
            upv = up_ref[:, sl].astype(F32)
            dact = _nt(db, wd_ref[sl, :])
            sig = _sigmoid(gtv)
            silu = gtv * sig
            dup = (dact * silu).astype(dup_ref.dtype)
            dgt = (dact * upv * (sig * (1.0 + gtv * (1.0 - sig)))).astype(dgt_ref.dtype)
            act_ref[:, sl] = (silu * upv).astype(act_ref.dtype)
            dgt_ref[:, sl] = dgt
            dup_ref[:, sl] = dup
            dhf = dhf + _nn(dgt, wg_ref[sl, :]) + _nn(dup, wu_ref[sl, :])
        x2v = x2_ref[...]
        g = g_ref[...]
        _, r = _rms(x2v, g)
        dxn, dg = _rms_bwd(dhf, x2v, r, g)
        dx2 = dx3v + dxn
        dx2_ref[...] = dx2
        dx2b_ref[...] = dx2.astype(dx2b_ref.dtype)
        _acc_row(dg_ref, dg)

    return _hbm_call(
        body, name="ffn_bwd", grid=(S // tm,),
        in_specs=[_rows(tm, D), _rows(tm, D), _rows(tm, F), _rows(tm, F), _full((1, D)),
                  _once((F, D)), _once((F, D)), _once((F, D))],
        out_specs=[_rows(tm, D), _rows(tm, D), _rows(tm, F), _rows(tm, F), _rows(tm, F), _full((8, D))],
        out_shape=[_sds((S, D), F32), _sds((S, D), _MXU), _sds((S, F), _MXU), _sds((S, F), _MXU),
                   _sds((S, F), _MXU), _sds((8, D), F32)],
        compiler_params=_params(56),
    )(dx3, x2, gt, up, g_ffn, wd, wgT, wuT)


_AFTER = pl.BlockSpec(memory_space=pl.ANY)


def _wgrad(pairs, name, after, rc=256):
    S, R = pairs[0][0].shape
    N = pairs[0][1].shape[1]
    rights = []
    for _, b in pairs:
        if not any(b is r for r in rights):
            rights.append(b)
    which = [next(i for i, r in enumerate(rights) if r is b) for _, b in pairs]
    n, nb = len(pairs), len(rights)

    def body(*refs):
        for k in range(n):
            refs[n + nb + 1 + k][...] = _tn(refs[k][...], refs[n + which[k]][...]).astype(jnp.bfloat16)

    res = _hbm_call(
        body, name=name, grid=(R // rc,),
        in_specs=[pl.BlockSpec((S, rc), lambda j: (0, j))] * n + [_once((S, N))] * nb + [_AFTER],
        out_specs=[_rows(rc, N)] * n,
        out_shape=[_sds((R, N), jnp.bfloat16)] * n,
        compiler_params=_params(56),
    )(*[a for a, _ in pairs], *rights, after)
    return list(res)


def _wgrad_acc(pairs, name, after, ts=1024):
    S, R = pairs[0][0].shape
    N = pairs[0][1].shape[1]
    n = len(pairs)

    def body(*refs):
        outs, accs = refs[2 * n + 1:3 * n + 1], refs[3 * n + 1:]
        i = pl.program_id(0)
        _acc_init(i, *accs)
        for k in range(n):
            accs[k][...] += _tn(refs[k][...], refs[n + k][...])

        @pl.when(i == pl.num_programs(0) - 1)
        def _():
            for k in range(n):
                outs[k][...] = accs[k][...].astype(outs[k].dtype)

    res = _hbm_call(
        body, name=name, grid=(S // ts,),
        in_specs=[_rows(ts, R)] * n + [_rows(ts, N)] * n + [_AFTER],
        out_specs=[_full((R, N))] * n,
        out_shape=[_sds((R, N), jnp.bfloat16)] * n,
        scratch_shapes=[pltpu.VMEM((R, N), F32)] * n,
        compiler_params=_params(56),
    )(*[a for a, _ in pairs], *[b for _, b in pairs], after)
    return list(res)


def _mem_attn_bwd(dx2, x1, qm, km, vm, g_q, wq, wo, wout, y, ln_g, ln_b, after, tm=512):
    S = x1.shape[0]
    M = km.shape[0]

    def body(dx2_ref, x1_ref, qm_ref, km_ref, vm_ref, g_ref, wq_ref, wo_ref, wout_ref, y_ref, lg_ref, lb_ref,
             after_ref, dx1_ref, dx1b_ref, hq_ref, dqm_ref, dy_ref, dya_ref, dkm_ref, dvm_ref, dg_ref, dbo_ref,
             dlg_ref, dlb_ref, dbdw_ref):
        i = pl.program_id(0)
        _acc_init(i, dkm_ref, dvm_ref, dg_ref, dbo_ref, dlg_ref, dlb_ref, dbdw_ref)
        dx2v = dx2_ref[...]
        domt = _nt(wo_ref[...], dx2v).astype(_MXU)
        dqs = []
        for h in range(MH):
            sl = slice(h * MHD, (h + 1) * MHD)
            qh = qm_ref[:, sl]
            kh = km_ref[:, sl]
            e, inv = _softmax_t(_nt(kh, qh))
            p = e * inv
            dp = _nn(vm_ref[:, sl], domt[sl, :])
            ds = (p * (dp - jnp.sum(dp * p, axis=0, keepdims=True))).astype(_MXU)
            dqs.append(_tn(ds, kh) * MSCALE)
            dkm_ref[:, sl] += _nn(ds, qh)
            dvm_ref[:, sl] += _nt(p, domt[sl, :])
        dqm = jnp.concatenate(dqs, axis=1).astype(dqm_ref.dtype)
        dqm_ref[...] = dqm
        dhq = _nt(dqm, wq_ref[...])
        x1v = x1_ref[...]
        g = g_ref[...]
        hq, r = _rms(x1v, g)
        hq_ref[...] = hq.astype(hq_ref.dtype)
        dxn, dg = _rms_bwd(dhq, x1v, r, g)
        dx1 = dx2v + dxn
        dx1_ref[...] = dx1
        dx1b = dx1.astype(dx1b_ref.dtype)
        dx1b_ref[...] = dx1b
        dym = _nt(dx1b, wout_ref[...])
        dya_ref[...] = dym[:, CC:].astype(dya_ref.dtype)
        _acc_row(dg_ref, dg)
        _acc_row(dbo_ref, jnp.sum(dx1, axis=0, keepdims=True))
        lg = lg_ref[...]
        yh, rstd, z = _ln_silu(y_ref[...], lg, lb_ref[...])
        sig = _sigmoid(z)
        dz = dym[:, :CC] * (sig * (1.0 + z * (1.0 - sig)))
        dyh = dz * lg
        dy = rstd * (dyh - jnp.mean(dyh, axis=-1, keepdims=True) - yh * jnp.mean(dyh * yh, axis=-1, keepdims=True))
        dy_ref[...] = dy
        _acc_row(dlg_ref, jnp.sum(dz * yh, axis=0, keepdims=True))
        _acc_row(dlb_ref, jnp.sum(dz, axis=0, keepdims=True))
        _acc_row(dbdw_ref, jnp.sum(dy, axis=0, keepdims=True))

    return _hbm_call(
        body, name="mem_attn_bwd", grid=(S // tm,),
        in_specs=[_rows(tm, D), _rows(tm, D), _rows(tm, D), _full((M, D)), _full((M, D)), _full((1, D)),
                  _once((D, D)), _once((D, D)), _once((D, D)), _rows(tm, CC), _full((1, CC)), _full((1, CC)),
                  _AFTER],
        out_specs=[_rows(tm, D), _rows(tm, D), _rows(tm, D), _rows(tm, D), _rows(tm, CC), _rows(tm, ATT),
                   _full((M, D)), _full((M, D)), _full((8, D)), _full((8, D)),
                   _full((8, CC)), _full((8, CC)), _full((8, CC))],
        out_shape=[_sds((S, D), F32), _sds((S, D), _MXU), _sds((S, D), _MXU), _sds((S, D), _MXU),
                   _sds((S, CC), F32), _sds((S, ATT), _MXU),
                   _sds((M, D), F32), _sds((M, D), F32), _sds((8, D), F32), _sds((8, D), F32),
                   _sds((8, CC), F32), _sds((8, CC), F32), _sds((8, CC), F32)],
        compiler_params=_params(56),
    )(dx2, x1, qm, km, vm, g_q, wq, wo, wout, y, ln_g, ln_b, after)


def _mem_kv_bwd(dkm, dvm, mem, g_kv, memn, wkvT):
    M = mem.shape[0]

    def body(dkm_ref, dvm_ref, mem_ref, g_ref, mn_ref, w_ref, gw_ref, dg_ref):
        dkv = jnp.concatenate([dkm_ref[...], dvm_ref[...]], axis=1).astype(_MXU)
        gw_ref[...] = _tn(dkv, mn_ref[...]).astype(gw_ref.dtype)
        dmn = _nn(dkv, w_ref[...])
        memv = mem_ref[...]
        g = g_ref[...]
        _, r = _rms(memv, g)
        _, dg = _rms_bwd(dmn, memv, r, g)
        dg_ref[...] = jnp.broadcast_to(dg, dg_ref.shape)

    return _hbm_call(
        body, name="mem_kv_bwd", grid=(1,),
        in_specs=[_full((M, D)), _full((M, D)), _full((M, D)), _full((1, D)), _full((M, D)), _full((2 * D, D))],
        out_specs=[_full((2 * D, D)), _full((8, D))],
        out_shape=[_sds((2 * D, D), jnp.bfloat16), _sds((8, D), F32)],
        compiler_params=_params(48),
    )(dkm, dvm, mem, g_kv, memn, wkvT)


def _attn_bwd(qr, kr, vv, sink, dya, after):
    S = qr.shape[0]
    tq = ATTN_BLOCKS * BLK

    def body(q_ref, k_ref, v_ref, sk_ref, do_ref, after_ref, dq_ref, dk_ref, dv_ref, dsk_ref, sink_acc):
        _acc_init(pl.program_id(0), dk_ref, dv_ref, sink_acc)
        base = _attn_mask_base()
        for b in range(ATTN_BLOCKS):
            i = pl.program_id(0) * ATTN_BLOCKS + b
            rows = slice(b * BLK, (b + 1) * BLK)
            start = _attn_window(i, S)
            kb = k_ref[pl.ds(start, 3 * BLK), :]
            vb = v_ref[pl.ds(start, 3 * BLK), :]
            q = q_ref[rows, :]
            do = do_ref[rows, :]
            mask = _attn_mask(base, i, start)
            dqs, dks, dvs = [], [], []
            for hk in range(NKV):
                hs = slice(hk * HD, (hk + 1) * HD)
                qs = _stack_heads(q, hk)
                dos = _stack_heads(do, hk)
                e, inv, es = _attn_probs(qs, kb[:, hs], sk_ref, hk, mask)
                p = e * inv
                dp = _nt(vb[:, hs], dos)
                delta = jnp.sum(dp * p, axis=0, keepdims=True)
                ds = (p * (dp - delta)).astype(_MXU)
                dqs.append(_unstack_heads_t(_tn(kb[:, hs], ds) * SCALE))
                dks.append(_nn(ds, qs))
                dvs.append(_nn(p, dos))
                sink_acc[hk:hk + 1, :] -= es * inv * delta
            dq_ref[rows, :] = jnp.concatenate(dqs, axis=1)
            dk_ref[pl.ds(start, 3 * BLK), :] += jnp.concatenate(dks, axis=1)
            dv_ref[pl.ds(start, 3 * BLK), :] += jnp.concatenate(dvs, axis=1)

        @pl.when(pl.program_id(0) == pl.num_programs(0) - 1)
        def _():
            for h in range(NQ):
                tot = jnp.sum(sink_acc[h // GRP:h // GRP + 1, (h % GRP) * BLK:(h % GRP + 1) * BLK], axis=1,
                              keepdims=True)
                dsk_ref[h:h + 1, :] = jnp.broadcast_to(tot, (1, LANES))

    return _hbm_call(
        body, name="attn_bwd", grid=(S // tq,),
        in_specs=[_rows(tq, ATT), _full((S, KVC)), _full((S, KVC)), _full((1, NQ)), _rows(tq, ATT), _AFTER],
        out_specs=[_rows(tq, ATT), _full((S, KVC)), _full((S, KVC)), _full((8, LANES))],
        out_shape=[_sds((S, ATT), F32), _sds((S, KVC), F32), _sds((S, KVC), F32), _sds((8, LANES), F32)],
        scratch_shapes=[pltpu.VMEM((8, ATTN_COLS), F32)],
        compiler_params=_params(40),
    )(qr, kr, vv, sink, dya, after)


def _conv_bwd(dy, vg, uglu, w_dw, after):
    S = dy.shape[0]
    tm = min(CONV_TILE, S)

    def body(dy_hbm, vg_hbm, uglu_ref, w_ref, after_ref, du_ref, gw_ref, dbin_ref, dyp_ref, vgp_ref, gacc_ref, sh_ref,
             dvg_ref, sems):
        i = pl.program_id(0)
        _acc_init(i, gacc_ref, dbin_ref)
        _fill_padded(i, dy_hbm, dyp_ref, sems.at[0], S)
        _fill_padded(i, vg_hbm, vgp_ref, sems.at[1], S)
        t0 = pl.multiple_of(i * tm, tm)
        _shifted_copies(dyp_ref, sh_ref.at[0], t0, tm)
        _shifted_copies(vgp_ref, sh_ref.at[1], t0, tm)
        _dwconv_tile(sh_ref.at[0], tm, w_ref, True, dvg_ref)
        dvg = dvg_ref[...]
        sub = CONV_ROWS
        for cl in range(CC // LANES):
            lanes = slice(cl * LANES, (cl + 1) * LANES)
            for k, b, off in _conv_taps():
                part = jnp.zeros((8, LANES), F32)
                for r in range(0, tm, sub):
                    prod = (sh_ref[0, cl, 0, CONV_HALO + r:CONV_HALO + r + sub, :]
                            * sh_ref[1, cl, b, off + r:off + r + sub, :])
                    part = part + jnp.sum(prod.reshape(sub // 8, 8, LANES), axis=0)
                gacc_ref[k, :, lanes] += part
        a_v = uglu_ref[:, :CC].astype(F32)
        sg = _sigmoid(uglu_ref[:, CC:].astype(F32))
        da = dvg * sg
        dgate = dvg * a_v * sg * (1.0 - sg)
        du = jnp.concatenate([da, dgate], axis=1)
        du_ref[...] = du.astype(du_ref.dtype)
        _acc_row(dbin_ref, jnp.sum(du, axis=0, keepdims=True))

        @pl.when(i == pl.num_programs(0) - 1)
        def _():
            gw_ref[...] = jnp.zeros(gw_ref.shape, F32)
            for k in range(TAPS):
                gw_ref[k:k + 1, :] = jnp.sum(gacc_ref[k], axis=0, keepdims=True)

    return _hbm_call(
        body, name="conv_bwd", grid=(S // tm,),
        in_specs=[pl.BlockSpec(memory_space=pl.ANY), pl.BlockSpec(memory_space=pl.ANY), _rows(tm, 2 * CC),
                  _full((TAPS, CC)), _AFTER],
        out_specs=[_rows(tm, 2 * CC), _full((TAPS + 1, CC)), _full((8, 2 * CC))],
        out_shape=[_sds((S, 2 * CC), _MXU), _sds((TAPS + 1, CC), F32), _sds((8, 2 * CC), F32)],
        scratch_shapes=[pltpu.VMEM((S + 2 * CONV_HALO, CC), F32), pltpu.VMEM((S + 2 * CONV_HALO, CC), F32),
                        pltpu.VMEM((TAPS, 8, CC), F32), _shift_scratch(tm, 2), pltpu.VMEM((tm, CC), F32),
                        pltpu.SemaphoreType.DMA((2,))],
        compiler_params=_params(56),
    )(dy, vg, uglu, w_dw, after)


def _in_proj_bwd(duglu, dqr, dkr, dv, cosq, sinq, x, dx1, g_mix, winT, tm=512):
    S = x.shape[0]
    NA = ATT + 2 * KVC

    def body(dug_ref, dq_ref, dk_ref, dv_ref, c_ref, s_ref, x_ref, dx1_ref, g_ref, w_ref,
             dx_ref, du_ref, dg_ref, dbin_ref):
        i = pl.program_id(0)
        _acc_init(i, dg_ref, dbin_ref)
        c = c_ref[...]
        s = s_ref[...]
        datt = jnp.concatenate([_rope_bwd(dq_ref[...], c, s), _rope_bwd(dk_ref[...], c, s), dv_ref[...]], axis=1)
        _acc_row(dbin_ref, jnp.sum(datt, axis=0, keepdims=True))
        du = jnp.concatenate([dug_ref[...], datt.astype(du_ref.dtype)], axis=1)
        du_ref[...] = du
        dh = _nn(du, w_ref[...])
        xv = x_ref[...]
        g = g_ref[...]
        _, r = _rms(xv, g)
        dxn, dg = _rms_bwd(dh, xv, r, g)
        dx_ref[...] = dx1_ref[...] + dxn
        _acc_row(dg_ref, dg)

    return _hbm_call(
        body, name="in_proj_bwd", grid=(S // tm,),
        in_specs=[_rows(tm, 2 * CC), _rows(tm, ATT), _rows(tm, KVC), _rows(tm, KVC), _rows(tm, LANES),
                  _rows(tm, LANES), _rows(tm, D), _rows(tm, D), _full((1, D)), _once((INC, D))],
        out_specs=[_rows(tm, D), _rows(tm, INC), _full((8, D)), _full((8, NA))],
        out_shape=[_sds((S, D), F32), _sds((S, INC), _MXU), _sds((8, D), F32), _sds((8, NA), F32)],
        compiler_params=_params(48),
    )(duglu, dqr, dkr, dv, cosq, sinq, x, dx1, g_mix, winT)


def _rope_tables(S):
    pos = jnp.arange(S, dtype=F32)
    inv_freq = ROPE_THETA ** (-jnp.arange(0, HD, 2, dtype=F32) / HD)
    ang = pos[:, None] * inv_freq[None, :]
    cos, sin = jnp.cos(ang), jnp.sin(ang)
    cosq = jnp.tile(jnp.concatenate([cos, cos], axis=1), (1, LANES // HD))
    sinq = jnp.tile(jnp.concatenate([-sin, sin], axis=1), (1, LANES // HD))
    return cosq, sinq


GATHER_GROUPS = (("w_inT", "w_dw"), ("w_out", "w_mem_kvT", "w_mem_q", "w_mem_o"), ("w_gateT", "w_upT"), ("w_down",))
ID_GATHER, ID_SIBLING, ID_CHIP = 1, 5, 8


def _local_step(x, mem, tgt, sp, gather, reduce):
    S = x.shape[0]
    assert S % (ATTN_BLOCKS * BLK) == 0 and S % 512 == 0, S
    cosq, sinq = _rope_tables(S)

    gather.forward(0, cosq)
    w0 = gather.finish(0, sinq)
    winT, w_dw_full = w0["w_inT"], w0["w_dw"]
    h, uglu, vg, qr, kr, vv = _in_proj_fwd(x, sp["g_mix"], winT, sp["b_in"], cosq, sinq)
    ya = _attn_fwd(qr, kr, vv, sp["attn_sink"])
    gather.forward(1, ya)
    y, yc = _conv_fwd(vg, w_dw_full, sp["b_dw"], sp["g_conv_ln"], sp["b_conv_ln"])
    w1 = gather.finish(1, yc)
    wout, wkvT, wq, wo = w1["w_out"], w1["w_mem_kvT"], w1["w_mem_q"], w1["w_mem_o"]
    x1, ymix = _out_proj_fwd(x, yc, ya, wout, sp["b_out"])
    token = gather.forward(2, x1)
    x2, qm, om, km, vm, memn = _mem_attn_fwd(x1, sp["g_mem_q"], wq, mem, sp["g_mem_kv"], wkvT, wo, token)
    w2 = gather.finish(2, om)
    gather.forward(3, om)
    wgT, wuT, wd = w2["w_gateT"], w2["w_upT"], gather.finish(3, om)["w_down"]
    gt, up, hf, dx3, dx3b, loss_acc, dg_final = _ffn_loss_fwd(x2, tgt, sp["g_ffn"], sp["g_final"], wgT, wuT, wd)

    dx2, dx2b, dgt, dup, act, dg_ffn = _ffn_bwd(dx3, x2, gt, up, sp["g_ffn"], wd, wgT, wuT)
    token = reduce.start(0, _wgrad([(dgt, hf), (dup, hf), (act, dx3b)], "wgrad_ffn", dx2))

    dx1, dx1b, hq, dqm, dy, dya, dkm, dvm, dg_mem_q, db_out, dg_cln, db_cln, db_dw = _mem_attn_bwd(
        dx2, x1, qm, km, vm, sp["g_mem_q"], wq, wo, wout, y, sp["g_conv_ln"], sp["b_conv_ln"], token)
    token = reduce.middle(0, dx1)
    g_wout, g_wq, g_wo = _wgrad_acc([(ymix, dx1b), (hq, dqm), (om, dx2b)], "wgrad_mix_mem", token)
    g_wkvT, dg_mem_kv = _mem_kv_bwd(dkm, dvm, mem, sp["g_mem_kv"], memn, wkvT)
    token = reduce.start(1, [g_wout, g_wq, g_wkvT, g_wo])

    dqr, dkr, dvv, dsink = _attn_bwd(qr, kr, vv, sp["attn_sink"], dya, token)
    token = reduce.middle(1, dqr)
    duglu, g_wdw, dbin_glu = _conv_bwd(dy, vg, uglu, w_dw_full, token)
    grad_x, du, dg_mix, dbin_att = _in_proj_bwd(duglu, dqr, dkr, dvv, cosq, sinq, x, dx1, sp["g_mix"], winT)
    token = reduce.start(2, _wgrad_acc([(du, h)], "wgrad_in", grad_x))

    small = {
        "loss": loss_acc[0:1, :],
        "g_mix": dg_mix[0:1], "b_in": jnp.concatenate([dbin_glu[0:1], dbin_att[0:1]], axis=1),
        "b_dw": db_dw[0:1], "g_conv_ln": dg_cln[0:1], "b_conv_ln": db_cln[0:1],
        "attn_sink": dsink[:, 0].reshape(1, NQ), "b_out": db_out[0:1], "g_mem_q": dg_mem_q[0:1],
        "g_mem_kv": dg_mem_kv[0:1], "g_ffn": dg_ffn[0:1], "g_final": dg_final[0:1],
    }
    token = reduce.middle(2, token, reduce.small_zones(small, g_wdw))
    reduce.finish(0, token)
    reduce.finish(1, token)
    return grad_x


def _place():
    return lax.axis_index("x"), lax.axis_index("y"), lax.axis_index("c")


_HBM = pl.BlockSpec(memory_space=pltpu.HBM)
_SEM = pl.BlockSpec(memory_space=pltpu.SEMAPHORE)
_ANY = pl.BlockSpec(memory_space=pl.ANY)
_EFFECT = pltpu.SideEffectType.DATAFLOW_SIDE_EFFECTING
_VMEM = pl.BlockSpec(memory_space=pltpu.VMEM)
_TOKEN = jax.ShapeDtypeStruct((8, LANES), F32)


def _dev_index(p):
    return 4 * p[0] + 2 * p[1] + p[2]


def _landing_zones(shards, me_idx):
    def body(m_ref, *refs):
        n = len(refs) // 2
        for k in range(n):
            refs[n + k][...] = refs[k][...].astype(jnp.bfloat16)

    return pl.pallas_call(
        body, name="landing_zones",
        grid_spec=pltpu.PrefetchScalarGridSpec(
            num_scalar_prefetch=1, grid=(1,),
            in_specs=[pl.BlockSpec(s.shape, lambda i, m: (0, 0)) for s in shards],
            out_specs=[pl.BlockSpec((None,) + s.shape, lambda i, m: (m[0], 0, 0)) for s in shards]),
        out_shape=[pltpu.HBM((NDEV,) + s.shape, jnp.bfloat16) for s in shards],
        compiler_params=_params(40),
    )(me_idx, *shards)


def _gather_start(groups):
    sizes = [len(g) for g in groups]
    flat = [z for g in groups for z in g]
    n = len(flat)

    def body(*refs):
        ins = refs[:n]
        sems = refs[2 * n:]
        x, y, c = _place()
        me = _dev_index((x, y, c))
        peers = [(x, y, 1 - c), (1 - x, y, c), (x, 1 - y, c), (1 - x, 1 - y, c)]
        base = 0
        for gi, size in enumerate(sizes):
            send_sems, recv_sems = sems[2 * gi], sems[2 * gi + 1]
            for k in range(size):
                zone = ins[base + k]
                for j, peer in enumerate(peers):
                    pltpu.make_async_remote_copy(
                        src_ref=zone.at[me], dst_ref=zone.at[me], send_sem=send_sems.at[4 * k + j],
                        recv_sem=recv_sems.at[4 * k + j], device_id=peer, device_id_type=MESH).start()
            base += size

    out_shape = [pltpu.HBM(z.shape, z.dtype) for z in flat]
    out_specs = [_HBM] * n
    for size in sizes:
        out_shape += [pltpu.SemaphoreType.DMA((4 * size,))] * 2
        out_specs += [_SEM, _SEM]
    res = pl.pallas_call(
        body, name="gather_start", in_specs=[_HBM] * n, out_specs=out_specs, out_shape=out_shape,
        input_output_aliases={k: k for k in range(n)},
        compiler_params=pltpu.CompilerParams(has_side_effects=_EFFECT),
    )(*[pltpu.with_memory_space_constraint(z, pltpu.HBM) for z in flat])
    out, base = [], 0
    for gi, size in enumerate(sizes):
        out.append((list(res[base:base + size]), res[n + 2 * gi], res[n + 2 * gi + 1]))
        base += size
    return out


def _gather_forward(zones, send_sems, recv_sems, after, collective_id, name):
    n = len(zones)

    def body(*refs):
        ins = refs[:n]
        send_ref, recv_ref = refs[n], refs[n + 1]
        fsend, frecv = refs[2 * n + 3], refs[2 * n + 4]
        x, y, c = _place()
        sibling = (x, y, 1 - c)
        barrier = pltpu.get_barrier_semaphore()
        pl.semaphore_signal(barrier, inc=1, device_id=sibling, device_id_type=MESH)
        pl.semaphore_wait(barrier, 1)
        chips = [(1 - x, y), (x, 1 - y), (1 - x, 1 - y)]
        for j, chip in enumerate(chips):
            for k in range(n):
                slot = ins[k].at[_dev_index((*chip, c))]
                pltpu.make_async_remote_copy(
                    src_ref=slot, dst_ref=slot, send_sem=send_ref.at[4 * k + 1 + j],
                    recv_sem=recv_ref.at[4 * k + 1 + j], device_id=sibling, device_id_type=MESH).wait_recv()
                pltpu.make_async_remote_copy(
                    src_ref=slot, dst_ref=slot, send_sem=fsend.at[3 * k + j], recv_sem=frecv.at[3 * k + j],
                    device_id=sibling, device_id_type=MESH).start()
        for k in range(n):
            own = ins[k].at[_dev_index((x, y, c))]
            theirs = ins[k].at[_dev_index(sibling)]
            pltpu.make_async_remote_copy(
                src_ref=theirs, dst_ref=theirs, send_sem=send_ref.at[4 * k], recv_sem=recv_ref.at[4 * k],
                device_id=sibling, device_id_type=MESH).wait_recv()
            for j in range(4):
                pltpu.make_async_remote_copy(
                    src_ref=own, dst_ref=own, send_sem=send_ref.at[4 * k + j], recv_sem=recv_ref.at[4 * k + j],
                    device_id=sibling, device_id_type=MESH).wait_send()
        refs[2 * n + 5][...] = jnp.zeros(_TOKEN.shape, _TOKEN.dtype)

    res = pl.pallas_call(
        body, name=name, in_specs=[_HBM] * n + [_SEM, _SEM, _ANY],
        out_specs=[_HBM] * n + [_SEM, _SEM, _VMEM],
        out_shape=[pltpu.HBM(z.shape, z.dtype) for z in zones] + [pltpu.SemaphoreType.DMA((3 * n,))] * 2 + [_TOKEN],
        input_output_aliases={k: k for k in range(n)},
        compiler_params=pltpu.CompilerParams(has_side_effects=_EFFECT, collective_id=collective_id),
    )(*zones, send_sems, recv_sems, after)
    return (list(res[:n]), res[n], res[n + 1]), res[n + 2]


def _gather_finish(zones, fsend, frecv, after, name):
    n = len(zones)

    def body(*refs):
        ins = refs[:n]
        fsend_ref, frecv_ref = refs[n], refs[n + 1]
        x, y, c = _place()
        sibling = (x, y, 1 - c)
        chips = [(1 - x, y), (x, 1 - y), (1 - x, 1 - y)]
        for j, chip in enumerate(chips):
            for k in range(n):
                mine = ins[k].at[_dev_index((*chip, c))]
                theirs = ins[k].at[_dev_index((*chip, 1 - c))]
                pltpu.make_async_remote_copy(
                    src_ref=theirs, dst_ref=theirs, send_sem=fsend_ref.at[3 * k + j],
                    recv_sem=frecv_ref.at[3 * k + j], device_id=sibling, device_id_type=MESH).wait_recv()
                pltpu.make_async_remote_copy(
                    src_ref=mine, dst_ref=mine, send_sem=fsend_ref.at[3 * k + j],
                    recv_sem=frecv_ref.at[3 * k + j], device_id=sibling, device_id_type=MESH).wait_send()

    res = pl.pallas_call(
        body, name=name, in_specs=[_HBM] * n + [_SEM, _SEM, _ANY], out_specs=[_HBM] * n,
        out_shape=[pltpu.HBM(z.shape, z.dtype) for z in zones],
        input_output_aliases={k: k for k in range(n)},
        compiler_params=pltpu.CompilerParams(has_side_effects=_EFFECT),
    )(*zones, fsend, frecv, after)
    return list(res)


class _Gather:
    def __init__(self, started):
        self._phase = list(started)

    def forward(self, gi, after):
        zones, send_sems, recv_sems = self._phase[gi]
        self._phase[gi], token = _gather_forward(zones, send_sems, recv_sems, after, ID_GATHER + gi,
                                                  "gather_forward_%d" % gi)
        return token

    def finish(self, gi, after):
        zones, fsend, frecv = self._phase[gi]
        out = {}
        for name, z in zip(GATHER_GROUPS[gi], _gather_finish(zones, fsend, frecv, after, "gather_finish_%d" % gi)):
            if name == "w_dw":
                out[name] = jnp.transpose(z, (1, 0, 2)).reshape(TAPS + 1, CC)[:TAPS]
            else:
                out[name] = z.reshape(NDEV * z.shape[1], D)
        return out


def _handshake(peers):
    barrier = pltpu.get_barrier_semaphore()
    for peer in peers:
        pl.semaphore_signal(barrier, inc=1, device_id=peer, device_id_type=MESH)
    pl.semaphore_wait(barrier, len(peers))


def _sibling_copies(srcs, lands, send_sems, recv_sems):
    x, y, c = _place()
    return [pltpu.make_async_remote_copy(
        src_ref=srcs[k].at[2 * s + (1 - c)], dst_ref=lands[k].at[s], send_sem=send_sems.at[4 * k + s],
        recv_sem=recv_sems.at[4 * k + s], device_id=(x, y, 1 - c), device_id_type=MESH)
        for k in range(len(srcs)) for s in range(4)]


def _sibling_start(gbs, collective_id, name):
    n = len(gbs)
    lands = [lax.empty((4,) + g.shape[1:], g.dtype) for g in gbs]

    def body(*refs):
        x, y, c = _place()
        _handshake([(x, y, 1 - c)])
        for cp in _sibling_copies(refs[:n], refs[n:2 * n], refs[4 * n], refs[4 * n + 1]):
            cp.start()
        refs[4 * n + 2][...] = jnp.zeros(_TOKEN.shape, _TOKEN.dtype)

    res = pl.pallas_call(
        body, name=name, in_specs=[_HBM] * (2 * n), out_specs=[_HBM] * (2 * n) + [_SEM, _SEM, _VMEM],
        out_shape=[pltpu.HBM(a.shape, a.dtype) for a in gbs + lands] + [pltpu.SemaphoreType.DMA((4 * n,))] * 2
        + [_TOKEN],
        input_output_aliases={k: k for k in range(2 * n)},
        compiler_params=pltpu.CompilerParams(has_side_effects=_EFFECT, collective_id=collective_id),
    )(*[pltpu.with_memory_space_constraint(a, pltpu.HBM) for a in gbs + lands])
    return list(res[:n]), list(res[n:2 * n]), res[2 * n], res[2 * n + 1], res[2 * n + 2]


def _sibling_wait(gbs, lands, send_sems, recv_sems, after, name):
    n = len(gbs)

    def body(*refs):
        for cp in _sibling_copies(refs[:n], refs[n:2 * n], refs[2 * n], refs[2 * n + 1]):
            cp.wait_recv()
            cp.wait_send()

    res = pl.pallas_call(
        body, name=name, in_specs=[_HBM] * (2 * n) + [_SEM, _SEM, _ANY], out_specs=[_HBM] * (2 * n),
        out_shape=[pltpu.HBM(a.shape, a.dtype) for a in gbs + lands],
        input_output_aliases={k: k for k in range(2 * n)},
        compiler_params=pltpu.CompilerParams(has_side_effects=_EFFECT),
    )(*gbs, *lands, send_sems, recv_sems, after)
    return list(res[:n]), list(res[n:])


def _pair_sum(gfs, ts, c_idx, name):
    n = len(gfs)
    halves = 1

    def body(c_ref, *refs):
        for k in range(n):
            refs[2 * n + k][...] = (refs[k][...].astype(F32) + refs[n + k][...].astype(F32)).astype(jnp.bfloat16)

    in_specs, out_specs, out_shape = [], [], []
    for g in gfs:
        rh = g.shape[1] // halves
        in_specs.append(pl.BlockSpec((None, rh, D), lambda s, h, c_ref: (2 * s + c_ref[0], h, 0)))
    for g in gfs:
        rh = g.shape[1] // halves
        in_specs.append(pl.BlockSpec((None, rh, D), lambda s, h, c_ref: (s, h, 0)))
        out_specs.append(pl.BlockSpec((None, rh, D), lambda s, h, c_ref: (s, h, 0)))
        out_shape.append(_sds((4, g.shape[1], D), jnp.bfloat16))
    return pl.pallas_call(
        body, name=name,
        grid_spec=pltpu.PrefetchScalarGridSpec(num_scalar_prefetch=1, grid=(4, halves), in_specs=in_specs,
                                               out_specs=out_specs),
        out_shape=out_shape,
        compiler_params=pltpu.CompilerParams(dimension_semantics=("arbitrary", "arbitrary"),
                                             vmem_limit_bytes=48 * 2 ** 20),
    )(c_idx, *gfs, *ts)


def _chip_copies(pbs, lands, smalls, send_sems, recv_sems):
    x, y, c = _place()
    own = 2 * x + y
    me = _dev_index((x, y, c))
    chips = [(1 - x, y), (x, 1 - y), (1 - x, 1 - y)]
    n = len(pbs)
    copies = [pltpu.make_async_remote_copy(
        src_ref=pbs[k].at[2 * chip[0] + chip[1]], dst_ref=lands[k].at[own], send_sem=send_sems.at[3 * k + j],
        recv_sem=recv_sems.at[3 * k + j], device_id=(*chip, c), device_id_type=MESH)
        for k in range(n) for j, chip in enumerate(chips)]
    peers = [(x, y, 1 - c)] + [(*chip, cc) for chip in chips for cc in (c, 1 - c)]
    copies += [pltpu.make_async_remote_copy(
        src_ref=smalls[t].at[me], dst_ref=smalls[t].at[me], send_sem=send_sems.at[3 * n + 7 * t + j],
        recv_sem=recv_sems.at[3 * n + 7 * t + j], device_id=peer, device_id_type=MESH)
        for t in range(len(smalls)) for j, peer in enumerate(peers)]
    return copies


def _chip_start(pbs, smalls, collective_id, name):
    n, m = len(pbs), len(smalls)
    lands = [lax.empty(p.shape, p.dtype) for p in pbs]
    arrays = list(pbs) + lands + list(smalls)
    na = len(arrays)
    nsem = 3 * n + 7 * m

    def body(*refs):
        x, y, c = _place()
        chips = [(1 - x, y), (x, 1 - y), (1 - x, 1 - y)]
        peers = [(*chip, c) for chip in chips]
        if m:
            peers += [(x, y, 1 - c)] + [(*chip, 1 - c) for chip in chips]
        _handshake(peers)
        for cp in _chip_copies(refs[:n], refs[n:2 * n], refs[2 * n:na], refs[2 * na], refs[2 * na + 1]):
            cp.start()
        refs[2 * na + 2][...] = jnp.zeros(_TOKEN.shape, _TOKEN.dtype)

    res = pl.pallas_call(
        body, name=name, in_specs=[_HBM] * na, out_specs=[_HBM] * na + [_SEM, _SEM, _VMEM],
        out_shape=[pltpu.HBM(a.shape, a.dtype) for a in arrays] + [pltpu.SemaphoreType.DMA((nsem,))] * 2 + [_TOKEN],
        input_output_aliases={k: k for k in range(na)},
        compiler_params=pltpu.CompilerParams(has_side_effects=_EFFECT, collective_id=collective_id),
    )(*[pltpu.with_memory_space_constraint(a, pltpu.HBM) for a in arrays])
    return (list(res[:n]), list(res[n:2 * n]), list(res[2 * n:na]), res[na], res[na + 1]), res[na + 2]


def _chip_wait(pbs, lands, smalls, send_sems, recv_sems, after, name):
    n = len(pbs)
    arrays = list(pbs) + list(lands) + list(smalls)
    na = len(arrays)

    def body(*refs):
        for cp in _chip_copies(refs[:n], refs[n:2 * n], refs[2 * n:na], refs[na], refs[na + 1]):
            cp.wait_recv()
            cp.wait_send()

    res = pl.pallas_call(
        body, name=name, in_specs=[_HBM] * na + [_SEM, _SEM, _ANY], out_specs=[_HBM] * na,
        out_shape=[pltpu.HBM(a.shape, a.dtype) for a in arrays],
        input_output_aliases={k: k for k in range(na)},
        compiler_params=pltpu.CompilerParams(has_side_effects=_EFFECT),
    )(*arrays, send_sems, recv_sems, after)
    return list(res[:n]), list(res[n:2 * n]), list(res[2 * n:])


def _chip_sum(pbs, qs, own_idx, states, name):
    n = len(pbs)
    halves = 2
    n_in = 4 * n + 3 * sum(s is not None for s in states)

    def body(o_ref, *refs):
        nxt_in, nxt_out = 4 * n, n_in
        for k in range(n):
            acc = refs[4 * k][...].astype(F32)
            for j in range(1, 4):
                acc = acc + refs[4 * k + j][...].astype(F32)
            refs[nxt_out][...] = acc
            nxt_out += 1
            if states[k] is not None:
                w_ref, m_ref, v_ref = refs[nxt_in:nxt_in + 3]
                nxt_in += 3
                d, nm, nv = _adamw_math(w_ref[...], acc, m_ref[...], v_ref[...])
                refs[nxt_out][...] = d
                refs[nxt_out + 1][...] = nm
                refs[nxt_out + 2][...] = nv
                nxt_out += 3

    def other(j):
        return lambda h, o: ((o[0] + j) % 4, h, 0)

    def rows(r):
        return pl.BlockSpec((r // halves, D), lambda h, o: (h, 0))

    in_specs, ins = [], []
    for p, q in zip(pbs, qs):
        rh = p.shape[1] // halves
        in_specs.append(pl.BlockSpec((None, rh, D), lambda h, o: (o[0], h, 0)))
        ins.append(p)
        for j in range(1, 4):
            in_specs.append(pl.BlockSpec((None, rh, D), other(j)))
            ins.append(q)
    out_specs, out_shape, counts = [], [], []
    for p, s in zip(pbs, states):
        r = p.shape[1]
        if s is not None:
            in_specs += [rows(r)] * 3
            ins += list(s)
        counts.append(1 if s is None else 4)
        out_specs += [rows(r)] * counts[-1]
        out_shape += [_sds((r, D), F32)] * counts[-1]
    res = pl.pallas_call(
        body, name=name,
        grid_spec=pltpu.PrefetchScalarGridSpec(num_scalar_prefetch=1, grid=(halves,), in_specs=in_specs,
                                               out_specs=out_specs),
        out_shape=out_shape,
        compiler_params=pltpu.CompilerParams(dimension_semantics=("arbitrary",), vmem_limit_bytes=56 * 2 ** 20),
    )(own_idx, *ins)
    out, base = [], 0
    for c in counts:
        out.append(tuple(res[base:base + c]))
        base += c
    return out


REDUCE_GROUPS = (("w_gateT", "w_upT", "w_down"), ("w_out", "w_mem_q", "w_mem_kvT", "w_mem_o"), ("w_inT",))


class _Reduce:
    def __init__(self, states):
        x, y, c = _place()
        self._c = c.astype(jnp.int32).reshape(1)
        self._own = (2 * x + y).astype(jnp.int32).reshape(1)
        self._me = _dev_index((x, y, c))
        self._state = {}
        self._adam = states
        self.results = {}
        self.smalls = None

    def start(self, gi, wgrads):
        gbs = [g.reshape(NDEV, g.shape[0] // NDEV, D) for g in wgrads]
        gbs, lands, send_sems, recv_sems, token = _sibling_start(gbs, ID_SIBLING + gi, "reduce_sibling_start_%d" % gi)
        self._state[gi] = (gbs, lands, send_sems, recv_sems)
        return token

    def middle(self, gi, after, smalls=()):
        gbs, lands, send_sems, recv_sems = self._state[gi]
        gbs, ts = _sibling_wait(gbs, lands, send_sems, recv_sems, after, "reduce_sibling_wait_%d" % gi)
        pbs = _pair_sum(gbs, ts, self._c, "reduce_pair_sum_%d" % gi)
        self._state[gi], token = _chip_start(pbs, list(smalls), ID_CHIP + gi, "reduce_chip_start_%d" % gi)
        return token

    def finish(self, gi, after):
        pbs, lands, smalls, send_sems, recv_sems = self._state[gi]
        pbs, qs, smalls = _chip_wait(pbs, lands, smalls, send_sems, recv_sems, after, "reduce_chip_wait_%d" % gi)
        names = REDUCE_GROUPS[gi]
        states = [self._adam.get(n) for n in names]
        self.results.update(zip(names, _chip_sum(pbs, qs, self._own, states, "reduce_update_%d" % gi)))
        if smalls:
            self.smalls = smalls

    def small_zones(self, small, g_wdw):
        pack = jnp.concatenate(
            [jnp.pad(small[name], ((0, 0), (0, lanes - small[name].shape[1]))) for name, lanes, _ in PACK], axis=1)
        zones = []
        for a in (pack, g_wdw):
            zone = lax.empty((NDEV,) + a.shape, a.dtype)
            zones.append(lax.dynamic_update_slice(zone, a[None], (self._me,) + (0,) * a.ndim))
        return zones


def _adamw_math(w, g, m, v):
    m = ADAM_B1 * m + (1.0 - ADAM_B1) * g
    v = ADAM_B2 * v + (1.0 - ADAM_B2) * (g * g)
    m_hat = m / (1.0 - ADAM_B1 ** ADAM_STEP)
    v_hat = v / (1.0 - ADAM_B2 ** ADAM_STEP)
    delta = -ADAM_LR * (m_hat / (jnp.sqrt(v_hat) + ADAM_EPS) + ADAM_WD * w)
    return delta, m, v


def _adamw(w, g, m, v, name):
    R, C = w.shape
    rc = 256 if R % 256 == 0 and R > 256 else R

    def body(w_ref, g_ref, m_ref, v_ref, d_ref, nm_ref, nv_ref):
        d, nm, nv = _adamw_math(w_ref[...], g_ref[...], m_ref[...], v_ref[...])
        d_ref[...] = d
        nm_ref[...] = nm
        nv_ref[...] = nv

    return _hbm_call(
        body, name=name, grid=(R // rc,),
        in_specs=[_rows(rc, C)] * 4, out_specs=[_rows(rc, C)] * 3,
        out_shape=[_sds((R, C), F32)] * 3,
        compiler_params=_params(40),
    )(w, g, m, v)


PACK = (("loss", LANES, 1), ("g_mix", D, D), ("b_in", INC, INC), ("b_dw", CC, CC), ("g_conv_ln", CC, CC),
        ("b_conv_ln", CC, CC), ("attn_sink", LANES, NQ), ("b_out", D, D), ("g_mem_q", D, D), ("g_mem_kv", D, D),
        ("g_ffn", D, D), ("g_final", D, D))
PACK_LANES = sum(p[1] for p in PACK)


def _small_update(packs, gdws, ws, ms, vs):
    names = [p[0] for p in PACK[1:]]
    n = len(names)

    def body(*refs):
        packs_ref, gdws_ref = refs[0], refs[1]
        w_refs, m_refs, v_refs = refs[2:2 + n], refs[2 + n:2 + 2 * n], refs[2 + 2 * n:2 + 3 * n]
        outs = refs[2 + 3 * n:]
        tot = packs_ref[0]
        gdw = gdws_ref[0]
        for s in range(1, NDEV):
            tot = tot + packs_ref[s]
            gdw = gdw + gdws_ref[s]
        outs[0][...] = tot[:, 0:LANES]
        outs[1][...] = gdw
        off = LANES
        for k, (_, lanes, used) in enumerate(PACK[1:]):
            g = tot[:, off:off + used]
            off += lanes
            d, nm, nv = _adamw_math(w_refs[k][...], g, m_refs[k][...], v_refs[k][...])
            o = outs[2 + 4 * k:6 + 4 * k]
            o[0][...] = g
            o[1][...] = d
            o[2][...] = nm
            o[3][...] = nv

    def full(a):
        nd = len(a.shape)
        return pl.BlockSpec(a.shape, lambda i: (0,) * nd)

    ins = [packs, gdws, *ws, *ms, *vs]
    out_shape = [_sds((1, LANES), F32), _sds(gdws.shape[1:], F32)]
    for w in ws:
        out_shape += [_sds(w.shape, F32)] * 4
    return pl.pallas_call(
        body, name="small_update", grid=(1,),
        in_specs=[full(a) for a in ins], out_specs=[full(o) for o in out_shape], out_shape=out_shape,
        compiler_params=_params(40),
    )(*ins)


def kernel(x, mem, g_mix, w_in, b_in, w_dw, b_dw, g_conv_ln, b_conv_ln, attn_sink, w_out, b_out, g_mem_q, g_mem_kv, w_mem_q, w_mem_kv, w_mem_o, g_ffn, w_gate, w_up, w_down, g_final, loss_target, m_g_mix, m_w_in, m_b_in, m_w_dw, m_b_dw, m_g_conv_ln, m_b_conv_ln, m_attn_sink, m_w_out, m_b_out, m_g_mem_q, m_g_mem_kv, m_w_mem_q, m_w_mem_kv, m_w_mem_o, m_g_ffn, m_w_gate, m_w_up, m_w_down, m_g_final, v_g_mix, v_w_in, v_b_in, v_w_dw, v_b_dw, v_g_conv_ln, v_b_conv_ln, v_attn_sink, v_w_out, v_b_out, v_g_mem_q, v_g_mem_kv, v_w_mem_q, v_w_mem_kv, v_w_mem_o, v_g_ffn, v_w_gate, v_w_up, v_w_down, v_g_final):
    given = dict(locals())
    bf = jnp.bfloat16
    xi, yi, ci = _place()
    me = 4 * xi + 2 * yi + ci

    transposed = {"w_inT": w_in, "w_mem_kvT": w_mem_kv, "w_gateT": w_gate, "w_upT": w_up}
    plain = {"w_out": w_out, "w_mem_q": w_mem_q, "w_mem_o": w_mem_o, "w_down": w_down}
    shards = [transposed[n][0].T if n in transposed else plain[n][0] for n in W_NAMES]
    zones = dict(zip(W_NAMES, _landing_zones(shards, me.astype(jnp.int32).reshape(1))))
    dw = jnp.pad(w_dw[0], ((0, 1), (0, 0)))
    zones["w_dw"] = lax.dynamic_update_slice(lax.empty((NDEV,) + dw.shape, dw.dtype), dw[None], (me, 0, 0))
    gather = _Gather(_gather_start([[zones[n] for n in names] for names in GATHER_GROUPS]))

    sp = {n: given[n] for n in ("g_mix", "b_in", "b_dw", "g_conv_ln", "b_conv_ln", "attn_sink", "b_out", "g_mem_q",
                                "g_mem_kv", "g_ffn")}
    sp["g_final"] = g_final.reshape(1, D)
    frames = {"w_inT": ("w_in", True), "w_gateT": ("w_gate", True), "w_upT": ("w_up", True),
              "w_out": ("w_out", False), "w_mem_q": ("w_mem_q", False), "w_mem_o": ("w_mem_o", False),
              "w_down": ("w_down", False)}
    states = {}
    for frame, (n, transposed) in frames.items():
        wmv = (given[n][0], given["m_" + n][0], given["v_" + n][0])
        states[frame] = tuple(a.T for a in wmv) if transposed else wmv
    reduce = _Reduce(states)
    grad_x = _local_step(x[0], mem[0], loss_target[0], sp, gather, reduce)

    grads, deltas, new_m, new_v = {}, {}, {}, {}

    def update(n, g):
        d, nm, nv = _adamw(given[n][0], g, given["m_" + n][0], given["v_" + n][0], "adamw_" + n)
        grads[n], deltas[n], new_m[n], new_v[n] = g[None], d[None], nm[None], nv[None]
        return d

    reduce.finish(2, update("w_mem_kv", reduce.results["w_mem_kvT"][0].T))
    for frame, (n, transposed) in frames.items():
        outs = [a.T if transposed else a for a in reduce.results[frame]]
        grads[n], deltas[n], new_m[n], new_v[n] = [a[None] for a in outs]

    packs, gdws = reduce.smalls
    names_small = [p[0] for p in PACK[1:]]

    def as_row(a):
        return a.reshape(1, -1)

    res = _small_update(packs, gdws, [as_row(given[n]) for n in names_small],
                        [as_row(given["m_" + n]) for n in names_small],
                        [as_row(given["v_" + n]) for n in names_small])
    loss = res[0][0, 0]
    for k, n in enumerate(names_small):
        shape = given[n].shape
        g, d, nm, nv = res[2 + 4 * k:6 + 4 * k]
        grads[n], deltas[n], new_m[n], new_v[n] = (g.reshape(shape), d.reshape(shape), nm.reshape(shape),
                                                   nv.reshape(shape))
    update("w_dw", lax.dynamic_slice(res[1], (0, me * (CC // NDEV)), (TAPS, CC // NDEV)))

    order = ("g_mix", "w_in", "b_in", "w_dw", "b_dw", "g_conv_ln", "b_conv_ln", "attn_sink", "w_out", "b_out",
             "g_mem_q", "g_mem_kv", "w_mem_q", "w_mem_kv", "w_mem_o", "g_ffn", "w_gate", "w_up", "w_down", "g_final")
    return (loss, grad_x[None], *[grads[n] for n in order], *[deltas[n] for n in order],
            *[new_m[n] for n in order], *[new_v[n] for n in order])
```

```python
import jax
import jax.numpy as jnp
from jax import lax
from jax.experimental import pallas as pl
from jax.experimental.pallas import tpu as pltpu

D = 1024
CC = 512
HD = 64
NQ = 8
NKV = 2
GRP = NQ // NKV
ATT = NQ * HD
KVC = NKV * HD
BLK = 128
MH = 4
MHD = D // MH
F = 2816
FCH = 256
FCH_BWD = F
INC = 2 * CC + ATT + 2 * KVC
TAPS = 31
EPS = 1e-6
ROPE_THETA = 10000.0
SCALE = HD ** -0.5
MSCALE = MHD ** -0.5
LANES = 128
NDEV = 8

ADAM_LR = 0.001
ADAM_B1 = 0.9
ADAM_B2 = 0.999
ADAM_EPS = 1e-08
ADAM_WD = 0.01
ADAM_STEP = 10

F32 = jnp.float32
_MXU = jnp.bfloat16
MESH = pl.DeviceIdType.MESH

W_NAMES = ("w_inT", "w_out", "w_mem_q", "w_mem_kvT", "w_mem_o", "w_gateT", "w_upT", "w_down")


def _mm(a, b, ca, cb):
    return lax.dot_general(a.astype(_MXU), b.astype(_MXU), (((ca,), (cb,)), ((), ())),
                           preferred_element_type=F32)


def _nn(a, b):
    return _mm(a, b, 1, 0)


def _nt(a, b):
    return _mm(a, b, 1, 1)


def _tn(a, b):
    return _mm(a, b, 0, 0)


def _sigmoid(x):
    return 0.5 * jnp.tanh(0.5 * x) + 0.5


def _rms(x, g):
    r = lax.rsqrt(jnp.mean(x * x, axis=-1, keepdims=True) + EPS)
    return x * r * g, r


def _rms_bwd(dy, x, r, g):
    xh = x * r
    dg = jnp.sum(dy * xh, axis=0, keepdims=True)
    dxh = dy * g
    dx = r * (dxh - xh * jnp.mean(dxh * xh, axis=-1, keepdims=True))
    return dx, dg


def _rope(t, c, s):
    n = t.shape[1]
    reps = n // LANES
    if reps > 1:
        c = jnp.tile(c, (1, reps))
        s = jnp.tile(s, (1, reps))
    lane = lax.broadcasted_iota(jnp.int32, t.shape, 1)
    first = (lane & (HD - 1)) < (HD // 2)
    partner = jnp.where(first, pltpu.roll(t, n - HD // 2, 1), pltpu.roll(t, HD // 2, 1))
    return t * c + partner * s


def _rope_bwd(dt, c, s):
    n = dt.shape[1]
    reps = n // LANES
    if reps > 1:
        c = jnp.tile(c, (1, reps))
        s = jnp.tile(s, (1, reps))
    lane = lax.broadcasted_iota(jnp.int32, dt.shape, 1)
    first = (lane & (HD - 1)) < (HD // 2)
    ds = dt * s
    partner = jnp.where(first, pltpu.roll(ds, n - HD // 2, 1), pltpu.roll(ds, HD // 2, 1))
    return dt * c + partner


def _acc_init(i, *refs):
    @pl.when(i == 0)
    def _():
        for r in refs:
            r[...] = jnp.zeros(r.shape, r.dtype)


def _acc_row(ref, v):
    ref[...] += jnp.broadcast_to(v, ref.shape)


def _rows(tm, n):
    return pl.BlockSpec((tm, n), lambda i: (i, 0))


def _full(shape):
    nd = len(shape)
    return pl.BlockSpec(shape, lambda i: (0,) * nd)


def _once(shape):
    nd = len(shape)
    return pl.BlockSpec(shape, lambda i: (0,) * nd, pipeline_mode=pl.Buffered(1))


def _params(vmem_mb):
    return pltpu.CompilerParams(dimension_semantics=("arbitrary",), vmem_limit_bytes=vmem_mb * 2 ** 20)


def _sds(shape, dtype):
    return pltpu.HBM(shape, dtype)


def _hbm_call(*args, **kwargs):
    call = pl.pallas_call(*args, **kwargs)
    return lambda *ops: call(*[pltpu.with_memory_space_constraint(o, pltpu.HBM) for o in ops])


def _in_proj_fwd(x, g_mix, winT, b_in, cosq, sinq, tm=512):
    S = x.shape[0]

    def body(x_ref, g_ref, w_ref, b_ref, c_ref, s_ref, h_ref, uglu_ref, vg_ref, q_ref, k_ref, v_ref):
        h, _ = _rms(x_ref[...], g_ref[...])
        hb = h.astype(h_ref.dtype)
        h_ref[...] = hb
        u = _nt(hb, w_ref[...]) + b_ref[...]
        uglu_ref[...] = u[:, :2 * CC].astype(uglu_ref.dtype)
        vg_ref[...] = u[:, :CC] * _sigmoid(u[:, CC:2 * CC])
        c = c_ref[...]
        s = s_ref[...]
        q_ref[...] = (_rope(u[:, 2 * CC:2 * CC + ATT], c, s) * SCALE).astype(q_ref.dtype)
        k_ref[...] = _rope(u[:, 2 * CC + ATT:2 * CC + ATT + KVC], c, s).astype(k_ref.dtype)
        v_ref[...] = u[:, 2 * CC + ATT + KVC:].astype(v_ref.dtype)

    return _hbm_call(
        body, name="in_proj_fwd", grid=(S // tm,),
        in_specs=[_rows(tm, D), _full((1, D)), _full((INC, D)), _full((1, INC)), _rows(tm, LANES), _rows(tm, LANES)],
        out_specs=[_rows(tm, D), _rows(tm, 2 * CC), _rows(tm, CC), _rows(tm, ATT), _rows(tm, KVC), _rows(tm, KVC)],
        out_shape=[_sds((S, D), _MXU), _sds((S, 2 * CC), _MXU), _sds((S, CC), F32), _sds((S, ATT), _MXU),
                   _sds((S, KVC), _MXU), _sds((S, KVC), _MXU)],
        compiler_params=_params(48),
    )(x, g_mix, winT, b_in, cosq, sinq)


CONV_HALO = 16


CONV_TILE = 512


def _shift_scratch(tm, sets):
    return pltpu.VMEM((sets, CC // LANES, 8, tm + 24, LANES), F32)


def _shifted_copies(pad_ref, sh_ref, t0, tm):
    for cl in range(CC // LANES):
        blk = pad_ref[pl.ds(t0, tm + 2 * CONV_HALO), cl * LANES:(cl + 1) * LANES]
        for b in range(8):
            sh_ref[cl, b, :, :] = blk[b:b + tm + 24, :]


def _conv_taps():
    return [(8 * a + b - 1, b, 8 * a) for b in range(8) for a in range(4) if 0 <= 8 * a + b - 1 < TAPS]


CONV_ROWS = 64
SUBLANES = 8


def _dwconv_tile(sh_ref, tm, w_ref, flip, out_ref):
    for cl in range(CC // LANES):
        lanes = slice(cl * LANES, (cl + 1) * LANES)

        def chunk(c, carry):
            r = pl.multiple_of(c * CONV_ROWS, CONV_ROWS)
            accs = [None] * 4
            for n, (k, b, off) in enumerate(_conv_taps()):
                kk = TAPS - 1 - k if flip else k
                term = sh_ref[cl, b, pl.ds(off + r, CONV_ROWS), :] * w_ref[kk:kk + 1, lanes]
                accs[n % 4] = term if accs[n % 4] is None else accs[n % 4] + term
            out_ref[pl.ds(r, CONV_ROWS), lanes] = (accs[0] + accs[1]) + (accs[2] + accs[3])
            return carry

        lax.fori_loop(0, tm // CONV_ROWS, chunk, 0)


def _fill_padded(i, src_hbm, pad_ref, sem, S):
    @pl.when(i == 0)
    def _():
        zeros = jnp.zeros((CONV_HALO, pad_ref.shape[1]), pad_ref.dtype)
        pad_ref[0:CONV_HALO, :] = zeros
        pad_ref[CONV_HALO + S:2 * CONV_HALO + S, :] = zeros
        cp = pltpu.make_async_copy(src_hbm, pad_ref.at[pl.ds(CONV_HALO, S), :], sem)
        cp.start()
        cp.wait()


def _ln_silu(y, g, b):
    mu = jnp.mean(y, axis=-1, keepdims=True)
    yc = y - mu
    var = jnp.mean(yc * yc, axis=-1, keepdims=True)
    rstd = lax.rsqrt(var + EPS)
    yh = yc * rstd
    z = yh * g + b
    return yh, rstd, z


def _conv_fwd(vg, w_dw, b_dw, ln_g, ln_b):
    S = vg.shape[0]
    tm = min(CONV_TILE, S)

    def body(vg_hbm, w_ref, b_ref, g_ref, bb_ref, y_ref, yc_ref, pad_ref, sh_ref, sem):
        i = pl.program_id(0)
        _fill_padded(i, vg_hbm, pad_ref, sem, S)
        _shifted_copies(pad_ref, sh_ref.at[0], pl.multiple_of(i * tm, tm), tm)
        _dwconv_tile(sh_ref.at[0], tm, w_ref, False, y_ref)
        y = y_ref[...] + b_ref[...]
        y_ref[...] = y
        _, _, z = _ln_silu(y, g_ref[...], bb_ref[...])
        yc_ref[...] = (z * _sigmoid(z)).astype(yc_ref.dtype)

    return _hbm_call(
        body, name="conv_fwd", grid=(S // tm,),
        in_specs=[pl.BlockSpec(memory_space=pl.ANY), _full((TAPS, CC)), _full((1, CC)), _full((1, CC)), _full((1, CC))],
        out_specs=[_rows(tm, CC), _rows(tm, CC)],
        out_shape=[_sds((S, CC), F32), _sds((S, CC), _MXU)],
        scratch_shapes=[pltpu.VMEM((S + 2 * CONV_HALO, CC), F32), _shift_scratch(tm, 1), pltpu.SemaphoreType.DMA],
        compiler_params=_params(40),
    )(vg, w_dw, b_dw, ln_g, ln_b)


def _attn_window(i, S):
    start = jnp.clip((i - 1) * BLK, 0, S - 3 * BLK)
    return pl.multiple_of(start, BLK)


ATTN_COLS = GRP * BLK


def _attn_mask_base():
    keys = lax.broadcasted_iota(jnp.int32, (3 * BLK, ATTN_COLS), 0)
    cols = lax.broadcasted_iota(jnp.int32, (3 * BLK, ATTN_COLS), 1)
    return keys - (cols & (BLK - 1))


def _attn_mask(base, i, start):
    return jnp.abs(base + (start - i * BLK)) <= BLK


def _stack_heads(t, hk):
    return jnp.concatenate([t[:, (hk * GRP + g) * HD:(hk * GRP + g + 1) * HD] for g in range(GRP)], axis=0)


def _unstack_heads_t(t):
    cols = []
    for p in range(GRP // 2):
        pair = jnp.concatenate([t[:, (2 * p) * BLK:(2 * p + 1) * BLK], t[:, (2 * p + 1) * BLK:(2 * p + 2) * BLK]],
                               axis=0)
        cols.append(pair.T)
    return jnp.concatenate(cols, axis=1)


def _attn_probs(qs, kh, sk_ref, hk, mask):
    s = jnp.where(mask, _nt(kh, qs), -1e30)
    sk = jnp.concatenate(
        [jnp.broadcast_to(sk_ref[0:1, hk * GRP + g:hk * GRP + g + 1], (1, BLK)) for g in range(GRP)], axis=1)
    m = jnp.maximum(jnp.max(s, axis=0, keepdims=True), sk)
    e = jnp.exp(s - m)
    es = jnp.exp(sk - m)
    inv = 1.0 / (jnp.sum(e, axis=0, keepdims=True) + es)
    return e, inv, es


ATTN_BLOCKS = 8


def _attn_fwd(qr, kr, vv, sink):
    S = qr.shape[0]
    tq = ATTN_BLOCKS * BLK

    def body(q_ref, k_ref, v_ref, sk_ref, o_ref):
        base = _attn_mask_base()
        for b in range(ATTN_BLOCKS):
            i = pl.program_id(0) * ATTN_BLOCKS + b
            rows = slice(b * BLK, (b + 1) * BLK)
            start = _attn_window(i, S)
            kb = k_ref[pl.ds(start, 3 * BLK), :]
            vb = v_ref[pl.ds(start, 3 * BLK), :]
            q = q_ref[rows, :]
            mask = _attn_mask(base, i, start)
            outs = []
            for hk in range(NKV):
                hs = slice(hk * HD, (hk + 1) * HD)
                e, inv, _ = _attn_probs(_stack_heads(q, hk), kb[:, hs], sk_ref, hk, mask)
                outs.append(_unstack_heads_t(_tn(vb[:, hs], e) * inv))
            o_ref[rows, :] = jnp.concatenate(outs, axis=1).astype(o_ref.dtype)

    return _hbm_call(
        body, name="attn_fwd", grid=(S // tq,),
        in_specs=[_rows(tq, ATT), _full((S, KVC)), _full((S, KVC)), _full((1, NQ))],
        out_specs=_rows(tq, ATT),
        out_shape=_sds((S, ATT), _MXU),
        compiler_params=_params(40),
    )(qr, kr, vv, sink)


def _out_proj_fwd(x, yc, ya, wout, b_out, tm=512):
    S = x.shape[0]

    def body(x_ref, yc_ref, ya_ref, w_ref, b_ref, x1_ref, ymix_ref):
        ymix = jnp.concatenate([yc_ref[...], ya_ref[...]], axis=1)
        ymix_ref[...] = ymix
        x1_ref[...] = x_ref[...] + _nn(ymix, w_ref[...]) + b_ref[...]

    return _hbm_call(
        body, name="out_proj_fwd", grid=(S // tm,),
        in_specs=[_rows(tm, D), _rows(tm, CC), _rows(tm, ATT), _full((D, D)), _full((1, D))],
        out_specs=[_rows(tm, D), _rows(tm, D)],
        out_shape=[_sds((S, D), F32), _sds((S, D), _MXU)],
        compiler_params=_params(40),
    )(x, yc, ya, wout, b_out)


def _softmax_t(s):
    m = jnp.max(s, axis=0, keepdims=True)
    e = jnp.exp(s - m)
    return e, 1.0 / jnp.sum(e, axis=0, keepdims=True)


def _softmax(s):
    m = jnp.max(s, axis=-1, keepdims=True)
    e = jnp.exp(s - m)
    return e * (1.0 / jnp.sum(e, axis=-1, keepdims=True))


def _mem_attn_fwd(x1, g_q, wq, mem, g_kv, wkvT, wo, after, tm=512):
    S = x1.shape[0]
    M = mem.shape[0]

    def body(x1_ref, g_ref, wq_ref, mem_ref, gkv_ref, wkv_ref, wo_ref, after_ref,
             x2_ref, qm_ref, om_ref, km_ref, vm_ref, mn_ref):
        @pl.when(pl.program_id(0) == 0)
        def _():
            memn, _ = _rms(mem_ref[...], gkv_ref[...])
            kv = _nt(memn, wkv_ref[...])
            km_ref[...] = kv[:, :D].astype(km_ref.dtype)
            vm_ref[...] = kv[:, D:].astype(vm_ref.dtype)
            mn_ref[...] = memn.astype(mn_ref.dtype)

        x1v = x1_ref[...]
        hq, _ = _rms(x1v, g_ref[...])
        qm = (_nn(hq, wq_ref[...]) * MSCALE).astype(qm_ref.dtype)
        qm_ref[...] = qm
        oms = []
        for h in range(MH):
            sl = slice(h * MHD, (h + 1) * MHD)
            p = _softmax(_nt(qm[:, sl], km_ref[:, sl]))
            oms.append(_nn(p, vm_ref[:, sl]))
        om = jnp.concatenate(oms, axis=1).astype(om_ref.dtype)
        om_ref[...] = om
        x2_ref[...] = x1v + _nn(om, wo_ref[...])

    return _hbm_call(
        body, name="mem_attn_fwd", grid=(S // tm,),
        in_specs=[_rows(tm, D), _full((1, D)), _full((D, D)), _full((M, D)), _full((1, D)), _once((2 * D, D)),
                  _full((D, D)), _AFTER],
        out_specs=[_rows(tm, D)] * 3 + [_full((M, D))] * 3,
        out_shape=[_sds((S, D), F32), _sds((S, D), _MXU), _sds((S, D), _MXU)] + [_sds((M, D), _MXU)] * 3,
        compiler_params=_params(48),
    )(x1, g_q, wq, mem, g_kv, wkvT, wo, after)


def _ffn_loss_fwd(x2, tgt, g_ffn, g_final, wgT, wuT, wd, tm=512):
    S = x2.shape[0]

    def body(x2_ref, t_ref, gf_ref, gl_ref, wg_ref, wu_ref, wd_ref,
             gt_ref, up_ref, hf_ref, dx3_ref, dx3b_ref, loss_ref, dgl_ref):
        i = pl.program_id(0)
        _acc_init(i, loss_ref, dgl_ref)
        x2v = x2_ref[...]
        hf, _ = _rms(x2v, gf_ref[...])
        hb = hf.astype(hf_ref.dtype)
        hf_ref[...] = hb
        x3 = x2v
        for c in range(F // FCH):
            sl = slice(c * FCH, (c + 1) * FCH)
            gt = _nt(hb, wg_ref[sl, :])
            up = _nt(hb, wu_ref[sl, :])
            gt_ref[:, sl] = gt.astype(gt_ref.dtype)
            up_ref[:, sl] = up.astype(up_ref.dtype)
            x3 = x3 + _nn(gt * _sigmoid(gt) * up, wd_ref[sl, :])
        gl = gl_ref[...]
        y, r = _rms(x3, gl)
        diff = y - t_ref[...]
        _acc_row(loss_ref, jnp.sum(jnp.sum(diff * diff, axis=-1, keepdims=True), axis=0, keepdims=True) * (0.5 / D))
        dx3, dgl = _rms_bwd(diff * (1.0 / D), x3, r, gl)
        dx3_ref[...] = dx3
        dx3b_ref[...] = dx3.astype(dx3b_ref.dtype)
        _acc_row(dgl_ref, dgl)

    return _hbm_call(
        body, name="ffn_loss_fwd", grid=(S // tm,),
        in_specs=[_rows(tm, D), _rows(tm, D), _full((1, D)), _full((1, D)),
                  _once((F, D)), _once((F, D)), _once((F, D))],
        out_specs=[_rows(tm, F), _rows(tm, F), _rows(tm, D), _rows(tm, D), _rows(tm, D),
                   _full((8, LANES)), _full((8, D))],
        out_shape=[_sds((S, F), _MXU), _sds((S, F), _MXU), _sds((S, D), _MXU), _sds((S, D), F32),
                   _sds((S, D), _MXU), _sds((8, LANES), F32), _sds((8, D), F32)],
        compiler_params=_params(56),
    )(x2, tgt, g_ffn, g_final, wgT, wuT, wd)


WCH = 256


def _ffn_bwd(dx3, x2, gt, up, g_ffn, wd, wgT, wuT, tm=256):
    S = x2.shape[0]

    def body(dx3_ref, x2_ref, gt_ref, up_ref, g_ref, wd_ref, wg_ref, wu_ref,
             dx2_ref, dx2b_ref, dgt_ref, dup_ref, act_ref, dg_ref):
        i = pl.program_id(0)
        _acc_init(i, dg_ref)
        dx3v = dx3_ref[...]
        db = dx3v.astype(_MXU)
        dhf = jnp.zeros((tm, D), F32)
        for c in range(F // FCH_BWD):
            sl = slice(c * FCH_BWD, (c + 1) * FCH_BWD)
            gtv = gt_ref[:, sl].astype(F32)
            upv = up_ref[:, sl].astype(F32)
            dact = _nt(db, wd_ref[sl, :])
            sig = _sigmoid(gtv)
            silu = gtv * sig
            dup = (dact * silu).astype(dup_ref.dtype)
            dgt = (dact * upv * (sig * (1.0 + gtv * (1.0 - sig)))).astype(dgt_ref.dtype)
            act = (silu * upv).astype(act_ref.dtype)
            for j in range(FCH_BWD // WCH):
                cj = c * (FCH_BWD // WCH) + j
                act_ref[cj] = act[:, j * WCH:(j + 1) * WCH]
                dgt_ref[cj] = dgt[:, j * WCH:(j + 1) * WCH]
                dup_ref[cj] = dup[:, j * WCH:(j + 1) * WCH]
            dhf = dhf + _nn(dgt, wg_ref[sl, :]) + _nn(dup, wu_ref[sl, :])
        x2v = x2_ref[...]
        g = g_ref[...]
        _, r = _rms(x2v, g)
        dxn, dg = _rms_bwd(dhf, x2v, r, g)
        dx2 = dx3v + dxn
        dx2_ref[...] = dx2
        dx2b_ref[...] = dx2.astype(dx2b_ref.dtype)
        _acc_row(dg_ref, dg)

    return _hbm_call(
        body, name="ffn_bwd", grid=(S // tm,),
        in_specs=[_rows(tm, D), _rows(tm, D), _rows(tm, F), _rows(tm, F), _full((1, D)),
                  _once((F, D)), _once((F, D)), _once((F, D))],
        out_specs=[_rows(tm, D), _rows(tm, D)] + [pl.BlockSpec((F // WCH, tm, WCH), lambda i: (0, i, 0))] * 3
                  + [_full((8, D))],
        out_shape=[_sds((S, D), F32), _sds((S, D), _MXU)] + [_sds((F // WCH, S, WCH), _MXU)] * 3
                  + [_sds((8, D), F32)],
        compiler_params=_params(56),
    )(dx3, x2, gt, up, g_ffn, wd, wgT, wuT)


_AFTER = pl.BlockSpec(memory_space=pl.ANY)


def _wgrad(pairs, name, after, rc=256):
    major = len(pairs[0][0].shape) == 3
    if major:
        S, R = pairs[0][0].shape[1], pairs[0][0].shape[0] * rc
        left = pl.BlockSpec((None, S, rc), lambda j: (j, 0, 0))
    else:
        S, R = pairs[0][0].shape
        left = pl.BlockSpec((S, rc), lambda j: (0, j))
    N = pairs[0][1].shape[1]
    rights = []
    for _, b in pairs:
        if not any(b is r for r in rights):
            rights.append(b)
    which = [next(i for i, r in enumerate(rights) if r is b) for _, b in pairs]
    n, nb = len(pairs), len(rights)

    def body(*refs):
        for k in range(n):
            refs[n + nb + 1 + k][...] = _tn(refs[k][...], refs[n + which[k]][...]).astype(jnp.bfloat16)

    res = _hbm_call(
        body, name=name, grid=(R // rc,),
        in_specs=[left] * n + [_once((S, N))] * nb + [_AFTER],
        out_specs=[_rows(rc, N)] * n,
        out_shape=[_sds((R, N), jnp.bfloat16)] * n,
        compiler_params=_params(56),
    )(*[a for a, _ in pairs], *rights, after)
    return list(res)


def _wgrad_acc(pairs, name, after, ts=1024):
    S, R = pairs[0][0].shape
    N = pairs[0][1].shape[1]
    n = len(pairs)

    def body(*refs):
        outs, accs = refs[2 * n + 1:3 * n + 1], refs[3 * n + 1:]
        i = pl.program_id(0)
        _acc_init(i, *accs)
        for k in range(n):
            accs[k][...] += _tn(refs[k][...], refs[n + k][...])

        @pl.when(i == pl.num_programs(0) - 1)
        def _():
            for k in range(n):
                outs[k][...] = accs[k][...].astype(outs[k].dtype)

    res = _hbm_call(
        body, name=name, grid=(S // ts,),
        in_specs=[_rows(ts, R)] * n + [_rows(ts, N)] * n + [_AFTER],
        out_specs=[_full((R, N))] * n,
        out_shape=[_sds((R, N), jnp.bfloat16)] * n,
        scratch_shapes=[pltpu.VMEM((R, N), F32)] * n,
        compiler_params=_params(56),
    )(*[a for a, _ in pairs], *[b for _, b in pairs], after)
    return list(res)


def _mem_attn_bwd(dx2, x1, qm, km, vm, g_q, wq, wo, wout, y, ln_g, ln_b, after, tm=512):
    S = x1.shape[0]
    M = km.shape[0]

    def body(dx2_ref, x1_ref, qm_ref, km_ref, vm_ref, g_ref, wq_ref, wo_ref, wout_ref, y_ref, lg_ref, lb_ref,
             after_ref, dx1_ref, dx1b_ref, hq_ref, dqm_ref, dy_ref, dya_ref, dkm_ref, dvm_ref, dg_ref, dbo_ref,
             dlg_ref, dlb_ref, dbdw_ref):
        i = pl.program_id(0)
        _acc_init(i, dkm_ref, dvm_ref, dg_ref, dbo_ref, dlg_ref, dlb_ref, dbdw_ref)
        dx2v = dx2_ref[...]
        domt = _nt(wo_ref[...], dx2v).astype(_MXU)
        dqs = []
        for h in range(MH):
            sl = slice(h * MHD, (h + 1) * MHD)
            qh = qm_ref[:, sl]
            kh = km_ref[:, sl]
            e, inv = _softmax_t(_nt(kh, qh))
            p = e * inv
            dp = _nn(vm_ref[:, sl], domt[sl, :])
            ds = (p * (dp - jnp.sum(dp * p, axis=0, keepdims=True))).astype(_MXU)
            dqs.append(_tn(ds, kh) * MSCALE)
            dkm_ref[:, sl] += _nn(ds, qh)
            dvm_ref[:, sl] += _nt(p, domt[sl, :])
        dqm = jnp.concatenate(dqs, axis=1).astype(dqm_ref.dtype)
        dqm_ref[...] = dqm
        dhq = _nt(dqm, wq_ref[...])
        x1v = x1_ref[...]
        g = g_ref[...]
        hq, r = _rms(x1v, g)
        hq_ref[...] = hq.astype(hq_ref.dtype)
        dxn, dg = _rms_bwd(dhq, x1v, r, g)
        dx1 = dx2v + dxn
        dx1_ref[...] = dx1
        dx1b = dx1.astype(dx1b_ref.dtype)
        dx1b_ref[...] = dx1b
        dym = _nt(dx1b, wout_ref[...])
        dya_ref[...] = dym[:, CC:].astype(dya_ref.dtype)
        _acc_row(dg_ref, dg)
        _acc_row(dbo_ref, jnp.sum(dx1, axis=0, keepdims=True))
        lg = lg_ref[...]
        yh, rstd, z = _ln_silu(y_ref[...], lg, lb_ref[...])
        sig = _sigmoid(z)
        dz = dym[:, :CC] * (sig * (1.0 + z * (1.0 - sig)))
        dyh = dz * lg
        dy = rstd * (dyh - jnp.mean(dyh, axis=-1, keepdims=True) - yh * jnp.mean(dyh * yh, axis=-1, keepdims=True))
        dy_ref[...] = dy
        _acc_row(dlg_ref, jnp.sum(dz * yh, axis=0, keepdims=True))
        _acc_row(dlb_ref, jnp.sum(dz, axis=0, keepdims=True))
        _acc_row(dbdw_ref, jnp.sum(dy, axis=0, keepdims=True))

    return _hbm_call(
        body, name="mem_attn_bwd", grid=(S // tm,),
        in_specs=[_rows(tm, D), _rows(tm, D), _rows(tm, D), _full((M, D)), _full((M, D)), _full((1, D)),
                  _once((D, D)), _once((D, D)), _once((D, D)), _rows(tm, CC), _full((1, CC)), _full((1, CC)),
                  _AFTER],
        out_specs=[_rows(tm, D), _rows(tm, D), _rows(tm, D), _rows(tm, D), _rows(tm, CC), _rows(tm, ATT),
                   _full((M, D)), _full((M, D)), _full((8, D)), _full((8, D)),
                   _full((8, CC)), _full((8, CC)), _full((8, CC))],
        out_shape=[_sds((S, D), F32), _sds((S, D), _MXU), _sds((S, D), _MXU), _sds((S, D), _MXU),
                   _sds((S, CC), F32), _sds((S, ATT), _MXU),
                   _sds((M, D), F32), _sds((M, D), F32), _sds((8, D), F32), _sds((8, D), F32),
                   _sds((8, CC), F32), _sds((8, CC), F32), _sds((8, CC), F32)],
        compiler_params=_params(56),
    )(dx2, x1, qm, km, vm, g_q, wq, wo, wout, y, ln_g, ln_b, after)


def _mem_kv_bwd(dkm, dvm, mem, g_kv, memn, wkvT):
    M = mem.shape[0]

    def body(dkm_ref, dvm_ref, mem_ref, g_ref, mn_ref, w_ref, gw_ref, dg_ref):
        dkv = jnp.concatenate([dkm_ref[...], dvm_ref[...]], axis=1).astype(_MXU)
        gw_ref[...] = _tn(dkv, mn_ref[...]).astype(gw_ref.dtype)
        dmn = _nn(dkv, w_ref[...])
        memv = mem_ref[...]
        g = g_ref[...]
        _, r = _rms(memv, g)
        _, dg = _rms_bwd(dmn, memv, r, g)
        dg_ref[...] = jnp.broadcast_to(dg, dg_ref.shape)

    return _hbm_call(
        body, name="mem_kv_bwd", grid=(1,),
        in_specs=[_full((M, D)), _full((M, D)), _full((M, D)), _full((1, D)), _full((M, D)), _full((2 * D, D))],
        out_specs=[_full((2 * D, D)), _full((8, D))],
        out_shape=[_sds((2 * D, D), jnp.bfloat16), _sds((8, D), F32)],
        compiler_params=_params(48),
    )(dkm, dvm, mem, g_kv, memn, wkvT)


def _attn_bwd(qr, kr, vv, sink, dya, after):
    S = qr.shape[0]
    tq = ATTN_BLOCKS * BLK

    def body(q_ref, k_ref, v_ref, sk_ref, do_ref, after_ref, dq_ref, dk_ref, dv_ref, dsk_ref, sink_acc):
        _acc_init(pl.program_id(0), dk_ref, dv_ref, sink_acc)
        base = _attn_mask_base()
        for b in range(ATTN_BLOCKS):
            i = pl.program_id(0) * ATTN_BLOCKS + b
            rows = slice(b * BLK, (b + 1) * BLK)
            start = _attn_window(i, S)
            kb = k_ref[pl.ds(start, 3 * BLK), :]
            vb = v_ref[pl.ds(start, 3 * BLK), :]
            q = q_ref[rows, :]
            do = do_ref[rows, :]
            mask = _attn_mask(base, i, start)
            dqs, dks, dvs = [], [], []
            for hk in range(NKV):
                hs = slice(hk * HD, (hk + 1) * HD)
                qs = _stack_heads(q, hk)
                dos = _stack_heads(do, hk)
                e, inv, es = _attn_probs(qs, kb[:, hs], sk_ref, hk, mask)
                p = e * inv
                dp = _nt(vb[:, hs], dos)
                delta = jnp.sum(dp * p, axis=0, keepdims=True)
                ds = (p * (dp - delta)).astype(_MXU)
                dqs.append(_unstack_heads_t(_tn(kb[:, hs], ds) * SCALE))
                dks.append(_nn(ds, qs))
                dvs.append(_nn(p, dos))
                sink_acc[hk:hk + 1, :] -= es * inv * delta
            dq_ref[rows, :] = jnp.concatenate(dqs, axis=1)
            dk_ref[pl.ds(start, 3 * BLK), :] += jnp.concatenate(dks, axis=1)
            dv_ref[pl.ds(start, 3 * BLK), :] += jnp.concatenate(dvs, axis=1)

        @pl.when(pl.program_id(0) == pl.num_programs(0) - 1)
        def _():
            for h in range(NQ):
                tot = jnp.sum(sink_acc[h // GRP:h // GRP + 1, (h % GRP) * BLK:(h % GRP + 1) * BLK], axis=1,
                              keepdims=True)
                dsk_ref[h:h + 1, :] = jnp.broadcast_to(tot, (1, LANES))

    return _hbm_call(
        body, name="attn_bwd", grid=(S // tq,),
        in_specs=[_rows(tq, ATT), _full((S, KVC)), _full((S, KVC)), _full((1, NQ)), _rows(tq, ATT), _AFTER],
        out_specs=[_rows(tq, ATT), _full((S, KVC)), _full((S, KVC)), _full((8, LANES))],
        out_shape=[_sds((S, ATT), F32), _sds((S, KVC), F32), _sds((S, KVC), F32), _sds((8, LANES), F32)],
        scratch_shapes=[pltpu.VMEM((8, ATTN_COLS), F32)],
        compiler_params=_params(40),
    )(qr, kr, vv, sink, dya, after)


def _conv_bwd(dy, vg, uglu, w_dw, after):
    S = dy.shape[0]
    tm = min(CONV_TILE, S)

    def body(dy_hbm, vg_hbm, uglu_ref, w_ref, after_ref, du_ref, gw_ref, dbin_ref, dyp_ref, vgp_ref, gacc_ref, sh_ref,
             dvg_ref, sems):
        i = pl.program_id(0)
        _acc_init(i, gacc_ref, dbin_ref)
        _fill_padded(i, dy_hbm, dyp_ref, sems.at[0], S)
        _fill_padded(i, vg_hbm, vgp_ref, sems.at[1], S)
        t0 = pl.multiple_of(i * tm, tm)
        _shifted_copies(dyp_ref, sh_ref.at[0], t0, tm)
        _shifted_copies(vgp_ref, sh_ref.at[1], t0, tm)
        _dwconv_tile(sh_ref.at[0], tm, w_ref, True, dvg_ref)
        dvg = dvg_ref[...]
        sub = CONV_ROWS
        for cl in range(CC // LANES):
            lanes = slice(cl * LANES, (cl + 1) * LANES)
            for k, b, off in _conv_taps():
                part = jnp.zeros((8, LANES), F32)
                for r in range(0, tm, sub):
                    prod = (sh_ref[0, cl, 0, CONV_HALO + r:CONV_HALO + r + sub, :]
                            * sh_ref[1, cl, b, off + r:off + r + sub, :])
                    part = part + jnp.sum(prod.reshape(sub // 8, 8, LANES), axis=0)
                gacc_ref[k, :, lanes] += part
        a_v = uglu_ref[:, :CC].astype(F32)
        sg = _sigmoid(uglu_ref[:, CC:].astype(F32))
        da = dvg * sg
        dgate = dvg * a_v * sg * (1.0 - sg)
        du = jnp.concatenate([da, dgate], axis=1)
        du_ref[...] = du.astype(du_ref.dtype)
        _acc_row(dbin_ref, jnp.sum(du, axis=0, keepdims=True))

        @pl.when(i == pl.num_programs(0) - 1)
        def _():
            gw_ref[...] = jnp.zeros(gw_ref.shape, F32)
            for k in range(TAPS):
                gw_ref[k:k + 1, :] = jnp.sum(gacc_ref[k], axis=0, keepdims=True)

    return _hbm_call(
        body, name="conv_bwd", grid=(S // tm,),
        in_specs=[pl.BlockSpec(memory_space=pl.ANY), pl.BlockSpec(memory_space=pl.ANY), _rows(tm, 2 * CC),
                  _full((TAPS, CC)), _AFTER],
        out_specs=[_rows(tm, 2 * CC), _full((TAPS + 1, CC)), _full((8, 2 * CC))],
        out_shape=[_sds((S, 2 * CC), _MXU), _sds((TAPS + 1, CC), F32), _sds((8, 2 * CC), F32)],
        scratch_shapes=[pltpu.VMEM((S + 2 * CONV_HALO, CC), F32), pltpu.VMEM((S + 2 * CONV_HALO, CC), F32),
                        pltpu.VMEM((TAPS, 8, CC), F32), _shift_scratch(tm, 2), pltpu.VMEM((tm, CC), F32),
                        pltpu.SemaphoreType.DMA((2,))],
        compiler_params=_params(56),
    )(dy, vg, uglu, w_dw, after)


def _in_proj_bwd(duglu, dqr, dkr, dv, cosq, sinq, x, dx1, g_mix, winT, tm=512):
    S = x.shape[0]
    NA = ATT + 2 * KVC

    def body(dug_ref, dq_ref, dk_ref, dv_ref, c_ref, s_ref, x_ref, dx1_ref, g_ref, w_ref,
             dx_ref, du_ref, dg_ref, dbin_ref):
        i = pl.program_id(0)
        _acc_init(i, dg_ref, dbin_ref)
        c = c_ref[...]
        s = s_ref[...]
        datt = jnp.concatenate([_rope_bwd(dq_ref[...], c, s), _rope_bwd(dk_ref[...], c, s), dv_ref[...]], axis=1)
        _acc_row(dbin_ref, jnp.sum(datt, axis=0, keepdims=True))
        du = jnp.concatenate([dug_ref[...], datt.astype(du_ref.dtype)], axis=1)
        du_ref[...] = du
        dh = _nn(du, w_ref[...])
        xv = x_ref[...]
        g = g_ref[...]
        _, r = _rms(xv, g)
        dxn, dg = _rms_bwd(dh, xv, r, g)
        dx_ref[...] = dx1_ref[...] + dxn
        _acc_row(dg_ref, dg)

    return _hbm_call(
        body, name="in_proj_bwd", grid=(S // tm,),
        in_specs=[_rows(tm, 2 * CC), _rows(tm, ATT), _rows(tm, KVC), _rows(tm, KVC), _rows(tm, LANES),
                  _rows(tm, LANES), _rows(tm, D), _rows(tm, D), _full((1, D)), _once((INC, D))],
        out_specs=[_rows(tm, D), _rows(tm, INC), _full((8, D)), _full((8, NA))],
        out_shape=[_sds((S, D), F32), _sds((S, INC), _MXU), _sds((8, D), F32), _sds((8, NA), F32)],
        compiler_params=_params(48),
    )(duglu, dqr, dkr, dv, cosq, sinq, x, dx1, g_mix, winT)


def _rope_tables(S):
    pos = jnp.arange(S, dtype=F32)
    inv_freq = ROPE_THETA ** (-jnp.arange(0, HD, 2, dtype=F32) / HD)
    ang = pos[:, None] * inv_freq[None, :]
    cos, sin = jnp.cos(ang), jnp.sin(ang)
    cosq = jnp.tile(jnp.concatenate([cos, cos], axis=1), (1, LANES // HD))
    sinq = jnp.tile(jnp.concatenate([-sin, sin], axis=1), (1, LANES // HD))
    return cosq, sinq


GATHER_GROUPS = (("w_inT", "w_dw"), ("w_out", "w_mem_kvT", "w_mem_q", "w_mem_o"), ("w_gateT", "w_upT"), ("w_down",))
ID_GATHER, ID_SIBLING, ID_CHIP = 1, 5, 8


def _local_step(x, mem, tgt, sp, gather, reduce):
    S = x.shape[0]
    assert S % (ATTN_BLOCKS * BLK) == 0 and S % 512 == 0, S
    cosq, sinq = _rope_tables(S)

    gather.forward(0, cosq)
    w0 = gather.finish(0, sinq)
    winT, w_dw_full = w0["w_inT"], w0["w_dw"]
    h, uglu, vg, qr, kr, vv = _in_proj_fwd(x, sp["g_mix"], winT, sp["b_in"], cosq, sinq)
    ya = _attn_fwd(qr, kr, vv, sp["attn_sink"])
    gather.forward(1, ya)
    y, yc = _conv_fwd(vg, w_dw_full, sp["b_dw"], sp["g_conv_ln"], sp["b_conv_ln"])
    w1 = gather.finish(1, yc)
    wout, wkvT, wq, wo = w1["w_out"], w1["w_mem_kvT"], w1["w_mem_q"], w1["w_mem_o"]
    x1, ymix = _out_proj_fwd(x, yc, ya, wout, sp["b_out"])
    token = gather.forward(2, x1)
    x2, qm, om, km, vm, memn = _mem_attn_fwd(x1, sp["g_mem_q"], wq, mem, sp["g_mem_kv"], wkvT, wo, token)
    w2 = gather.finish(2, om)
    gather.forward(3, om)
    wgT, wuT, wd = w2["w_gateT"], w2["w_upT"], gather.finish(3, om)["w_down"]
    gt, up, hf, dx3, dx3b, loss_acc, dg_final = _ffn_loss_fwd(x2, tgt, sp["g_ffn"], sp["g_final"], wgT, wuT, wd)

    dx2, dx2b, dgt, dup, act, dg_ffn = _ffn_bwd(dx3, x2, gt, up, sp["g_ffn"], wd, wgT, wuT)
    token = reduce.start(0, _wgrad([(dgt, hf), (dup, hf), (act, dx3b)], "wgrad_ffn", dx2))

    dx1, dx1b, hq, dqm, dy, dya, dkm, dvm, dg_mem_q, db_out, dg_cln, db_cln, db_dw = _mem_attn_bwd(
        dx2, x1, qm, km, vm, sp["g_mem_q"], wq, wo, wout, y, sp["g_conv_ln"], sp["b_conv_ln"], token)
    token = reduce.middle(0, dx1)
    g_wout, g_wq, g_wo = _wgrad_acc([(ymix, dx1b), (hq, dqm), (om, dx2b)], "wgrad_mix_mem", token)
    g_wkvT, dg_mem_kv = _mem_kv_bwd(dkm, dvm, mem, sp["g_mem_kv"], memn, wkvT)
    token = reduce.start(1, [g_wout, g_wq, g_wkvT, g_wo])

    dqr, dkr, dvv, dsink = _attn_bwd(qr, kr, vv, sp["attn_sink"], dya, token)
    token = reduce.middle(1, dqr)
    duglu, g_wdw, dbin_glu = _conv_bwd(dy, vg, uglu, w_dw_full, token)
    grad_x, du, dg_mix, dbin_att = _in_proj_bwd(duglu, dqr, dkr, dvv, cosq, sinq, x, dx1, sp["g_mix"], winT)
    token = reduce.start(2, _wgrad_acc([(du, h)], "wgrad_in", grad_x))

    small = {
        "loss": loss_acc[0:1, :],
        "g_mix": dg_mix[0:1], "b_in": jnp.concatenate([dbin_glu[0:1], dbin_att[0:1]], axis=1),
        "b_dw": db_dw[0:1], "g_conv_ln": dg_cln[0:1], "b_conv_ln": db_cln[0:1],
        "attn_sink": dsink[:, 0].reshape(1, NQ), "b_out": db_out[0:1], "g_mem_q": dg_mem_q[0:1],
        "g_mem_kv": dg_mem_kv[0:1], "g_ffn": dg_ffn[0:1], "g_final": dg_final[0:1],
    }
    token = reduce.middle(2, token, reduce.small_zones(small, g_wdw))
    reduce.finish(0, token)
    reduce.finish(1, token)
    return grad_x


def _place():
    return lax.axis_index("x"), lax.axis_index("y"), lax.axis_index("c")


_HBM = pl.BlockSpec(memory_space=pltpu.HBM)
_SEM = pl.BlockSpec(memory_space=pltpu.SEMAPHORE)
_ANY = pl.BlockSpec(memory_space=pl.ANY)
_EFFECT = pltpu.SideEffectType.DATAFLOW_SIDE_EFFECTING
_VMEM = pl.BlockSpec(memory_space=pltpu.VMEM)
_TOKEN = jax.ShapeDtypeStruct((8, LANES), F32)


def _dev_index(p):
    return 4 * p[0] + 2 * p[1] + p[2]


def _landing_zones(shards, me_idx):
    def body(m_ref, *refs):
        n = len(refs) // 2
        for k in range(n):
            refs[n + k][...] = refs[k][...].astype(jnp.bfloat16)

    return pl.pallas_call(
        body, name="landing_zones",
        grid_spec=pltpu.PrefetchScalarGridSpec(
            num_scalar_prefetch=1, grid=(1,),
            in_specs=[pl.BlockSpec(s.shape, lambda i, m: (0, 0)) for s in shards],
            out_specs=[pl.BlockSpec((None,) + s.shape, lambda i, m: (m[0], 0, 0)) for s in shards]),
        out_shape=[pltpu.HBM((NDEV,) + s.shape, jnp.bfloat16) for s in shards],
        compiler_params=_params(40),
    )(me_idx, *shards)


def _gather_start(groups):
    sizes = [len(g) for g in groups]
    flat = [z for g in groups for z in g]
    n = len(flat)

    def body(*refs):
        ins = refs[:n]
        sems = refs[2 * n:]
        x, y, c = _place()
        me = _dev_index((x, y, c))
        peers = [(x, y, 1 - c), (1 - x, y, c), (x, 1 - y, c), (1 - x, 1 - y, c)]
        base = 0
        for gi, size in enumerate(sizes):
            send_sems, recv_sems = sems[2 * gi], sems[2 * gi + 1]
            for k in range(size):
                zone = ins[base + k]
                for j, peer in enumerate(peers):
                    pltpu.make_async_remote_copy(
                        src_ref=zone.at[me], dst_ref=zone.at[me], send_sem=send_sems.at[4 * k + j],
                        recv_sem=recv_sems.at[4 * k + j], device_id=peer, device_id_type=MESH).start()
            base += size

    out_shape = [pltpu.HBM(z.shape, z.dtype) for z in flat]
    out_specs = [_HBM] * n
    for size in sizes:
        out_shape += [pltpu.SemaphoreType.DMA((4 * size,))] * 2
        out_specs += [_SEM, _SEM]
    res = pl.pallas_call(
        body, name="gather_start", in_specs=[_HBM] * n, out_specs=out_specs, out_shape=out_shape,
        input_output_aliases={k: k for k in range(n)},
        compiler_params=pltpu.CompilerParams(has_side_effects=_EFFECT),
    )(*[pltpu.with_memory_space_constraint(z, pltpu.HBM) for z in flat])
    out, base = [], 0
    for gi, size in enumerate(sizes):
        out.append((list(res[base:base + size]), res[n + 2 * gi], res[n + 2 * gi + 1]))
        base += size
    return out


def _gather_forward(zones, send_sems, recv_sems, after, collective_id, name):
    n = len(zones)

    def body(*refs):
        ins = refs[:n]
        send_ref, recv_ref = refs[n], refs[n + 1]
        fsend, frecv = refs[2 * n + 3], refs[2 * n + 4]
        x, y, c = _place()
        sibling = (x, y, 1 - c)
        barrier = pltpu.get_barrier_semaphore()
        pl.semaphore_signal(barrier, inc=1, device_id=sibling, device_id_type=MESH)
        pl.semaphore_wait(barrier, 1)
        chips = [(1 - x, y), (x, 1 - y), (1 - x, 1 - y)]
        for j, chip in enumerate(chips):
            for k in range(n):
                slot = ins[k].at[_dev_index((*chip, c))]
                pltpu.make_async_remote_copy(
                    src_ref=slot, dst_ref=slot, send_sem=send_ref.at[4 * k + 1 + j],
                    recv_sem=recv_ref.at[4 * k + 1 + j], device_id=sibling, device_id_type=MESH).wait_recv()
                pltpu.make_async_remote_copy(
                    src_ref=slot, dst_ref=slot, send_sem=fsend.at[3 * k + j], recv_sem=frecv.at[3 * k + j],
                    device_id=sibling, device_id_type=MESH).start()
        for k in range(n):
            own = ins[k].at[_dev_index((x, y, c))]
            theirs = ins[k].at[_dev_index(sibling)]
            pltpu.make_async_remote_copy(
                src_ref=theirs, dst_ref=theirs, send_sem=send_ref.at[4 * k], recv_sem=recv_ref.at[4 * k],
                device_id=sibling, device_id_type=MESH).wait_recv()
            for j in range(4):
                pltpu.make_async_remote_copy(
                    src_ref=own, dst_ref=own, send_sem=send_ref.at[4 * k + j], recv_sem=recv_ref.at[4 * k + j],
                    device_id=sibling, device_id_type=MESH).wait_send()
        refs[2 * n + 5][...] = jnp.zeros(_TOKEN.shape, _TOKEN.dtype)

    res = pl.pallas_call(
        body, name=name, in_specs=[_HBM] * n + [_SEM, _SEM, _ANY],
        out_specs=[_HBM] * n + [_SEM, _SEM, _VMEM],
        out_shape=[pltpu.HBM(z.shape, z.dtype) for z in zones] + [pltpu.SemaphoreType.DMA((3 * n,))] * 2 + [_TOKEN],
        input_output_aliases={k: k for k in range(n)},
        compiler_params=pltpu.CompilerParams(has_side_effects=_EFFECT, collective_id=collective_id),
    )(*zones, send_sems, recv_sems, after)
    return (list(res[:n]), res[n], res[n + 1]), res[n + 2]


def _gather_finish(zones, fsend, frecv, after, name):
    n = len(zones)

    def body(*refs):
        ins = refs[:n]
        fsend_ref, frecv_ref = refs[n], refs[n + 1]
        x, y, c = _place()
        sibling = (x, y, 1 - c)
        chips = [(1 - x, y), (x, 1 - y), (1 - x, 1 - y)]
        for j, chip in enumerate(chips):
            for k in range(n):
                mine = ins[k].at[_dev_index((*chip, c))]
                theirs = ins[k].at[_dev_index((*chip, 1 - c))]
                pltpu.make_async_remote_copy(
                    src_ref=theirs, dst_ref=theirs, send_sem=fsend_ref.at[3 * k + j],
                    recv_sem=frecv_ref.at[3 * k + j], device_id=sibling, device_id_type=MESH).wait_recv()
                pltpu.make_async_remote_copy(
                    src_ref=mine, dst_ref=mine, send_sem=fsend_ref.at[3 * k + j],
                    recv_sem=frecv_ref.at[3 * k + j], device_id=sibling, device_id_type=MESH).wait_send()

    res = pl.pallas_call(
        body, name=name, in_specs=[_HBM] * n + [_SEM, _SEM, _ANY], out_specs=[_HBM] * n,
        out_shape=[pltpu.HBM(z.shape, z.dtype) for z in zones],
        input_output_aliases={k: k for k in range(n)},
        compiler_params=pltpu.CompilerParams(has_side_effects=_EFFECT),
    )(*zones, fsend, frecv, after)
    return list(res)


class _Gather:
    def __init__(self, started):
        self._phase = list(started)

    def forward(self, gi, after):
        zones, send_sems, recv_sems = self._phase[gi]
        self._phase[gi], token = _gather_forward(zones, send_sems, recv_sems, after, ID_GATHER + gi,
                                                  "gather_forward_%d" % gi)
        return token

    def finish(self, gi, after):
        zones, fsend, frecv = self._phase[gi]
        out = {}
        for name, z in zip(GATHER_GROUPS[gi], _gather_finish(zones, fsend, frecv, after, "gather_finish_%d" % gi)):
            if name == "w_dw":
                out[name] = jnp.transpose(z, (1, 0, 2)).reshape(TAPS + 1, CC)[:TAPS]
            else:
                out[name] = z.reshape(NDEV * z.shape[1], D)
        return out


def _handshake(peers):
    barrier = pltpu.get_barrier_semaphore()
    for peer in peers:
        pl.semaphore_signal(barrier, inc=1, device_id=peer, device_id_type=MESH)
    pl.semaphore_wait(barrier, len(peers))


def _sibling_copies(srcs, lands, send_sems, recv_sems):
    x, y, c = _place()
    return [pltpu.make_async_remote_copy(
        src_ref=srcs[k].at[2 * s + (1 - c)], dst_ref=lands[k].at[s], send_sem=send_sems.at[4 * k + s],
        recv_sem=recv_sems.at[4 * k + s], device_id=(x, y, 1 - c), device_id_type=MESH)
        for k in range(len(srcs)) for s in range(4)]


def _sibling_start(gbs, collective_id, name):
    n = len(gbs)
    lands = [lax.empty((4,) + g.shape[1:], g.dtype) for g in gbs]

    def body(*refs):
        x, y, c = _place()
        _handshake([(x, y, 1 - c)])
        for cp in _sibling_copies(refs[:n], refs[n:2 * n], refs[4 * n], refs[4 * n + 1]):
            cp.start()
        refs[4 * n + 2][...] = jnp.zeros(_TOKEN.shape, _TOKEN.dtype)

    res = pl.pallas_call(
        body, name=name, in_specs=[_HBM] * (2 * n), out_specs=[_HBM] * (2 * n) + [_SEM, _SEM, _VMEM],
        out_shape=[pltpu.HBM(a.shape, a.dtype) for a in gbs + lands] + [pltpu.SemaphoreType.DMA((4 * n,))] * 2
        + [_TOKEN],
        input_output_aliases={k: k for k in range(2 * n)},
        compiler_params=pltpu.CompilerParams(has_side_effects=_EFFECT, collective_id=collective_id),
    )(*[pltpu.with_memory_space_constraint(a, pltpu.HBM) for a in gbs + lands])
    return list(res[:n]), list(res[n:2 * n]), res[2 * n], res[2 * n + 1], res[2 * n + 2]


def _sibling_wait(gbs, lands, send_sems, recv_sems, after, name):
    n = len(gbs)

    def body(*refs):
        for cp in _sibling_copies(refs[:n], refs[n:2 * n], refs[2 * n], refs[2 * n + 1]):
            cp.wait_recv()
            cp.wait_send()

    res = pl.pallas_call(
        body, name=name, in_specs=[_HBM] * (2 * n) + [_SEM, _SEM, _ANY], out_specs=[_HBM] * (2 * n),
        out_shape=[pltpu.HBM(a.shape, a.dtype) for a in gbs + lands],
        input_output_aliases={k: k for k in range(2 * n)},
        compiler_params=pltpu.CompilerParams(has_side_effects=_EFFECT),
    )(*gbs, *lands, send_sems, recv_sems, after)
    return list(res[:n]), list(res[n:])


def _pair_sum(gfs, ts, c_idx, name):
    n = len(gfs)
    halves = 1

    def body(c_ref, *refs):
        for k in range(n):
            refs[2 * n + k][...] = (refs[k][...].astype(F32) + refs[n + k][...].astype(F32)).astype(jnp.bfloat16)

    in_specs, out_specs, out_shape = [], [], []
    for g in gfs:
        rh = g.shape[1] // halves
        in_specs.append(pl.BlockSpec((None, rh, D), lambda s, h, c_ref: (2 * s + c_ref[0], h, 0)))
    for g in gfs:
        rh = g.shape[1] // halves
        in_specs.append(pl.BlockSpec((None, rh, D), lambda s, h, c_ref: (s, h, 0)))
        out_specs.append(pl.BlockSpec((None, rh, D), lambda s, h, c_ref: (s, h, 0)))
        out_shape.append(_sds((4, g.shape[1], D), jnp.bfloat16))
    return pl.pallas_call(
        body, name=name,
        grid_spec=pltpu.PrefetchScalarGridSpec(num_scalar_prefetch=1, grid=(4, halves), in_specs=in_specs,
                                               out_specs=out_specs),
        out_shape=out_shape,
        compiler_params=pltpu.CompilerParams(dimension_semantics=("arbitrary", "arbitrary"),
                                             vmem_limit_bytes=48 * 2 ** 20),
    )(c_idx, *gfs, *ts)


def _chip_copies(pbs, lands, smalls, send_sems, recv_sems):
    x, y, c = _place()
    own = 2 * x + y
    me = _dev_index((x, y, c))
    chips = [(1 - x, y), (x, 1 - y), (1 - x, 1 - y)]
    n = len(pbs)
    copies = [pltpu.make_async_remote_copy(
        src_ref=pbs[k].at[2 * chip[0] + chip[1]], dst_ref=lands[k].at[own], send_sem=send_sems.at[3 * k + j],
        recv_sem=recv_sems.at[3 * k + j], device_id=(*chip, c), device_id_type=MESH)
        for k in range(n) for j, chip in enumerate(chips)]
    peers = [(x, y, 1 - c)] + [(*chip, cc) for chip in chips for cc in (c, 1 - c)]
    copies += [pltpu.make_async_remote_copy(
        src_ref=smalls[t].at[me], dst_ref=smalls[t].at[me], send_sem=send_sems.at[3 * n + 7 * t + j],
        recv_sem=recv_sems.at[3 * n + 7 * t + j], device_id=peer, device_id_type=MESH)
        for t in range(len(smalls)) for j, peer in enumerate(peers)]
    return copies


def _chip_start(pbs, smalls, collective_id, name):
    n, m = len(pbs), len(smalls)
    lands = [lax.empty(p.shape, p.dtype) for p in pbs]
    arrays = list(pbs) + lands + list(smalls)
    na = len(arrays)
    nsem = 3 * n + 7 * m

    def body(*refs):
        x, y, c = _place()
        chips = [(1 - x, y), (x, 1 - y), (1 - x, 1 - y)]
        peers = [(*chip, c) for chip in chips]
        if m:
            peers += [(x, y, 1 - c)] + [(*chip, 1 - c) for chip in chips]
        _handshake(peers)
        for cp in _chip_copies(refs[:n], refs[n:2 * n], refs[2 * n:na], refs[2 * na], refs[2 * na + 1]):
            cp.start()
        refs[2 * na + 2][...] = jnp.zeros(_TOKEN.shape, _TOKEN.dtype)

    res = pl.pallas_call(
        body, name=name, in_specs=[_HBM] * na, out_specs=[_HBM] * na + [_SEM, _SEM, _VMEM],
        out_shape=[pltpu.HBM(a.shape, a.dtype) for a in arrays] + [pltpu.SemaphoreType.DMA((nsem,))] * 2 + [_TOKEN],
        input_output_aliases={k: k for k in range(na)},
        compiler_params=pltpu.CompilerParams(has_side_effects=_EFFECT, collective_id=collective_id),
    )(*[pltpu.with_memory_space_constraint(a, pltpu.HBM) for a in arrays])
    return (list(res[:n]), list(res[n:2 * n]), list(res[2 * n:na]), res[na], res[na + 1]), res[na + 2]


def _chip_wait(pbs, lands, smalls, send_sems, recv_sems, after, name):
    n = len(pbs)
    arrays = list(pbs) + list(lands) + list(smalls)
    na = len(arrays)

    def body(*refs):
        for cp in _chip_copies(refs[:n], refs[n:2 * n], refs[2 * n:na], refs[na], refs[na + 1]):
            cp.wait_recv()
            cp.wait_send()

    res = pl.pallas_call(
        body, name=name, in_specs=[_HBM] * na + [_SEM, _SEM, _ANY], out_specs=[_HBM] * na,
        out_shape=[pltpu.HBM(a.shape, a.dtype) for a in arrays],
        input_output_aliases={k: k for k in range(na)},
        compiler_params=pltpu.CompilerParams(has_side_effects=_EFFECT),
    )(*arrays, send_sems, recv_sems, after)
    return list(res[:n]), list(res[n:2 * n]), list(res[2 * n:])


def _chip_sum(pbs, qs, own_idx, states, name):
    n = len(pbs)
    halves = 2
    n_in = 4 * n + 3 * sum(s is not None for s in states)

    def body(o_ref, *refs):
        nxt_in, nxt_out = 4 * n, n_in
        for k in range(n):
            acc = refs[4 * k][...].astype(F32)
            for j in range(1, 4):
                acc = acc + refs[4 * k + j][...].astype(F32)
            refs[nxt_out][...] = acc
            nxt_out += 1
            if states[k] is not None:
                w_ref, m_ref, v_ref = refs[nxt_in:nxt_in + 3]
                nxt_in += 3
                d, nm, nv = _adamw_math(w_ref[...], acc, m_ref[...], v_ref[...])
                refs[nxt_out][...] = d
                refs[nxt_out + 1][...] = nm
                refs[nxt_out + 2][...] = nv
                nxt_out += 3

    def other(j):
        return lambda h, o: ((o[0] + j) % 4, h, 0)

    def rows(r):
        return pl.BlockSpec((r // halves, D), lambda h, o: (h, 0))

    in_specs, ins = [], []
    for p, q in zip(pbs, qs):
        rh = p.shape[1] // halves
        in_specs.append(pl.BlockSpec((None, rh, D), lambda h, o: (o[0], h, 0)))
        ins.append(p)
        for j in range(1, 4):
            in_specs.append(pl.BlockSpec((None, rh, D), other(j)))
            ins.append(q)
    out_specs, out_shape, counts = [], [], []
    for p, s in zip(pbs, states):
        r = p.shape[1]
        if s is not None:
            in_specs += [rows(r)] * 3
            ins += list(s)
        counts.append(1 if s is None else 4)
        out_specs += [rows(r)] * counts[-1]
        out_shape += [_sds((r, D), F32)] * counts[-1]
    res = pl.pallas_call(
        body, name=name,
        grid_spec=pltpu.PrefetchScalarGridSpec(num_scalar_prefetch=1, grid=(halves,), in_specs=in_specs,
                                               out_specs=out_specs),
        out_shape=out_shape,
        compiler_params=pltpu.CompilerParams(dimension_semantics=("arbitrary",), vmem_limit_bytes=56 * 2 ** 20),
    )(own_idx, *ins)
    out, base = [], 0
    for c in counts:
        out.append(tuple(res[base:base + c]))
        base += c
    return out


REDUCE_GROUPS = (("w_gateT", "w_upT", "w_down"), ("w_out", "w_mem_q", "w_mem_kvT", "w_mem_o"), ("w_inT",))


class _Reduce:
    def __init__(self, states):
        x, y, c = _place()
        self._c = c.astype(jnp.int32).reshape(1)
        self._own = (2 * x + y).astype(jnp.int32).reshape(1)
        self._me = _dev_index((x, y, c))
        self._state = {}
        self._adam = states
        self.results = {}
        self.smalls = None

    def start(self, gi, wgrads):
        gbs = [g.reshape(NDEV, g.shape[0] // NDEV, D) for g in wgrads]
        gbs, lands, send_sems, recv_sems, token = _sibling_start(gbs, ID_SIBLING + gi, "reduce_sibling_start_%d" % gi)
        self._state[gi] = (gbs, lands, send_sems, recv_sems)
        return token

    def middle(self, gi, after, smalls=()):
        gbs, lands, send_sems, recv_sems = self._state[gi]
        gbs, ts = _sibling_wait(gbs, lands, send_sems, recv_sems, after, "reduce_sibling_wait_%d" % gi)
        pbs = _pair_sum(gbs, ts, self._c, "reduce_pair_sum_%d" % gi)
        self._state[gi], token = _chip_start(pbs, list(smalls), ID_CHIP + gi, "reduce_chip_start_%d" % gi)
        return token

    def finish(self, gi, after):
        pbs, lands, smalls, send_sems, recv_sems = self._state[gi]
        pbs, qs, smalls = _chip_wait(pbs, lands, smalls, send_sems, recv_sems, after, "reduce_chip_wait_%d" % gi)
        names = REDUCE_GROUPS[gi]
        states = [self._adam.get(n) for n in names]
        self.results.update(zip(names, _chip_sum(pbs, qs, self._own, states, "reduce_update_%d" % gi)))
        if smalls:
            self.smalls = smalls

    def small_zones(self, small, g_wdw):
        pack = jnp.concatenate(
            [jnp.pad(small[name], ((0, 0), (0, lanes - small[name].shape[1]))) for name, lanes, _ in PACK], axis=1)
        zones = []
        for a in (pack, g_wdw):
            zone = lax.empty((NDEV,) + a.shape, a.dtype)
            zones.append(lax.dynamic_update_slice(zone, a[None], (self._me,) + (0,) * a.ndim))
        return zones


def _adamw_math(w, g, m, v):
    m = ADAM_B1 * m + (1.0 - ADAM_B1) * g
    v = ADAM_B2 * v + (1.0 - ADAM_B2) * (g * g)
    m_hat = m / (1.0 - ADAM_B1 ** ADAM_STEP)
    v_hat = v / (1.0 - ADAM_B2 ** ADAM_STEP)
    delta = -ADAM_LR * (m_hat / (jnp.sqrt(v_hat) + ADAM_EPS) + ADAM_WD * w)
    return delta, m, v


def _adamw(w, g, m, v, name):
    R, C = w.shape
    rc = 256 if R % 256 == 0 and R > 256 else R

    def body(w_ref, g_ref, m_ref, v_ref, d_ref, nm_ref, nv_ref):
        d, nm, nv = _adamw_math(w_ref[...], g_ref[...], m_ref[...], v_ref[...])
        d_ref[...] = d
        nm_ref[...] = nm
        nv_ref[...] = nv

    return _hbm_call(
        body, name=name, grid=(R // rc,),
        in_specs=[_rows(rc, C)] * 4, out_specs=[_rows(rc, C)] * 3,
        out_shape=[_sds((R, C), F32)] * 3,
        compiler_params=_params(40),
    )(w, g, m, v)


PACK = (("loss", LANES, 1), ("g_mix", D, D), ("b_in", INC, INC), ("b_dw", CC, CC), ("g_conv_ln", CC, CC),
        ("b_conv_ln", CC, CC), ("attn_sink", LANES, NQ), ("b_out", D, D), ("g_mem_q", D, D), ("g_mem_kv", D, D),
        ("g_ffn", D, D), ("g_final", D, D))
PACK_LANES = sum(p[1] for p in PACK)


def _small_update(packs, gdws, ws, ms, vs):
    names = [p[0] for p in PACK[1:]]
    n = len(names)

    def body(*refs):
        packs_ref, gdws_ref = refs[0], refs[1]
        w_refs, m_refs, v_refs = refs[2:2 + n], refs[2 + n:2 + 2 * n], refs[2 + 2 * n:2 + 3 * n]
        outs = refs[2 + 3 * n:]
        tot = packs_ref[0]
        gdw = gdws_ref[0]
        for s in range(1, NDEV):
            tot = tot + packs_ref[s]
            gdw = gdw + gdws_ref[s]
        outs[0][...] = tot[:, 0:LANES]
        outs[1][...] = gdw
        off = LANES
        for k, (_, lanes, used) in enumerate(PACK[1:]):
            g = tot[:, off:off + used]
            off += lanes
            d, nm, nv = _adamw_math(w_refs[k][...], g, m_refs[k][...], v_refs[k][...])
            o = outs[2 + 4 * k:6 + 4 * k]
            o[0][...] = g
            o[1][...] = d
            o[2][...] = nm
            o[3][...] = nv

    def full(a):
        nd = len(a.shape)
        return pl.BlockSpec(a.shape, lambda i: (0,) * nd)

    ins = [packs, gdws, *ws, *ms, *vs]
    out_shape = [_sds((1, LANES), F32), _sds(gdws.shape[1:], F32)]
    for w in ws:
        out_shape += [_sds(w.shape, F32)] * 4
    return pl.pallas_call(
        body, name="small_update", grid=(1,),
        in_specs=[full(a) for a in ins], out_specs=[full(o) for o in out_shape], out_shape=out_shape,
        compiler_params=_params(40),
    )(*ins)


def kernel(x, mem, g_mix, w_in, b_in, w_dw, b_dw, g_conv_ln, b_conv_ln, attn_sink, w_out, b_out, g_mem_q, g_mem_kv, w_mem_q, w_mem_kv, w_mem_o, g_ffn, w_gate, w_up, w_down, g_final, loss_target, m_g_mix, m_w_in, m_b_in, m_w_dw, m_b_dw, m_g_conv_ln, m_b_conv_ln, m_attn_sink, m_w_out, m_b_out, m_g_mem_q, m_g_mem_kv, m_w_mem_q, m_w_mem_kv, m_w_mem_o, m_g_ffn, m_w_gate, m_w_up, m_w_down, m_g_final, v_g_mix, v_w_in, v_b_in, v_w_dw, v_b_dw, v_g_conv_ln, v_b_conv_ln, v_attn_sink, v_w_out, v_b_out, v_g_mem_q, v_g_mem_kv, v_w_mem_q, v_w_mem_kv, v_w_mem_o, v_g_ffn, v_w_gate, v_w_up, v_w_down, v_g_final):
    given = dict(locals())
    bf = jnp.bfloat16
    xi, yi, ci = _place()
    me = 4 * xi + 2 * yi + ci

    transposed = {"w_inT": w_in, "w_mem_kvT": w_mem_kv, "w_gateT": w_gate, "w_upT": w_up}
    plain = {"w_out": w_out, "w_mem_q": w_mem_q, "w_mem_o": w_mem_o, "w_down": w_down}
    shards = [transposed[n][0].T if n in transposed else plain[n][0] for n in W_NAMES]
    zones = dict(zip(W_NAMES, _landing_zones(shards, me.astype(jnp.int32).reshape(1))))
    dw = jnp.pad(w_dw[0], ((0, 1), (0, 0)))
    zones["w_dw"] = lax.dynamic_update_slice(lax.empty((NDEV,) + dw.shape, dw.dtype), dw[None], (me, 0, 0))
    gather = _Gather(_gather_start([[zones[n] for n in names] for names in GATHER_GROUPS]))

    sp = {n: given[n] for n in ("g_mix", "b_in", "b_dw", "g_conv_ln", "b_conv_ln", "attn_sink", "b_out", "g_mem_q",
                                "g_mem_kv", "g_ffn")}
    sp["g_final"] = g_final.reshape(1, D)
    frames = {"w_inT": ("w_in", True), "w_gateT": ("w_gate", True), "w_upT": ("w_up", True),
              "w_out": ("w_out", False), "w_mem_q": ("w_mem_q", False), "w_mem_o": ("w_mem_o", False),
              "w_down": ("w_down", False)}
    states = {}
    for frame, (n, transposed) in frames.items():
        wmv = (given[n][0], given["m_" + n][0], given["v_" + n][0])
        states[frame] = tuple(a.T for a in wmv) if transposed else wmv
    reduce = _Reduce(states)
    grad_x = _local_step(x[0], mem[0], loss_target[0], sp, gather, reduce)

    grads, deltas, new_m, new_v = {}, {}, {}, {}

    def update(n, g):
        d, nm, nv = _adamw(given[n][0], g, given["m_" + n][0], given["v_" + n][0], "adamw_" + n)
        grads[n], deltas[n], new_m[n], new_v[n] = g[None], d[None], nm[None], nv[None]
        return d

    reduce.finish(2, update("w_mem_kv", reduce.results["w_mem_kvT"][0].T))
    for frame, (n, transposed) in frames.items():
        outs = [a.T if transposed else a for a in reduce.results[frame]]
        grads[n], deltas[n], new_m[n], new_v[n] = [a[None] for a in outs]

    packs, gdws = reduce.smalls
    names_small = [p[0] for p in PACK[1:]]

    def as_row(a):
        return a.reshape(1, -1)

    res = _small_update(packs, gdws, [as_row(given[n]) for n in names_small],
                        [as_row(given["m_" + n]) for n in names_small],
                        [as_row(given["v_" + n]) for n in names_small])
    loss = res[0][0, 0]
    for k, n in enumerate(names_small):
        shape = given[n].shape
        g, d, nm, nv = res[2 + 4 * k:6 + 4 * k]
        grads[n], deltas[n], new_m[n], new_v[n] = (g.reshape(shape), d.reshape(shape), nm.reshape(shape),
                                                   nv.reshape(shape))
    update("w_dw", lax.dynamic_slice(res[1], (0, me * (CC // NDEV)), (TAPS, CC // NDEV)))

    order = ("g_mix", "w_in", "b_in", "w_dw", "b_dw", "g_conv_ln", "b_conv_ln", "attn_sink", "w_out", "b_out",
             "g_mem_q", "g_mem_kv", "w_mem_q", "w_mem_kv", "w_mem_o", "g_ffn", "w_gate", "w_up", "w_down", "g_final")
    return (loss, grad_x[None], *[grads[n] for n in order], *[deltas[n] for n in order],
            *[new_m[n] for n in order], *[new_v[n] for n in order])
```
